```python
import math
import numpy as np
import jax
import jax.numpy as jnp
from jax import lax

D_MODEL = 1024
BATCH = 16
SEQ = 256
DEPTH = 4
DEC_BATCH = 2
DEC_SEQ = 1024
PAST_LEN = 512

GRID_W = 64
HEAD_DIM = 64
NA_HEADS = 4
NA_WIN_ROWS = 8
NA_WIN_COLS = 16
DIFF_HEADS = 4
DIFF_QK_DIM = 32
DIFF_V_DIM = 64
MLA_HEADS = 4
MLA_NOPE = 64
MLA_ROPE = 32
MLA_V = 64
MLA_KV_RANK = 128
SGU_GROUPS = 4
SGU_GROUP_DIM = 64
SGU_CHUNK = 128
D_FF = 4 * D_MODEL
ROPE_BASE = 10000.0
QBLOCK = 128
EPS = 1e-6
NEG_INF = -1e30

WIDTH_A = NA_HEADS * HEAD_DIM
WIDTH_B = DIFF_HEADS * DIFF_V_DIM
WIDTH_C = MLA_HEADS * MLA_V
WIDTH_D = SGU_GROUPS * SGU_GROUP_DIM
MIX_WIDTH = WIDTH_A + WIDTH_B + WIDTH_C + WIDTH_D
DIFF_QK_W = DIFF_HEADS * 2 * DIFF_QK_DIM
MLA_Q_W = MLA_HEADS * (MLA_NOPE + MLA_ROPE)
IN_SPLITS = (WIDTH_A, WIDTH_A, WIDTH_A, DIFF_QK_W, DIFF_QK_W, WIDTH_B,
             MLA_Q_W, MLA_KV_RANK, MLA_ROPE, WIDTH_D, WIDTH_D)
IN_COLS = 3 * WIDTH_A + 2 * DIFF_QK_W + WIDTH_B + MLA_Q_W + MLA_KV_RANK + MLA_ROPE + 2 * WIDTH_D

kernel_name = 'hybrid_na_diff_mla_sgu_dit_step'


def rmsnorm(x, g):
    xf = x.astype(jnp.float32)
    y = xf * lax.rsqrt(jnp.mean(xf * xf, axis=-1, keepdims=True) + EPS)
    return (y * g.astype(jnp.float32)).astype(x.dtype)


def _heads(x, n):
    b, s, _ = x.shape
    return x.reshape(b, s, n, -1).transpose(0, 2, 1, 3)


def _merge(x):
    b, h, s, d = x.shape
    return x.transpose(0, 2, 1, 3).reshape(b, s, h * d)


def _rope1d(x, pos):
    half = x.shape[-1] // 2
    freqs = ROPE_BASE ** (-jnp.arange(half, dtype=jnp.float32) / half)
    ang = pos[:, None] * freqs[None, :]
    cos, sin = jnp.cos(ang).astype(x.dtype), jnp.sin(ang).astype(x.dtype)
    x1, x2 = x[..., :half], x[..., half:]
    return jnp.concatenate([x1 * cos - x2 * sin, x2 * cos + x1 * sin], axis=-1)


def rope2d(x, rows, cols):
    r = x.shape[-1] // 2
    return jnp.concatenate([_rope1d(x[..., :r], rows), _rope1d(x[..., r:], cols)], axis=-1)


def _grid_positions(s):
    t = jnp.arange(s)
    return (t // GRID_W).astype(jnp.float32), (t % GRID_W).astype(jnp.float32)


def _map_query_blocks(fn, *qs):
    b, h, s = qs[0].shape[:3]
    nb = s // QBLOCK
    blocks = tuple(q.reshape(b, h, nb, QBLOCK, q.shape[-1]).transpose(2, 0, 1, 3, 4) for q in qs)
    out = lax.map(lambda a: fn(*a), blocks)
    return out.transpose(1, 2, 0, 3, 4).reshape(b, h, s, out.shape[-1])


def dense_attn(q, k, v, scale):
    def blk(qb):
        s = jnp.einsum('bhqd,bhkd->bhqk', qb, k).astype(jnp.float32) * scale
        p = jax.nn.softmax(s, axis=-1).astype(v.dtype)
        return jnp.einsum('bhqk,bhkd->bhqd', p, v)
    return _map_query_blocks(blk, q)


def diff_attn(q1, q2, k1, k2, v, lam, scale):
    def blk(q1b, q2b):
        s1 = jnp.einsum('bhqd,bhkd->bhqk', q1b, k1).astype(jnp.float32) * scale
        s2 = jnp.einsum('bhqd,bhkd->bhqk', q2b, k2).astype(jnp.float32) * scale
        p = jax.nn.softmax(s1, axis=-1) - lam * jax.nn.softmax(s2, axis=-1)
        return jnp.einsum('bhqk,bhkd->bhqd', p.astype(v.dtype), v)
    return _map_query_blocks(blk, q1, q2)


def mla_attn(q_nope, q_pe, k_nope, k_pe, v, scale):
    def blk(qn, qp):
        s = (jnp.einsum('bhqd,bhkd->bhqk', qn, k_nope)
             + jnp.einsum('bhqr,bkr->bhqk', qp, k_pe)).astype(jnp.float32) * scale
        p = jax.nn.softmax(s, axis=-1).astype(v.dtype)
        return jnp.einsum('bhqk,bhkd->bhqd', p, v)
    return _map_query_blocks(blk, q_nope, q_pe)


def neighbourhood_attn(q, k, v, k_ctx, v_ctx, rpb):
    b, h, s, d = q.shape
    rows = s // GRID_W
    kh = min(NA_WIN_ROWS, rows)
    kw = NA_WIN_COLS
    scale = d ** -0.5
    lc = k_ctx.shape[2]
    cols = jnp.arange(GRID_W)
    c0 = jnp.clip(cols - kw // 2, 0, GRID_W - kw)
    in_win = (cols[None, :] >= c0[:, None]) & (cols[None, :] < c0[:, None] + kw)
    dcol = jnp.clip(cols[None, :] - cols[:, None], -(kw - 1), kw - 1) + (kw - 1)
    qg = q.reshape(b, h, rows, GRID_W, d).transpose(2, 0, 1, 3, 4)
    kg = k.reshape(b, h, rows, GRID_W, d)
    vg = v.reshape(b, h, rows, GRID_W, d)

    def row_block(args):
        r, qr = args
        r0 = jnp.clip(r - kh // 2, 0, rows - kh)
        kr = lax.dynamic_slice_in_dim(kg, r0, kh, axis=2)
        vr = lax.dynamic_slice_in_dim(vg, r0, kh, axis=2)
        drow = r0 + jnp.arange(kh) - r + (NA_WIN_ROWS - 1)
        bias = rpb[:, drow[None, :, None], dcol[:, None, :]]
        s_loc = (jnp.einsum('bhqd,bhikd->bhqik', qr, kr).astype(jnp.float32) * scale
                 + bias.astype(jnp.float32))
        s_loc = jnp.where(in_win[:, None, :], s_loc, NEG_INF)
        s_ctx = jnp.einsum('bhqd,bhkd->bhqk', qr, k_ctx).astype(jnp.float32) * scale
        s_all = jnp.concatenate([s_ctx, s_loc.reshape(b, h, GRID_W, kh * GRID_W)], axis=-1)
        p = jax.nn.softmax(s_all, axis=-1).astype(v.dtype)
        p_loc = p[..., lc:].reshape(b, h, GRID_W, kh, GRID_W)
        return (jnp.einsum('bhqk,bhkd->bhqd', p[..., :lc], v_ctx)
                + jnp.einsum('bhqik,bhikd->bhqd', p_loc, vr))

    out = lax.map(row_block, (jnp.arange(rows), qg))
    return out.transpose(1, 2, 0, 3, 4).reshape(b, h, s, d)


def spatial_gating(u, v, lp):
    u = jax.nn.gelu(u)
    v = jax.nn.gelu(v)
    bsz, length, _ = u.shape
    n = length // SGU_CHUNK
    vg = rmsnorm(v.reshape(bsz, n, SGU_CHUNK, SGU_GROUPS, SGU_GROUP_DIM), lp['sgu_g'])
    mixed = jnp.einsum('gpq,bnqgc->bnpgc', lp['sgu_w'], vg) + lp['sgu_b'].T[:, :, None]
    return u * mixed.reshape(bsz, length, WIDTH_D)


def _in_proj(h, w):
    points = np.cumsum(IN_SPLITS)[:-1].tolist()
    return jnp.split(h @ w, points, axis=-1)


def _diff_lambda(lp, lam_init):
    f = jnp.float32
    return (jnp.exp(jnp.sum((lp['diff_lq1'] * lp['diff_lk1']).astype(f)))
            - jnp.exp(jnp.sum((lp['diff_lq2'] * lp['diff_lk2']).astype(f))) + lam_init)


def _diff_mixer(q, k, v, lp, lam_init):
    lam = _diff_lambda(lp, lam_init)
    o = diff_attn(q[..., :DIFF_QK_DIM], q[..., DIFF_QK_DIM:], k[..., :DIFF_QK_DIM], k[..., DIFF_QK_DIM:],
                  v, lam, DIFF_QK_DIM ** -0.5)
    return rmsnorm(o, lp['diff_g_subln']) * (1.0 - lam_init)


def _rope_pair(x, rows, cols):
    return jnp.concatenate([rope2d(x[..., :DIFF_QK_DIM], rows, cols),
                            rope2d(x[..., DIFF_QK_DIM:], rows, cols)], axis=-1)


def _mla_mixer(q_nope, q_pe, ckv_all, kpe_all, lp):
    k_nope = _heads(ckv_all @ lp['mla_w_uk'], MLA_HEADS)
    v = _heads(ckv_all @ lp['mla_w_uv'], MLA_HEADS)
    return mla_attn(q_nope, q_pe, k_nope, kpe_all, v, (MLA_NOPE + MLA_ROPE) ** -0.5)


def _mix_context(h, lp, lam_init):
    qa, ka, va, qb, kb, vb, qc, ckv, kpe, u, vs = _in_proj(h, lp['w_in'])
    qa, ka, va = _heads(qa, NA_HEADS), _heads(ka, NA_HEADS), _heads(va, NA_HEADS)
    o_a = dense_attn(qa, ka, va, HEAD_DIM ** -0.5)
    qb, kb, vb = _heads(qb, DIFF_HEADS), _heads(kb, DIFF_HEADS), _heads(vb, DIFF_HEADS)
    o_b = _diff_mixer(qb, kb, vb, lp, lam_init)
    ckv = rmsnorm(ckv, lp['mla_g_ckv'])
    qc = _heads(qc, MLA_HEADS)
    o_c = _mla_mixer(qc[..., :MLA_NOPE], qc[..., MLA_NOPE:], ckv, kpe, lp)
    o_d = spatial_gating(u, vs, lp)
    out = jnp.concatenate([_merge(o_a), _merge(o_b), _merge(o_c), o_d], axis=-1)
    return out, (ka, va, kb, vb, ckv, kpe)


def _mix_latent(h, lp, lam_init, na_k, na_v, diff_k, diff_v, mla_ckv, mla_kpe):
    qa, ka, va, qb, kb, vb, qc, ckv, kpe, u, vs = _in_proj(h, lp['w_in'])
    rows, cols = _grid_positions(h.shape[1])
    qa, ka, va = _heads(qa, NA_HEADS), _heads(ka, NA_HEADS), _heads(va, NA_HEADS)
    o_a = neighbourhood_attn(qa, ka, va, na_k, na_v, lp['na_rpb'])
    qb = _rope_pair(_heads(qb, DIFF_HEADS), rows, cols)
    kb = _rope_pair(_heads(kb, DIFF_HEADS), rows, cols)
    vb = _heads(vb, DIFF_HEADS)
    o_b = _diff_mixer(qb, jnp.concatenate([diff_k, kb], axis=2), jnp.concatenate([diff_v, vb], axis=2),
                      lp, lam_init)
    ckv = rmsnorm(ckv, lp['mla_g_ckv'])
    kpe = rope2d(kpe, rows, cols)
    qc = _heads(qc, MLA_HEADS)
    o_c = _mla_mixer(qc[..., :MLA_NOPE], rope2d(qc[..., MLA_NOPE:], rows, cols),
                     jnp.concatenate([mla_ckv, ckv], axis=1), jnp.concatenate([mla_kpe, kpe], axis=1), lp)
    o_d = spatial_gating(u, vs, lp)
    return jnp.concatenate([_merge(o_a), _merge(o_b), _merge(o_c), o_d], axis=-1)


def _block(x, m, lp, mix):
    e = jax.nn.silu(m) @ lp['w_ada'] + lp['b_ada']
    sh1, sc1, g1, sh2, sc2, g2 = jnp.split(e, 6, axis=-1)
    h = rmsnorm(x, lp['g_mix']) * (1 + sc1[:, None]) + sh1[:, None]
    o, extra = mix(h)
    x = x + g1[:, None] * (o @ lp['w_out'])
    hf = rmsnorm(x, lp['g_ffn']) * (1 + sc2[:, None]) + sh2[:, None]
    x = x + g2[:, None] * (jnp.square(jax.nn.relu(hf @ lp['w_ff1'])) @ lp['w_ff2'])
    return x, extra


def setup_inputs(seed: int = 0) -> dict:
    key = jax.random.key(seed)
    ks = jax.random.split(key, 32)
    f32 = jnp.float32

    def nrm(k, shape, s):
        return jax.random.normal(k, shape, f32) * s

    def gain(k, shape):
        return 1.0 + 0.01 * jax.random.normal(k, shape, f32)

    L, D = DEPTH, D_MODEL
    return {
        'x_prompt': nrm(ks[0], (BATCH, SEQ, D), 1.0),
        'x_sample': nrm(ks[1], (DEC_BATCH, DEC_SEQ, D), 1.0),
        'cache_na_k': nrm(ks[2], (DEC_BATCH, L, NA_HEADS, PAST_LEN, HEAD_DIM), 1.0),
        'cache_na_v': nrm(ks[3], (DEC_BATCH, L, NA_HEADS, PAST_LEN, HEAD_DIM), 1.0),
        'cache_diff_k': nrm(ks[4], (DEC_BATCH, L, DIFF_HEADS, PAST_LEN, 2 * DIFF_QK_DIM), 1.0),
        'cache_diff_v': nrm(ks[5], (DEC_BATCH, L, DIFF_HEADS, PAST_LEN, DIFF_V_DIM), 1.0),
        'cache_mla_ckv': nrm(ks[6], (DEC_BATCH, L, PAST_LEN, MLA_KV_RANK), 1.0),
        'cache_mla_kpe': nrm(ks[7], (DEC_BATCH, L, PAST_LEN, MLA_ROPE), 1.0),
        'c': nrm(ks[8], (DEC_BATCH, D), 1.0),
        'c_ctx': nrm(ks[9], (D,), 1.0),
        'w_ada': nrm(ks[10], (L, D, 6 * D), D ** -0.5),
        'b_ada': nrm(ks[11], (L, 6 * D), 0.01),
        'g_mix': gain(ks[12], (L, D)),
        'g_ffn': gain(ks[13], (L, D)),
        'w_in': nrm(ks[14], (L, D, IN_COLS), D ** -0.5),
        'w_out': nrm(ks[15], (L, MIX_WIDTH, D), MIX_WIDTH ** -0.5),
        'na_rpb': nrm(ks[16], (L, NA_HEADS, 2 * NA_WIN_ROWS - 1, 2 * NA_WIN_COLS - 1), 0.1),
        'diff_lq1': nrm(ks[17], (L, DIFF_QK_DIM), 0.1),
        'diff_lk1': nrm(ks[18], (L, DIFF_QK_DIM), 0.1),
        'diff_lq2': nrm(ks[19], (L, DIFF_QK_DIM), 0.1),
        'diff_lk2': nrm(ks[20], (L, DIFF_QK_DIM), 0.1),
        'diff_g_subln': gain(ks[21], (L, DIFF_V_DIM)),
        'mla_g_ckv': gain(ks[22], (L, MLA_KV_RANK)),
        'mla_w_uk': nrm(ks[23], (L, MLA_KV_RANK, MLA_HEADS * MLA_NOPE), MLA_KV_RANK ** -0.5),
        'mla_w_uv': nrm(ks[24], (L, MLA_KV_RANK, MLA_HEADS * MLA_V), MLA_KV_RANK ** -0.5),
        'sgu_g': gain(ks[25], (L, SGU_GROUPS, SGU_GROUP_DIM)),
        'sgu_w': nrm(ks[26], (L, SGU_GROUPS, SGU_CHUNK, SGU_CHUNK), SGU_CHUNK ** -0.5),
        'sgu_b': gain(ks[27], (L, SGU_GROUPS, SGU_CHUNK)),
        'w_ff1': nrm(ks[28], (L, D, D_FF), D ** -0.5),
        'w_ff2': nrm(ks[29], (L, D_FF, D), D_FF ** -0.5),
        'g_final': gain(ks[30], (D,)),
    }


def reference(x_prompt, x_sample, cache_na_k, cache_na_v, cache_diff_k, cache_diff_v,
              cache_mla_ckv, cache_mla_kpe, c, c_ctx,
              w_ada, b_ada, g_mix, g_ffn, w_in, w_out, na_rpb,
              diff_lq1, diff_lk1, diff_lq2, diff_lk2, diff_g_subln,
              mla_g_ckv, mla_w_uk, mla_w_uv, sgu_g, sgu_w, sgu_b, w_ff1, w_ff2, g_final):
    stacked = {
        'w_ada': w_ada, 'b_ada': b_ada, 'g_mix': g_mix, 'g_ffn': g_ffn, 'w_in': w_in, 'w_out': w_out,
        'na_rpb': na_rpb, 'diff_lq1': diff_lq1, 'diff_lk1': diff_lk1, 'diff_lq2': diff_lq2,
        'diff_lk2': diff_lk2, 'diff_g_subln': diff_g_subln, 'mla_g_ckv': mla_g_ckv,
        'mla_w_uk': mla_w_uk, 'mla_w_uv': mla_w_uv, 'sgu_g': sgu_g, 'sgu_w': sgu_w, 'sgu_b': sgu_b,
        'w_ff1': w_ff1, 'w_ff2': w_ff2,
    }
    m_ctx = c_ctx[None, :]
    y_p = x_prompt
    y_s = x_sample
    st = ([], [], [], [], [], [])
    for l in range(DEPTH):
        lp = {name: arr[l] for name, arr in stacked.items()}
        lam_init = 0.8 - 0.6 * math.exp(-0.3 * l)
        y_p, ctx = _block(y_p, m_ctx, lp, lambda h: _mix_context(h, lp, lam_init))
        for lst, t in zip(st, ctx):
            lst.append(t)
        y_s, _ = _block(y_s, c, lp, lambda h: (_mix_latent(
            h, lp, lam_init, cache_na_k[:, l], cache_na_v[:, l], cache_diff_k[:, l], cache_diff_v[:, l],
            cache_mla_ckv[:, l], cache_mla_kpe[:, l]), None))
    y_prompt = rmsnorm(y_p, g_final)
    y_sample = rmsnorm(y_s, g_final)
    new_na_k = jnp.stack(st[0], axis=1)
    new_na_v = jnp.stack(st[1], axis=1)
    new_diff_k = jnp.stack(st[2], axis=1)
    new_diff_v = jnp.stack(st[3], axis=1)
    new_mla_ckv = jnp.stack(st[4], axis=1)
    new_mla_kpe = jnp.stack(st[5], axis=1)
    return (y_prompt, y_sample, new_na_k, new_na_v, new_diff_k, new_diff_v, new_mla_ckv, new_mla_kpe)
```

```python
import functools
import math

import numpy as np
import jax
import jax.numpy as jnp
from jax import lax
from jax.experimental import pallas as pl
from jax.experimental.pallas import tpu as pltpu

D_MODEL = 1024
BATCH = 16
SEQ = 256
DEPTH = 4
DEC_BATCH = 2
DEC_SEQ = 1024
PAST_LEN = 512
GRID_W = 64
GRID_ROWS = DEC_SEQ // GRID_W
HEAD_DIM = 64
NA_HEADS = 4
NA_WIN_ROWS = 8
NA_WIN_COLS = 16
DIFF_HEADS = 4
DIFF_QK_DIM = 32
DIFF_V_DIM = 64
MLA_HEADS = 4
MLA_NOPE = 64
MLA_ROPE = 32
MLA_V = 64
MLA_KV_RANK = 128
SGU_GROUPS = 4
SGU_GROUP_DIM = 64
SGU_CHUNK = 128
D_FF = 4 * D_MODEL
ROPE_BASE = 10000.0
EPS = 1e-6
NEG_INF = -1e30

WIDTH = 256
N_PROMPT = BATCH * SEQ
N_SAMPLE = DEC_BATCH * DEC_SEQ
N_TOK = N_PROMPT + N_SAMPLE
N_MOD_ROWS = 8

SEG_A = 3 * WIDTH
SEG_B = 3 * WIDTH
SEG_C = 640
SEG_D = 2 * WIDTH
SEG_C_PAD = 96
IN_COLS_P = SEG_A + SEG_B + SEG_C + SEG_D

TM = 512
VMEM_LIMIT = 56 * 1024 * 1024


def _bf(x):
    return x.astype(jnp.bfloat16)


def _dot(a, b):
    return jnp.dot(a, b, preferred_element_type=jnp.float32)


def _dot_nt(a, b):
    return lax.dot_general(a, b, (((1,), (1,)), ((), ())), preferred_element_type=jnp.float32)


def _rms(x, g):
    ms = jnp.mean(x * x, axis=-1, keepdims=True)
    return x * lax.rsqrt(ms + EPS) * g


def _softmax_parts(parts):
    m = functools.reduce(jnp.maximum, [jnp.max(s, axis=-1, keepdims=True) for s in parts])
    es = [jnp.exp(s - m) for s in parts]
    tot = functools.reduce(lambda a, b: a + b, [jnp.sum(e, axis=-1, keepdims=True) for e in es])
    inv = 1.0 / tot
    return [e * inv for e in es]


def _swap8(x):
    lane = lax.broadcasted_iota(jnp.int32, x.shape, 1)
    return jnp.where((lane & 15) < 8, pltpu.roll(x, 120, 1), pltpu.roll(x, 8, 1))


def _rope(x, cos, sin):
    outs = []
    for c in range(x.shape[1] // 128):
        sl = slice(128 * c, 128 * (c + 1))
        xc = x[:, sl]
        outs.append(xc * cos[:, sl] + _swap8(xc) * sin[:, sl])
    return outs[0] if len(outs) == 1 else jnp.concatenate(outs, axis=1)


def _params(*sem):
    return pltpu.CompilerParams(dimension_semantics=sem, vmem_limit_bytes=VMEM_LIMIT)


ADA_TN = 1536


def _ada_kernel(m_ref, w_ref, b_ref, o_ref):
    m = m_ref[...]
    s = m * jax.nn.sigmoid(m)
    o_ref[0] = _dot(_bf(s), _bf(w_ref[0])) + b_ref[0]


def _ada(m, w_ada, b_ada):
    n = 6 * D_MODEL
    return pl.pallas_call(
        _ada_kernel,
        grid=(DEPTH, n // ADA_TN),
        in_specs=[
            pl.BlockSpec((N_MOD_ROWS, D_MODEL), lambda l, j: (0, 0)),
            pl.BlockSpec((1, D_MODEL, ADA_TN), lambda l, j: (l, 0, j)),
            pl.BlockSpec((1, 1, ADA_TN), lambda l, j: (l, 0, j)),
        ],
        out_specs=pl.BlockSpec((1, N_MOD_ROWS, ADA_TN), lambda l, j: (l, 0, j)),
        out_shape=jax.ShapeDtypeStruct((DEPTH, N_MOD_ROWS, n), jnp.float32),
        compiler_params=_params("parallel", "parallel"),
        name="ada",
    )(m, w_ada, b_ada.reshape(DEPTH, 1, n))


def _lam_kernel(lq1_ref, lk1_ref, lq2_ref, lk2_ref, init_ref, o_ref):
    init = init_ref[...]
    a = jnp.exp(jnp.sum(lq1_ref[...] * lk1_ref[...], axis=-1, keepdims=True))
    b = jnp.exp(jnp.sum(lq2_ref[...] * lk2_ref[...], axis=-1, keepdims=True))
    lam = a - b + init
    post = 1.0 - init
    for l in range(DEPTH):
        o_ref[l, 0:1, :] = jnp.broadcast_to(lam[l:l + 1], (1, 128))
        o_ref[l, 1:2, :] = jnp.broadcast_to(post[l:l + 1], (1, 128))


def _lam_consts(lq1, lk1, lq2, lk2):
    init = np.array([[0.8 - 0.6 * math.exp(-0.3 * l)] for l in range(DEPTH)], np.float32)
    return pl.pallas_call(
        _lam_kernel,
        out_shape=jax.ShapeDtypeStruct((DEPTH, 2, 128), jnp.float32),
        name="diff_lambda",
    )(lq1, lk1, lq2, lk2, jnp.asarray(init))


N_DROW = 2 * NA_WIN_ROWS - 1
N_DCOL = 2 * NA_WIN_COLS - 1


def _bias_kernel(rpb_ref, o_ref):
    l = pl.program_id(0)
    h = pl.program_id(1)
    base = (l * NA_HEADS + h) * (N_DROW * N_DCOL)
    cq = lax.broadcasted_iota(jnp.int32, (GRID_W, 128), 0)
    lane = lax.broadcasted_iota(jnp.int32, (GRID_W, 128), 1)
    ck = lane & (GRID_W - 1)
    dcol = jnp.clip(ck - cq, -(NA_WIN_COLS - 1), NA_WIN_COLS - 1) + (NA_WIN_COLS - 1)
    hi = lane >= GRID_W
    for a in range(N_DROW - 1):
        acc = jnp.zeros((GRID_W, 128), jnp.float32)
        for j in range(N_DCOL):
            lo_v = rpb_ref[base + a * N_DCOL + j]
            hi_v = rpb_ref[base + (a + 1) * N_DCOL + j]
            acc = jnp.where(dcol == j, jnp.where(hi, hi_v, lo_v), acc)
        o_ref[0, 0, a] = acc


def _bias_tiles(na_rpb):
    return pl.pallas_call(
        _bias_kernel,
        grid=(DEPTH, NA_HEADS),
        in_specs=[pl.BlockSpec(memory_space=pltpu.SMEM)],
        out_specs=pl.BlockSpec((1, 1, N_DROW - 1, GRID_W, 128), lambda l, h: (l, h, 0, 0, 0)),
        out_shape=jax.ShapeDtypeStruct((DEPTH, NA_HEADS, N_DROW - 1, GRID_W, 128), jnp.float32),
        compiler_params=_params("parallel", "parallel"),
        name="na_bias_tiles",
    )(na_rpb.reshape(-1))


def _row_group(i):
    tiles_prompt = N_PROMPT // TM
    return jnp.where(i < tiles_prompt, 0, 1 + (i - tiles_prompt) // (DEC_SEQ // TM))


def _inproj_kernel(x_ref, g_ref, sh_ref, sc_ref, w_ref, pa_ref, pb_ref, pc_ref, pd_ref):
    h = _rms(x_ref[...], g_ref[0]) * (1.0 + sc_ref[...]) + sh_ref[...]
    hb = _bf(h)
    off = 0
    for ref in (pa_ref, pb_ref, pc_ref, pd_ref):
        n = ref.shape[1]
        ref[...] = _dot(hb, w_ref[0, :, off:off + n])
        off += n


def _inproj(l, x, g_mix, mod, w_in_p):
    def mod_spec(j):
        return pl.BlockSpec((None, None, 1, D_MODEL), lambda i: (l, _row_group(i), 0, j))

    segs = (SEG_A, SEG_B, SEG_C, SEG_D)
    return pl.pallas_call(
        _inproj_kernel,
        grid=(N_TOK // TM,),
        in_specs=[
            pl.BlockSpec((TM, D_MODEL), lambda i: (i, 0)),
            pl.BlockSpec((1, 1, D_MODEL), lambda i: (l, 0, 0)),
            mod_spec(0), mod_spec(1),
            pl.BlockSpec((1, D_MODEL, IN_COLS_P), lambda i: (l, 0, 0)),
        ],
        out_specs=[pl.BlockSpec((TM, n), lambda i: (i, 0)) for n in segs],
        out_shape=[jax.ShapeDtypeStruct((N_TOK, n), jnp.float32) for n in segs],
        compiler_params=_params("parallel"),
        name="inproj",
    )(x, g_mix, mod, mod, w_in_p)


def _dense_a_kernel(pa_ref, o_ref):
    scale = HEAD_DIM ** -0.5
    for h in range(NA_HEADS):
        q = _bf(pa_ref[:, HEAD_DIM * h:HEAD_DIM * (h + 1)])
        k = _bf(pa_ref[:, WIDTH + HEAD_DIM * h:WIDTH + HEAD_DIM * (h + 1)])
        v = _bf(pa_ref[:, 2 * WIDTH + HEAD_DIM * h:2 * WIDTH + HEAD_DIM * (h + 1)])
        (p,) = _softmax_parts([_dot_nt(q, k) * scale])
        o_ref[:, HEAD_DIM * h:HEAD_DIM * (h + 1)] = _dot(_bf(p), v)


def _mix_a_prompt(pa):
    return pl.pallas_call(
        _dense_a_kernel,
        grid=(BATCH,),
        in_specs=[pl.BlockSpec((SEQ, SEG_A), lambda b: (b, 0))],
        out_specs=pl.BlockSpec((SEQ, WIDTH), lambda b: (b, 0)),
        out_shape=jax.ShapeDtypeStruct((N_TOK, WIDTH), jnp.float32),
        compiler_params=_params("parallel"),
        name="mix_a_prompt",
    )(pa)


def _na_row_groups():
    kh = min(NA_WIN_ROWS, GRID_ROWS)
    r0s = [min(max(r - kh // 2, 0), GRID_ROWS - kh) for r in range(GRID_ROWS)]
    groups = []
    for r, r0 in enumerate(r0s):
        if groups and groups[-1][2] == r0:
            groups[-1][1] = r
        else:
            groups.append([r, r, r0])
    return kh, [tuple(g) for g in groups]


def _na_kernel(pa_ref, ck_ref, cv_ref, tt_ref, prev_ref, o_ref):
    del prev_ref
    scale = HEAD_DIM ** -0.5
    kh, groups = _na_row_groups()
    lk = kh * GRID_W

    def in_window(n_rows):
        cq = lax.broadcasted_iota(jnp.int32, (n_rows * GRID_W, lk), 0) & (GRID_W - 1)
        ck = lax.broadcasted_iota(jnp.int32, (n_rows * GRID_W, lk), 1) & (GRID_W - 1)
        c0 = jnp.clip(cq - NA_WIN_COLS // 2, 0, GRID_W - NA_WIN_COLS)
        return (ck >= c0) & (ck < c0 + NA_WIN_COLS)

    for h in range(NA_HEADS):
        q = _bf(pa_ref[:, HEAD_DIM * h:HEAD_DIM * (h + 1)])
        k = _bf(pa_ref[:, WIDTH + HEAD_DIM * h:WIDTH + HEAD_DIM * (h + 1)])
        v = _bf(pa_ref[:, 2 * WIDTH + HEAD_DIM * h:2 * WIDTH + HEAD_DIM * (h + 1)])
        kc = _bf(ck_ref[h])
        vc = _bf(cv_ref[h])
        for (r_lo, r_hi, r0) in groups:
            rows = slice(r_lo * GRID_W, (r_hi + 1) * GRID_W)
            keys = slice(r0 * GRID_W, r0 * GRID_W + lk)
            qg = q[rows]
            bias = jnp.concatenate([
                jnp.concatenate([tt_ref[h, r0 + 2 * j - r + NA_WIN_ROWS - 1] for j in range(kh // 2)], axis=1)
                for r in range(r_lo, r_hi + 1)], axis=0)
            s_loc = jnp.where(in_window(r_hi - r_lo + 1),_dot_nt(qg, k[keys]) * scale + bias, NEG_INF)
            s_ctx = _dot_nt(qg, kc) * scale
            p_ctx, p_loc = _softmax_parts([s_ctx, s_loc])
            o = _dot(_bf(p_ctx), vc) + _dot(_bf(p_loc), v[keys])
            o_ref[rows, HEAD_DIM * h:HEAD_DIM * (h + 1)] = o


def _mix_a_sample(l, pa, cache_k, cache_v, tt, o_prev):
    first = N_PROMPT // DEC_SEQ
    cache_spec = pl.BlockSpec((None, None, NA_HEADS, PAST_LEN, HEAD_DIM), lambda b: (b, l, 0, 0, 0))
    return pl.pallas_call(
        _na_kernel,
        grid=(DEC_BATCH,),
        in_specs=[
            pl.BlockSpec((DEC_SEQ, SEG_A), lambda b: (first + b, 0)),
            cache_spec, cache_spec,
            pl.BlockSpec((None, NA_HEADS, N_DROW - 1, GRID_W, 128), lambda b: (l, 0, 0, 0, 0)),
            pl.BlockSpec(memory_space=pl.ANY),
        ],
        out_specs=pl.BlockSpec((DEC_SEQ, WIDTH), lambda b: (first + b, 0)),
        out_shape=jax.ShapeDtypeStruct((N_TOK, WIDTH), jnp.float32),
        input_output_aliases={4: 0},
        compiler_params=_params("parallel"),
        name="mix_a_sample",
    )(pa, cache_k, cache_v, tt, o_prev)


QB = 256


def _diff_core(q, k_all, v_all, lam, post, g, o_ref):
    scale = DIFF_QK_DIM ** -0.5
    lane = lax.broadcasted_iota(jnp.int32, (QB, 2 * DIFF_QK_DIM), 1)
    first = lane < DIFF_QK_DIM
    for h in range(DIFF_HEADS):
        for qi in range(q.shape[0] // QB):
            rows = slice(QB * qi, QB * (qi + 1))
            qh = q[rows, 64 * h:64 * (h + 1)]
            q1 = _bf(jnp.where(first, qh, 0.0))
            q2 = _bf(jnp.where(first, 0.0, qh))
            (p1,) = _softmax_parts([_dot_nt(q1, k_all[h]) * scale])
            (p2,) = _softmax_parts([_dot_nt(q2, k_all[h]) * scale])
            o = _dot(_bf(p1 - lam * p2), v_all[h])
            o_ref[rows, 64 * h:64 * (h + 1)] = _rms(o, g) * post


def _diff_prompt_kernel(pb_ref, cst_ref, g_ref, o_ref):
    lam = cst_ref[0, 0:1, 0:1]
    post = cst_ref[0, 1:2, 0:1]
    k_all = [_bf(pb_ref[:, WIDTH + 64 * h:WIDTH + 64 * (h + 1)]) for h in range(DIFF_HEADS)]
    v_all = [_bf(pb_ref[:, 2 * WIDTH + 64 * h:2 * WIDTH + 64 * (h + 1)]) for h in range(DIFF_HEADS)]
    _diff_core(pb_ref[:, 0:WIDTH], k_all, v_all, lam, post, g_ref[0], o_ref)


def _mix_b_prompt(l, pb, cst, g_subln):
    return pl.pallas_call(
        _diff_prompt_kernel,
        grid=(BATCH,),
        in_specs=[
            pl.BlockSpec((SEQ, SEG_B), lambda b: (b, 0)),
            pl.BlockSpec((1, 2, 128), lambda b: (l, 0, 0)),
            pl.BlockSpec((1, 1, DIFF_V_DIM), lambda b: (l, 0, 0)),
        ],
        out_specs=pl.BlockSpec((SEQ, WIDTH), lambda b: (b, 0)),
        out_shape=jax.ShapeDtypeStruct((N_TOK, WIDTH), jnp.float32),
        compiler_params=_params("parallel"),
        name="mix_b_prompt",
    )(pb, cst, g_subln)


def _diff_sample_kernel(pb_ref, ck_ref, cv_ref, cos_ref, sin_ref, cst_ref, g_ref, prev_ref, o_ref):
    del prev_ref
    lam = cst_ref[0, 0:1, 0:1]
    post = cst_ref[0, 1:2, 0:1]
    cos = cos_ref[...]
    sin = sin_ref[...]
    q = _rope(pb_ref[:, 0:WIDTH], cos, sin)
    k = _rope(pb_ref[:, WIDTH:2 * WIDTH], cos, sin)
    k_all = [jnp.concatenate([_bf(ck_ref[h]), _bf(k[:, 64 * h:64 * (h + 1)])], axis=0)
             for h in range(DIFF_HEADS)]
    v_all = [jnp.concatenate([_bf(cv_ref[h]), _bf(pb_ref[:, 2 * WIDTH + 64 * h:2 * WIDTH + 64 * (h + 1)])], axis=0)
             for h in range(DIFF_HEADS)]
    _diff_core(q, k_all, v_all, lam, post, g_ref[0], o_ref)


def _mix_b_sample(l, pb, cache_k, cache_v, cos, sin, cst, g_subln, o_prev):
    first = N_PROMPT // DEC_SEQ
    cache_spec = pl.BlockSpec((None, None, DIFF_HEADS, PAST_LEN, 64), lambda b: (b, l, 0, 0, 0))
    tab_spec = pl.BlockSpec((DEC_SEQ, WIDTH), lambda b: (0, 0))
    return pl.pallas_call(
        _diff_sample_kernel,
        grid=(DEC_BATCH,),
        in_specs=[
            pl.BlockSpec((DEC_SEQ, SEG_B), lambda b: (first + b, 0)),
            cache_spec, cache_spec, tab_spec, tab_spec,
            pl.BlockSpec((1, 2, 128), lambda b: (l, 0, 0)),
            pl.BlockSpec((1, 1, DIFF_V_DIM), lambda b: (l, 0, 0)),
            pl.BlockSpec(memory_space=pl.ANY),
        ],
        out_specs=pl.BlockSpec((DEC_SEQ, WIDTH), lambda b: (first + b, 0)),
        out_shape=jax.ShapeDtypeStruct((N_TOK, WIDTH), jnp.float32),
        input_output_aliases={7: 0},
        compiler_params=_params("parallel"),
        name="mix_b_sample",
    )(pb, cache_k, cache_v, cos, sin, cst, g_subln, o_prev)


C_QN, C_QP, C_CKV, C_KPE = 0, 256, 384, 512


def _mla_core(qn, qp, ckv_all, kpe_all, wuk_ref, wuv_ref, o_ref):
    scale = (MLA_NOPE + MLA_ROPE) ** -0.5
    kn = _bf(_dot(ckv_all, _bf(wuk_ref[0])))
    v = _bf(_dot(ckv_all, _bf(wuv_ref[0])))
    for h in range(MLA_HEADS):
        kn_h = kn[:, MLA_NOPE * h:MLA_NOPE * (h + 1)]
        v_h = v[:, MLA_V * h:MLA_V * (h + 1)]
        for qi in range(qn.shape[0] // QB):
            rows = slice(QB * qi, QB * (qi + 1))
            s = (_dot_nt(_bf(qn[rows, MLA_NOPE * h:MLA_NOPE * (h + 1)]), kn_h)
                 + _dot_nt(_bf(qp[rows, MLA_ROPE * h:MLA_ROPE * (h + 1)]), kpe_all)) * scale
            (p,) = _softmax_parts([s])
            o_ref[rows, MLA_V * h:MLA_V * (h + 1)] = _dot(_bf(p), v_h)


def _mla_prompt_kernel(pc_ref, gckv_ref, wuk_ref, wuv_ref, o_ref, ckv_ref):
    ckv = _rms(pc_ref[:, C_CKV:C_CKV + MLA_KV_RANK], gckv_ref[0])
    ckv_ref[...] = ckv
    _mla_core(pc_ref[:, C_QN:C_QN + 256], pc_ref[:, C_QP:C_QP + 128], _bf(ckv),
              _bf(pc_ref[:, C_KPE:C_KPE + MLA_ROPE]), wuk_ref, wuv_ref, o_ref)


def _mix_c_prompt(l, pc, g_ckv, w_uk, w_uv):
    w_spec = pl.BlockSpec((1, MLA_KV_RANK, 256), lambda b: (l, 0, 0))
    return pl.pallas_call(
        _mla_prompt_kernel,
        grid=(BATCH,),
        in_specs=[
            pl.BlockSpec((SEQ, SEG_C), lambda b: (b, 0)),
            pl.BlockSpec((1, 1, MLA_KV_RANK), lambda b: (l, 0, 0)),
            w_spec, w_spec,
        ],
        out_specs=[pl.BlockSpec((SEQ, WIDTH), lambda b: (b, 0)),
                   pl.BlockSpec((SEQ, MLA_KV_RANK), lambda b: (b, 0))],
        out_shape=[jax.ShapeDtypeStruct((N_TOK, WIDTH), jnp.float32),
                   jax.ShapeDtypeStruct((N_PROMPT, MLA_KV_RANK), jnp.float32)],
        compiler_params=_params("parallel"),
        name="mix_c_prompt",
    )(pc, g_ckv, w_uk, w_uv)


def _mla_sample_kernel(pc_ref, cckv_ref, ckpe_ref, cosq_ref, sinq_ref, cosk_ref, sink_ref,
                       gckv_ref, wuk_ref, wuv_ref, prev_ref, o_ref):
    del prev_ref
    ckv = _rms(pc_ref[:, C_CKV:C_CKV + MLA_KV_RANK], gckv_ref[0])
    ckv_all = jnp.concatenate([_bf(cckv_ref[...]), _bf(ckv)], axis=0)
    kpe = _rope(pc_ref[:, C_KPE:C_KPE + 128], cosk_ref[...], sink_ref[...])[:, 0:MLA_ROPE]
    kpe_all = jnp.concatenate([_bf(ckpe_ref[...]), _bf(kpe)], axis=0)
    qp = _rope(pc_ref[:, C_QP:C_QP + 128], cosq_ref[...], sinq_ref[...])
    _mla_core(pc_ref[:, C_QN:C_QN + 256], qp, ckv_all, kpe_all, wuk_ref, wuv_ref, o_ref)


def _mix_c_sample(l, pc, cache_ckv, cache_kpe, cosq, sinq, cosk, sink, g_ckv, w_uk, w_uv, o_prev):
    first = N_PROMPT // DEC_SEQ
    w_spec = pl.BlockSpec((1, MLA_KV_RANK, 256), lambda b: (l, 0, 0))
    tab_spec = pl.BlockSpec((DEC_SEQ, 128), lambda b: (0, 0))
    return pl.pallas_call(
        _mla_sample_kernel,
        grid=(DEC_BATCH,),
        in_specs=[
            pl.BlockSpec((DEC_SEQ, SEG_C), lambda b: (first + b, 0)),
            pl.BlockSpec((None, None, PAST_LEN, MLA_KV_RANK), lambda b: (b, l, 0, 0)),
            pl.BlockSpec((None, None, PAST_LEN, MLA_ROPE), lambda b: (b, l, 0, 0)),
            tab_spec, tab_spec, tab_spec, tab_spec,
            pl.BlockSpec((1, 1, MLA_KV_RANK), lambda b: (l, 0, 0)),
            w_spec, w_spec,
            pl.BlockSpec(memory_space=pl.ANY),
        ],
        out_specs=pl.BlockSpec((DEC_SEQ, WIDTH), lambda b: (first + b, 0)),
        out_shape=jax.ShapeDtypeStruct((N_TOK, WIDTH), jnp.float32),
        input_output_aliases={10: 0},
        compiler_params=_params("parallel"),
        name="mix_c_sample",
    )(pc, cache_ckv, cache_kpe, cosq, sinq, cosk, sink, g_ckv, w_uk, w_uv, o_prev)


def _gelu_tanh(x):
    return 0.5 * x * (1.0 + jnp.tanh(math.sqrt(2.0 / math.pi) * (x + 0.044715 * (x * x * x))))


def _sgu_kernel(pd_ref, g_ref, w_ref, bt_ref, o_ref):
    u = _gelu_tanh(pd_ref[:, 0:WIDTH])
    v = _gelu_tanh(pd_ref[:, WIDTH:2 * WIDTH])
    grp = lax.broadcasted_iota(jnp.int32, (1, WIDTH), 1) // SGU_GROUP_DIM
    v2 = v * v
    ms = jnp.zeros_like(v)
    for g in range(SGU_GROUPS):
        sel = grp == g
        tot = jnp.sum(jnp.where(sel, v2, 0.0), axis=-1, keepdims=True)
        ms = jnp.where(sel, tot * (1.0 / SGU_GROUP_DIM), ms)
    vg = _bf(v * lax.rsqrt(ms + EPS) * g_ref[0])
    bt = bt_ref[0]
    for c in range(pd_ref.shape[0] // SGU_CHUNK):
        rows = slice(SGU_CHUNK * c, SGU_CHUNK * (c + 1))
        mixed = jnp.zeros((SGU_CHUNK, WIDTH), jnp.float32)
        for g in range(SGU_GROUPS):
            full = _dot(_bf(w_ref[0, g]), vg[rows]) + bt[:, g:g + 1]
            mixed = jnp.where(grp == g, full, mixed)
        o_ref[rows] = u[rows] * mixed


def _mix_d(l, pd, sgu_g, sgu_w, sgu_bt):
    return pl.pallas_call(
        _sgu_kernel,
        grid=(N_TOK // TM,),
        in_specs=[
            pl.BlockSpec((TM, SEG_D), lambda i: (i, 0)),
            pl.BlockSpec((1, 1, WIDTH), lambda i: (l, 0, 0)),
            pl.BlockSpec((1, SGU_GROUPS, SGU_CHUNK, SGU_CHUNK), lambda i: (l, 0, 0, 0)),
            pl.BlockSpec((1, SGU_CHUNK, SGU_GROUPS), lambda i: (l, 0, 0)),
        ],
        out_specs=pl.BlockSpec((TM, WIDTH), lambda i: (i, 0)),
        out_shape=jax.ShapeDtypeStruct((N_TOK, WIDTH), jnp.float32),
        compiler_params=_params("parallel"),
        name="mix_d",
    )(pd, sgu_g, sgu_w, sgu_bt)


FF_CHUNK = 1024


def _outffn_kernel(final, x_ref, oa_ref, ob_ref, oc_ref, od_ref, wout_ref, g1_ref, gffn_ref,
                   sh2_ref, sc2_ref, g2_ref, w1_ref, w2_ref, gfin_ref, y_ref):
    acc = jnp.zeros((TM, D_MODEL), jnp.float32)
    for i, ref in enumerate((oa_ref, ob_ref, oc_ref, od_ref)):
        acc += _dot(_bf(ref[...]), wout_ref[0, WIDTH * i:WIDTH * (i + 1), :])
    x1 = x_ref[...] + g1_ref[...] * acc
    hf = _bf(_rms(x1, gffn_ref[0]) * (1.0 + sc2_ref[...]) + sh2_ref[...])
    acc = jnp.zeros((TM, D_MODEL), jnp.float32)
    for c in range(D_FF // FF_CHUNK):
        cols = slice(FF_CHUNK * c, FF_CHUNK * (c + 1))
        a = jnp.square(jnp.maximum(_dot(hf, w1_ref[0, :, cols]), 0.0))
        acc += _dot(_bf(a), w2_ref[0, cols, :])
    y = x1 + g2_ref[...] * acc
    if final:
        y = _rms(y, gfin_ref[...])
    y_ref[...] = y


def _outffn(l, x, oa, ob, oc, od, w_out, g_ffn, mod, w1, w2, g_final):
    def mod_spec(j):
        return pl.BlockSpec((None, None, 1, D_MODEL), lambda i: (l, _row_group(i), 0, j))

    def resident(shape):
        return pl.BlockSpec(shape, lambda i: (l,) + (0,) * (len(shape) - 1), pipeline_mode=pl.Buffered(1))

    o_spec = pl.BlockSpec((TM, WIDTH), lambda i: (i, 0))
    return pl.pallas_call(
        functools.partial(_outffn_kernel, l == DEPTH - 1),
        grid=(N_TOK // TM,),
        in_specs=[
            pl.BlockSpec((TM, D_MODEL), lambda i: (i, 0)),
            o_spec, o_spec, o_spec, o_spec,
            resident((1, 4 * WIDTH, D_MODEL)),
            mod_spec(2),
            pl.BlockSpec((1, 1, D_MODEL), lambda i: (l, 0, 0)),
            mod_spec(3), mod_spec(4), mod_spec(5),
            resident((1, D_MODEL, D_FF)),
            resident((1, D_FF, D_MODEL)),
            pl.BlockSpec((1, D_MODEL), lambda i: (0, 0)),
        ],
        out_specs=pl.BlockSpec((TM, D_MODEL), lambda i: (i, 0)),
        out_shape=jax.ShapeDtypeStruct((N_TOK, D_MODEL), jnp.float32),
        compiler_params=_params("parallel"),
        name="outffn",
    )(x, oa, ob, oc, od, w_out, mod, g_ffn, mod, mod, mod, w1, w2, g_final)


def _rope32_tables():
    t = np.arange(DEC_SEQ)
    rows, cols = (t // GRID_W).astype(np.float64), (t % GRID_W).astype(np.float64)
    half = 8
    freqs = ROPE_BASE ** (-np.arange(half, dtype=np.float64) / half)
    cos, sin = [], []
    for pos in (rows, cols):
        ang = pos[:, None] * freqs[None, :]
        cos += [np.cos(ang), np.cos(ang)]
        sin += [-np.sin(ang), np.sin(ang)]
    return np.concatenate(cos, axis=1).astype(np.float32), np.concatenate(sin, axis=1).astype(np.float32)


def _rope_tables():
    c32, s32 = _rope32_tables()
    tile = lambda a, n: np.tile(a, (1, n))
    pad = np.zeros((DEC_SEQ, 96), np.float32)
    cos_k = np.concatenate([c32, pad + 1.0], axis=1)
    sin_k = np.concatenate([s32, pad], axis=1)
    return (tile(c32, 8), tile(s32, 8),
            tile(c32, 4), tile(s32, 4),
            cos_k, sin_k)


def _permute_w_in(w_in):
    offs = np.cumsum((0, 256, 256, 256, 256, 256, 256, 384, 128, 32, 256, 256))
    qc = np.arange(offs[6], offs[7]).reshape(MLA_HEADS, MLA_NOPE + MLA_ROPE)
    idx_abc = np.concatenate([np.arange(0, offs[6]), qc[:, :MLA_NOPE].reshape(-1), qc[:, MLA_NOPE:].reshape(-1),
                              np.arange(offs[7], offs[9])])
    pad = jnp.zeros((DEPTH, D_MODEL, SEG_C_PAD), w_in.dtype)
    return _bf(jnp.concatenate([w_in[:, :, idx_abc], pad, w_in[:, :, offs[9]:]], axis=-1))


def _split_heads(p, n_heads):
    return p.reshape(BATCH, SEQ, n_heads, -1).transpose(0, 2, 1, 3)


def kernel(x_prompt, x_sample, cache_na_k, cache_na_v, cache_diff_k, cache_diff_v, cache_mla_ckv, cache_mla_kpe, c, c_ctx, w_ada, b_ada, g_mix, g_ffn, w_in, w_out, na_rpb, diff_lq1, diff_lk1, diff_lq2, diff_lk2, diff_g_subln, mla_g_ckv, mla_w_uk, mla_w_uv, sgu_g, sgu_w, sgu_b, w_ff1, w_ff2, g_final):
    f32 = jnp.float32
    m = jnp.concatenate([c_ctx[None, :], c, jnp.zeros((N_MOD_ROWS - 1 - DEC_BATCH, D_MODEL), f32)], axis=0)
    mod = _ada(m, w_ada, b_ada).reshape(DEPTH, N_MOD_ROWS, 1, 6 * D_MODEL)
    cst = _lam_consts(diff_lq1, diff_lk1, diff_lq2, diff_lk2)
    tt = _bias_tiles(na_rpb)
    cos_b, sin_b, cos_q, sin_q, cos_k, sin_k = [jnp.asarray(t) for t in _rope_tables()]

    w_in_p = _permute_w_in(w_in)
    w_out_b, w1_b, w2_b = _bf(w_out), _bf(w_ff1), _bf(w_ff2)
    g_mix3 = g_mix.reshape(DEPTH, 1, D_MODEL)
    g_ffn3 = g_ffn.reshape(DEPTH, 1, D_MODEL)
    g_sub3 = diff_g_subln.reshape(DEPTH, 1, DIFF_V_DIM)
    g_ckv3 = mla_g_ckv.reshape(DEPTH, 1, MLA_KV_RANK)
    sgu_g3 = sgu_g.reshape(DEPTH, 1, WIDTH)
    sgu_bt = sgu_b.transpose(0, 2, 1)
    g_fin2 = g_final.reshape(1, D_MODEL)

    x = jnp.concatenate([x_prompt.reshape(N_PROMPT, D_MODEL), x_sample.reshape(N_SAMPLE, D_MODEL)], axis=0)
    new = ([], [], [], [], [], [])
    for l in range(DEPTH):
        pa, pb, pc, pd = _inproj(l, x, g_mix3, mod, w_in_p)
        oa = _mix_a_sample(l, pa, cache_na_k, cache_na_v, tt, _mix_a_prompt(pa))
        ob = _mix_b_sample(l, pb, cache_diff_k, cache_diff_v, cos_b, sin_b, cst, g_sub3,
                           _mix_b_prompt(l, pb, cst, g_sub3))
        oc_p, ckv_p = _mix_c_prompt(l, pc, g_ckv3, mla_w_uk, mla_w_uv)
        oc = _mix_c_sample(l, pc, cache_mla_ckv, cache_mla_kpe, cos_q, sin_q, cos_k, sin_k, g_ckv3,
                           mla_w_uk, mla_w_uv, oc_p)
        od = _mix_d(l, pd, sgu_g3, sgu_w, sgu_bt)
        x = _outffn(l, x, oa, ob, oc, od, w_out_b, g_ffn3, mod, w1_b, w2_b, g_fin2)
        new[0].append(_split_heads(pa[:N_PROMPT, WIDTH:2 * WIDTH], NA_HEADS))
        new[1].append(_split_heads(pa[:N_PROMPT, 2 * WIDTH:], NA_HEADS))
        new[2].append(_split_heads(pb[:N_PROMPT, WIDTH:2 * WIDTH], DIFF_HEADS))
        new[3].append(_split_heads(pb[:N_PROMPT, 2 * WIDTH:], DIFF_HEADS))
        new[4].append(ckv_p.reshape(BATCH, SEQ, MLA_KV_RANK))
        new[5].append(pc[:N_PROMPT, C_KPE:C_KPE + MLA_ROPE].reshape(BATCH, SEQ, MLA_ROPE))
    y_prompt = x[:N_PROMPT].reshape(BATCH, SEQ, D_MODEL)
    y_sample = x[N_PROMPT:].reshape(DEC_BATCH, DEC_SEQ, D_MODEL)
    return (y_prompt, y_sample) + tuple(jnp.stack(t, axis=1) for t in new)
```

```python
import functools
import math

import numpy as np
import jax
import jax.numpy as jnp
from jax import lax
from jax.experimental import pallas as pl
from jax.experimental.pallas import tpu as pltpu

D_MODEL = 1024
BATCH = 16
SEQ = 256
DEPTH = 4
DEC_BATCH = 2
DEC_SEQ = 1024
PAST_LEN = 512
GRID_W = 64
GRID_ROWS = DEC_SEQ // GRID_W
HEAD_DIM = 64
NA_HEADS = 4
NA_WIN_ROWS = 8
NA_WIN_COLS = 16
DIFF_HEADS = 4
DIFF_QK_DIM = 32
DIFF_V_DIM = 64
MLA_HEADS = 4
MLA_NOPE = 64
MLA_ROPE = 32
MLA_V = 64
MLA_KV_RANK = 128
SGU_GROUPS = 4
SGU_GROUP_DIM = 64
SGU_CHUNK = 128
D_FF = 4 * D_MODEL
ROPE_BASE = 10000.0
EPS = 1e-6
NEG_INF = -1e30
LOG2E = 1.4426950408889634

N_HEADS = 4
N_PAIRS = N_HEADS // 2
LANES = 128
WIDTH = 256
N_PROMPT = BATCH * SEQ
N_SAMPLE = DEC_BATCH * DEC_SEQ
N_TOK = N_PROMPT + N_SAMPLE
N_MOD_ROWS = 8

SEG_A = 3 * WIDTH
SEG_B = 3 * WIDTH
SEG_C = 640
SEG_D = 2 * WIDTH
SEG_C_PAD = 96
IN_COLS_P = SEG_A + SEG_B + SEG_C + SEG_D

TM = 512
TILES_PROMPT = N_PROMPT // TM
TILES_SAMPLE = N_SAMPLE // TM
PB = 2
QB = 256
VMEM_LIMIT = 56 * 1024 * 1024


def _bf(x):
    return x.astype(jnp.bfloat16)


def _dot(a, b):
    return jnp.dot(a, b, preferred_element_type=jnp.float32)


def _dot_nt(a, b):
    return lax.dot_general(a, b, (((1,), (1,)), ((), ())), preferred_element_type=jnp.float32)


def _rms(x, g):
    ms = jnp.mean(x * x, axis=-1, keepdims=True)
    return x * lax.rsqrt(ms + EPS) * g


def _lane_range(lo, hi, width=LANES):
    lane = lax.broadcasted_iota(jnp.int32, (1, width), 1)
    return (lane >= lo) & (lane < hi)


def _ones_col(rows):
    lane = lax.broadcasted_iota(jnp.int32, (rows, LANES), 1)
    return jnp.where(lane == 0, 1.0, 0.0).astype(jnp.bfloat16)


def _exp_weights(parts):
    m = functools.reduce(jnp.maximum, [jnp.max(s, axis=-1, keepdims=True) for s in parts])
    return [_bf(jnp.exp2(s - m)) for s in parts]


def _normalised(o_ext):
    return o_ext[:, 0:LANES] * (1.0 / o_ext[:, LANES:LANES + 1])


def _swap8(x):
    lane = lax.broadcasted_iota(jnp.int32, (1, LANES), 1)
    return jnp.where((lane & 15) < 8, pltpu.roll(x, LANES - 8, 1), pltpu.roll(x, 8, 1))


def _rope(x, cos, sin):
    outs = []
    for c in range(x.shape[1] // LANES):
        sl = slice(LANES * c, LANES * (c + 1))
        xc = x[:, sl]
        outs.append(xc * cos[:, sl] + _swap8(xc) * sin[:, sl])
    return outs[0] if len(outs) == 1 else jnp.concatenate(outs, axis=1)


def _tile4(x):
    return x + pltpu.roll(x, 32, 1) + pltpu.roll(x, 64, 1) + pltpu.roll(x, 96, 1)


def _params(*sem):
    return pltpu.CompilerParams(dimension_semantics=sem, vmem_limit_bytes=VMEM_LIMIT)


ADA_TN = 1536


def _ada_kernel(m_ref, w_ref, b_ref, o_ref):
    m = m_ref[...]
    s = m * jax.nn.sigmoid(m)
    o_ref[0] = _dot(_bf(s), _bf(w_ref[0])) + b_ref[0]


def _ada(m, w_ada, b_ada):
    n = 6 * D_MODEL
    return pl.pallas_call(
        _ada_kernel,
        grid=(DEPTH, n // ADA_TN),
        in_specs=[
            pl.BlockSpec((N_MOD_ROWS, D_MODEL), lambda l, j: (0, 0)),
            pl.BlockSpec((1, D_MODEL, ADA_TN), lambda l, j: (l, 0, j)),
            pl.BlockSpec((1, 1, ADA_TN), lambda l, j: (l, 0, j)),
        ],
        out_specs=pl.BlockSpec((1, N_MOD_ROWS, ADA_TN), lambda l, j: (l, 0, j)),
        out_shape=jax.ShapeDtypeStruct((DEPTH, N_MOD_ROWS, n), jnp.float32),
        compiler_params=_params("parallel", "parallel"),
        name="ada",
    )(m, w_ada, b_ada.reshape(DEPTH, 1, n))


def _lam_kernel(lq1_ref, lk1_ref, lq2_ref, lk2_ref, init_ref, o_ref):
    init = init_ref[...]
    a = jnp.exp(jnp.sum(lq1_ref[...] * lk1_ref[...], axis=-1, keepdims=True))
    b = jnp.exp(jnp.sum(lq2_ref[...] * lk2_ref[...], axis=-1, keepdims=True))
    lam = a - b + init
    post = 1.0 - init
    for l in range(DEPTH):
        o_ref[l, 0:1, :] = jnp.broadcast_to(lam[l:l + 1], (1, LANES))
        o_ref[l, 1:2, :] = jnp.broadcast_to(post[l:l + 1], (1, LANES))


def _lam_consts(lq1, lk1, lq2, lk2):
    init = np.array([[0.8 - 0.6 * math.exp(-0.3 * l)] for l in range(DEPTH)], np.float32)
    return pl.pallas_call(
        _lam_kernel,
        out_shape=jax.ShapeDtypeStruct((DEPTH, 2, LANES), jnp.float32),
        name="diff_lambda",
    )(lq1, lk1, lq2, lk2, jnp.asarray(init))


N_DROW = 2 * NA_WIN_ROWS - 1
N_DCOL = 2 * NA_WIN_COLS - 1


def _bias_kernel(rpb_ref, o_ref):
    l = pl.program_id(0)
    h = pl.program_id(1)
    base = (l * NA_HEADS + h) * (N_DROW * N_DCOL)
    cq = lax.broadcasted_iota(jnp.int32, (GRID_W, LANES), 0)
    lane = lax.broadcasted_iota(jnp.int32, (GRID_W, LANES), 1)
    ck = lane & (GRID_W - 1)
    dcol = jnp.clip(ck - cq, -(NA_WIN_COLS - 1), NA_WIN_COLS - 1) + (NA_WIN_COLS - 1)
    hi = lane >= GRID_W
    for a in range(N_DROW - 1):
        acc = jnp.zeros((GRID_W, LANES), jnp.float32)
        for j in range(N_DCOL):
            lo_v = rpb_ref[base + a * N_DCOL + j]
            hi_v = rpb_ref[base + (a + 1) * N_DCOL + j]
            acc = jnp.where(dcol == j, jnp.where(hi, hi_v, lo_v), acc)
        o_ref[0, 0, a] = acc * LOG2E


def _bias_tiles(na_rpb):
    return pl.pallas_call(
        _bias_kernel,
        grid=(DEPTH, NA_HEADS),
        in_specs=[pl.BlockSpec(memory_space=pltpu.SMEM)],
        out_specs=pl.BlockSpec((1, 1, N_DROW - 1, GRID_W, LANES), lambda l, h: (l, h, 0, 0, 0)),
        out_shape=jax.ShapeDtypeStruct((DEPTH, NA_HEADS, N_DROW - 1, GRID_W, LANES), jnp.float32),
        compiler_params=_params("parallel", "parallel"),
        name="na_bias_tiles",
    )(na_rpb.reshape(-1))


def _row_group(i):
    return jnp.where(i < TILES_PROMPT, 0, 1 + (i - TILES_PROMPT) // (DEC_SEQ // TM))


def _x_specs(split, first):
    if not split:
        return [pl.BlockSpec((TM, D_MODEL), lambda i: (first + i, 0))]
    return [pl.BlockSpec((TM, D_MODEL), lambda i: (jnp.minimum(first + i, TILES_PROMPT - 1), 0)),
            pl.BlockSpec((TM, D_MODEL), lambda i: (jnp.maximum(first + i - TILES_PROMPT, 0), 0))]


def _read_x(x_refs, first):
    if len(x_refs) == 1:
        return x_refs[0][...]
    return jnp.where(first + pl.program_id(0) < TILES_PROMPT, x_refs[0][...], x_refs[1][...])


def _inproj_kernel(n_x, *refs):
    x_refs, (g_ref, sh_ref, sc_ref, w_ref, pa_ref, pb_ref, pc_ref, pd_ref) = refs[:n_x], refs[n_x:]
    h = _rms(_read_x(x_refs, 0), g_ref[0]) * (1.0 + sc_ref[...]) + sh_ref[...]
    hb = _bf(h)
    off = 0
    for ref in (pa_ref, pb_ref, pc_ref, pd_ref):
        n = ref.shape[1]
        ref[...] = _dot(hb, w_ref[0, :, off:off + n])
        off += n


def _inproj(l, xs, g_mix, mod, w_in_p):
    def mod_spec(j):
        return pl.BlockSpec((None, None, 1, D_MODEL), lambda i: (l, _row_group(i), 0, j))

    segs = (SEG_A, SEG_B, SEG_C, SEG_D)
    return pl.pallas_call(
        functools.partial(_inproj_kernel, len(xs)),
        grid=(N_TOK // TM,),
        in_specs=_x_specs(len(xs) == 2, 0) + [
            pl.BlockSpec((1, 1, D_MODEL), lambda i: (l, 0, 0)),
            mod_spec(0), mod_spec(1),
            pl.BlockSpec((1, D_MODEL, IN_COLS_P), lambda i: (l, 0, 0)),
        ],
        out_specs=[pl.BlockSpec((TM, n), lambda i: (i, 0)) for n in segs],
        out_shape=[jax.ShapeDtypeStruct((N_TOK, n), jnp.float32) for n in segs],
        compiler_params=_params("parallel"),
        name="inproj",
    )(*xs, g_mix, mod, mod, w_in_p)


FIRST_SAMPLE_BLOCK = N_PROMPT // DEC_SEQ


def _prompt_rows(width):
    return pl.BlockSpec((PB * SEQ, width), lambda b: (b, 0))


def _sample_rows(width):
    return pl.BlockSpec((DEC_SEQ, width), lambda b: (FIRST_SAMPLE_BLOCK + b, 0))


def _new_cache_spec(l, *tail):
    return pl.BlockSpec((PB, None) + tail, lambda b: (b, l) + (0,) * len(tail))


def _new_cache_shape(*tail):
    return jax.ShapeDtypeStruct((BATCH, DEPTH) + tail, jnp.float32)


def _old_cache_spec(l, *tail):
    return pl.BlockSpec((None, None) + tail, lambda b: (b, l) + (0,) * len(tail))


def _layer_spec(l, *tail):
    return pl.BlockSpec((1,) + tail, lambda b: (l,) + (0,) * len(tail))


ANY_SPEC = pl.BlockSpec(memory_space=pl.ANY)


def _write_heads(p_ref, rows, col0, out_ref, bb):
    for h in range(N_HEADS):
        out_ref[bb, h] = p_ref[rows, col0 + 64 * h:col0 + 64 * (h + 1)]


def _pair_cache(c_ref, j):
    return jnp.concatenate([c_ref[2 * j], c_ref[2 * j + 1]], axis=1)


def _stack_heads(qp):
    lo = _lane_range(0, 64)
    return jnp.concatenate([_bf(jnp.where(lo, qp, 0.0)), _bf(jnp.where(lo, 0.0, qp))], axis=0)


def _unstack_heads(o, n):
    return jnp.where(_lane_range(0, 64), o[0:n], o[n:2 * n])


def _dense_a_kernel(n_prev, *refs):
    pa_ref, (o_ref, kout_ref, vout_ref) = refs[0], refs[1 + n_prev:]
    c = HEAD_DIM ** -0.5 * LOG2E
    for bb in range(PB):
        rows = slice(SEQ * bb, SEQ * (bb + 1))
        for j in range(N_PAIRS):
            cols = slice(LANES * j, LANES * (j + 1))
            q = _stack_heads(pa_ref[rows, cols] * c)
            k = _bf(pa_ref[rows, WIDTH + LANES * j:WIDTH + LANES * (j + 1)])
            v = jnp.concatenate([_bf(pa_ref[rows, 2 * WIDTH + LANES * j:2 * WIDTH + LANES * (j + 1)]),
                                 _ones_col(SEQ)], axis=1)
            (e,) = _exp_weights([_dot_nt(q, k)])
            o_ref[rows, cols] = _unstack_heads(_normalised(_dot(e, v)), SEQ)
        _write_heads(pa_ref, rows, WIDTH, kout_ref, bb)
        _write_heads(pa_ref, rows, 2 * WIDTH, vout_ref, bb)


def _mix_a_prompt(l, pa, prev):
    cache = _new_cache_spec(l, NA_HEADS, SEQ, HEAD_DIM)
    n_prev = len(prev)
    return pl.pallas_call(
        functools.partial(_dense_a_kernel, n_prev),
        grid=(BATCH // PB,),
        in_specs=[_prompt_rows(SEG_A)] + [ANY_SPEC] * n_prev,
        out_specs=[_prompt_rows(WIDTH), cache, cache],
        out_shape=[jax.ShapeDtypeStruct((N_TOK, WIDTH), jnp.float32),
                   _new_cache_shape(NA_HEADS, SEQ, HEAD_DIM), _new_cache_shape(NA_HEADS, SEQ, HEAD_DIM)],
        input_output_aliases={1 + i: 1 + i for i in range(n_prev)},
        compiler_params=_params("parallel"),
        name="mix_a_prompt",
    )(pa, *prev)


def _na_row_groups():
    kh = min(NA_WIN_ROWS, GRID_ROWS)
    r0s = [min(max(r - kh // 2, 0), GRID_ROWS - kh) for r in range(GRID_ROWS)]
    groups = []
    for r, r0 in enumerate(r0s):
        if groups and groups[-1][2] == r0:
            groups[-1][1] = r
        else:
            groups.append([r, r, r0])
    return kh, [tuple(g) for g in groups]


def _na_kernel(pa_ref, ck_ref, cv_ref, tt_ref, prev_ref, o_ref):
    del prev_ref
    c = HEAD_DIM ** -0.5 * LOG2E
    kh, groups = _na_row_groups()
    lk = kh * GRID_W

    def in_window(n):
        cq = lax.broadcasted_iota(jnp.int32, (n, lk), 0) & (GRID_W - 1)
        ck = lax.broadcasted_iota(jnp.int32, (n, lk), 1) & (GRID_W - 1)
        c0 = jnp.clip(cq - NA_WIN_COLS // 2, 0, GRID_W - NA_WIN_COLS)
        return (ck >= c0) & (ck < c0 + NA_WIN_COLS)

    for j in range(N_PAIRS):
        cols = slice(LANES * j, LANES * (j + 1))
        q = _stack_heads(pa_ref[:, cols] * c)
        k = _bf(pa_ref[:, WIDTH + LANES * j:WIDTH + LANES * (j + 1)])
        v = jnp.concatenate([_bf(pa_ref[:, 2 * WIDTH + LANES * j:2 * WIDTH + LANES * (j + 1)]),
                             _ones_col(DEC_SEQ)], axis=1)
        kc = _bf(_pair_cache(ck_ref, j))
        vc = jnp.concatenate([_bf(_pair_cache(cv_ref, j)), _ones_col(PAST_LEN)], axis=1)
        for (r_lo, r_hi, r0) in groups:
            n = (r_hi - r_lo + 1) * GRID_W
            rows = slice(r_lo * GRID_W, r_lo * GRID_W + n)
            keys = slice(r0 * GRID_W, r0 * GRID_W + lk)
            qg = jnp.concatenate([q[rows], q[DEC_SEQ + r_lo * GRID_W:DEC_SEQ + r_lo * GRID_W + n]], axis=0)
            bias = jnp.concatenate([
                jnp.concatenate([tt_ref[2 * j + t, r0 + 2 * i - r + NA_WIN_ROWS - 1] for i in range(kh // 2)], axis=1)
                for t in range(2) for r in range(r_lo, r_hi + 1)], axis=0)
            s_loc = jnp.where(in_window(2 * n), _dot_nt(qg, k[keys]) + bias, NEG_INF)
            e_ctx, e_loc = _exp_weights([_dot_nt(qg, kc), s_loc])
            o = _normalised(_dot(e_ctx, vc) + _dot(e_loc, v[keys]))
            o_ref[rows, cols] = _unstack_heads(o, n)


def _mix_a_sample(l, pa, cache_k, cache_v, tt, o_prev):
    cache_spec = _old_cache_spec(l, NA_HEADS, PAST_LEN, HEAD_DIM)
    return pl.pallas_call(
        _na_kernel,
        grid=(DEC_BATCH,),
        in_specs=[_sample_rows(SEG_A), cache_spec, cache_spec,
                  pl.BlockSpec((None, NA_HEADS, N_DROW - 1, GRID_W, LANES), lambda b: (l, 0, 0, 0, 0)),
                  ANY_SPEC],
        out_specs=_sample_rows(WIDTH),
        out_shape=jax.ShapeDtypeStruct((N_TOK, WIDTH), jnp.float32),
        input_output_aliases={4: 0},
        compiler_params=_params("parallel"),
        name="mix_a_sample",
    )(pa, cache_k, cache_v, tt, o_prev)


def _diff_block(qp, k, v, lam, post, g2):
    n = qp.shape[0]
    q = jnp.concatenate([_bf(jnp.where(_lane_range(32 * t, 32 * (t + 1)), qp, 0.0)) for t in range(4)], axis=0)
    (e,) = _exp_weights([_dot_nt(q, k)])
    o = _dot(e, v)
    den = o[:, LANES:LANES + 1]
    outs = []
    for t in range(2):
        p1 = o[2 * t * n:(2 * t + 1) * n, 0:LANES] * (1.0 / den[2 * t * n:(2 * t + 1) * n])
        p2 = o[(2 * t + 1) * n:(2 * t + 2) * n, 0:LANES] * (lam / den[(2 * t + 1) * n:(2 * t + 2) * n])
        d = p1 - p2
        own = _lane_range(64 * t, 64 * (t + 1))
        ms = jnp.sum(jnp.where(own, d * d, 0.0), axis=-1, keepdims=True) * (1.0 / DIFF_V_DIM)
        outs.append(d * lax.rsqrt(ms + EPS))
    return jnp.where(_lane_range(0, 64), outs[0], outs[1]) * g2 * post


def _diff_prompt_kernel(n_prev, *refs):
    pb_ref, cst_ref, g_ref = refs[:3]
    o_ref, kout_ref, vout_ref = refs[3 + n_prev:]
    c = DIFF_QK_DIM ** -0.5 * LOG2E
    lam = cst_ref[0, 0:1, 0:1]
    post = cst_ref[0, 1:2, 0:1]
    for bb in range(PB):
        rows = slice(SEQ * bb, SEQ * (bb + 1))
        for j in range(N_PAIRS):
            cols = slice(LANES * j, LANES * (j + 1))
            k = _bf(pb_ref[rows, WIDTH + LANES * j:WIDTH + LANES * (j + 1)])
            v = jnp.concatenate([_bf(pb_ref[rows, 2 * WIDTH + LANES * j:2 * WIDTH + LANES * (j + 1)]),
                                 _ones_col(SEQ)], axis=1)
            o_ref[rows, cols] = _diff_block(pb_ref[rows, cols] * c, k, v, lam, post, g_ref[0])
        _write_heads(pb_ref, rows, WIDTH, kout_ref, bb)
        _write_heads(pb_ref, rows, 2 * WIDTH, vout_ref, bb)


def _mix_b_prompt(l, pb, cst, g_sub2, prev):
    cache = _new_cache_spec(l, DIFF_HEADS, SEQ, 64)
    n_prev = len(prev)
    return pl.pallas_call(
        functools.partial(_diff_prompt_kernel, n_prev),
        grid=(BATCH // PB,),
        in_specs=[_prompt_rows(SEG_B), _layer_spec(l, 2, LANES), _layer_spec(l, 1, LANES)] + [ANY_SPEC] * n_prev,
        out_specs=[_prompt_rows(WIDTH), cache, cache],
        out_shape=[jax.ShapeDtypeStruct((N_TOK, WIDTH), jnp.float32),
                   _new_cache_shape(DIFF_HEADS, SEQ, 64), _new_cache_shape(DIFF_HEADS, SEQ, 64)],
        input_output_aliases={3 + i: 1 + i for i in range(n_prev)},
        compiler_params=_params("parallel"),
        name="mix_b_prompt",
    )(pb, cst, g_sub2, *prev)


def _diff_sample_kernel(pb_ref, ck_ref, cv_ref, cos_ref, sin_ref, cst_ref, g_ref, prev_ref, o_ref):
    del prev_ref
    c = DIFF_QK_DIM ** -0.5 * LOG2E
    lam = cst_ref[0, 0:1, 0:1]
    post = cst_ref[0, 1:2, 0:1]
    cos = cos_ref[...]
    sin = sin_ref[...]
    q = _rope(pb_ref[:, 0:WIDTH], cos, sin) * c
    k_new = _rope(pb_ref[:, WIDTH:2 * WIDTH], cos, sin)
    for j in range(N_PAIRS):
        cols = slice(LANES * j, LANES * (j + 1))
        k = jnp.concatenate([_bf(_pair_cache(ck_ref, j)), _bf(k_new[:, cols])], axis=0)
        v = jnp.concatenate([_bf(_pair_cache(cv_ref, j)),
                             _bf(pb_ref[:, 2 * WIDTH + LANES * j:2 * WIDTH + LANES * (j + 1)])], axis=0)
        v = jnp.concatenate([v, _ones_col(PAST_LEN + DEC_SEQ)], axis=1)
        for qi in range(DEC_SEQ // QB):
            rows = slice(QB * qi, QB * (qi + 1))
            o_ref[rows, cols] = _diff_block(q[rows, cols], k, v, lam, post, g_ref[0])


def _mix_b_sample(l, pb, cache_k, cache_v, cos, sin, cst, g_sub2, o_prev):
    cache_spec = _old_cache_spec(l, DIFF_HEADS, PAST_LEN, 64)
    tab_spec = pl.BlockSpec((DEC_SEQ, WIDTH), lambda b: (0, 0))
    return pl.pallas_call(
        _diff_sample_kernel,
        grid=(DEC_BATCH,),
        in_specs=[_sample_rows(SEG_B), cache_spec, cache_spec, tab_spec, tab_spec,
                  _layer_spec(l, 2, LANES), _layer_spec(l, 1, LANES), ANY_SPEC],
        out_specs=_sample_rows(WIDTH),
        out_shape=jax.ShapeDtypeStruct((N_TOK, WIDTH), jnp.float32),
        input_output_aliases={7: 0},
        compiler_params=_params("parallel"),
        name="mix_b_sample",
    )(pb, cache_k, cache_v, cos, sin, cst, g_sub2, o_prev)


C_QN, C_QP, C_CKV, C_KPE = 0, 256, 384, 512


def _mla_queries(qn_pair, qp_all, j):
    halves = []
    for t in range(2):
        h = 2 * j + t
        halves.append(jnp.concatenate([
            _bf(jnp.where(_lane_range(64 * t, 64 * (t + 1)), qn_pair, 0.0)),
            _bf(jnp.where(_lane_range(MLA_ROPE * h, MLA_ROPE * (h + 1)), qp_all, 0.0))], axis=1))
    return jnp.concatenate(halves, axis=0)


def _mla_core(qn, qp, kn, kpe4, v, o_ref, row0, qb):
    lk = kn.shape[0]
    for j in range(N_PAIRS):
        cols = slice(LANES * j, LANES * (j + 1))
        k = jnp.concatenate([kn[:, cols], kpe4], axis=1)
        vj = jnp.concatenate([v[:, cols], _ones_col(lk)], axis=1)
        for qi in range(qn.shape[0] // qb):
            rows = slice(qb * qi, qb * (qi + 1))
            (e,) = _exp_weights([_dot_nt(_mla_queries(qn[rows, cols], qp[rows], j), k)])
            o_ref[row0 + qb * qi:row0 + qb * (qi + 1), cols] = _unstack_heads(_normalised(_dot(e, vj)), qb)


def _mla_prompt_kernel(n_prev, *refs):
    pc_ref, gckv_ref, wuk_ref, wuv_ref = refs[:4]
    o_ref, ckv_ref, kpe_ref = refs[4 + n_prev:]
    c = (MLA_NOPE + MLA_ROPE) ** -0.5 * LOG2E
    wuk, wuv = _bf(wuk_ref[0]), _bf(wuv_ref[0])
    for bb in range(PB):
        rows = slice(SEQ * bb, SEQ * (bb + 1))
        ckv = _rms(pc_ref[rows, C_CKV:C_CKV + MLA_KV_RANK], gckv_ref[0])
        ckv_ref[bb] = ckv
        kpe_ref[bb] = pc_ref[rows, C_KPE:C_KPE + MLA_ROPE]
        ckv_b = _bf(ckv)
        _mla_core(pc_ref[rows, C_QN:C_QN + WIDTH] * c, pc_ref[rows, C_QP:C_QP + LANES] * c,
                  _bf(_dot(ckv_b, wuk)), _bf(_tile4(pc_ref[rows, C_KPE:C_KPE + LANES])), _bf(_dot(ckv_b, wuv)),
                  o_ref, SEQ * bb, SEQ)


def _mix_c_prompt(l, pc, g_ckv, w_uk, w_uv, prev):
    n_prev = len(prev)
    w_spec = _layer_spec(l, MLA_KV_RANK, WIDTH)
    return pl.pallas_call(
        functools.partial(_mla_prompt_kernel, n_prev),
        grid=(BATCH // PB,),
        in_specs=[_prompt_rows(SEG_C), _layer_spec(l, 1, MLA_KV_RANK), w_spec, w_spec] + [ANY_SPEC] * n_prev,
        out_specs=[_prompt_rows(WIDTH), _new_cache_spec(l, SEQ, MLA_KV_RANK), _new_cache_spec(l, SEQ, MLA_ROPE)],
        out_shape=[jax.ShapeDtypeStruct((N_TOK, WIDTH), jnp.float32),
                   _new_cache_shape(SEQ, MLA_KV_RANK), _new_cache_shape(SEQ, MLA_ROPE)],
        input_output_aliases={4 + i: 1 + i for i in range(n_prev)},
        compiler_params=_params("parallel"),
        name="mix_c_prompt",
    )(pc, g_ckv, w_uk, w_uv, *prev)


def _mla_sample_kernel(pc_ref, cckv_ref, ckpe_ref, cosq_ref, sinq_ref, cosk_ref, sink_ref,
                       gckv_ref, wuk_ref, wuv_ref, prev_ref, o_ref):
    del prev_ref
    c = (MLA_NOPE + MLA_ROPE) ** -0.5 * LOG2E
    ckv = _rms(pc_ref[:, C_CKV:C_CKV + MLA_KV_RANK], gckv_ref[0])
    ckv_all = jnp.concatenate([_bf(cckv_ref[...]), _bf(ckv)], axis=0)
    kpe_new = _tile4(_rope(pc_ref[:, C_KPE:C_KPE + LANES], cosk_ref[...], sink_ref[...]))
    place = (lax.broadcasted_iota(jnp.int32, (MLA_ROPE, LANES), 1) % MLA_ROPE
             == lax.broadcasted_iota(jnp.int32, (MLA_ROPE, LANES), 0))
    kpe_old = _dot(_bf(ckpe_ref[...]), jnp.where(place, 1.0, 0.0).astype(jnp.bfloat16))
    kpe4 = jnp.concatenate([_bf(kpe_old), _bf(kpe_new)], axis=0)
    qp = _rope(pc_ref[:, C_QP:C_QP + LANES], cosq_ref[...], sinq_ref[...]) * c
    _mla_core(pc_ref[:, C_QN:C_QN + WIDTH] * c, qp, _bf(_dot(ckv_all, _bf(wuk_ref[0]))), kpe4,
              _bf(_dot(ckv_all, _bf(wuv_ref[0]))), o_ref, 0, QB)


def _mix_c_sample(l, pc, cache_ckv, cache_kpe, cosq, sinq, cosk, sink, g_ckv, w_uk, w_uv, o_prev):
    w_spec = _layer_spec(l, MLA_KV_RANK, WIDTH)
    tab_spec = pl.BlockSpec((DEC_SEQ, LANES), lambda b: (0, 0))
    return pl.pallas_call(
        _mla_sample_kernel,
        grid=(DEC_BATCH,),
        in_specs=[_sample_rows(SEG_C), _old_cache_spec(l, PAST_LEN, MLA_KV_RANK), _old_cache_spec(l, PAST_LEN, MLA_ROPE),
                  tab_spec, tab_spec, tab_spec, tab_spec, _layer_spec(l, 1, MLA_KV_RANK), w_spec, w_spec, ANY_SPEC],
        out_specs=_sample_rows(WIDTH),
        out_shape=jax.ShapeDtypeStruct((N_TOK, WIDTH), jnp.float32),
        input_output_aliases={10: 0},
        compiler_params=_params("parallel"),
        name="mix_c_sample",
    )(pc, cache_ckv, cache_kpe, cosq, sinq, cosk, sink, g_ckv, w_uk, w_uv, o_prev)


def _gelu_tanh(x):
    return 0.5 * x * (1.0 + jnp.tanh(math.sqrt(2.0 / math.pi) * (x + 0.044715 * (x * x * x))))


def _sgu_kernel(pd_ref, g_ref, w_ref, bt_ref, o_ref):
    u = _gelu_tanh(pd_ref[:, 0:WIDTH])
    v = _gelu_tanh(pd_ref[:, WIDTH:2 * WIDTH])
    grp = lax.broadcasted_iota(jnp.int32, (1, WIDTH), 1) // SGU_GROUP_DIM
    v2 = v * v
    ms = jnp.zeros_like(v)
    for g in range(SGU_GROUPS):
        sel = grp == g
        tot = jnp.sum(jnp.where(sel, v2, 0.0), axis=-1, keepdims=True)
        ms = jnp.where(sel, tot * (1.0 / SGU_GROUP_DIM), ms)
    vg = _bf(v * lax.rsqrt(ms + EPS) * g_ref[0])
    bt = bt_ref[0]
    for c in range(pd_ref.shape[0] // SGU_CHUNK):
        rows = slice(SGU_CHUNK * c, SGU_CHUNK * (c + 1))
        mixed = jnp.zeros((SGU_CHUNK, WIDTH), jnp.float32)
        for g in range(SGU_GROUPS):
            full = _dot(_bf(w_ref[0, g]), vg[rows]) + bt[:, g:g + 1]
            mixed = jnp.where(grp == g, full, mixed)
        o_ref[rows] = u[rows] * mixed


def _mix_d(l, pd, sgu_g, sgu_w, sgu_bt):
    return pl.pallas_call(
        _sgu_kernel,
        grid=(N_TOK // TM,),
        in_specs=[
            pl.BlockSpec((TM, SEG_D), lambda i: (i, 0)),
            pl.BlockSpec((1, 1, WIDTH), lambda i: (l, 0, 0)),
            pl.BlockSpec((1, SGU_GROUPS, SGU_CHUNK, SGU_CHUNK), lambda i: (l, 0, 0, 0)),
            pl.BlockSpec((1, SGU_CHUNK, SGU_GROUPS), lambda i: (l, 0, 0)),
        ],
        out_specs=pl.BlockSpec((TM, WIDTH), lambda i: (i, 0)),
        out_shape=jax.ShapeDtypeStruct((N_TOK, WIDTH), jnp.float32),
        compiler_params=_params("parallel"),
        name="mix_d",
    )(pd, sgu_g, sgu_w, sgu_bt)


FF_CHUNK = 1024


def _outffn_kernel(n_x, first, final, *refs):
    x_refs = refs[:n_x]
    (oa_ref, ob_ref, oc_ref, od_ref, wout_ref, g1_ref, gffn_ref,
     sh2_ref, sc2_ref, g2_ref, w1_ref, w2_ref, gfin_ref, y_ref) = refs[n_x:]
    acc = jnp.zeros((TM, D_MODEL), jnp.float32)
    for i, ref in enumerate((oa_ref, ob_ref, oc_ref, od_ref)):
        acc += _dot(_bf(ref[...]), wout_ref[0, WIDTH * i:WIDTH * (i + 1), :])
    x1 = _read_x(x_refs, first) + g1_ref[...] * acc
    hf = _bf(_rms(x1, gffn_ref[0]) * (1.0 + sc2_ref[...]) + sh2_ref[...])
    acc = jnp.zeros((TM, D_MODEL), jnp.float32)
    for c in range(D_FF // FF_CHUNK):
        cols = slice(FF_CHUNK * c, FF_CHUNK * (c + 1))
        a = jnp.square(jnp.maximum(_dot(hf, w1_ref[0, :, cols]), 0.0))
        acc += _dot(_bf(a), w2_ref[0, cols, :])
    y = x1 + g2_ref[...] * acc
    if final:
        y = _rms(y, gfin_ref[...])
    y_ref[...] = y


def _outffn(l, xs, oa, ob, oc, od, w_out, g_ffn, mod, w1, w2, g_final, first, n_tiles):
    def mod_spec(j):
        return pl.BlockSpec((None, None, 1, D_MODEL), lambda i: (l, _row_group(first + i), 0, j))

    def resident(shape):
        return pl.BlockSpec(shape, lambda i: (l,) + (0,) * (len(shape) - 1), pipeline_mode=pl.Buffered(1))

    o_spec = pl.BlockSpec((TM, WIDTH), lambda i: (first + i, 0))
    return pl.pallas_call(
        functools.partial(_outffn_kernel, len(xs), first, l == DEPTH - 1),
        grid=(n_tiles,),
        in_specs=_x_specs(len(xs) == 2, first) + [
            o_spec, o_spec, o_spec, o_spec,
            resident((1, 4 * WIDTH, D_MODEL)),
            mod_spec(2),
            pl.BlockSpec((1, 1, D_MODEL), lambda i: (l, 0, 0)),
            mod_spec(3), mod_spec(4), mod_spec(5),
            resident((1, D_MODEL, D_FF)),
            resident((1, D_FF, D_MODEL)),
            pl.BlockSpec((1, D_MODEL), lambda i: (0, 0)),
        ],
        out_specs=pl.BlockSpec((TM, D_MODEL), lambda i: (i, 0)),
        out_shape=jax.ShapeDtypeStruct((n_tiles * TM, D_MODEL), jnp.float32),
        compiler_params=_params("parallel"),
        name="outffn",
    )(*xs, oa, ob, oc, od, w_out, mod, g_ffn, mod, mod, mod, w1, w2, g_final)


def _rope32_tables():
    t = np.arange(DEC_SEQ)
    rows, cols = (t // GRID_W).astype(np.float64), (t % GRID_W).astype(np.float64)
    half = 8
    freqs = ROPE_BASE ** (-np.arange(half, dtype=np.float64) / half)
    cos, sin = [], []
    for pos in (rows, cols):
        ang = pos[:, None] * freqs[None, :]
        cos += [np.cos(ang), np.cos(ang)]
        sin += [-np.sin(ang), np.sin(ang)]
    return np.concatenate(cos, axis=1).astype(np.float32), np.concatenate(sin, axis=1).astype(np.float32)


def _rope_tables():
    c32, s32 = _rope32_tables()
    tile = lambda a, n: np.tile(a, (1, n))
    pad = np.zeros((DEC_SEQ, 96), np.float32)
    cos_k = np.concatenate([c32, pad + 1.0], axis=1)
    sin_k = np.concatenate([s32, pad], axis=1)
    return (tile(c32, 8), tile(s32, 8),
            tile(c32, 4), tile(s32, 4),
            cos_k, sin_k)


def _permute_w_in(w_in):
    offs = np.cumsum((0, 256, 256, 256, 256, 256, 256, 384, 128, 32, 256, 256))
    qc = np.arange(offs[6], offs[7]).reshape(MLA_HEADS, MLA_NOPE + MLA_ROPE)
    idx_abc = np.concatenate([np.arange(0, offs[6]), qc[:, :MLA_NOPE].reshape(-1), qc[:, MLA_NOPE:].reshape(-1),
                              np.arange(offs[7], offs[9])])
    pad = jnp.zeros((DEPTH, D_MODEL, SEG_C_PAD), w_in.dtype)
    return _bf(jnp.concatenate([w_in[:, :, idx_abc], pad, w_in[:, :, offs[9]:]], axis=-1))


def kernel(x_prompt, x_sample, cache_na_k, cache_na_v, cache_diff_k, cache_diff_v, cache_mla_ckv, cache_mla_kpe, c, c_ctx, w_ada, b_ada, g_mix, g_ffn, w_in, w_out, na_rpb, diff_lq1, diff_lk1, diff_lq2, diff_lk2, diff_g_subln, mla_g_ckv, mla_w_uk, mla_w_uv, sgu_g, sgu_w, sgu_b, w_ff1, w_ff2, g_final):
    f32 = jnp.float32
    m = jnp.concatenate([c_ctx[None, :], c, jnp.zeros((N_MOD_ROWS - 1 - DEC_BATCH, D_MODEL), f32)], axis=0)
    mod = _ada(m, w_ada, b_ada).reshape(DEPTH, N_MOD_ROWS, 1, 6 * D_MODEL)
    cst = _lam_consts(diff_lq1, diff_lk1, diff_lq2, diff_lk2)
    tt = _bias_tiles(na_rpb)
    cos_b, sin_b, cos_q, sin_q, cos_k, sin_k = [jnp.asarray(t) for t in _rope_tables()]

    w_in_p = _permute_w_in(w_in)
    w_out_b, w1_b, w2_b = _bf(w_out), _bf(w_ff1), _bf(w_ff2)
    g_mix3 = g_mix.reshape(DEPTH, 1, D_MODEL)
    g_ffn3 = g_ffn.reshape(DEPTH, 1, D_MODEL)
    g_sub2 = jnp.tile(diff_g_subln, (1, 2)).reshape(DEPTH, 1, LANES)
    g_ckv3 = mla_g_ckv.reshape(DEPTH, 1, MLA_KV_RANK)
    sgu_g3 = sgu_g.reshape(DEPTH, 1, WIDTH)
    sgu_bt = sgu_b.transpose(0, 2, 1)
    g_fin2 = g_final.reshape(1, D_MODEL)

    xs = (x_prompt.reshape(N_PROMPT, D_MODEL), x_sample.reshape(N_SAMPLE, D_MODEL))
    new_a, new_b, new_c = (), (), ()
    for l in range(DEPTH):
        pa, pb, pc, pd = _inproj(l, xs, g_mix3, mod, w_in_p)
        oa, *new_a = _mix_a_prompt(l, pa, new_a)
        oa = _mix_a_sample(l, pa, cache_na_k, cache_na_v, tt, oa)
        ob, *new_b = _mix_b_prompt(l, pb, cst, g_sub2, new_b)
        ob = _mix_b_sample(l, pb, cache_diff_k, cache_diff_v, cos_b, sin_b, cst, g_sub2, ob)
        oc, *new_c = _mix_c_prompt(l, pc, g_ckv3, mla_w_uk, mla_w_uv, new_c)
        oc = _mix_c_sample(l, pc, cache_mla_ckv, cache_mla_kpe, cos_q, sin_q, cos_k, sin_k, g_ckv3,
                           mla_w_uk, mla_w_uv, oc)
        od = _mix_d(l, pd, sgu_g3, sgu_w, sgu_bt)
        ffn = functools.partial(_outffn, l, xs, oa, ob, oc, od, w_out_b, g_ffn3, mod, w1_b, w2_b, g_fin2)
        if l < DEPTH - 1:
            xs = (ffn(0, TILES_PROMPT + TILES_SAMPLE),)
        else:
            xs = (ffn(0, TILES_PROMPT), ffn(TILES_PROMPT, TILES_SAMPLE))
    y_prompt = xs[0].reshape(BATCH, SEQ, D_MODEL)
    y_sample = xs[1].reshape(DEC_BATCH, DEC_SEQ, D_MODEL)
    return (y_prompt, y_sample, *new_a, *new_b, *new_c)
```

```python
import functools
import math

import numpy as np
import jax
import jax.numpy as jnp
from jax import lax
from jax.experimental import pallas as pl
from jax.experimental.pallas import tpu as pltpu

D_MODEL = 1024
BATCH = 16
SEQ = 256
DEPTH = 4
DEC_BATCH = 2
DEC_SEQ = 1024
PAST_LEN = 512
GRID_W = 64
GRID_ROWS = DEC_SEQ // GRID_W
HEAD_DIM = 64
NA_HEADS = 4
NA_WIN_ROWS = 8
NA_WIN_COLS = 16
DIFF_HEADS = 4
DIFF_QK_DIM = 32
DIFF_V_DIM = 64
MLA_HEADS = 4
MLA_NOPE = 64
MLA_ROPE = 32
MLA_V = 64
MLA_KV_RANK = 128
SGU_GROUPS = 4
SGU_GROUP_DIM = 64
SGU_CHUNK = 128
D_FF = 4 * D_MODEL
ROPE_BASE = 10000.0
EPS = 1e-6
NEG_INF = -1e30
LOG2E = 1.4426950408889634

N_HEADS = 4
N_PAIRS = N_HEADS // 2
LANES = 128
WIDTH = 256
N_PROMPT = BATCH * SEQ
N_SAMPLE = DEC_BATCH * DEC_SEQ
N_TOK = N_PROMPT + N_SAMPLE
N_MOD_ROWS = 8

SEG_A = 3 * WIDTH
SEG_B = 3 * WIDTH
SEG_C = 640
SEG_D = 2 * WIDTH
SEG_C_PAD = 96
IN_COLS_P = SEG_A + SEG_B + SEG_C + SEG_D

TM = 512
TILES_PROMPT = N_PROMPT // TM
TILES_SAMPLE = N_SAMPLE // TM
PB = 2
QB = 256
VMEM_LIMIT = 56 * 1024 * 1024


def _bf(x):
    return x.astype(jnp.bfloat16)


def _dot(a, b):
    return jnp.dot(a, b, preferred_element_type=jnp.float32)


def _dot_nt(a, b):
    return lax.dot_general(a, b, (((1,), (1,)), ((), ())), preferred_element_type=jnp.float32)


def _rms(x, g):
    ms = jnp.mean(x * x, axis=-1, keepdims=True)
    return x * lax.rsqrt(ms + EPS) * g


def _lane_range(lo, hi, width=LANES):
    lane = lax.broadcasted_iota(jnp.int32, (1, width), 1)
    return (lane >= lo) & (lane < hi)


def _ones_col(rows):
    lane = lax.broadcasted_iota(jnp.int32, (rows, LANES), 1)
    return jnp.where(lane == 0, 1.0, 0.0).astype(jnp.bfloat16)


def _exp_weights(parts):
    m = functools.reduce(jnp.maximum, [jnp.max(s, axis=-1, keepdims=True) for s in parts])
    return [_bf(jnp.exp2(s - m)) for s in parts]


def _normalised(o_ext):
    return o_ext[:, 0:LANES] * (1.0 / o_ext[:, LANES:LANES + 1])


def _swap8(x):
    lane = lax.broadcasted_iota(jnp.int32, (1, LANES), 1)
    return jnp.where((lane & 15) < 8, pltpu.roll(x, LANES - 8, 1), pltpu.roll(x, 8, 1))


def _rope(x, cos, sin):
    outs = []
    for c in range(x.shape[1] // LANES):
        sl = slice(LANES * c, LANES * (c + 1))
        xc = x[:, sl]
        outs.append(xc * cos[:, sl] + _swap8(xc) * sin[:, sl])
    return outs[0] if len(outs) == 1 else jnp.concatenate(outs, axis=1)


def _tile4(x):
    return x + pltpu.roll(x, 32, 1) + pltpu.roll(x, 64, 1) + pltpu.roll(x, 96, 1)


def _params(*sem):
    return pltpu.CompilerParams(dimension_semantics=sem, vmem_limit_bytes=VMEM_LIMIT)


ADA_TN = 1536


def _ada_kernel(m_ref, w_ref, b_ref, o_ref):
    m = m_ref[...]
    s = m * jax.nn.sigmoid(m)
    o_ref[0] = _dot(_bf(s), _bf(w_ref[0])) + b_ref[0]


def _ada(m, w_ada, b_ada):
    n = 6 * D_MODEL
    return pl.pallas_call(
        _ada_kernel,
        grid=(DEPTH, n // ADA_TN),
        in_specs=[
            pl.BlockSpec((N_MOD_ROWS, D_MODEL), lambda l, j: (0, 0)),
            pl.BlockSpec((1, D_MODEL, ADA_TN), lambda l, j: (l, 0, j)),
            pl.BlockSpec((1, 1, ADA_TN), lambda l, j: (l, 0, j)),
        ],
        out_specs=pl.BlockSpec((1, N_MOD_ROWS, ADA_TN), lambda l, j: (l, 0, j)),
        out_shape=jax.ShapeDtypeStruct((DEPTH, N_MOD_ROWS, n), jnp.float32),
        compiler_params=_params("parallel", "parallel"),
        name="ada",
    )(m, w_ada, b_ada.reshape(DEPTH, 1, n))


def _lam_kernel(lq1_ref, lk1_ref, lq2_ref, lk2_ref, init_ref, o_ref):
    init = init_ref[...]
    a = jnp.exp(jnp.sum(lq1_ref[...] * lk1_ref[...], axis=-1, keepdims=True))
    b = jnp.exp(jnp.sum(lq2_ref[...] * lk2_ref[...], axis=-1, keepdims=True))
    lam = a - b + init
    post = 1.0 - init
    for l in range(DEPTH):
        o_ref[l, 0:1, :] = jnp.broadcast_to(lam[l:l + 1], (1, LANES))
        o_ref[l, 1:2, :] = jnp.broadcast_to(post[l:l + 1], (1, LANES))


def _lam_consts(lq1, lk1, lq2, lk2):
    init = np.array([[0.8 - 0.6 * math.exp(-0.3 * l)] for l in range(DEPTH)], np.float32)
    return pl.pallas_call(
        _lam_kernel,
        out_shape=jax.ShapeDtypeStruct((DEPTH, 2, LANES), jnp.float32),
        name="diff_lambda",
    )(lq1, lk1, lq2, lk2, jnp.asarray(init))


N_DROW = 2 * NA_WIN_ROWS - 1
N_DCOL = 2 * NA_WIN_COLS - 1


def _bias_kernel(rpb_ref, o_ref):
    l = pl.program_id(0)
    h = pl.program_id(1)
    base = (l * NA_HEADS + h) * (N_DROW * N_DCOL)
    cq = lax.broadcasted_iota(jnp.int32, (GRID_W, LANES), 0)
    lane = lax.broadcasted_iota(jnp.int32, (GRID_W, LANES), 1)
    ck = lane & (GRID_W - 1)
    dcol = jnp.clip(ck - cq, -(NA_WIN_COLS - 1), NA_WIN_COLS - 1) + (NA_WIN_COLS - 1)
    hi = lane >= GRID_W
    for a in range(N_DROW - 1):
        acc = jnp.zeros((GRID_W, LANES), jnp.float32)
        for j in range(N_DCOL):
            lo_v = rpb_ref[base + a * N_DCOL + j]
            hi_v = rpb_ref[base + (a + 1) * N_DCOL + j]
            acc = jnp.where(dcol == j, jnp.where(hi, hi_v, lo_v), acc)
        o_ref[0, 0, a] = acc * LOG2E


def _bias_tiles(na_rpb):
    return pl.pallas_call(
        _bias_kernel,
        grid=(DEPTH, NA_HEADS),
        in_specs=[pl.BlockSpec(memory_space=pltpu.SMEM)],
        out_specs=pl.BlockSpec((1, 1, N_DROW - 1, GRID_W, LANES), lambda l, h: (l, h, 0, 0, 0)),
        out_shape=jax.ShapeDtypeStruct((DEPTH, NA_HEADS, N_DROW - 1, GRID_W, LANES), jnp.float32),
        compiler_params=_params("parallel", "parallel"),
        name="na_bias_tiles",
    )(na_rpb.reshape(-1))


def _row_group(i):
    return jnp.where(i < TILES_PROMPT, 0, 1 + (i - TILES_PROMPT) // (DEC_SEQ // TM))


def _x_specs(split, first):
    if not split:
        return [pl.BlockSpec((TM, D_MODEL), lambda i: (first + i, 0))]
    return [pl.BlockSpec((TM, D_MODEL), lambda i: (jnp.minimum(first + i, TILES_PROMPT - 1), 0)),
            pl.BlockSpec((TM, D_MODEL), lambda i: (jnp.maximum(first + i - TILES_PROMPT, 0), 0))]


def _read_x(x_refs, first):
    if len(x_refs) == 1:
        return x_refs[0][...]
    return jnp.where(first + pl.program_id(0) < TILES_PROMPT, x_refs[0][...], x_refs[1][...])


IN_COLS = 2592
IN_QC, IN_CKV, IN_D = 1536, 1920, 2080
TR_ROWS = 256


def _gelu_tanh(x):
    return 0.5 * x * (1.0 + jnp.tanh(math.sqrt(2.0 / math.pi) * (x + 0.044715 * (x * x * x))))


def _sgu(pd, g, w_ref, bt):
    u = _gelu_tanh(pd[:, 0:WIDTH])
    v = _gelu_tanh(pd[:, WIDTH:2 * WIDTH])
    grp = lax.broadcasted_iota(jnp.int32, (1, WIDTH), 1) // SGU_GROUP_DIM
    v2 = v * v
    ms = jnp.zeros_like(v)
    for gi in range(SGU_GROUPS):
        sel = grp == gi
        tot = jnp.sum(jnp.where(sel, v2, 0.0), axis=-1, keepdims=True)
        ms = jnp.where(sel, tot * (1.0 / SGU_GROUP_DIM), ms)
    vg = _bf(v * lax.rsqrt(ms + EPS) * g)
    outs = []
    for c in range(pd.shape[0] // SGU_CHUNK):
        rows = slice(SGU_CHUNK * c, SGU_CHUNK * (c + 1))
        mixed = jnp.zeros((SGU_CHUNK, WIDTH), jnp.float32)
        for gi in range(SGU_GROUPS):
            full = _dot(_bf(w_ref[0, gi]), vg[rows]) + bt[:, gi:gi + 1]
            mixed = jnp.where(grp == gi, full, mixed)
        outs.append(u[rows] * mixed)
    return jnp.concatenate(outs, axis=0)


def _w_in_row_pieces():
    qn = [(IN_QC + 96 * h, MLA_NOPE) for h in range(MLA_HEADS)]
    qp = [(IN_QC + 96 * h + MLA_NOPE, MLA_ROPE) for h in range(MLA_HEADS)]
    seg_c = qn + qp + [(IN_CKV, MLA_KV_RANK + MLA_ROPE)]
    return (0, SEG_A + SEG_B), seg_c, (IN_D, SEG_D)


def _load_w_in(wt_ref, w_scr):
    ab, seg_c, d = _w_in_row_pieces()
    c_rows = jnp.concatenate([wt_ref[0, s:s + n, :] for s, n in seg_c]
                             + [jnp.zeros((SEG_C_PAD, D_MODEL), jnp.float32)], axis=0)
    for t in range(SEG_C // LANES):
        w_scr[:, SEG_A + SEG_B + LANES * t:SEG_A + SEG_B + LANES * (t + 1)] = _bf(c_rows[LANES * t:LANES * (t + 1)].T)
    for (src, n), dst in ((ab, 0), (d, SEG_A + SEG_B + SEG_C)):
        for t in range(n // TR_ROWS):
            rows = wt_ref[0, src + TR_ROWS * t:src + TR_ROWS * (t + 1), :]
            w_scr[:, dst + TR_ROWS * t:dst + TR_ROWS * (t + 1)] = _bf(rows.T)


def _inproj_kernel(n_x, *refs):
    x_refs = refs[:n_x]
    (g_ref, sh_ref, sc_ref, wt_ref, sg_ref, sw_ref, sbt_ref, pa_ref, pb_ref, pc_ref, od_ref, w_scr) = refs[n_x:]

    @pl.when(pl.program_id(0) == 0)
    def _():
        _load_w_in(wt_ref, w_scr)

    h = _rms(_read_x(x_refs, 0), g_ref[0]) * (1.0 + sc_ref[...]) + sh_ref[...]
    hb = _bf(h)
    off = 0
    for ref in (pa_ref, pb_ref, pc_ref):
        n = ref.shape[1]
        ref[...] = _dot(hb, w_scr[:, off:off + n])
        off += n
    od_ref[...] = _sgu(_dot(hb, w_scr[:, off:off + SEG_D]), sg_ref[0], sw_ref, sbt_ref[0])


def _inproj(l, xs, g_mix, mod, w_in_t, sgu_g, sgu_w, sgu_bt):
    def mod_spec(j):
        return pl.BlockSpec((None, None, 1, D_MODEL), lambda i: (l, _row_group(i), 0, j))

    widths = (SEG_A, SEG_B, SEG_C, WIDTH)
    return pl.pallas_call(
        functools.partial(_inproj_kernel, len(xs)),
        grid=(N_TOK // TM,),
        in_specs=_x_specs(len(xs) == 2, 0) + [
            pl.BlockSpec((1, 1, D_MODEL), lambda i: (l, 0, 0)),
            mod_spec(0), mod_spec(1),
            pl.BlockSpec((1, IN_COLS, D_MODEL), lambda i: (l, 0, 0), pipeline_mode=pl.Buffered(1)),
            pl.BlockSpec((1, 1, WIDTH), lambda i: (l, 0, 0)),
            pl.BlockSpec((1, SGU_GROUPS, SGU_CHUNK, SGU_CHUNK), lambda i: (l, 0, 0, 0)),
            pl.BlockSpec((1, SGU_CHUNK, SGU_GROUPS), lambda i: (l, 0, 0)),
        ],
        out_specs=[pl.BlockSpec((TM, n), lambda i: (i, 0)) for n in widths],
        out_shape=[jax.ShapeDtypeStruct((N_TOK, n), jnp.float32) for n in widths],
        scratch_shapes=[pltpu.VMEM((D_MODEL, IN_COLS_P), jnp.bfloat16)],
        compiler_params=_params("arbitrary"),
        name="inproj",
    )(*xs, g_mix, mod, mod, w_in_t, sgu_g, sgu_w, sgu_bt)


FIRST_SAMPLE_BLOCK = N_PROMPT // DEC_SEQ


def _prompt_rows(width):
    return pl.BlockSpec((PB * SEQ, width), lambda b: (b, 0))


def _sample_rows(width):
    return pl.BlockSpec((DEC_SEQ, width), lambda b: (FIRST_SAMPLE_BLOCK + b, 0))


def _new_cache_spec(l, *tail):
    return pl.BlockSpec((PB, None) + tail, lambda b: (b, l) + (0,) * len(tail))


def _new_cache_shape(*tail):
    return jax.ShapeDtypeStruct((BATCH, DEPTH) + tail, jnp.float32)


def _old_cache_spec(l, *tail):
    return pl.BlockSpec((None, None) + tail, lambda b: (b, l) + (0,) * len(tail))


def _layer_spec(l, *tail):
    return pl.BlockSpec((1,) + tail, lambda b: (l,) + (0,) * len(tail))


ANY_SPEC = pl.BlockSpec(memory_space=pl.ANY)


def _write_heads(p_ref, rows, col0, out_ref, bb):
    for h in range(N_HEADS):
        out_ref[bb, h] = p_ref[rows, col0 + 64 * h:col0 + 64 * (h + 1)]


def _pair_cache(c_ref, j):
    return jnp.concatenate([c_ref[2 * j], c_ref[2 * j + 1]], axis=1)


def _stack_heads(qp):
    lo = _lane_range(0, 64)
    return jnp.concatenate([_bf(jnp.where(lo, qp, 0.0)), _bf(jnp.where(lo, 0.0, qp))], axis=0)


def _unstack_heads(o, n):
    return jnp.where(_lane_range(0, 64), o[0:n], o[n:2 * n])


def _dense_a_kernel(n_prev, *refs):
    pa_ref, (o_ref, kout_ref, vout_ref) = refs[0], refs[1 + n_prev:]
    c = HEAD_DIM ** -0.5 * LOG2E
    for bb in range(PB):
        rows = slice(SEQ * bb, SEQ * (bb + 1))
        for j in range(N_PAIRS):
            cols = slice(LANES * j, LANES * (j + 1))
            q = _stack_heads(pa_ref[rows, cols] * c)
            k = _bf(pa_ref[rows, WIDTH + LANES * j:WIDTH + LANES * (j + 1)])
            v = jnp.concatenate([_bf(pa_ref[rows, 2 * WIDTH + LANES * j:2 * WIDTH + LANES * (j + 1)]),
                                 _ones_col(SEQ)], axis=1)
            (e,) = _exp_weights([_dot_nt(q, k)])
            o_ref[rows, cols] = _unstack_heads(_normalised(_dot(e, v)), SEQ)
        _write_heads(pa_ref, rows, WIDTH, kout_ref, bb)
        _write_heads(pa_ref, rows, 2 * WIDTH, vout_ref, bb)


def _mix_a_prompt(l, pa, prev):
    cache = _new_cache_spec(l, NA_HEADS, SEQ, HEAD_DIM)
    n_prev = len(prev)
    return pl.pallas_call(
        functools.partial(_dense_a_kernel, n_prev),
        grid=(BATCH // PB,),
        in_specs=[_prompt_rows(SEG_A)] + [ANY_SPEC] * n_prev,
        out_specs=[_prompt_rows(WIDTH), cache, cache],
        out_shape=[jax.ShapeDtypeStruct((N_TOK, WIDTH), jnp.float32),
                   _new_cache_shape(NA_HEADS, SEQ, HEAD_DIM), _new_cache_shape(NA_HEADS, SEQ, HEAD_DIM)],
        input_output_aliases={1 + i: 1 + i for i in range(n_prev)},
        compiler_params=_params("parallel"),
        name="mix_a_prompt",
    )(pa, *prev)


def _na_row_groups():
    kh = min(NA_WIN_ROWS, GRID_ROWS)
    r0s = [min(max(r - kh // 2, 0), GRID_ROWS - kh) for r in range(GRID_ROWS)]
    groups = []
    for r, r0 in enumerate(r0s):
        if groups and groups[-1][2] == r0:
            groups[-1][1] = r
        else:
            groups.append([r, r, r0])
    return kh, [tuple(g) for g in groups]


def _na_kernel(pa_ref, ck_ref, cv_ref, tt_ref, prev_ref, o_ref):
    del prev_ref
    c = HEAD_DIM ** -0.5 * LOG2E
    kh, groups = _na_row_groups()
    lk = kh * GRID_W

    def in_window(n):
        cq = lax.broadcasted_iota(jnp.int32, (n, lk), 0) & (GRID_W - 1)
        ck = lax.broadcasted_iota(jnp.int32, (n, lk), 1) & (GRID_W - 1)
        c0 = jnp.clip(cq - NA_WIN_COLS // 2, 0, GRID_W - NA_WIN_COLS)
        return (ck >= c0) & (ck < c0 + NA_WIN_COLS)

    for j in range(N_PAIRS):
        cols = slice(LANES * j, LANES * (j + 1))
        q = _stack_heads(pa_ref[:, cols] * c)
        k = _bf(pa_ref[:, WIDTH + LANES * j:WIDTH + LANES * (j + 1)])
        v = jnp.concatenate([_bf(pa_ref[:, 2 * WIDTH + LANES * j:2 * WIDTH + LANES * (j + 1)]),
                             _ones_col(DEC_SEQ)], axis=1)
        kc = _bf(_pair_cache(ck_ref, j))
        vc = jnp.concatenate([_bf(_pair_cache(cv_ref, j)), _ones_col(PAST_LEN)], axis=1)
        for (r_lo, r_hi, r0) in groups:
            n = (r_hi - r_lo + 1) * GRID_W
            rows = slice(r_lo * GRID_W, r_lo * GRID_W + n)
            keys = slice(r0 * GRID_W, r0 * GRID_W + lk)
            qg = jnp.concatenate([q[rows], q[DEC_SEQ + r_lo * GRID_W:DEC_SEQ + r_lo * GRID_W + n]], axis=0)
            bias = jnp.concatenate([
                jnp.concatenate([tt_ref[2 * j + t, r0 + 2 * i - r + NA_WIN_ROWS - 1] for i in range(kh // 2)], axis=1)
                for t in range(2) for r in range(r_lo, r_hi + 1)], axis=0)
            s_loc = jnp.where(in_window(2 * n), _dot_nt(qg, k[keys]) + bias, NEG_INF)
            e_ctx, e_loc = _exp_weights([_dot_nt(qg, kc), s_loc])
            o = _normalised(_dot(e_ctx, vc) + _dot(e_loc, v[keys]))
            o_ref[rows, cols] = _unstack_heads(o, n)


def _mix_a_sample(l, pa, cache_k, cache_v, tt, o_prev):
    cache_spec = _old_cache_spec(l, NA_HEADS, PAST_LEN, HEAD_DIM)
    return pl.pallas_call(
        _na_kernel,
        grid=(DEC_BATCH,),
        in_specs=[_sample_rows(SEG_A), cache_spec, cache_spec,
                  pl.BlockSpec((None, NA_HEADS, N_DROW - 1, GRID_W, LANES), lambda b: (l, 0, 0, 0, 0)),
                  ANY_SPEC],
        out_specs=_sample_rows(WIDTH),
        out_shape=jax.ShapeDtypeStruct((N_TOK, WIDTH), jnp.float32),
        input_output_aliases={4: 0},
        compiler_params=_params("parallel"),
        name="mix_a_sample",
    )(pa, cache_k, cache_v, tt, o_prev)


def _diff_block(qp, k, v, lam, post, g2):
    n = qp.shape[0]
    q = jnp.concatenate([_bf(jnp.where(_lane_range(32 * t, 32 * (t + 1)), qp, 0.0)) for t in range(4)], axis=0)
    (e,) = _exp_weights([_dot_nt(q, k)])
    o = _dot(e, v)
    den = o[:, LANES:LANES + 1]
    outs = []
    for t in range(2):
        p1 = o[2 * t * n:(2 * t + 1) * n, 0:LANES] * (1.0 / den[2 * t * n:(2 * t + 1) * n])
        p2 = o[(2 * t + 1) * n:(2 * t + 2) * n, 0:LANES] * (lam / den[(2 * t + 1) * n:(2 * t + 2) * n])
        d = p1 - p2
        own = _lane_range(64 * t, 64 * (t + 1))
        ms = jnp.sum(jnp.where(own, d * d, 0.0), axis=-1, keepdims=True) * (1.0 / DIFF_V_DIM)
        outs.append(d * lax.rsqrt(ms + EPS))
    return jnp.where(_lane_range(0, 64), outs[0], outs[1]) * g2 * post


def _diff_prompt_kernel(n_prev, *refs):
    pb_ref, cst_ref, g_ref = refs[:3]
    o_ref, kout_ref, vout_ref = refs[3 + n_prev:]
    c = DIFF_QK_DIM ** -0.5 * LOG2E
    lam = cst_ref[0, 0:1, 0:1]
    post = cst_ref[0, 1:2, 0:1]
    for bb in range(PB):
        rows = slice(SEQ * bb, SEQ * (bb + 1))
        for j in range(N_PAIRS):
            cols = slice(LANES * j, LANES * (j + 1))
            k = _bf(pb_ref[rows, WIDTH + LANES * j:WIDTH + LANES * (j + 1)])
            v = jnp.concatenate([_bf(pb_ref[rows, 2 * WIDTH + LANES * j:2 * WIDTH + LANES * (j + 1)]),
                                 _ones_col(SEQ)], axis=1)
            o_ref[rows, cols] = _diff_block(pb_ref[rows, cols] * c, k, v, lam, post, g_ref[0])
        _write_heads(pb_ref, rows, WIDTH, kout_ref, bb)
        _write_heads(pb_ref, rows, 2 * WIDTH, vout_ref, bb)


def _mix_b_prompt(l, pb, cst, g_sub2, prev):
    cache = _new_cache_spec(l, DIFF_HEADS, SEQ, 64)
    n_prev = len(prev)
    return pl.pallas_call(
        functools.partial(_diff_prompt_kernel, n_prev),
        grid=(BATCH // PB,),
        in_specs=[_prompt_rows(SEG_B), _layer_spec(l, 2, LANES), _layer_spec(l, 1, LANES)] + [ANY_SPEC] * n_prev,
        out_specs=[_prompt_rows(WIDTH), cache, cache],
        out_shape=[jax.ShapeDtypeStruct((N_TOK, WIDTH), jnp.float32),
                   _new_cache_shape(DIFF_HEADS, SEQ, 64), _new_cache_shape(DIFF_HEADS, SEQ, 64)],
        input_output_aliases={3 + i: 1 + i for i in range(n_prev)},
        compiler_params=_params("parallel"),
        name="mix_b_prompt",
    )(pb, cst, g_sub2, *prev)


def _diff_sample_kernel(pb_ref, ck_ref, cv_ref, cos_ref, sin_ref, cst_ref, g_ref, prev_ref, o_ref):
    del prev_ref
    c = DIFF_QK_DIM ** -0.5 * LOG2E
    lam = cst_ref[0, 0:1, 0:1]
    post = cst_ref[0, 1:2, 0:1]
    cos = cos_ref[...]
    sin = sin_ref[...]
    q = _rope(pb_ref[:, 0:WIDTH], cos, sin) * c
    k_new = _rope(pb_ref[:, WIDTH:2 * WIDTH], cos, sin)
    for j in range(N_PAIRS):
        cols = slice(LANES * j, LANES * (j + 1))
        k = jnp.concatenate([_bf(_pair_cache(ck_ref, j)), _bf(k_new[:, cols])], axis=0)
        v = jnp.concatenate([_bf(_pair_cache(cv_ref, j)),
                             _bf(pb_ref[:, 2 * WIDTH + LANES * j:2 * WIDTH + LANES * (j + 1)])], axis=0)
        v = jnp.concatenate([v, _ones_col(PAST_LEN + DEC_SEQ)], axis=1)
        for qi in range(DEC_SEQ // QB):
            rows = slice(QB * qi, QB * (qi + 1))
            o_ref[rows, cols] = _diff_block(q[rows, cols], k, v, lam, post, g_ref[0])


def _mix_b_sample(l, pb, cache_k, cache_v, cos, sin, cst, g_sub2, o_prev):
    cache_spec = _old_cache_spec(l, DIFF_HEADS, PAST_LEN, 64)
    tab_spec = pl.BlockSpec((DEC_SEQ, WIDTH), lambda b: (0, 0))
    return pl.pallas_call(
        _diff_sample_kernel,
        grid=(DEC_BATCH,),
        in_specs=[_sample_rows(SEG_B), cache_spec, cache_spec, tab_spec, tab_spec,
                  _layer_spec(l, 2, LANES), _layer_spec(l, 1, LANES), ANY_SPEC],
        out_specs=_sample_rows(WIDTH),
        out_shape=jax.ShapeDtypeStruct((N_TOK, WIDTH), jnp.float32),
        input_output_aliases={7: 0},
        compiler_params=_params("parallel"),
        name="mix_b_sample",
    )(pb, cache_k, cache_v, cos, sin, cst, g_sub2, o_prev)


C_QN, C_QP, C_CKV, C_KPE = 0, 256, 384, 512


def _mla_queries(qn_pair, qp_all, j):
    halves = []
    for t in range(2):
        h = 2 * j + t
        halves.append(jnp.concatenate([
            _bf(jnp.where(_lane_range(64 * t, 64 * (t + 1)), qn_pair, 0.0)),
            _bf(jnp.where(_lane_range(MLA_ROPE * h, MLA_ROPE * (h + 1)), qp_all, 0.0))], axis=1))
    return jnp.concatenate(halves, axis=0)


def _mla_core(qn, qp, kn, kpe4, v, o_ref, row0, qb):
    lk = kn.shape[0]
    for j in range(N_PAIRS):
        cols = slice(LANES * j, LANES * (j + 1))
        k = jnp.concatenate([kn[:, cols], kpe4], axis=1)
        vj = jnp.concatenate([v[:, cols], _ones_col(lk)], axis=1)
        for qi in range(qn.shape[0] // qb):
            rows = slice(qb * qi, qb * (qi + 1))
            (e,) = _exp_weights([_dot_nt(_mla_queries(qn[rows, cols], qp[rows], j), k)])
            o_ref[row0 + qb * qi:row0 + qb * (qi + 1), cols] = _unstack_heads(_normalised(_dot(e, vj)), qb)


def _mla_prompt_kernel(n_prev, *refs):
    pc_ref, gckv_ref, wuk_ref, wuv_ref = refs[:4]
    o_ref, ckv_ref, kpe_ref = refs[4 + n_prev:]
    c = (MLA_NOPE + MLA_ROPE) ** -0.5 * LOG2E
    wuk, wuv = _bf(wuk_ref[0]), _bf(wuv_ref[0])
    for bb in range(PB):
        rows = slice(SEQ * bb, SEQ * (bb + 1))
        ckv = _rms(pc_ref[rows, C_CKV:C_CKV + MLA_KV_RANK], gckv_ref[0])
        ckv_ref[bb] = ckv
        kpe_ref[bb] = pc_ref[rows, C_KPE:C_KPE + MLA_ROPE]
        ckv_b = _bf(ckv)
        _mla_core(pc_ref[rows, C_QN:C_QN + WIDTH] * c, pc_ref[rows, C_QP:C_QP + LANES] * c,
                  _bf(_dot(ckv_b, wuk)), _bf(_tile4(pc_ref[rows, C_KPE:C_KPE + LANES])), _bf(_dot(ckv_b, wuv)),
                  o_ref, SEQ * bb, SEQ)


def _mix_c_prompt(l, pc, g_ckv, w_uk, w_uv, prev):
    n_prev = len(prev)
    w_spec = _layer_spec(l, MLA_KV_RANK, WIDTH)
    return pl.pallas_call(
        functools.partial(_mla_prompt_kernel, n_prev),
        grid=(BATCH // PB,),
        in_specs=[_prompt_rows(SEG_C), _layer_spec(l, 1, MLA_KV_RANK), w_spec, w_spec] + [ANY_SPEC] * n_prev,
        out_specs=[_prompt_rows(WIDTH), _new_cache_spec(l, SEQ, MLA_KV_RANK), _new_cache_spec(l, SEQ, MLA_ROPE)],
        out_shape=[jax.ShapeDtypeStruct((N_TOK, WIDTH), jnp.float32),
                   _new_cache_shape(SEQ, MLA_KV_RANK), _new_cache_shape(SEQ, MLA_ROPE)],
        input_output_aliases={4 + i: 1 + i for i in range(n_prev)},
        compiler_params=_params("parallel"),
        name="mix_c_prompt",
    )(pc, g_ckv, w_uk, w_uv, *prev)


def _mla_sample_kernel(pc_ref, cckv_ref, ckpe_ref, cosq_ref, sinq_ref, cosk_ref, sink_ref,
                       gckv_ref, wuk_ref, wuv_ref, prev_ref, o_ref):
    del prev_ref
    c = (MLA_NOPE + MLA_ROPE) ** -0.5 * LOG2E
    ckv = _rms(pc_ref[:, C_CKV:C_CKV + MLA_KV_RANK], gckv_ref[0])
    ckv_all = jnp.concatenate([_bf(cckv_ref[...]), _bf(ckv)], axis=0)
    kpe_new = _tile4(_rope(pc_ref[:, C_KPE:C_KPE + LANES], cosk_ref[...], sink_ref[...]))
    place = (lax.broadcasted_iota(jnp.int32, (MLA_ROPE, LANES), 1) % MLA_ROPE
             == lax.broadcasted_iota(jnp.int32, (MLA_ROPE, LANES), 0))
    kpe_old = _dot(_bf(ckpe_ref[...]), jnp.where(place, 1.0, 0.0).astype(jnp.bfloat16))
    kpe4 = jnp.concatenate([_bf(kpe_old), _bf(kpe_new)], axis=0)
    qp = _rope(pc_ref[:, C_QP:C_QP + LANES], cosq_ref[...], sinq_ref[...]) * c
    _mla_core(pc_ref[:, C_QN:C_QN + WIDTH] * c, qp, _bf(_dot(ckv_all, _bf(wuk_ref[0]))), kpe4,
              _bf(_dot(ckv_all, _bf(wuv_ref[0]))), o_ref, 0, QB)


def _mix_c_sample(l, pc, cache_ckv, cache_kpe, cosq, sinq, cosk, sink, g_ckv, w_uk, w_uv, o_prev):
    w_spec = _layer_spec(l, MLA_KV_RANK, WIDTH)
    tab_spec = pl.BlockSpec((DEC_SEQ, LANES), lambda b: (0, 0))
    return pl.pallas_call(
        _mla_sample_kernel,
        grid=(DEC_BATCH,),
        in_specs=[_sample_rows(SEG_C), _old_cache_spec(l, PAST_LEN, MLA_KV_RANK), _old_cache_spec(l, PAST_LEN, MLA_ROPE),
                  tab_spec, tab_spec, tab_spec, tab_spec, _layer_spec(l, 1, MLA_KV_RANK), w_spec, w_spec, ANY_SPEC],
        out_specs=_sample_rows(WIDTH),
        out_shape=jax.ShapeDtypeStruct((N_TOK, WIDTH), jnp.float32),
        input_output_aliases={10: 0},
        compiler_params=_params("parallel"),
        name="mix_c_sample",
    )(pc, cache_ckv, cache_kpe, cosq, sinq, cosk, sink, g_ckv, w_uk, w_uv, o_prev)


FF_CHUNK = 1024


def _outffn_kernel(n_x, first, final, *refs):
    x_refs = refs[:n_x]
    (oa_ref, ob_ref, oc_ref, od_ref, wout_ref, g1_ref, gffn_ref,
     sh2_ref, sc2_ref, g2_ref, w1_ref, w2_ref, gfin_ref, y_ref) = refs[n_x:]
    acc = jnp.zeros((TM, D_MODEL), jnp.float32)
    for i, ref in enumerate((oa_ref, ob_ref, oc_ref, od_ref)):
        acc += _dot(_bf(ref[...]), wout_ref[0, WIDTH * i:WIDTH * (i + 1), :])
    x1 = _read_x(x_refs, first) + g1_ref[...] * acc
    hf = _bf(_rms(x1, gffn_ref[0]) * (1.0 + sc2_ref[...]) + sh2_ref[...])
    acc = jnp.zeros((TM, D_MODEL), jnp.float32)
    for c in range(D_FF // FF_CHUNK):
        cols = slice(FF_CHUNK * c, FF_CHUNK * (c + 1))
        a = jnp.square(jnp.maximum(_dot(hf, w1_ref[0, :, cols]), 0.0))
        acc += _dot(_bf(a), w2_ref[0, cols, :])
    y = x1 + g2_ref[...] * acc
    if final:
        y = _rms(y, gfin_ref[...])
    y_ref[...] = y


def _outffn(l, xs, oa, ob, oc, od, w_out, g_ffn, mod, w1, w2, g_final, first, n_tiles):
    def mod_spec(j):
        return pl.BlockSpec((None, None, 1, D_MODEL), lambda i: (l, _row_group(first + i), 0, j))

    def resident(shape):
        return pl.BlockSpec(shape, lambda i: (l,) + (0,) * (len(shape) - 1), pipeline_mode=pl.Buffered(1))

    o_spec = pl.BlockSpec((TM, WIDTH), lambda i: (first + i, 0))
    return pl.pallas_call(
        functools.partial(_outffn_kernel, len(xs), first, l == DEPTH - 1),
        grid=(n_tiles,),
        in_specs=_x_specs(len(xs) == 2, first) + [
            o_spec, o_spec, o_spec, o_spec,
            resident((1, 4 * WIDTH, D_MODEL)),
            mod_spec(2),
            pl.BlockSpec((1, 1, D_MODEL), lambda i: (l, 0, 0)),
            mod_spec(3), mod_spec(4), mod_spec(5),
            resident((1, D_MODEL, D_FF)),
            resident((1, D_FF, D_MODEL)),
            pl.BlockSpec((1, D_MODEL), lambda i: (0, 0)),
        ],
        out_specs=pl.BlockSpec((TM, D_MODEL), lambda i: (i, 0)),
        out_shape=jax.ShapeDtypeStruct((n_tiles * TM, D_MODEL), jnp.float32),
        compiler_params=_params("parallel"),
        name="outffn",
    )(*xs, oa, ob, oc, od, w_out, mod, g_ffn, mod, mod, mod, w1, w2, g_final)


def _rope32_tables():
    t = np.arange(DEC_SEQ)
    rows, cols = (t // GRID_W).astype(np.float64), (t % GRID_W).astype(np.float64)
    half = 8
    freqs = ROPE_BASE ** (-np.arange(half, dtype=np.float64) / half)
    cos, sin = [], []
    for pos in (rows, cols):
        ang = pos[:, None] * freqs[None, :]
        cos += [np.cos(ang), np.cos(ang)]
        sin += [-np.sin(ang), np.sin(ang)]
    return np.concatenate(cos, axis=1).astype(np.float32), np.concatenate(sin, axis=1).astype(np.float32)


def _rope_tables():
    c32, s32 = _rope32_tables()
    tile = lambda a, n: np.tile(a, (1, n))
    pad = np.zeros((DEC_SEQ, 96), np.float32)
    cos_k = np.concatenate([c32, pad + 1.0], axis=1)
    sin_k = np.concatenate([s32, pad], axis=1)
    return (tile(c32, 8), tile(s32, 8),
            tile(c32, 4), tile(s32, 4),
            cos_k, sin_k)


def kernel(x_prompt, x_sample, cache_na_k, cache_na_v, cache_diff_k, cache_diff_v, cache_mla_ckv, cache_mla_kpe, c, c_ctx, w_ada, b_ada, g_mix, g_ffn, w_in, w_out, na_rpb, diff_lq1, diff_lk1, diff_lq2, diff_lk2, diff_g_subln, mla_g_ckv, mla_w_uk, mla_w_uv, sgu_g, sgu_w, sgu_b, w_ff1, w_ff2, g_final):
    f32 = jnp.float32
    m = jnp.concatenate([c_ctx[None, :], c, jnp.zeros((N_MOD_ROWS - 1 - DEC_BATCH, D_MODEL), f32)], axis=0)
    mod = _ada(m, w_ada, b_ada).reshape(DEPTH, N_MOD_ROWS, 1, 6 * D_MODEL)
    cst = _lam_consts(diff_lq1, diff_lk1, diff_lq2, diff_lk2)
    tt = _bias_tiles(na_rpb)
    cos_b, sin_b, cos_q, sin_q, cos_k, sin_k = [jnp.asarray(t) for t in _rope_tables()]

    w_in_t = jnp.swapaxes(w_in, 1, 2)
    w_out_b, w1_b, w2_b = _bf(w_out), _bf(w_ff1), _bf(w_ff2)
    g_mix3 = g_mix.reshape(DEPTH, 1, D_MODEL)
    g_ffn3 = g_ffn.reshape(DEPTH, 1, D_MODEL)
    g_sub2 = jnp.tile(diff_g_subln, (1, 2)).reshape(DEPTH, 1, LANES)
    g_ckv3 = mla_g_ckv.reshape(DEPTH, 1, MLA_KV_RANK)
    sgu_g3 = sgu_g.reshape(DEPTH, 1, WIDTH)
    sgu_bt = sgu_b.transpose(0, 2, 1)
    g_fin2 = g_final.reshape(1, D_MODEL)

    xs = (x_prompt.reshape(N_PROMPT, D_MODEL), x_sample.reshape(N_SAMPLE, D_MODEL))
    new_a, new_b, new_c = (), (), ()
    for l in range(DEPTH):
        pa, pb, pc, od = _inproj(l, xs, g_mix3, mod, w_in_t, sgu_g3, sgu_w, sgu_bt)
        oa, *new_a = _mix_a_prompt(l, pa, new_a)
        oa = _mix_a_sample(l, pa, cache_na_k, cache_na_v, tt, oa)
        ob, *new_b = _mix_b_prompt(l, pb, cst, g_sub2, new_b)
        ob = _mix_b_sample(l, pb, cache_diff_k, cache_diff_v, cos_b, sin_b, cst, g_sub2, ob)
        oc, *new_c = _mix_c_prompt(l, pc, g_ckv3, mla_w_uk, mla_w_uv, new_c)
        oc = _mix_c_sample(l, pc, cache_mla_ckv, cache_mla_kpe, cos_q, sin_q, cos_k, sin_k, g_ckv3,
                           mla_w_uk, mla_w_uv, oc)
        ffn = functools.partial(_outffn, l, xs, oa, ob, oc, od, w_out_b, g_ffn3, mod, w1_b, w2_b, g_fin2)
        if l < DEPTH - 1:
            xs = (ffn(0, TILES_PROMPT + TILES_SAMPLE),)
        else:
            xs = (ffn(0, TILES_PROMPT), ffn(TILES_PROMPT, TILES_SAMPLE))
    y_prompt = xs[0].reshape(BATCH, SEQ, D_MODEL)
    y_sample = xs[1].reshape(DEC_BATCH, DEC_SEQ, D_MODEL)
    return (y_prompt, y_sample, *new_a, *new_b, *new_c)
```

```python
import functools
import math

import numpy as np
import jax
import jax.numpy as jnp
from jax import lax
from jax.experimental import pallas as pl
from jax.experimental.pallas import tpu as pltpu

D_MODEL = 1024
BATCH = 16
SEQ = 256
DEPTH = 4
DEC_BATCH = 2
DEC_SEQ = 1024
PAST_LEN = 512
GRID_W = 64
GRID_ROWS = DEC_SEQ // GRID_W
HEAD_DIM = 64
NA_HEADS = 4
NA_WIN_ROWS = 8
NA_WIN_COLS = 16
DIFF_HEADS = 4
DIFF_QK_DIM = 32
DIFF_V_DIM = 64
MLA_HEADS = 4
MLA_NOPE = 64
MLA_ROPE = 32
MLA_V = 64
MLA_KV_RANK = 128
SGU_GROUPS = 4
SGU_GROUP_DIM = 64
SGU_CHUNK = 128
D_FF = 4 * D_MODEL
ROPE_BASE = 10000.0
EPS = 1e-6
NEG_INF = -1e30
LOG2E = 1.4426950408889634

N_HEADS = 4
N_PAIRS = N_HEADS // 2
LANES = 128
WIDTH = 256
N_PROMPT = BATCH * SEQ
N_SAMPLE = DEC_BATCH * DEC_SEQ
N_TOK = N_PROMPT + N_SAMPLE
N_MOD_ROWS = 8

SEG_A = 3 * WIDTH
SEG_B = 3 * WIDTH
SEG_C = 640
SEG_D = 2 * WIDTH
SEG_C_PAD = 96
IN_COLS_P = SEG_A + SEG_B + SEG_C + SEG_D
O_ATT = 3 * WIDTH

TM = 512
TILES_PROMPT = N_PROMPT // TM
TILES_SAMPLE = N_SAMPLE // TM
PB = 2
QB = 256
VMEM_LIMIT = 56 * 1024 * 1024


def _bf(x):
    return x.astype(jnp.bfloat16)


def _dot(a, b):
    return jnp.dot(a, b, preferred_element_type=jnp.float32)


def _dot_nt(a, b):
    return lax.dot_general(a, b, (((1,), (1,)), ((), ())), preferred_element_type=jnp.float32)


def _rms(x, g):
    ms = jnp.mean(x * x, axis=-1, keepdims=True)
    return x * lax.rsqrt(ms + EPS) * g


def _lane_range(lo, hi, width=LANES):
    lane = lax.broadcasted_iota(jnp.int32, (1, width), 1)
    return (lane >= lo) & (lane < hi)


def _with_ones(v):
    lane = lax.broadcasted_iota(jnp.int32, (v.shape[0], LANES), 1)
    return jnp.concatenate([v, jnp.where(lane == 0, 1.0, 0.0).astype(jnp.bfloat16)], axis=1)


def _exp_weights(parts):
    m = functools.reduce(jnp.maximum, [jnp.max(s, axis=-1, keepdims=True) for s in parts])
    return [_bf(jnp.exp2(s - m)) for s in parts]


def _attend(scores, values):
    es = _exp_weights(scores)
    return functools.reduce(lambda a, b: a + b, [_dot(e, v) for e, v in zip(es, values)])


def _normalised(o_ext):
    return o_ext[:, 0:LANES] * (1.0 / o_ext[:, LANES:LANES + 1])


def _swap8(x):
    lane = lax.broadcasted_iota(jnp.int32, (1, LANES), 1)
    return jnp.where((lane & 15) < 8, pltpu.roll(x, LANES - 8, 1), pltpu.roll(x, 8, 1))


def _rope(x, cos, sin):
    outs = []
    for c in range(x.shape[1] // LANES):
        sl = slice(LANES * c, LANES * (c + 1))
        xc = x[:, sl]
        outs.append(xc * cos[:, sl] + _swap8(xc) * sin[:, sl])
    return outs[0] if len(outs) == 1 else jnp.concatenate(outs, axis=1)


def _tile4(x):
    return x + pltpu.roll(x, 32, 1) + pltpu.roll(x, 64, 1) + pltpu.roll(x, 96, 1)


def _params(*sem):
    return pltpu.CompilerParams(dimension_semantics=sem, vmem_limit_bytes=VMEM_LIMIT)


ADA_TN = 1536


def _ada_kernel(m_ref, w_ref, b_ref, o_ref):
    m = m_ref[...]
    s = m * jax.nn.sigmoid(m)
    o_ref[0] = _dot(_bf(s), _bf(w_ref[0])) + b_ref[0]


def _ada(m, w_ada, b_ada):
    n = 6 * D_MODEL
    return pl.pallas_call(
        _ada_kernel,
        grid=(DEPTH, n // ADA_TN),
        in_specs=[
            pl.BlockSpec((N_MOD_ROWS, D_MODEL), lambda l, j: (0, 0)),
            pl.BlockSpec((1, D_MODEL, ADA_TN), lambda l, j: (l, 0, j)),
            pl.BlockSpec((1, 1, ADA_TN), lambda l, j: (l, 0, j)),
        ],
        out_specs=pl.BlockSpec((1, N_MOD_ROWS, ADA_TN), lambda l, j: (l, 0, j)),
        out_shape=jax.ShapeDtypeStruct((DEPTH, N_MOD_ROWS, n), jnp.float32),
        compiler_params=_params("parallel", "parallel"),
        name="ada",
    )(m, w_ada, b_ada.reshape(DEPTH, 1, n))


def _lam_kernel(lq1_ref, lk1_ref, lq2_ref, lk2_ref, init_ref, o_ref):
    init = init_ref[...]
    a = jnp.exp(jnp.sum(lq1_ref[...] * lk1_ref[...], axis=-1, keepdims=True))
    b = jnp.exp(jnp.sum(lq2_ref[...] * lk2_ref[...], axis=-1, keepdims=True))
    lam = a - b + init
    post = 1.0 - init
    for l in range(DEPTH):
        o_ref[l, 0:1, :] = jnp.broadcast_to(lam[l:l + 1], (1, LANES))
        o_ref[l, 1:2, :] = jnp.broadcast_to(post[l:l + 1], (1, LANES))


def _lam_consts(lq1, lk1, lq2, lk2):
    init = np.array([[0.8 - 0.6 * math.exp(-0.3 * l)] for l in range(DEPTH)], np.float32)
    return pl.pallas_call(
        _lam_kernel,
        out_shape=jax.ShapeDtypeStruct((DEPTH, 2, LANES), jnp.float32),
        name="diff_lambda",
    )(lq1, lk1, lq2, lk2, jnp.asarray(init))


N_DROW = 2 * NA_WIN_ROWS - 1
N_DCOL = 2 * NA_WIN_COLS - 1


def _bias_kernel(rpb_ref, o_ref):
    l = pl.program_id(0)
    h = pl.program_id(1)
    base = (l * NA_HEADS + h) * (N_DROW * N_DCOL)
    cq = lax.broadcasted_iota(jnp.int32, (GRID_W, LANES), 0)
    lane = lax.broadcasted_iota(jnp.int32, (GRID_W, LANES), 1)
    ck = lane & (GRID_W - 1)
    dcol = jnp.clip(ck - cq, -(NA_WIN_COLS - 1), NA_WIN_COLS - 1) + (NA_WIN_COLS - 1)
    hi = lane >= GRID_W
    for a in range(N_DROW - 1):
        acc = jnp.zeros((GRID_W, LANES), jnp.float32)
        for j in range(N_DCOL):
            lo_v = rpb_ref[base + a * N_DCOL + j]
            hi_v = rpb_ref[base + (a + 1) * N_DCOL + j]
            acc = jnp.where(dcol == j, jnp.where(hi, hi_v, lo_v), acc)
        o_ref[0, 0, a] = acc * LOG2E


def _bias_tiles(na_rpb):
    return pl.pallas_call(
        _bias_kernel,
        grid=(DEPTH, NA_HEADS),
        in_specs=[pl.BlockSpec(memory_space=pltpu.SMEM)],
        out_specs=pl.BlockSpec((1, 1, N_DROW - 1, GRID_W, LANES), lambda l, h: (l, h, 0, 0, 0)),
        out_shape=jax.ShapeDtypeStruct((DEPTH, NA_HEADS, N_DROW - 1, GRID_W, LANES), jnp.float32),
        compiler_params=_params("parallel", "parallel"),
        name="na_bias_tiles",
    )(na_rpb.reshape(-1))


def _row_group(i):
    return jnp.where(i < TILES_PROMPT, 0, 1 + (i - TILES_PROMPT) // (DEC_SEQ // TM))


def _split_specs(width, first):
    return [pl.BlockSpec((TM, width), lambda i: (jnp.minimum(first + i, TILES_PROMPT - 1), 0)),
            pl.BlockSpec((TM, width), lambda i: (jnp.maximum(first + i - TILES_PROMPT, 0), 0))]


def _x_specs(split, first):
    if not split:
        return [pl.BlockSpec((TM, D_MODEL), lambda i: (first + i, 0))]
    return _split_specs(D_MODEL, first)


def _read_tile(refs, first):
    if len(refs) == 1:
        return refs[0][...]
    return jnp.where(first + pl.program_id(0) < TILES_PROMPT, refs[0][...], refs[1][...])


IN_COLS = 2592
IN_QC, IN_CKV, IN_D = 1536, 1920, 2080
TR_ROWS = 256


def _gelu_tanh(x):
    return 0.5 * x * (1.0 + jnp.tanh(math.sqrt(2.0 / math.pi) * (x + 0.044715 * (x * x * x))))


def _sgu(pd, g, w_ref, bt):
    u = _gelu_tanh(pd[:, 0:WIDTH])
    v = _gelu_tanh(pd[:, WIDTH:2 * WIDTH])
    grp = lax.broadcasted_iota(jnp.int32, (1, WIDTH), 1) // SGU_GROUP_DIM
    v2 = v * v
    ms = jnp.zeros_like(v)
    for gi in range(SGU_GROUPS):
        sel = grp == gi
        tot = jnp.sum(jnp.where(sel, v2, 0.0), axis=-1, keepdims=True)
        ms = jnp.where(sel, tot * (1.0 / SGU_GROUP_DIM), ms)
    vg = _bf(v * lax.rsqrt(ms + EPS) * g)
    outs = []
    for c in range(pd.shape[0] // SGU_CHUNK):
        rows = slice(SGU_CHUNK * c, SGU_CHUNK * (c + 1))
        mixed = jnp.zeros((SGU_CHUNK, WIDTH), jnp.float32)
        for gi in range(SGU_GROUPS):
            full = _dot(_bf(w_ref[0, gi]), vg[rows]) + bt[:, gi:gi + 1]
            mixed = jnp.where(grp == gi, full, mixed)
        outs.append(u[rows] * mixed)
    return jnp.concatenate(outs, axis=0)


def _w_in_row_pieces():
    qn = [(IN_QC + 96 * h, MLA_NOPE) for h in range(MLA_HEADS)]
    qp = [(IN_QC + 96 * h + MLA_NOPE, MLA_ROPE) for h in range(MLA_HEADS)]
    seg_c = qn + qp + [(IN_CKV, MLA_KV_RANK + MLA_ROPE)]
    return (0, SEG_A + SEG_B), seg_c, (IN_D, SEG_D)


def _load_w_in(wt_ref, w_scr):
    ab, seg_c, d = _w_in_row_pieces()
    c_rows = jnp.concatenate([wt_ref[0, s:s + n, :] for s, n in seg_c]
                             + [jnp.zeros((SEG_C_PAD, D_MODEL), jnp.float32)], axis=0)
    for t in range(SEG_C // LANES):
        w_scr[:, SEG_A + SEG_B + LANES * t:SEG_A + SEG_B + LANES * (t + 1)] = _bf(c_rows[LANES * t:LANES * (t + 1)].T)
    for (src, n), dst in ((ab, 0), (d, SEG_A + SEG_B + SEG_C)):
        for t in range(n // TR_ROWS):
            rows = wt_ref[0, src + TR_ROWS * t:src + TR_ROWS * (t + 1), :]
            w_scr[:, dst + TR_ROWS * t:dst + TR_ROWS * (t + 1)] = _bf(rows.T)


def _inproj_kernel(n_x, *refs):
    x_refs = refs[:n_x]
    (g_ref, sh_ref, sc_ref, wt_ref, sg_ref, sw_ref, sbt_ref, pa_ref, pb_ref, pc_ref, od_ref, w_scr) = refs[n_x:]

    @pl.when(pl.program_id(0) == 0)
    def _():
        _load_w_in(wt_ref, w_scr)

    h = _rms(_read_tile(x_refs, 0), g_ref[0]) * (1.0 + sc_ref[...]) + sh_ref[...]
    hb = _bf(h)
    off = 0
    for ref in (pa_ref, pb_ref, pc_ref):
        n = ref.shape[1]
        ref[...] = _dot(hb, w_scr[:, off:off + n])
        off += n
    od_ref[...] = _sgu(_dot(hb, w_scr[:, off:off + SEG_D]), sg_ref[0], sw_ref, sbt_ref[0])


def _inproj(l, xs, g_mix, mod, w_in_t, sgu_g, sgu_w, sgu_bt):
    def mod_spec(j):
        return pl.BlockSpec((None, None, 1, D_MODEL), lambda i: (l, _row_group(i), 0, j))

    widths = (SEG_A, SEG_B, SEG_C, WIDTH)
    return pl.pallas_call(
        functools.partial(_inproj_kernel, len(xs)),
        grid=(N_TOK // TM,),
        in_specs=_x_specs(len(xs) == 2, 0) + [
            pl.BlockSpec((1, 1, D_MODEL), lambda i: (l, 0, 0)),
            mod_spec(0), mod_spec(1),
            pl.BlockSpec((1, IN_COLS, D_MODEL), lambda i: (l, 0, 0), pipeline_mode=pl.Buffered(1)),
            pl.BlockSpec((1, 1, WIDTH), lambda i: (l, 0, 0)),
            pl.BlockSpec((1, SGU_GROUPS, SGU_CHUNK, SGU_CHUNK), lambda i: (l, 0, 0, 0)),
            pl.BlockSpec((1, SGU_CHUNK, SGU_GROUPS), lambda i: (l, 0, 0)),
        ],
        out_specs=[pl.BlockSpec((TM, n), lambda i: (i, 0)) for n in widths],
        out_shape=[jax.ShapeDtypeStruct((N_TOK, n), jnp.float32) for n in widths],
        scratch_shapes=[pltpu.VMEM((D_MODEL, IN_COLS_P), jnp.bfloat16)],
        compiler_params=_params("arbitrary"),
        name="inproj",
    )(*xs, g_mix, mod, mod, w_in_t, sgu_g, sgu_w, sgu_bt)


C_QN, C_QP, C_CKV, C_KPE = 0, 256, 384, 512


def _stack_heads(qp):
    lo = _lane_range(0, 64)
    return jnp.concatenate([_bf(jnp.where(lo, qp, 0.0)), _bf(jnp.where(lo, 0.0, qp))], axis=0)


def _unstack_heads(o, n):
    return jnp.where(_lane_range(0, 64), o[0:n], o[n:2 * n])


def _pair_t(c_ref, j):
    return jnp.concatenate([c_ref[2 * j], c_ref[2 * j + 1]], axis=0)


def _stack_components(qp):
    return jnp.concatenate([_bf(jnp.where(_lane_range(32 * t, 32 * (t + 1)), qp, 0.0)) for t in range(4)], axis=0)


def _diff_finish(o, n, lam, post, g2):
    den = o[:, LANES:LANES + 1]
    outs = []
    for t in range(2):
        p1 = o[2 * t * n:(2 * t + 1) * n, 0:LANES] * (1.0 / den[2 * t * n:(2 * t + 1) * n])
        p2 = o[(2 * t + 1) * n:(2 * t + 2) * n, 0:LANES] * (lam / den[(2 * t + 1) * n:(2 * t + 2) * n])
        d = p1 - p2
        own = _lane_range(64 * t, 64 * (t + 1))
        ms = jnp.sum(jnp.where(own, d * d, 0.0), axis=-1, keepdims=True) * (1.0 / DIFF_V_DIM)
        outs.append(d * lax.rsqrt(ms + EPS))
    return jnp.where(_lane_range(0, 64), outs[0], outs[1]) * g2 * post


def _mla_queries(qn_pair, qp_all, j):
    halves = []
    for t in range(2):
        h = 2 * j + t
        halves.append(jnp.concatenate([
            _bf(jnp.where(_lane_range(64 * t, 64 * (t + 1)), qn_pair, 0.0)),
            _bf(jnp.where(_lane_range(MLA_ROPE * h, MLA_ROPE * (h + 1)), qp_all, 0.0))], axis=1))
    return jnp.concatenate(halves, axis=0)


def _write_heads_t(p_ref, rows, col0, out_ref, bb):
    xt = p_ref[rows, col0:col0 + WIDTH].T
    for h in range(N_HEADS):
        out_ref[bb, 0, h] = xt[64 * h:64 * (h + 1)]
    _clear_other_layers(out_ref, bb)


def _clear_other_layers(out_ref, bb):
    if out_ref.shape[1] > 1:
        out_ref[bb, 1:] = jnp.zeros(out_ref.shape[1:], jnp.float32)[1:]


def _mix_prompt_kernel(n_prev, *refs):
    pa_ref, pb_ref, pc_ref, cst_ref, gsub_ref, gckv_ref, wuk_ref, wuv_ref = refs[:8]
    o_ref, nak_ref, nav_ref, dk_ref, dv_ref, ckv_ref, kpe_ref = refs[8 + n_prev:]
    c_a = HEAD_DIM ** -0.5 * LOG2E
    c_b = DIFF_QK_DIM ** -0.5 * LOG2E
    c_c = (MLA_NOPE + MLA_ROPE) ** -0.5 * LOG2E
    lam = cst_ref[0, 0:1, 0:1]
    post = cst_ref[0, 1:2, 0:1]
    wuk, wuv = _bf(wuk_ref[0]), _bf(wuv_ref[0])
    for bb in range(PB):
        rows = slice(SEQ * bb, SEQ * (bb + 1))
        for j in range(N_PAIRS):
            cols = slice(LANES * j, LANES * (j + 1))
            k = _bf(pa_ref[rows, WIDTH + LANES * j:WIDTH + LANES * (j + 1)])
            v = _with_ones(_bf(pa_ref[rows, 2 * WIDTH + LANES * j:2 * WIDTH + LANES * (j + 1)]))
            o = _attend([_dot_nt(_stack_heads(pa_ref[rows, cols] * c_a), k)], [v])
            o_ref[rows, cols] = _unstack_heads(_normalised(o), SEQ)
        _write_heads_t(pa_ref, rows, WIDTH, nak_ref, bb)
        _write_heads_t(pa_ref, rows, 2 * WIDTH, nav_ref, bb)
        for j in range(N_PAIRS):
            cols = slice(LANES * j, LANES * (j + 1))
            k = _bf(pb_ref[rows, WIDTH + LANES * j:WIDTH + LANES * (j + 1)])
            v = _with_ones(_bf(pb_ref[rows, 2 * WIDTH + LANES * j:2 * WIDTH + LANES * (j + 1)]))
            o = _attend([_dot_nt(_stack_components(pb_ref[rows, cols] * c_b), k)], [v])
            o_ref[rows, WIDTH + LANES * j:WIDTH + LANES * (j + 1)] = _diff_finish(o, SEQ, lam, post, gsub_ref[0])
        _write_heads_t(pb_ref, rows, WIDTH, dk_ref, bb)
        _write_heads_t(pb_ref, rows, 2 * WIDTH, dv_ref, bb)
        ckv = _rms(pc_ref[rows, C_CKV:C_CKV + MLA_KV_RANK], gckv_ref[0])
        ckv_ref[bb, 0] = ckv
        _clear_other_layers(ckv_ref, bb)
        kpe_slot = pc_ref[rows, C_KPE:C_KPE + LANES]
        kpe_ref[bb, 0] = kpe_slot.T[0:MLA_ROPE]
        _clear_other_layers(kpe_ref, bb)
        ckv_b = _bf(ckv)
        kn = _bf(_dot(ckv_b, wuk))
        vv = _bf(_dot(ckv_b, wuv))
        kpe4 = _bf(_tile4(kpe_slot))
        qn = pc_ref[rows, C_QN:C_QN + WIDTH] * c_c
        qp = pc_ref[rows, C_QP:C_QP + LANES] * c_c
        for j in range(N_PAIRS):
            cols = slice(LANES * j, LANES * (j + 1))
            k = jnp.concatenate([kn[:, cols], kpe4], axis=1)
            o = _attend([_dot_nt(_mla_queries(qn[:, cols], qp, j), k)], [_with_ones(vv[:, cols])])
            o_ref[rows, 2 * WIDTH + LANES * j:2 * WIDTH + LANES * (j + 1)] = _unstack_heads(_normalised(o), SEQ)


def _mix_prompt(l, pa, pb, pc, cst, g_sub2, g_ckv, w_uk, w_uv, prev):
    n_prev = len(prev)
    tails = [(NA_HEADS, HEAD_DIM, SEQ)] * 2 + [(DIFF_HEADS, 64, SEQ)] * 2 + [(SEQ, MLA_KV_RANK), (MLA_ROPE, SEQ)]

    def cache_spec(tail):
        if l == 0:
            return pl.BlockSpec((PB, DEPTH) + tail, lambda b: (b, 0) + (0,) * len(tail))
        return pl.BlockSpec((PB, 1) + tail, lambda b: (b, l) + (0,) * len(tail))

    def rows(width):
        return pl.BlockSpec((PB * SEQ, width), lambda b: (b, 0))

    def layer(*tail):
        return pl.BlockSpec((1,) + tail, lambda b: (l,) + (0,) * len(tail))

    return pl.pallas_call(
        functools.partial(_mix_prompt_kernel, n_prev),
        grid=(BATCH // PB,),
        in_specs=[rows(SEG_A), rows(SEG_B), rows(SEG_C), layer(2, LANES), layer(1, LANES), layer(1, MLA_KV_RANK),
                  layer(MLA_KV_RANK, WIDTH), layer(MLA_KV_RANK, WIDTH)] + [pl.BlockSpec(memory_space=pl.ANY)] * n_prev,
        out_specs=[rows(O_ATT)] + [cache_spec(t) for t in tails],
        out_shape=[jax.ShapeDtypeStruct((N_PROMPT, O_ATT), jnp.float32)]
        + [jax.ShapeDtypeStruct((BATCH, DEPTH) + t, jnp.float32) for t in tails],
        input_output_aliases={8 + i: 1 + i for i in range(n_prev)},
        compiler_params=_params("parallel"),
        name="mix_prompt",
    )(pa, pb, pc, cst, g_sub2, g_ckv, w_uk, w_uv, *prev)


def _na_row_groups():
    kh = min(NA_WIN_ROWS, GRID_ROWS)
    r0s = [min(max(r - kh // 2, 0), GRID_ROWS - kh) for r in range(GRID_ROWS)]
    groups = []
    for r, r0 in enumerate(r0s):
        if groups and groups[-1][2] == r0:
            groups[-1][1] = r
        else:
            groups.append([r, r, r0])
    return kh, [tuple(g) for g in groups]


def _na_sample(pa_ref, ck_ref, cv_ref, tt_ref, o_ref):
    c = HEAD_DIM ** -0.5 * LOG2E
    kh, groups = _na_row_groups()
    lk = kh * GRID_W

    def in_window(n):
        cq = lax.broadcasted_iota(jnp.int32, (n, lk), 0) & (GRID_W - 1)
        ck = lax.broadcasted_iota(jnp.int32, (n, lk), 1) & (GRID_W - 1)
        c0 = jnp.clip(cq - NA_WIN_COLS // 2, 0, GRID_W - NA_WIN_COLS)
        return (ck >= c0) & (ck < c0 + NA_WIN_COLS)

    for j in range(N_PAIRS):
        cols = slice(LANES * j, LANES * (j + 1))
        q = _stack_heads(pa_ref[:, cols] * c)
        k = _bf(pa_ref[:, WIDTH + LANES * j:WIDTH + LANES * (j + 1)])
        v = _with_ones(_bf(pa_ref[:, 2 * WIDTH + LANES * j:2 * WIDTH + LANES * (j + 1)]))
        kc_t = _bf(_pair_t(ck_ref, j))
        vc = _with_ones(_bf(_pair_t(cv_ref, j).T))
        for (r_lo, r_hi, r0) in groups:
            n = (r_hi - r_lo + 1) * GRID_W
            rows = slice(r_lo * GRID_W, r_lo * GRID_W + n)
            keys = slice(r0 * GRID_W, r0 * GRID_W + lk)
            qg = jnp.concatenate([q[rows], q[DEC_SEQ + r_lo * GRID_W:DEC_SEQ + r_lo * GRID_W + n]], axis=0)
            bias = jnp.concatenate([
                jnp.concatenate([tt_ref[2 * j + t, r0 + 2 * i - r + NA_WIN_ROWS - 1] for i in range(kh // 2)], axis=1)
                for t in range(2) for r in range(r_lo, r_hi + 1)], axis=0)
            s_loc = jnp.where(in_window(2 * n), _dot_nt(qg, k[keys]) + bias, NEG_INF)
            o = _attend([_dot(qg, kc_t), s_loc], [vc, v[keys]])
            o_ref[rows, cols] = _unstack_heads(_normalised(o), n)


def _diff_sample(pb_ref, ck_ref, cv_ref, cos_ref, sin_ref, cst_ref, g_ref, o_ref):
    c = DIFF_QK_DIM ** -0.5 * LOG2E
    lam = cst_ref[0, 0:1, 0:1]
    post = cst_ref[0, 1:2, 0:1]
    cos = cos_ref[...]
    sin = sin_ref[...]
    q = _rope(pb_ref[:, 0:WIDTH], cos, sin) * c
    k_new = _bf(_rope(pb_ref[:, WIDTH:2 * WIDTH], cos, sin))
    for j in range(N_PAIRS):
        cols = slice(LANES * j, LANES * (j + 1))
        kc_t = _bf(_pair_t(ck_ref, j))
        vc = _with_ones(_bf(_pair_t(cv_ref, j).T))
        v = _with_ones(_bf(pb_ref[:, 2 * WIDTH + LANES * j:2 * WIDTH + LANES * (j + 1)]))
        for qi in range(DEC_SEQ // QB):
            rows = slice(QB * qi, QB * (qi + 1))
            qs = _stack_components(q[rows, cols])
            o = _attend([_dot(qs, kc_t), _dot_nt(qs, k_new[:, cols])], [vc, v])
            o_ref[rows, WIDTH + LANES * j:WIDTH + LANES * (j + 1)] = _diff_finish(o, QB, lam, post, g_ref[0])


def _mla_sample(pc_ref, cckv_ref, ckpe_ref, cosq_ref, sinq_ref, cosk_ref, sink_ref, gckv_ref, wuk_ref, wuv_ref, o_ref):
    c = (MLA_NOPE + MLA_ROPE) ** -0.5 * LOG2E
    wuk, wuv = _bf(wuk_ref[0]), _bf(wuv_ref[0])
    ckv_new = _bf(_rms(pc_ref[:, C_CKV:C_CKV + MLA_KV_RANK], gckv_ref[0]))
    ckv_old = _bf(cckv_ref[...])
    kpe_new = _bf(_tile4(_rope(pc_ref[:, C_KPE:C_KPE + LANES], cosk_ref[...], sink_ref[...])))
    kpe_old = _bf(jnp.concatenate([ckpe_ref[...]] * MLA_HEADS, axis=0).T)
    kn_new, kn_old = _bf(_dot(ckv_new, wuk)), _bf(_dot(ckv_old, wuk))
    v_new, v_old = _bf(_dot(ckv_new, wuv)), _bf(_dot(ckv_old, wuv))
    qn = pc_ref[:, C_QN:C_QN + WIDTH] * c
    qp = _rope(pc_ref[:, C_QP:C_QP + LANES], cosq_ref[...], sinq_ref[...]) * c
    for j in range(N_PAIRS):
        cols = slice(LANES * j, LANES * (j + 1))
        k_old = jnp.concatenate([kn_old[:, cols], kpe_old], axis=1)
        k_new = jnp.concatenate([kn_new[:, cols], kpe_new], axis=1)
        vo, vn = _with_ones(v_old[:, cols]), _with_ones(v_new[:, cols])
        for qi in range(DEC_SEQ // QB):
            rows = slice(QB * qi, QB * (qi + 1))
            qs = _mla_queries(qn[rows, cols], qp[rows], j)
            o = _attend([_dot_nt(qs, k_old), _dot_nt(qs, k_new)], [vo, vn])
            o_ref[rows, 2 * WIDTH + LANES * j:2 * WIDTH + LANES * (j + 1)] = _unstack_heads(_normalised(o), QB)


def _mix_sample_kernel(pa_ref, pb_ref, pc_ref, cnak_ref, cnav_ref, cdk_ref, cdv_ref, cckv_ref, ckpe_ref, tt_ref,
                       cosb_ref, sinb_ref, cosq_ref, sinq_ref, cosk_ref, sink_ref,
                       cst_ref, gsub_ref, gckv_ref, wuk_ref, wuv_ref, o_ref):
    _na_sample(pa_ref, cnak_ref, cnav_ref, tt_ref, o_ref)
    _diff_sample(pb_ref, cdk_ref, cdv_ref, cosb_ref, sinb_ref, cst_ref, gsub_ref, o_ref)
    _mla_sample(pc_ref, cckv_ref, ckpe_ref, cosq_ref, sinq_ref, cosk_ref, sink_ref, gckv_ref, wuk_ref, wuv_ref, o_ref)


def _mix_sample(l, pa, pb, pc, caches_t, tt, tables, cst, g_sub2, g_ckv, w_uk, w_uv):
    first = N_PROMPT // DEC_SEQ

    def rows(width):
        return pl.BlockSpec((DEC_SEQ, width), lambda b: (first + b, 0))

    def cache(*tail):
        return pl.BlockSpec((None, None) + tail, lambda b: (b, l) + (0,) * len(tail))

    def layer(*tail):
        return pl.BlockSpec((1,) + tail, lambda b: (l,) + (0,) * len(tail))

    def table(width):
        return pl.BlockSpec((DEC_SEQ, width), lambda b: (0, 0))

    kv_t = cache(N_HEADS, 64, PAST_LEN)
    return pl.pallas_call(
        _mix_sample_kernel,
        grid=(DEC_BATCH,),
        in_specs=[rows(SEG_A), rows(SEG_B), rows(SEG_C), kv_t, kv_t, kv_t, kv_t,
                  cache(PAST_LEN, MLA_KV_RANK), cache(MLA_ROPE, PAST_LEN),
                  pl.BlockSpec((None, NA_HEADS, N_DROW - 1, GRID_W, LANES), lambda b: (l, 0, 0, 0, 0)),
                  table(WIDTH), table(WIDTH), table(LANES), table(LANES), table(LANES), table(LANES),
                  layer(2, LANES), layer(1, LANES), layer(1, MLA_KV_RANK),
                  layer(MLA_KV_RANK, WIDTH), layer(MLA_KV_RANK, WIDTH)],
        out_specs=pl.BlockSpec((DEC_SEQ, O_ATT), lambda b: (b, 0)),
        out_shape=jax.ShapeDtypeStruct((N_SAMPLE, O_ATT), jnp.float32),
        compiler_params=_params("parallel"),
        name="mix_sample",
    )(pa, pb, pc, *caches_t, tt, *tables, cst, g_sub2, g_ckv, w_uk, w_uv)


FF_CHUNK = 1024


def _outffn_kernel(n_x, first, final, *refs):
    x_refs, o_refs = refs[:n_x], refs[n_x:n_x + 2]
    (od_ref, wout_ref, g1_ref, gffn_ref, sh2_ref, sc2_ref, g2_ref, w1_ref, w2_ref, gfin_ref, y_ref) = refs[n_x + 2:]
    acc = (_dot(_bf(_read_tile(o_refs, first)), wout_ref[0, 0:O_ATT, :])
           + _dot(_bf(od_ref[...]), wout_ref[0, O_ATT:O_ATT + WIDTH, :]))
    x1 = _read_tile(x_refs, first) + g1_ref[...] * acc
    hf = _bf(_rms(x1, gffn_ref[0]) * (1.0 + sc2_ref[...]) + sh2_ref[...])
    acc = jnp.zeros((TM, D_MODEL), jnp.float32)
    for c in range(D_FF // FF_CHUNK):
        cols = slice(FF_CHUNK * c, FF_CHUNK * (c + 1))
        a = jnp.square(jnp.maximum(_dot(hf, w1_ref[0, :, cols]), 0.0))
        acc += _dot(_bf(a), w2_ref[0, cols, :])
    y = x1 + g2_ref[...] * acc
    if final:
        y = _rms(y, gfin_ref[...])
    y_ref[...] = y


def _outffn(l, xs, o_p, o_s, od, w_out, g_ffn, mod, w1, w2, g_final, first, n_tiles):
    def mod_spec(j):
        return pl.BlockSpec((None, None, 1, D_MODEL), lambda i: (l, _row_group(first + i), 0, j))

    def resident(shape):
        return pl.BlockSpec(shape, lambda i: (l,) + (0,) * (len(shape) - 1), pipeline_mode=pl.Buffered(1))

    return pl.pallas_call(
        functools.partial(_outffn_kernel, len(xs), first, l == DEPTH - 1),
        grid=(n_tiles,),
        in_specs=_x_specs(len(xs) == 2, first) + _split_specs(O_ATT, first) + [
            pl.BlockSpec((TM, WIDTH), lambda i: (first + i, 0)),
            resident((1, 4 * WIDTH, D_MODEL)),
            mod_spec(2),
            pl.BlockSpec((1, 1, D_MODEL), lambda i: (l, 0, 0)),
            mod_spec(3), mod_spec(4), mod_spec(5),
            resident((1, D_MODEL, D_FF)),
            resident((1, D_FF, D_MODEL)),
            pl.BlockSpec((1, D_MODEL), lambda i: (0, 0)),
        ],
        out_specs=pl.BlockSpec((TM, D_MODEL), lambda i: (i, 0)),
        out_shape=jax.ShapeDtypeStruct((n_tiles * TM, D_MODEL), jnp.float32),
        compiler_params=_params("parallel"),
        name="outffn",
    )(*xs, o_p, o_s, od, w_out, mod, g_ffn, mod, mod, mod, w1, w2, g_final)


def _rope32_tables():
    t = np.arange(DEC_SEQ)
    rows, cols = (t // GRID_W).astype(np.float64), (t % GRID_W).astype(np.float64)
    half = 8
    freqs = ROPE_BASE ** (-np.arange(half, dtype=np.float64) / half)
    cos, sin = [], []
    for pos in (rows, cols):
        ang = pos[:, None] * freqs[None, :]
        cos += [np.cos(ang), np.cos(ang)]
        sin += [-np.sin(ang), np.sin(ang)]
    return np.concatenate(cos, axis=1).astype(np.float32), np.concatenate(sin, axis=1).astype(np.float32)


def _rope_tables():
    c32, s32 = _rope32_tables()
    tile = lambda a, n: np.tile(a, (1, n))
    pad = np.zeros((DEC_SEQ, 96), np.float32)
    cos_k = np.concatenate([c32, pad + 1.0], axis=1)
    sin_k = np.concatenate([s32, pad], axis=1)
    return (tile(c32, 8), tile(s32, 8),
            tile(c32, 4), tile(s32, 4),
            cos_k, sin_k)


def kernel(x_prompt, x_sample, cache_na_k, cache_na_v, cache_diff_k, cache_diff_v, cache_mla_ckv, cache_mla_kpe, c, c_ctx, w_ada, b_ada, g_mix, g_ffn, w_in, w_out, na_rpb, diff_lq1, diff_lk1, diff_lq2, diff_lk2, diff_g_subln, mla_g_ckv, mla_w_uk, mla_w_uv, sgu_g, sgu_w, sgu_b, w_ff1, w_ff2, g_final):
    f32 = jnp.float32
    m = jnp.concatenate([c_ctx[None, :], c, jnp.zeros((N_MOD_ROWS - 1 - DEC_BATCH, D_MODEL), f32)], axis=0)
    mod = _ada(m, w_ada, b_ada).reshape(DEPTH, N_MOD_ROWS, 1, 6 * D_MODEL)
    cst = _lam_consts(diff_lq1, diff_lk1, diff_lq2, diff_lk2)
    tt = _bias_tiles(na_rpb)
    tables = [jnp.asarray(t) for t in _rope_tables()]

    t_last = lambda a: jnp.swapaxes(a, -1, -2)
    w_in_t = t_last(w_in)
    caches_t = (t_last(cache_na_k), t_last(cache_na_v), t_last(cache_diff_k), t_last(cache_diff_v),
                cache_mla_ckv, t_last(cache_mla_kpe))
    w_out_b, w1_b, w2_b = _bf(w_out), _bf(w_ff1), _bf(w_ff2)
    g_mix3 = g_mix.reshape(DEPTH, 1, D_MODEL)
    g_ffn3 = g_ffn.reshape(DEPTH, 1, D_MODEL)
    g_sub2 = jnp.tile(diff_g_subln, (1, 2)).reshape(DEPTH, 1, LANES)
    g_ckv3 = mla_g_ckv.reshape(DEPTH, 1, MLA_KV_RANK)
    sgu_g3 = sgu_g.reshape(DEPTH, 1, WIDTH)
    sgu_bt = sgu_b.transpose(0, 2, 1)
    g_fin2 = g_final.reshape(1, D_MODEL)

    xs = (x_prompt.reshape(N_PROMPT, D_MODEL), x_sample.reshape(N_SAMPLE, D_MODEL))
    new = ()
    for l in range(DEPTH):
        pa, pb, pc, od = _inproj(l, xs, g_mix3, mod, w_in_t, sgu_g3, sgu_w, sgu_bt)
        o_p, *new = _mix_prompt(l, pa, pb, pc, cst, g_sub2, g_ckv3, mla_w_uk, mla_w_uv, new)
        o_s = _mix_sample(l, pa, pb, pc, caches_t, tt, tables, cst, g_sub2, g_ckv3, mla_w_uk, mla_w_uv)
        ffn = functools.partial(_outffn, l, xs, o_p, o_s, od, w_out_b, g_ffn3, mod, w1_b, w2_b, g_fin2)
        if l < DEPTH - 1:
            xs = (ffn(0, TILES_PROMPT + TILES_SAMPLE),)
        else:
            xs = (ffn(0, TILES_PROMPT), ffn(TILES_PROMPT, TILES_SAMPLE))
    y_prompt = xs[0].reshape(BATCH, SEQ, D_MODEL)
    y_sample = xs[1].reshape(DEC_BATCH, DEC_SEQ, D_MODEL)
    na_k, na_v, diff_k, diff_v, mla_ckv, mla_kpe = new
    return (y_prompt, y_sample, t_last(na_k), t_last(na_v), t_last(diff_k), t_last(diff_v), mla_ckv, t_last(mla_kpe))
```

```python
import functools
import math

import numpy as np
import jax
import jax.numpy as jnp
from jax import lax
from jax.experimental import pallas as pl
from jax.experimental.pallas import tpu as pltpu

D_MODEL = 1024
BATCH = 16
SEQ = 256
DEPTH = 4
DEC_BATCH = 2
DEC_SEQ = 1024
PAST_LEN = 512
GRID_W = 64
GRID_ROWS = DEC_SEQ // GRID_W
HEAD_DIM = 64
NA_HEADS = 4
NA_WIN_ROWS = 8
NA_WIN_COLS = 16
DIFF_HEADS = 4
DIFF_QK_DIM = 32
DIFF_V_DIM = 64
MLA_HEADS = 4
MLA_NOPE = 64
MLA_ROPE = 32
MLA_V = 64
MLA_KV_RANK = 128
SGU_GROUPS = 4
SGU_GROUP_DIM = 64
SGU_CHUNK = 128
D_FF = 4 * D_MODEL
ROPE_BASE = 10000.0
EPS = 1e-6
NEG_INF = -1e30
LOG2E = 1.4426950408889634

N_HEADS = 4
N_PAIRS = N_HEADS // 2
LANES = 128
WIDTH = 256
N_PROMPT = BATCH * SEQ
N_SAMPLE = DEC_BATCH * DEC_SEQ
N_TOK = N_PROMPT + N_SAMPLE
N_MOD_ROWS = 8

SEG_A = 3 * WIDTH
SEG_B = 3 * WIDTH
SEG_C = 640
SEG_D = 2 * WIDTH
SEG_C_PAD = 96
IN_COLS_P = SEG_A + SEG_B + SEG_C + SEG_D
O_ATT = 3 * WIDTH

TM = 512
TILES_PROMPT = N_PROMPT // TM
TILES_SAMPLE = N_SAMPLE // TM
PB = 2
QB = 256
VMEM_LIMIT = 56 * 1024 * 1024


def _bf(x):
    return x.astype(jnp.bfloat16)


def _dot(a, b):
    return jnp.dot(a, b, preferred_element_type=jnp.float32)


def _dot_nt(a, b):
    return lax.dot_general(a, b, (((1,), (1,)), ((), ())), preferred_element_type=jnp.float32)


def _rms(x, g):
    ms = jnp.mean(x * x, axis=-1, keepdims=True)
    return x * lax.rsqrt(ms + EPS) * g


def _lane_range(lo, hi, width=LANES):
    lane = lax.broadcasted_iota(jnp.int32, (1, width), 1)
    return (lane >= lo) & (lane < hi)


def _with_ones(v):
    lane = lax.broadcasted_iota(jnp.int32, (v.shape[0], LANES), 1)
    return jnp.concatenate([v, jnp.where(lane == 0, 1.0, 0.0).astype(jnp.bfloat16)], axis=1)


def _attend(scores, values, m=None):
    if m is None:
        m = functools.reduce(jnp.maximum, [jnp.max(s, axis=-1, keepdims=True) for s in scores])
    return functools.reduce(lambda a, b: a + b, [_dot(_bf(jnp.exp2(s - m)), v) for s, v in zip(scores, values)])


BOUND_SLACK = 1.02


def _row_norm(q):
    qf = q.astype(jnp.float32)
    return jnp.sqrt(jnp.sum(qf * qf, axis=-1, keepdims=True))


def _max_key_norm(k):
    kf = k.astype(jnp.float32)
    return jnp.sqrt(jnp.max(jnp.sum(kf * kf, axis=-1, keepdims=True), axis=0, keepdims=True))


def _max_key_norm_t(kt):
    kf = kt.astype(jnp.float32)
    return jnp.sqrt(jnp.max(jnp.sum(kf * kf, axis=0, keepdims=True), axis=1, keepdims=True))


def _normalised(o_ext):
    return o_ext[:, 0:LANES] * (1.0 / o_ext[:, LANES:LANES + 1])


def _swap8(x):
    lane = lax.broadcasted_iota(jnp.int32, (1, LANES), 1)
    return jnp.where((lane & 15) < 8, pltpu.roll(x, LANES - 8, 1), pltpu.roll(x, 8, 1))


def _rope(x, cos, sin):
    outs = []
    for c in range(x.shape[1] // LANES):
        sl = slice(LANES * c, LANES * (c + 1))
        xc = x[:, sl]
        outs.append(xc * cos[:, sl] + _swap8(xc) * sin[:, sl])
    return outs[0] if len(outs) == 1 else jnp.concatenate(outs, axis=1)


def _tile4(x):
    return x + pltpu.roll(x, 32, 1) + pltpu.roll(x, 64, 1) + pltpu.roll(x, 96, 1)


def _params(*sem):
    return pltpu.CompilerParams(dimension_semantics=sem, vmem_limit_bytes=VMEM_LIMIT)


ADA_TN = 1536


def _ada_kernel(m_ref, w_ref, b_ref, o_ref):
    m = m_ref[...]
    s = m * jax.nn.sigmoid(m)
    o_ref[0] = _dot(_bf(s), _bf(w_ref[0])) + b_ref[0]


def _ada(m, w_ada, b_ada):
    n = 6 * D_MODEL
    return pl.pallas_call(
        _ada_kernel,
        grid=(DEPTH, n // ADA_TN),
        in_specs=[
            pl.BlockSpec((N_MOD_ROWS, D_MODEL), lambda l, j: (0, 0)),
            pl.BlockSpec((1, D_MODEL, ADA_TN), lambda l, j: (l, 0, j)),
            pl.BlockSpec((1, 1, ADA_TN), lambda l, j: (l, 0, j)),
        ],
        out_specs=pl.BlockSpec((1, N_MOD_ROWS, ADA_TN), lambda l, j: (l, 0, j)),
        out_shape=jax.ShapeDtypeStruct((DEPTH, N_MOD_ROWS, n), jnp.float32),
        compiler_params=_params("parallel", "parallel"),
        name="ada",
    )(m, w_ada, b_ada.reshape(DEPTH, 1, n))


def _lam_kernel(lq1_ref, lk1_ref, lq2_ref, lk2_ref, init_ref, o_ref):
    init = init_ref[...]
    a = jnp.exp(jnp.sum(lq1_ref[...] * lk1_ref[...], axis=-1, keepdims=True))
    b = jnp.exp(jnp.sum(lq2_ref[...] * lk2_ref[...], axis=-1, keepdims=True))
    lam = a - b + init
    post = 1.0 - init
    for l in range(DEPTH):
        o_ref[l, 0:1, :] = jnp.broadcast_to(lam[l:l + 1], (1, LANES))
        o_ref[l, 1:2, :] = jnp.broadcast_to(post[l:l + 1], (1, LANES))


def _lam_consts(lq1, lk1, lq2, lk2):
    init = np.array([[0.8 - 0.6 * math.exp(-0.3 * l)] for l in range(DEPTH)], np.float32)
    return pl.pallas_call(
        _lam_kernel,
        out_shape=jax.ShapeDtypeStruct((DEPTH, 2, LANES), jnp.float32),
        name="diff_lambda",
    )(lq1, lk1, lq2, lk2, jnp.asarray(init))


N_DROW = 2 * NA_WIN_ROWS - 1
N_DCOL = 2 * NA_WIN_COLS - 1


def _bias_kernel(rpb_ref, o_ref):
    l = pl.program_id(0)
    h = pl.program_id(1)
    base = (l * NA_HEADS + h) * (N_DROW * N_DCOL)
    cq = lax.broadcasted_iota(jnp.int32, (GRID_W, LANES), 0)
    lane = lax.broadcasted_iota(jnp.int32, (GRID_W, LANES), 1)
    ck = lane & (GRID_W - 1)
    dcol = jnp.clip(ck - cq, -(NA_WIN_COLS - 1), NA_WIN_COLS - 1) + (NA_WIN_COLS - 1)
    hi = lane >= GRID_W
    for a in range(N_DROW - 1):
        acc = jnp.zeros((GRID_W, LANES), jnp.float32)
        for j in range(N_DCOL):
            lo_v = rpb_ref[base + a * N_DCOL + j]
            hi_v = rpb_ref[base + (a + 1) * N_DCOL + j]
            acc = jnp.where(dcol == j, jnp.where(hi, hi_v, lo_v), acc)
        o_ref[0, 0, a] = acc * LOG2E


def _bias_tiles(na_rpb):
    return pl.pallas_call(
        _bias_kernel,
        grid=(DEPTH, NA_HEADS),
        in_specs=[pl.BlockSpec(memory_space=pltpu.SMEM)],
        out_specs=pl.BlockSpec((1, 1, N_DROW - 1, GRID_W, LANES), lambda l, h: (l, h, 0, 0, 0)),
        out_shape=jax.ShapeDtypeStruct((DEPTH, NA_HEADS, N_DROW - 1, GRID_W, LANES), jnp.float32),
        compiler_params=_params("parallel", "parallel"),
        name="na_bias_tiles",
    )(na_rpb.reshape(-1))


def _row_group(i):
    return jnp.where(i < TILES_PROMPT, 0, 1 + (i - TILES_PROMPT) // (DEC_SEQ // TM))


def _split_specs(width, first):
    return [pl.BlockSpec((TM, width), lambda i: (jnp.minimum(first + i, TILES_PROMPT - 1), 0)),
            pl.BlockSpec((TM, width), lambda i: (jnp.maximum(first + i - TILES_PROMPT, 0), 0))]


def _x_specs(split, first):
    if not split:
        return [pl.BlockSpec((TM, D_MODEL), lambda i: (first + i, 0))]
    return _split_specs(D_MODEL, first)


def _read_tile(refs, first):
    if len(refs) == 1:
        return refs[0][...]
    return jnp.where(first + pl.program_id(0) < TILES_PROMPT, refs[0][...], refs[1][...])


IN_COLS = 2592
IN_QC, IN_CKV, IN_D = 1536, 1920, 2080
TR_ROWS = 256


def _gelu_tanh(x):
    return 0.5 * x * (1.0 + jnp.tanh(math.sqrt(2.0 / math.pi) * (x + 0.044715 * (x * x * x))))


def _sgu(pd, g, w_ref, bt):
    u = _gelu_tanh(pd[:, 0:WIDTH])
    v = _gelu_tanh(pd[:, WIDTH:2 * WIDTH])
    grp = lax.broadcasted_iota(jnp.int32, (1, WIDTH), 1) // SGU_GROUP_DIM
    v2 = v * v
    ms = jnp.zeros_like(v)
    for gi in range(SGU_GROUPS):
        sel = grp == gi
        tot = jnp.sum(jnp.where(sel, v2, 0.0), axis=-1, keepdims=True)
        ms = jnp.where(sel, tot * (1.0 / SGU_GROUP_DIM), ms)
    vg = _bf(v * lax.rsqrt(ms + EPS) * g)
    outs = []
    for c in range(pd.shape[0] // SGU_CHUNK):
        rows = slice(SGU_CHUNK * c, SGU_CHUNK * (c + 1))
        mixed = jnp.zeros((SGU_CHUNK, WIDTH), jnp.float32)
        for gi in range(SGU_GROUPS):
            full = _dot(_bf(w_ref[0, gi]), vg[rows]) + bt[:, gi:gi + 1]
            mixed = jnp.where(grp == gi, full, mixed)
        outs.append(u[rows] * mixed)
    return jnp.concatenate(outs, axis=0)


def _w_in_row_pieces():
    qn = [(IN_QC + 96 * h, MLA_NOPE) for h in range(MLA_HEADS)]
    qp = [(IN_QC + 96 * h + MLA_NOPE, MLA_ROPE) for h in range(MLA_HEADS)]
    seg_c = qn + qp + [(IN_CKV, MLA_KV_RANK + MLA_ROPE)]
    return (0, SEG_A + SEG_B), seg_c, (IN_D, SEG_D)


def _load_w_in(wt_ref, w_scr):
    ab, seg_c, d = _w_in_row_pieces()
    c_rows = jnp.concatenate([wt_ref[0, s:s + n, :] for s, n in seg_c]
                             + [jnp.zeros((SEG_C_PAD, D_MODEL), jnp.float32)], axis=0)
    for t in range(SEG_C // LANES):
        w_scr[:, SEG_A + SEG_B + LANES * t:SEG_A + SEG_B + LANES * (t + 1)] = _bf(c_rows[LANES * t:LANES * (t + 1)].T)
    for (src, n), dst in ((ab, 0), (d, SEG_A + SEG_B + SEG_C)):
        for t in range(n // TR_ROWS):
            rows = wt_ref[0, src + TR_ROWS * t:src + TR_ROWS * (t + 1), :]
            w_scr[:, dst + TR_ROWS * t:dst + TR_ROWS * (t + 1)] = _bf(rows.T)


def _inproj_kernel(n_x, *refs):
    x_refs = refs[:n_x]
    (g_ref, sh_ref, sc_ref, wt_ref, sg_ref, sw_ref, sbt_ref, pa_ref, pb_ref, pc_ref, od_ref, w_scr) = refs[n_x:]

    @pl.when(pl.program_id(0) == 0)
    def _():
        _load_w_in(wt_ref, w_scr)

    h = _rms(_read_tile(x_refs, 0), g_ref[0]) * (1.0 + sc_ref[...]) + sh_ref[...]
    hb = _bf(h)
    off = 0
    for ref in (pa_ref, pb_ref, pc_ref):
        n = ref.shape[1]
        ref[...] = _dot(hb, w_scr[:, off:off + n])
        off += n
    od_ref[...] = _sgu(_dot(hb, w_scr[:, off:off + SEG_D]), sg_ref[0], sw_ref, sbt_ref[0])


def _inproj(l, xs, g_mix, mod, w_in_t, sgu_g, sgu_w, sgu_bt):
    def mod_spec(j):
        return pl.BlockSpec((None, None, 1, D_MODEL), lambda i: (l, _row_group(i), 0, j))

    widths = (SEG_A, SEG_B, SEG_C, WIDTH)
    return pl.pallas_call(
        functools.partial(_inproj_kernel, len(xs)),
        grid=(N_TOK // TM,),
        in_specs=_x_specs(len(xs) == 2, 0) + [
            pl.BlockSpec((1, 1, D_MODEL), lambda i: (l, 0, 0)),
            mod_spec(0), mod_spec(1),
            pl.BlockSpec((1, IN_COLS, D_MODEL), lambda i: (l, 0, 0), pipeline_mode=pl.Buffered(1)),
            pl.BlockSpec((1, 1, WIDTH), lambda i: (l, 0, 0)),
            pl.BlockSpec((1, SGU_GROUPS, SGU_CHUNK, SGU_CHUNK), lambda i: (l, 0, 0, 0)),
            pl.BlockSpec((1, SGU_CHUNK, SGU_GROUPS), lambda i: (l, 0, 0)),
        ],
        out_specs=[pl.BlockSpec((TM, n), lambda i: (i, 0)) for n in widths],
        out_shape=[jax.ShapeDtypeStruct((N_TOK, n), jnp.float32) for n in widths],
        scratch_shapes=[pltpu.VMEM((D_MODEL, IN_COLS_P), jnp.bfloat16)],
        compiler_params=_params("arbitrary"),
        name="inproj",
    )(*xs, g_mix, mod, mod, w_in_t, sgu_g, sgu_w, sgu_bt)


C_QN, C_QP, C_CKV, C_KPE = 0, 256, 384, 512


def _stack_heads(qp):
    lo = _lane_range(0, 64)
    return jnp.concatenate([_bf(jnp.where(lo, qp, 0.0)), _bf(jnp.where(lo, 0.0, qp))], axis=0)


def _unstack_heads(o, n):
    return jnp.where(_lane_range(0, 64), o[0:n], o[n:2 * n])


def _pair_t(c_ref, j):
    return jnp.concatenate([c_ref[2 * j], c_ref[2 * j + 1]], axis=0)


def _stack_components(qp):
    return jnp.concatenate([_bf(jnp.where(_lane_range(32 * t, 32 * (t + 1)), qp, 0.0)) for t in range(4)], axis=0)


def _diff_finish(o, n, lam, post, g2):
    den = o[:, LANES:LANES + 1]
    outs = []
    for t in range(2):
        p1 = o[2 * t * n:(2 * t + 1) * n, 0:LANES] * (1.0 / den[2 * t * n:(2 * t + 1) * n])
        p2 = o[(2 * t + 1) * n:(2 * t + 2) * n, 0:LANES] * (lam / den[(2 * t + 1) * n:(2 * t + 2) * n])
        d = p1 - p2
        own = _lane_range(64 * t, 64 * (t + 1))
        ms = jnp.sum(jnp.where(own, d * d, 0.0), axis=-1, keepdims=True) * (1.0 / DIFF_V_DIM)
        outs.append(d * lax.rsqrt(ms + EPS))
    return jnp.where(_lane_range(0, 64), outs[0], outs[1]) * g2 * post


def _mla_queries(qn_pair, qp_all, j):
    halves = []
    for t in range(2):
        h = 2 * j + t
        halves.append(jnp.concatenate([
            _bf(jnp.where(_lane_range(64 * t, 64 * (t + 1)), qn_pair, 0.0)),
            _bf(jnp.where(_lane_range(MLA_ROPE * h, MLA_ROPE * (h + 1)), qp_all, 0.0))], axis=1))
    return jnp.concatenate(halves, axis=0)


def _write_heads_t(p_ref, rows, col0, out_ref, bb):
    xt = p_ref[rows, col0:col0 + WIDTH].T
    for h in range(N_HEADS):
        out_ref[bb, 0, h] = xt[64 * h:64 * (h + 1)]
    _clear_other_layers(out_ref, bb)


def _clear_other_layers(out_ref, bb):
    if out_ref.shape[1] > 1:
        out_ref[bb, 1:] = jnp.zeros(out_ref.shape[1:], jnp.float32)[1:]


def _mix_prompt_kernel(n_prev, *refs):
    pa_ref, pb_ref, pc_ref, cst_ref, gsub_ref, gckv_ref, wuk_ref, wuv_ref = refs[:8]
    o_ref, nak_ref, nav_ref, dk_ref, dv_ref, ckv_ref, kpe_ref = refs[8 + n_prev:]
    c_a = HEAD_DIM ** -0.5 * LOG2E
    c_b = DIFF_QK_DIM ** -0.5 * LOG2E
    c_c = (MLA_NOPE + MLA_ROPE) ** -0.5 * LOG2E
    lam = cst_ref[0, 0:1, 0:1]
    post = cst_ref[0, 1:2, 0:1]
    wuk, wuv = _bf(wuk_ref[0]), _bf(wuv_ref[0])
    for bb in range(PB):
        rows = slice(SEQ * bb, SEQ * (bb + 1))
        for j in range(N_PAIRS):
            cols = slice(LANES * j, LANES * (j + 1))
            k = _bf(pa_ref[rows, WIDTH + LANES * j:WIDTH + LANES * (j + 1)])
            v = _with_ones(_bf(pa_ref[rows, 2 * WIDTH + LANES * j:2 * WIDTH + LANES * (j + 1)]))
            qs = _stack_heads(pa_ref[rows, cols] * c_a)
            o = _attend([_dot_nt(qs, k)], [v], _row_norm(qs) * (_max_key_norm(k) * BOUND_SLACK))
            o_ref[rows, cols] = _unstack_heads(_normalised(o), SEQ)
        _write_heads_t(pa_ref, rows, WIDTH, nak_ref, bb)
        _write_heads_t(pa_ref, rows, 2 * WIDTH, nav_ref, bb)
        for j in range(N_PAIRS):
            cols = slice(LANES * j, LANES * (j + 1))
            k = _bf(pb_ref[rows, WIDTH + LANES * j:WIDTH + LANES * (j + 1)])
            v = _with_ones(_bf(pb_ref[rows, 2 * WIDTH + LANES * j:2 * WIDTH + LANES * (j + 1)]))
            qs = _stack_components(pb_ref[rows, cols] * c_b)
            o = _attend([_dot_nt(qs, k)], [v], _row_norm(qs) * (_max_key_norm(k) * BOUND_SLACK))
            o_ref[rows, WIDTH + LANES * j:WIDTH + LANES * (j + 1)] = _diff_finish(o, SEQ, lam, post, gsub_ref[0])
        _write_heads_t(pb_ref, rows, WIDTH, dk_ref, bb)
        _write_heads_t(pb_ref, rows, 2 * WIDTH, dv_ref, bb)
        ckv = _rms(pc_ref[rows, C_CKV:C_CKV + MLA_KV_RANK], gckv_ref[0])
        ckv_ref[bb, 0] = ckv
        _clear_other_layers(ckv_ref, bb)
        kpe_slot = pc_ref[rows, C_KPE:C_KPE + LANES]
        kpe_ref[bb, 0] = kpe_slot.T[0:MLA_ROPE]
        _clear_other_layers(kpe_ref, bb)
        ckv_b = _bf(ckv)
        kn = _bf(_dot(ckv_b, wuk))
        vv = _bf(_dot(ckv_b, wuv))
        kpe4 = _bf(_tile4(kpe_slot))
        qn = pc_ref[rows, C_QN:C_QN + WIDTH] * c_c
        qp = pc_ref[rows, C_QP:C_QP + LANES] * c_c
        for j in range(N_PAIRS):
            cols = slice(LANES * j, LANES * (j + 1))
            k = jnp.concatenate([kn[:, cols], kpe4], axis=1)
            qs = _mla_queries(qn[:, cols], qp, j)
            o = _attend([_dot_nt(qs, k)], [_with_ones(vv[:, cols])], _row_norm(qs) * (_max_key_norm(k) * BOUND_SLACK))
            o_ref[rows, 2 * WIDTH + LANES * j:2 * WIDTH + LANES * (j + 1)] = _unstack_heads(_normalised(o), SEQ)


def _mix_prompt(l, pa, pb, pc, cst, g_sub2, g_ckv, w_uk, w_uv, prev):
    n_prev = len(prev)
    tails = [(NA_HEADS, HEAD_DIM, SEQ)] * 2 + [(DIFF_HEADS, 64, SEQ)] * 2 + [(SEQ, MLA_KV_RANK), (MLA_ROPE, SEQ)]

    def cache_spec(tail):
        if l == 0:
            return pl.BlockSpec((PB, DEPTH) + tail, lambda b: (b, 0) + (0,) * len(tail))
        return pl.BlockSpec((PB, 1) + tail, lambda b: (b, l) + (0,) * len(tail))

    def rows(width):
        return pl.BlockSpec((PB * SEQ, width), lambda b: (b, 0))

    def layer(*tail):
        return pl.BlockSpec((1,) + tail, lambda b: (l,) + (0,) * len(tail))

    return pl.pallas_call(
        functools.partial(_mix_prompt_kernel, n_prev),
        grid=(BATCH // PB,),
        in_specs=[rows(SEG_A), rows(SEG_B), rows(SEG_C), layer(2, LANES), layer(1, LANES), layer(1, MLA_KV_RANK),
                  layer(MLA_KV_RANK, WIDTH), layer(MLA_KV_RANK, WIDTH)] + [pl.BlockSpec(memory_space=pl.ANY)] * n_prev,
        out_specs=[rows(O_ATT)] + [cache_spec(t) for t in tails],
        out_shape=[jax.ShapeDtypeStruct((N_PROMPT, O_ATT), jnp.float32)]
        + [jax.ShapeDtypeStruct((BATCH, DEPTH) + t, jnp.float32) for t in tails],
        input_output_aliases={8 + i: 1 + i for i in range(n_prev)},
        compiler_params=_params("parallel"),
        name="mix_prompt",
    )(pa, pb, pc, cst, g_sub2, g_ckv, w_uk, w_uv, *prev)


def _na_row_groups():
    kh = min(NA_WIN_ROWS, GRID_ROWS)
    r0s = [min(max(r - kh // 2, 0), GRID_ROWS - kh) for r in range(GRID_ROWS)]
    groups = []
    for r, r0 in enumerate(r0s):
        if groups and groups[-1][2] == r0:
            groups[-1][1] = r
        else:
            groups.append([r, r, r0])
    return kh, [tuple(g) for g in groups]


def _na_sample(pa_ref, ck_ref, cv_ref, tt_ref, o_ref):
    c = HEAD_DIM ** -0.5 * LOG2E
    kh, groups = _na_row_groups()
    lk = kh * GRID_W

    def in_window(n):
        cq = lax.broadcasted_iota(jnp.int32, (n, lk), 0) & (GRID_W - 1)
        ck = lax.broadcasted_iota(jnp.int32, (n, lk), 1) & (GRID_W - 1)
        c0 = jnp.clip(cq - NA_WIN_COLS // 2, 0, GRID_W - NA_WIN_COLS)
        return (ck >= c0) & (ck < c0 + NA_WIN_COLS)

    for j in range(N_PAIRS):
        cols = slice(LANES * j, LANES * (j + 1))
        q = _stack_heads(pa_ref[:, cols] * c)
        k = _bf(pa_ref[:, WIDTH + LANES * j:WIDTH + LANES * (j + 1)])
        v = _with_ones(_bf(pa_ref[:, 2 * WIDTH + LANES * j:2 * WIDTH + LANES * (j + 1)]))
        kc_t = _bf(_pair_t(ck_ref, j))
        vc = _with_ones(_bf(_pair_t(cv_ref, j).T))
        kmax = jnp.maximum(_max_key_norm(k), _max_key_norm_t(kc_t)) * BOUND_SLACK
        tmax = functools.reduce(jnp.maximum, [tt_ref[2 * j + t, a] for t in range(2) for a in range(N_DROW - 1)])
        bplus = jnp.maximum(jnp.max(jnp.max(tmax, axis=-1, keepdims=True), axis=0, keepdims=True), 0.0)
        for (r_lo, r_hi, r0) in groups:
            n = (r_hi - r_lo + 1) * GRID_W
            rows = slice(r_lo * GRID_W, r_lo * GRID_W + n)
            keys = slice(r0 * GRID_W, r0 * GRID_W + lk)
            qg = jnp.concatenate([q[rows], q[DEC_SEQ + r_lo * GRID_W:DEC_SEQ + r_lo * GRID_W + n]], axis=0)
            bias = jnp.concatenate([
                jnp.concatenate([tt_ref[2 * j + t, r0 + 2 * i - r + NA_WIN_ROWS - 1] for i in range(kh // 2)], axis=1)
                for t in range(2) for r in range(r_lo, r_hi + 1)], axis=0)
            s_loc = jnp.where(in_window(2 * n), _dot_nt(qg, k[keys]) + bias, NEG_INF)
            o = _attend([_dot(qg, kc_t), s_loc], [vc, v[keys]], _row_norm(qg) * kmax + bplus)
            o_ref[rows, cols] = _unstack_heads(_normalised(o), n)


def _diff_sample(pb_ref, ck_ref, cv_ref, cos_ref, sin_ref, cst_ref, g_ref, o_ref):
    c = DIFF_QK_DIM ** -0.5 * LOG2E
    lam = cst_ref[0, 0:1, 0:1]
    post = cst_ref[0, 1:2, 0:1]
    cos = cos_ref[...]
    sin = sin_ref[...]
    q = _rope(pb_ref[:, 0:WIDTH], cos, sin) * c
    k_new = _bf(_rope(pb_ref[:, WIDTH:2 * WIDTH], cos, sin))
    for j in range(N_PAIRS):
        cols = slice(LANES * j, LANES * (j + 1))
        kc_t = _bf(_pair_t(ck_ref, j))
        vc = _with_ones(_bf(_pair_t(cv_ref, j).T))
        v = _with_ones(_bf(pb_ref[:, 2 * WIDTH + LANES * j:2 * WIDTH + LANES * (j + 1)]))
        kmax = jnp.maximum(_max_key_norm(k_new[:, cols]), _max_key_norm_t(kc_t)) * BOUND_SLACK
        for qi in range(DEC_SEQ // QB):
            rows = slice(QB * qi, QB * (qi + 1))
            qs = _stack_components(q[rows, cols])
            o = _attend([_dot(qs, kc_t), _dot_nt(qs, k_new[:, cols])], [vc, v], _row_norm(qs) * kmax)
            o_ref[rows, WIDTH + LANES * j:WIDTH + LANES * (j + 1)] = _diff_finish(o, QB, lam, post, g_ref[0])


def _mla_sample(pc_ref, cckv_ref, ckpe_ref, cosq_ref, sinq_ref, cosk_ref, sink_ref, gckv_ref, wuk_ref, wuv_ref, o_ref):
    c = (MLA_NOPE + MLA_ROPE) ** -0.5 * LOG2E
    wuk, wuv = _bf(wuk_ref[0]), _bf(wuv_ref[0])
    ckv_new = _bf(_rms(pc_ref[:, C_CKV:C_CKV + MLA_KV_RANK], gckv_ref[0]))
    ckv_old = _bf(cckv_ref[...])
    kpe_new = _bf(_tile4(_rope(pc_ref[:, C_KPE:C_KPE + LANES], cosk_ref[...], sink_ref[...])))
    kpe_old = _bf(jnp.concatenate([ckpe_ref[...]] * MLA_HEADS, axis=0).T)
    kn_new, kn_old = _bf(_dot(ckv_new, wuk)), _bf(_dot(ckv_old, wuk))
    v_new, v_old = _bf(_dot(ckv_new, wuv)), _bf(_dot(ckv_old, wuv))
    qn = pc_ref[:, C_QN:C_QN + WIDTH] * c
    qp = _rope(pc_ref[:, C_QP:C_QP + LANES], cosq_ref[...], sinq_ref[...]) * c
    for j in range(N_PAIRS):
        cols = slice(LANES * j, LANES * (j + 1))
        k_old = jnp.concatenate([kn_old[:, cols], kpe_old], axis=1)
        k_new = jnp.concatenate([kn_new[:, cols], kpe_new], axis=1)
        vo, vn = _with_ones(v_old[:, cols]), _with_ones(v_new[:, cols])
        kmax = jnp.maximum(_max_key_norm(k_old), _max_key_norm(k_new)) * BOUND_SLACK
        for qi in range(DEC_SEQ // QB):
            rows = slice(QB * qi, QB * (qi + 1))
            qs = _mla_queries(qn[rows, cols], qp[rows], j)
            o = _attend([_dot_nt(qs, k_old), _dot_nt(qs, k_new)], [vo, vn], _row_norm(qs) * kmax)
            o_ref[rows, 2 * WIDTH + LANES * j:2 * WIDTH + LANES * (j + 1)] = _unstack_heads(_normalised(o), QB)


def _mix_sample_kernel(pa_ref, pb_ref, pc_ref, cnak_ref, cnav_ref, cdk_ref, cdv_ref, cckv_ref, ckpe_ref, tt_ref,
                       cosb_ref, sinb_ref, cosq_ref, sinq_ref, cosk_ref, sink_ref,
                       cst_ref, gsub_ref, gckv_ref, wuk_ref, wuv_ref, o_ref):
    _na_sample(pa_ref, cnak_ref, cnav_ref, tt_ref, o_ref)
    _diff_sample(pb_ref, cdk_ref, cdv_ref, cosb_ref, sinb_ref, cst_ref, gsub_ref, o_ref)
    _mla_sample(pc_ref, cckv_ref, ckpe_ref, cosq_ref, sinq_ref, cosk_ref, sink_ref, gckv_ref, wuk_ref, wuv_ref, o_ref)


def _mix_sample(l, pa, pb, pc, caches_t, tt, tables, cst, g_sub2, g_ckv, w_uk, w_uv):
    first = N_PROMPT // DEC_SEQ

    def rows(width):
        return pl.BlockSpec((DEC_SEQ, width), lambda b: (first + b, 0))

    def cache(*tail):
        return pl.BlockSpec((None, None) + tail, lambda b: (b, l) + (0,) * len(tail))

    def layer(*tail):
        return pl.BlockSpec((1,) + tail, lambda b: (l,) + (0,) * len(tail))

    def table(width):
        return pl.BlockSpec((DEC_SEQ, width), lambda b: (0, 0))

    kv_t = cache(N_HEADS, 64, PAST_LEN)
    return pl.pallas_call(
        _mix_sample_kernel,
        grid=(DEC_BATCH,),
        in_specs=[rows(SEG_A), rows(SEG_B), rows(SEG_C), kv_t, kv_t, kv_t, kv_t,
                  cache(PAST_LEN, MLA_KV_RANK), cache(MLA_ROPE, PAST_LEN),
                  pl.BlockSpec((None, NA_HEADS, N_DROW - 1, GRID_W, LANES), lambda b: (l, 0, 0, 0, 0)),
                  table(WIDTH), table(WIDTH), table(LANES), table(LANES), table(LANES), table(LANES),
                  layer(2, LANES), layer(1, LANES), layer(1, MLA_KV_RANK),
                  layer(MLA_KV_RANK, WIDTH), layer(MLA_KV_RANK, WIDTH)],
        out_specs=pl.BlockSpec((DEC_SEQ, O_ATT), lambda b: (b, 0)),
        out_shape=jax.ShapeDtypeStruct((N_SAMPLE, O_ATT), jnp.float32),
        compiler_params=_params("parallel"),
        name="mix_sample",
    )(pa, pb, pc, *caches_t, tt, *tables, cst, g_sub2, g_ckv, w_uk, w_uv)


FF_CHUNK = 1024


def _outffn_kernel(n_x, first, final, *refs):
    x_refs, o_refs = refs[:n_x], refs[n_x:n_x + 2]
    (od_ref, wout_ref, g1_ref, gffn_ref, sh2_ref, sc2_ref, g2_ref, w1_ref, w2_ref, gfin_ref, y_ref) = refs[n_x + 2:]
    acc = (_dot(_bf(_read_tile(o_refs, first)), wout_ref[0, 0:O_ATT, :])
           + _dot(_bf(od_ref[...]), wout_ref[0, O_ATT:O_ATT + WIDTH, :]))
    x1 = _read_tile(x_refs, first) + g1_ref[...] * acc
    hf = _bf(_rms(x1, gffn_ref[0]) * (1.0 + sc2_ref[...]) + sh2_ref[...])
    acc = jnp.zeros((TM, D_MODEL), jnp.float32)
    for c in range(D_FF // FF_CHUNK):
        cols = slice(FF_CHUNK * c, FF_CHUNK * (c + 1))
        a = jnp.square(jnp.maximum(_dot(hf, w1_ref[0, :, cols]), 0.0))
        acc += _dot(_bf(a), w2_ref[0, cols, :])
    y = x1 + g2_ref[...] * acc
    if final:
        y = _rms(y, gfin_ref[...])
    y_ref[...] = y


def _outffn(l, xs, o_p, o_s, od, w_out, g_ffn, mod, w1, w2, g_final, first, n_tiles):
    def mod_spec(j):
        return pl.BlockSpec((None, None, 1, D_MODEL), lambda i: (l, _row_group(first + i), 0, j))

    def resident(shape):
        return pl.BlockSpec(shape, lambda i: (l,) + (0,) * (len(shape) - 1), pipeline_mode=pl.Buffered(1))

    return pl.pallas_call(
        functools.partial(_outffn_kernel, len(xs), first, l == DEPTH - 1),
        grid=(n_tiles,),
        in_specs=_x_specs(len(xs) == 2, first) + _split_specs(O_ATT, first) + [
            pl.BlockSpec((TM, WIDTH), lambda i: (first + i, 0)),
            resident((1, 4 * WIDTH, D_MODEL)),
            mod_spec(2),
            pl.BlockSpec((1, 1, D_MODEL), lambda i: (l, 0, 0)),
            mod_spec(3), mod_spec(4), mod_spec(5),
            resident((1, D_MODEL, D_FF)),
            resident((1, D_FF, D_MODEL)),
            pl.BlockSpec((1, D_MODEL), lambda i: (0, 0)),
        ],
        out_specs=pl.BlockSpec((TM, D_MODEL), lambda i: (i, 0)),
        out_shape=jax.ShapeDtypeStruct((n_tiles * TM, D_MODEL), jnp.float32),
        compiler_params=_params("parallel"),
        name="outffn",
    )(*xs, o_p, o_s, od, w_out, mod, g_ffn, mod, mod, mod, w1, w2, g_final)


def _rope32_tables():
    t = np.arange(DEC_SEQ)
    rows, cols = (t // GRID_W).astype(np.float64), (t % GRID_W).astype(np.float64)
    half = 8
    freqs = ROPE_BASE ** (-np.arange(half, dtype=np.float64) / half)
    cos, sin = [], []
    for pos in (rows, cols):
        ang = pos[:, None] * freqs[None, :]
        cos += [np.cos(ang), np.cos(ang)]
        sin += [-np.sin(ang), np.sin(ang)]
    return np.concatenate(cos, axis=1).astype(np.float32), np.concatenate(sin, axis=1).astype(np.float32)


def _rope_tables():
    c32, s32 = _rope32_tables()
    tile = lambda a, n: np.tile(a, (1, n))
    pad = np.zeros((DEC_SEQ, 96), np.float32)
    cos_k = np.concatenate([c32, pad + 1.0], axis=1)
    sin_k = np.concatenate([s32, pad], axis=1)
    return (tile(c32, 8), tile(s32, 8),
            tile(c32, 4), tile(s32, 4),
            cos_k, sin_k)


def kernel(x_prompt, x_sample, cache_na_k, cache_na_v, cache_diff_k, cache_diff_v, cache_mla_ckv, cache_mla_kpe, c, c_ctx, w_ada, b_ada, g_mix, g_ffn, w_in, w_out, na_rpb, diff_lq1, diff_lk1, diff_lq2, diff_lk2, diff_g_subln, mla_g_ckv, mla_w_uk, mla_w_uv, sgu_g, sgu_w, sgu_b, w_ff1, w_ff2, g_final):
    f32 = jnp.float32
    m = jnp.concatenate([c_ctx[None, :], c, jnp.zeros((N_MOD_ROWS - 1 - DEC_BATCH, D_MODEL), f32)], axis=0)
    mod = _ada(m, w_ada, b_ada).reshape(DEPTH, N_MOD_ROWS, 1, 6 * D_MODEL)
    cst = _lam_consts(diff_lq1, diff_lk1, diff_lq2, diff_lk2)
    tt = _bias_tiles(na_rpb)
    tables = [jnp.asarray(t) for t in _rope_tables()]

    t_last = lambda a: jnp.swapaxes(a, -1, -2)
    w_in_t = t_last(w_in)
    caches_t = (t_last(cache_na_k), t_last(cache_na_v), t_last(cache_diff_k), t_last(cache_diff_v),
                cache_mla_ckv, t_last(cache_mla_kpe))
    w_out_b, w1_b, w2_b = _bf(w_out), _bf(w_ff1), _bf(w_ff2)
    g_mix3 = g_mix.reshape(DEPTH, 1, D_MODEL)
    g_ffn3 = g_ffn.reshape(DEPTH, 1, D_MODEL)
    g_sub2 = jnp.tile(diff_g_subln, (1, 2)).reshape(DEPTH, 1, LANES)
    g_ckv3 = mla_g_ckv.reshape(DEPTH, 1, MLA_KV_RANK)
    sgu_g3 = sgu_g.reshape(DEPTH, 1, WIDTH)
    sgu_bt = sgu_b.transpose(0, 2, 1)
    g_fin2 = g_final.reshape(1, D_MODEL)

    xs = (x_prompt.reshape(N_PROMPT, D_MODEL), x_sample.reshape(N_SAMPLE, D_MODEL))
    new = ()
    for l in range(DEPTH):
        pa, pb, pc, od = _inproj(l, xs, g_mix3, mod, w_in_t, sgu_g3, sgu_w, sgu_bt)
        o_p, *new = _mix_prompt(l, pa, pb, pc, cst, g_sub2, g_ckv3, mla_w_uk, mla_w_uv, new)
        o_s = _mix_sample(l, pa, pb, pc, caches_t, tt, tables, cst, g_sub2, g_ckv3, mla_w_uk, mla_w_uv)
        ffn = functools.partial(_outffn, l, xs, o_p, o_s, od, w_out_b, g_ffn3, mod, w1_b, w2_b, g_fin2)
        if l < DEPTH - 1:
            xs = (ffn(0, TILES_PROMPT + TILES_SAMPLE),)
        else:
            xs = (ffn(0, TILES_PROMPT), ffn(TILES_PROMPT, TILES_SAMPLE))
    y_prompt = xs[0].reshape(BATCH, SEQ, D_MODEL)
    y_sample = xs[1].reshape(DEC_BATCH, DEC_SEQ, D_MODEL)
    na_k, na_v, diff_k, diff_v, mla_ckv, mla_kpe = new
    return (y_prompt, y_sample, t_last(na_k), t_last(na_v), t_last(diff_k), t_last(diff_v), mla_ckv, t_last(mla_kpe))
```

```python
import functools
import math

import numpy as np
import jax
import jax.numpy as jnp
from jax import lax
from jax.experimental import pallas as pl
from jax.experimental.pallas import tpu as pltpu

D_MODEL = 1024
BATCH = 16
SEQ = 256
DEPTH = 4
DEC_BATCH = 2
DEC_SEQ = 1024
PAST_LEN = 512
GRID_W = 64
GRID_ROWS = DEC_SEQ // GRID_W
HEAD_DIM = 64
NA_HEADS = 4
NA_WIN_ROWS = 8
NA_WIN_COLS = 16
DIFF_HEADS = 4
DIFF_QK_DIM = 32
DIFF_V_DIM = 64
MLA_HEADS = 4
MLA_NOPE = 64
MLA_ROPE = 32
MLA_V = 64
MLA_KV_RANK = 128
SGU_GROUPS = 4
SGU_GROUP_DIM = 64
SGU_CHUNK = 128
D_FF = 4 * D_MODEL
ROPE_BASE = 10000.0
EPS = 1e-6
NEG_INF = -1e30
LOG2E = 1.4426950408889634

N_HEADS = 4
N_PAIRS = N_HEADS // 2
LANES = 128
WIDTH = 256
N_PROMPT = BATCH * SEQ
N_SAMPLE = DEC_BATCH * DEC_SEQ
N_TOK = N_PROMPT + N_SAMPLE
N_MOD_ROWS = 8

SEG_A = 3 * WIDTH
SEG_B = 3 * WIDTH
SEG_C = 640
SEG_D = 2 * WIDTH
SEG_C_PAD = 96
IN_COLS_P = SEG_A + SEG_B + SEG_C + SEG_D
O_ATT = 3 * WIDTH

TM = 512
TILES_PROMPT = N_PROMPT // TM
TILES_SAMPLE = N_SAMPLE // TM
PB = 2
QB = 256
VMEM_LIMIT = 56 * 1024 * 1024


def _bf(x):
    return x.astype(jnp.bfloat16)


def _dot(a, b):
    return jnp.dot(a, b, preferred_element_type=jnp.float32)


def _dot_nt(a, b):
    return lax.dot_general(a, b, (((1,), (1,)), ((), ())), preferred_element_type=jnp.float32)


def _rms(x, g):
    ms = jnp.mean(x * x, axis=-1, keepdims=True)
    return x * lax.rsqrt(ms + EPS) * g


def _lane_range(lo, hi, width=LANES):
    lane = lax.broadcasted_iota(jnp.int32, (1, width), 1)
    return (lane >= lo) & (lane < hi)


def _with_ones(v):
    lane = lax.broadcasted_iota(jnp.int32, (v.shape[0], LANES), 1)
    return jnp.concatenate([v, jnp.where(lane == 0, 1.0, 0.0).astype(jnp.bfloat16)], axis=1)


def _attend(scores, values, m=None):
    if m is None:
        m = functools.reduce(jnp.maximum, [jnp.max(s, axis=-1, keepdims=True) for s in scores])
    return functools.reduce(lambda a, b: a + b, [_dot(_bf(jnp.exp2(s - m)), v) for s, v in zip(scores, values)])


BOUND_SLACK = 1.02
SHIFT_LIMIT = 50.0


class _ShiftLog:
    def __init__(self, worst=None):
        self.worst = jnp.zeros((1, 1), jnp.float32) if worst is None else worst

    def unsafe(self):
        return jnp.logical_not(self.worst[0, 0] < SHIFT_LIMIT)


def _shift(log, qs, kmax, extra=0.0):
    if log is None:
        return None
    m = _row_norm(qs) * kmax + extra
    log.worst = jnp.maximum(log.worst, jnp.max(m, axis=0, keepdims=True))
    return m


def _loop(n, log, body):
    def step(i, worst):
        inner = None if log is None else _ShiftLog(worst)
        body(i, inner)
        return worst if inner is None else inner.worst

    worst = lax.fori_loop(0, n, step, jnp.zeros((1, 1), jnp.float32) if log is None else log.worst)
    if log is not None:
        log.worst = worst


def _row_norm(q):
    qf = q.astype(jnp.float32)
    return jnp.sqrt(jnp.sum(qf * qf, axis=-1, keepdims=True))


def _max_key_norm(k):
    kf = k.astype(jnp.float32)
    return jnp.sqrt(jnp.max(jnp.sum(kf * kf, axis=-1, keepdims=True), axis=0, keepdims=True))


def _max_key_norm_t(kt):
    kf = kt.astype(jnp.float32)
    return jnp.sqrt(jnp.max(jnp.sum(kf * kf, axis=0, keepdims=True), axis=1, keepdims=True))


def _normalised(o_ext):
    return o_ext[:, 0:LANES] * (1.0 / o_ext[:, LANES:LANES + 1])


def _swap8(x):
    lane = lax.broadcasted_iota(jnp.int32, (1, LANES), 1)
    return jnp.where((lane & 15) < 8, pltpu.roll(x, LANES - 8, 1), pltpu.roll(x, 8, 1))


def _rope(x, cos, sin):
    outs = []
    for c in range(x.shape[1] // LANES):
        sl = slice(LANES * c, LANES * (c + 1))
        xc = x[:, sl]
        outs.append(xc * cos[:, sl] + _swap8(xc) * sin[:, sl])
    return outs[0] if len(outs) == 1 else jnp.concatenate(outs, axis=1)


def _tile4(x):
    return x + pltpu.roll(x, 32, 1) + pltpu.roll(x, 64, 1) + pltpu.roll(x, 96, 1)


def _params(*sem):
    return pltpu.CompilerParams(dimension_semantics=sem, vmem_limit_bytes=VMEM_LIMIT)


ADA_TN = 1536


def _ada_kernel(m_ref, w_ref, b_ref, o_ref):
    m = m_ref[...]
    s = m * jax.nn.sigmoid(m)
    o_ref[0] = _dot(_bf(s), _bf(w_ref[0])) + b_ref[0]


def _ada(m, w_ada, b_ada):
    n = 6 * D_MODEL
    return pl.pallas_call(
        _ada_kernel,
        grid=(DEPTH, n // ADA_TN),
        in_specs=[
            pl.BlockSpec((N_MOD_ROWS, D_MODEL), lambda l, j: (0, 0)),
            pl.BlockSpec((1, D_MODEL, ADA_TN), lambda l, j: (l, 0, j)),
            pl.BlockSpec((1, 1, ADA_TN), lambda l, j: (l, 0, j)),
        ],
        out_specs=pl.BlockSpec((1, N_MOD_ROWS, ADA_TN), lambda l, j: (l, 0, j)),
        out_shape=jax.ShapeDtypeStruct((DEPTH, N_MOD_ROWS, n), jnp.float32),
        compiler_params=_params("parallel", "parallel"),
        name="ada",
    )(m, w_ada, b_ada.reshape(DEPTH, 1, n))


def _lam_kernel(lq1_ref, lk1_ref, lq2_ref, lk2_ref, init_ref, o_ref):
    init = init_ref[...]
    a = jnp.exp(jnp.sum(lq1_ref[...] * lk1_ref[...], axis=-1, keepdims=True))
    b = jnp.exp(jnp.sum(lq2_ref[...] * lk2_ref[...], axis=-1, keepdims=True))
    lam = a - b + init
    post = 1.0 - init
    for l in range(DEPTH):
        o_ref[l, 0:1, :] = jnp.broadcast_to(lam[l:l + 1], (1, LANES))
        o_ref[l, 1:2, :] = jnp.broadcast_to(post[l:l + 1], (1, LANES))


def _lam_consts(lq1, lk1, lq2, lk2):
    init = np.array([[0.8 - 0.6 * math.exp(-0.3 * l)] for l in range(DEPTH)], np.float32)
    return pl.pallas_call(
        _lam_kernel,
        out_shape=jax.ShapeDtypeStruct((DEPTH, 2, LANES), jnp.float32),
        name="diff_lambda",
    )(lq1, lk1, lq2, lk2, jnp.asarray(init))


N_DROW = 2 * NA_WIN_ROWS - 1
N_DCOL = 2 * NA_WIN_COLS - 1


def _bias_kernel(rpb_ref, o_ref):
    l = pl.program_id(0)
    h = pl.program_id(1)
    base = (l * NA_HEADS + h) * (N_DROW * N_DCOL)
    cq = lax.broadcasted_iota(jnp.int32, (GRID_W, LANES), 0)
    lane = lax.broadcasted_iota(jnp.int32, (GRID_W, LANES), 1)
    ck = lane & (GRID_W - 1)
    dcol = jnp.clip(ck - cq, -(NA_WIN_COLS - 1), NA_WIN_COLS - 1) + (NA_WIN_COLS - 1)
    hi = lane >= GRID_W
    for a in range(N_DROW - 1):
        acc = jnp.zeros((GRID_W, LANES), jnp.float32)
        for j in range(N_DCOL):
            lo_v = rpb_ref[base + a * N_DCOL + j]
            hi_v = rpb_ref[base + (a + 1) * N_DCOL + j]
            acc = jnp.where(dcol == j, jnp.where(hi, hi_v, lo_v), acc)
        o_ref[0, 0, a] = acc * LOG2E


def _bias_tiles(na_rpb):
    return pl.pallas_call(
        _bias_kernel,
        grid=(DEPTH, NA_HEADS),
        in_specs=[pl.BlockSpec(memory_space=pltpu.SMEM)],
        out_specs=pl.BlockSpec((1, 1, N_DROW - 1, GRID_W, LANES), lambda l, h: (l, h, 0, 0, 0)),
        out_shape=jax.ShapeDtypeStruct((DEPTH, NA_HEADS, N_DROW - 1, GRID_W, LANES), jnp.float32),
        compiler_params=_params("parallel", "parallel"),
        name="na_bias_tiles",
    )(na_rpb.reshape(-1))


def _row_group(i):
    return jnp.where(i < TILES_PROMPT, 0, 1 + (i - TILES_PROMPT) // (DEC_SEQ // TM))


def _split_specs(width, first):
    return [pl.BlockSpec((TM, width), lambda i: (jnp.minimum(first + i, TILES_PROMPT - 1), 0)),
            pl.BlockSpec((TM, width), lambda i: (jnp.maximum(first + i - TILES_PROMPT, 0), 0))]


def _x_specs(split, first):
    if not split:
        return [pl.BlockSpec((TM, D_MODEL), lambda i: (first + i, 0))]
    return _split_specs(D_MODEL, first)


def _read_tile(refs, first):
    if len(refs) == 1:
        return refs[0][...]
    return jnp.where(first + pl.program_id(0) < TILES_PROMPT, refs[0][...], refs[1][...])


IN_COLS = 2592
IN_QC, IN_CKV, IN_D = 1536, 1920, 2080
TR_ROWS = 256


def _gelu_tanh(x):
    return 0.5 * x * (1.0 + jnp.tanh(math.sqrt(2.0 / math.pi) * (x + 0.044715 * (x * x * x))))


def _sgu(pd, g, w_ref, bt):
    u = _gelu_tanh(pd[:, 0:WIDTH])
    v = _gelu_tanh(pd[:, WIDTH:2 * WIDTH])
    grp = lax.broadcasted_iota(jnp.int32, (1, WIDTH), 1) // SGU_GROUP_DIM
    v2 = v * v
    ms = jnp.zeros_like(v)
    for gi in range(SGU_GROUPS):
        sel = grp == gi
        tot = jnp.sum(jnp.where(sel, v2, 0.0), axis=-1, keepdims=True)
        ms = jnp.where(sel, tot * (1.0 / SGU_GROUP_DIM), ms)
    vg = _bf(v * lax.rsqrt(ms + EPS) * g)
    outs = []
    for c in range(pd.shape[0] // SGU_CHUNK):
        rows = slice(SGU_CHUNK * c, SGU_CHUNK * (c + 1))
        mixed = jnp.zeros((SGU_CHUNK, WIDTH), jnp.float32)
        for gi in range(SGU_GROUPS):
            full = _dot(_bf(w_ref[0, gi]), vg[rows]) + bt[:, gi:gi + 1]
            mixed = jnp.where(grp == gi, full, mixed)
        outs.append(u[rows] * mixed)
    return jnp.concatenate(outs, axis=0)


def _w_in_row_pieces():
    qn = [(IN_QC + 96 * h, MLA_NOPE) for h in range(MLA_HEADS)]
    qp = [(IN_QC + 96 * h + MLA_NOPE, MLA_ROPE) for h in range(MLA_HEADS)]
    seg_c = qn + qp + [(IN_CKV, MLA_KV_RANK + MLA_ROPE)]
    return (0, SEG_A + SEG_B), seg_c, (IN_D, SEG_D)


def _load_w_in(wt_ref, w_scr):
    ab, seg_c, d = _w_in_row_pieces()
    c_rows = jnp.concatenate([wt_ref[0, s:s + n, :] for s, n in seg_c]
                             + [jnp.zeros((SEG_C_PAD, D_MODEL), jnp.float32)], axis=0)
    for t in range(SEG_C // LANES):
        w_scr[:, SEG_A + SEG_B + LANES * t:SEG_A + SEG_B + LANES * (t + 1)] = _bf(c_rows[LANES * t:LANES * (t + 1)].T)
    for (src, n), dst in ((ab, 0), (d, SEG_A + SEG_B + SEG_C)):
        for t in range(n // TR_ROWS):
            rows = wt_ref[0, src + TR_ROWS * t:src + TR_ROWS * (t + 1), :]
            w_scr[:, dst + TR_ROWS * t:dst + TR_ROWS * (t + 1)] = _bf(rows.T)


def _inproj_kernel(n_x, *refs):
    x_refs = refs[:n_x]
    (g_ref, sh_ref, sc_ref, wt_ref, sg_ref, sw_ref, sbt_ref, pa_ref, pb_ref, pc_ref, od_ref, w_scr) = refs[n_x:]

    @pl.when(pl.program_id(0) == 0)
    def _():
        _load_w_in(wt_ref, w_scr)

    h = _rms(_read_tile(x_refs, 0), g_ref[0]) * (1.0 + sc_ref[...]) + sh_ref[...]
    hb = _bf(h)
    off = SEG_A + SEG_B + SEG_C
    od_ref[...] = _sgu(_dot(hb, w_scr[:, off:off + SEG_D]), sg_ref[0], sw_ref, sbt_ref[0])
    off = 0
    for ref in (pa_ref, pb_ref, pc_ref):
        n = ref.shape[1]
        ref[...] = _dot(hb, w_scr[:, off:off + n])
        off += n


def _inproj(l, xs, g_mix, mod, w_in_t, sgu_g, sgu_w, sgu_bt):
    def mod_spec(j):
        return pl.BlockSpec((None, None, 1, D_MODEL), lambda i: (l, _row_group(i), 0, j))

    widths = (SEG_A, SEG_B, SEG_C, WIDTH)
    return pl.pallas_call(
        functools.partial(_inproj_kernel, len(xs)),
        grid=(N_TOK // TM,),
        in_specs=_x_specs(len(xs) == 2, 0) + [
            pl.BlockSpec((1, 1, D_MODEL), lambda i: (l, 0, 0)),
            mod_spec(0), mod_spec(1),
            pl.BlockSpec((1, IN_COLS, D_MODEL), lambda i: (l, 0, 0), pipeline_mode=pl.Buffered(1)),
            pl.BlockSpec((1, 1, WIDTH), lambda i: (l, 0, 0)),
            pl.BlockSpec((1, SGU_GROUPS, SGU_CHUNK, SGU_CHUNK), lambda i: (l, 0, 0, 0)),
            pl.BlockSpec((1, SGU_CHUNK, SGU_GROUPS), lambda i: (l, 0, 0)),
        ],
        out_specs=[pl.BlockSpec((TM, n), lambda i: (i, 0)) for n in widths],
        out_shape=[jax.ShapeDtypeStruct((N_TOK, n), jnp.float32) for n in widths],
        scratch_shapes=[pltpu.VMEM((D_MODEL, IN_COLS_P), jnp.bfloat16)],
        compiler_params=_params("arbitrary"),
        name="inproj",
    )(*xs, g_mix, mod, mod, w_in_t, sgu_g, sgu_w, sgu_bt)


C_QN, C_QP, C_CKV, C_KPE = 0, 256, 384, 512


def _stack_heads(qp):
    lo = _lane_range(0, 64)
    return jnp.concatenate([_bf(jnp.where(lo, qp, 0.0)), _bf(jnp.where(lo, 0.0, qp))], axis=0)


def _unstack_heads(o, n):
    return jnp.where(_lane_range(0, 64), o[0:n], o[n:2 * n])


def _pair_t(c_ref, j):
    return jnp.concatenate([c_ref[2 * j], c_ref[2 * j + 1]], axis=0)


def _stack_components(qp):
    return jnp.concatenate([_bf(jnp.where(_lane_range(32 * t, 32 * (t + 1)), qp, 0.0)) for t in range(4)], axis=0)


def _diff_finish(o, n, lam, post, g2):
    den = o[:, LANES:LANES + 1]
    outs = []
    for t in range(2):
        p1 = o[2 * t * n:(2 * t + 1) * n, 0:LANES] * (1.0 / den[2 * t * n:(2 * t + 1) * n])
        p2 = o[(2 * t + 1) * n:(2 * t + 2) * n, 0:LANES] * (lam / den[(2 * t + 1) * n:(2 * t + 2) * n])
        d = p1 - p2
        own = _lane_range(64 * t, 64 * (t + 1))
        ms = jnp.sum(jnp.where(own, d * d, 0.0), axis=-1, keepdims=True) * (1.0 / DIFF_V_DIM)
        outs.append(d * lax.rsqrt(ms + EPS))
    return jnp.where(_lane_range(0, 64), outs[0], outs[1]) * g2 * post


def _mla_queries(qn_pair, qp_all, j):
    halves = []
    for t in range(2):
        h = 2 * j + t
        halves.append(jnp.concatenate([
            _bf(jnp.where(_lane_range(64 * t, 64 * (t + 1)), qn_pair, 0.0)),
            _bf(jnp.where(_lane_range(MLA_ROPE * h, MLA_ROPE * (h + 1)), qp_all, 0.0))], axis=1))
    return jnp.concatenate(halves, axis=0)


def _write_heads_t(p_ref, rows, col0, out_ref, bb):
    xt = p_ref[rows, col0:col0 + WIDTH].T
    for h in range(N_HEADS):
        out_ref[bb, 0, h] = xt[64 * h:64 * (h + 1)]
    _clear_other_layers(out_ref, bb)


def _clear_other_layers(out_ref, bb):
    if out_ref.shape[1] > 1:
        out_ref[bb, 1:] = jnp.zeros(out_ref.shape[1:], jnp.float32)[1:]


def _mix_prompt_kernel(n_prev, *refs):
    ins, outs = refs[:8], refs[8 + n_prev:]
    log = _ShiftLog()
    _mix_prompt_pass(ins, outs, log)

    @pl.when(log.unsafe())
    def _():
        _mix_prompt_pass(ins, outs, None)


def _mix_prompt_pass(ins, outs, log):
    pa_ref, pb_ref, pc_ref, cst_ref, gsub_ref, gckv_ref, wuk_ref, wuv_ref = ins
    o_ref, nak_ref, nav_ref, dk_ref, dv_ref, ckv_ref, kpe_ref = outs
    first_pass = log is not None
    c_a = HEAD_DIM ** -0.5 * LOG2E
    c_b = DIFF_QK_DIM ** -0.5 * LOG2E
    c_c = (MLA_NOPE + MLA_ROPE) ** -0.5 * LOG2E
    lam = cst_ref[0, 0:1, 0:1]
    post = cst_ref[0, 1:2, 0:1]
    wuk, wuv = _bf(wuk_ref[0]), _bf(wuv_ref[0])

    def sequence(bb, log):
        rows = pl.ds(pl.multiple_of(bb * SEQ, SEQ), SEQ)
        for j in range(N_PAIRS):
            cols = slice(LANES * j, LANES * (j + 1))
            k = _bf(pa_ref[rows, WIDTH + LANES * j:WIDTH + LANES * (j + 1)])
            v = _with_ones(_bf(pa_ref[rows, 2 * WIDTH + LANES * j:2 * WIDTH + LANES * (j + 1)]))
            qs = _stack_heads(pa_ref[rows, cols] * c_a)
            o = _attend([_dot_nt(qs, k)], [v], _shift(log, qs, _max_key_norm(k) * BOUND_SLACK))
            o_ref[rows, cols] = _unstack_heads(_normalised(o), SEQ)
        if first_pass:
            _write_heads_t(pa_ref, rows, WIDTH, nak_ref, bb)
            _write_heads_t(pa_ref, rows, 2 * WIDTH, nav_ref, bb)
        for j in range(N_PAIRS):
            cols = slice(LANES * j, LANES * (j + 1))
            k = _bf(pb_ref[rows, WIDTH + LANES * j:WIDTH + LANES * (j + 1)])
            v = _with_ones(_bf(pb_ref[rows, 2 * WIDTH + LANES * j:2 * WIDTH + LANES * (j + 1)]))
            qs = _stack_components(pb_ref[rows, cols] * c_b)
            o = _attend([_dot_nt(qs, k)], [v], _shift(log, qs, _max_key_norm(k) * BOUND_SLACK))
            o_ref[rows, WIDTH + LANES * j:WIDTH + LANES * (j + 1)] = _diff_finish(o, SEQ, lam, post, gsub_ref[0])
        if first_pass:
            _write_heads_t(pb_ref, rows, WIDTH, dk_ref, bb)
            _write_heads_t(pb_ref, rows, 2 * WIDTH, dv_ref, bb)
        ckv = _rms(pc_ref[rows, C_CKV:C_CKV + MLA_KV_RANK], gckv_ref[0])
        kpe_slot = pc_ref[rows, C_KPE:C_KPE + LANES]
        if first_pass:
            ckv_ref[bb, 0] = ckv
            _clear_other_layers(ckv_ref, bb)
            kpe_ref[bb, 0] = kpe_slot.T[0:MLA_ROPE]
            _clear_other_layers(kpe_ref, bb)
        ckv_b = _bf(ckv)
        kn = _bf(_dot(ckv_b, wuk))
        vv = _bf(_dot(ckv_b, wuv))
        kpe4 = _bf(_tile4(kpe_slot))
        qn = pc_ref[rows, C_QN:C_QN + WIDTH] * c_c
        qp = pc_ref[rows, C_QP:C_QP + LANES] * c_c
        for j in range(N_PAIRS):
            cols = slice(LANES * j, LANES * (j + 1))
            k = jnp.concatenate([kn[:, cols], kpe4], axis=1)
            qs = _mla_queries(qn[:, cols], qp, j)
            o = _attend([_dot_nt(qs, k)], [_with_ones(vv[:, cols])], _shift(log, qs, _max_key_norm(k) * BOUND_SLACK))
            o_ref[rows, 2 * WIDTH + LANES * j:2 * WIDTH + LANES * (j + 1)] = _unstack_heads(_normalised(o), SEQ)

    _loop(PB, log, sequence)


def _mix_prompt(l, pa, pb, pc, cst, g_sub2, g_ckv, w_uk, w_uv, prev):
    n_prev = len(prev)
    tails = [(NA_HEADS, HEAD_DIM, SEQ)] * 2 + [(DIFF_HEADS, 64, SEQ)] * 2 + [(SEQ, MLA_KV_RANK), (MLA_ROPE, SEQ)]

    def cache_spec(tail):
        if l == 0:
            return pl.BlockSpec((PB, DEPTH) + tail, lambda b: (b, 0) + (0,) * len(tail))
        return pl.BlockSpec((PB, 1) + tail, lambda b: (b, l) + (0,) * len(tail))

    def rows(width):
        return pl.BlockSpec((PB * SEQ, width), lambda b: (b, 0))

    def layer(*tail):
        return pl.BlockSpec((1,) + tail, lambda b: (l,) + (0,) * len(tail))

    return pl.pallas_call(
        functools.partial(_mix_prompt_kernel, n_prev),
        grid=(BATCH // PB,),
        in_specs=[rows(SEG_A), rows(SEG_B), rows(SEG_C), layer(2, LANES), layer(1, LANES), layer(1, MLA_KV_RANK),
                  layer(MLA_KV_RANK, WIDTH), layer(MLA_KV_RANK, WIDTH)] + [pl.BlockSpec(memory_space=pl.ANY)] * n_prev,
        out_specs=[rows(O_ATT)] + [cache_spec(t) for t in tails],
        out_shape=[jax.ShapeDtypeStruct((N_PROMPT, O_ATT), jnp.float32)]
        + [jax.ShapeDtypeStruct((BATCH, DEPTH) + t, jnp.float32) for t in tails],
        input_output_aliases={8 + i: 1 + i for i in range(n_prev)},
        compiler_params=_params("parallel"),
        name="mix_prompt",
    )(pa, pb, pc, cst, g_sub2, g_ckv, w_uk, w_uv, *prev)


def _na_row_groups():
    kh = min(NA_WIN_ROWS, GRID_ROWS)
    r0s = [min(max(r - kh // 2, 0), GRID_ROWS - kh) for r in range(GRID_ROWS)]
    groups = []
    for r, r0 in enumerate(r0s):
        if groups and groups[-1][2] == r0:
            groups[-1][1] = r
        else:
            groups.append([r, r, r0])
    return kh, [tuple(g) for g in groups]


def _na_sample(pa_ref, ck_ref, cv_ref, tt_ref, o_ref, log):
    c = HEAD_DIM ** -0.5 * LOG2E
    kh, groups = _na_row_groups()
    lk = kh * GRID_W
    edge = [g for g in groups if g[1] > g[0]]
    inner = [g for g in groups if g[1] == g[0]]
    depth = inner[0][0] - inner[0][2]
    assert all(g[0] - g[2] == depth for g in inner) and [g[0] for g in inner] == list(range(inner[0][0], inner[-1][0] + 1))

    def in_window(n):
        cq = lax.broadcasted_iota(jnp.int32, (n, lk), 0) & (GRID_W - 1)
        ck = lax.broadcasted_iota(jnp.int32, (n, lk), 1) & (GRID_W - 1)
        c0 = jnp.clip(cq - NA_WIN_COLS // 2, 0, GRID_W - NA_WIN_COLS)
        return (ck >= c0) & (ck < c0 + NA_WIN_COLS)

    for j in range(N_PAIRS):
        cols = slice(LANES * j, LANES * (j + 1))
        kcols = slice(WIDTH + LANES * j, WIDTH + LANES * (j + 1))
        vcols = slice(2 * WIDTH + LANES * j, 2 * WIDTH + LANES * (j + 1))
        kc_t = _bf(_pair_t(ck_ref, j))
        vc = _with_ones(_bf(_pair_t(cv_ref, j).T))
        kmax = jnp.maximum(_max_key_norm(_bf(pa_ref[:, kcols])), _max_key_norm_t(kc_t)) * BOUND_SLACK
        tmax = functools.reduce(jnp.maximum, [tt_ref[2 * j + t, a] for t in range(2) for a in range(N_DROW - 1)])
        bplus = jnp.maximum(jnp.max(jnp.max(tmax, axis=-1, keepdims=True), axis=0, keepdims=True), 0.0)

        def group(row0, key0, offsets, log):
            n = len(offsets) * GRID_W
            rows, keys = pl.ds(row0, n), pl.ds(key0, lk)
            qg = _stack_heads(pa_ref[rows, cols] * c)
            k = _bf(pa_ref[keys, kcols])
            v = _with_ones(_bf(pa_ref[keys, vcols]))
            bias = jnp.concatenate([
                jnp.concatenate([tt_ref[2 * j + t, 2 * i - off + NA_WIN_ROWS - 1] for i in range(kh // 2)], axis=1)
                for t in range(2) for off in offsets], axis=0)
            s_loc = jnp.where(in_window(2 * n), _dot_nt(qg, k) + bias, NEG_INF)
            o = _attend([_dot(qg, kc_t), s_loc], [vc, v], _shift(log, qg, kmax, bplus))
            o_ref[rows, cols] = _unstack_heads(_normalised(o), n)

        for (r_lo, r_hi, r0) in edge:
            group(r_lo * GRID_W, r0 * GRID_W, [r - r0 for r in range(r_lo, r_hi + 1)], log)

        def inner_row(i, log):
            r = inner[0][0] + i
            group(pl.multiple_of(r * GRID_W, GRID_W), pl.multiple_of((r - depth) * GRID_W, GRID_W), [depth], log)

        _loop(len(inner), log, inner_row)


def _diff_sample(pb_ref, ck_ref, cv_ref, cos_ref, sin_ref, cst_ref, g_ref, o_ref, log):
    c = DIFF_QK_DIM ** -0.5 * LOG2E
    lam = cst_ref[0, 0:1, 0:1]
    post = cst_ref[0, 1:2, 0:1]
    for j in range(N_PAIRS):
        cols = slice(LANES * j, LANES * (j + 1))
        k_new = _bf(_rope(pb_ref[:, WIDTH + LANES * j:WIDTH + LANES * (j + 1)], cos_ref[:, cols], sin_ref[:, cols]))
        kc_t = _bf(_pair_t(ck_ref, j))
        vc = _with_ones(_bf(_pair_t(cv_ref, j).T))
        v = _with_ones(_bf(pb_ref[:, 2 * WIDTH + LANES * j:2 * WIDTH + LANES * (j + 1)]))
        kmax = jnp.maximum(_max_key_norm(k_new), _max_key_norm_t(kc_t)) * BOUND_SLACK

        def block(qi, log):
            rows = pl.ds(pl.multiple_of(qi * QB, QB), QB)
            qs = _stack_components(_rope(pb_ref[rows, cols], cos_ref[rows, cols], sin_ref[rows, cols]) * c)
            o = _attend([_dot(qs, kc_t), _dot_nt(qs, k_new)], [vc, v], _shift(log, qs, kmax))
            o_ref[rows, WIDTH + LANES * j:WIDTH + LANES * (j + 1)] = _diff_finish(o, QB, lam, post, g_ref[0])

        _loop(DEC_SEQ // QB, log, block)


def _mla_sample(pc_ref, cckv_ref, ckpe_ref, cosq_ref, sinq_ref, cosk_ref, sink_ref, gckv_ref, wuk_ref, wuv_ref, o_ref,
                log):
    c = (MLA_NOPE + MLA_ROPE) ** -0.5 * LOG2E
    wuk, wuv = _bf(wuk_ref[0]), _bf(wuv_ref[0])
    ckv_new = _bf(_rms(pc_ref[:, C_CKV:C_CKV + MLA_KV_RANK], gckv_ref[0]))
    ckv_old = _bf(cckv_ref[...])
    kpe_new = _bf(_tile4(_rope(pc_ref[:, C_KPE:C_KPE + LANES], cosk_ref[...], sink_ref[...])))
    kpe_old = _bf(jnp.concatenate([ckpe_ref[...]] * MLA_HEADS, axis=0).T)
    kn_new, kn_old = _bf(_dot(ckv_new, wuk)), _bf(_dot(ckv_old, wuk))
    v_new, v_old = _bf(_dot(ckv_new, wuv)), _bf(_dot(ckv_old, wuv))
    for j in range(N_PAIRS):
        cols = slice(LANES * j, LANES * (j + 1))
        k_old = jnp.concatenate([kn_old[:, cols], kpe_old], axis=1)
        k_new = jnp.concatenate([kn_new[:, cols], kpe_new], axis=1)
        vo, vn = _with_ones(v_old[:, cols]), _with_ones(v_new[:, cols])
        kmax = jnp.maximum(_max_key_norm(k_old), _max_key_norm(k_new)) * BOUND_SLACK

        def block(qi, log):
            rows = pl.ds(pl.multiple_of(qi * QB, QB), QB)
            qn = pc_ref[rows, C_QN + LANES * j:C_QN + LANES * (j + 1)] * c
            qp = _rope(pc_ref[rows, C_QP:C_QP + LANES], cosq_ref[rows, :], sinq_ref[rows, :]) * c
            qs = _mla_queries(qn, qp, j)
            o = _attend([_dot_nt(qs, k_old), _dot_nt(qs, k_new)], [vo, vn], _shift(log, qs, kmax))
            o_ref[rows, 2 * WIDTH + LANES * j:2 * WIDTH + LANES * (j + 1)] = _unstack_heads(_normalised(o), QB)

        _loop(DEC_SEQ // QB, log, block)


def _mix_sample_kernel(pa_ref, pb_ref, pc_ref, cnak_ref, cnav_ref, cdk_ref, cdv_ref, cckv_ref, ckpe_ref, tt_ref,
                       cosb_ref, sinb_ref, cosq_ref, sinq_ref, cosk_ref, sink_ref,
                       cst_ref, gsub_ref, gckv_ref, wuk_ref, wuv_ref, o_ref):
    def run(log):
        _na_sample(pa_ref, cnak_ref, cnav_ref, tt_ref, o_ref, log)
        _diff_sample(pb_ref, cdk_ref, cdv_ref, cosb_ref, sinb_ref, cst_ref, gsub_ref, o_ref, log)
        _mla_sample(pc_ref, cckv_ref, ckpe_ref, cosq_ref, sinq_ref, cosk_ref, sink_ref, gckv_ref, wuk_ref, wuv_ref,
                    o_ref, log)

    log = _ShiftLog()
    run(log)

    @pl.when(log.unsafe())
    def _():
        run(None)


def _mix_sample(l, pa, pb, pc, caches_t, tt, tables, cst, g_sub2, g_ckv, w_uk, w_uv):
    first = N_PROMPT // DEC_SEQ

    def rows(width):
        return pl.BlockSpec((DEC_SEQ, width), lambda b: (first + b, 0))

    def cache(*tail):
        return pl.BlockSpec((None, None) + tail, lambda b: (b, l) + (0,) * len(tail))

    def layer(*tail):
        return pl.BlockSpec((1,) + tail, lambda b: (l,) + (0,) * len(tail))

    def table(width):
        return pl.BlockSpec((DEC_SEQ, width), lambda b: (0, 0))

    kv_t = cache(N_HEADS, 64, PAST_LEN)
    return pl.pallas_call(
        _mix_sample_kernel,
        grid=(DEC_BATCH,),
        in_specs=[rows(SEG_A), rows(SEG_B), rows(SEG_C), kv_t, kv_t, kv_t, kv_t,
                  cache(PAST_LEN, MLA_KV_RANK), cache(MLA_ROPE, PAST_LEN),
                  pl.BlockSpec((None, NA_HEADS, N_DROW - 1, GRID_W, LANES), lambda b: (l, 0, 0, 0, 0)),
                  table(WIDTH), table(WIDTH), table(LANES), table(LANES), table(LANES), table(LANES),
                  layer(2, LANES), layer(1, LANES), layer(1, MLA_KV_RANK),
                  layer(MLA_KV_RANK, WIDTH), layer(MLA_KV_RANK, WIDTH)],
        out_specs=pl.BlockSpec((DEC_SEQ, O_ATT), lambda b: (b, 0)),
        out_shape=jax.ShapeDtypeStruct((N_SAMPLE, O_ATT), jnp.float32),
        compiler_params=_params("parallel"),
        name="mix_sample",
    )(pa, pb, pc, *caches_t, tt, *tables, cst, g_sub2, g_ckv, w_uk, w_uv)


FF_CHUNK = 1024


def _outffn_kernel(n_x, first, final, *refs):
    x_refs, o_refs = refs[:n_x], refs[n_x:n_x + 2]
    (od_ref, wout_ref, g1_ref, gffn_ref, sh2_ref, sc2_ref, g2_ref, w1_ref, w2_ref, gfin_ref, y_ref) = refs[n_x + 2:]
    acc = (_dot(_bf(_read_tile(o_refs, first)), wout_ref[0, 0:O_ATT, :])
           + _dot(_bf(od_ref[...]), wout_ref[0, O_ATT:O_ATT + WIDTH, :]))
    x1 = _read_tile(x_refs, first) + g1_ref[...] * acc
    hf = _bf(_rms(x1, gffn_ref[0]) * (1.0 + sc2_ref[...]) + sh2_ref[...])
    acc = jnp.zeros((TM, D_MODEL), jnp.float32)
    for c in range(D_FF // FF_CHUNK):
        cols = slice(FF_CHUNK * c, FF_CHUNK * (c + 1))
        a = jnp.square(jnp.maximum(_dot(hf, w1_ref[0, :, cols]), 0.0))
        acc += _dot(_bf(a), w2_ref[0, cols, :])
    y = x1 + g2_ref[...] * acc
    if final:
        y = _rms(y, gfin_ref[...])
    y_ref[...] = y


def _outffn(l, xs, o_p, o_s, od, w_out, g_ffn, mod, w1, w2, g_final, first, n_tiles):
    def mod_spec(j):
        return pl.BlockSpec((None, None, 1, D_MODEL), lambda i: (l, _row_group(first + i), 0, j))

    def resident(shape):
        return pl.BlockSpec(shape, lambda i: (l,) + (0,) * (len(shape) - 1), pipeline_mode=pl.Buffered(1))

    return pl.pallas_call(
        functools.partial(_outffn_kernel, len(xs), first, l == DEPTH - 1),
        grid=(n_tiles,),
        in_specs=_x_specs(len(xs) == 2, first) + _split_specs(O_ATT, first) + [
            pl.BlockSpec((TM, WIDTH), lambda i: (first + i, 0)),
            resident((1, 4 * WIDTH, D_MODEL)),
            mod_spec(2),
            pl.BlockSpec((1, 1, D_MODEL), lambda i: (l, 0, 0)),
            mod_spec(3), mod_spec(4), mod_spec(5),
            resident((1, D_MODEL, D_FF)),
            resident((1, D_FF, D_MODEL)),
            pl.BlockSpec((1, D_MODEL), lambda i: (0, 0)),
        ],
        out_specs=pl.BlockSpec((TM, D_MODEL), lambda i: (i, 0)),
        out_shape=jax.ShapeDtypeStruct((n_tiles * TM, D_MODEL), jnp.float32),
        compiler_params=_params("parallel"),
        name="outffn",
    )(*xs, o_p, o_s, od, w_out, mod, g_ffn, mod, mod, mod, w1, w2, g_final)


def _rope32_tables():
    t = np.arange(DEC_SEQ)
    rows, cols = (t // GRID_W).astype(np.float64), (t % GRID_W).astype(np.float64)
    half = 8
    freqs = ROPE_BASE ** (-np.arange(half, dtype=np.float64) / half)
    cos, sin = [], []
    for pos in (rows, cols):
        ang = pos[:, None] * freqs[None, :]
        cos += [np.cos(ang), np.cos(ang)]
        sin += [-np.sin(ang), np.sin(ang)]
    return np.concatenate(cos, axis=1).astype(np.float32), np.concatenate(sin, axis=1).astype(np.float32)


def _rope_tables():
    c32, s32 = _rope32_tables()
    tile = lambda a, n: np.tile(a, (1, n))
    pad = np.zeros((DEC_SEQ, 96), np.float32)
    cos_k = np.concatenate([c32, pad + 1.0], axis=1)
    sin_k = np.concatenate([s32, pad], axis=1)
    return (tile(c32, 8), tile(s32, 8),
            tile(c32, 4), tile(s32, 4),
            cos_k, sin_k)


def kernel(x_prompt, x_sample, cache_na_k, cache_na_v, cache_diff_k, cache_diff_v, cache_mla_ckv, cache_mla_kpe, c, c_ctx, w_ada, b_ada, g_mix, g_ffn, w_in, w_out, na_rpb, diff_lq1, diff_lk1, diff_lq2, diff_lk2, diff_g_subln, mla_g_ckv, mla_w_uk, mla_w_uv, sgu_g, sgu_w, sgu_b, w_ff1, w_ff2, g_final):
    f32 = jnp.float32
    m = jnp.concatenate([c_ctx[None, :], c, jnp.zeros((N_MOD_ROWS - 1 - DEC_BATCH, D_MODEL), f32)], axis=0)
    mod = _ada(m, w_ada, b_ada).reshape(DEPTH, N_MOD_ROWS, 1, 6 * D_MODEL)
    cst = _lam_consts(diff_lq1, diff_lk1, diff_lq2, diff_lk2)
    tt = _bias_tiles(na_rpb)
    tables = [jnp.asarray(t) for t in _rope_tables()]

    t_last = lambda a: jnp.swapaxes(a, -1, -2)
    w_in_t = t_last(w_in)
    caches_t = (t_last(cache_na_k), t_last(cache_na_v), t_last(cache_diff_k), t_last(cache_diff_v),
                cache_mla_ckv, t_last(cache_mla_kpe))
    w_out_b, w1_b, w2_b = _bf(w_out), _bf(w_ff1), _bf(w_ff2)
    g_mix3 = g_mix.reshape(DEPTH, 1, D_MODEL)
    g_ffn3 = g_ffn.reshape(DEPTH, 1, D_MODEL)
    g_sub2 = jnp.tile(diff_g_subln, (1, 2)).reshape(DEPTH, 1, LANES)
    g_ckv3 = mla_g_ckv.reshape(DEPTH, 1, MLA_KV_RANK)
    sgu_g3 = sgu_g.reshape(DEPTH, 1, WIDTH)
    sgu_bt = sgu_b.transpose(0, 2, 1)
    g_fin2 = g_final.reshape(1, D_MODEL)

    xs = (x_prompt.reshape(N_PROMPT, D_MODEL), x_sample.reshape(N_SAMPLE, D_MODEL))
    new = ()
    for l in range(DEPTH):
        pa, pb, pc, od = _inproj(l, xs, g_mix3, mod, w_in_t, sgu_g3, sgu_w, sgu_bt)
        o_p, *new = _mix_prompt(l, pa, pb, pc, cst, g_sub2, g_ckv3, mla_w_uk, mla_w_uv, new)
        o_s = _mix_sample(l, pa, pb, pc, caches_t, tt, tables, cst, g_sub2, g_ckv3, mla_w_uk, mla_w_uv)
        ffn = functools.partial(_outffn, l, xs, o_p, o_s, od, w_out_b, g_ffn3, mod, w1_b, w2_b, g_fin2)
        if l < DEPTH - 1:
            xs = (ffn(0, TILES_PROMPT + TILES_SAMPLE),)
        else:
            xs = (ffn(0, TILES_PROMPT), ffn(TILES_PROMPT, TILES_SAMPLE))
    y_prompt = xs[0].reshape(BATCH, SEQ, D_MODEL)
    y_sample = xs[1].reshape(DEC_BATCH, DEC_SEQ, D_MODEL)
    na_k, na_v, diff_k, diff_v, mla_ckv, mla_kpe = new
    return (y_prompt, y_sample, t_last(na_k), t_last(na_v), t_last(diff_k), t_last(diff_v), mla_ckv, t_last(mla_kpe))
```

```python
import functools
import math

import numpy as np
import jax
import jax.numpy as jnp
from jax import lax
from jax.experimental import pallas as pl
from jax.experimental.pallas import tpu as pltpu

D_MODEL = 1024
BATCH = 16
SEQ = 256
DEPTH = 4
DEC_BATCH = 2
DEC_SEQ = 1024
PAST_LEN = 512
GRID_W = 64
GRID_ROWS = DEC_SEQ // GRID_W
HEAD_DIM = 64
NA_HEADS = 4
NA_WIN_ROWS = 8
NA_WIN_COLS = 16
DIFF_HEADS = 4
DIFF_QK_DIM = 32
DIFF_V_DIM = 64
MLA_HEADS = 4
MLA_NOPE = 64
MLA_ROPE = 32
MLA_V = 64
MLA_KV_RANK = 128
SGU_GROUPS = 4
SGU_GROUP_DIM = 64
SGU_CHUNK = 128
D_FF = 4 * D_MODEL
ROPE_BASE = 10000.0
EPS = 1e-6
NEG_INF = -1e30
LOG2E = 1.4426950408889634

N_HEADS = 4
N_PAIRS = N_HEADS // 2
LANES = 128
WIDTH = 256
N_PROMPT = BATCH * SEQ
N_SAMPLE = DEC_BATCH * DEC_SEQ
N_TOK = N_PROMPT + N_SAMPLE
N_MOD_ROWS = 8

SEG_A = 3 * WIDTH
SEG_B = 3 * WIDTH
SEG_C = 640
SEG_D = 2 * WIDTH
SEG_C_PAD = 96
IN_COLS_P = SEG_A + SEG_B + SEG_C + SEG_D
O_ATT = 3 * WIDTH

TM = 512
TILES_PROMPT = N_PROMPT // TM
TILES_SAMPLE = N_SAMPLE // TM
PB = 2
QB = 256
VMEM_LIMIT = 56 * 1024 * 1024


def _bf(x):
    return x.astype(jnp.bfloat16)


def _dot(a, b):
    return jnp.dot(a, b, preferred_element_type=jnp.float32)


def _dot_nt(a, b):
    return lax.dot_general(a, b, (((1,), (1,)), ((), ())), preferred_element_type=jnp.float32)


def _rms(x, g):
    ms = jnp.mean(x * x, axis=-1, keepdims=True)
    return x * lax.rsqrt(ms + EPS) * g


def _lane_range(lo, hi, width=LANES):
    lane = lax.broadcasted_iota(jnp.int32, (1, width), 1)
    return (lane >= lo) & (lane < hi)


def _with_ones(v):
    lane = lax.broadcasted_iota(jnp.int32, (v.shape[0], LANES), 1)
    return jnp.concatenate([v, jnp.where(lane == 0, 1.0, 0.0).astype(jnp.bfloat16)], axis=1)


def _attend(scores, values, log=None, bound=None):
    if bound is None:
        m = functools.reduce(jnp.maximum, [jnp.max(s, axis=-1, keepdims=True) for s in scores])
    else:
        m = bound
        seen = jnp.max(scores[0][:, 0:LANES], axis=-1, keepdims=True)
        log.worst = jnp.maximum(log.worst, jnp.max(m - seen, axis=0, keepdims=True))
    return functools.reduce(lambda a, b: a + b, [_dot(_bf(jnp.exp2(s - m)), v) for s, v in zip(scores, values)])


BOUND_SLACK = 1.02
OVERSHOOT_LIMIT = 100.0


class _ShiftLog:
    def __init__(self, worst=None):
        self.worst = jnp.zeros((1, 1), jnp.float32) if worst is None else worst

    def unsafe(self):
        return jnp.logical_not(self.worst[0, 0] < OVERSHOOT_LIMIT)


def _bound(log, qs, kmaxes, extra=0.0):
    if log is None:
        return None
    n = qs.shape[0] // len(kmaxes)
    kmax = jnp.concatenate([jnp.broadcast_to(km * BOUND_SLACK, (n, 1)) for km in kmaxes], axis=0)
    return _row_norm(qs) * kmax + extra


def _sumsq_rows(k, lo, hi):
    kf = k.astype(jnp.float32)
    return jnp.sum(jnp.where(_lane_range(lo, hi, k.shape[1]), kf * kf, 0.0), axis=-1, keepdims=True)


def _sumsq_cols(kt, lo, hi):
    kf = kt[lo:hi].astype(jnp.float32)
    return jnp.sum(kf * kf, axis=0, keepdims=True)


def _largest_norm(*sumsqs):
    return jnp.sqrt(functools.reduce(jnp.maximum, [jnp.max(s, axis=(0, 1), keepdims=True) for s in sumsqs]))


HEAD_LANES = ((0, 64), (64, 128))
COMPONENT_LANES = tuple((32 * t, 32 * (t + 1)) for t in range(4))


def _loop(n, log, body):
    def step(i, worst):
        inner = None if log is None else _ShiftLog(worst)
        body(i, inner)
        return worst if inner is None else inner.worst

    worst = lax.fori_loop(0, n, step, jnp.zeros((1, 1), jnp.float32) if log is None else log.worst)
    if log is not None:
        log.worst = worst


def _row_norm(q):
    qf = q.astype(jnp.float32)
    return jnp.sqrt(jnp.sum(qf * qf, axis=-1, keepdims=True))


def _normalised(o_ext):
    return o_ext[:, 0:LANES] * (1.0 / o_ext[:, LANES:LANES + 1])


def _swap8(x):
    lane = lax.broadcasted_iota(jnp.int32, (1, LANES), 1)
    return jnp.where((lane & 15) < 8, pltpu.roll(x, LANES - 8, 1), pltpu.roll(x, 8, 1))


def _rope(x, cos, sin):
    outs = []
    for c in range(x.shape[1] // LANES):
        sl = slice(LANES * c, LANES * (c + 1))
        xc = x[:, sl]
        outs.append(xc * cos[:, sl] + _swap8(xc) * sin[:, sl])
    return outs[0] if len(outs) == 1 else jnp.concatenate(outs, axis=1)


def _tile4(x):
    return x + pltpu.roll(x, 32, 1) + pltpu.roll(x, 64, 1) + pltpu.roll(x, 96, 1)


def _params(*sem):
    return pltpu.CompilerParams(dimension_semantics=sem, vmem_limit_bytes=VMEM_LIMIT)


ADA_TN = 1536


def _ada_kernel(m_ref, w_ref, b_ref, o_ref):
    m = m_ref[...]
    s = m * jax.nn.sigmoid(m)
    o_ref[0] = _dot(_bf(s), _bf(w_ref[0])) + b_ref[0]


def _ada(m, w_ada, b_ada):
    n = 6 * D_MODEL
    return pl.pallas_call(
        _ada_kernel,
        grid=(DEPTH, n // ADA_TN),
        in_specs=[
            pl.BlockSpec((N_MOD_ROWS, D_MODEL), lambda l, j: (0, 0)),
            pl.BlockSpec((1, D_MODEL, ADA_TN), lambda l, j: (l, 0, j)),
            pl.BlockSpec((1, 1, ADA_TN), lambda l, j: (l, 0, j)),
        ],
        out_specs=pl.BlockSpec((1, N_MOD_ROWS, ADA_TN), lambda l, j: (l, 0, j)),
        out_shape=jax.ShapeDtypeStruct((DEPTH, N_MOD_ROWS, n), jnp.float32),
        compiler_params=_params("parallel", "parallel"),
        name="ada",
    )(m, w_ada, b_ada.reshape(DEPTH, 1, n))


def _lam_kernel(lq1_ref, lk1_ref, lq2_ref, lk2_ref, init_ref, o_ref):
    init = init_ref[...]
    a = jnp.exp(jnp.sum(lq1_ref[...] * lk1_ref[...], axis=-1, keepdims=True))
    b = jnp.exp(jnp.sum(lq2_ref[...] * lk2_ref[...], axis=-1, keepdims=True))
    lam = a - b + init
    post = 1.0 - init
    for l in range(DEPTH):
        o_ref[l, 0:1, :] = jnp.broadcast_to(lam[l:l + 1], (1, LANES))
        o_ref[l, 1:2, :] = jnp.broadcast_to(post[l:l + 1], (1, LANES))


def _lam_consts(lq1, lk1, lq2, lk2):
    init = np.array([[0.8 - 0.6 * math.exp(-0.3 * l)] for l in range(DEPTH)], np.float32)
    return pl.pallas_call(
        _lam_kernel,
        out_shape=jax.ShapeDtypeStruct((DEPTH, 2, LANES), jnp.float32),
        name="diff_lambda",
    )(lq1, lk1, lq2, lk2, jnp.asarray(init))


N_DROW = 2 * NA_WIN_ROWS - 1
N_DCOL = 2 * NA_WIN_COLS - 1


def _bias_kernel(rpb_ref, o_ref):
    l = pl.program_id(0)
    h = pl.program_id(1)
    base = (l * NA_HEADS + h) * (N_DROW * N_DCOL)
    cq = lax.broadcasted_iota(jnp.int32, (GRID_W, LANES), 0)
    lane = lax.broadcasted_iota(jnp.int32, (GRID_W, LANES), 1)
    ck = lane & (GRID_W - 1)
    dcol = jnp.clip(ck - cq, -(NA_WIN_COLS - 1), NA_WIN_COLS - 1) + (NA_WIN_COLS - 1)
    hi = lane >= GRID_W
    for a in range(N_DROW - 1):
        acc = jnp.zeros((GRID_W, LANES), jnp.float32)
        for j in range(N_DCOL):
            lo_v = rpb_ref[base + a * N_DCOL + j]
            hi_v = rpb_ref[base + (a + 1) * N_DCOL + j]
            acc = jnp.where(dcol == j, jnp.where(hi, hi_v, lo_v), acc)
        o_ref[0, 0, a] = acc * LOG2E


def _bias_tiles(na_rpb):
    return pl.pallas_call(
        _bias_kernel,
        grid=(DEPTH, NA_HEADS),
        in_specs=[pl.BlockSpec(memory_space=pltpu.SMEM)],
        out_specs=pl.BlockSpec((1, 1, N_DROW - 1, GRID_W, LANES), lambda l, h: (l, h, 0, 0, 0)),
        out_shape=jax.ShapeDtypeStruct((DEPTH, NA_HEADS, N_DROW - 1, GRID_W, LANES), jnp.float32),
        compiler_params=_params("parallel", "parallel"),
        name="na_bias_tiles",
    )(na_rpb.reshape(-1))


def _row_group(i):
    return jnp.where(i < TILES_PROMPT, 0, 1 + (i - TILES_PROMPT) // (DEC_SEQ // TM))


def _split_specs(width, first):
    return [pl.BlockSpec((TM, width), lambda i: (jnp.minimum(first + i, TILES_PROMPT - 1), 0)),
            pl.BlockSpec((TM, width), lambda i: (jnp.maximum(first + i - TILES_PROMPT, 0), 0))]


def _x_specs(split, first):
    if not split:
        return [pl.BlockSpec((TM, D_MODEL), lambda i: (first + i, 0))]
    return _split_specs(D_MODEL, first)


def _read_tile(refs, first):
    if len(refs) == 1:
        return refs[0][...]
    return jnp.where(first + pl.program_id(0) < TILES_PROMPT, refs[0][...], refs[1][...])


IN_COLS = 2592
IN_QC, IN_CKV, IN_D = 1536, 1920, 2080
TR_ROWS = 256


def _gelu_tanh(x):
    return 0.5 * x * (1.0 + jnp.tanh(math.sqrt(2.0 / math.pi) * (x + 0.044715 * (x * x * x))))


def _sgu(pd, g, w_ref, bt):
    u = _gelu_tanh(pd[:, 0:WIDTH])
    v = _gelu_tanh(pd[:, WIDTH:2 * WIDTH])
    grp = lax.broadcasted_iota(jnp.int32, (1, WIDTH), 1) // SGU_GROUP_DIM
    v2 = v * v
    ms = jnp.zeros_like(v)
    for gi in range(SGU_GROUPS):
        sel = grp == gi
        tot = jnp.sum(jnp.where(sel, v2, 0.0), axis=-1, keepdims=True)
        ms = jnp.where(sel, tot * (1.0 / SGU_GROUP_DIM), ms)
    vg = _bf(v * lax.rsqrt(ms + EPS) * g)
    outs = []
    for c in range(pd.shape[0] // SGU_CHUNK):
        rows = slice(SGU_CHUNK * c, SGU_CHUNK * (c + 1))
        mixed = jnp.zeros((SGU_CHUNK, WIDTH), jnp.float32)
        for gi in range(SGU_GROUPS):
            full = _dot(_bf(w_ref[0, gi]), vg[rows]) + bt[:, gi:gi + 1]
            mixed = jnp.where(grp == gi, full, mixed)
        outs.append(u[rows] * mixed)
    return jnp.concatenate(outs, axis=0)


def _w_in_row_pieces():
    qn = [(IN_QC + 96 * h, MLA_NOPE) for h in range(MLA_HEADS)]
    qp = [(IN_QC + 96 * h + MLA_NOPE, MLA_ROPE) for h in range(MLA_HEADS)]
    seg_c = qn + qp + [(IN_CKV, MLA_KV_RANK + MLA_ROPE)]
    return (0, SEG_A + SEG_B), seg_c, (IN_D, SEG_D)


def _load_w_in(wt_ref, w_scr):
    ab, seg_c, d = _w_in_row_pieces()
    c_rows = jnp.concatenate([wt_ref[0, s:s + n, :] for s, n in seg_c]
                             + [jnp.zeros((SEG_C_PAD, D_MODEL), jnp.float32)], axis=0)
    for t in range(SEG_C // LANES):
        w_scr[:, SEG_A + SEG_B + LANES * t:SEG_A + SEG_B + LANES * (t + 1)] = _bf(c_rows[LANES * t:LANES * (t + 1)].T)
    for (src, n), dst in ((ab, 0), (d, SEG_A + SEG_B + SEG_C)):
        for t in range(n // TR_ROWS):
            rows = wt_ref[0, src + TR_ROWS * t:src + TR_ROWS * (t + 1), :]
            w_scr[:, dst + TR_ROWS * t:dst + TR_ROWS * (t + 1)] = _bf(rows.T)


def _inproj_kernel(n_x, *refs):
    x_refs = refs[:n_x]
    (g_ref, sh_ref, sc_ref, wt_ref, sg_ref, sw_ref, sbt_ref, pa_ref, pb_ref, pc_ref, od_ref, w_scr) = refs[n_x:]

    @pl.when(pl.program_id(0) == 0)
    def _():
        _load_w_in(wt_ref, w_scr)

    h = _rms(_read_tile(x_refs, 0), g_ref[0]) * (1.0 + sc_ref[...]) + sh_ref[...]
    hb = _bf(h)
    off = SEG_A + SEG_B + SEG_C
    od_ref[...] = _sgu(_dot(hb, w_scr[:, off:off + SEG_D]), sg_ref[0], sw_ref, sbt_ref[0])
    off = 0
    for ref in (pa_ref, pb_ref, pc_ref):
        n = ref.shape[1]
        ref[...] = _dot(hb, w_scr[:, off:off + n])
        off += n


def _inproj(l, xs, g_mix, mod, w_in_t, sgu_g, sgu_w, sgu_bt):
    def mod_spec(j):
        return pl.BlockSpec((None, None, 1, D_MODEL), lambda i: (l, _row_group(i), 0, j))

    widths = (SEG_A, SEG_B, SEG_C, WIDTH)
    return pl.pallas_call(
        functools.partial(_inproj_kernel, len(xs)),
        grid=(N_TOK // TM,),
        in_specs=_x_specs(len(xs) == 2, 0) + [
            pl.BlockSpec((1, 1, D_MODEL), lambda i: (l, 0, 0)),
            mod_spec(0), mod_spec(1),
            pl.BlockSpec((1, IN_COLS, D_MODEL), lambda i: (l, 0, 0), pipeline_mode=pl.Buffered(1)),
            pl.BlockSpec((1, 1, WIDTH), lambda i: (l, 0, 0)),
            pl.BlockSpec((1, SGU_GROUPS, SGU_CHUNK, SGU_CHUNK), lambda i: (l, 0, 0, 0)),
            pl.BlockSpec((1, SGU_CHUNK, SGU_GROUPS), lambda i: (l, 0, 0)),
        ],
        out_specs=[pl.BlockSpec((TM, n), lambda i: (i, 0)) for n in widths],
        out_shape=[jax.ShapeDtypeStruct((N_TOK, n), jnp.float32) for n in widths],
        scratch_shapes=[pltpu.VMEM((D_MODEL, IN_COLS_P), jnp.bfloat16)],
        compiler_params=_params("arbitrary"),
        name="inproj",
    )(*xs, g_mix, mod, mod, w_in_t, sgu_g, sgu_w, sgu_bt)


C_QN, C_QP, C_CKV, C_KPE = 0, 256, 384, 512


def _stack_heads(qp):
    lo = _lane_range(0, 64)
    return jnp.concatenate([_bf(jnp.where(lo, qp, 0.0)), _bf(jnp.where(lo, 0.0, qp))], axis=0)


def _unstack_heads(o, n):
    return jnp.where(_lane_range(0, 64), o[0:n], o[n:2 * n])


def _pair_t(c_ref, j):
    return jnp.concatenate([c_ref[2 * j], c_ref[2 * j + 1]], axis=0)


def _stack_components(qp):
    return jnp.concatenate([_bf(jnp.where(_lane_range(32 * t, 32 * (t + 1)), qp, 0.0)) for t in range(4)], axis=0)


def _diff_finish(o, n, lam, post, g2):
    den = o[:, LANES:LANES + 1]
    outs = []
    for t in range(2):
        p1 = o[2 * t * n:(2 * t + 1) * n, 0:LANES] * (1.0 / den[2 * t * n:(2 * t + 1) * n])
        p2 = o[(2 * t + 1) * n:(2 * t + 2) * n, 0:LANES] * (lam / den[(2 * t + 1) * n:(2 * t + 2) * n])
        d = p1 - p2
        own = _lane_range(64 * t, 64 * (t + 1))
        ms = jnp.sum(jnp.where(own, d * d, 0.0), axis=-1, keepdims=True) * (1.0 / DIFF_V_DIM)
        outs.append(d * lax.rsqrt(ms + EPS))
    return jnp.where(_lane_range(0, 64), outs[0], outs[1]) * g2 * post


def _mla_queries(qn_pair, qp_all, j):
    halves = []
    for t in range(2):
        h = 2 * j + t
        halves.append(jnp.concatenate([
            _bf(jnp.where(_lane_range(64 * t, 64 * (t + 1)), qn_pair, 0.0)),
            _bf(jnp.where(_lane_range(MLA_ROPE * h, MLA_ROPE * (h + 1)), qp_all, 0.0))], axis=1))
    return jnp.concatenate(halves, axis=0)


def _write_heads_t(p_ref, rows, col0, out_ref, bb):
    xt = p_ref[rows, col0:col0 + WIDTH].T
    for h in range(N_HEADS):
        out_ref[bb, 0, h] = xt[64 * h:64 * (h + 1)]
    _clear_other_layers(out_ref, bb)


def _clear_other_layers(out_ref, bb):
    if out_ref.shape[1] > 1:
        out_ref[bb, 1:] = jnp.zeros(out_ref.shape[1:], jnp.float32)[1:]


def _mix_prompt_kernel(n_prev, *refs):
    ins, outs = refs[:8], refs[8 + n_prev:]
    log = _ShiftLog()
    _mix_prompt_pass(ins, outs, log)

    @pl.when(log.unsafe())
    def _():
        _mix_prompt_pass(ins, outs, None)


def _mix_prompt_pass(ins, outs, log):
    pa_ref, pb_ref, pc_ref, cst_ref, gsub_ref, gckv_ref, wuk_ref, wuv_ref = ins
    o_ref, nak_ref, nav_ref, dk_ref, dv_ref, ckv_ref, kpe_ref = outs
    first_pass = log is not None
    c_a = HEAD_DIM ** -0.5 * LOG2E
    c_b = DIFF_QK_DIM ** -0.5 * LOG2E
    c_c = (MLA_NOPE + MLA_ROPE) ** -0.5 * LOG2E
    lam = cst_ref[0, 0:1, 0:1]
    post = cst_ref[0, 1:2, 0:1]
    wuk, wuv = _bf(wuk_ref[0]), _bf(wuv_ref[0])

    def sequence(bb, log):
        rows = pl.ds(pl.multiple_of(bb * SEQ, SEQ), SEQ)
        for j in range(N_PAIRS):
            cols = slice(LANES * j, LANES * (j + 1))
            k = _bf(pa_ref[rows, WIDTH + LANES * j:WIDTH + LANES * (j + 1)])
            v = _with_ones(_bf(pa_ref[rows, 2 * WIDTH + LANES * j:2 * WIDTH + LANES * (j + 1)]))
            qs = _stack_heads(pa_ref[rows, cols] * c_a)
            kmaxes = [_largest_norm(_sumsq_rows(k, lo, hi)) for lo, hi in HEAD_LANES]
            o = _attend([_dot_nt(qs, k)], [v], log, _bound(log, qs, kmaxes))
            o_ref[rows, cols] = _unstack_heads(_normalised(o), SEQ)
        if first_pass:
            _write_heads_t(pa_ref, rows, WIDTH, nak_ref, bb)
            _write_heads_t(pa_ref, rows, 2 * WIDTH, nav_ref, bb)
        for j in range(N_PAIRS):
            cols = slice(LANES * j, LANES * (j + 1))
            k = _bf(pb_ref[rows, WIDTH + LANES * j:WIDTH + LANES * (j + 1)])
            v = _with_ones(_bf(pb_ref[rows, 2 * WIDTH + LANES * j:2 * WIDTH + LANES * (j + 1)]))
            qs = _stack_components(pb_ref[rows, cols] * c_b)
            kmaxes = [_largest_norm(_sumsq_rows(k, lo, hi)) for lo, hi in COMPONENT_LANES]
            o = _attend([_dot_nt(qs, k)], [v], log, _bound(log, qs, kmaxes))
            o_ref[rows, WIDTH + LANES * j:WIDTH + LANES * (j + 1)] = _diff_finish(o, SEQ, lam, post, gsub_ref[0])
        if first_pass:
            _write_heads_t(pb_ref, rows, WIDTH, dk_ref, bb)
            _write_heads_t(pb_ref, rows, 2 * WIDTH, dv_ref, bb)
        ckv = _rms(pc_ref[rows, C_CKV:C_CKV + MLA_KV_RANK], gckv_ref[0])
        kpe_slot = pc_ref[rows, C_KPE:C_KPE + LANES]
        if first_pass:
            ckv_ref[bb, 0] = ckv
            _clear_other_layers(ckv_ref, bb)
            kpe_ref[bb, 0] = kpe_slot.T[0:MLA_ROPE]
            _clear_other_layers(kpe_ref, bb)
        ckv_b = _bf(ckv)
        kn = _bf(_dot(ckv_b, wuk))
        vv = _bf(_dot(ckv_b, wuv))
        kpe4 = _bf(_tile4(kpe_slot))
        qn = pc_ref[rows, C_QN:C_QN + WIDTH] * c_c
        qp = pc_ref[rows, C_QP:C_QP + LANES] * c_c
        kpe_sq = _sumsq_rows(kpe4, 0, MLA_ROPE)
        for j in range(N_PAIRS):
            cols = slice(LANES * j, LANES * (j + 1))
            k = jnp.concatenate([kn[:, cols], kpe4], axis=1)
            qs = _mla_queries(qn[:, cols], qp, j)
            kmaxes = [_largest_norm(_sumsq_rows(kn[:, cols], lo, hi) + kpe_sq) for lo, hi in HEAD_LANES]
            o = _attend([_dot_nt(qs, k)], [_with_ones(vv[:, cols])], log, _bound(log, qs, kmaxes))
            o_ref[rows, 2 * WIDTH + LANES * j:2 * WIDTH + LANES * (j + 1)] = _unstack_heads(_normalised(o), SEQ)

    _loop(PB, log, sequence)


def _mix_prompt(l, pa, pb, pc, cst, g_sub2, g_ckv, w_uk, w_uv, prev):
    n_prev = len(prev)
    tails = [(NA_HEADS, HEAD_DIM, SEQ)] * 2 + [(DIFF_HEADS, 64, SEQ)] * 2 + [(SEQ, MLA_KV_RANK), (MLA_ROPE, SEQ)]

    def cache_spec(tail):
        if l == 0:
            return pl.BlockSpec((PB, DEPTH) + tail, lambda b: (b, 0) + (0,) * len(tail))
        return pl.BlockSpec((PB, 1) + tail, lambda b: (b, l) + (0,) * len(tail))

    def rows(width):
        return pl.BlockSpec((PB * SEQ, width), lambda b: (b, 0))

    def layer(*tail):
        return pl.BlockSpec((1,) + tail, lambda b: (l,) + (0,) * len(tail))

    return pl.pallas_call(
        functools.partial(_mix_prompt_kernel, n_prev),
        grid=(BATCH // PB,),
        in_specs=[rows(SEG_A), rows(SEG_B), rows(SEG_C), layer(2, LANES), layer(1, LANES), layer(1, MLA_KV_RANK),
                  layer(MLA_KV_RANK, WIDTH), layer(MLA_KV_RANK, WIDTH)] + [pl.BlockSpec(memory_space=pl.ANY)] * n_prev,
        out_specs=[rows(O_ATT)] + [cache_spec(t) for t in tails],
        out_shape=[jax.ShapeDtypeStruct((N_PROMPT, O_ATT), jnp.float32)]
        + [jax.ShapeDtypeStruct((BATCH, DEPTH) + t, jnp.float32) for t in tails],
        input_output_aliases={8 + i: 1 + i for i in range(n_prev)},
        compiler_params=_params("parallel"),
        name="mix_prompt",
    )(pa, pb, pc, cst, g_sub2, g_ckv, w_uk, w_uv, *prev)


def _na_row_groups():
    kh = min(NA_WIN_ROWS, GRID_ROWS)
    r0s = [min(max(r - kh // 2, 0), GRID_ROWS - kh) for r in range(GRID_ROWS)]
    groups = []
    for r, r0 in enumerate(r0s):
        if groups and groups[-1][2] == r0:
            groups[-1][1] = r
        else:
            groups.append([r, r, r0])
    return kh, [tuple(g) for g in groups]


def _na_sample(pa_ref, ck_ref, cv_ref, tt_ref, o_ref, log):
    c = HEAD_DIM ** -0.5 * LOG2E
    kh, groups = _na_row_groups()
    lk = kh * GRID_W
    edge = [g for g in groups if g[1] > g[0]]
    inner = [g for g in groups if g[1] == g[0]]
    depth = inner[0][0] - inner[0][2]
    assert all(g[0] - g[2] == depth for g in inner) and [g[0] for g in inner] == list(range(inner[0][0], inner[-1][0] + 1))

    def in_window(n):
        cq = lax.broadcasted_iota(jnp.int32, (n, lk), 0) & (GRID_W - 1)
        ck = lax.broadcasted_iota(jnp.int32, (n, lk), 1) & (GRID_W - 1)
        c0 = jnp.clip(cq - NA_WIN_COLS // 2, 0, GRID_W - NA_WIN_COLS)
        return (ck >= c0) & (ck < c0 + NA_WIN_COLS)

    for j in range(N_PAIRS):
        cols = slice(LANES * j, LANES * (j + 1))
        kcols = slice(WIDTH + LANES * j, WIDTH + LANES * (j + 1))
        vcols = slice(2 * WIDTH + LANES * j, 2 * WIDTH + LANES * (j + 1))
        kc_t = _bf(_pair_t(ck_ref, j))
        vc = _with_ones(_bf(_pair_t(cv_ref, j).T))
        kmaxes = [_largest_norm(_sumsq_rows(_bf(pa_ref[:, kcols]), lo, hi), _sumsq_cols(kc_t, lo, hi))
                  for lo, hi in HEAD_LANES]
        tmax = functools.reduce(jnp.maximum, [tt_ref[2 * j + t, a] for t in range(2) for a in range(N_DROW - 1)])
        bplus = jnp.maximum(jnp.max(jnp.max(tmax, axis=-1, keepdims=True), axis=0, keepdims=True), 0.0)

        def group(row0, key0, offsets, log):
            n = len(offsets) * GRID_W
            rows, keys = pl.ds(row0, n), pl.ds(key0, lk)
            qg = _stack_heads(pa_ref[rows, cols] * c)
            k = _bf(pa_ref[keys, kcols])
            v = _with_ones(_bf(pa_ref[keys, vcols]))
            bias = jnp.concatenate([
                jnp.concatenate([tt_ref[2 * j + t, 2 * i - off + NA_WIN_ROWS - 1] for i in range(kh // 2)], axis=1)
                for t in range(2) for off in offsets], axis=0)
            s_loc = jnp.where(in_window(2 * n), _dot_nt(qg, k) + bias, NEG_INF)
            o = _attend([_dot(qg, kc_t), s_loc], [vc, v], log, _bound(log, qg, kmaxes, bplus))
            o_ref[rows, cols] = _unstack_heads(_normalised(o), n)

        for (r_lo, r_hi, r0) in edge:
            group(r_lo * GRID_W, r0 * GRID_W, [r - r0 for r in range(r_lo, r_hi + 1)], log)

        def inner_row(i, log):
            r = inner[0][0] + i
            group(pl.multiple_of(r * GRID_W, GRID_W), pl.multiple_of((r - depth) * GRID_W, GRID_W), [depth], log)

        _loop(len(inner), log, inner_row)


def _diff_sample(pb_ref, ck_ref, cv_ref, cos_ref, sin_ref, cst_ref, g_ref, o_ref, log):
    c = DIFF_QK_DIM ** -0.5 * LOG2E
    lam = cst_ref[0, 0:1, 0:1]
    post = cst_ref[0, 1:2, 0:1]
    for j in range(N_PAIRS):
        cols = slice(LANES * j, LANES * (j + 1))
        k_new = _bf(_rope(pb_ref[:, WIDTH + LANES * j:WIDTH + LANES * (j + 1)], cos_ref[:, cols], sin_ref[:, cols]))
        kc_t = _bf(_pair_t(ck_ref, j))
        vc = _with_ones(_bf(_pair_t(cv_ref, j).T))
        v = _with_ones(_bf(pb_ref[:, 2 * WIDTH + LANES * j:2 * WIDTH + LANES * (j + 1)]))
        kmaxes = [_largest_norm(_sumsq_rows(k_new, lo, hi), _sumsq_cols(kc_t, lo, hi)) for lo, hi in COMPONENT_LANES]

        def block(qi, log):
            rows = pl.ds(pl.multiple_of(qi * QB, QB), QB)
            qs = _stack_components(_rope(pb_ref[rows, cols], cos_ref[rows, cols], sin_ref[rows, cols]) * c)
            o = _attend([_dot(qs, kc_t), _dot_nt(qs, k_new)], [vc, v], log, _bound(log, qs, kmaxes))
            o_ref[rows, WIDTH + LANES * j:WIDTH + LANES * (j + 1)] = _diff_finish(o, QB, lam, post, g_ref[0])

        _loop(DEC_SEQ // QB, log, block)


def _mla_sample(pc_ref, cckv_ref, ckpe_ref, cosq_ref, sinq_ref, cosk_ref, sink_ref, gckv_ref, wuk_ref, wuv_ref, o_ref,
                log):
    c = (MLA_NOPE + MLA_ROPE) ** -0.5 * LOG2E
    wuk, wuv = _bf(wuk_ref[0]), _bf(wuv_ref[0])
    ckv_new = _bf(_rms(pc_ref[:, C_CKV:C_CKV + MLA_KV_RANK], gckv_ref[0]))
    ckv_old = _bf(cckv_ref[...])
    kpe_new = _bf(_tile4(_rope(pc_ref[:, C_KPE:C_KPE + LANES], cosk_ref[...], sink_ref[...])))
    kpe_old = _bf(jnp.concatenate([ckpe_ref[...]] * MLA_HEADS, axis=0).T)
    kn_new, kn_old = _bf(_dot(ckv_new, wuk)), _bf(_dot(ckv_old, wuk))
    v_new, v_old = _bf(_dot(ckv_new, wuv)), _bf(_dot(ckv_old, wuv))
    kpe_sq_new, kpe_sq_old = _sumsq_rows(kpe_new, 0, MLA_ROPE), _sumsq_rows(kpe_old, 0, MLA_ROPE)
    for j in range(N_PAIRS):
        cols = slice(LANES * j, LANES * (j + 1))
        k_old = jnp.concatenate([kn_old[:, cols], kpe_old], axis=1)
        k_new = jnp.concatenate([kn_new[:, cols], kpe_new], axis=1)
        vo, vn = _with_ones(v_old[:, cols]), _with_ones(v_new[:, cols])
        kmaxes = [_largest_norm(_sumsq_rows(kn_old[:, cols], lo, hi) + kpe_sq_old,
                                _sumsq_rows(kn_new[:, cols], lo, hi) + kpe_sq_new) for lo, hi in HEAD_LANES]

        def block(qi, log):
            rows = pl.ds(pl.multiple_of(qi * QB, QB), QB)
            qn = pc_ref[rows, C_QN + LANES * j:C_QN + LANES * (j + 1)] * c
            qp = _rope(pc_ref[rows, C_QP:C_QP + LANES], cosq_ref[rows, :], sinq_ref[rows, :]) * c
            qs = _mla_queries(qn, qp, j)
            o = _attend([_dot_nt(qs, k_old), _dot_nt(qs, k_new)], [vo, vn], log, _bound(log, qs, kmaxes))
            o_ref[rows, 2 * WIDTH + LANES * j:2 * WIDTH + LANES * (j + 1)] = _unstack_heads(_normalised(o), QB)

        _loop(DEC_SEQ // QB, log, block)


def _mix_sample_kernel(pa_ref, pb_ref, pc_ref, cnak_ref, cnav_ref, cdk_ref, cdv_ref, cckv_ref, ckpe_ref, tt_ref,
                       cosb_ref, sinb_ref, cosq_ref, sinq_ref, cosk_ref, sink_ref,
                       cst_ref, gsub_ref, gckv_ref, wuk_ref, wuv_ref, o_ref):
    def run(log):
        _na_sample(pa_ref, cnak_ref, cnav_ref, tt_ref, o_ref, log)
        _diff_sample(pb_ref, cdk_ref, cdv_ref, cosb_ref, sinb_ref, cst_ref, gsub_ref, o_ref, log)
        _mla_sample(pc_ref, cckv_ref, ckpe_ref, cosq_ref, sinq_ref, cosk_ref, sink_ref, gckv_ref, wuk_ref, wuv_ref,
                    o_ref, log)

    log = _ShiftLog()
    run(log)

    @pl.when(log.unsafe())
    def _():
        run(None)


def _mix_sample(l, pa, pb, pc, caches_t, tt, tables, cst, g_sub2, g_ckv, w_uk, w_uv):
    first = N_PROMPT // DEC_SEQ

    def rows(width):
        return pl.BlockSpec((DEC_SEQ, width), lambda b: (first + b, 0))

    def cache(*tail):
        return pl.BlockSpec((None, None) + tail, lambda b: (b, l) + (0,) * len(tail))

    def layer(*tail):
        return pl.BlockSpec((1,) + tail, lambda b: (l,) + (0,) * len(tail))

    def table(width):
        return pl.BlockSpec((DEC_SEQ, width), lambda b: (0, 0))

    kv_t = cache(N_HEADS, 64, PAST_LEN)
    return pl.pallas_call(
        _mix_sample_kernel,
        grid=(DEC_BATCH,),
        in_specs=[rows(SEG_A), rows(SEG_B), rows(SEG_C), kv_t, kv_t, kv_t, kv_t,
                  cache(PAST_LEN, MLA_KV_RANK), cache(MLA_ROPE, PAST_LEN),
                  pl.BlockSpec((None, NA_HEADS, N_DROW - 1, GRID_W, LANES), lambda b: (l, 0, 0, 0, 0)),
                  table(WIDTH), table(WIDTH), table(LANES), table(LANES), table(LANES), table(LANES),
                  layer(2, LANES), layer(1, LANES), layer(1, MLA_KV_RANK),
                  layer(MLA_KV_RANK, WIDTH), layer(MLA_KV_RANK, WIDTH)],
        out_specs=pl.BlockSpec((DEC_SEQ, O_ATT), lambda b: (b, 0)),
        out_shape=jax.ShapeDtypeStruct((N_SAMPLE, O_ATT), jnp.float32),
        compiler_params=_params("parallel"),
        name="mix_sample",
    )(pa, pb, pc, *caches_t, tt, *tables, cst, g_sub2, g_ckv, w_uk, w_uv)


FF_CHUNK = 1024


def _outffn_kernel(n_x, first, final, *refs):
    x_refs, o_refs = refs[:n_x], refs[n_x:n_x + 2]
    (od_ref, wout_ref, g1_ref, gffn_ref, sh2_ref, sc2_ref, g2_ref, w1_ref, w2_ref, gfin_ref, y_ref) = refs[n_x + 2:]
    acc = (_dot(_bf(_read_tile(o_refs, first)), wout_ref[0, 0:O_ATT, :])
           + _dot(_bf(od_ref[...]), wout_ref[0, O_ATT:O_ATT + WIDTH, :]))
    x1 = _read_tile(x_refs, first) + g1_ref[...] * acc
    hf = _bf(_rms(x1, gffn_ref[0]) * (1.0 + sc2_ref[...]) + sh2_ref[...])
    acc = jnp.zeros((TM, D_MODEL), jnp.float32)
    for c in range(D_FF // FF_CHUNK):
        cols = slice(FF_CHUNK * c, FF_CHUNK * (c + 1))
        a = jnp.square(jnp.maximum(_dot(hf, w1_ref[0, :, cols]), 0.0))
        acc += _dot(_bf(a), w2_ref[0, cols, :])
    y = x1 + g2_ref[...] * acc
    if final:
        y = _rms(y, gfin_ref[...])
    y_ref[...] = y


def _outffn(l, xs, o_p, o_s, od, w_out, g_ffn, mod, w1, w2, g_final, first, n_tiles):
    def mod_spec(j):
        return pl.BlockSpec((None, None, 1, D_MODEL), lambda i: (l, _row_group(first + i), 0, j))

    def resident(shape):
        return pl.BlockSpec(shape, lambda i: (l,) + (0,) * (len(shape) - 1), pipeline_mode=pl.Buffered(1))

    return pl.pallas_call(
        functools.partial(_outffn_kernel, len(xs), first, l == DEPTH - 1),
        grid=(n_tiles,),
        in_specs=_x_specs(len(xs) == 2, first) + _split_specs(O_ATT, first) + [
            pl.BlockSpec((TM, WIDTH), lambda i: (first + i, 0)),
            resident((1, 4 * WIDTH, D_MODEL)),
            mod_spec(2),
            pl.BlockSpec((1, 1, D_MODEL), lambda i: (l, 0, 0)),
            mod_spec(3), mod_spec(4), mod_spec(5),
            resident((1, D_MODEL, D_FF)),
            resident((1, D_FF, D_MODEL)),
            pl.BlockSpec((1, D_MODEL), lambda i: (0, 0)),
        ],
        out_specs=pl.BlockSpec((TM, D_MODEL), lambda i: (i, 0)),
        out_shape=jax.ShapeDtypeStruct((n_tiles * TM, D_MODEL), jnp.float32),
        compiler_params=_params("parallel"),
        name="outffn",
    )(*xs, o_p, o_s, od, w_out, mod, g_ffn, mod, mod, mod, w1, w2, g_final)


def _rope32_tables():
    t = np.arange(DEC_SEQ)
    rows, cols = (t // GRID_W).astype(np.float64), (t % GRID_W).astype(np.float64)
    half = 8
    freqs = ROPE_BASE ** (-np.arange(half, dtype=np.float64) / half)
    cos, sin = [], []
    for pos in (rows, cols):
        ang = pos[:, None] * freqs[None, :]
        cos += [np.cos(ang), np.cos(ang)]
        sin += [-np.sin(ang), np.sin(ang)]
    return np.concatenate(cos, axis=1).astype(np.float32), np.concatenate(sin, axis=1).astype(np.float32)


def _rope_tables():
    c32, s32 = _rope32_tables()
    tile = lambda a, n: np.tile(a, (1, n))
    pad = np.zeros((DEC_SEQ, 96), np.float32)
    cos_k = np.concatenate([c32, pad + 1.0], axis=1)
    sin_k = np.concatenate([s32, pad], axis=1)
    return (tile(c32, 8), tile(s32, 8),
            tile(c32, 4), tile(s32, 4),
            cos_k, sin_k)


def kernel(x_prompt, x_sample, cache_na_k, cache_na_v, cache_diff_k, cache_diff_v, cache_mla_ckv, cache_mla_kpe, c, c_ctx, w_ada, b_ada, g_mix, g_ffn, w_in, w_out, na_rpb, diff_lq1, diff_lk1, diff_lq2, diff_lk2, diff_g_subln, mla_g_ckv, mla_w_uk, mla_w_uv, sgu_g, sgu_w, sgu_b, w_ff1, w_ff2, g_final):
    f32 = jnp.float32
    m = jnp.concatenate([c_ctx[None, :], c, jnp.zeros((N_MOD_ROWS - 1 - DEC_BATCH, D_MODEL), f32)], axis=0)
    mod = _ada(m, w_ada, b_ada).reshape(DEPTH, N_MOD_ROWS, 1, 6 * D_MODEL)
    cst = _lam_consts(diff_lq1, diff_lk1, diff_lq2, diff_lk2)
    tt = _bias_tiles(na_rpb)
    tables = [jnp.asarray(t) for t in _rope_tables()]

    t_last = lambda a: jnp.swapaxes(a, -1, -2)
    w_in_t = t_last(w_in)
    caches_t = (t_last(cache_na_k), t_last(cache_na_v), t_last(cache_diff_k), t_last(cache_diff_v),
                cache_mla_ckv, t_last(cache_mla_kpe))
    w_out_b, w1_b, w2_b = _bf(w_out), _bf(w_ff1), _bf(w_ff2)
    g_mix3 = g_mix.reshape(DEPTH, 1, D_MODEL)
    g_ffn3 = g_ffn.reshape(DEPTH, 1, D_MODEL)
    g_sub2 = jnp.tile(diff_g_subln, (1, 2)).reshape(DEPTH, 1, LANES)
    g_ckv3 = mla_g_ckv.reshape(DEPTH, 1, MLA_KV_RANK)
    sgu_g3 = sgu_g.reshape(DEPTH, 1, WIDTH)
    sgu_bt = sgu_b.transpose(0, 2, 1)
    g_fin2 = g_final.reshape(1, D_MODEL)

    xs = (x_prompt.reshape(N_PROMPT, D_MODEL), x_sample.reshape(N_SAMPLE, D_MODEL))
    new = ()
    for l in range(DEPTH):
        pa, pb, pc, od = _inproj(l, xs, g_mix3, mod, w_in_t, sgu_g3, sgu_w, sgu_bt)
        o_p, *new = _mix_prompt(l, pa, pb, pc, cst, g_sub2, g_ckv3, mla_w_uk, mla_w_uv, new)
        o_s = _mix_sample(l, pa, pb, pc, caches_t, tt, tables, cst, g_sub2, g_ckv3, mla_w_uk, mla_w_uv)
        ffn = functools.partial(_outffn, l, xs, o_p, o_s, od, w_out_b, g_ffn3, mod, w1_b, w2_b, g_fin2)
        if l < DEPTH - 1:
            xs = (ffn(0, TILES_PROMPT + TILES_SAMPLE),)
        else:
            xs = (ffn(0, TILES_PROMPT), ffn(TILES_PROMPT, TILES_SAMPLE))
    y_prompt = xs[0].reshape(BATCH, SEQ, D_MODEL)
    y_sample = xs[1].reshape(DEC_BATCH, DEC_SEQ, D_MODEL)
    na_k, na_v, diff_k, diff_v, mla_ckv, mla_kpe = new
    return (y_prompt, y_sample, t_last(na_k), t_last(na_v), t_last(diff_k), t_last(diff_v), mla_ckv, t_last(mla_kpe))
```

```python
import functools
import math

import numpy as np
import jax
import jax.numpy as jnp
from jax import lax
from jax.experimental import pallas as pl
from jax.experimental.pallas import tpu as pltpu

D_MODEL = 1024
BATCH = 16
SEQ = 256
DEPTH = 4
DEC_BATCH = 2
DEC_SEQ = 1024
PAST_LEN = 512
GRID_W = 64
GRID_ROWS = DEC_SEQ // GRID_W
HEAD_DIM = 64
NA_HEADS = 4
NA_WIN_ROWS = 8
NA_WIN_COLS = 16
DIFF_HEADS = 4
DIFF_QK_DIM = 32
DIFF_V_DIM = 64
MLA_HEADS = 4
MLA_NOPE = 64
MLA_ROPE = 32
MLA_V = 64
MLA_KV_RANK = 128
SGU_GROUPS = 4
SGU_GROUP_DIM = 64
SGU_CHUNK = 128
D_FF = 4 * D_MODEL
ROPE_BASE = 10000.0
EPS = 1e-6
NEG_INF = -1e30
LOG2E = 1.4426950408889634

N_HEADS = 4
N_PAIRS = N_HEADS // 2
LANES = 128
WIDTH = 256
N_PROMPT = BATCH * SEQ
N_SAMPLE = DEC_BATCH * DEC_SEQ
N_TOK = N_PROMPT + N_SAMPLE
N_MOD_ROWS = 8

SEG_A = 3 * WIDTH
SEG_B = 3 * WIDTH
SEG_C = 640
SEG_D = 2 * WIDTH
SEG_C_PAD = 96
IN_COLS_P = SEG_A + SEG_B + SEG_C + SEG_D
O_ATT = 3 * WIDTH

TM = 512
TILES_PROMPT = N_PROMPT // TM
TILES_SAMPLE = N_SAMPLE // TM
PB = 2
QB = 256
VMEM_LIMIT = 56 * 1024 * 1024


def _bf(x):
    return x.astype(jnp.bfloat16)


def _dot(a, b):
    return jnp.dot(a, b, preferred_element_type=jnp.float32)


def _dot_nt(a, b):
    return lax.dot_general(a, b, (((1,), (1,)), ((), ())), preferred_element_type=jnp.float32)


def _rms(x, g):
    ms = jnp.mean(x * x, axis=-1, keepdims=True)
    return x * lax.rsqrt(ms + EPS) * g


def _lane_range(lo, hi, width=LANES):
    lane = lax.broadcasted_iota(jnp.int32, (1, width), 1)
    return (lane >= lo) & (lane < hi)


def _with_ones(v):
    lane = lax.broadcasted_iota(jnp.int32, (v.shape[0], LANES), 1)
    return jnp.concatenate([v, jnp.where(lane == 0, 1.0, 0.0).astype(jnp.bfloat16)], axis=1)


def _attend(scores, values, log=None, bound=None):
    if bound is None:
        m = functools.reduce(jnp.maximum, [jnp.max(s, axis=-1, keepdims=True) for s in scores])
    else:
        m = bound
        seen = jnp.max(scores[0][:, 0:LANES], axis=-1, keepdims=True)
        log.worst = jnp.maximum(log.worst, jnp.max(m - seen, axis=0, keepdims=True))
    return functools.reduce(lambda a, b: a + b, [_dot(_bf(jnp.exp2(s - m)), v) for s, v in zip(scores, values)])


BOUND_SLACK = 1.02
OVERSHOOT_LIMIT = 100.0


class _ShiftLog:
    def __init__(self, worst=None):
        self.worst = jnp.zeros((1, 1), jnp.float32) if worst is None else worst

    def unsafe(self):
        return jnp.logical_not(self.worst[0, 0] < OVERSHOOT_LIMIT)


def _bound(log, qs, kmaxes, extra=0.0):
    if log is None:
        return None
    n = qs.shape[0] // len(kmaxes)
    kmax = jnp.concatenate([jnp.broadcast_to(km * BOUND_SLACK, (n, 1)) for km in kmaxes], axis=0)
    return _row_norm(qs) * kmax + extra


def _sumsq_rows(k, lo, hi):
    kf = k.astype(jnp.float32)
    return jnp.sum(jnp.where(_lane_range(lo, hi, k.shape[1]), kf * kf, 0.0), axis=-1, keepdims=True)


def _sumsq_cols(kt, lo, hi):
    kf = kt[lo:hi].astype(jnp.float32)
    return jnp.sum(kf * kf, axis=0, keepdims=True)


def _largest_norm(*sumsqs):
    return jnp.sqrt(functools.reduce(jnp.maximum, [jnp.max(s, axis=(0, 1), keepdims=True) for s in sumsqs]))


HEAD_LANES = ((0, 64), (64, 128))
COMPONENT_LANES = tuple((32 * t, 32 * (t + 1)) for t in range(4))


FAST_UNROLL = 2


def _loop(n, log, body):
    unroll = 1 if log is None else min(FAST_UNROLL, n)

    def step(t, worst):
        inner = None if log is None else _ShiftLog(worst)
        for u in range(unroll):
            body(t * unroll + u, inner)
        return worst if inner is None else inner.worst

    worst = lax.fori_loop(0, n // unroll, step, jnp.zeros((1, 1), jnp.float32) if log is None else log.worst)
    if log is not None:
        log.worst = worst
    for i in range(n - n % unroll, n):
        body(i, log)


def _aligned(start, multiple):
    return start if isinstance(start, int) else pl.multiple_of(start, multiple)


def _row_norm(q):
    qf = q.astype(jnp.float32)
    return jnp.sqrt(jnp.sum(qf * qf, axis=-1, keepdims=True))


def _normalised(o_ext):
    return o_ext[:, 0:LANES] * (1.0 / o_ext[:, LANES:LANES + 1])


def _swap8(x):
    lane = lax.broadcasted_iota(jnp.int32, (1, LANES), 1)
    return jnp.where((lane & 15) < 8, pltpu.roll(x, LANES - 8, 1), pltpu.roll(x, 8, 1))


def _rope(x, cos, sin):
    outs = []
    for c in range(x.shape[1] // LANES):
        sl = slice(LANES * c, LANES * (c + 1))
        xc = x[:, sl]
        outs.append(xc * cos[:, sl] + _swap8(xc) * sin[:, sl])
    return outs[0] if len(outs) == 1 else jnp.concatenate(outs, axis=1)


def _tile4(x):
    return x + pltpu.roll(x, 32, 1) + pltpu.roll(x, 64, 1) + pltpu.roll(x, 96, 1)


def _params(*sem):
    return pltpu.CompilerParams(dimension_semantics=sem, vmem_limit_bytes=VMEM_LIMIT)


ADA_TN = 1536


def _ada_kernel(m_ref, w_ref, b_ref, o_ref):
    m = m_ref[...]
    s = m * jax.nn.sigmoid(m)
    o_ref[0] = _dot(_bf(s), _bf(w_ref[0])) + b_ref[0]


def _ada(m, w_ada, b_ada):
    n = 6 * D_MODEL
    return pl.pallas_call(
        _ada_kernel,
        grid=(DEPTH, n // ADA_TN),
        in_specs=[
            pl.BlockSpec((N_MOD_ROWS, D_MODEL), lambda l, j: (0, 0)),
            pl.BlockSpec((1, D_MODEL, ADA_TN), lambda l, j: (l, 0, j)),
            pl.BlockSpec((1, 1, ADA_TN), lambda l, j: (l, 0, j)),
        ],
        out_specs=pl.BlockSpec((1, N_MOD_ROWS, ADA_TN), lambda l, j: (l, 0, j)),
        out_shape=jax.ShapeDtypeStruct((DEPTH, N_MOD_ROWS, n), jnp.float32),
        compiler_params=_params("parallel", "parallel"),
        name="ada",
    )(m, w_ada, b_ada.reshape(DEPTH, 1, n))


def _lam_kernel(lq1_ref, lk1_ref, lq2_ref, lk2_ref, init_ref, o_ref):
    init = init_ref[...]
    a = jnp.exp(jnp.sum(lq1_ref[...] * lk1_ref[...], axis=-1, keepdims=True))
    b = jnp.exp(jnp.sum(lq2_ref[...] * lk2_ref[...], axis=-1, keepdims=True))
    lam = a - b + init
    post = 1.0 - init
    for l in range(DEPTH):
        o_ref[l, 0:1, :] = jnp.broadcast_to(lam[l:l + 1], (1, LANES))
        o_ref[l, 1:2, :] = jnp.broadcast_to(post[l:l + 1], (1, LANES))


def _lam_consts(lq1, lk1, lq2, lk2):
    init = np.array([[0.8 - 0.6 * math.exp(-0.3 * l)] for l in range(DEPTH)], np.float32)
    return pl.pallas_call(
        _lam_kernel,
        out_shape=jax.ShapeDtypeStruct((DEPTH, 2, LANES), jnp.float32),
        name="diff_lambda",
    )(lq1, lk1, lq2, lk2, jnp.asarray(init))


N_DROW = 2 * NA_WIN_ROWS - 1
N_DCOL = 2 * NA_WIN_COLS - 1


def _bias_kernel(rpb_ref, o_ref):
    l = pl.program_id(0)
    h = pl.program_id(1)
    base = (l * NA_HEADS + h) * (N_DROW * N_DCOL)
    cq = lax.broadcasted_iota(jnp.int32, (GRID_W, LANES), 0)
    lane = lax.broadcasted_iota(jnp.int32, (GRID_W, LANES), 1)
    ck = lane & (GRID_W - 1)
    dcol = jnp.clip(ck - cq, -(NA_WIN_COLS - 1), NA_WIN_COLS - 1) + (NA_WIN_COLS - 1)
    hi = lane >= GRID_W
    for a in range(N_DROW - 1):
        acc = jnp.zeros((GRID_W, LANES), jnp.float32)
        for j in range(N_DCOL):
            lo_v = rpb_ref[base + a * N_DCOL + j]
            hi_v = rpb_ref[base + (a + 1) * N_DCOL + j]
            acc = jnp.where(dcol == j, jnp.where(hi, hi_v, lo_v), acc)
        o_ref[0, 0, a] = acc * LOG2E


def _bias_tiles(na_rpb):
    return pl.pallas_call(
        _bias_kernel,
        grid=(DEPTH, NA_HEADS),
        in_specs=[pl.BlockSpec(memory_space=pltpu.SMEM)],
        out_specs=pl.BlockSpec((1, 1, N_DROW - 1, GRID_W, LANES), lambda l, h: (l, h, 0, 0, 0)),
        out_shape=jax.ShapeDtypeStruct((DEPTH, NA_HEADS, N_DROW - 1, GRID_W, LANES), jnp.float32),
        compiler_params=_params("parallel", "parallel"),
        name="na_bias_tiles",
    )(na_rpb.reshape(-1))


def _row_group(i):
    return jnp.where(i < TILES_PROMPT, 0, 1 + (i - TILES_PROMPT) // (DEC_SEQ // TM))


def _split_specs(width, first):
    return [pl.BlockSpec((TM, width), lambda i: (jnp.minimum(first + i, TILES_PROMPT - 1), 0)),
            pl.BlockSpec((TM, width), lambda i: (jnp.maximum(first + i - TILES_PROMPT, 0), 0))]


def _x_specs(split, first):
    if not split:
        return [pl.BlockSpec((TM, D_MODEL), lambda i: (first + i, 0))]
    return _split_specs(D_MODEL, first)


def _read_tile(refs, first):
    if len(refs) == 1:
        return refs[0][...]
    return jnp.where(first + pl.program_id(0) < TILES_PROMPT, refs[0][...], refs[1][...])


IN_COLS = 2592
IN_QC, IN_CKV, IN_D = 1536, 1920, 2080
TR_ROWS = 256


def _gelu_tanh(x):
    return 0.5 * x * (1.0 + jnp.tanh(math.sqrt(2.0 / math.pi) * (x + 0.044715 * (x * x * x))))


def _sgu(pd, g, w_ref, bt):
    u = _gelu_tanh(pd[:, 0:WIDTH])
    v = _gelu_tanh(pd[:, WIDTH:2 * WIDTH])
    grp = lax.broadcasted_iota(jnp.int32, (1, WIDTH), 1) // SGU_GROUP_DIM
    v2 = v * v
    ms = jnp.zeros_like(v)
    for gi in range(SGU_GROUPS):
        sel = grp == gi
        tot = jnp.sum(jnp.where(sel, v2, 0.0), axis=-1, keepdims=True)
        ms = jnp.where(sel, tot * (1.0 / SGU_GROUP_DIM), ms)
    vg = _bf(v * lax.rsqrt(ms + EPS) * g)
    outs = []
    for c in range(pd.shape[0] // SGU_CHUNK):
        rows = slice(SGU_CHUNK * c, SGU_CHUNK * (c + 1))
        mixed = jnp.zeros((SGU_CHUNK, WIDTH), jnp.float32)
        for gi in range(SGU_GROUPS):
            full = _dot(_bf(w_ref[0, gi]), vg[rows]) + bt[:, gi:gi + 1]
            mixed = jnp.where(grp == gi, full, mixed)
        outs.append(u[rows] * mixed)
    return jnp.concatenate(outs, axis=0)


def _w_in_row_pieces():
    qn = [(IN_QC + 96 * h, MLA_NOPE) for h in range(MLA_HEADS)]
    qp = [(IN_QC + 96 * h + MLA_NOPE, MLA_ROPE) for h in range(MLA_HEADS)]
    seg_c = qn + qp + [(IN_CKV, MLA_KV_RANK + MLA_ROPE)]
    return (0, SEG_A + SEG_B), seg_c, (IN_D, SEG_D)


def _load_w_in(wt_ref, w_scr):
    ab, seg_c, d = _w_in_row_pieces()
    c_rows = jnp.concatenate([wt_ref[0, s:s + n, :] for s, n in seg_c]
                             + [jnp.zeros((SEG_C_PAD, D_MODEL), jnp.float32)], axis=0)
    for t in range(SEG_C // LANES):
        w_scr[:, SEG_A + SEG_B + LANES * t:SEG_A + SEG_B + LANES * (t + 1)] = _bf(c_rows[LANES * t:LANES * (t + 1)].T)
    for (src, n), dst in ((ab, 0), (d, SEG_A + SEG_B + SEG_C)):
        for t in range(n // TR_ROWS):
            rows = wt_ref[0, src + TR_ROWS * t:src + TR_ROWS * (t + 1), :]
            w_scr[:, dst + TR_ROWS * t:dst + TR_ROWS * (t + 1)] = _bf(rows.T)


def _inproj_kernel(n_x, *refs):
    x_refs = refs[:n_x]
    (g_ref, sh_ref, sc_ref, wt_ref, sg_ref, sw_ref, sbt_ref, pa_ref, pb_ref, pc_ref, od_ref, w_scr) = refs[n_x:]

    @pl.when(pl.program_id(0) == 0)
    def _():
        _load_w_in(wt_ref, w_scr)

    h = _rms(_read_tile(x_refs, 0), g_ref[0]) * (1.0 + sc_ref[...]) + sh_ref[...]
    hb = _bf(h)
    off = SEG_A + SEG_B + SEG_C
    od_ref[...] = _sgu(_dot(hb, w_scr[:, off:off + SEG_D]), sg_ref[0], sw_ref, sbt_ref[0])
    off = 0
    for ref in (pa_ref, pb_ref, pc_ref):
        n = ref.shape[1]
        ref[...] = _dot(hb, w_scr[:, off:off + n])
        off += n


def _inproj(l, xs, g_mix, mod, w_in_t, sgu_g, sgu_w, sgu_bt):
    def mod_spec(j):
        return pl.BlockSpec((None, None, 1, D_MODEL), lambda i: (l, _row_group(i), 0, j))

    widths = (SEG_A, SEG_B, SEG_C, WIDTH)
    return pl.pallas_call(
        functools.partial(_inproj_kernel, len(xs)),
        grid=(N_TOK // TM,),
        in_specs=_x_specs(len(xs) == 2, 0) + [
            pl.BlockSpec((1, 1, D_MODEL), lambda i: (l, 0, 0)),
            mod_spec(0), mod_spec(1),
            pl.BlockSpec((1, IN_COLS, D_MODEL), lambda i: (l, 0, 0), pipeline_mode=pl.Buffered(1)),
            pl.BlockSpec((1, 1, WIDTH), lambda i: (l, 0, 0)),
            pl.BlockSpec((1, SGU_GROUPS, SGU_CHUNK, SGU_CHUNK), lambda i: (l, 0, 0, 0)),
            pl.BlockSpec((1, SGU_CHUNK, SGU_GROUPS), lambda i: (l, 0, 0)),
        ],
        out_specs=[pl.BlockSpec((TM, n), lambda i: (i, 0)) for n in widths],
        out_shape=[jax.ShapeDtypeStruct((N_TOK, n), jnp.float32) for n in widths],
        scratch_shapes=[pltpu.VMEM((D_MODEL, IN_COLS_P), jnp.bfloat16)],
        compiler_params=_params("arbitrary"),
        name="inproj",
    )(*xs, g_mix, mod, mod, w_in_t, sgu_g, sgu_w, sgu_bt)


C_QN, C_QP, C_CKV, C_KPE = 0, 256, 384, 512


def _stack_heads(qp):
    lo = _lane_range(0, 64)
    return jnp.concatenate([_bf(jnp.where(lo, qp, 0.0)), _bf(jnp.where(lo, 0.0, qp))], axis=0)


def _unstack_heads(o, n):
    return jnp.where(_lane_range(0, 64), o[0:n], o[n:2 * n])


def _pair_t(c_ref, j):
    return jnp.concatenate([c_ref[2 * j], c_ref[2 * j + 1]], axis=0)


def _stack_components(qp):
    return jnp.concatenate([_bf(jnp.where(_lane_range(32 * t, 32 * (t + 1)), qp, 0.0)) for t in range(4)], axis=0)


def _diff_finish(o, n, lam, post, g2):
    den = o[:, LANES:LANES + 1]
    outs = []
    for t in range(2):
        p1 = o[2 * t * n:(2 * t + 1) * n, 0:LANES] * (1.0 / den[2 * t * n:(2 * t + 1) * n])
        p2 = o[(2 * t + 1) * n:(2 * t + 2) * n, 0:LANES] * (lam / den[(2 * t + 1) * n:(2 * t + 2) * n])
        d = p1 - p2
        own = _lane_range(64 * t, 64 * (t + 1))
        ms = jnp.sum(jnp.where(own, d * d, 0.0), axis=-1, keepdims=True) * (1.0 / DIFF_V_DIM)
        outs.append(d * lax.rsqrt(ms + EPS))
    return jnp.where(_lane_range(0, 64), outs[0], outs[1]) * g2 * post


def _mla_queries(qn_pair, qp_all, j):
    halves = []
    for t in range(2):
        h = 2 * j + t
        halves.append(jnp.concatenate([
            _bf(jnp.where(_lane_range(64 * t, 64 * (t + 1)), qn_pair, 0.0)),
            _bf(jnp.where(_lane_range(MLA_ROPE * h, MLA_ROPE * (h + 1)), qp_all, 0.0))], axis=1))
    return jnp.concatenate(halves, axis=0)


def _write_heads_t(p_ref, rows, col0, out_ref, bb):
    xt = p_ref[rows, col0:col0 + WIDTH].T
    for h in range(N_HEADS):
        out_ref[bb, 0, h] = xt[64 * h:64 * (h + 1)]
    _clear_other_layers(out_ref, bb)


def _clear_other_layers(out_ref, bb):
    if out_ref.shape[1] > 1:
        out_ref[bb, 1:] = jnp.zeros(out_ref.shape[1:], jnp.float32)[1:]


def _mix_prompt_kernel(n_prev, *refs):
    ins, outs = refs[:8], refs[8 + n_prev:]
    log = _ShiftLog()
    _mix_prompt_pass(ins, outs, log)

    @pl.when(log.unsafe())
    def _():
        _mix_prompt_pass(ins, outs, None)


def _mix_prompt_pass(ins, outs, log):
    pa_ref, pb_ref, pc_ref, cst_ref, gsub_ref, gckv_ref, wuk_ref, wuv_ref = ins
    o_ref, nak_ref, nav_ref, dk_ref, dv_ref, ckv_ref, kpe_ref = outs
    first_pass = log is not None
    c_a = HEAD_DIM ** -0.5 * LOG2E
    c_b = DIFF_QK_DIM ** -0.5 * LOG2E
    c_c = (MLA_NOPE + MLA_ROPE) ** -0.5 * LOG2E
    lam = cst_ref[0, 0:1, 0:1]
    post = cst_ref[0, 1:2, 0:1]
    wuk, wuv = _bf(wuk_ref[0]), _bf(wuv_ref[0])

    def sequence(bb, log):
        rows = pl.ds(_aligned(bb * SEQ, SEQ), SEQ)
        for j in range(N_PAIRS):
            cols = slice(LANES * j, LANES * (j + 1))
            k = _bf(pa_ref[rows, WIDTH + LANES * j:WIDTH + LANES * (j + 1)])
            v = _with_ones(_bf(pa_ref[rows, 2 * WIDTH + LANES * j:2 * WIDTH + LANES * (j + 1)]))
            qs = _stack_heads(pa_ref[rows, cols] * c_a)
            kmaxes = [_largest_norm(_sumsq_rows(k, lo, hi)) for lo, hi in HEAD_LANES]
            o = _attend([_dot_nt(qs, k)], [v], log, _bound(log, qs, kmaxes))
            o_ref[rows, cols] = _unstack_heads(_normalised(o), SEQ)
        if first_pass:
            _write_heads_t(pa_ref, rows, WIDTH, nak_ref, bb)
            _write_heads_t(pa_ref, rows, 2 * WIDTH, nav_ref, bb)
        for j in range(N_PAIRS):
            cols = slice(LANES * j, LANES * (j + 1))
            k = _bf(pb_ref[rows, WIDTH + LANES * j:WIDTH + LANES * (j + 1)])
            v = _with_ones(_bf(pb_ref[rows, 2 * WIDTH + LANES * j:2 * WIDTH + LANES * (j + 1)]))
            qs = _stack_components(pb_ref[rows, cols] * c_b)
            kmaxes = [_largest_norm(_sumsq_rows(k, lo, hi)) for lo, hi in COMPONENT_LANES]
            o = _attend([_dot_nt(qs, k)], [v], log, _bound(log, qs, kmaxes))
            o_ref[rows, WIDTH + LANES * j:WIDTH + LANES * (j + 1)] = _diff_finish(o, SEQ, lam, post, gsub_ref[0])
        if first_pass:
            _write_heads_t(pb_ref, rows, WIDTH, dk_ref, bb)
            _write_heads_t(pb_ref, rows, 2 * WIDTH, dv_ref, bb)
        ckv = _rms(pc_ref[rows, C_CKV:C_CKV + MLA_KV_RANK], gckv_ref[0])
        kpe_slot = pc_ref[rows, C_KPE:C_KPE + LANES]
        if first_pass:
            ckv_ref[bb, 0] = ckv
            _clear_other_layers(ckv_ref, bb)
            kpe_ref[bb, 0] = kpe_slot.T[0:MLA_ROPE]
            _clear_other_layers(kpe_ref, bb)
        ckv_b = _bf(ckv)
        kn = _bf(_dot(ckv_b, wuk))
        vv = _bf(_dot(ckv_b, wuv))
        kpe4 = _bf(_tile4(kpe_slot))
        qn = pc_ref[rows, C_QN:C_QN + WIDTH] * c_c
        qp = pc_ref[rows, C_QP:C_QP + LANES] * c_c
        kpe_sq = _sumsq_rows(kpe4, 0, MLA_ROPE)
        for j in range(N_PAIRS):
            cols = slice(LANES * j, LANES * (j + 1))
            k = jnp.concatenate([kn[:, cols], kpe4], axis=1)
            qs = _mla_queries(qn[:, cols], qp, j)
            kmaxes = [_largest_norm(_sumsq_rows(kn[:, cols], lo, hi) + kpe_sq) for lo, hi in HEAD_LANES]
            o = _attend([_dot_nt(qs, k)], [_with_ones(vv[:, cols])], log, _bound(log, qs, kmaxes))
            o_ref[rows, 2 * WIDTH + LANES * j:2 * WIDTH + LANES * (j + 1)] = _unstack_heads(_normalised(o), SEQ)

    _loop(PB, log, sequence)


def _mix_prompt(l, pa, pb, pc, cst, g_sub2, g_ckv, w_uk, w_uv, prev):
    n_prev = len(prev)
    tails = [(NA_HEADS, HEAD_DIM, SEQ)] * 2 + [(DIFF_HEADS, 64, SEQ)] * 2 + [(SEQ, MLA_KV_RANK), (MLA_ROPE, SEQ)]

    def cache_spec(tail):
        if l == 0:
            return pl.BlockSpec((PB, DEPTH) + tail, lambda b: (b, 0) + (0,) * len(tail))
        return pl.BlockSpec((PB, 1) + tail, lambda b: (b, l) + (0,) * len(tail))

    def rows(width):
        return pl.BlockSpec((PB * SEQ, width), lambda b: (b, 0))

    def layer(*tail):
        return pl.BlockSpec((1,) + tail, lambda b: (l,) + (0,) * len(tail))

    return pl.pallas_call(
        functools.partial(_mix_prompt_kernel, n_prev),
        grid=(BATCH // PB,),
        in_specs=[rows(SEG_A), rows(SEG_B), rows(SEG_C), layer(2, LANES), layer(1, LANES), layer(1, MLA_KV_RANK),
                  layer(MLA_KV_RANK, WIDTH), layer(MLA_KV_RANK, WIDTH)] + [pl.BlockSpec(memory_space=pl.ANY)] * n_prev,
        out_specs=[rows(O_ATT)] + [cache_spec(t) for t in tails],
        out_shape=[jax.ShapeDtypeStruct((N_PROMPT, O_ATT), jnp.float32)]
        + [jax.ShapeDtypeStruct((BATCH, DEPTH) + t, jnp.float32) for t in tails],
        input_output_aliases={8 + i: 1 + i for i in range(n_prev)},
        compiler_params=_params("parallel"),
        name="mix_prompt",
    )(pa, pb, pc, cst, g_sub2, g_ckv, w_uk, w_uv, *prev)


def _na_row_groups():
    kh = min(NA_WIN_ROWS, GRID_ROWS)
    r0s = [min(max(r - kh // 2, 0), GRID_ROWS - kh) for r in range(GRID_ROWS)]
    groups = []
    for r, r0 in enumerate(r0s):
        if groups and groups[-1][2] == r0:
            groups[-1][1] = r
        else:
            groups.append([r, r, r0])
    return kh, [tuple(g) for g in groups]


def _na_sample(pa_ref, ck_ref, cv_ref, tt_ref, o_ref, log):
    c = HEAD_DIM ** -0.5 * LOG2E
    kh, groups = _na_row_groups()
    lk = kh * GRID_W
    edge = [g for g in groups if g[1] > g[0]]
    inner = [g for g in groups if g[1] == g[0]]
    depth = inner[0][0] - inner[0][2]
    assert all(g[0] - g[2] == depth for g in inner) and [g[0] for g in inner] == list(range(inner[0][0], inner[-1][0] + 1))

    def in_window(n):
        cq = lax.broadcasted_iota(jnp.int32, (n, lk), 0) & (GRID_W - 1)
        ck = lax.broadcasted_iota(jnp.int32, (n, lk), 1) & (GRID_W - 1)
        c0 = jnp.clip(cq - NA_WIN_COLS // 2, 0, GRID_W - NA_WIN_COLS)
        return (ck >= c0) & (ck < c0 + NA_WIN_COLS)

    for j in range(N_PAIRS):
        cols = slice(LANES * j, LANES * (j + 1))
        kcols = slice(WIDTH + LANES * j, WIDTH + LANES * (j + 1))
        vcols = slice(2 * WIDTH + LANES * j, 2 * WIDTH + LANES * (j + 1))
        kc_t = _bf(_pair_t(ck_ref, j))
        vc = _with_ones(_bf(_pair_t(cv_ref, j).T))
        kmaxes = [_largest_norm(_sumsq_rows(_bf(pa_ref[:, kcols]), lo, hi), _sumsq_cols(kc_t, lo, hi))
                  for lo, hi in HEAD_LANES]
        tmax = functools.reduce(jnp.maximum, [tt_ref[2 * j + t, a] for t in range(2) for a in range(N_DROW - 1)])
        bplus = jnp.maximum(jnp.max(jnp.max(tmax, axis=-1, keepdims=True), axis=0, keepdims=True), 0.0)

        def group(row0, key0, offsets, log):
            n = len(offsets) * GRID_W
            rows, keys = pl.ds(row0, n), pl.ds(key0, lk)
            qg = _stack_heads(pa_ref[rows, cols] * c)
            k = _bf(pa_ref[keys, kcols])
            v = _with_ones(_bf(pa_ref[keys, vcols]))
            bias = jnp.concatenate([
                jnp.concatenate([tt_ref[2 * j + t, 2 * i - off + NA_WIN_ROWS - 1] for i in range(kh // 2)], axis=1)
                for t in range(2) for off in offsets], axis=0)
            s_loc = jnp.where(in_window(2 * n), _dot_nt(qg, k) + bias, NEG_INF)
            o = _attend([_dot(qg, kc_t), s_loc], [vc, v], log, _bound(log, qg, kmaxes, bplus))
            o_ref[rows, cols] = _unstack_heads(_normalised(o), n)

        for (r_lo, r_hi, r0) in edge:
            group(r_lo * GRID_W, r0 * GRID_W, [r - r0 for r in range(r_lo, r_hi + 1)], log)

        def inner_row(i, log):
            r = inner[0][0] + i
            group(_aligned(r * GRID_W, GRID_W), _aligned((r - depth) * GRID_W, GRID_W), [depth], log)

        _loop(len(inner), log, inner_row)


def _diff_sample(pb_ref, ck_ref, cv_ref, cos_ref, sin_ref, cst_ref, g_ref, o_ref, log):
    c = DIFF_QK_DIM ** -0.5 * LOG2E
    lam = cst_ref[0, 0:1, 0:1]
    post = cst_ref[0, 1:2, 0:1]
    for j in range(N_PAIRS):
        cols = slice(LANES * j, LANES * (j + 1))
        k_new = _bf(_rope(pb_ref[:, WIDTH + LANES * j:WIDTH + LANES * (j + 1)], cos_ref[:, cols], sin_ref[:, cols]))
        kc_t = _bf(_pair_t(ck_ref, j))
        vc = _with_ones(_bf(_pair_t(cv_ref, j).T))
        v = _with_ones(_bf(pb_ref[:, 2 * WIDTH + LANES * j:2 * WIDTH + LANES * (j + 1)]))
        kmaxes = [_largest_norm(_sumsq_rows(k_new, lo, hi), _sumsq_cols(kc_t, lo, hi)) for lo, hi in COMPONENT_LANES]

        def block(qi, log):
            rows = pl.ds(_aligned(qi * QB, QB), QB)
            qs = _stack_components(_rope(pb_ref[rows, cols], cos_ref[rows, cols], sin_ref[rows, cols]) * c)
            o = _attend([_dot(qs, kc_t), _dot_nt(qs, k_new)], [vc, v], log, _bound(log, qs, kmaxes))
            o_ref[rows, WIDTH + LANES * j:WIDTH + LANES * (j + 1)] = _diff_finish(o, QB, lam, post, g_ref[0])

        _loop(DEC_SEQ // QB, log, block)


def _mla_sample(pc_ref, cckv_ref, ckpe_ref, cosq_ref, sinq_ref, cosk_ref, sink_ref, gckv_ref, wuk_ref, wuv_ref, o_ref,
                log):
    c = (MLA_NOPE + MLA_ROPE) ** -0.5 * LOG2E
    wuk, wuv = _bf(wuk_ref[0]), _bf(wuv_ref[0])
    ckv_new = _bf(_rms(pc_ref[:, C_CKV:C_CKV + MLA_KV_RANK], gckv_ref[0]))
    ckv_old = _bf(cckv_ref[...])
    kpe_new = _bf(_tile4(_rope(pc_ref[:, C_KPE:C_KPE + LANES], cosk_ref[...], sink_ref[...])))
    kpe_old = _bf(jnp.concatenate([ckpe_ref[...]] * MLA_HEADS, axis=0).T)
    kn_new, kn_old = _bf(_dot(ckv_new, wuk)), _bf(_dot(ckv_old, wuk))
    v_new, v_old = _bf(_dot(ckv_new, wuv)), _bf(_dot(ckv_old, wuv))
    kpe_sq_new, kpe_sq_old = _sumsq_rows(kpe_new, 0, MLA_ROPE), _sumsq_rows(kpe_old, 0, MLA_ROPE)
    for j in range(N_PAIRS):
        cols = slice(LANES * j, LANES * (j + 1))
        k_old = jnp.concatenate([kn_old[:, cols], kpe_old], axis=1)
        k_new = jnp.concatenate([kn_new[:, cols], kpe_new], axis=1)
        vo, vn = _with_ones(v_old[:, cols]), _with_ones(v_new[:, cols])
        kmaxes = [_largest_norm(_sumsq_rows(kn_old[:, cols], lo, hi) + kpe_sq_old,
                                _sumsq_rows(kn_new[:, cols], lo, hi) + kpe_sq_new) for lo, hi in HEAD_LANES]

        def block(qi, log):
            rows = pl.ds(_aligned(qi * QB, QB), QB)
            qn = pc_ref[rows, C_QN + LANES * j:C_QN + LANES * (j + 1)] * c
            qp = _rope(pc_ref[rows, C_QP:C_QP + LANES], cosq_ref[rows, :], sinq_ref[rows, :]) * c
            qs = _mla_queries(qn, qp, j)
            o = _attend([_dot_nt(qs, k_old), _dot_nt(qs, k_new)], [vo, vn], log, _bound(log, qs, kmaxes))
            o_ref[rows, 2 * WIDTH + LANES * j:2 * WIDTH + LANES * (j + 1)] = _unstack_heads(_normalised(o), QB)

        _loop(DEC_SEQ // QB, log, block)


def _mix_sample_kernel(pa_ref, pb_ref, pc_ref, cnak_ref, cnav_ref, cdk_ref, cdv_ref, cckv_ref, ckpe_ref, tt_ref,
                       cosb_ref, sinb_ref, cosq_ref, sinq_ref, cosk_ref, sink_ref,
                       cst_ref, gsub_ref, gckv_ref, wuk_ref, wuv_ref, o_ref):
    def run(log):
        _na_sample(pa_ref, cnak_ref, cnav_ref, tt_ref, o_ref, log)
        _diff_sample(pb_ref, cdk_ref, cdv_ref, cosb_ref, sinb_ref, cst_ref, gsub_ref, o_ref, log)
        _mla_sample(pc_ref, cckv_ref, ckpe_ref, cosq_ref, sinq_ref, cosk_ref, sink_ref, gckv_ref, wuk_ref, wuv_ref,
                    o_ref, log)

    log = _ShiftLog()
    run(log)

    @pl.when(log.unsafe())
    def _():
        run(None)


def _mix_sample(l, pa, pb, pc, caches_t, tt, tables, cst, g_sub2, g_ckv, w_uk, w_uv):
    first = N_PROMPT // DEC_SEQ

    def rows(width):
        return pl.BlockSpec((DEC_SEQ, width), lambda b: (first + b, 0))

    def cache(*tail):
        return pl.BlockSpec((None, None) + tail, lambda b: (b, l) + (0,) * len(tail))

    def layer(*tail):
        return pl.BlockSpec((1,) + tail, lambda b: (l,) + (0,) * len(tail))

    def table(width):
        return pl.BlockSpec((DEC_SEQ, width), lambda b: (0, 0), pipeline_mode=pl.Buffered(1))

    kv_t = cache(N_HEADS, 64, PAST_LEN)
    return pl.pallas_call(
        _mix_sample_kernel,
        grid=(DEC_BATCH,),
        in_specs=[rows(SEG_A), rows(SEG_B), rows(SEG_C), kv_t, kv_t, kv_t, kv_t,
                  cache(PAST_LEN, MLA_KV_RANK), cache(MLA_ROPE, PAST_LEN),
                  pl.BlockSpec((None, NA_HEADS, N_DROW - 1, GRID_W, LANES), lambda b: (l, 0, 0, 0, 0),
                               pipeline_mode=pl.Buffered(1)),
                  table(WIDTH), table(WIDTH), table(LANES), table(LANES), table(LANES), table(LANES),
                  layer(2, LANES), layer(1, LANES), layer(1, MLA_KV_RANK),
                  layer(MLA_KV_RANK, WIDTH), layer(MLA_KV_RANK, WIDTH)],
        out_specs=pl.BlockSpec((DEC_SEQ, O_ATT), lambda b: (b, 0)),
        out_shape=jax.ShapeDtypeStruct((N_SAMPLE, O_ATT), jnp.float32),
        compiler_params=_params("parallel"),
        name="mix_sample",
    )(pa, pb, pc, *caches_t, tt, *tables, cst, g_sub2, g_ckv, w_uk, w_uv)


FF_CHUNK = 1024


def _outffn_kernel(n_x, first, final, *refs):
    x_refs, o_refs = refs[:n_x], refs[n_x:n_x + 2]
    (od_ref, wout_ref, g1_ref, gffn_ref, sh2_ref, sc2_ref, g2_ref, w1_ref, w2_ref, gfin_ref, y_ref) = refs[n_x + 2:]
    acc = (_dot(_bf(_read_tile(o_refs, first)), wout_ref[0, 0:O_ATT, :])
           + _dot(_bf(od_ref[...]), wout_ref[0, O_ATT:O_ATT + WIDTH, :]))
    x1 = _read_tile(x_refs, first) + g1_ref[...] * acc
    hf = _bf(_rms(x1, gffn_ref[0]) * (1.0 + sc2_ref[...]) + sh2_ref[...])
    acc = jnp.zeros((TM, D_MODEL), jnp.float32)
    for c in range(D_FF // FF_CHUNK):
        cols = slice(FF_CHUNK * c, FF_CHUNK * (c + 1))
        a = jnp.square(jnp.maximum(_dot(hf, w1_ref[0, :, cols]), 0.0))
        acc += _dot(_bf(a), w2_ref[0, cols, :])
    y = x1 + g2_ref[...] * acc
    if final:
        y = _rms(y, gfin_ref[...])
    y_ref[...] = y


def _outffn(l, xs, o_p, o_s, od, w_out, g_ffn, mod, w1, w2, g_final, first, n_tiles):
    def mod_spec(j):
        return pl.BlockSpec((None, None, 1, D_MODEL), lambda i: (l, _row_group(first + i), 0, j))

    def resident(shape):
        return pl.BlockSpec(shape, lambda i: (l,) + (0,) * (len(shape) - 1), pipeline_mode=pl.Buffered(1))

    return pl.pallas_call(
        functools.partial(_outffn_kernel, len(xs), first, l == DEPTH - 1),
        grid=(n_tiles,),
        in_specs=_x_specs(len(xs) == 2, first) + _split_specs(O_ATT, first) + [
            pl.BlockSpec((TM, WIDTH), lambda i: (first + i, 0)),
            resident((1, 4 * WIDTH, D_MODEL)),
            mod_spec(2),
            pl.BlockSpec((1, 1, D_MODEL), lambda i: (l, 0, 0)),
            mod_spec(3), mod_spec(4), mod_spec(5),
            resident((1, D_MODEL, D_FF)),
            resident((1, D_FF, D_MODEL)),
            pl.BlockSpec((1, D_MODEL), lambda i: (0, 0)),
        ],
        out_specs=pl.BlockSpec((TM, D_MODEL), lambda i: (i, 0)),
        out_shape=jax.ShapeDtypeStruct((n_tiles * TM, D_MODEL), jnp.float32),
        compiler_params=_params("parallel"),
        name="outffn",
    )(*xs, o_p, o_s, od, w_out, mod, g_ffn, mod, mod, mod, w1, w2, g_final)


def _rope32_tables():
    t = np.arange(DEC_SEQ)
    rows, cols = (t // GRID_W).astype(np.float64), (t % GRID_W).astype(np.float64)
    half = 8
    freqs = ROPE_BASE ** (-np.arange(half, dtype=np.float64) / half)
    cos, sin = [], []
    for pos in (rows, cols):
        ang = pos[:, None] * freqs[None, :]
        cos += [np.cos(ang), np.cos(ang)]
        sin += [-np.sin(ang), np.sin(ang)]
    return np.concatenate(cos, axis=1).astype(np.float32), np.concatenate(sin, axis=1).astype(np.float32)


def _rope_tables():
    c32, s32 = _rope32_tables()
    tile = lambda a, n: np.tile(a, (1, n))
    pad = np.zeros((DEC_SEQ, 96), np.float32)
    cos_k = np.concatenate([c32, pad + 1.0], axis=1)
    sin_k = np.concatenate([s32, pad], axis=1)
    return (tile(c32, 8), tile(s32, 8),
            tile(c32, 4), tile(s32, 4),
            cos_k, sin_k)


def kernel(x_prompt, x_sample, cache_na_k, cache_na_v, cache_diff_k, cache_diff_v, cache_mla_ckv, cache_mla_kpe, c, c_ctx, w_ada, b_ada, g_mix, g_ffn, w_in, w_out, na_rpb, diff_lq1, diff_lk1, diff_lq2, diff_lk2, diff_g_subln, mla_g_ckv, mla_w_uk, mla_w_uv, sgu_g, sgu_w, sgu_b, w_ff1, w_ff2, g_final):
    f32 = jnp.float32
    m = jnp.concatenate([c_ctx[None, :], c, jnp.zeros((N_MOD_ROWS - 1 - DEC_BATCH, D_MODEL), f32)], axis=0)
    mod = _ada(m, w_ada, b_ada).reshape(DEPTH, N_MOD_ROWS, 1, 6 * D_MODEL)
    cst = _lam_consts(diff_lq1, diff_lk1, diff_lq2, diff_lk2)
    tt = _bias_tiles(na_rpb)
    tables = [jnp.asarray(t) for t in _rope_tables()]

    t_last = lambda a: jnp.swapaxes(a, -1, -2)
    w_in_t = t_last(w_in)
    caches_t = (t_last(cache_na_k), t_last(cache_na_v), t_last(cache_diff_k), t_last(cache_diff_v),
                cache_mla_ckv, t_last(cache_mla_kpe))
    w_out_b, w1_b, w2_b = _bf(w_out), _bf(w_ff1), _bf(w_ff2)
    g_mix3 = g_mix.reshape(DEPTH, 1, D_MODEL)
    g_ffn3 = g_ffn.reshape(DEPTH, 1, D_MODEL)
    g_sub2 = jnp.tile(diff_g_subln, (1, 2)).reshape(DEPTH, 1, LANES)
    g_ckv3 = mla_g_ckv.reshape(DEPTH, 1, MLA_KV_RANK)
    sgu_g3 = sgu_g.reshape(DEPTH, 1, WIDTH)
    sgu_bt = sgu_b.transpose(0, 2, 1)
    g_fin2 = g_final.reshape(1, D_MODEL)

    xs = (x_prompt.reshape(N_PROMPT, D_MODEL), x_sample.reshape(N_SAMPLE, D_MODEL))
    new = ()
    for l in range(DEPTH):
        pa, pb, pc, od = _inproj(l, xs, g_mix3, mod, w_in_t, sgu_g3, sgu_w, sgu_bt)
        o_p, *new = _mix_prompt(l, pa, pb, pc, cst, g_sub2, g_ckv3, mla_w_uk, mla_w_uv, new)
        o_s = _mix_sample(l, pa, pb, pc, caches_t, tt, tables, cst, g_sub2, g_ckv3, mla_w_uk, mla_w_uv)
        ffn = functools.partial(_outffn, l, xs, o_p, o_s, od, w_out_b, g_ffn3, mod, w1_b, w2_b, g_fin2)
        if l < DEPTH - 1:
            xs = (ffn(0, TILES_PROMPT + TILES_SAMPLE),)
        else:
            xs = (ffn(0, TILES_PROMPT), ffn(TILES_PROMPT, TILES_SAMPLE))
    y_prompt = xs[0].reshape(BATCH, SEQ, D_MODEL)
    y_sample = xs[1].reshape(DEC_BATCH, DEC_SEQ, D_MODEL)
    na_k, na_v, diff_k, diff_v, mla_ckv, mla_kpe = new
    return (y_prompt, y_sample, t_last(na_k), t_last(na_v), t_last(diff_k), t_last(diff_v), mla_ckv, t_last(mla_kpe))
```

```python
import functools
import math

import numpy as np
import jax
import jax.numpy as jnp
from jax import lax
from jax.experimental import pallas as pl
from jax.experimental.pallas import tpu as pltpu

D_MODEL = 1024
BATCH = 16
SEQ = 256
DEPTH = 4
DEC_BATCH = 2
DEC_SEQ = 1024
PAST_LEN = 512
GRID_W = 64
GRID_ROWS = DEC_SEQ // GRID_W
HEAD_DIM = 64
NA_HEADS = 4
NA_WIN_ROWS = 8
NA_WIN_COLS = 16
DIFF_HEADS = 4
DIFF_QK_DIM = 32
DIFF_V_DIM = 64
MLA_HEADS = 4
MLA_NOPE = 64
MLA_ROPE = 32
MLA_V = 64
MLA_KV_RANK = 128
SGU_GROUPS = 4
SGU_GROUP_DIM = 64
SGU_CHUNK = 128
D_FF = 4 * D_MODEL
ROPE_BASE = 10000.0
EPS = 1e-6
NEG_INF = -1e30
LOG2E = 1.4426950408889634

N_HEADS = 4
N_PAIRS = N_HEADS // 2
LANES = 128
WIDTH = 256
N_PROMPT = BATCH * SEQ
N_SAMPLE = DEC_BATCH * DEC_SEQ
N_TOK = N_PROMPT + N_SAMPLE
N_MOD_ROWS = 8

SEG_A = 3 * WIDTH
SEG_B = 3 * WIDTH
SEG_C = 640
SEG_D = 2 * WIDTH
SEG_C_PAD = 96
IN_COLS_P = SEG_A + SEG_B + SEG_C + SEG_D
O_ATT = 3 * WIDTH

TM = 512
TILES_PROMPT = N_PROMPT // TM
TILES_SAMPLE = N_SAMPLE // TM
PB = 2
QB = 256
VMEM_LIMIT = 56 * 1024 * 1024


def _bf(x):
    return x.astype(jnp.bfloat16)


def _dot(a, b):
    return jnp.dot(a, b, preferred_element_type=jnp.float32)


def _dot_nt(a, b):
    return lax.dot_general(a, b, (((1,), (1,)), ((), ())), preferred_element_type=jnp.float32)


def _rms(x, g):
    ms = jnp.mean(x * x, axis=-1, keepdims=True)
    return x * lax.rsqrt(ms + EPS) * g


def _lane_range(lo, hi, width=LANES):
    lane = lax.broadcasted_iota(jnp.int32, (1, width), 1)
    return (lane >= lo) & (lane < hi)


def _with_ones(v):
    return jnp.concatenate([v, jnp.ones((v.shape[0], LANES), jnp.bfloat16)], axis=1)


def _attend(scores, values, log=None, bound=None):
    if bound is None:
        m = functools.reduce(jnp.maximum, [jnp.max(s, axis=-1, keepdims=True) for s in scores])
        shifted = [s - m for s in scores]
    else:
        over = jnp.max(bound - scores[0][:, 0:LANES], axis=0, keepdims=True)
        log.worst = jnp.maximum(log.worst, jnp.min(over, axis=1, keepdims=True))
        shifted = [s - jnp.concatenate([bound] * (s.shape[1] // LANES), axis=1) for s in scores]
    return functools.reduce(lambda a, b: a + b, [_dot(_bf(jnp.exp2(s)), v) for s, v in zip(shifted, values)])


BOUND_SLACK = 1.02
OVERSHOOT_LIMIT = 100.0


class _ShiftLog:
    def __init__(self, worst=None):
        self.worst = jnp.zeros((1, 1), jnp.float32) if worst is None else worst

    def unsafe(self):
        return jnp.logical_not(self.worst[0, 0] < OVERSHOOT_LIMIT)


def _group_matrix(width, n_groups, extra=None):
    i = lax.broadcasted_iota(jnp.int32, (width, LANES), 0)
    j = lax.broadcasted_iota(jnp.int32, (width, LANES), 1)
    size = LANES // n_groups
    hit = (i // size == j // size) & (i < LANES)
    for t in range(n_groups if extra else 0):
        lo, hi = extra(t)
        hit = hit | ((i >= lo) & (i < hi) & (j // size == t))
    return jnp.where(hit, 1.0, 0.0).astype(jnp.bfloat16)


def _squares(x):
    xf = x.astype(jnp.float32)
    return _bf(xf * xf)


def _key_bound(keys, groups):
    return functools.reduce(jnp.maximum, [jnp.max(_dot(_squares(k), groups), axis=0, keepdims=True) for k in keys])


def _bound(log, qs, k2max, groups, n_groups, extra=0.0):
    if log is None:
        return None
    per_group = _dot(_squares(qs), groups) * k2max
    ones = jnp.ones((LANES, LANES), jnp.bfloat16)
    return jnp.sqrt(_dot(_bf(per_group), ones) * (n_groups / LANES)) * BOUND_SLACK + extra


FAST_UNROLL = 2


def _loop(n, log, body):
    unroll = 1 if log is None else min(FAST_UNROLL, n)

    def step(t, worst):
        inner = None if log is None else _ShiftLog(worst)
        for u in range(unroll):
            body(t * unroll + u, inner)
        return worst if inner is None else inner.worst

    worst = lax.fori_loop(0, n // unroll, step, jnp.zeros((1, 1), jnp.float32) if log is None else log.worst)
    if log is not None:
        log.worst = worst
    for i in range(n - n % unroll, n):
        body(i, log)


def _aligned(start, multiple):
    return start if isinstance(start, int) else pl.multiple_of(start, multiple)


def _normalised(o_ext):
    return o_ext[:, 0:LANES] * (1.0 / o_ext[:, LANES:2 * LANES])


def _swap8(x):
    lane = lax.broadcasted_iota(jnp.int32, (1, LANES), 1)
    return jnp.where((lane & 15) < 8, pltpu.roll(x, LANES - 8, 1), pltpu.roll(x, 8, 1))


def _rope(x, cos, sin):
    outs = []
    for c in range(x.shape[1] // LANES):
        sl = slice(LANES * c, LANES * (c + 1))
        xc = x[:, sl]
        outs.append(xc * cos[:, sl] + _swap8(xc) * sin[:, sl])
    return outs[0] if len(outs) == 1 else jnp.concatenate(outs, axis=1)


def _tile4(x):
    return x + pltpu.roll(x, 32, 1) + pltpu.roll(x, 64, 1) + pltpu.roll(x, 96, 1)


def _params(*sem):
    return pltpu.CompilerParams(dimension_semantics=sem, vmem_limit_bytes=VMEM_LIMIT)


ADA_TN = 1536


def _ada_kernel(m_ref, w_ref, b_ref, o_ref):
    m = m_ref[...]
    s = m * jax.nn.sigmoid(m)
    o_ref[0] = _dot(_bf(s), _bf(w_ref[0])) + b_ref[0]


def _ada(m, w_ada, b_ada):
    n = 6 * D_MODEL
    return pl.pallas_call(
        _ada_kernel,
        grid=(DEPTH, n // ADA_TN),
        in_specs=[
            pl.BlockSpec((N_MOD_ROWS, D_MODEL), lambda l, j: (0, 0)),
            pl.BlockSpec((1, D_MODEL, ADA_TN), lambda l, j: (l, 0, j)),
            pl.BlockSpec((1, 1, ADA_TN), lambda l, j: (l, 0, j)),
        ],
        out_specs=pl.BlockSpec((1, N_MOD_ROWS, ADA_TN), lambda l, j: (l, 0, j)),
        out_shape=jax.ShapeDtypeStruct((DEPTH, N_MOD_ROWS, n), jnp.float32),
        compiler_params=_params("parallel", "parallel"),
        name="ada",
    )(m, w_ada, b_ada.reshape(DEPTH, 1, n))


def _lam_kernel(lq1_ref, lk1_ref, lq2_ref, lk2_ref, init_ref, o_ref):
    init = init_ref[...]
    a = jnp.exp(jnp.sum(lq1_ref[...] * lk1_ref[...], axis=-1, keepdims=True))
    b = jnp.exp(jnp.sum(lq2_ref[...] * lk2_ref[...], axis=-1, keepdims=True))
    lam = a - b + init
    post = 1.0 - init
    for l in range(DEPTH):
        o_ref[l, 0:1, :] = jnp.broadcast_to(lam[l:l + 1], (1, LANES))
        o_ref[l, 1:2, :] = jnp.broadcast_to(post[l:l + 1], (1, LANES))


def _lam_consts(lq1, lk1, lq2, lk2):
    init = np.array([[0.8 - 0.6 * math.exp(-0.3 * l)] for l in range(DEPTH)], np.float32)
    return pl.pallas_call(
        _lam_kernel,
        out_shape=jax.ShapeDtypeStruct((DEPTH, 2, LANES), jnp.float32),
        name="diff_lambda",
    )(lq1, lk1, lq2, lk2, jnp.asarray(init))


N_DROW = 2 * NA_WIN_ROWS - 1
N_DCOL = 2 * NA_WIN_COLS - 1


def _bias_kernel(rpb_ref, o_ref):
    l = pl.program_id(0)
    h = pl.program_id(1)
    base = (l * NA_HEADS + h) * (N_DROW * N_DCOL)
    cq = lax.broadcasted_iota(jnp.int32, (GRID_W, LANES), 0)
    lane = lax.broadcasted_iota(jnp.int32, (GRID_W, LANES), 1)
    ck = lane & (GRID_W - 1)
    dcol = jnp.clip(ck - cq, -(NA_WIN_COLS - 1), NA_WIN_COLS - 1) + (NA_WIN_COLS - 1)
    hi = lane >= GRID_W
    for a in range(N_DROW - 1):
        acc = jnp.zeros((GRID_W, LANES), jnp.float32)
        for j in range(N_DCOL):
            lo_v = rpb_ref[base + a * N_DCOL + j]
            hi_v = rpb_ref[base + (a + 1) * N_DCOL + j]
            acc = jnp.where(dcol == j, jnp.where(hi, hi_v, lo_v), acc)
        o_ref[0, 0, a] = acc * LOG2E


def _bias_tiles(na_rpb):
    return pl.pallas_call(
        _bias_kernel,
        grid=(DEPTH, NA_HEADS),
        in_specs=[pl.BlockSpec(memory_space=pltpu.SMEM)],
        out_specs=pl.BlockSpec((1, 1, N_DROW - 1, GRID_W, LANES), lambda l, h: (l, h, 0, 0, 0)),
        out_shape=jax.ShapeDtypeStruct((DEPTH, NA_HEADS, N_DROW - 1, GRID_W, LANES), jnp.float32),
        compiler_params=_params("parallel", "parallel"),
        name="na_bias_tiles",
    )(na_rpb.reshape(-1))


def _row_group(i):
    return jnp.where(i < TILES_PROMPT, 0, 1 + (i - TILES_PROMPT) // (DEC_SEQ // TM))


def _split_specs(width, first):
    return [pl.BlockSpec((TM, width), lambda i: (jnp.minimum(first + i, TILES_PROMPT - 1), 0)),
            pl.BlockSpec((TM, width), lambda i: (jnp.maximum(first + i - TILES_PROMPT, 0), 0))]


def _x_specs(split, first):
    if not split:
        return [pl.BlockSpec((TM, D_MODEL), lambda i: (first + i, 0))]
    return _split_specs(D_MODEL, first)


def _read_tile(refs, first):
    if len(refs) == 1:
        return refs[0][...]
    return jnp.where(first + pl.program_id(0) < TILES_PROMPT, refs[0][...], refs[1][...])


IN_COLS = 2592
IN_QC, IN_CKV, IN_D = 1536, 1920, 2080
TR_ROWS = 256


def _gelu_tanh(x):
    return 0.5 * x * (1.0 + jnp.tanh(math.sqrt(2.0 / math.pi) * (x + 0.044715 * (x * x * x))))


def _sgu(pd, g, w_ref, bt):
    u = _gelu_tanh(pd[:, 0:WIDTH])
    v = _gelu_tanh(pd[:, WIDTH:2 * WIDTH])
    grp = lax.broadcasted_iota(jnp.int32, (1, WIDTH), 1) // SGU_GROUP_DIM
    v2 = v * v
    ms = jnp.zeros_like(v)
    for gi in range(SGU_GROUPS):
        sel = grp == gi
        tot = jnp.sum(jnp.where(sel, v2, 0.0), axis=-1, keepdims=True)
        ms = jnp.where(sel, tot * (1.0 / SGU_GROUP_DIM), ms)
    vg = _bf(v * lax.rsqrt(ms + EPS) * g)
    outs = []
    for c in range(pd.shape[0] // SGU_CHUNK):
        rows = slice(SGU_CHUNK * c, SGU_CHUNK * (c + 1))
        mixed = jnp.zeros((SGU_CHUNK, WIDTH), jnp.float32)
        for gi in range(SGU_GROUPS):
            full = _dot(_bf(w_ref[0, gi]), vg[rows]) + bt[:, gi:gi + 1]
            mixed = jnp.where(grp == gi, full, mixed)
        outs.append(u[rows] * mixed)
    return jnp.concatenate(outs, axis=0)


def _w_in_row_pieces():
    qn = [(IN_QC + 96 * h, MLA_NOPE) for h in range(MLA_HEADS)]
    qp = [(IN_QC + 96 * h + MLA_NOPE, MLA_ROPE) for h in range(MLA_HEADS)]
    seg_c = qn + qp + [(IN_CKV, MLA_KV_RANK + MLA_ROPE)]
    return (0, SEG_A + SEG_B), seg_c, (IN_D, SEG_D)


def _load_w_in(wt_ref, w_scr):
    ab, seg_c, d = _w_in_row_pieces()
    c_rows = jnp.concatenate([wt_ref[0, s:s + n, :] for s, n in seg_c]
                             + [jnp.zeros((SEG_C_PAD, D_MODEL), jnp.float32)], axis=0)
    for t in range(SEG_C // LANES):
        w_scr[:, SEG_A + SEG_B + LANES * t:SEG_A + SEG_B + LANES * (t + 1)] = _bf(c_rows[LANES * t:LANES * (t + 1)].T)
    for (src, n), dst in ((ab, 0), (d, SEG_A + SEG_B + SEG_C)):
        for t in range(n // TR_ROWS):
            rows = wt_ref[0, src + TR_ROWS * t:src + TR_ROWS * (t + 1), :]
            w_scr[:, dst + TR_ROWS * t:dst + TR_ROWS * (t + 1)] = _bf(rows.T)


def _inproj_kernel(n_x, *refs):
    x_refs = refs[:n_x]
    (g_ref, sh_ref, sc_ref, wt_ref, sg_ref, sw_ref, sbt_ref, pa_ref, pb_ref, pc_ref, od_ref, w_scr) = refs[n_x:]

    @pl.when(pl.program_id(0) == 0)
    def _():
        _load_w_in(wt_ref, w_scr)

    h = _rms(_read_tile(x_refs, 0), g_ref[0]) * (1.0 + sc_ref[...]) + sh_ref[...]
    hb = _bf(h)
    off = SEG_A + SEG_B + SEG_C
    od_ref[...] = _sgu(_dot(hb, w_scr[:, off:off + SEG_D]), sg_ref[0], sw_ref, sbt_ref[0])
    off = 0
    for ref in (pa_ref, pb_ref, pc_ref):
        n = ref.shape[1]
        ref[...] = _dot(hb, w_scr[:, off:off + n])
        off += n


def _inproj(l, xs, g_mix, mod, w_in_t, sgu_g, sgu_w, sgu_bt):
    def mod_spec(j):
        return pl.BlockSpec((None, None, 1, D_MODEL), lambda i: (l, _row_group(i), 0, j))

    widths = (SEG_A, SEG_B, SEG_C, WIDTH)
    return pl.pallas_call(
        functools.partial(_inproj_kernel, len(xs)),
        grid=(N_TOK // TM,),
        in_specs=_x_specs(len(xs) == 2, 0) + [
            pl.BlockSpec((1, 1, D_MODEL), lambda i: (l, 0, 0)),
            mod_spec(0), mod_spec(1),
            pl.BlockSpec((1, IN_COLS, D_MODEL), lambda i: (l, 0, 0), pipeline_mode=pl.Buffered(1)),
            pl.BlockSpec((1, 1, WIDTH), lambda i: (l, 0, 0)),
            pl.BlockSpec((1, SGU_GROUPS, SGU_CHUNK, SGU_CHUNK), lambda i: (l, 0, 0, 0)),
            pl.BlockSpec((1, SGU_CHUNK, SGU_GROUPS), lambda i: (l, 0, 0)),
        ],
        out_specs=[pl.BlockSpec((TM, n), lambda i: (i, 0)) for n in widths],
        out_shape=[jax.ShapeDtypeStruct((N_TOK, n), jnp.float32) for n in widths],
        scratch_shapes=[pltpu.VMEM((D_MODEL, IN_COLS_P), jnp.bfloat16)],
        compiler_params=_params("arbitrary"),
        name="inproj",
    )(*xs, g_mix, mod, mod, w_in_t, sgu_g, sgu_w, sgu_bt)


C_QN, C_QP, C_CKV, C_KPE = 0, 256, 384, 512


def _stack_heads(qp):
    lo = _lane_range(0, 64)
    return jnp.concatenate([_bf(jnp.where(lo, qp, 0.0)), _bf(jnp.where(lo, 0.0, qp))], axis=0)


def _unstack_heads(o, n):
    return jnp.where(_lane_range(0, 64), o[0:n], o[n:2 * n])


def _pair_t(c_ref, j):
    return jnp.concatenate([c_ref[2 * j], c_ref[2 * j + 1]], axis=0)


def _stack_components(qp):
    return jnp.concatenate([_bf(jnp.where(_lane_range(32 * t, 32 * (t + 1)), qp, 0.0)) for t in range(4)], axis=0)


def _group_mean_sq(x, groups, size):
    sq = x * x
    hi = _bf(sq)
    rest = sq - hi.astype(jnp.float32)
    mid = _bf(rest)
    lo = _bf(rest - mid.astype(jnp.float32))
    return (_dot(hi, groups) + _dot(mid, groups) + _dot(lo, groups)) * (1.0 / size)


def _diff_finish(o, n, lam, post, g2, by_head):
    den = o[:, LANES:2 * LANES]
    outs = []
    for t in range(2):
        p1 = o[2 * t * n:(2 * t + 1) * n, 0:LANES] * (1.0 / den[2 * t * n:(2 * t + 1) * n])
        p2 = o[(2 * t + 1) * n:(2 * t + 2) * n, 0:LANES] * (lam / den[(2 * t + 1) * n:(2 * t + 2) * n])
        outs.append(p1 - p2)
    d = jnp.where(_lane_range(0, 64), outs[0], outs[1])
    return d * lax.rsqrt(_group_mean_sq(d, by_head, DIFF_V_DIM) + EPS) * g2 * post


def _mla_groups(j):
    return _group_matrix(2 * LANES, 2,
                         lambda t: (LANES + MLA_ROPE * (2 * j + t), LANES + MLA_ROPE * (2 * j + t + 1)))


def _mla_queries(qn_pair, qp_all, j):
    halves = []
    for t in range(2):
        h = 2 * j + t
        halves.append(jnp.concatenate([
            _bf(jnp.where(_lane_range(64 * t, 64 * (t + 1)), qn_pair, 0.0)),
            _bf(jnp.where(_lane_range(MLA_ROPE * h, MLA_ROPE * (h + 1)), qp_all, 0.0))], axis=1))
    return jnp.concatenate(halves, axis=0)


def _write_heads_t(p_ref, rows, col0, out_ref, bb):
    xt = p_ref[rows, col0:col0 + WIDTH].T
    for h in range(N_HEADS):
        out_ref[bb, 0, h] = xt[64 * h:64 * (h + 1)]
    _clear_other_layers(out_ref, bb)


def _clear_other_layers(out_ref, bb):
    if out_ref.shape[1] > 1:
        out_ref[bb, 1:] = jnp.zeros(out_ref.shape[1:], jnp.float32)[1:]


def _mix_prompt_kernel(n_prev, *refs):
    ins, outs = refs[:8], refs[8 + n_prev:]
    log = _ShiftLog()
    _mix_prompt_pass(ins, outs, log)

    @pl.when(log.unsafe())
    def _():
        _mix_prompt_pass(ins, outs, None)


def _mix_prompt_pass(ins, outs, log):
    pa_ref, pb_ref, pc_ref, cst_ref, gsub_ref, gckv_ref, wuk_ref, wuv_ref = ins
    o_ref, nak_ref, nav_ref, dk_ref, dv_ref, ckv_ref, kpe_ref = outs
    first_pass = log is not None
    c_a = HEAD_DIM ** -0.5 * LOG2E
    c_b = DIFF_QK_DIM ** -0.5 * LOG2E
    c_c = (MLA_NOPE + MLA_ROPE) ** -0.5 * LOG2E
    lam = cst_ref[0, 0:1, 0:1]
    post = cst_ref[0, 1:2, 0:1]
    wuk, wuv = _bf(wuk_ref[0]), _bf(wuv_ref[0])
    by_head, by_comp = _group_matrix(LANES, 2), _group_matrix(LANES, 4)

    def sequence(bb, log):
        rows = pl.ds(_aligned(bb * SEQ, SEQ), SEQ)
        for j in range(N_PAIRS):
            cols = slice(LANES * j, LANES * (j + 1))
            k = _bf(pa_ref[rows, WIDTH + LANES * j:WIDTH + LANES * (j + 1)])
            v = _with_ones(_bf(pa_ref[rows, 2 * WIDTH + LANES * j:2 * WIDTH + LANES * (j + 1)]))
            qs = _stack_heads(pa_ref[rows, cols] * c_a)
            o = _attend([_dot_nt(qs, k)], [v], log, _bound(log, qs, _key_bound([k], by_head), by_head, 2))
            o_ref[rows, cols] = _unstack_heads(_normalised(o), SEQ)
        if first_pass:
            _write_heads_t(pa_ref, rows, WIDTH, nak_ref, bb)
            _write_heads_t(pa_ref, rows, 2 * WIDTH, nav_ref, bb)
        for j in range(N_PAIRS):
            cols = slice(LANES * j, LANES * (j + 1))
            k = _bf(pb_ref[rows, WIDTH + LANES * j:WIDTH + LANES * (j + 1)])
            v = _with_ones(_bf(pb_ref[rows, 2 * WIDTH + LANES * j:2 * WIDTH + LANES * (j + 1)]))
            qs = _stack_components(pb_ref[rows, cols] * c_b)
            o = _attend([_dot_nt(qs, k)], [v], log, _bound(log, qs, _key_bound([k], by_comp), by_comp, 4))
            o_ref[rows, WIDTH + LANES * j:WIDTH + LANES * (j + 1)] = _diff_finish(o, SEQ, lam, post, gsub_ref[0], by_head)
        if first_pass:
            _write_heads_t(pb_ref, rows, WIDTH, dk_ref, bb)
            _write_heads_t(pb_ref, rows, 2 * WIDTH, dv_ref, bb)
        ckv = _rms(pc_ref[rows, C_CKV:C_CKV + MLA_KV_RANK], gckv_ref[0])
        kpe_slot = pc_ref[rows, C_KPE:C_KPE + LANES]
        if first_pass:
            ckv_ref[bb, 0] = ckv
            _clear_other_layers(ckv_ref, bb)
            kpe_ref[bb, 0] = kpe_slot.T[0:MLA_ROPE]
            _clear_other_layers(kpe_ref, bb)
        ckv_b = _bf(ckv)
        kn = _bf(_dot(ckv_b, wuk))
        vv = _bf(_dot(ckv_b, wuv))
        kpe4 = _bf(_tile4(kpe_slot))
        qn = pc_ref[rows, C_QN:C_QN + WIDTH] * c_c
        qp = pc_ref[rows, C_QP:C_QP + LANES] * c_c
        for j in range(N_PAIRS):
            cols = slice(LANES * j, LANES * (j + 1))
            k = jnp.concatenate([kn[:, cols], kpe4], axis=1)
            qs = _mla_queries(qn[:, cols], qp, j)
            groups = _mla_groups(j)
            o = _attend([_dot_nt(qs, k)], [_with_ones(vv[:, cols])], log,
                        _bound(log, qs, _key_bound([k], groups), groups, 2))
            o_ref[rows, 2 * WIDTH + LANES * j:2 * WIDTH + LANES * (j + 1)] = _unstack_heads(_normalised(o), SEQ)

    _loop(PB, log, sequence)


def _mix_prompt(l, pa, pb, pc, cst, g_sub2, g_ckv, w_uk, w_uv, prev):
    n_prev = len(prev)
    tails = [(NA_HEADS, HEAD_DIM, SEQ)] * 2 + [(DIFF_HEADS, 64, SEQ)] * 2 + [(SEQ, MLA_KV_RANK), (MLA_ROPE, SEQ)]

    def cache_spec(tail):
        if l == 0:
            return pl.BlockSpec((PB, DEPTH) + tail, lambda b: (b, 0) + (0,) * len(tail))
        return pl.BlockSpec((PB, 1) + tail, lambda b: (b, l) + (0,) * len(tail))

    def rows(width):
        return pl.BlockSpec((PB * SEQ, width), lambda b: (b, 0))

    def layer(*tail):
        return pl.BlockSpec((1,) + tail, lambda b: (l,) + (0,) * len(tail))

    return pl.pallas_call(
        functools.partial(_mix_prompt_kernel, n_prev),
        grid=(BATCH // PB,),
        in_specs=[rows(SEG_A), rows(SEG_B), rows(SEG_C), layer(2, LANES), layer(1, LANES), layer(1, MLA_KV_RANK),
                  layer(MLA_KV_RANK, WIDTH), layer(MLA_KV_RANK, WIDTH)] + [pl.BlockSpec(memory_space=pl.ANY)] * n_prev,
        out_specs=[rows(O_ATT)] + [cache_spec(t) for t in tails],
        out_shape=[jax.ShapeDtypeStruct((N_PROMPT, O_ATT), jnp.float32)]
        + [jax.ShapeDtypeStruct((BATCH, DEPTH) + t, jnp.float32) for t in tails],
        input_output_aliases={8 + i: 1 + i for i in range(n_prev)},
        compiler_params=_params("parallel"),
        name="mix_prompt",
    )(pa, pb, pc, cst, g_sub2, g_ckv, w_uk, w_uv, *prev)


def _na_row_groups():
    kh = min(NA_WIN_ROWS, GRID_ROWS)
    r0s = [min(max(r - kh // 2, 0), GRID_ROWS - kh) for r in range(GRID_ROWS)]
    groups = []
    for r, r0 in enumerate(r0s):
        if groups and groups[-1][2] == r0:
            groups[-1][1] = r
        else:
            groups.append([r, r, r0])
    return kh, [tuple(g) for g in groups]


def _na_sample(pa_ref, ck_ref, cv_ref, tt_ref, o_ref, log):
    c = HEAD_DIM ** -0.5 * LOG2E
    kh, groups = _na_row_groups()
    lk = kh * GRID_W
    edge = [g for g in groups if g[1] > g[0]]
    inner = [g for g in groups if g[1] == g[0]]
    depth = inner[0][0] - inner[0][2]
    assert all(g[0] - g[2] == depth for g in inner) and [g[0] for g in inner] == list(range(inner[0][0], inner[-1][0] + 1))

    def in_window(n):
        cq = lax.broadcasted_iota(jnp.int32, (n, lk), 0) & (GRID_W - 1)
        ck = lax.broadcasted_iota(jnp.int32, (n, lk), 1) & (GRID_W - 1)
        c0 = jnp.clip(cq - NA_WIN_COLS // 2, 0, GRID_W - NA_WIN_COLS)
        return (ck >= c0) & (ck < c0 + NA_WIN_COLS)

    for j in range(N_PAIRS):
        cols = slice(LANES * j, LANES * (j + 1))
        kcols = slice(WIDTH + LANES * j, WIDTH + LANES * (j + 1))
        vcols = slice(2 * WIDTH + LANES * j, 2 * WIDTH + LANES * (j + 1))
        kc_t = _bf(_pair_t(ck_ref, j))
        vc = _with_ones(_bf(_pair_t(cv_ref, j).T))
        by_head = _group_matrix(LANES, 2)
        k2max = _key_bound([_bf(pa_ref[:, kcols]), _bf(_pair_t(ck_ref, j).T)], by_head)
        tmax = functools.reduce(jnp.maximum, [tt_ref[2 * j + t, a] for t in range(2) for a in range(N_DROW - 1)])
        bplus = jnp.maximum(jnp.max(jnp.max(tmax, axis=-1, keepdims=True), axis=0, keepdims=True), 0.0)

        def group(row0, key0, offsets, log):
            n = len(offsets) * GRID_W
            rows, keys = pl.ds(row0, n), pl.ds(key0, lk)
            qg = _stack_heads(pa_ref[rows, cols] * c)
            k = _bf(pa_ref[keys, kcols])
            v = _with_ones(_bf(pa_ref[keys, vcols]))
            bias = jnp.concatenate([
                jnp.concatenate([tt_ref[2 * j + t, 2 * i - off + NA_WIN_ROWS - 1] for i in range(kh // 2)], axis=1)
                for t in range(2) for off in offsets], axis=0)
            s_loc = jnp.where(in_window(2 * n), _dot_nt(qg, k) + bias, NEG_INF)
            o = _attend([_dot(qg, kc_t), s_loc], [vc, v], log, _bound(log, qg, k2max, by_head, 2, bplus))
            o_ref[rows, cols] = _unstack_heads(_normalised(o), n)

        for (r_lo, r_hi, r0) in edge:
            group(r_lo * GRID_W, r0 * GRID_W, [r - r0 for r in range(r_lo, r_hi + 1)], log)

        def inner_row(i, log):
            r = inner[0][0] + i
            group(_aligned(r * GRID_W, GRID_W), _aligned((r - depth) * GRID_W, GRID_W), [depth], log)

        _loop(len(inner), log, inner_row)


def _diff_sample(pb_ref, ck_ref, cv_ref, cos_ref, sin_ref, cst_ref, g_ref, o_ref, log):
    c = DIFF_QK_DIM ** -0.5 * LOG2E
    lam = cst_ref[0, 0:1, 0:1]
    post = cst_ref[0, 1:2, 0:1]
    for j in range(N_PAIRS):
        cols = slice(LANES * j, LANES * (j + 1))
        k_new = _bf(_rope(pb_ref[:, WIDTH + LANES * j:WIDTH + LANES * (j + 1)], cos_ref[:, cols], sin_ref[:, cols]))
        kc_t = _bf(_pair_t(ck_ref, j))
        vc = _with_ones(_bf(_pair_t(cv_ref, j).T))
        v = _with_ones(_bf(pb_ref[:, 2 * WIDTH + LANES * j:2 * WIDTH + LANES * (j + 1)]))
        by_head, by_comp = _group_matrix(LANES, 2), _group_matrix(LANES, 4)
        k2max = _key_bound([k_new, _bf(_pair_t(ck_ref, j).T)], by_comp)

        def block(qi, log):
            rows = pl.ds(_aligned(qi * QB, QB), QB)
            qs = _stack_components(_rope(pb_ref[rows, cols], cos_ref[rows, cols], sin_ref[rows, cols]) * c)
            o = _attend([_dot(qs, kc_t), _dot_nt(qs, k_new)], [vc, v], log, _bound(log, qs, k2max, by_comp, 4))
            o_ref[rows, WIDTH + LANES * j:WIDTH + LANES * (j + 1)] = _diff_finish(o, QB, lam, post, g_ref[0], by_head)

        _loop(DEC_SEQ // QB, log, block)


def _mla_sample(pc_ref, cckv_ref, ckpe_ref, cosq_ref, sinq_ref, cosk_ref, sink_ref, gckv_ref, wuk_ref, wuv_ref, o_ref,
                log):
    c = (MLA_NOPE + MLA_ROPE) ** -0.5 * LOG2E
    wuk, wuv = _bf(wuk_ref[0]), _bf(wuv_ref[0])
    ckv_new = _bf(_rms(pc_ref[:, C_CKV:C_CKV + MLA_KV_RANK], gckv_ref[0]))
    ckv_old = _bf(cckv_ref[...])
    kpe_new = _bf(_tile4(_rope(pc_ref[:, C_KPE:C_KPE + LANES], cosk_ref[...], sink_ref[...])))
    kpe_old = _bf(jnp.concatenate([ckpe_ref[...]] * MLA_HEADS, axis=0).T)
    kn_new, kn_old = _bf(_dot(ckv_new, wuk)), _bf(_dot(ckv_old, wuk))
    v_new, v_old = _bf(_dot(ckv_new, wuv)), _bf(_dot(ckv_old, wuv))
    for j in range(N_PAIRS):
        cols = slice(LANES * j, LANES * (j + 1))
        k_old = jnp.concatenate([kn_old[:, cols], kpe_old], axis=1)
        k_new = jnp.concatenate([kn_new[:, cols], kpe_new], axis=1)
        vo, vn = _with_ones(v_old[:, cols]), _with_ones(v_new[:, cols])
        groups = _mla_groups(j)
        k2max = _key_bound([k_old, k_new], groups)

        def block(qi, log):
            rows = pl.ds(_aligned(qi * QB, QB), QB)
            qn = pc_ref[rows, C_QN + LANES * j:C_QN + LANES * (j + 1)] * c
            qp = _rope(pc_ref[rows, C_QP:C_QP + LANES], cosq_ref[rows, :], sinq_ref[rows, :]) * c
            qs = _mla_queries(qn, qp, j)
            o = _attend([_dot_nt(qs, k_old), _dot_nt(qs, k_new)], [vo, vn], log, _bound(log, qs, k2max, groups, 2))
            o_ref[rows, 2 * WIDTH + LANES * j:2 * WIDTH + LANES * (j + 1)] = _unstack_heads(_normalised(o), QB)

        _loop(DEC_SEQ // QB, log, block)


def _mix_sample_kernel(pa_ref, pb_ref, pc_ref, cnak_ref, cnav_ref, cdk_ref, cdv_ref, cckv_ref, ckpe_ref, tt_ref,
                       cosb_ref, sinb_ref, cosq_ref, sinq_ref, cosk_ref, sink_ref,
                       cst_ref, gsub_ref, gckv_ref, wuk_ref, wuv_ref, o_ref):
    def run(log):
        _na_sample(pa_ref, cnak_ref, cnav_ref, tt_ref, o_ref, log)
        _diff_sample(pb_ref, cdk_ref, cdv_ref, cosb_ref, sinb_ref, cst_ref, gsub_ref, o_ref, log)
        _mla_sample(pc_ref, cckv_ref, ckpe_ref, cosq_ref, sinq_ref, cosk_ref, sink_ref, gckv_ref, wuk_ref, wuv_ref,
                    o_ref, log)

    log = _ShiftLog()
    run(log)

    @pl.when(log.unsafe())
    def _():
        run(None)


def _mix_sample(l, pa, pb, pc, caches_t, tt, tables, cst, g_sub2, g_ckv, w_uk, w_uv):
    first = N_PROMPT // DEC_SEQ

    def rows(width):
        return pl.BlockSpec((DEC_SEQ, width), lambda b: (first + b, 0))

    def cache(*tail):
        return pl.BlockSpec((None, None) + tail, lambda b: (b, l) + (0,) * len(tail))

    def layer(*tail):
        return pl.BlockSpec((1,) + tail, lambda b: (l,) + (0,) * len(tail))

    def table(width):
        return pl.BlockSpec((DEC_SEQ, width), lambda b: (0, 0), pipeline_mode=pl.Buffered(1))

    kv_t = cache(N_HEADS, 64, PAST_LEN)
    return pl.pallas_call(
        _mix_sample_kernel,
        grid=(DEC_BATCH,),
        in_specs=[rows(SEG_A), rows(SEG_B), rows(SEG_C), kv_t, kv_t, kv_t, kv_t,
                  cache(PAST_LEN, MLA_KV_RANK), cache(MLA_ROPE, PAST_LEN),
                  pl.BlockSpec((None, NA_HEADS, N_DROW - 1, GRID_W, LANES), lambda b: (l, 0, 0, 0, 0),
                               pipeline_mode=pl.Buffered(1)),
                  table(WIDTH), table(WIDTH), table(LANES), table(LANES), table(LANES), table(LANES),
                  layer(2, LANES), layer(1, LANES), layer(1, MLA_KV_RANK),
                  layer(MLA_KV_RANK, WIDTH), layer(MLA_KV_RANK, WIDTH)],
        out_specs=pl.BlockSpec((DEC_SEQ, O_ATT), lambda b: (b, 0)),
        out_shape=jax.ShapeDtypeStruct((N_SAMPLE, O_ATT), jnp.float32),
        compiler_params=_params("parallel"),
        name="mix_sample",
    )(pa, pb, pc, *caches_t, tt, *tables, cst, g_sub2, g_ckv, w_uk, w_uv)


FF_CHUNK = 1024


def _outffn_kernel(n_x, first, final, *refs):
    x_refs, o_refs = refs[:n_x], refs[n_x:n_x + 2]
    (od_ref, wout_ref, g1_ref, gffn_ref, sh2_ref, sc2_ref, g2_ref, w1_ref, w2_ref, gfin_ref, y_ref) = refs[n_x + 2:]
    acc = (_dot(_bf(_read_tile(o_refs, first)), wout_ref[0, 0:O_ATT, :])
           + _dot(_bf(od_ref[...]), wout_ref[0, O_ATT:O_ATT + WIDTH, :]))
    x1 = _read_tile(x_refs, first) + g1_ref[...] * acc
    hf = _bf(_rms(x1, gffn_ref[0]) * (1.0 + sc2_ref[...]) + sh2_ref[...])
    acc = jnp.zeros((TM, D_MODEL), jnp.float32)
    for c in range(D_FF // FF_CHUNK):
        cols = slice(FF_CHUNK * c, FF_CHUNK * (c + 1))
        a = jnp.square(jnp.maximum(_dot(hf, w1_ref[0, :, cols]), 0.0))
        acc += _dot(_bf(a), w2_ref[0, cols, :])
    y = x1 + g2_ref[...] * acc
    if final:
        y = _rms(y, gfin_ref[...])
    y_ref[...] = y


def _outffn(l, xs, o_p, o_s, od, w_out, g_ffn, mod, w1, w2, g_final, first, n_tiles):
    def mod_spec(j):
        return pl.BlockSpec((None, None, 1, D_MODEL), lambda i: (l, _row_group(first + i), 0, j))

    def resident(shape):
        return pl.BlockSpec(shape, lambda i: (l,) + (0,) * (len(shape) - 1), pipeline_mode=pl.Buffered(1))

    return pl.pallas_call(
        functools.partial(_outffn_kernel, len(xs), first, l == DEPTH - 1),
        grid=(n_tiles,),
        in_specs=_x_specs(len(xs) == 2, first) + _split_specs(O_ATT, first) + [
            pl.BlockSpec((TM, WIDTH), lambda i: (first + i, 0)),
            resident((1, 4 * WIDTH, D_MODEL)),
            mod_spec(2),
            pl.BlockSpec((1, 1, D_MODEL), lambda i: (l, 0, 0)),
            mod_spec(3), mod_spec(4), mod_spec(5),
            resident((1, D_MODEL, D_FF)),
            resident((1, D_FF, D_MODEL)),
            pl.BlockSpec((1, D_MODEL), lambda i: (0, 0)),
        ],
        out_specs=pl.BlockSpec((TM, D_MODEL), lambda i: (i, 0)),
        out_shape=jax.ShapeDtypeStruct((n_tiles * TM, D_MODEL), jnp.float32),
        compiler_params=_params("parallel"),
        name="outffn",
    )(*xs, o_p, o_s, od, w_out, mod, g_ffn, mod, mod, mod, w1, w2, g_final)


def _rope32_tables():
    t = np.arange(DEC_SEQ)
    rows, cols = (t // GRID_W).astype(np.float64), (t % GRID_W).astype(np.float64)
    half = 8
    freqs = ROPE_BASE ** (-np.arange(half, dtype=np.float64) / half)
    cos, sin = [], []
    for pos in (rows, cols):
        ang = pos[:, None] * freqs[None, :]
        cos += [np.cos(ang), np.cos(ang)]
        sin += [-np.sin(ang), np.sin(ang)]
    return np.concatenate(cos, axis=1).astype(np.float32), np.concatenate(sin, axis=1).astype(np.float32)


def _rope_tables():
    c32, s32 = _rope32_tables()
    tile = lambda a, n: np.tile(a, (1, n))
    pad = np.zeros((DEC_SEQ, 96), np.float32)
    cos_k = np.concatenate([c32, pad + 1.0], axis=1)
    sin_k = np.concatenate([s32, pad], axis=1)
    return (tile(c32, 8), tile(s32, 8),
            tile(c32, 4), tile(s32, 4),
            cos_k, sin_k)


def kernel(x_prompt, x_sample, cache_na_k, cache_na_v, cache_diff_k, cache_diff_v, cache_mla_ckv, cache_mla_kpe, c, c_ctx, w_ada, b_ada, g_mix, g_ffn, w_in, w_out, na_rpb, diff_lq1, diff_lk1, diff_lq2, diff_lk2, diff_g_subln, mla_g_ckv, mla_w_uk, mla_w_uv, sgu_g, sgu_w, sgu_b, w_ff1, w_ff2, g_final):
    f32 = jnp.float32
    m = jnp.concatenate([c_ctx[None, :], c, jnp.zeros((N_MOD_ROWS - 1 - DEC_BATCH, D_MODEL), f32)], axis=0)
    mod = _ada(m, w_ada, b_ada).reshape(DEPTH, N_MOD_ROWS, 1, 6 * D_MODEL)
    cst = _lam_consts(diff_lq1, diff_lk1, diff_lq2, diff_lk2)
    tt = _bias_tiles(na_rpb)
    tables = [jnp.asarray(t) for t in _rope_tables()]

    t_last = lambda a: jnp.swapaxes(a, -1, -2)
    w_in_t = t_last(w_in)
    caches_t = (t_last(cache_na_k), t_last(cache_na_v), t_last(cache_diff_k), t_last(cache_diff_v),
                cache_mla_ckv, t_last(cache_mla_kpe))
    w_out_b, w1_b, w2_b = _bf(w_out), _bf(w_ff1), _bf(w_ff2)
    g_mix3 = g_mix.reshape(DEPTH, 1, D_MODEL)
    g_ffn3 = g_ffn.reshape(DEPTH, 1, D_MODEL)
    g_sub2 = jnp.tile(diff_g_subln, (1, 2)).reshape(DEPTH, 1, LANES)
    g_ckv3 = mla_g_ckv.reshape(DEPTH, 1, MLA_KV_RANK)
    sgu_g3 = sgu_g.reshape(DEPTH, 1, WIDTH)
    sgu_bt = sgu_b.transpose(0, 2, 1)
    g_fin2 = g_final.reshape(1, D_MODEL)

    xs = (x_prompt.reshape(N_PROMPT, D_MODEL), x_sample.reshape(N_SAMPLE, D_MODEL))
    new = ()
    for l in range(DEPTH):
        pa, pb, pc, od = _inproj(l, xs, g_mix3, mod, w_in_t, sgu_g3, sgu_w, sgu_bt)
        o_p, *new = _mix_prompt(l, pa, pb, pc, cst, g_sub2, g_ckv3, mla_w_uk, mla_w_uv, new)
        o_s = _mix_sample(l, pa, pb, pc, caches_t, tt, tables, cst, g_sub2, g_ckv3, mla_w_uk, mla_w_uv)
        ffn = functools.partial(_outffn, l, xs, o_p, o_s, od, w_out_b, g_ffn3, mod, w1_b, w2_b, g_fin2)
        if l < DEPTH - 1:
            xs = (ffn(0, TILES_PROMPT + TILES_SAMPLE),)
        else:
            xs = (ffn(0, TILES_PROMPT), ffn(TILES_PROMPT, TILES_SAMPLE))
    y_prompt = xs[0].reshape(BATCH, SEQ, D_MODEL)
    y_sample = xs[1].reshape(DEC_BATCH, DEC_SEQ, D_MODEL)
    na_k, na_v, diff_k, diff_v, mla_ckv, mla_kpe = new
    return (y_prompt, y_sample, t_last(na_k), t_last(na_v), t_last(diff_k), t_last(diff_v), mla_ckv, t_last(mla_kpe))
```

```python
import functools
import math

import numpy as np
import jax
import jax.numpy as jnp
from jax import lax
from jax.experimental import pallas as pl
from jax.experimental.pallas import tpu as pltpu

D_MODEL = 1024
BATCH = 16
SEQ = 256
DEPTH = 4
DEC_BATCH = 2
DEC_SEQ = 1024
PAST_LEN = 512
GRID_W = 64
GRID_ROWS = DEC_SEQ // GRID_W
HEAD_DIM = 64
NA_HEADS = 4
NA_WIN_ROWS = 8
NA_WIN_COLS = 16
DIFF_HEADS = 4
DIFF_QK_DIM = 32
DIFF_V_DIM = 64
MLA_HEADS = 4
MLA_NOPE = 64
MLA_ROPE = 32
MLA_V = 64
MLA_KV_RANK = 128
SGU_GROUPS = 4
SGU_GROUP_DIM = 64
SGU_CHUNK = 128
D_FF = 4 * D_MODEL
ROPE_BASE = 10000.0
EPS = 1e-6
NEG_INF = -1e30
LOG2E = 1.4426950408889634

N_HEADS = 4
N_PAIRS = N_HEADS // 2
LANES = 128
WIDTH = 256
N_PROMPT = BATCH * SEQ
N_SAMPLE = DEC_BATCH * DEC_SEQ
N_TOK = N_PROMPT + N_SAMPLE
N_MOD_ROWS = 8

SEG_A = 3 * WIDTH
SEG_B = 3 * WIDTH
SEG_C = 640
SEG_D = 2 * WIDTH
SEG_C_PAD = 96
IN_COLS_P = SEG_A + SEG_B + SEG_C + SEG_D
O_ATT = 3 * WIDTH

TM = 512
TILES_PROMPT = N_PROMPT // TM
TILES_SAMPLE = N_SAMPLE // TM
PB = 2
QB = 256
VMEM_LIMIT = 56 * 1024 * 1024


def _bf(x):
    return x.astype(jnp.bfloat16)


def _dot(a, b):
    return jnp.dot(a, b, preferred_element_type=jnp.float32)


def _dot_nt(a, b):
    return lax.dot_general(a, b, (((1,), (1,)), ((), ())), preferred_element_type=jnp.float32)


def _rms(x, g):
    ms = jnp.mean(x * x, axis=-1, keepdims=True)
    return x * lax.rsqrt(ms + EPS) * g


def _lane_range(lo, hi, width=LANES):
    lane = lax.broadcasted_iota(jnp.int32, (1, width), 1)
    return (lane >= lo) & (lane < hi)


def _with_ones(v):
    return jnp.concatenate([v, jnp.ones((v.shape[0], LANES), jnp.bfloat16)], axis=1)


def _attend(scores, values, log=None, bound=None):
    if bound is None:
        m = functools.reduce(jnp.maximum, [jnp.max(s, axis=-1, keepdims=True) for s in scores])
        shifted = [s - m for s in scores]
    else:
        over = jnp.max(bound - scores[0][:, 0:LANES], axis=0, keepdims=True)
        log.worst = jnp.maximum(log.worst, jnp.min(over, axis=1, keepdims=True))
        shifted = [s - jnp.concatenate([bound] * (s.shape[1] // LANES), axis=1) for s in scores]
    return functools.reduce(lambda a, b: a + b, [_dot(_bf(jnp.exp2(s)), v) for s, v in zip(shifted, values)])


BOUND_SLACK = 1.02
OVERSHOOT_LIMIT = 100.0


class _ShiftLog:
    def __init__(self, worst=None):
        self.worst = jnp.zeros((1, 1), jnp.float32) if worst is None else worst

    def unsafe(self):
        return jnp.logical_not(self.worst[0, 0] < OVERSHOOT_LIMIT)


def _group_matrix(width, n_groups, extra=None):
    i = lax.broadcasted_iota(jnp.int32, (width, LANES), 0)
    j = lax.broadcasted_iota(jnp.int32, (width, LANES), 1)
    size = LANES // n_groups
    hit = (i // size == j // size) & (i < LANES)
    for t in range(n_groups if extra else 0):
        lo, hi = extra(t)
        hit = hit | ((i >= lo) & (i < hi) & (j // size == t))
    return jnp.where(hit, 1.0, 0.0).astype(jnp.bfloat16)


def _squares(x):
    xf = x.astype(jnp.float32)
    return _bf(xf * xf)


def _key_bound(keys, groups):
    return functools.reduce(jnp.maximum, [jnp.max(_dot(_squares(k), groups), axis=0, keepdims=True) for k in keys])


def _bound(log, qs, k2max, groups, n_groups, extra=0.0):
    if log is None:
        return None
    per_group = _dot(_squares(qs), groups) * k2max
    ones = jnp.ones((LANES, LANES), jnp.bfloat16)
    return jnp.sqrt(_dot(_bf(per_group), ones) * (n_groups / LANES)) * BOUND_SLACK + extra


FAST_UNROLL = 2


def _loop(n, log, body):
    unroll = 1 if log is None else min(FAST_UNROLL, n)

    def step(t, worst):
        inner = None if log is None else _ShiftLog(worst)
        for u in range(unroll):
            body(t * unroll + u, inner)
        return worst if inner is None else inner.worst

    worst = lax.fori_loop(0, n // unroll, step, jnp.zeros((1, 1), jnp.float32) if log is None else log.worst)
    if log is not None:
        log.worst = worst
    for i in range(n - n % unroll, n):
        body(i, log)


def _aligned(start, multiple):
    return start if isinstance(start, int) else pl.multiple_of(start, multiple)


def _normalised(o_ext):
    return o_ext[:, 0:LANES] * (1.0 / o_ext[:, LANES:2 * LANES])


def _swap8(x):
    lane = lax.broadcasted_iota(jnp.int32, (1, LANES), 1)
    return jnp.where((lane & 15) < 8, pltpu.roll(x, LANES - 8, 1), pltpu.roll(x, 8, 1))


def _rope(x, cos, sin):
    outs = []
    for c in range(x.shape[1] // LANES):
        sl = slice(LANES * c, LANES * (c + 1))
        xc = x[:, sl]
        outs.append(xc * cos[:, sl] + _swap8(xc) * sin[:, sl])
    return outs[0] if len(outs) == 1 else jnp.concatenate(outs, axis=1)


def _tile4(x):
    return x + pltpu.roll(x, 32, 1) + pltpu.roll(x, 64, 1) + pltpu.roll(x, 96, 1)


def _params(*sem):
    return pltpu.CompilerParams(dimension_semantics=sem, vmem_limit_bytes=VMEM_LIMIT)


ADA_TN = 1536


def _ada_kernel(m_ref, w_ref, b_ref, o_ref):
    m = m_ref[...]
    s = m * jax.nn.sigmoid(m)
    o_ref[0] = _dot(_bf(s), _bf(w_ref[0])) + b_ref[0]


def _ada(m, w_ada, b_ada):
    n = 6 * D_MODEL
    return pl.pallas_call(
        _ada_kernel,
        grid=(DEPTH, n // ADA_TN),
        in_specs=[
            pl.BlockSpec((N_MOD_ROWS, D_MODEL), lambda l, j: (0, 0)),
            pl.BlockSpec((1, D_MODEL, ADA_TN), lambda l, j: (l, 0, j)),
            pl.BlockSpec((1, 1, ADA_TN), lambda l, j: (l, 0, j)),
        ],
        out_specs=pl.BlockSpec((1, N_MOD_ROWS, ADA_TN), lambda l, j: (l, 0, j)),
        out_shape=jax.ShapeDtypeStruct((DEPTH, N_MOD_ROWS, n), jnp.float32),
        compiler_params=_params("parallel", "parallel"),
        name="ada",
    )(m, w_ada, b_ada.reshape(DEPTH, 1, n))


def _lam_kernel(lq1_ref, lk1_ref, lq2_ref, lk2_ref, init_ref, o_ref):
    init = init_ref[...]
    a = jnp.exp(jnp.sum(lq1_ref[...] * lk1_ref[...], axis=-1, keepdims=True))
    b = jnp.exp(jnp.sum(lq2_ref[...] * lk2_ref[...], axis=-1, keepdims=True))
    lam = a - b + init
    post = 1.0 - init
    for l in range(DEPTH):
        o_ref[l, 0:1, :] = jnp.broadcast_to(lam[l:l + 1], (1, LANES))
        o_ref[l, 1:2, :] = jnp.broadcast_to(post[l:l + 1], (1, LANES))


def _lam_consts(lq1, lk1, lq2, lk2):
    init = np.array([[0.8 - 0.6 * math.exp(-0.3 * l)] for l in range(DEPTH)], np.float32)
    return pl.pallas_call(
        _lam_kernel,
        out_shape=jax.ShapeDtypeStruct((DEPTH, 2, LANES), jnp.float32),
        name="diff_lambda",
    )(lq1, lk1, lq2, lk2, jnp.asarray(init))


N_DROW = 2 * NA_WIN_ROWS - 1
N_DCOL = 2 * NA_WIN_COLS - 1


def _bias_kernel(rpb_ref, o_ref):
    l = pl.program_id(0)
    h = pl.program_id(1)
    base = (l * NA_HEADS + h) * (N_DROW * N_DCOL)
    cq = lax.broadcasted_iota(jnp.int32, (GRID_W, LANES), 0)
    lane = lax.broadcasted_iota(jnp.int32, (GRID_W, LANES), 1)
    ck = lane & (GRID_W - 1)
    dcol = jnp.clip(ck - cq, -(NA_WIN_COLS - 1), NA_WIN_COLS - 1) + (NA_WIN_COLS - 1)
    hi = lane >= GRID_W
    for a in range(N_DROW - 1):
        acc = jnp.zeros((GRID_W, LANES), jnp.float32)
        for j in range(N_DCOL):
            lo_v = rpb_ref[base + a * N_DCOL + j]
            hi_v = rpb_ref[base + (a + 1) * N_DCOL + j]
            acc = jnp.where(dcol == j, jnp.where(hi, hi_v, lo_v), acc)
        o_ref[0, 0, a] = acc * LOG2E


def _bias_tiles(na_rpb):
    return pl.pallas_call(
        _bias_kernel,
        grid=(DEPTH, NA_HEADS),
        in_specs=[pl.BlockSpec(memory_space=pltpu.SMEM)],
        out_specs=pl.BlockSpec((1, 1, N_DROW - 1, GRID_W, LANES), lambda l, h: (l, h, 0, 0, 0)),
        out_shape=jax.ShapeDtypeStruct((DEPTH, NA_HEADS, N_DROW - 1, GRID_W, LANES), jnp.float32),
        compiler_params=_params("parallel", "parallel"),
        name="na_bias_tiles",
    )(na_rpb.reshape(-1))


def _row_group(i):
    return jnp.where(i < TILES_PROMPT, 0, 1 + (i - TILES_PROMPT) // (DEC_SEQ // TM))


def _split_specs(width, first):
    return [pl.BlockSpec((TM, width), lambda i: (jnp.minimum(first + i, TILES_PROMPT - 1), 0)),
            pl.BlockSpec((TM, width), lambda i: (jnp.maximum(first + i - TILES_PROMPT, 0), 0))]


def _x_specs(split, first):
    if not split:
        return [pl.BlockSpec((TM, D_MODEL), lambda i: (first + i, 0))]
    return _split_specs(D_MODEL, first)


def _read_tile(refs, first):
    if len(refs) == 1:
        return refs[0][...]
    return jnp.where(first + pl.program_id(0) < TILES_PROMPT, refs[0][...], refs[1][...])


IN_COLS = 2592
IN_QC, IN_CKV, IN_D = 1536, 1920, 2080
TR_ROWS = 256


def _gelu_tanh(x):
    return 0.5 * x * (1.0 + jnp.tanh(math.sqrt(2.0 / math.pi) * (x + 0.044715 * (x * x * x))))


def _sgu(pd, g, w_ref, bt):
    u = _gelu_tanh(pd[:, 0:WIDTH])
    v = _gelu_tanh(pd[:, WIDTH:2 * WIDTH])
    grp = lax.broadcasted_iota(jnp.int32, (1, WIDTH), 1) // SGU_GROUP_DIM
    v2 = v * v
    ms = jnp.zeros_like(v)
    for gi in range(SGU_GROUPS):
        sel = grp == gi
        tot = jnp.sum(jnp.where(sel, v2, 0.0), axis=-1, keepdims=True)
        ms = jnp.where(sel, tot * (1.0 / SGU_GROUP_DIM), ms)
    vg = _bf(v * lax.rsqrt(ms + EPS) * g)
    outs = []
    for c in range(pd.shape[0] // SGU_CHUNK):
        rows = slice(SGU_CHUNK * c, SGU_CHUNK * (c + 1))
        mixed = jnp.zeros((SGU_CHUNK, WIDTH), jnp.float32)
        for gi in range(SGU_GROUPS):
            full = _dot(_bf(w_ref[0, gi]), vg[rows]) + bt[:, gi:gi + 1]
            mixed = jnp.where(grp == gi, full, mixed)
        outs.append(u[rows] * mixed)
    return jnp.concatenate(outs, axis=0)


def _w_in_row_pieces():
    qn = [(IN_QC + 96 * h, MLA_NOPE) for h in range(MLA_HEADS)]
    qp = [(IN_QC + 96 * h + MLA_NOPE, MLA_ROPE) for h in range(MLA_HEADS)]
    seg_c = qn + qp + [(IN_CKV, MLA_KV_RANK + MLA_ROPE)]
    return (0, SEG_A + SEG_B), seg_c, (IN_D, SEG_D)


def _load_w_in(wt_ref, w_scr):
    ab, seg_c, d = _w_in_row_pieces()
    c_rows = jnp.concatenate([wt_ref[0, s:s + n, :] for s, n in seg_c]
                             + [jnp.zeros((SEG_C_PAD, D_MODEL), jnp.float32)], axis=0)
    for t in range(SEG_C // LANES):
        w_scr[:, SEG_A + SEG_B + LANES * t:SEG_A + SEG_B + LANES * (t + 1)] = _bf(c_rows[LANES * t:LANES * (t + 1)].T)
    for (src, n), dst in ((ab, 0), (d, SEG_A + SEG_B + SEG_C)):
        for t in range(n // TR_ROWS):
            rows = wt_ref[0, src + TR_ROWS * t:src + TR_ROWS * (t + 1), :]
            w_scr[:, dst + TR_ROWS * t:dst + TR_ROWS * (t + 1)] = _bf(rows.T)


def _inproj_kernel(n_x, *refs):
    x_refs = refs[:n_x]
    (g_ref, sh_ref, sc_ref, wt_ref, sg_ref, sw_ref, sbt_ref, pa_ref, pb_ref, pc_ref, od_ref, w_scr) = refs[n_x:]

    @pl.when(pl.program_id(0) == 0)
    def _():
        _load_w_in(wt_ref, w_scr)

    h = _rms(_read_tile(x_refs, 0), g_ref[0]) * (1.0 + sc_ref[...]) + sh_ref[...]
    hb = _bf(h)
    off = SEG_A + SEG_B + SEG_C
    od_ref[...] = _sgu(_dot(hb, w_scr[:, off:off + SEG_D]), sg_ref[0], sw_ref, sbt_ref[0])
    off = 0
    for ref in (pa_ref, pb_ref, pc_ref):
        n = ref.shape[1]
        ref[...] = _dot(hb, w_scr[:, off:off + n])
        off += n


def _inproj(l, xs, g_mix, mod, w_in_t, sgu_g, sgu_w, sgu_bt):
    def mod_spec(j):
        return pl.BlockSpec((None, None, 1, D_MODEL), lambda i: (l, _row_group(i), 0, j))

    widths = (SEG_A, SEG_B, SEG_C, WIDTH)
    return pl.pallas_call(
        functools.partial(_inproj_kernel, len(xs)),
        grid=(N_TOK // TM,),
        in_specs=_x_specs(len(xs) == 2, 0) + [
            pl.BlockSpec((1, 1, D_MODEL), lambda i: (l, 0, 0)),
            mod_spec(0), mod_spec(1),
            pl.BlockSpec((1, IN_COLS, D_MODEL), lambda i: (l, 0, 0), pipeline_mode=pl.Buffered(1)),
            pl.BlockSpec((1, 1, WIDTH), lambda i: (l, 0, 0)),
            pl.BlockSpec((1, SGU_GROUPS, SGU_CHUNK, SGU_CHUNK), lambda i: (l, 0, 0, 0)),
            pl.BlockSpec((1, SGU_CHUNK, SGU_GROUPS), lambda i: (l, 0, 0)),
        ],
        out_specs=[pl.BlockSpec((TM, n), lambda i: (i, 0)) for n in widths],
        out_shape=[jax.ShapeDtypeStruct((N_TOK, n), jnp.float32) for n in widths],
        scratch_shapes=[pltpu.VMEM((D_MODEL, IN_COLS_P), jnp.bfloat16)],
        compiler_params=_params("arbitrary"),
        name="inproj",
    )(*xs, g_mix, mod, mod, w_in_t, sgu_g, sgu_w, sgu_bt)


C_QN, C_QP, C_CKV, C_KPE = 0, 256, 384, 512


def _stack_heads(qp):
    lo = _lane_range(0, 64)
    return jnp.concatenate([_bf(jnp.where(lo, qp, 0.0)), _bf(jnp.where(lo, 0.0, qp))], axis=0)


def _unstack_heads(o, n):
    return jnp.where(_lane_range(0, 64), o[0:n], o[n:2 * n])


def _pair_t(c_ref, j):
    return jnp.concatenate([c_ref[2 * j], c_ref[2 * j + 1]], axis=0)


def _stack_components(qp):
    return jnp.concatenate([_bf(jnp.where(_lane_range(32 * t, 32 * (t + 1)), qp, 0.0)) for t in range(4)], axis=0)


def _group_mean_sq(x, groups, size):
    sq = x * x
    hi = _bf(sq)
    rest = sq - hi.astype(jnp.float32)
    mid = _bf(rest)
    lo = _bf(rest - mid.astype(jnp.float32))
    return (_dot(hi, groups) + _dot(mid, groups) + _dot(lo, groups)) * (1.0 / size)


def _diff_finish(o, n, lam, post, g2, by_head):
    den = o[:, LANES:2 * LANES]
    outs = []
    for t in range(2):
        p1 = o[2 * t * n:(2 * t + 1) * n, 0:LANES] * (1.0 / den[2 * t * n:(2 * t + 1) * n])
        p2 = o[(2 * t + 1) * n:(2 * t + 2) * n, 0:LANES] * (lam / den[(2 * t + 1) * n:(2 * t + 2) * n])
        outs.append(p1 - p2)
    d = jnp.where(_lane_range(0, 64), outs[0], outs[1])
    return d * lax.rsqrt(_group_mean_sq(d, by_head, DIFF_V_DIM) + EPS) * g2 * post


def _mla_groups(j):
    return _group_matrix(2 * LANES, 2,
                         lambda t: (LANES + MLA_ROPE * (2 * j + t), LANES + MLA_ROPE * (2 * j + t + 1)))


def _mla_queries(qn_pair, qp_all, j):
    halves = []
    for t in range(2):
        h = 2 * j + t
        halves.append(jnp.concatenate([
            _bf(jnp.where(_lane_range(64 * t, 64 * (t + 1)), qn_pair, 0.0)),
            _bf(jnp.where(_lane_range(MLA_ROPE * h, MLA_ROPE * (h + 1)), qp_all, 0.0))], axis=1))
    return jnp.concatenate(halves, axis=0)


def _write_heads_t(p_ref, rows, col0, out_ref, bb):
    xt = p_ref[rows, col0:col0 + WIDTH].T
    for h in range(N_HEADS):
        out_ref[bb, 0, h] = xt[64 * h:64 * (h + 1)]
    _clear_other_layers(out_ref, bb)


def _clear_other_layers(out_ref, bb):
    if out_ref.shape[1] > 1:
        out_ref[bb, 1:] = jnp.zeros(out_ref.shape[1:], jnp.float32)[1:]


def _mix_prompt_kernel(n_prev, *refs):
    ins, outs = refs[:8], refs[8 + n_prev:]
    log = _ShiftLog()
    _mix_prompt_pass(ins, outs, log)

    @pl.when(log.unsafe())
    def _():
        _mix_prompt_pass(ins, outs, None)


def _mix_prompt_pass(ins, outs, log):
    pa_ref, pb_ref, pc_ref, cst_ref, gsub_ref, gckv_ref, wuk_ref, wuv_ref = ins
    o_ref, nak_ref, nav_ref, dk_ref, dv_ref, ckv_ref, kpe_ref = outs
    first_pass = log is not None
    c_a = HEAD_DIM ** -0.5 * LOG2E
    c_b = DIFF_QK_DIM ** -0.5 * LOG2E
    c_c = (MLA_NOPE + MLA_ROPE) ** -0.5 * LOG2E
    lam = cst_ref[0, 0:1, 0:1]
    post = cst_ref[0, 1:2, 0:1]
    wuk, wuv = _bf(wuk_ref[0]), _bf(wuv_ref[0])
    by_head, by_comp = _group_matrix(LANES, 2), _group_matrix(LANES, 4)

    def sequence(bb, log):
        rows = pl.ds(_aligned(bb * SEQ, SEQ), SEQ)
        for j in range(N_PAIRS):
            cols = slice(LANES * j, LANES * (j + 1))
            k = _bf(pa_ref[rows, WIDTH + LANES * j:WIDTH + LANES * (j + 1)])
            v = _with_ones(_bf(pa_ref[rows, 2 * WIDTH + LANES * j:2 * WIDTH + LANES * (j + 1)]))
            qs = _stack_heads(pa_ref[rows, cols] * c_a)
            o = _attend([_dot_nt(qs, k)], [v], log, _bound(log, qs, _key_bound([k], by_head), by_head, 2))
            o_ref[rows, cols] = _unstack_heads(_normalised(o), SEQ)
        if first_pass:
            _write_heads_t(pa_ref, rows, WIDTH, nak_ref, bb)
            _write_heads_t(pa_ref, rows, 2 * WIDTH, nav_ref, bb)
        for j in range(N_PAIRS):
            cols = slice(LANES * j, LANES * (j + 1))
            k = _bf(pb_ref[rows, WIDTH + LANES * j:WIDTH + LANES * (j + 1)])
            v = _with_ones(_bf(pb_ref[rows, 2 * WIDTH + LANES * j:2 * WIDTH + LANES * (j + 1)]))
            qs = _stack_components(pb_ref[rows, cols] * c_b)
            o = _attend([_dot_nt(qs, k)], [v], log, _bound(log, qs, _key_bound([k], by_comp), by_comp, 4))
            o_ref[rows, WIDTH + LANES * j:WIDTH + LANES * (j + 1)] = _diff_finish(o, SEQ, lam, post, gsub_ref[0], by_head)
        if first_pass:
            _write_heads_t(pb_ref, rows, WIDTH, dk_ref, bb)
            _write_heads_t(pb_ref, rows, 2 * WIDTH, dv_ref, bb)
        ckv = _rms(pc_ref[rows, C_CKV:C_CKV + MLA_KV_RANK], gckv_ref[0])
        kpe_slot = pc_ref[rows, C_KPE:C_KPE + LANES]
        if first_pass:
            ckv_ref[bb, 0] = ckv
            _clear_other_layers(ckv_ref, bb)
            kpe_ref[bb, 0] = kpe_slot.T[0:MLA_ROPE]
            _clear_other_layers(kpe_ref, bb)
        ckv_b = _bf(ckv)
        kn = _bf(_dot(ckv_b, wuk))
        vv = _bf(_dot(ckv_b, wuv))
        kpe4 = _bf(_tile4(kpe_slot))
        qn = pc_ref[rows, C_QN:C_QN + WIDTH] * c_c
        qp = pc_ref[rows, C_QP:C_QP + LANES] * c_c
        for j in range(N_PAIRS):
            cols = slice(LANES * j, LANES * (j + 1))
            k = jnp.concatenate([kn[:, cols], kpe4], axis=1)
            qs = _mla_queries(qn[:, cols], qp, j)
            groups = _mla_groups(j)
            o = _attend([_dot_nt(qs, k)], [_with_ones(vv[:, cols])], log,
                        _bound(log, qs, _key_bound([k], groups), groups, 2))
            o_ref[rows, 2 * WIDTH + LANES * j:2 * WIDTH + LANES * (j + 1)] = _unstack_heads(_normalised(o), SEQ)

    _loop(PB, log, sequence)


def _mix_prompt(l, pa, pb, pc, cst, g_sub2, g_ckv, w_uk, w_uv, prev):
    n_prev = len(prev)
    tails = [(NA_HEADS, HEAD_DIM, SEQ)] * 2 + [(DIFF_HEADS, 64, SEQ)] * 2 + [(SEQ, MLA_KV_RANK), (MLA_ROPE, SEQ)]

    def cache_spec(tail):
        if l == 0:
            return pl.BlockSpec((PB, DEPTH) + tail, lambda b: (b, 0) + (0,) * len(tail))
        return pl.BlockSpec((PB, 1) + tail, lambda b: (b, l) + (0,) * len(tail))

    def rows(width):
        return pl.BlockSpec((PB * SEQ, width), lambda b: (b, 0))

    def layer(*tail):
        return pl.BlockSpec((1,) + tail, lambda b: (l,) + (0,) * len(tail))

    return pl.pallas_call(
        functools.partial(_mix_prompt_kernel, n_prev),
        grid=(BATCH // PB,),
        in_specs=[rows(SEG_A), rows(SEG_B), rows(SEG_C), layer(2, LANES), layer(1, LANES), layer(1, MLA_KV_RANK),
                  layer(MLA_KV_RANK, WIDTH), layer(MLA_KV_RANK, WIDTH)] + [pl.BlockSpec(memory_space=pl.ANY)] * n_prev,
        out_specs=[rows(O_ATT)] + [cache_spec(t) for t in tails],
        out_shape=[jax.ShapeDtypeStruct((N_PROMPT, O_ATT), jnp.float32)]
        + [jax.ShapeDtypeStruct((BATCH, DEPTH) + t, jnp.float32) for t in tails],
        input_output_aliases={8 + i: 1 + i for i in range(n_prev)},
        compiler_params=_params("parallel"),
        name="mix_prompt",
    )(pa, pb, pc, cst, g_sub2, g_ckv, w_uk, w_uv, *prev)


def _na_row_groups():
    kh = min(NA_WIN_ROWS, GRID_ROWS)
    r0s = [min(max(r - kh // 2, 0), GRID_ROWS - kh) for r in range(GRID_ROWS)]
    groups = []
    for r, r0 in enumerate(r0s):
        if groups and groups[-1][2] == r0:
            groups[-1][1] = r
        else:
            groups.append([r, r, r0])
    return kh, [tuple(g) for g in groups]


def _na_sample(pa_ref, ck_ref, cv_ref, tt_ref, o_ref, log):
    c = HEAD_DIM ** -0.5 * LOG2E
    kh, groups = _na_row_groups()
    lk = kh * GRID_W
    edge = [g for g in groups if g[1] > g[0]]
    inner = [g for g in groups if g[1] == g[0]]
    depth = inner[0][0] - inner[0][2]
    assert all(g[0] - g[2] == depth for g in inner) and [g[0] for g in inner] == list(range(inner[0][0], inner[-1][0] + 1))

    def in_window(n):
        cq = lax.broadcasted_iota(jnp.int32, (n, lk), 0) & (GRID_W - 1)
        ck = lax.broadcasted_iota(jnp.int32, (n, lk), 1) & (GRID_W - 1)
        c0 = jnp.clip(cq - NA_WIN_COLS // 2, 0, GRID_W - NA_WIN_COLS)
        return (ck >= c0) & (ck < c0 + NA_WIN_COLS)

    for j in range(N_PAIRS):
        cols = slice(LANES * j, LANES * (j + 1))
        kcols = slice(WIDTH + LANES * j, WIDTH + LANES * (j + 1))
        vcols = slice(2 * WIDTH + LANES * j, 2 * WIDTH + LANES * (j + 1))
        kc_t = _bf(_pair_t(ck_ref, j))
        vc = _with_ones(_bf(_pair_t(cv_ref, j).T))
        by_head = _group_matrix(LANES, 2)
        k2max = _key_bound([_bf(pa_ref[:, kcols]), _bf(_pair_t(ck_ref, j).T)], by_head)
        tmax = functools.reduce(jnp.maximum, [tt_ref[2 * j + t, a] for t in range(2) for a in range(N_DROW - 1)])
        bplus = jnp.maximum(jnp.max(jnp.max(tmax, axis=-1, keepdims=True), axis=0, keepdims=True), 0.0)

        def group(row0, key0, offsets, log):
            n = len(offsets) * GRID_W
            rows, keys = pl.ds(row0, n), pl.ds(key0, lk)
            qg = _stack_heads(pa_ref[rows, cols] * c)
            k = _bf(pa_ref[keys, kcols])
            v = _with_ones(_bf(pa_ref[keys, vcols]))
            bias = jnp.concatenate([
                jnp.concatenate([tt_ref[2 * j + t, 2 * i - off + NA_WIN_ROWS - 1] for i in range(kh // 2)], axis=1)
                for t in range(2) for off in offsets], axis=0)
            s_loc = jnp.where(in_window(2 * n), _dot_nt(qg, k) + bias, NEG_INF)
            o = _attend([_dot(qg, kc_t), s_loc], [vc, v], log, _bound(log, qg, k2max, by_head, 2, bplus))
            o_ref[rows, cols] = _unstack_heads(_normalised(o), n)

        for (r_lo, r_hi, r0) in edge:
            group(r_lo * GRID_W, r0 * GRID_W, [r - r0 for r in range(r_lo, r_hi + 1)], log)

        def inner_row(i, log):
            r = inner[0][0] + i
            group(_aligned(r * GRID_W, GRID_W), _aligned((r - depth) * GRID_W, GRID_W), [depth], log)

        _loop(len(inner), log, inner_row)


def _diff_sample(pb_ref, ck_ref, cv_ref, cos_ref, sin_ref, cst_ref, g_ref, o_ref, log):
    c = DIFF_QK_DIM ** -0.5 * LOG2E
    lam = cst_ref[0, 0:1, 0:1]
    post = cst_ref[0, 1:2, 0:1]
    for j in range(N_PAIRS):
        cols = slice(LANES * j, LANES * (j + 1))
        k_new = _bf(_rope(pb_ref[:, WIDTH + LANES * j:WIDTH + LANES * (j + 1)], cos_ref[:, cols], sin_ref[:, cols]))
        kc_t = _bf(_pair_t(ck_ref, j))
        vc = _with_ones(_bf(_pair_t(cv_ref, j).T))
        v = _with_ones(_bf(pb_ref[:, 2 * WIDTH + LANES * j:2 * WIDTH + LANES * (j + 1)]))
        by_head, by_comp = _group_matrix(LANES, 2), _group_matrix(LANES, 4)
        k2max = _key_bound([k_new, _bf(_pair_t(ck_ref, j).T)], by_comp)

        def block(qi, log):
            rows = pl.ds(_aligned(qi * QB, QB), QB)
            qs = _stack_components(_rope(pb_ref[rows, cols], cos_ref[rows, cols], sin_ref[rows, cols]) * c)
            o = _attend([_dot(qs, kc_t), _dot_nt(qs, k_new)], [vc, v], log, _bound(log, qs, k2max, by_comp, 4))
            o_ref[rows, WIDTH + LANES * j:WIDTH + LANES * (j + 1)] = _diff_finish(o, QB, lam, post, g_ref[0], by_head)

        _loop(DEC_SEQ // QB, log, block)


def _mla_sample(pc_ref, cckv_ref, ckpe_ref, cosq_ref, sinq_ref, cosk_ref, sink_ref, gckv_ref, wuk_ref, wuv_ref, o_ref,
                log):
    c = (MLA_NOPE + MLA_ROPE) ** -0.5 * LOG2E
    wuk, wuv = _bf(wuk_ref[0]), _bf(wuv_ref[0])
    ckv_new = _bf(_rms(pc_ref[:, C_CKV:C_CKV + MLA_KV_RANK], gckv_ref[0]))
    ckv_old = _bf(cckv_ref[...])
    kpe_new = _bf(_tile4(_rope(pc_ref[:, C_KPE:C_KPE + LANES], cosk_ref[...], sink_ref[...])))
    kpe_old = _bf(jnp.concatenate([ckpe_ref[...]] * MLA_HEADS, axis=0).T)
    kn_new, kn_old = _bf(_dot(ckv_new, wuk)), _bf(_dot(ckv_old, wuk))
    v_new, v_old = _bf(_dot(ckv_new, wuv)), _bf(_dot(ckv_old, wuv))
    for j in range(N_PAIRS):
        cols = slice(LANES * j, LANES * (j + 1))
        k_old = jnp.concatenate([kn_old[:, cols], kpe_old], axis=1)
        k_new = jnp.concatenate([kn_new[:, cols], kpe_new], axis=1)
        vo, vn = _with_ones(v_old[:, cols]), _with_ones(v_new[:, cols])
        groups = _mla_groups(j)
        k2max = _key_bound([k_old, k_new], groups)

        def block(qi, log):
            rows = pl.ds(_aligned(qi * QB, QB), QB)
            qn = pc_ref[rows, C_QN + LANES * j:C_QN + LANES * (j + 1)] * c
            qp = _rope(pc_ref[rows, C_QP:C_QP + LANES], cosq_ref[rows, :], sinq_ref[rows, :]) * c
            qs = _mla_queries(qn, qp, j)
            o = _attend([_dot_nt(qs, k_old), _dot_nt(qs, k_new)], [vo, vn], log, _bound(log, qs, k2max, groups, 2))
            o_ref[rows, 2 * WIDTH + LANES * j:2 * WIDTH + LANES * (j + 1)] = _unstack_heads(_normalised(o), QB)

        _loop(DEC_SEQ // QB, log, block)


def _mix_sample_kernel(pa_ref, pb_ref, pc_ref, cnak_ref, cnav_ref, cdk_ref, cdv_ref, cckv_ref, ckpe_ref, tt_ref,
                       cosb_ref, sinb_ref, cosq_ref, sinq_ref, cosk_ref, sink_ref,
                       cst_ref, gsub_ref, gckv_ref, wuk_ref, wuv_ref, o_ref):
    def run(log):
        _na_sample(pa_ref, cnak_ref, cnav_ref, tt_ref, o_ref, log)
        _diff_sample(pb_ref, cdk_ref, cdv_ref, cosb_ref, sinb_ref, cst_ref, gsub_ref, o_ref, log)
        _mla_sample(pc_ref, cckv_ref, ckpe_ref, cosq_ref, sinq_ref, cosk_ref, sink_ref, gckv_ref, wuk_ref, wuv_ref,
                    o_ref, log)

    log = _ShiftLog()
    run(log)


def _mix_sample(l, pa, pb, pc, caches_t, tt, tables, cst, g_sub2, g_ckv, w_uk, w_uv):
    first = N_PROMPT // DEC_SEQ

    def rows(width):
        return pl.BlockSpec((DEC_SEQ, width), lambda b: (first + b, 0))

    def cache(*tail):
        return pl.BlockSpec((None, None) + tail, lambda b: (b, l) + (0,) * len(tail))

    def layer(*tail):
        return pl.BlockSpec((1,) + tail, lambda b: (l,) + (0,) * len(tail))

    def table(width):
        return pl.BlockSpec((DEC_SEQ, width), lambda b: (0, 0), pipeline_mode=pl.Buffered(1))

    kv_t = cache(N_HEADS, 64, PAST_LEN)
    return pl.pallas_call(
        _mix_sample_kernel,
        grid=(DEC_BATCH,),
        in_specs=[rows(SEG_A), rows(SEG_B), rows(SEG_C), kv_t, kv_t, kv_t, kv_t,
                  cache(PAST_LEN, MLA_KV_RANK), cache(MLA_ROPE, PAST_LEN),
                  pl.BlockSpec((None, NA_HEADS, N_DROW - 1, GRID_W, LANES), lambda b: (l, 0, 0, 0, 0),
                               pipeline_mode=pl.Buffered(1)),
                  table(WIDTH), table(WIDTH), table(LANES), table(LANES), table(LANES), table(LANES),
                  layer(2, LANES), layer(1, LANES), layer(1, MLA_KV_RANK),
                  layer(MLA_KV_RANK, WIDTH), layer(MLA_KV_RANK, WIDTH)],
        out_specs=pl.BlockSpec((DEC_SEQ, O_ATT), lambda b: (b, 0)),
        out_shape=jax.ShapeDtypeStruct((N_SAMPLE, O_ATT), jnp.float32),
        compiler_params=_params("parallel"),
        name="mix_sample",
    )(pa, pb, pc, *caches_t, tt, *tables, cst, g_sub2, g_ckv, w_uk, w_uv)


FF_CHUNK = 1024


def _outffn_kernel(n_x, first, final, *refs):
    x_refs, o_refs = refs[:n_x], refs[n_x:n_x + 2]
    (od_ref, wout_ref, g1_ref, gffn_ref, sh2_ref, sc2_ref, g2_ref, w1_ref, w2_ref, gfin_ref, y_ref) = refs[n_x + 2:]
    acc = (_dot(_bf(_read_tile(o_refs, first)), wout_ref[0, 0:O_ATT, :])
           + _dot(_bf(od_ref[...]), wout_ref[0, O_ATT:O_ATT + WIDTH, :]))
    x1 = _read_tile(x_refs, first) + g1_ref[...] * acc
    hf = _bf(_rms(x1, gffn_ref[0]) * (1.0 + sc2_ref[...]) + sh2_ref[...])
    acc = jnp.zeros((TM, D_MODEL), jnp.float32)
    for c in range(D_FF // FF_CHUNK):
        cols = slice(FF_CHUNK * c, FF_CHUNK * (c + 1))
        a = jnp.square(jnp.maximum(_dot(hf, w1_ref[0, :, cols]), 0.0))
        acc += _dot(_bf(a), w2_ref[0, cols, :])
    y = x1 + g2_ref[...] * acc
    if final:
        y = _rms(y, gfin_ref[...])
    y_ref[...] = y


def _outffn(l, xs, o_p, o_s, od, w_out, g_ffn, mod, w1, w2, g_final, first, n_tiles):
    def mod_spec(j):
        return pl.BlockSpec((None, None, 1, D_MODEL), lambda i: (l, _row_group(first + i), 0, j))

    def resident(shape):
        return pl.BlockSpec(shape, lambda i: (l,) + (0,) * (len(shape) - 1), pipeline_mode=pl.Buffered(1))

    return pl.pallas_call(
        functools.partial(_outffn_kernel, len(xs), first, l == DEPTH - 1),
        grid=(n_tiles,),
        in_specs=_x_specs(len(xs) == 2, first) + _split_specs(O_ATT, first) + [
            pl.BlockSpec((TM, WIDTH), lambda i: (first + i, 0)),
            resident((1, 4 * WIDTH, D_MODEL)),
            mod_spec(2),
            pl.BlockSpec((1, 1, D_MODEL), lambda i: (l, 0, 0)),
            mod_spec(3), mod_spec(4), mod_spec(5),
            resident((1, D_MODEL, D_FF)),
            resident((1, D_FF, D_MODEL)),
            pl.BlockSpec((1, D_MODEL), lambda i: (0, 0)),
        ],
        out_specs=pl.BlockSpec((TM, D_MODEL), lambda i: (i, 0)),
        out_shape=jax.ShapeDtypeStruct((n_tiles * TM, D_MODEL), jnp.float32),
        compiler_params=_params("parallel"),
        name="outffn",
    )(*xs, o_p, o_s, od, w_out, mod, g_ffn, mod, mod, mod, w1, w2, g_final)


def _rope32_tables():
    t = np.arange(DEC_SEQ)
    rows, cols = (t // GRID_W).astype(np.float64), (t % GRID_W).astype(np.float64)
    half = 8
    freqs = ROPE_BASE ** (-np.arange(half, dtype=np.float64) / half)
    cos, sin = [], []
    for pos in (rows, cols):
        ang = pos[:, None] * freqs[None, :]
        cos += [np.cos(ang), np.cos(ang)]
        sin += [-np.sin(ang), np.sin(ang)]
    return np.concatenate(cos, axis=1).astype(np.float32), np.concatenate(sin, axis=1).astype(np.float32)


def _rope_tables():
    c32, s32 = _rope32_tables()
    tile = lambda a, n: np.tile(a, (1, n))
    pad = np.zeros((DEC_SEQ, 96), np.float32)
    cos_k = np.concatenate([c32, pad + 1.0], axis=1)
    sin_k = np.concatenate([s32, pad], axis=1)
    return (tile(c32, 8), tile(s32, 8),
            tile(c32, 4), tile(s32, 4),
            cos_k, sin_k)


def kernel(x_prompt, x_sample, cache_na_k, cache_na_v, cache_diff_k, cache_diff_v, cache_mla_ckv, cache_mla_kpe, c, c_ctx, w_ada, b_ada, g_mix, g_ffn, w_in, w_out, na_rpb, diff_lq1, diff_lk1, diff_lq2, diff_lk2, diff_g_subln, mla_g_ckv, mla_w_uk, mla_w_uv, sgu_g, sgu_w, sgu_b, w_ff1, w_ff2, g_final):
    f32 = jnp.float32
    m = jnp.concatenate([c_ctx[None, :], c, jnp.zeros((N_MOD_ROWS - 1 - DEC_BATCH, D_MODEL), f32)], axis=0)
    mod = _ada(m, w_ada, b_ada).reshape(DEPTH, N_MOD_ROWS, 1, 6 * D_MODEL)
    cst = _lam_consts(diff_lq1, diff_lk1, diff_lq2, diff_lk2)
    tt = _bias_tiles(na_rpb)
    tables = [jnp.asarray(t) for t in _rope_tables()]

    t_last = lambda a: jnp.swapaxes(a, -1, -2)
    w_in_t = t_last(w_in)
    caches_t = (t_last(cache_na_k), t_last(cache_na_v), t_last(cache_diff_k), t_last(cache_diff_v),
                cache_mla_ckv, t_last(cache_mla_kpe))
    w_out_b, w1_b, w2_b = _bf(w_out), _bf(w_ff1), _bf(w_ff2)
    g_mix3 = g_mix.reshape(DEPTH, 1, D_MODEL)
    g_ffn3 = g_ffn.reshape(DEPTH, 1, D_MODEL)
    g_sub2 = jnp.tile(diff_g_subln, (1, 2)).reshape(DEPTH, 1, LANES)
    g_ckv3 = mla_g_ckv.reshape(DEPTH, 1, MLA_KV_RANK)
    sgu_g3 = sgu_g.reshape(DEPTH, 1, WIDTH)
    sgu_bt = sgu_b.transpose(0, 2, 1)
    g_fin2 = g_final.reshape(1, D_MODEL)

    xs = (x_prompt.reshape(N_PROMPT, D_MODEL), x_sample.reshape(N_SAMPLE, D_MODEL))
    new = ()
    for l in range(DEPTH):
        pa, pb, pc, od = _inproj(l, xs, g_mix3, mod, w_in_t, sgu_g3, sgu_w, sgu_bt)
        o_p, *new = _mix_prompt(l, pa, pb, pc, cst, g_sub2, g_ckv3, mla_w_uk, mla_w_uv, new)
        o_s = _mix_sample(l, pa, pb, pc, caches_t, tt, tables, cst, g_sub2, g_ckv3, mla_w_uk, mla_w_uv)
        ffn = functools.partial(_outffn, l, xs, o_p, o_s, od, w_out_b, g_ffn3, mod, w1_b, w2_b, g_fin2)
        if l < DEPTH - 1:
            xs = (ffn(0, TILES_PROMPT + TILES_SAMPLE),)
        else:
            xs = (ffn(0, TILES_PROMPT), ffn(TILES_PROMPT, TILES_SAMPLE))
    y_prompt = xs[0].reshape(BATCH, SEQ, D_MODEL)
    y_sample = xs[1].reshape(DEC_BATCH, DEC_SEQ, D_MODEL)
    na_k, na_v, diff_k, diff_v, mla_ckv, mla_kpe = new
    return (y_prompt, y_sample, t_last(na_k), t_last(na_v), t_last(diff_k), t_last(diff_v), mla_ckv, t_last(mla_kpe))
```

```python
import functools
import math

import numpy as np
import jax
import jax.numpy as jnp
from jax import lax
from jax.experimental import pallas as pl
from jax.experimental.pallas import tpu as pltpu

D_MODEL = 1024
BATCH = 16
SEQ = 256
DEPTH = 4
DEC_BATCH = 2
DEC_SEQ = 1024
PAST_LEN = 512
GRID_W = 64
GRID_ROWS = DEC_SEQ // GRID_W
HEAD_DIM = 64
NA_HEADS = 4
NA_WIN_ROWS = 8
NA_WIN_COLS = 16
DIFF_HEADS = 4
DIFF_QK_DIM = 32
DIFF_V_DIM = 64
MLA_HEADS = 4
MLA_NOPE = 64
MLA_ROPE = 32
MLA_V = 64
MLA_KV_RANK = 128
SGU_GROUPS = 4
SGU_GROUP_DIM = 64
SGU_CHUNK = 128
D_FF = 4 * D_MODEL
ROPE_BASE = 10000.0
EPS = 1e-6
NEG_INF = -1e30
LOG2E = 1.4426950408889634

N_HEADS = 4
N_PAIRS = N_HEADS // 2
LANES = 128
WIDTH = 256
N_PROMPT = BATCH * SEQ
N_SAMPLE = DEC_BATCH * DEC_SEQ
N_TOK = N_PROMPT + N_SAMPLE
N_MOD_ROWS = 8

SEG_A = 3 * WIDTH
SEG_B = 3 * WIDTH
SEG_C = 640
SEG_D = 2 * WIDTH
SEG_C_PAD = 96
IN_COLS_P = SEG_A + SEG_B + SEG_C + SEG_D
O_ATT = 3 * WIDTH

TM = 512
TILES_PROMPT = N_PROMPT // TM
TILES_SAMPLE = N_SAMPLE // TM
PB = 2
QB = 256
VMEM_LIMIT = 56 * 1024 * 1024


def _bf(x):
    return x.astype(jnp.bfloat16)


def _dot(a, b):
    return jnp.dot(a, b, preferred_element_type=jnp.float32)


def _dot_nt(a, b):
    return lax.dot_general(a, b, (((1,), (1,)), ((), ())), preferred_element_type=jnp.float32)


def _rms(x, g):
    ms = jnp.mean(x * x, axis=-1, keepdims=True)
    return x * lax.rsqrt(ms + EPS) * g


def _lane_range(lo, hi, width=LANES):
    lane = lax.broadcasted_iota(jnp.int32, (1, width), 1)
    return (lane >= lo) & (lane < hi)


def _with_ones(v):
    return jnp.concatenate([v, jnp.ones((v.shape[0], LANES), jnp.bfloat16)], axis=1)


def _attend(scores, values, log=None, bound=None):
    if bound is None:
        m = functools.reduce(jnp.maximum, [jnp.max(s, axis=-1, keepdims=True) for s in scores])
        shifted = [s - m for s in scores]
    else:
        over = jnp.max(bound - scores[0][:, 0:LANES], axis=0, keepdims=True)
        log.worst = jnp.maximum(log.worst, jnp.min(over, axis=1, keepdims=True))
        shifted = [s - jnp.concatenate([bound] * (s.shape[1] // LANES), axis=1) for s in scores]
    return functools.reduce(lambda a, b: a + b, [_dot(_bf(jnp.exp2(s)), v) for s, v in zip(shifted, values)])


BOUND_SLACK = 1.02
OVERSHOOT_LIMIT = 100.0


class _ShiftLog:
    def __init__(self, worst=None):
        self.worst = jnp.zeros((1, 1), jnp.float32) if worst is None else worst

    def unsafe(self):
        return jnp.logical_not(self.worst[0, 0] < OVERSHOOT_LIMIT)


def _group_matrix(width, n_groups, extra=None):
    i = lax.broadcasted_iota(jnp.int32, (width, LANES), 0)
    j = lax.broadcasted_iota(jnp.int32, (width, LANES), 1)
    size = LANES // n_groups
    hit = (i // size == j // size) & (i < LANES)
    for t in range(n_groups if extra else 0):
        lo, hi = extra(t)
        hit = hit | ((i >= lo) & (i < hi) & (j // size == t))
    return jnp.where(hit, 1.0, 0.0).astype(jnp.bfloat16)


def _squares(x):
    xf = x.astype(jnp.float32)
    return _bf(xf * xf)


def _key_bound(keys, groups):
    return functools.reduce(jnp.maximum, [jnp.max(_dot(_squares(k), groups), axis=0, keepdims=True) for k in keys])


def _bound(log, qs, k2max, groups, n_groups, extra=0.0):
    if log is None:
        return None
    per_group = _dot(_squares(qs), groups) * k2max
    ones = jnp.ones((LANES, LANES), jnp.bfloat16)
    return jnp.sqrt(_dot(_bf(per_group), ones) * (n_groups / LANES)) * BOUND_SLACK + extra


def _loop(n, log, body):
    if log is not None:
        for i in range(n):
            body(i, log)
    else:
        lax.fori_loop(0, n, lambda i, carry: body(i, None) or carry, 0)


def _aligned(start, multiple):
    return start if isinstance(start, int) else pl.multiple_of(start, multiple)


def _normalised(o_ext):
    return o_ext[:, 0:LANES] * (1.0 / o_ext[:, LANES:2 * LANES])


def _swap8(x):
    lane = lax.broadcasted_iota(jnp.int32, (1, LANES), 1)
    return jnp.where((lane & 15) < 8, pltpu.roll(x, LANES - 8, 1), pltpu.roll(x, 8, 1))


def _rope(x, cos, sin):
    outs = []
    for c in range(x.shape[1] // LANES):
        sl = slice(LANES * c, LANES * (c + 1))
        xc = x[:, sl]
        outs.append(xc * cos[:, sl] + _swap8(xc) * sin[:, sl])
    return outs[0] if len(outs) == 1 else jnp.concatenate(outs, axis=1)


def _tile4(x):
    return x + pltpu.roll(x, 32, 1) + pltpu.roll(x, 64, 1) + pltpu.roll(x, 96, 1)


def _params(*sem):
    return pltpu.CompilerParams(dimension_semantics=sem, vmem_limit_bytes=VMEM_LIMIT)


ADA_TN = 1536


def _ada_kernel(m_ref, w_ref, b_ref, o_ref):
    m = m_ref[...]
    s = m * jax.nn.sigmoid(m)
    o_ref[0] = _dot(_bf(s), _bf(w_ref[0])) + b_ref[0]


def _ada(m, w_ada, b_ada):
    n = 6 * D_MODEL
    return pl.pallas_call(
        _ada_kernel,
        grid=(DEPTH, n // ADA_TN),
        in_specs=[
            pl.BlockSpec((N_MOD_ROWS, D_MODEL), lambda l, j: (0, 0)),
            pl.BlockSpec((1, D_MODEL, ADA_TN), lambda l, j: (l, 0, j)),
            pl.BlockSpec((1, 1, ADA_TN), lambda l, j: (l, 0, j)),
        ],
        out_specs=pl.BlockSpec((1, N_MOD_ROWS, ADA_TN), lambda l, j: (l, 0, j)),
        out_shape=jax.ShapeDtypeStruct((DEPTH, N_MOD_ROWS, n), jnp.float32),
        compiler_params=_params("parallel", "parallel"),
        name="ada",
    )(m, w_ada, b_ada.reshape(DEPTH, 1, n))


def _lam_kernel(lq1_ref, lk1_ref, lq2_ref, lk2_ref, init_ref, o_ref):
    init = init_ref[...]
    a = jnp.exp(jnp.sum(lq1_ref[...] * lk1_ref[...], axis=-1, keepdims=True))
    b = jnp.exp(jnp.sum(lq2_ref[...] * lk2_ref[...], axis=-1, keepdims=True))
    lam = a - b + init
    post = 1.0 - init
    for l in range(DEPTH):
        o_ref[l, 0:1, :] = jnp.broadcast_to(lam[l:l + 1], (1, LANES))
        o_ref[l, 1:2, :] = jnp.broadcast_to(post[l:l + 1], (1, LANES))


def _lam_consts(lq1, lk1, lq2, lk2):
    init = np.array([[0.8 - 0.6 * math.exp(-0.3 * l)] for l in range(DEPTH)], np.float32)
    return pl.pallas_call(
        _lam_kernel,
        out_shape=jax.ShapeDtypeStruct((DEPTH, 2, LANES), jnp.float32),
        name="diff_lambda",
    )(lq1, lk1, lq2, lk2, jnp.asarray(init))


N_DROW = 2 * NA_WIN_ROWS - 1
N_DCOL = 2 * NA_WIN_COLS - 1


def _bias_kernel(rpb_ref, o_ref):
    l = pl.program_id(0)
    h = pl.program_id(1)
    base = (l * NA_HEADS + h) * (N_DROW * N_DCOL)
    cq = lax.broadcasted_iota(jnp.int32, (GRID_W, LANES), 0)
    lane = lax.broadcasted_iota(jnp.int32, (GRID_W, LANES), 1)
    ck = lane & (GRID_W - 1)
    dcol = jnp.clip(ck - cq, -(NA_WIN_COLS - 1), NA_WIN_COLS - 1) + (NA_WIN_COLS - 1)
    hi = lane >= GRID_W
    for a in range(N_DROW - 1):
        acc = jnp.zeros((GRID_W, LANES), jnp.float32)
        for j in range(N_DCOL):
            lo_v = rpb_ref[base + a * N_DCOL + j]
            hi_v = rpb_ref[base + (a + 1) * N_DCOL + j]
            acc = jnp.where(dcol == j, jnp.where(hi, hi_v, lo_v), acc)
        o_ref[0, 0, a] = acc * LOG2E


def _bias_tiles(na_rpb):
    return pl.pallas_call(
        _bias_kernel,
        grid=(DEPTH, NA_HEADS),
        in_specs=[pl.BlockSpec(memory_space=pltpu.SMEM)],
        out_specs=pl.BlockSpec((1, 1, N_DROW - 1, GRID_W, LANES), lambda l, h: (l, h, 0, 0, 0)),
        out_shape=jax.ShapeDtypeStruct((DEPTH, NA_HEADS, N_DROW - 1, GRID_W, LANES), jnp.float32),
        compiler_params=_params("parallel", "parallel"),
        name="na_bias_tiles",
    )(na_rpb.reshape(-1))


def _row_group(i):
    return jnp.where(i < TILES_PROMPT, 0, 1 + (i - TILES_PROMPT) // (DEC_SEQ // TM))


def _split_specs(width, first):
    return [pl.BlockSpec((TM, width), lambda i: (jnp.minimum(first + i, TILES_PROMPT - 1), 0)),
            pl.BlockSpec((TM, width), lambda i: (jnp.maximum(first + i - TILES_PROMPT, 0), 0))]


def _x_specs(split, first):
    if not split:
        return [pl.BlockSpec((TM, D_MODEL), lambda i: (first + i, 0))]
    return _split_specs(D_MODEL, first)


def _read_tile(refs, first):
    if len(refs) == 1:
        return refs[0][...]
    return jnp.where(first + pl.program_id(0) < TILES_PROMPT, refs[0][...], refs[1][...])


IN_COLS = 2592
IN_QC, IN_CKV, IN_D = 1536, 1920, 2080
TR_ROWS = 256


def _gelu_tanh(x):
    return 0.5 * x * (1.0 + jnp.tanh(math.sqrt(2.0 / math.pi) * (x + 0.044715 * (x * x * x))))


def _sgu(pd, g, w_ref, bt):
    u = _gelu_tanh(pd[:, 0:WIDTH])
    v = _gelu_tanh(pd[:, WIDTH:2 * WIDTH])
    grp = lax.broadcasted_iota(jnp.int32, (1, WIDTH), 1) // SGU_GROUP_DIM
    v2 = v * v
    ms = jnp.zeros_like(v)
    for gi in range(SGU_GROUPS):
        sel = grp == gi
        tot = jnp.sum(jnp.where(sel, v2, 0.0), axis=-1, keepdims=True)
        ms = jnp.where(sel, tot * (1.0 / SGU_GROUP_DIM), ms)
    vg = _bf(v * lax.rsqrt(ms + EPS) * g)
    outs = []
    for c in range(pd.shape[0] // SGU_CHUNK):
        rows = slice(SGU_CHUNK * c, SGU_CHUNK * (c + 1))
        mixed = jnp.zeros((SGU_CHUNK, WIDTH), jnp.float32)
        for gi in range(SGU_GROUPS):
            full = _dot(_bf(w_ref[0, gi]), vg[rows]) + bt[:, gi:gi + 1]
            mixed = jnp.where(grp == gi, full, mixed)
        outs.append(u[rows] * mixed)
    return jnp.concatenate(outs, axis=0)


def _w_in_row_pieces():
    qn = [(IN_QC + 96 * h, MLA_NOPE) for h in range(MLA_HEADS)]
    qp = [(IN_QC + 96 * h + MLA_NOPE, MLA_ROPE) for h in range(MLA_HEADS)]
    seg_c = qn + qp + [(IN_CKV, MLA_KV_RANK + MLA_ROPE)]
    return (0, SEG_A + SEG_B), seg_c, (IN_D, SEG_D)


def _load_w_in(wt_ref, w_scr):
    ab, seg_c, d = _w_in_row_pieces()
    c_rows = jnp.concatenate([wt_ref[0, s:s + n, :] for s, n in seg_c]
                             + [jnp.zeros((SEG_C_PAD, D_MODEL), jnp.float32)], axis=0)
    for t in range(SEG_C // LANES):
        w_scr[:, SEG_A + SEG_B + LANES * t:SEG_A + SEG_B + LANES * (t + 1)] = _bf(c_rows[LANES * t:LANES * (t + 1)].T)
    for (src, n), dst in ((ab, 0), (d, SEG_A + SEG_B + SEG_C)):
        for t in range(n // TR_ROWS):
            rows = wt_ref[0, src + TR_ROWS * t:src + TR_ROWS * (t + 1), :]
            w_scr[:, dst + TR_ROWS * t:dst + TR_ROWS * (t + 1)] = _bf(rows.T)


def _inproj_kernel(n_x, *refs):
    x_refs = refs[:n_x]
    (g_ref, sh_ref, sc_ref, wt_ref, sg_ref, sw_ref, sbt_ref, pa_ref, pb_ref, pc_ref, od_ref, w_scr) = refs[n_x:]

    @pl.when(pl.program_id(0) == 0)
    def _():
        _load_w_in(wt_ref, w_scr)

    h = _rms(_read_tile(x_refs, 0), g_ref[0]) * (1.0 + sc_ref[...]) + sh_ref[...]
    hb = _bf(h)
    off = SEG_A + SEG_B + SEG_C
    od_ref[...] = _sgu(_dot(hb, w_scr[:, off:off + SEG_D]), sg_ref[0], sw_ref, sbt_ref[0])
    off = 0
    for ref in (pa_ref, pb_ref, pc_ref):
        n = ref.shape[1]
        ref[...] = _dot(hb, w_scr[:, off:off + n])
        off += n


def _inproj(l, xs, g_mix, mod, w_in_t, sgu_g, sgu_w, sgu_bt):
    def mod_spec(j):
        return pl.BlockSpec((None, None, 1, D_MODEL), lambda i: (l, _row_group(i), 0, j))

    widths = (SEG_A, SEG_B, SEG_C, WIDTH)
    return pl.pallas_call(
        functools.partial(_inproj_kernel, len(xs)),
        grid=(N_TOK // TM,),
        in_specs=_x_specs(len(xs) == 2, 0) + [
            pl.BlockSpec((1, 1, D_MODEL), lambda i: (l, 0, 0)),
            mod_spec(0), mod_spec(1),
            pl.BlockSpec((1, IN_COLS, D_MODEL), lambda i: (l, 0, 0), pipeline_mode=pl.Buffered(1)),
            pl.BlockSpec((1, 1, WIDTH), lambda i: (l, 0, 0)),
            pl.BlockSpec((1, SGU_GROUPS, SGU_CHUNK, SGU_CHUNK), lambda i: (l, 0, 0, 0)),
            pl.BlockSpec((1, SGU_CHUNK, SGU_GROUPS), lambda i: (l, 0, 0)),
        ],
        out_specs=[pl.BlockSpec((TM, n), lambda i: (i, 0)) for n in widths],
        out_shape=[jax.ShapeDtypeStruct((N_TOK, n), jnp.float32) for n in widths],
        scratch_shapes=[pltpu.VMEM((D_MODEL, IN_COLS_P), jnp.bfloat16)],
        compiler_params=_params("arbitrary"),
        name="inproj",
    )(*xs, g_mix, mod, mod, w_in_t, sgu_g, sgu_w, sgu_bt)


C_QN, C_QP, C_CKV, C_KPE = 0, 256, 384, 512


def _stack_heads(qp):
    lo = _lane_range(0, 64)
    return jnp.concatenate([_bf(jnp.where(lo, qp, 0.0)), _bf(jnp.where(lo, 0.0, qp))], axis=0)


def _unstack_heads(o, n):
    return jnp.where(_lane_range(0, 64), o[0:n], o[n:2 * n])


def _pair_t(c_ref):
    return jnp.concatenate([c_ref[0], c_ref[1]], axis=0)


def _stack_components(qp):
    return jnp.concatenate([_bf(jnp.where(_lane_range(32 * t, 32 * (t + 1)), qp, 0.0)) for t in range(4)], axis=0)


def _group_mean_sq(x, groups, size):
    sq = x * x
    hi = _bf(sq)
    rest = sq - hi.astype(jnp.float32)
    mid = _bf(rest)
    lo = _bf(rest - mid.astype(jnp.float32))
    return (_dot(hi, groups) + _dot(mid, groups) + _dot(lo, groups)) * (1.0 / size)


def _diff_finish(o, n, lam, post, g2, by_head):
    den = o[:, LANES:2 * LANES]
    outs = []
    for t in range(2):
        p1 = o[2 * t * n:(2 * t + 1) * n, 0:LANES] * (1.0 / den[2 * t * n:(2 * t + 1) * n])
        p2 = o[(2 * t + 1) * n:(2 * t + 2) * n, 0:LANES] * (lam / den[(2 * t + 1) * n:(2 * t + 2) * n])
        outs.append(p1 - p2)
    d = jnp.where(_lane_range(0, 64), outs[0], outs[1])
    return d * lax.rsqrt(_group_mean_sq(d, by_head, DIFF_V_DIM) + EPS) * g2 * post


def _mla_groups(j):
    return _group_matrix(2 * LANES, 2,
                         lambda t: (LANES + MLA_ROPE * (2 * j + t), LANES + MLA_ROPE * (2 * j + t + 1)))


def _mla_queries(qn_pair, qp_all, j):
    halves = []
    for t in range(2):
        h = 2 * j + t
        halves.append(jnp.concatenate([
            _bf(jnp.where(_lane_range(64 * t, 64 * (t + 1)), qn_pair, 0.0)),
            _bf(jnp.where(_lane_range(MLA_ROPE * h, MLA_ROPE * (h + 1)), qp_all, 0.0))], axis=1))
    return jnp.concatenate(halves, axis=0)


def _write_heads_t(p_ref, rows, col0, out_ref, bb):
    xt = p_ref[rows, col0:col0 + WIDTH].T
    for h in range(N_HEADS):
        out_ref[bb, 0, h] = xt[64 * h:64 * (h + 1)]
    _clear_other_layers(out_ref, bb)


def _clear_other_layers(out_ref, bb):
    if out_ref.shape[1] > 1:
        out_ref[bb, 1:] = jnp.zeros(out_ref.shape[1:], jnp.float32)[1:]


def _mix_prompt_kernel(n_prev, *refs):
    ins, outs = refs[:8], refs[8 + n_prev:]
    log = _ShiftLog()
    _mix_prompt_pass(ins, outs, log)

    @pl.when(log.unsafe())
    def _():
        _mix_prompt_pass(ins, outs, None)


def _mix_prompt_pass(ins, outs, log):
    pa_ref, pb_ref, pc_ref, cst_ref, gsub_ref, gckv_ref, wuk_ref, wuv_ref = ins
    o_ref, nak_ref, nav_ref, dk_ref, dv_ref, ckv_ref, kpe_ref = outs
    first_pass = log is not None
    c_a = HEAD_DIM ** -0.5 * LOG2E
    c_b = DIFF_QK_DIM ** -0.5 * LOG2E
    c_c = (MLA_NOPE + MLA_ROPE) ** -0.5 * LOG2E
    lam = cst_ref[0, 0:1, 0:1]
    post = cst_ref[0, 1:2, 0:1]
    wuk, wuv = _bf(wuk_ref[0]), _bf(wuv_ref[0])
    by_head, by_comp = _group_matrix(LANES, 2), _group_matrix(LANES, 4)

    def sequence(bb, log):
        rows = pl.ds(_aligned(bb * SEQ, SEQ), SEQ)
        for j in range(N_PAIRS):
            cols = slice(LANES * j, LANES * (j + 1))
            k = _bf(pa_ref[rows, WIDTH + LANES * j:WIDTH + LANES * (j + 1)])
            v = _with_ones(_bf(pa_ref[rows, 2 * WIDTH + LANES * j:2 * WIDTH + LANES * (j + 1)]))
            qs = _stack_heads(pa_ref[rows, cols] * c_a)
            o = _attend([_dot_nt(qs, k)], [v], log, _bound(log, qs, _key_bound([k], by_head), by_head, 2))
            o_ref[rows, cols] = _unstack_heads(_normalised(o), SEQ)
        if first_pass:
            _write_heads_t(pa_ref, rows, WIDTH, nak_ref, bb)
            _write_heads_t(pa_ref, rows, 2 * WIDTH, nav_ref, bb)
        for j in range(N_PAIRS):
            cols = slice(LANES * j, LANES * (j + 1))
            k = _bf(pb_ref[rows, WIDTH + LANES * j:WIDTH + LANES * (j + 1)])
            v = _with_ones(_bf(pb_ref[rows, 2 * WIDTH + LANES * j:2 * WIDTH + LANES * (j + 1)]))
            qs = _stack_components(pb_ref[rows, cols] * c_b)
            o = _attend([_dot_nt(qs, k)], [v], log, _bound(log, qs, _key_bound([k], by_comp), by_comp, 4))
            o_ref[rows, WIDTH + LANES * j:WIDTH + LANES * (j + 1)] = _diff_finish(o, SEQ, lam, post, gsub_ref[0], by_head)
        if first_pass:
            _write_heads_t(pb_ref, rows, WIDTH, dk_ref, bb)
            _write_heads_t(pb_ref, rows, 2 * WIDTH, dv_ref, bb)
        ckv = _rms(pc_ref[rows, C_CKV:C_CKV + MLA_KV_RANK], gckv_ref[0])
        kpe_slot = pc_ref[rows, C_KPE:C_KPE + LANES]
        if first_pass:
            ckv_ref[bb, 0] = ckv
            _clear_other_layers(ckv_ref, bb)
            kpe_ref[bb, 0] = kpe_slot.T[0:MLA_ROPE]
            _clear_other_layers(kpe_ref, bb)
        ckv_b = _bf(ckv)
        kn = _bf(_dot(ckv_b, wuk))
        vv = _bf(_dot(ckv_b, wuv))
        kpe4 = _bf(_tile4(kpe_slot))
        qn = pc_ref[rows, C_QN:C_QN + WIDTH] * c_c
        qp = pc_ref[rows, C_QP:C_QP + LANES] * c_c
        for j in range(N_PAIRS):
            cols = slice(LANES * j, LANES * (j + 1))
            k = jnp.concatenate([kn[:, cols], kpe4], axis=1)
            qs = _mla_queries(qn[:, cols], qp, j)
            groups = _mla_groups(j)
            o = _attend([_dot_nt(qs, k)], [_with_ones(vv[:, cols])], log,
                        _bound(log, qs, _key_bound([k], groups), groups, 2))
            o_ref[rows, 2 * WIDTH + LANES * j:2 * WIDTH + LANES * (j + 1)] = _unstack_heads(_normalised(o), SEQ)

    _loop(PB, log, sequence)


def _mix_prompt(l, pa, pb, pc, cst, g_sub2, g_ckv, w_uk, w_uv, prev):
    n_prev = len(prev)
    tails = [(NA_HEADS, HEAD_DIM, SEQ)] * 2 + [(DIFF_HEADS, 64, SEQ)] * 2 + [(SEQ, MLA_KV_RANK), (MLA_ROPE, SEQ)]

    def cache_spec(tail):
        if l == 0:
            return pl.BlockSpec((PB, DEPTH) + tail, lambda b: (b, 0) + (0,) * len(tail))
        return pl.BlockSpec((PB, 1) + tail, lambda b: (b, l) + (0,) * len(tail))

    def rows(width):
        return pl.BlockSpec((PB * SEQ, width), lambda b: (b, 0))

    def layer(*tail):
        return pl.BlockSpec((1,) + tail, lambda b: (l,) + (0,) * len(tail))

    return pl.pallas_call(
        functools.partial(_mix_prompt_kernel, n_prev),
        grid=(BATCH // PB,),
        in_specs=[rows(SEG_A), rows(SEG_B), rows(SEG_C), layer(2, LANES), layer(1, LANES), layer(1, MLA_KV_RANK),
                  layer(MLA_KV_RANK, WIDTH), layer(MLA_KV_RANK, WIDTH)] + [pl.BlockSpec(memory_space=pl.ANY)] * n_prev,
        out_specs=[rows(O_ATT)] + [cache_spec(t) for t in tails],
        out_shape=[jax.ShapeDtypeStruct((N_PROMPT, O_ATT), jnp.float32)]
        + [jax.ShapeDtypeStruct((BATCH, DEPTH) + t, jnp.float32) for t in tails],
        input_output_aliases={8 + i: 1 + i for i in range(n_prev)},
        compiler_params=_params("parallel"),
        name="mix_prompt",
    )(pa, pb, pc, cst, g_sub2, g_ckv, w_uk, w_uv, *prev)


def _na_row_groups():
    kh = min(NA_WIN_ROWS, GRID_ROWS)
    r0s = [min(max(r - kh // 2, 0), GRID_ROWS - kh) for r in range(GRID_ROWS)]
    groups = []
    for r, r0 in enumerate(r0s):
        if groups and groups[-1][2] == r0:
            groups[-1][1] = r
        else:
            groups.append([r, r, r0])
    return kh, [tuple(g) for g in groups]


def _na_sample(q_ref, k_ref, v_ref, ck_ref, cv_ref, tt_ref, o_ref, log):
    c = HEAD_DIM ** -0.5 * LOG2E
    kh, groups = _na_row_groups()
    lk = kh * GRID_W
    edge = [g for g in groups if g[1] > g[0]]
    inner = [g for g in groups if g[1] == g[0]]
    depth = inner[0][0] - inner[0][2]
    assert all(g[0] - g[2] == depth for g in inner) and [g[0] for g in inner] == list(range(inner[0][0], inner[-1][0] + 1))

    def in_window(n):
        cq = lax.broadcasted_iota(jnp.int32, (n, lk), 0) & (GRID_W - 1)
        ck = lax.broadcasted_iota(jnp.int32, (n, lk), 1) & (GRID_W - 1)
        c0 = jnp.clip(cq - NA_WIN_COLS // 2, 0, GRID_W - NA_WIN_COLS)
        return (ck >= c0) & (ck < c0 + NA_WIN_COLS)

    kc_t = _bf(_pair_t(ck_ref))
    vc = _with_ones(_bf(_pair_t(cv_ref).T))
    by_head = _group_matrix(LANES, 2)
    k2max = _key_bound([_bf(k_ref[...]), _bf(_pair_t(ck_ref).T)], by_head)
    tmax = functools.reduce(jnp.maximum, [tt_ref[t, a] for t in range(2) for a in range(N_DROW - 1)])
    bplus = jnp.maximum(jnp.max(jnp.max(tmax, axis=-1, keepdims=True), axis=0, keepdims=True), 0.0)

    def group(row0, key0, offsets, log):
        n = len(offsets) * GRID_W
        rows, keys = pl.ds(row0, n), pl.ds(key0, lk)
        qg = _stack_heads(q_ref[rows, :] * c)
        k = _bf(k_ref[keys, :])
        v = _with_ones(_bf(v_ref[keys, :]))
        bias = jnp.concatenate([
            jnp.concatenate([tt_ref[t, 2 * i - off + NA_WIN_ROWS - 1] for i in range(kh // 2)], axis=1)
            for t in range(2) for off in offsets], axis=0)
        s_loc = jnp.where(in_window(2 * n), _dot_nt(qg, k) + bias, NEG_INF)
        o = _attend([_dot(qg, kc_t), s_loc], [vc, v], log, _bound(log, qg, k2max, by_head, 2, bplus))
        o_ref[rows, :] = _unstack_heads(_normalised(o), n)

    for (r_lo, r_hi, r0) in edge:
        group(r_lo * GRID_W, r0 * GRID_W, [r - r0 for r in range(r_lo, r_hi + 1)], log)

    def inner_row(i, log):
        r = inner[0][0] + i
        group(_aligned(r * GRID_W, GRID_W), _aligned((r - depth) * GRID_W, GRID_W), [depth], log)

    _loop(len(inner), log, inner_row)


def _diff_sample(q_ref, k_ref, v_ref, ck_ref, cv_ref, cos_ref, sin_ref, cst_ref, g_ref, o_ref, log):
    c = DIFF_QK_DIM ** -0.5 * LOG2E
    lam = cst_ref[0, 0:1, 0:1]
    post = cst_ref[0, 1:2, 0:1]
    k_new = _bf(_rope(k_ref[...], cos_ref[...], sin_ref[...]))
    kc_t = _bf(_pair_t(ck_ref))
    vc = _with_ones(_bf(_pair_t(cv_ref).T))
    v = _with_ones(_bf(v_ref[...]))
    by_head, by_comp = _group_matrix(LANES, 2), _group_matrix(LANES, 4)
    k2max = _key_bound([k_new, _bf(_pair_t(ck_ref).T)], by_comp)

    def block(qi, log):
        rows = pl.ds(_aligned(qi * QB, QB), QB)
        qs = _stack_components(_rope(q_ref[rows, :], cos_ref[rows, :], sin_ref[rows, :]) * c)
        o = _attend([_dot(qs, kc_t), _dot_nt(qs, k_new)], [vc, v], log, _bound(log, qs, k2max, by_comp, 4))
        o_ref[rows, :] = _diff_finish(o, QB, lam, post, g_ref[0], by_head)

    _loop(DEC_SEQ // QB, log, block)


def _mla_sample(j, qn_ref, qp_ref, ckv_ref, kpe_ref, cckv_ref, ckpe_ref, cosq_ref, sinq_ref, cosk_ref, sink_ref,
                gckv_ref, wuk_ref, wuv_ref, o_ref, log):
    c = (MLA_NOPE + MLA_ROPE) ** -0.5 * LOG2E
    wuk, wuv = _bf(wuk_ref[0]), _bf(wuv_ref[0])
    ckv_new = _bf(_rms(ckv_ref[...], gckv_ref[0]))
    ckv_old = _bf(cckv_ref[...])
    kpe_new = _bf(_tile4(_rope(kpe_ref[...], cosk_ref[...], sink_ref[...])))
    kpe_old = _bf(jnp.concatenate([ckpe_ref[...]] * MLA_HEADS, axis=0).T)
    k_old = jnp.concatenate([_bf(_dot(ckv_old, wuk)), kpe_old], axis=1)
    k_new = jnp.concatenate([_bf(_dot(ckv_new, wuk)), kpe_new], axis=1)
    vo, vn = _with_ones(_bf(_dot(ckv_old, wuv))), _with_ones(_bf(_dot(ckv_new, wuv)))
    groups = _mla_groups(j)
    k2max = _key_bound([k_old, k_new], groups)

    def block(qi, log):
        rows = pl.ds(_aligned(qi * QB, QB), QB)
        qp = _rope(qp_ref[rows, :], cosq_ref[rows, :], sinq_ref[rows, :]) * c
        qs = _mla_queries(qn_ref[rows, :] * c, qp, j)
        o = _attend([_dot_nt(qs, k_old), _dot_nt(qs, k_new)], [vo, vn], log, _bound(log, qs, k2max, groups, 2))
        o_ref[rows, :] = _unstack_heads(_normalised(o), QB)

    _loop(DEC_SEQ // QB, log, block)


def _mix_sample_kernel(qa_ref, ka_ref, va_ref, qb_ref, kb_ref, vb_ref, qn_ref, qp_ref, ckv_ref, kpe_ref,
                       cnak_ref, cnav_ref, cdk_ref, cdv_ref, cckv_ref, ckpe_ref, tt_ref,
                       cosb_ref, sinb_ref, cosq_ref, sinq_ref, cosk_ref, sink_ref,
                       cst_ref, gsub_ref, gckv_ref, wuk_ref, wuv_ref, oa_ref, ob_ref, oc_ref):
    j = pl.program_id(1)

    def run(log):
        _na_sample(qa_ref, ka_ref, va_ref, cnak_ref, cnav_ref, tt_ref, oa_ref, log)
        _diff_sample(qb_ref, kb_ref, vb_ref, cdk_ref, cdv_ref, cosb_ref, sinb_ref, cst_ref, gsub_ref, ob_ref, log)
        _mla_sample(j, qn_ref, qp_ref, ckv_ref, kpe_ref, cckv_ref, ckpe_ref, cosq_ref, sinq_ref, cosk_ref, sink_ref,
                    gckv_ref, wuk_ref, wuv_ref, oc_ref, log)

    log = _ShiftLog()
    run(log)

    @pl.when(log.unsafe())
    def _():
        run(None)


def _mix_sample(l, pa, pb, pc, caches_t, tt, tables, cst, g_sub2, g_ckv, w_uk, w_uv):
    first = N_PROMPT // DEC_SEQ

    def cols(block):
        return pl.BlockSpec((DEC_SEQ, LANES), lambda b, j: (first + b, block(j)))

    def cache(*tail, pair=False):
        return pl.BlockSpec((None, None) + tail, lambda b, j: (b, l, j if pair else 0) + (0,) * (len(tail) - 1))

    def layer(*tail):
        return pl.BlockSpec((1,) + tail, lambda b, j: (l,) + (0,) * len(tail))

    table = pl.BlockSpec((DEC_SEQ, LANES), lambda b, j: (0, 0), pipeline_mode=pl.Buffered(1))
    qkv = [cols(lambda j: j), cols(lambda j: N_PAIRS + j), cols(lambda j: 2 * N_PAIRS + j)]
    seg_c = [cols(lambda j: j), cols(lambda j: C_QP // LANES), cols(lambda j: C_CKV // LANES), cols(lambda j: C_KPE // LANES)]
    kv_t = cache(2, 64, PAST_LEN, pair=True)
    w_pair = pl.BlockSpec((1, MLA_KV_RANK, LANES), lambda b, j: (l, 0, j))
    out = pl.BlockSpec((DEC_SEQ, LANES), lambda b, j: (b, j))
    return pl.pallas_call(
        _mix_sample_kernel,
        grid=(DEC_BATCH, N_PAIRS),
        in_specs=qkv + qkv + seg_c + [kv_t, kv_t, kv_t, kv_t, cache(PAST_LEN, MLA_KV_RANK), cache(MLA_ROPE, PAST_LEN),
                                      pl.BlockSpec((None, 2, N_DROW - 1, GRID_W, LANES), lambda b, j: (l, j, 0, 0, 0)),
                                      table, table, table, table, table, table,
                                      layer(2, LANES), layer(1, LANES), layer(1, MLA_KV_RANK), w_pair, w_pair],
        out_specs=[out, out, out],
        out_shape=[jax.ShapeDtypeStruct((N_SAMPLE, WIDTH), jnp.float32)] * 3,
        compiler_params=_params("parallel", "arbitrary"),
        name="mix_sample",
    )(pa, pa, pa, pb, pb, pb, pc, pc, pc, pc, *caches_t, tt, *tables, cst, g_sub2, g_ckv, w_uk, w_uv)


FF_CHUNK = 1024


def _outffn_kernel(n_x, first, final, *refs):
    x_refs, op_ref, os_refs = refs[:n_x], refs[n_x], refs[n_x + 1:n_x + 4]
    (od_ref, wout_ref, g1_ref, gffn_ref, sh2_ref, sc2_ref, g2_ref, w1_ref, w2_ref, gfin_ref, y_ref) = refs[n_x + 4:]
    o_att = jnp.where(first + pl.program_id(0) < TILES_PROMPT, op_ref[...],
                      jnp.concatenate([r[...] for r in os_refs], axis=1))
    acc = (_dot(_bf(o_att), wout_ref[0, 0:O_ATT, :])
           + _dot(_bf(od_ref[...]), wout_ref[0, O_ATT:O_ATT + WIDTH, :]))
    x1 = _read_tile(x_refs, first) + g1_ref[...] * acc
    hf = _bf(_rms(x1, gffn_ref[0]) * (1.0 + sc2_ref[...]) + sh2_ref[...])
    acc = jnp.zeros((TM, D_MODEL), jnp.float32)
    for c in range(D_FF // FF_CHUNK):
        cols = slice(FF_CHUNK * c, FF_CHUNK * (c + 1))
        a = jnp.square(jnp.maximum(_dot(hf, w1_ref[0, :, cols]), 0.0))
        acc += _dot(_bf(a), w2_ref[0, cols, :])
    y = x1 + g2_ref[...] * acc
    if final:
        y = _rms(y, gfin_ref[...])
    y_ref[...] = y


def _outffn(l, xs, o_p, o_s, od, w_out, g_ffn, mod, w1, w2, g_final, first, n_tiles):
    def mod_spec(j):
        return pl.BlockSpec((None, None, 1, D_MODEL), lambda i: (l, _row_group(first + i), 0, j))

    def resident(shape):
        return pl.BlockSpec(shape, lambda i: (l,) + (0,) * (len(shape) - 1), pipeline_mode=pl.Buffered(1))

    return pl.pallas_call(
        functools.partial(_outffn_kernel, len(xs), first, l == DEPTH - 1),
        grid=(n_tiles,),
        in_specs=_x_specs(len(xs) == 2, first) + _split_specs(O_ATT, first)[:1] + _split_specs(WIDTH, first)[1:] * 3 + [
            pl.BlockSpec((TM, WIDTH), lambda i: (first + i, 0)),
            resident((1, 4 * WIDTH, D_MODEL)),
            mod_spec(2),
            pl.BlockSpec((1, 1, D_MODEL), lambda i: (l, 0, 0)),
            mod_spec(3), mod_spec(4), mod_spec(5),
            resident((1, D_MODEL, D_FF)),
            resident((1, D_FF, D_MODEL)),
            pl.BlockSpec((1, D_MODEL), lambda i: (0, 0)),
        ],
        out_specs=pl.BlockSpec((TM, D_MODEL), lambda i: (i, 0)),
        out_shape=jax.ShapeDtypeStruct((n_tiles * TM, D_MODEL), jnp.float32),
        compiler_params=_params("parallel"),
        name="outffn",
    )(*xs, o_p, *o_s, od, w_out, mod, g_ffn, mod, mod, mod, w1, w2, g_final)


def _rope32_tables():
    t = np.arange(DEC_SEQ)
    rows, cols = (t // GRID_W).astype(np.float64), (t % GRID_W).astype(np.float64)
    half = 8
    freqs = ROPE_BASE ** (-np.arange(half, dtype=np.float64) / half)
    cos, sin = [], []
    for pos in (rows, cols):
        ang = pos[:, None] * freqs[None, :]
        cos += [np.cos(ang), np.cos(ang)]
        sin += [-np.sin(ang), np.sin(ang)]
    return np.concatenate(cos, axis=1).astype(np.float32), np.concatenate(sin, axis=1).astype(np.float32)


def _rope_tables():
    c32, s32 = _rope32_tables()
    tile = lambda a, n: np.tile(a, (1, n))
    pad = np.zeros((DEC_SEQ, LANES - MLA_ROPE), np.float32)
    cos_k = np.concatenate([c32, pad + 1.0], axis=1)
    sin_k = np.concatenate([s32, pad], axis=1)
    return (tile(c32, 4), tile(s32, 4),
            cos_k, sin_k)


def kernel(x_prompt, x_sample, cache_na_k, cache_na_v, cache_diff_k, cache_diff_v, cache_mla_ckv, cache_mla_kpe, c, c_ctx, w_ada, b_ada, g_mix, g_ffn, w_in, w_out, na_rpb, diff_lq1, diff_lk1, diff_lq2, diff_lk2, diff_g_subln, mla_g_ckv, mla_w_uk, mla_w_uv, sgu_g, sgu_w, sgu_b, w_ff1, w_ff2, g_final):
    f32 = jnp.float32
    m = jnp.concatenate([c_ctx[None, :], c, jnp.zeros((N_MOD_ROWS - 1 - DEC_BATCH, D_MODEL), f32)], axis=0)
    mod = _ada(m, w_ada, b_ada).reshape(DEPTH, N_MOD_ROWS, 1, 6 * D_MODEL)
    cst = _lam_consts(diff_lq1, diff_lk1, diff_lq2, diff_lk2)
    tt = _bias_tiles(na_rpb)
    cos4, sin4, cos_k, sin_k = [jnp.asarray(t) for t in _rope_tables()]
    tables = (cos4, sin4, cos4, sin4, cos_k, sin_k)

    t_last = lambda a: jnp.swapaxes(a, -1, -2)
    w_in_t = t_last(w_in)
    caches_t = (t_last(cache_na_k), t_last(cache_na_v), t_last(cache_diff_k), t_last(cache_diff_v),
                cache_mla_ckv, t_last(cache_mla_kpe))
    w_out_b, w1_b, w2_b = _bf(w_out), _bf(w_ff1), _bf(w_ff2)
    g_mix3 = g_mix.reshape(DEPTH, 1, D_MODEL)
    g_ffn3 = g_ffn.reshape(DEPTH, 1, D_MODEL)
    g_sub2 = jnp.tile(diff_g_subln, (1, 2)).reshape(DEPTH, 1, LANES)
    g_ckv3 = mla_g_ckv.reshape(DEPTH, 1, MLA_KV_RANK)
    sgu_g3 = sgu_g.reshape(DEPTH, 1, WIDTH)
    sgu_bt = sgu_b.transpose(0, 2, 1)
    g_fin2 = g_final.reshape(1, D_MODEL)

    xs = (x_prompt.reshape(N_PROMPT, D_MODEL), x_sample.reshape(N_SAMPLE, D_MODEL))
    new = ()
    for l in range(DEPTH):
        pa, pb, pc, od = _inproj(l, xs, g_mix3, mod, w_in_t, sgu_g3, sgu_w, sgu_bt)
        o_p, *new = _mix_prompt(l, pa, pb, pc, cst, g_sub2, g_ckv3, mla_w_uk, mla_w_uv, new)
        o_s = _mix_sample(l, pa, pb, pc, caches_t, tt, tables, cst, g_sub2, g_ckv3, mla_w_uk, mla_w_uv)
        ffn = functools.partial(_outffn, l, xs, o_p, o_s, od, w_out_b, g_ffn3, mod, w1_b, w2_b, g_fin2)
        if l < DEPTH - 1:
            xs = (ffn(0, TILES_PROMPT + TILES_SAMPLE),)
        else:
            xs = (ffn(0, TILES_PROMPT), ffn(TILES_PROMPT, TILES_SAMPLE))
    y_prompt = xs[0].reshape(BATCH, SEQ, D_MODEL)
    y_sample = xs[1].reshape(DEC_BATCH, DEC_SEQ, D_MODEL)
    na_k, na_v, diff_k, diff_v, mla_ckv, mla_kpe = new
    return (y_prompt, y_sample, t_last(na_k), t_last(na_v), t_last(diff_k), t_last(diff_v), mla_ckv, t_last(mla_kpe))
```

```python
import functools
import math

import numpy as np
import jax
import jax.numpy as jnp
from jax import lax
from jax.experimental import pallas as pl
from jax.experimental.pallas import tpu as pltpu

D_MODEL = 1024
BATCH = 16
SEQ = 256
DEPTH = 4
DEC_BATCH = 2
DEC_SEQ = 1024
PAST_LEN = 512
GRID_W = 64
GRID_ROWS = DEC_SEQ // GRID_W
HEAD_DIM = 64
NA_HEADS = 4
NA_WIN_ROWS = 8
NA_WIN_COLS = 16
DIFF_HEADS = 4
DIFF_QK_DIM = 32
DIFF_V_DIM = 64
MLA_HEADS = 4
MLA_NOPE = 64
MLA_ROPE = 32
MLA_V = 64
MLA_KV_RANK = 128
SGU_GROUPS = 4
SGU_GROUP_DIM = 64
SGU_CHUNK = 128
D_FF = 4 * D_MODEL
ROPE_BASE = 10000.0
EPS = 1e-6
NEG_INF = -1e30
LOG2E = 1.4426950408889634

N_HEADS = 4
N_PAIRS = N_HEADS // 2
LANES = 128
WIDTH = 256
N_PROMPT = BATCH * SEQ
N_SAMPLE = DEC_BATCH * DEC_SEQ
N_TOK = N_PROMPT + N_SAMPLE
N_MOD_ROWS = 8

SEG_A = 3 * WIDTH
SEG_B = 3 * WIDTH
SEG_C = 640
SEG_D = 2 * WIDTH
SEG_C_PAD = 96
IN_COLS_P = SEG_A + SEG_B + SEG_C + SEG_D
O_ATT = 3 * WIDTH

TM = 512
TILES_PROMPT = N_PROMPT // TM
TILES_SAMPLE = N_SAMPLE // TM
PB = 2
QB = 256
VMEM_LIMIT = 56 * 1024 * 1024


def _bf(x):
    return x.astype(jnp.bfloat16)


def _dot(a, b):
    return jnp.dot(a, b, preferred_element_type=jnp.float32)


def _dot_nt(a, b):
    return lax.dot_general(a, b, (((1,), (1,)), ((), ())), preferred_element_type=jnp.float32)


def _rms(x, g):
    ms = jnp.mean(x * x, axis=-1, keepdims=True)
    return x * lax.rsqrt(ms + EPS) * g


def _lane_range(lo, hi, width=LANES):
    lane = lax.broadcasted_iota(jnp.int32, (1, width), 1)
    return (lane >= lo) & (lane < hi)


def _with_ones(v):
    return jnp.concatenate([v, jnp.ones((v.shape[0], LANES), jnp.bfloat16)], axis=1)


def _attend(scores, values, log=None, bound=None):
    if bound is None:
        m = functools.reduce(jnp.maximum, [jnp.max(s, axis=-1, keepdims=True) for s in scores])
        shifted = [s - m for s in scores]
    else:
        over = jnp.max(bound - scores[0][:, 0:LANES], axis=0, keepdims=True)
        log.worst = jnp.maximum(log.worst, jnp.min(over, axis=1, keepdims=True))
        shifted = [s - bound for s in scores]
    return functools.reduce(lambda a, b: a + b, [_dot(_bf(jnp.exp2(s)), v) for s, v in zip(shifted, values)])


BOUND_SLACK = 1.02
OVERSHOOT_LIMIT = 100.0


class _ShiftLog:
    def __init__(self, worst=None):
        self.worst = jnp.zeros((1, 1), jnp.float32) if worst is None else worst

    def unsafe(self):
        return jnp.logical_not(self.worst[0, 0] < OVERSHOOT_LIMIT)


def _group_matrix(width, n_groups, extra=None):
    i = lax.broadcasted_iota(jnp.int32, (width, LANES), 0)
    j = lax.broadcasted_iota(jnp.int32, (width, LANES), 1)
    size = LANES // n_groups
    hit = (i // size == j // size) & (i < LANES)
    for t in range(n_groups if extra else 0):
        lo, hi = extra(t)
        hit = hit | ((i >= lo) & (i < hi) & (j // size == t))
    return jnp.where(hit, 1.0, 0.0).astype(jnp.bfloat16)


def _squares(x):
    xf = x.astype(jnp.float32)
    return _bf(xf * xf)


def _key_bound(keys, groups):
    return functools.reduce(jnp.maximum, [jnp.max(_dot(_squares(k), groups), axis=0, keepdims=True) for k in keys])


def _bound(log, qs, k2max, groups, n_groups, extra=0.0):
    if log is None:
        return None
    n, size = qs.shape[0] // n_groups, LANES // n_groups
    kmax = jnp.concatenate([jnp.broadcast_to(jnp.sqrt(k2max[:, size * t:size * t + 1]) * BOUND_SLACK, (n, 1))
                            for t in range(n_groups)], axis=0)
    qf = qs.astype(jnp.float32)
    return jnp.sqrt(jnp.sum(qf * qf, axis=-1, keepdims=True)) * kmax + extra


def _loop(n, log, body):
    if log is not None:
        for i in range(n):
            body(i, log)
    else:
        lax.fori_loop(0, n, lambda i, carry: body(i, None) or carry, 0)


def _aligned(start, multiple):
    return start if isinstance(start, int) else pl.multiple_of(start, multiple)


def _normalised(o_ext):
    return o_ext[:, 0:LANES] * (1.0 / o_ext[:, LANES:2 * LANES])


def _swap8(x):
    lane = lax.broadcasted_iota(jnp.int32, (1, LANES), 1)
    return jnp.where((lane & 15) < 8, pltpu.roll(x, LANES - 8, 1), pltpu.roll(x, 8, 1))


def _rope(x, cos, sin):
    outs = []
    for c in range(x.shape[1] // LANES):
        sl = slice(LANES * c, LANES * (c + 1))
        xc = x[:, sl]
        outs.append(xc * cos[:, sl] + _swap8(xc) * sin[:, sl])
    return outs[0] if len(outs) == 1 else jnp.concatenate(outs, axis=1)


def _tile4(x):
    return x + pltpu.roll(x, 32, 1) + pltpu.roll(x, 64, 1) + pltpu.roll(x, 96, 1)


def _params(*sem):
    return pltpu.CompilerParams(dimension_semantics=sem, vmem_limit_bytes=VMEM_LIMIT)


ADA_TN = 1536


def _ada_kernel(m_ref, w_ref, b_ref, o_ref):
    m = m_ref[...]
    s = m * jax.nn.sigmoid(m)
    o_ref[0] = _dot(_bf(s), _bf(w_ref[0])) + b_ref[0]


def _ada(m, w_ada, b_ada):
    n = 6 * D_MODEL
    return pl.pallas_call(
        _ada_kernel,
        grid=(DEPTH, n // ADA_TN),
        in_specs=[
            pl.BlockSpec((N_MOD_ROWS, D_MODEL), lambda l, j: (0, 0)),
            pl.BlockSpec((1, D_MODEL, ADA_TN), lambda l, j: (l, 0, j)),
            pl.BlockSpec((1, 1, ADA_TN), lambda l, j: (l, 0, j)),
        ],
        out_specs=pl.BlockSpec((1, N_MOD_ROWS, ADA_TN), lambda l, j: (l, 0, j)),
        out_shape=jax.ShapeDtypeStruct((DEPTH, N_MOD_ROWS, n), jnp.float32),
        compiler_params=_params("parallel", "parallel"),
        name="ada",
    )(m, w_ada, b_ada.reshape(DEPTH, 1, n))


def _lam_kernel(lq1_ref, lk1_ref, lq2_ref, lk2_ref, init_ref, o_ref):
    init = init_ref[...]
    a = jnp.exp(jnp.sum(lq1_ref[...] * lk1_ref[...], axis=-1, keepdims=True))
    b = jnp.exp(jnp.sum(lq2_ref[...] * lk2_ref[...], axis=-1, keepdims=True))
    lam = a - b + init
    post = 1.0 - init
    for l in range(DEPTH):
        o_ref[l, 0:1, :] = jnp.broadcast_to(lam[l:l + 1], (1, LANES))
        o_ref[l, 1:2, :] = jnp.broadcast_to(post[l:l + 1], (1, LANES))


def _lam_consts(lq1, lk1, lq2, lk2):
    init = np.array([[0.8 - 0.6 * math.exp(-0.3 * l)] for l in range(DEPTH)], np.float32)
    return pl.pallas_call(
        _lam_kernel,
        out_shape=jax.ShapeDtypeStruct((DEPTH, 2, LANES), jnp.float32),
        name="diff_lambda",
    )(lq1, lk1, lq2, lk2, jnp.asarray(init))


N_DROW = 2 * NA_WIN_ROWS - 1
N_DCOL = 2 * NA_WIN_COLS - 1


def _bias_kernel(rpb_ref, o_ref):
    l = pl.program_id(0)
    h = pl.program_id(1)
    base = (l * NA_HEADS + h) * (N_DROW * N_DCOL)
    cq = lax.broadcasted_iota(jnp.int32, (GRID_W, LANES), 0)
    lane = lax.broadcasted_iota(jnp.int32, (GRID_W, LANES), 1)
    ck = lane & (GRID_W - 1)
    dcol = jnp.clip(ck - cq, -(NA_WIN_COLS - 1), NA_WIN_COLS - 1) + (NA_WIN_COLS - 1)
    hi = lane >= GRID_W
    for a in range(N_DROW - 1):
        acc = jnp.zeros((GRID_W, LANES), jnp.float32)
        for j in range(N_DCOL):
            lo_v = rpb_ref[base + a * N_DCOL + j]
            hi_v = rpb_ref[base + (a + 1) * N_DCOL + j]
            acc = jnp.where(dcol == j, jnp.where(hi, hi_v, lo_v), acc)
        o_ref[0, 0, a] = acc * LOG2E


def _bias_tiles(na_rpb):
    return pl.pallas_call(
        _bias_kernel,
        grid=(DEPTH, NA_HEADS),
        in_specs=[pl.BlockSpec(memory_space=pltpu.SMEM)],
        out_specs=pl.BlockSpec((1, 1, N_DROW - 1, GRID_W, LANES), lambda l, h: (l, h, 0, 0, 0)),
        out_shape=jax.ShapeDtypeStruct((DEPTH, NA_HEADS, N_DROW - 1, GRID_W, LANES), jnp.float32),
        compiler_params=_params("parallel", "parallel"),
        name="na_bias_tiles",
    )(na_rpb.reshape(-1))


def _row_group(i):
    return jnp.where(i < TILES_PROMPT, 0, 1 + (i - TILES_PROMPT) // (DEC_SEQ // TM))


def _split_specs(width, first):
    return [pl.BlockSpec((TM, width), lambda i: (jnp.minimum(first + i, TILES_PROMPT - 1), 0)),
            pl.BlockSpec((TM, width), lambda i: (jnp.maximum(first + i - TILES_PROMPT, 0), 0))]


def _x_specs(split, first):
    if not split:
        return [pl.BlockSpec((TM, D_MODEL), lambda i: (first + i, 0))]
    return _split_specs(D_MODEL, first)


def _read_tile(refs, first):
    if len(refs) == 1:
        return refs[0][...]
    return jnp.where(first + pl.program_id(0) < TILES_PROMPT, refs[0][...], refs[1][...])


IN_COLS = 2592
IN_QC, IN_CKV, IN_D = 1536, 1920, 2080
TR_ROWS = 256


def _gelu_tanh(x):
    return 0.5 * x * (1.0 + jnp.tanh(math.sqrt(2.0 / math.pi) * (x + 0.044715 * (x * x * x))))


def _sgu(pd, g, w_ref, bt):
    u = _gelu_tanh(pd[:, 0:WIDTH])
    v = _gelu_tanh(pd[:, WIDTH:2 * WIDTH])
    grp = lax.broadcasted_iota(jnp.int32, (1, WIDTH), 1) // SGU_GROUP_DIM
    v2 = v * v
    ms = jnp.zeros_like(v)
    for gi in range(SGU_GROUPS):
        sel = grp == gi
        tot = jnp.sum(jnp.where(sel, v2, 0.0), axis=-1, keepdims=True)
        ms = jnp.where(sel, tot * (1.0 / SGU_GROUP_DIM), ms)
    vg = _bf(v * lax.rsqrt(ms + EPS) * g)
    outs = []
    for c in range(pd.shape[0] // SGU_CHUNK):
        rows = slice(SGU_CHUNK * c, SGU_CHUNK * (c + 1))
        mixed = jnp.zeros((SGU_CHUNK, WIDTH), jnp.float32)
        for gi in range(SGU_GROUPS):
            full = _dot(_bf(w_ref[0, gi]), vg[rows]) + bt[:, gi:gi + 1]
            mixed = jnp.where(grp == gi, full, mixed)
        outs.append(u[rows] * mixed)
    return jnp.concatenate(outs, axis=0)


def _w_in_row_pieces():
    qn = [(IN_QC + 96 * h, MLA_NOPE) for h in range(MLA_HEADS)]
    qp = [(IN_QC + 96 * h + MLA_NOPE, MLA_ROPE) for h in range(MLA_HEADS)]
    seg_c = qn + qp + [(IN_CKV, MLA_KV_RANK + MLA_ROPE)]
    return (0, SEG_A + SEG_B), seg_c, (IN_D, SEG_D)


def _load_w_in(wt_ref, w_scr):
    ab, seg_c, d = _w_in_row_pieces()
    c_rows = jnp.concatenate([wt_ref[0, s:s + n, :] for s, n in seg_c]
                             + [jnp.zeros((SEG_C_PAD, D_MODEL), jnp.float32)], axis=0)
    for t in range(SEG_C // LANES):
        w_scr[:, SEG_A + SEG_B + LANES * t:SEG_A + SEG_B + LANES * (t + 1)] = _bf(c_rows[LANES * t:LANES * (t + 1)].T)
    for (src, n), dst in ((ab, 0), (d, SEG_A + SEG_B + SEG_C)):
        for t in range(n // TR_ROWS):
            rows = wt_ref[0, src + TR_ROWS * t:src + TR_ROWS * (t + 1), :]
            w_scr[:, dst + TR_ROWS * t:dst + TR_ROWS * (t + 1)] = _bf(rows.T)


def _inproj_kernel(n_x, *refs):
    x_refs = refs[:n_x]
    (g_ref, sh_ref, sc_ref, wt_ref, sg_ref, sw_ref, sbt_ref, pa_ref, pb_ref, pc_ref, od_ref, w_scr) = refs[n_x:]

    @pl.when(pl.program_id(0) == 0)
    def _():
        _load_w_in(wt_ref, w_scr)

    h = _rms(_read_tile(x_refs, 0), g_ref[0]) * (1.0 + sc_ref[...]) + sh_ref[...]
    hb = _bf(h)
    off = SEG_A + SEG_B + SEG_C
    od_ref[...] = _sgu(_dot(hb, w_scr[:, off:off + SEG_D]), sg_ref[0], sw_ref, sbt_ref[0])
    off = 0
    for ref in (pa_ref, pb_ref, pc_ref):
        n = ref.shape[1]
        ref[...] = _dot(hb, w_scr[:, off:off + n])
        off += n


def _inproj(l, xs, g_mix, mod, w_in_t, sgu_g, sgu_w, sgu_bt):
    def mod_spec(j):
        return pl.BlockSpec((None, None, 1, D_MODEL), lambda i: (l, _row_group(i), 0, j))

    widths = (SEG_A, SEG_B, SEG_C, WIDTH)
    return pl.pallas_call(
        functools.partial(_inproj_kernel, len(xs)),
        grid=(N_TOK // TM,),
        in_specs=_x_specs(len(xs) == 2, 0) + [
            pl.BlockSpec((1, 1, D_MODEL), lambda i: (l, 0, 0)),
            mod_spec(0), mod_spec(1),
            pl.BlockSpec((1, IN_COLS, D_MODEL), lambda i: (l, 0, 0), pipeline_mode=pl.Buffered(1)),
            pl.BlockSpec((1, 1, WIDTH), lambda i: (l, 0, 0)),
            pl.BlockSpec((1, SGU_GROUPS, SGU_CHUNK, SGU_CHUNK), lambda i: (l, 0, 0, 0)),
            pl.BlockSpec((1, SGU_CHUNK, SGU_GROUPS), lambda i: (l, 0, 0)),
        ],
        out_specs=[pl.BlockSpec((TM, n), lambda i: (i, 0)) for n in widths],
        out_shape=[jax.ShapeDtypeStruct((N_TOK, n), jnp.float32) for n in widths],
        scratch_shapes=[pltpu.VMEM((D_MODEL, IN_COLS_P), jnp.bfloat16)],
        compiler_params=_params("arbitrary"),
        name="inproj",
    )(*xs, g_mix, mod, mod, w_in_t, sgu_g, sgu_w, sgu_bt)


C_QN, C_QP, C_CKV, C_KPE = 0, 256, 384, 512


def _stack_heads(qp):
    lo = _lane_range(0, 64)
    return jnp.concatenate([_bf(jnp.where(lo, qp, 0.0)), _bf(jnp.where(lo, 0.0, qp))], axis=0)


def _unstack_heads(o, n):
    return jnp.where(_lane_range(0, 64), o[0:n], o[n:2 * n])


def _pair_t(c_ref):
    return jnp.concatenate([c_ref[0], c_ref[1]], axis=0)


def _stack_components(qp):
    return jnp.concatenate([_bf(jnp.where(_lane_range(32 * t, 32 * (t + 1)), qp, 0.0)) for t in range(4)], axis=0)


def _group_mean_sq(x, groups, size):
    sq = x * x
    hi = _bf(sq)
    rest = sq - hi.astype(jnp.float32)
    mid = _bf(rest)
    lo = _bf(rest - mid.astype(jnp.float32))
    return (_dot(hi, groups) + _dot(mid, groups) + _dot(lo, groups)) * (1.0 / size)


def _diff_finish(o, n, lam, post, g2, by_head):
    den = o[:, LANES:2 * LANES]
    outs = []
    for t in range(2):
        p1 = o[2 * t * n:(2 * t + 1) * n, 0:LANES] * (1.0 / den[2 * t * n:(2 * t + 1) * n])
        p2 = o[(2 * t + 1) * n:(2 * t + 2) * n, 0:LANES] * (lam / den[(2 * t + 1) * n:(2 * t + 2) * n])
        outs.append(p1 - p2)
    d = jnp.where(_lane_range(0, 64), outs[0], outs[1])
    return d * lax.rsqrt(_group_mean_sq(d, by_head, DIFF_V_DIM) + EPS) * g2 * post


def _mla_groups(j):
    return _group_matrix(2 * LANES, 2,
                         lambda t: (LANES + MLA_ROPE * (2 * j + t), LANES + MLA_ROPE * (2 * j + t + 1)))


def _mla_queries(qn_pair, qp_all, j):
    halves = []
    for t in range(2):
        h = 2 * j + t
        halves.append(jnp.concatenate([
            _bf(jnp.where(_lane_range(64 * t, 64 * (t + 1)), qn_pair, 0.0)),
            _bf(jnp.where(_lane_range(MLA_ROPE * h, MLA_ROPE * (h + 1)), qp_all, 0.0))], axis=1))
    return jnp.concatenate(halves, axis=0)


def _write_heads_t(p_ref, rows, col0, out_ref, bb):
    xt = p_ref[rows, col0:col0 + WIDTH].T
    for h in range(N_HEADS):
        out_ref[bb, 0, h] = xt[64 * h:64 * (h + 1)]
    _clear_other_layers(out_ref, bb)


def _clear_other_layers(out_ref, bb):
    if out_ref.shape[1] > 1:
        out_ref[bb, 1:] = jnp.zeros(out_ref.shape[1:], jnp.float32)[1:]


def _mix_prompt_kernel(n_prev, *refs):
    ins, outs = refs[:8], refs[8 + n_prev:]
    log = _ShiftLog()
    _mix_prompt_pass(ins, outs, log)

    @pl.when(log.unsafe())
    def _():
        _mix_prompt_pass(ins, outs, None)


def _mix_prompt_pass(ins, outs, log):
    pa_ref, pb_ref, pc_ref, cst_ref, gsub_ref, gckv_ref, wuk_ref, wuv_ref = ins
    o_ref, nak_ref, nav_ref, dk_ref, dv_ref, ckv_ref, kpe_ref = outs
    first_pass = log is not None
    c_a = HEAD_DIM ** -0.5 * LOG2E
    c_b = DIFF_QK_DIM ** -0.5 * LOG2E
    c_c = (MLA_NOPE + MLA_ROPE) ** -0.5 * LOG2E
    lam = cst_ref[0, 0:1, 0:1]
    post = cst_ref[0, 1:2, 0:1]
    wuk, wuv = _bf(wuk_ref[0]), _bf(wuv_ref[0])
    by_head, by_comp = _group_matrix(LANES, 2), _group_matrix(LANES, 4)

    def sequence(bb, log):
        rows = pl.ds(_aligned(bb * SEQ, SEQ), SEQ)
        for j in range(N_PAIRS):
            cols = slice(LANES * j, LANES * (j + 1))
            k = _bf(pa_ref[rows, WIDTH + LANES * j:WIDTH + LANES * (j + 1)])
            v = _with_ones(_bf(pa_ref[rows, 2 * WIDTH + LANES * j:2 * WIDTH + LANES * (j + 1)]))
            qs = _stack_heads(pa_ref[rows, cols] * c_a)
            o = _attend([_dot_nt(qs, k)], [v], log, _bound(log, qs, _key_bound([k], by_head), by_head, 2))
            o_ref[rows, cols] = _unstack_heads(_normalised(o), SEQ)
        if first_pass:
            _write_heads_t(pa_ref, rows, WIDTH, nak_ref, bb)
            _write_heads_t(pa_ref, rows, 2 * WIDTH, nav_ref, bb)
        for j in range(N_PAIRS):
            cols = slice(LANES * j, LANES * (j + 1))
            k = _bf(pb_ref[rows, WIDTH + LANES * j:WIDTH + LANES * (j + 1)])
            v = _with_ones(_bf(pb_ref[rows, 2 * WIDTH + LANES * j:2 * WIDTH + LANES * (j + 1)]))
            qs = _stack_components(pb_ref[rows, cols] * c_b)
            o = _attend([_dot_nt(qs, k)], [v], log, _bound(log, qs, _key_bound([k], by_comp), by_comp, 4))
            o_ref[rows, WIDTH + LANES * j:WIDTH + LANES * (j + 1)] = _diff_finish(o, SEQ, lam, post, gsub_ref[0], by_head)
        if first_pass:
            _write_heads_t(pb_ref, rows, WIDTH, dk_ref, bb)
            _write_heads_t(pb_ref, rows, 2 * WIDTH, dv_ref, bb)
        ckv = _rms(pc_ref[rows, C_CKV:C_CKV + MLA_KV_RANK], gckv_ref[0])
        kpe_slot = pc_ref[rows, C_KPE:C_KPE + LANES]
        if first_pass:
            ckv_ref[bb, 0] = ckv
            _clear_other_layers(ckv_ref, bb)
            kpe_ref[bb, 0] = kpe_slot.T[0:MLA_ROPE]
            _clear_other_layers(kpe_ref, bb)
        ckv_b = _bf(ckv)
        kn = _bf(_dot(ckv_b, wuk))
        vv = _bf(_dot(ckv_b, wuv))
        kpe4 = _bf(_tile4(kpe_slot))
        qn = pc_ref[rows, C_QN:C_QN + WIDTH] * c_c
        qp = pc_ref[rows, C_QP:C_QP + LANES] * c_c
        for j in range(N_PAIRS):
            cols = slice(LANES * j, LANES * (j + 1))
            k = jnp.concatenate([kn[:, cols], kpe4], axis=1)
            qs = _mla_queries(qn[:, cols], qp, j)
            groups = _mla_groups(j)
            o = _attend([_dot_nt(qs, k)], [_with_ones(vv[:, cols])], log,
                        _bound(log, qs, _key_bound([k], groups), groups, 2))
            o_ref[rows, 2 * WIDTH + LANES * j:2 * WIDTH + LANES * (j + 1)] = _unstack_heads(_normalised(o), SEQ)

    _loop(PB, log, sequence)


def _mix_prompt(l, pa, pb, pc, cst, g_sub2, g_ckv, w_uk, w_uv, prev):
    n_prev = len(prev)
    tails = [(NA_HEADS, HEAD_DIM, SEQ)] * 2 + [(DIFF_HEADS, 64, SEQ)] * 2 + [(SEQ, MLA_KV_RANK), (MLA_ROPE, SEQ)]

    def cache_spec(tail):
        if l == 0:
            return pl.BlockSpec((PB, DEPTH) + tail, lambda b: (b, 0) + (0,) * len(tail))
        return pl.BlockSpec((PB, 1) + tail, lambda b: (b, l) + (0,) * len(tail))

    def rows(width):
        return pl.BlockSpec((PB * SEQ, width), lambda b: (b, 0))

    def layer(*tail):
        return pl.BlockSpec((1,) + tail, lambda b: (l,) + (0,) * len(tail))

    return pl.pallas_call(
        functools.partial(_mix_prompt_kernel, n_prev),
        grid=(BATCH // PB,),
        in_specs=[rows(SEG_A), rows(SEG_B), rows(SEG_C), layer(2, LANES), layer(1, LANES), layer(1, MLA_KV_RANK),
                  layer(MLA_KV_RANK, WIDTH), layer(MLA_KV_RANK, WIDTH)] + [pl.BlockSpec(memory_space=pl.ANY)] * n_prev,
        out_specs=[rows(O_ATT)] + [cache_spec(t) for t in tails],
        out_shape=[jax.ShapeDtypeStruct((N_PROMPT, O_ATT), jnp.float32)]
        + [jax.ShapeDtypeStruct((BATCH, DEPTH) + t, jnp.float32) for t in tails],
        input_output_aliases={8 + i: 1 + i for i in range(n_prev)},
        compiler_params=_params("parallel"),
        name="mix_prompt",
    )(pa, pb, pc, cst, g_sub2, g_ckv, w_uk, w_uv, *prev)


def _na_row_groups():
    kh = min(NA_WIN_ROWS, GRID_ROWS)
    r0s = [min(max(r - kh // 2, 0), GRID_ROWS - kh) for r in range(GRID_ROWS)]
    groups = []
    for r, r0 in enumerate(r0s):
        if groups and groups[-1][2] == r0:
            groups[-1][1] = r
        else:
            groups.append([r, r, r0])
    return kh, [tuple(g) for g in groups]


def _na_sample(q_ref, k_ref, v_ref, ck_ref, cv_ref, tt_ref, o_ref, log):
    c = HEAD_DIM ** -0.5 * LOG2E
    kh, groups = _na_row_groups()
    lk = kh * GRID_W
    edge = [g for g in groups if g[1] > g[0]]
    inner = [g for g in groups if g[1] == g[0]]
    depth = inner[0][0] - inner[0][2]
    assert all(g[0] - g[2] == depth for g in inner) and [g[0] for g in inner] == list(range(inner[0][0], inner[-1][0] + 1))

    def in_window(n):
        cq = lax.broadcasted_iota(jnp.int32, (n, lk), 0) & (GRID_W - 1)
        ck = lax.broadcasted_iota(jnp.int32, (n, lk), 1) & (GRID_W - 1)
        c0 = jnp.clip(cq - NA_WIN_COLS // 2, 0, GRID_W - NA_WIN_COLS)
        return (ck >= c0) & (ck < c0 + NA_WIN_COLS)

    kc_t = _bf(_pair_t(ck_ref))
    vc = _with_ones(_bf(_pair_t(cv_ref).T))
    by_head = _group_matrix(LANES, 2)
    k2max = _key_bound([_bf(k_ref[...]), _bf(_pair_t(ck_ref).T)], by_head)
    tmax = functools.reduce(jnp.maximum, [tt_ref[t, a] for t in range(2) for a in range(N_DROW - 1)])
    bplus = jnp.maximum(jnp.max(jnp.max(tmax, axis=-1, keepdims=True), axis=0, keepdims=True), 0.0)

    def group(row0, key0, offsets, log):
        n = len(offsets) * GRID_W
        rows, keys = pl.ds(row0, n), pl.ds(key0, lk)
        qg = _stack_heads(q_ref[rows, :] * c)
        k = _bf(k_ref[keys, :])
        v = _with_ones(_bf(v_ref[keys, :]))
        bias = jnp.concatenate([
            jnp.concatenate([tt_ref[t, 2 * i - off + NA_WIN_ROWS - 1] for i in range(kh // 2)], axis=1)
            for t in range(2) for off in offsets], axis=0)
        s_loc = jnp.where(in_window(2 * n), _dot_nt(qg, k) + bias, NEG_INF)
        o = _attend([_dot(qg, kc_t), s_loc], [vc, v], log, _bound(log, qg, k2max, by_head, 2, bplus))
        o_ref[rows, :] = _unstack_heads(_normalised(o), n)

    for (r_lo, r_hi, r0) in edge:
        group(r_lo * GRID_W, r0 * GRID_W, [r - r0 for r in range(r_lo, r_hi + 1)], log)

    def inner_row(i, log):
        r = inner[0][0] + i
        group(_aligned(r * GRID_W, GRID_W), _aligned((r - depth) * GRID_W, GRID_W), [depth], log)

    _loop(len(inner), log, inner_row)


def _diff_sample(q_ref, k_ref, v_ref, ck_ref, cv_ref, cos_ref, sin_ref, cst_ref, g_ref, o_ref, log):
    c = DIFF_QK_DIM ** -0.5 * LOG2E
    lam = cst_ref[0, 0:1, 0:1]
    post = cst_ref[0, 1:2, 0:1]
    k_new = _bf(_rope(k_ref[...], cos_ref[...], sin_ref[...]))
    kc_t = _bf(_pair_t(ck_ref))
    vc = _with_ones(_bf(_pair_t(cv_ref).T))
    v = _with_ones(_bf(v_ref[...]))
    by_head, by_comp = _group_matrix(LANES, 2), _group_matrix(LANES, 4)
    k2max = _key_bound([k_new, _bf(_pair_t(ck_ref).T)], by_comp)

    def block(qi, log):
        rows = pl.ds(_aligned(qi * QB, QB), QB)
        qs = _stack_components(_rope(q_ref[rows, :], cos_ref[rows, :], sin_ref[rows, :]) * c)
        o = _attend([_dot(qs, kc_t), _dot_nt(qs, k_new)], [vc, v], log, _bound(log, qs, k2max, by_comp, 4))
        o_ref[rows, :] = _diff_finish(o, QB, lam, post, g_ref[0], by_head)

    _loop(DEC_SEQ // QB, log, block)


def _mla_sample(j, qn_ref, qp_ref, ckv_ref, kpe_ref, cckv_ref, ckpe_ref, cosq_ref, sinq_ref, cosk_ref, sink_ref,
                gckv_ref, wuk_ref, wuv_ref, o_ref, log):
    c = (MLA_NOPE + MLA_ROPE) ** -0.5 * LOG2E
    wuk, wuv = _bf(wuk_ref[0]), _bf(wuv_ref[0])
    ckv_new = _bf(_rms(ckv_ref[...], gckv_ref[0]))
    ckv_old = _bf(cckv_ref[...])
    kpe_new = _bf(_tile4(_rope(kpe_ref[...], cosk_ref[...], sink_ref[...])))
    kpe_old = _bf(jnp.concatenate([ckpe_ref[...]] * MLA_HEADS, axis=0).T)
    k_old = jnp.concatenate([_bf(_dot(ckv_old, wuk)), kpe_old], axis=1)
    k_new = jnp.concatenate([_bf(_dot(ckv_new, wuk)), kpe_new], axis=1)
    vo, vn = _with_ones(_bf(_dot(ckv_old, wuv))), _with_ones(_bf(_dot(ckv_new, wuv)))
    groups = _mla_groups(j)
    k2max = _key_bound([k_old, k_new], groups)

    def block(qi, log):
        rows = pl.ds(_aligned(qi * QB, QB), QB)
        qp = _rope(qp_ref[rows, :], cosq_ref[rows, :], sinq_ref[rows, :]) * c
        qs = _mla_queries(qn_ref[rows, :] * c, qp, j)
        o = _attend([_dot_nt(qs, k_old), _dot_nt(qs, k_new)], [vo, vn], log, _bound(log, qs, k2max, groups, 2))
        o_ref[rows, :] = _unstack_heads(_normalised(o), QB)

    _loop(DEC_SEQ // QB, log, block)


def _mix_sample_kernel(qa_ref, ka_ref, va_ref, qb_ref, kb_ref, vb_ref, qn_ref, qp_ref, ckv_ref, kpe_ref,
                       cnak_ref, cnav_ref, cdk_ref, cdv_ref, cckv_ref, ckpe_ref, tt_ref,
                       cosb_ref, sinb_ref, cosq_ref, sinq_ref, cosk_ref, sink_ref,
                       cst_ref, gsub_ref, gckv_ref, wuk_ref, wuv_ref, oa_ref, ob_ref, oc_ref):
    j = pl.program_id(1)

    def run(log):
        _na_sample(qa_ref, ka_ref, va_ref, cnak_ref, cnav_ref, tt_ref, oa_ref, log)
        _diff_sample(qb_ref, kb_ref, vb_ref, cdk_ref, cdv_ref, cosb_ref, sinb_ref, cst_ref, gsub_ref, ob_ref, log)
        _mla_sample(j, qn_ref, qp_ref, ckv_ref, kpe_ref, cckv_ref, ckpe_ref, cosq_ref, sinq_ref, cosk_ref, sink_ref,
                    gckv_ref, wuk_ref, wuv_ref, oc_ref, log)

    log = _ShiftLog()
    run(log)

    @pl.when(log.unsafe())
    def _():
        run(None)


def _mix_sample(l, pa, pb, pc, caches_t, tt, tables, cst, g_sub2, g_ckv, w_uk, w_uv):
    first = N_PROMPT // DEC_SEQ

    def cols(block):
        return pl.BlockSpec((DEC_SEQ, LANES), lambda b, j: (first + b, block(j)))

    def cache(*tail, pair=False):
        return pl.BlockSpec((None, None) + tail, lambda b, j: (b, l, j if pair else 0) + (0,) * (len(tail) - 1))

    def layer(*tail):
        return pl.BlockSpec((1,) + tail, lambda b, j: (l,) + (0,) * len(tail))

    table = pl.BlockSpec((DEC_SEQ, LANES), lambda b, j: (0, 0), pipeline_mode=pl.Buffered(1))
    qkv = [cols(lambda j: j), cols(lambda j: N_PAIRS + j), cols(lambda j: 2 * N_PAIRS + j)]
    seg_c = [cols(lambda j: j), cols(lambda j: C_QP // LANES), cols(lambda j: C_CKV // LANES), cols(lambda j: C_KPE // LANES)]
    kv_t = cache(2, 64, PAST_LEN, pair=True)
    w_pair = pl.BlockSpec((1, MLA_KV_RANK, LANES), lambda b, j: (l, 0, j))
    out = pl.BlockSpec((DEC_SEQ, LANES), lambda b, j: (b, j))
    return pl.pallas_call(
        _mix_sample_kernel,
        grid=(DEC_BATCH, N_PAIRS),
        in_specs=qkv + qkv + seg_c + [kv_t, kv_t, kv_t, kv_t, cache(PAST_LEN, MLA_KV_RANK), cache(MLA_ROPE, PAST_LEN),
                                      pl.BlockSpec((None, 2, N_DROW - 1, GRID_W, LANES), lambda b, j: (l, j, 0, 0, 0)),
                                      table, table, table, table, table, table,
                                      layer(2, LANES), layer(1, LANES), layer(1, MLA_KV_RANK), w_pair, w_pair],
        out_specs=[out, out, out],
        out_shape=[jax.ShapeDtypeStruct((N_SAMPLE, WIDTH), jnp.float32)] * 3,
        compiler_params=_params("parallel", "arbitrary"),
        name="mix_sample",
    )(pa, pa, pa, pb, pb, pb, pc, pc, pc, pc, *caches_t, tt, *tables, cst, g_sub2, g_ckv, w_uk, w_uv)


FF_CHUNK = 1024


def _outffn_kernel(n_x, first, final, *refs):
    x_refs, op_ref, os_refs = refs[:n_x], refs[n_x], refs[n_x + 1:n_x + 4]
    (od_ref, wout_ref, g1_ref, gffn_ref, sh2_ref, sc2_ref, g2_ref, w1_ref, w2_ref, gfin_ref, y_ref) = refs[n_x + 4:]
    o_att = jnp.where(first + pl.program_id(0) < TILES_PROMPT, op_ref[...],
                      jnp.concatenate([r[...] for r in os_refs], axis=1))
    acc = (_dot(_bf(o_att), wout_ref[0, 0:O_ATT, :])
           + _dot(_bf(od_ref[...]), wout_ref[0, O_ATT:O_ATT + WIDTH, :]))
    x1 = _read_tile(x_refs, first) + g1_ref[...] * acc
    hf = _bf(_rms(x1, gffn_ref[0]) * (1.0 + sc2_ref[...]) + sh2_ref[...])
    acc = jnp.zeros((TM, D_MODEL), jnp.float32)
    for c in range(D_FF // FF_CHUNK):
        cols = slice(FF_CHUNK * c, FF_CHUNK * (c + 1))
        a = jnp.square(jnp.maximum(_dot(hf, w1_ref[0, :, cols]), 0.0))
        acc += _dot(_bf(a), w2_ref[0, cols, :])
    y = x1 + g2_ref[...] * acc
    if final:
        y = _rms(y, gfin_ref[...])
    y_ref[...] = y


def _outffn(l, xs, o_p, o_s, od, w_out, g_ffn, mod, w1, w2, g_final, first, n_tiles):
    def mod_spec(j):
        return pl.BlockSpec((None, None, 1, D_MODEL), lambda i: (l, _row_group(first + i), 0, j))

    def resident(shape):
        return pl.BlockSpec(shape, lambda i: (l,) + (0,) * (len(shape) - 1), pipeline_mode=pl.Buffered(1))

    return pl.pallas_call(
        functools.partial(_outffn_kernel, len(xs), first, l == DEPTH - 1),
        grid=(n_tiles,),
        in_specs=_x_specs(len(xs) == 2, first) + _split_specs(O_ATT, first)[:1] + _split_specs(WIDTH, first)[1:] * 3 + [
            pl.BlockSpec((TM, WIDTH), lambda i: (first + i, 0)),
            resident((1, 4 * WIDTH, D_MODEL)),
            mod_spec(2),
            pl.BlockSpec((1, 1, D_MODEL), lambda i: (l, 0, 0)),
            mod_spec(3), mod_spec(4), mod_spec(5),
            resident((1, D_MODEL, D_FF)),
            resident((1, D_FF, D_MODEL)),
            pl.BlockSpec((1, D_MODEL), lambda i: (0, 0)),
        ],
        out_specs=pl.BlockSpec((TM, D_MODEL), lambda i: (i, 0)),
        out_shape=jax.ShapeDtypeStruct((n_tiles * TM, D_MODEL), jnp.float32),
        compiler_params=_params("parallel"),
        name="outffn",
    )(*xs, o_p, *o_s, od, w_out, mod, g_ffn, mod, mod, mod, w1, w2, g_final)


def _rope32_tables():
    t = np.arange(DEC_SEQ)
    rows, cols = (t // GRID_W).astype(np.float64), (t % GRID_W).astype(np.float64)
    half = 8
    freqs = ROPE_BASE ** (-np.arange(half, dtype=np.float64) / half)
    cos, sin = [], []
    for pos in (rows, cols):
        ang = pos[:, None] * freqs[None, :]
        cos += [np.cos(ang), np.cos(ang)]
        sin += [-np.sin(ang), np.sin(ang)]
    return np.concatenate(cos, axis=1).astype(np.float32), np.concatenate(sin, axis=1).astype(np.float32)


def _rope_tables():
    c32, s32 = _rope32_tables()
    tile = lambda a, n: np.tile(a, (1, n))
    pad = np.zeros((DEC_SEQ, LANES - MLA_ROPE), np.float32)
    cos_k = np.concatenate([c32, pad + 1.0], axis=1)
    sin_k = np.concatenate([s32, pad], axis=1)
    return (tile(c32, 4), tile(s32, 4),
            cos_k, sin_k)


def kernel(x_prompt, x_sample, cache_na_k, cache_na_v, cache_diff_k, cache_diff_v, cache_mla_ckv, cache_mla_kpe, c, c_ctx, w_ada, b_ada, g_mix, g_ffn, w_in, w_out, na_rpb, diff_lq1, diff_lk1, diff_lq2, diff_lk2, diff_g_subln, mla_g_ckv, mla_w_uk, mla_w_uv, sgu_g, sgu_w, sgu_b, w_ff1, w_ff2, g_final):
    f32 = jnp.float32
    m = jnp.concatenate([c_ctx[None, :], c, jnp.zeros((N_MOD_ROWS - 1 - DEC_BATCH, D_MODEL), f32)], axis=0)
    mod = _ada(m, w_ada, b_ada).reshape(DEPTH, N_MOD_ROWS, 1, 6 * D_MODEL)
    cst = _lam_consts(diff_lq1, diff_lk1, diff_lq2, diff_lk2)
    tt = _bias_tiles(na_rpb)
    cos4, sin4, cos_k, sin_k = [jnp.asarray(t) for t in _rope_tables()]
    tables = (cos4, sin4, cos4, sin4, cos_k, sin_k)

    t_last = lambda a: jnp.swapaxes(a, -1, -2)
    w_in_t = t_last(w_in)
    caches_t = (t_last(cache_na_k), t_last(cache_na_v), t_last(cache_diff_k), t_last(cache_diff_v),
                cache_mla_ckv, t_last(cache_mla_kpe))
    w_out_b, w1_b, w2_b = _bf(w_out), _bf(w_ff1), _bf(w_ff2)
    g_mix3 = g_mix.reshape(DEPTH, 1, D_MODEL)
    g_ffn3 = g_ffn.reshape(DEPTH, 1, D_MODEL)
    g_sub2 = jnp.tile(diff_g_subln, (1, 2)).reshape(DEPTH, 1, LANES)
    g_ckv3 = mla_g_ckv.reshape(DEPTH, 1, MLA_KV_RANK)
    sgu_g3 = sgu_g.reshape(DEPTH, 1, WIDTH)
    sgu_bt = sgu_b.transpose(0, 2, 1)
    g_fin2 = g_final.reshape(1, D_MODEL)

    xs = (x_prompt.reshape(N_PROMPT, D_MODEL), x_sample.reshape(N_SAMPLE, D_MODEL))
    new = ()
    for l in range(DEPTH):
        pa, pb, pc, od = _inproj(l, xs, g_mix3, mod, w_in_t, sgu_g3, sgu_w, sgu_bt)
        o_p, *new = _mix_prompt(l, pa, pb, pc, cst, g_sub2, g_ckv3, mla_w_uk, mla_w_uv, new)
        o_s = _mix_sample(l, pa, pb, pc, caches_t, tt, tables, cst, g_sub2, g_ckv3, mla_w_uk, mla_w_uv)
        ffn = functools.partial(_outffn, l, xs, o_p, o_s, od, w_out_b, g_ffn3, mod, w1_b, w2_b, g_fin2)
        if l < DEPTH - 1:
            xs = (ffn(0, TILES_PROMPT + TILES_SAMPLE),)
        else:
            xs = (ffn(0, TILES_PROMPT), ffn(TILES_PROMPT, TILES_SAMPLE))
    y_prompt = xs[0].reshape(BATCH, SEQ, D_MODEL)
    y_sample = xs[1].reshape(DEC_BATCH, DEC_SEQ, D_MODEL)
    na_k, na_v, diff_k, diff_v, mla_ckv, mla_kpe = new
    return (y_prompt, y_sample, t_last(na_k), t_last(na_v), t_last(diff_k), t_last(diff_v), mla_ckv, t_last(mla_kpe))
```

```python
import functools
import math

import numpy as np
import jax
import jax.numpy as jnp
from jax import lax
from jax.experimental import pallas as pl
from jax.experimental.pallas import tpu as pltpu

D_MODEL = 1024
BATCH = 16
SEQ = 256
DEPTH = 4
DEC_BATCH = 2
DEC_SEQ = 1024
PAST_LEN = 512
GRID_W = 64
GRID_ROWS = DEC_SEQ // GRID_W
HEAD_DIM = 64
NA_HEADS = 4
NA_WIN_ROWS = 8
NA_WIN_COLS = 16
DIFF_HEADS = 4
DIFF_QK_DIM = 32
DIFF_V_DIM = 64
MLA_HEADS = 4
MLA_NOPE = 64
MLA_ROPE = 32
MLA_V = 64
MLA_KV_RANK = 128
SGU_GROUPS = 4
SGU_GROUP_DIM = 64
SGU_CHUNK = 128
D_FF = 4 * D_MODEL
ROPE_BASE = 10000.0
EPS = 1e-6
NEG_INF = -1e30
LOG2E = 1.4426950408889634

N_HEADS = 4
N_PAIRS = N_HEADS // 2
LANES = 128
WIDTH = 256
N_PROMPT = BATCH * SEQ
N_SAMPLE = DEC_BATCH * DEC_SEQ
N_TOK = N_PROMPT + N_SAMPLE
N_MOD_ROWS = 8

SEG_A = 3 * WIDTH
SEG_B = 3 * WIDTH
SEG_C = 640
SEG_D = 2 * WIDTH
SEG_C_PAD = 96
IN_COLS_P = SEG_A + SEG_B + SEG_C + SEG_D
O_ATT = 3 * WIDTH

TM = 512
TILES_PROMPT = N_PROMPT // TM
TILES_SAMPLE = N_SAMPLE // TM
PB = 2
QB = 256
VMEM_LIMIT = 56 * 1024 * 1024


def _bf(x):
    return x.astype(jnp.bfloat16)


def _dot(a, b):
    return jnp.dot(a, b, preferred_element_type=jnp.float32)


def _dot_nt(a, b):
    return lax.dot_general(a, b, (((1,), (1,)), ((), ())), preferred_element_type=jnp.float32)


def _rms(x, g):
    ms = jnp.mean(x * x, axis=-1, keepdims=True)
    return x * lax.rsqrt(ms + EPS) * g


def _lane_range(lo, hi, width=LANES):
    lane = lax.broadcasted_iota(jnp.int32, (1, width), 1)
    return (lane >= lo) & (lane < hi)


def _with_ones(v):
    return jnp.concatenate([v, jnp.ones((v.shape[0], LANES), jnp.bfloat16)], axis=1)


def _attend(scores, values, log=None, bound=None):
    if bound is None:
        m = functools.reduce(jnp.maximum, [jnp.max(s, axis=-1, keepdims=True) for s in scores])
        shifted = [s - m for s in scores]
    else:
        over = jnp.max(bound - scores[0][:, 0:LANES], axis=0, keepdims=True)
        log.worst = jnp.maximum(log.worst, jnp.min(over, axis=1, keepdims=True))
        shifted = [s - bound for s in scores]
    return functools.reduce(lambda a, b: a + b, [_dot(_bf(jnp.exp2(s)), v) for s, v in zip(shifted, values)])


BOUND_SLACK = 1.02
OVERSHOOT_LIMIT = 100.0


class _ShiftLog:
    def __init__(self, worst=None):
        self.worst = jnp.zeros((1, 1), jnp.float32) if worst is None else worst

    def unsafe(self):
        return jnp.logical_not(self.worst[0, 0] < OVERSHOOT_LIMIT)


def _group_matrix(width, n_groups, extra=None):
    i = lax.broadcasted_iota(jnp.int32, (width, LANES), 0)
    j = lax.broadcasted_iota(jnp.int32, (width, LANES), 1)
    size = LANES // n_groups
    hit = (i // size == j // size) & (i < LANES)
    for t in range(n_groups if extra else 0):
        lo, hi = extra(t)
        hit = hit | ((i >= lo) & (i < hi) & (j // size == t))
    return jnp.where(hit, 1.0, 0.0).astype(jnp.bfloat16)


def _squares(x):
    xf = x.astype(jnp.float32)
    return _bf(xf * xf)


def _key_bound(keys, groups):
    return functools.reduce(jnp.maximum, [jnp.max(_dot(_squares(k), groups), axis=0, keepdims=True) for k in keys])


def _bound(log, qs, k2max, groups, n_groups, extra=0.0):
    if log is None:
        return None
    per_group = _dot(_squares(qs), groups) * k2max
    ones = jnp.ones((LANES, LANES), jnp.bfloat16)
    sq = _dot(_bf(per_group), ones)[:, 0:1] * (n_groups / LANES)
    return jnp.sqrt(sq) * BOUND_SLACK + extra


def _loop(n, log, body):
    if log is not None:
        for i in range(n):
            body(i, log)
    else:
        lax.fori_loop(0, n, lambda i, carry: body(i, None) or carry, 0)


def _aligned(start, multiple):
    return start if isinstance(start, int) else pl.multiple_of(start, multiple)


def _normalised(o_ext):
    return o_ext[:, 0:LANES] * (1.0 / o_ext[:, LANES:2 * LANES])


def _swap8(x):
    lane = lax.broadcasted_iota(jnp.int32, (1, LANES), 1)
    return jnp.where((lane & 15) < 8, pltpu.roll(x, LANES - 8, 1), pltpu.roll(x, 8, 1))


def _rope(x, cos, sin):
    outs = []
    for c in range(x.shape[1] // LANES):
        sl = slice(LANES * c, LANES * (c + 1))
        xc = x[:, sl]
        outs.append(xc * cos[:, sl] + _swap8(xc) * sin[:, sl])
    return outs[0] if len(outs) == 1 else jnp.concatenate(outs, axis=1)


def _tile4(x):
    return x + pltpu.roll(x, 32, 1) + pltpu.roll(x, 64, 1) + pltpu.roll(x, 96, 1)


def _params(*sem):
    return pltpu.CompilerParams(dimension_semantics=sem, vmem_limit_bytes=VMEM_LIMIT)


ADA_TN = 1536


def _ada_kernel(m_ref, w_ref, b_ref, o_ref):
    m = m_ref[...]
    s = m * jax.nn.sigmoid(m)
    o_ref[0] = _dot(_bf(s), _bf(w_ref[0])) + b_ref[0]


def _ada(m, w_ada, b_ada):
    n = 6 * D_MODEL
    return pl.pallas_call(
        _ada_kernel,
        grid=(DEPTH, n // ADA_TN),
        in_specs=[
            pl.BlockSpec((N_MOD_ROWS, D_MODEL), lambda l, j: (0, 0)),
            pl.BlockSpec((1, D_MODEL, ADA_TN), lambda l, j: (l, 0, j)),
            pl.BlockSpec((1, 1, ADA_TN), lambda l, j: (l, 0, j)),
        ],
        out_specs=pl.BlockSpec((1, N_MOD_ROWS, ADA_TN), lambda l, j: (l, 0, j)),
        out_shape=jax.ShapeDtypeStruct((DEPTH, N_MOD_ROWS, n), jnp.float32),
        compiler_params=_params("parallel", "parallel"),
        name="ada",
    )(m, w_ada, b_ada.reshape(DEPTH, 1, n))


def _lam_kernel(lq1_ref, lk1_ref, lq2_ref, lk2_ref, init_ref, o_ref):
    init = init_ref[...]
    a = jnp.exp(jnp.sum(lq1_ref[...] * lk1_ref[...], axis=-1, keepdims=True))
    b = jnp.exp(jnp.sum(lq2_ref[...] * lk2_ref[...], axis=-1, keepdims=True))
    lam = a - b + init
    post = 1.0 - init
    for l in range(DEPTH):
        o_ref[l, 0:1, :] = jnp.broadcast_to(lam[l:l + 1], (1, LANES))
        o_ref[l, 1:2, :] = jnp.broadcast_to(post[l:l + 1], (1, LANES))


def _lam_consts(lq1, lk1, lq2, lk2):
    init = np.array([[0.8 - 0.6 * math.exp(-0.3 * l)] for l in range(DEPTH)], np.float32)
    return pl.pallas_call(
        _lam_kernel,
        out_shape=jax.ShapeDtypeStruct((DEPTH, 2, LANES), jnp.float32),
        name="diff_lambda",
    )(lq1, lk1, lq2, lk2, jnp.asarray(init))


N_DROW = 2 * NA_WIN_ROWS - 1
N_DCOL = 2 * NA_WIN_COLS - 1


def _bias_kernel(rpb_ref, o_ref):
    l = pl.program_id(0)
    h = pl.program_id(1)
    base = (l * NA_HEADS + h) * (N_DROW * N_DCOL)
    cq = lax.broadcasted_iota(jnp.int32, (GRID_W, LANES), 0)
    lane = lax.broadcasted_iota(jnp.int32, (GRID_W, LANES), 1)
    ck = lane & (GRID_W - 1)
    dcol = jnp.clip(ck - cq, -(NA_WIN_COLS - 1), NA_WIN_COLS - 1) + (NA_WIN_COLS - 1)
    hi = lane >= GRID_W
    for a in range(N_DROW - 1):
        acc = jnp.zeros((GRID_W, LANES), jnp.float32)
        for j in range(N_DCOL):
            lo_v = rpb_ref[base + a * N_DCOL + j]
            hi_v = rpb_ref[base + (a + 1) * N_DCOL + j]
            acc = jnp.where(dcol == j, jnp.where(hi, hi_v, lo_v), acc)
        o_ref[0, 0, a] = acc * LOG2E


def _bias_tiles(na_rpb):
    return pl.pallas_call(
        _bias_kernel,
        grid=(DEPTH, NA_HEADS),
        in_specs=[pl.BlockSpec(memory_space=pltpu.SMEM)],
        out_specs=pl.BlockSpec((1, 1, N_DROW - 1, GRID_W, LANES), lambda l, h: (l, h, 0, 0, 0)),
        out_shape=jax.ShapeDtypeStruct((DEPTH, NA_HEADS, N_DROW - 1, GRID_W, LANES), jnp.float32),
        compiler_params=_params("parallel", "parallel"),
        name="na_bias_tiles",
    )(na_rpb.reshape(-1))


def _row_group(i):
    return jnp.where(i < TILES_PROMPT, 0, 1 + (i - TILES_PROMPT) // (DEC_SEQ // TM))


def _split_specs(width, first):
    return [pl.BlockSpec((TM, width), lambda i: (jnp.minimum(first + i, TILES_PROMPT - 1), 0)),
            pl.BlockSpec((TM, width), lambda i: (jnp.maximum(first + i - TILES_PROMPT, 0), 0))]


def _x_specs(split, first):
    if not split:
        return [pl.BlockSpec((TM, D_MODEL), lambda i: (first + i, 0))]
    return _split_specs(D_MODEL, first)


def _read_tile(refs, first):
    if len(refs) == 1:
        return refs[0][...]
    return jnp.where(first + pl.program_id(0) < TILES_PROMPT, refs[0][...], refs[1][...])


IN_COLS = 2592
IN_QC, IN_CKV, IN_D = 1536, 1920, 2080
TR_ROWS = 256


def _gelu_tanh(x):
    return 0.5 * x * (1.0 + jnp.tanh(math.sqrt(2.0 / math.pi) * (x + 0.044715 * (x * x * x))))


def _sgu(pd, g, w_ref, bt):
    u = _gelu_tanh(pd[:, 0:WIDTH])
    v = _gelu_tanh(pd[:, WIDTH:2 * WIDTH])
    grp = lax.broadcasted_iota(jnp.int32, (1, WIDTH), 1) // SGU_GROUP_DIM
    by_group = _group_matrix(LANES, LANES // SGU_GROUP_DIM)
    ms = jnp.concatenate([_group_mean_sq(v[:, LANES * t:LANES * (t + 1)], by_group, SGU_GROUP_DIM)
                          for t in range(WIDTH // LANES)], axis=1)
    vg = _bf(v * lax.rsqrt(ms + EPS) * g)
    outs = []
    for c in range(pd.shape[0] // SGU_CHUNK):
        rows = slice(SGU_CHUNK * c, SGU_CHUNK * (c + 1))
        mixed = jnp.zeros((SGU_CHUNK, WIDTH), jnp.float32)
        for gi in range(SGU_GROUPS):
            full = _dot(_bf(w_ref[0, gi]), vg[rows]) + bt[:, gi:gi + 1]
            mixed = jnp.where(grp == gi, full, mixed)
        outs.append(u[rows] * mixed)
    return jnp.concatenate(outs, axis=0)


def _w_in_row_pieces():
    qn = [(IN_QC + 96 * h, MLA_NOPE) for h in range(MLA_HEADS)]
    qp = [(IN_QC + 96 * h + MLA_NOPE, MLA_ROPE) for h in range(MLA_HEADS)]
    seg_c = qn + qp + [(IN_CKV, MLA_KV_RANK + MLA_ROPE)]
    return (0, SEG_A + SEG_B), seg_c, (IN_D, SEG_D)


def _load_w_in(wt_ref, w_scr):
    ab, seg_c, d = _w_in_row_pieces()
    c_rows = jnp.concatenate([wt_ref[0, s:s + n, :] for s, n in seg_c]
                             + [jnp.zeros((SEG_C_PAD, D_MODEL), jnp.float32)], axis=0)
    for t in range(SEG_C // LANES):
        w_scr[:, SEG_A + SEG_B + LANES * t:SEG_A + SEG_B + LANES * (t + 1)] = _bf(c_rows[LANES * t:LANES * (t + 1)].T)
    for (src, n), dst in ((ab, 0), (d, SEG_A + SEG_B + SEG_C)):
        for t in range(n // TR_ROWS):
            rows = wt_ref[0, src + TR_ROWS * t:src + TR_ROWS * (t + 1), :]
            w_scr[:, dst + TR_ROWS * t:dst + TR_ROWS * (t + 1)] = _bf(rows.T)


def _inproj_kernel(n_x, *refs):
    x_refs = refs[:n_x]
    (g_ref, sh_ref, sc_ref, wt_ref, sg_ref, sw_ref, sbt_ref, pa_ref, pb_ref, pc_ref, od_ref, w_scr) = refs[n_x:]

    @pl.when(pl.program_id(0) == 0)
    def _():
        _load_w_in(wt_ref, w_scr)

    h = _rms(_read_tile(x_refs, 0), g_ref[0]) * (1.0 + sc_ref[...]) + sh_ref[...]
    hb = _bf(h)
    off = SEG_A + SEG_B + SEG_C
    od_ref[...] = _sgu(_dot(hb, w_scr[:, off:off + SEG_D]), sg_ref[0], sw_ref, sbt_ref[0])
    off = 0
    for ref in (pa_ref, pb_ref, pc_ref):
        n = ref.shape[1]
        ref[...] = _dot(hb, w_scr[:, off:off + n])
        off += n


def _inproj(l, xs, g_mix, mod, w_in_t, sgu_g, sgu_w, sgu_bt):
    def mod_spec(j):
        return pl.BlockSpec((None, None, 1, D_MODEL), lambda i: (l, _row_group(i), 0, j))

    widths = (SEG_A, SEG_B, SEG_C, WIDTH)
    return pl.pallas_call(
        functools.partial(_inproj_kernel, len(xs)),
        grid=(N_TOK // TM,),
        in_specs=_x_specs(len(xs) == 2, 0) + [
            pl.BlockSpec((1, 1, D_MODEL), lambda i: (l, 0, 0)),
            mod_spec(0), mod_spec(1),
            pl.BlockSpec((1, IN_COLS, D_MODEL), lambda i: (l, 0, 0), pipeline_mode=pl.Buffered(1)),
            pl.BlockSpec((1, 1, WIDTH), lambda i: (l, 0, 0)),
            pl.BlockSpec((1, SGU_GROUPS, SGU_CHUNK, SGU_CHUNK), lambda i: (l, 0, 0, 0)),
            pl.BlockSpec((1, SGU_CHUNK, SGU_GROUPS), lambda i: (l, 0, 0)),
        ],
        out_specs=[pl.BlockSpec((TM, n), lambda i: (i, 0)) for n in widths],
        out_shape=[jax.ShapeDtypeStruct((N_TOK, n), jnp.float32) for n in widths],
        scratch_shapes=[pltpu.VMEM((D_MODEL, IN_COLS_P), jnp.bfloat16)],
        compiler_params=_params("arbitrary"),
        name="inproj",
    )(*xs, g_mix, mod, mod, w_in_t, sgu_g, sgu_w, sgu_bt)


C_QN, C_QP, C_CKV, C_KPE = 0, 256, 384, 512


def _stack_heads(qp):
    lo = _lane_range(0, 64)
    return jnp.concatenate([_bf(jnp.where(lo, qp, 0.0)), _bf(jnp.where(lo, 0.0, qp))], axis=0)


def _unstack_heads(o, n):
    return jnp.where(_lane_range(0, 64), o[0:n], o[n:2 * n])


def _pair_t(c_ref):
    return jnp.concatenate([c_ref[0], c_ref[1]], axis=0)


def _stack_components(qp):
    return jnp.concatenate([_bf(jnp.where(_lane_range(32 * t, 32 * (t + 1)), qp, 0.0)) for t in range(4)], axis=0)


def _group_mean_sq(x, groups, size):
    sq = x * x
    hi = _bf(sq)
    rest = sq - hi.astype(jnp.float32)
    mid = _bf(rest)
    lo = _bf(rest - mid.astype(jnp.float32))
    return (_dot(hi, groups) + _dot(mid, groups) + _dot(lo, groups)) * (1.0 / size)


def _diff_finish(o, n, lam, post, g2, by_head):
    den = o[:, LANES:2 * LANES]
    outs = []
    for t in range(2):
        p1 = o[2 * t * n:(2 * t + 1) * n, 0:LANES] * (1.0 / den[2 * t * n:(2 * t + 1) * n])
        p2 = o[(2 * t + 1) * n:(2 * t + 2) * n, 0:LANES] * (lam / den[(2 * t + 1) * n:(2 * t + 2) * n])
        outs.append(p1 - p2)
    d = jnp.where(_lane_range(0, 64), outs[0], outs[1])
    return d * lax.rsqrt(_group_mean_sq(d, by_head, DIFF_V_DIM) + EPS) * g2 * post


def _mla_groups(j):
    return _group_matrix(2 * LANES, 2,
                         lambda t: (LANES + MLA_ROPE * (2 * j + t), LANES + MLA_ROPE * (2 * j + t + 1)))


def _mla_queries(qn_pair, qp_all, j):
    halves = []
    for t in range(2):
        h = 2 * j + t
        halves.append(jnp.concatenate([
            _bf(jnp.where(_lane_range(64 * t, 64 * (t + 1)), qn_pair, 0.0)),
            _bf(jnp.where(_lane_range(MLA_ROPE * h, MLA_ROPE * (h + 1)), qp_all, 0.0))], axis=1))
    return jnp.concatenate(halves, axis=0)


def _write_heads_t(p_ref, rows, col0, out_ref, bb):
    xt = p_ref[rows, col0:col0 + WIDTH].T
    for h in range(N_HEADS):
        out_ref[bb, 0, h] = xt[64 * h:64 * (h + 1)]
    _clear_other_layers(out_ref, bb)


def _clear_other_layers(out_ref, bb):
    if out_ref.shape[1] > 1:
        out_ref[bb, 1:] = jnp.zeros(out_ref.shape[1:], jnp.float32)[1:]


def _mix_prompt_kernel(n_prev, *refs):
    ins, outs = refs[:8], refs[8 + n_prev:]
    log = _ShiftLog()
    _mix_prompt_pass(ins, outs, log)

    @pl.when(log.unsafe())
    def _():
        _mix_prompt_pass(ins, outs, None)


def _mix_prompt_pass(ins, outs, log):
    pa_ref, pb_ref, pc_ref, cst_ref, gsub_ref, gckv_ref, wuk_ref, wuv_ref = ins
    o_ref, nak_ref, nav_ref, dk_ref, dv_ref, ckv_ref, kpe_ref = outs
    first_pass = log is not None
    c_a = HEAD_DIM ** -0.5 * LOG2E
    c_b = DIFF_QK_DIM ** -0.5 * LOG2E
    c_c = (MLA_NOPE + MLA_ROPE) ** -0.5 * LOG2E
    lam = cst_ref[0, 0:1, 0:1]
    post = cst_ref[0, 1:2, 0:1]
    wuk, wuv = _bf(wuk_ref[0]), _bf(wuv_ref[0])
    by_head, by_comp = _group_matrix(LANES, 2), _group_matrix(LANES, 4)

    def sequence(bb, log):
        rows = pl.ds(_aligned(bb * SEQ, SEQ), SEQ)
        for j in range(N_PAIRS):
            cols = slice(LANES * j, LANES * (j + 1))
            k = _bf(pa_ref[rows, WIDTH + LANES * j:WIDTH + LANES * (j + 1)])
            v = _with_ones(_bf(pa_ref[rows, 2 * WIDTH + LANES * j:2 * WIDTH + LANES * (j + 1)]))
            qs = _stack_heads(pa_ref[rows, cols] * c_a)
            o = _attend([_dot_nt(qs, k)], [v], log, _bound(log, qs, _key_bound([k], by_head), by_head, 2))
            o_ref[rows, cols] = _unstack_heads(_normalised(o), SEQ)
        if first_pass:
            _write_heads_t(pa_ref, rows, WIDTH, nak_ref, bb)
            _write_heads_t(pa_ref, rows, 2 * WIDTH, nav_ref, bb)
        for j in range(N_PAIRS):
            cols = slice(LANES * j, LANES * (j + 1))
            k = _bf(pb_ref[rows, WIDTH + LANES * j:WIDTH + LANES * (j + 1)])
            v = _with_ones(_bf(pb_ref[rows, 2 * WIDTH + LANES * j:2 * WIDTH + LANES * (j + 1)]))
            qs = _stack_components(pb_ref[rows, cols] * c_b)
            o = _attend([_dot_nt(qs, k)], [v], log, _bound(log, qs, _key_bound([k], by_comp), by_comp, 4))
            o_ref[rows, WIDTH + LANES * j:WIDTH + LANES * (j + 1)] = _diff_finish(o, SEQ, lam, post, gsub_ref[0], by_head)
        if first_pass:
            _write_heads_t(pb_ref, rows, WIDTH, dk_ref, bb)
            _write_heads_t(pb_ref, rows, 2 * WIDTH, dv_ref, bb)
        ckv = _rms(pc_ref[rows, C_CKV:C_CKV + MLA_KV_RANK], gckv_ref[0])
        kpe_slot = pc_ref[rows, C_KPE:C_KPE + LANES]
        if first_pass:
            ckv_ref[bb, 0] = ckv
            _clear_other_layers(ckv_ref, bb)
            kpe_ref[bb, 0] = kpe_slot.T[0:MLA_ROPE]
            _clear_other_layers(kpe_ref, bb)
        ckv_b = _bf(ckv)
        kn = _bf(_dot(ckv_b, wuk))
        vv = _bf(_dot(ckv_b, wuv))
        kpe4 = _bf(_tile4(kpe_slot))
        qn = pc_ref[rows, C_QN:C_QN + WIDTH] * c_c
        qp = pc_ref[rows, C_QP:C_QP + LANES] * c_c
        for j in range(N_PAIRS):
            cols = slice(LANES * j, LANES * (j + 1))
            k = jnp.concatenate([kn[:, cols], kpe4], axis=1)
            qs = _mla_queries(qn[:, cols], qp, j)
            groups = _mla_groups(j)
            o = _attend([_dot_nt(qs, k)], [_with_ones(vv[:, cols])], log,
                        _bound(log, qs, _key_bound([k], groups), groups, 2))
            o_ref[rows, 2 * WIDTH + LANES * j:2 * WIDTH + LANES * (j + 1)] = _unstack_heads(_normalised(o), SEQ)

    _loop(PB, log, sequence)


def _mix_prompt(l, pa, pb, pc, cst, g_sub2, g_ckv, w_uk, w_uv, prev):
    n_prev = len(prev)
    tails = [(NA_HEADS, HEAD_DIM, SEQ)] * 2 + [(DIFF_HEADS, 64, SEQ)] * 2 + [(SEQ, MLA_KV_RANK), (MLA_ROPE, SEQ)]

    def cache_spec(tail):
        if l == 0:
            return pl.BlockSpec((PB, DEPTH) + tail, lambda b: (b, 0) + (0,) * len(tail))
        return pl.BlockSpec((PB, 1) + tail, lambda b: (b, l) + (0,) * len(tail))

    def rows(width):
        return pl.BlockSpec((PB * SEQ, width), lambda b: (b, 0))

    def layer(*tail):
        return pl.BlockSpec((1,) + tail, lambda b: (l,) + (0,) * len(tail))

    return pl.pallas_call(
        functools.partial(_mix_prompt_kernel, n_prev),
        grid=(BATCH // PB,),
        in_specs=[rows(SEG_A), rows(SEG_B), rows(SEG_C), layer(2, LANES), layer(1, LANES), layer(1, MLA_KV_RANK),
                  layer(MLA_KV_RANK, WIDTH), layer(MLA_KV_RANK, WIDTH)] + [pl.BlockSpec(memory_space=pl.ANY)] * n_prev,
        out_specs=[rows(O_ATT)] + [cache_spec(t) for t in tails],
        out_shape=[jax.ShapeDtypeStruct((N_PROMPT, O_ATT), jnp.float32)]
        + [jax.ShapeDtypeStruct((BATCH, DEPTH) + t, jnp.float32) for t in tails],
        input_output_aliases={8 + i: 1 + i for i in range(n_prev)},
        compiler_params=_params("parallel"),
        name="mix_prompt",
    )(pa, pb, pc, cst, g_sub2, g_ckv, w_uk, w_uv, *prev)


def _na_row_groups():
    kh = min(NA_WIN_ROWS, GRID_ROWS)
    r0s = [min(max(r - kh // 2, 0), GRID_ROWS - kh) for r in range(GRID_ROWS)]
    groups = []
    for r, r0 in enumerate(r0s):
        if groups and groups[-1][2] == r0:
            groups[-1][1] = r
        else:
            groups.append([r, r, r0])
    return kh, [tuple(g) for g in groups]


def _na_sample(q_ref, k_ref, v_ref, ck_ref, cv_ref, tt_ref, o_ref, log):
    c = HEAD_DIM ** -0.5 * LOG2E
    kh, groups = _na_row_groups()
    lk = kh * GRID_W
    edge = [g for g in groups if g[1] > g[0]]
    inner = [g for g in groups if g[1] == g[0]]
    depth = inner[0][0] - inner[0][2]
    assert all(g[0] - g[2] == depth for g in inner) and [g[0] for g in inner] == list(range(inner[0][0], inner[-1][0] + 1))

    def in_window(n):
        cq = lax.broadcasted_iota(jnp.int32, (n, lk), 0) & (GRID_W - 1)
        ck = lax.broadcasted_iota(jnp.int32, (n, lk), 1) & (GRID_W - 1)
        c0 = jnp.clip(cq - NA_WIN_COLS // 2, 0, GRID_W - NA_WIN_COLS)
        return (ck >= c0) & (ck < c0 + NA_WIN_COLS)

    kc_t = _bf(_pair_t(ck_ref))
    vc = _with_ones(_bf(_pair_t(cv_ref).T))
    by_head = _group_matrix(LANES, 2)
    k2max = _key_bound([_bf(k_ref[...]), _bf(_pair_t(ck_ref).T)], by_head)
    tmax = functools.reduce(jnp.maximum, [tt_ref[t, a] for t in range(2) for a in range(N_DROW - 1)])
    bplus = jnp.maximum(jnp.max(jnp.max(tmax, axis=-1, keepdims=True), axis=0, keepdims=True), 0.0)

    def group(row0, key0, offsets, log):
        n = len(offsets) * GRID_W
        rows, keys = pl.ds(row0, n), pl.ds(key0, lk)
        qg = _stack_heads(q_ref[rows, :] * c)
        k = _bf(k_ref[keys, :])
        v = _with_ones(_bf(v_ref[keys, :]))
        bias = jnp.concatenate([
            jnp.concatenate([tt_ref[t, 2 * i - off + NA_WIN_ROWS - 1] for i in range(kh // 2)], axis=1)
            for t in range(2) for off in offsets], axis=0)
        s_loc = jnp.where(in_window(2 * n), _dot_nt(qg, k) + bias, NEG_INF)
        o = _attend([_dot(qg, kc_t), s_loc], [vc, v], log, _bound(log, qg, k2max, by_head, 2, bplus))
        o_ref[rows, :] = _unstack_heads(_normalised(o), n)

    for (r_lo, r_hi, r0) in edge:
        group(r_lo * GRID_W, r0 * GRID_W, [r - r0 for r in range(r_lo, r_hi + 1)], log)

    def inner_row(i, log):
        r = inner[0][0] + i
        group(_aligned(r * GRID_W, GRID_W), _aligned((r - depth) * GRID_W, GRID_W), [depth], log)

    _loop(len(inner), log, inner_row)


def _diff_sample(q_ref, k_ref, v_ref, ck_ref, cv_ref, cos_ref, sin_ref, cst_ref, g_ref, o_ref, log):
    c = DIFF_QK_DIM ** -0.5 * LOG2E
    lam = cst_ref[0, 0:1, 0:1]
    post = cst_ref[0, 1:2, 0:1]
    k_new = _bf(_rope(k_ref[...], cos_ref[...], sin_ref[...]))
    kc_t = _bf(_pair_t(ck_ref))
    vc = _with_ones(_bf(_pair_t(cv_ref).T))
    v = _with_ones(_bf(v_ref[...]))
    by_head, by_comp = _group_matrix(LANES, 2), _group_matrix(LANES, 4)
    k2max = _key_bound([k_new, _bf(_pair_t(ck_ref).T)], by_comp)

    def block(qi, log):
        rows = pl.ds(_aligned(qi * QB, QB), QB)
        qs = _stack_components(_rope(q_ref[rows, :], cos_ref[rows, :], sin_ref[rows, :]) * c)
        o = _attend([_dot(qs, kc_t), _dot_nt(qs, k_new)], [vc, v], log, _bound(log, qs, k2max, by_comp, 4))
        o_ref[rows, :] = _diff_finish(o, QB, lam, post, g_ref[0], by_head)

    _loop(DEC_SEQ // QB, log, block)


def _mla_sample(j, qn_ref, qp_ref, ckv_ref, kpe_ref, cckv_ref, ckpe_ref, cosq_ref, sinq_ref, cosk_ref, sink_ref,
                gckv_ref, wuk_ref, wuv_ref, o_ref, log):
    c = (MLA_NOPE + MLA_ROPE) ** -0.5 * LOG2E
    wuk, wuv = _bf(wuk_ref[0]), _bf(wuv_ref[0])
    ckv_new = _bf(_rms(ckv_ref[...], gckv_ref[0]))
    ckv_old = _bf(cckv_ref[...])
    kpe_new = _bf(_tile4(_rope(kpe_ref[...], cosk_ref[...], sink_ref[...])))
    kpe_old = _bf(jnp.concatenate([ckpe_ref[...]] * MLA_HEADS, axis=0).T)
    k_old = jnp.concatenate([_bf(_dot(ckv_old, wuk)), kpe_old], axis=1)
    k_new = jnp.concatenate([_bf(_dot(ckv_new, wuk)), kpe_new], axis=1)
    vo, vn = _with_ones(_bf(_dot(ckv_old, wuv))), _with_ones(_bf(_dot(ckv_new, wuv)))
    groups = _mla_groups(j)
    k2max = _key_bound([k_old, k_new], groups)

    def block(qi, log):
        rows = pl.ds(_aligned(qi * QB, QB), QB)
        qp = _rope(qp_ref[rows, :], cosq_ref[rows, :], sinq_ref[rows, :]) * c
        qs = _mla_queries(qn_ref[rows, :] * c, qp, j)
        o = _attend([_dot_nt(qs, k_old), _dot_nt(qs, k_new)], [vo, vn], log, _bound(log, qs, k2max, groups, 2))
        o_ref[rows, :] = _unstack_heads(_normalised(o), QB)

    _loop(DEC_SEQ // QB, log, block)


def _mix_sample_kernel(qa_ref, ka_ref, va_ref, qb_ref, kb_ref, vb_ref, qn_ref, qp_ref, ckv_ref, kpe_ref,
                       cnak_ref, cnav_ref, cdk_ref, cdv_ref, cckv_ref, ckpe_ref, tt_ref,
                       cosb_ref, sinb_ref, cosq_ref, sinq_ref, cosk_ref, sink_ref,
                       cst_ref, gsub_ref, gckv_ref, wuk_ref, wuv_ref, oa_ref, ob_ref, oc_ref):
    j = pl.program_id(1)

    def run(log):
        _na_sample(qa_ref, ka_ref, va_ref, cnak_ref, cnav_ref, tt_ref, oa_ref, log)
        _diff_sample(qb_ref, kb_ref, vb_ref, cdk_ref, cdv_ref, cosb_ref, sinb_ref, cst_ref, gsub_ref, ob_ref, log)
        _mla_sample(j, qn_ref, qp_ref, ckv_ref, kpe_ref, cckv_ref, ckpe_ref, cosq_ref, sinq_ref, cosk_ref, sink_ref,
                    gckv_ref, wuk_ref, wuv_ref, oc_ref, log)

    log = _ShiftLog()
    run(log)

    @pl.when(log.unsafe())
    def _():
        run(None)


def _mix_sample(l, pa, pb, pc, caches_t, tt, tables, cst, g_sub2, g_ckv, w_uk, w_uv):
    first = N_PROMPT // DEC_SEQ

    def cols(block):
        return pl.BlockSpec((DEC_SEQ, LANES), lambda b, j: (first + b, block(j)))

    def cache(*tail, pair=False):
        return pl.BlockSpec((None, None) + tail, lambda b, j: (b, l, j if pair else 0) + (0,) * (len(tail) - 1))

    def layer(*tail):
        return pl.BlockSpec((1,) + tail, lambda b, j: (l,) + (0,) * len(tail))

    table = pl.BlockSpec((DEC_SEQ, LANES), lambda b, j: (0, 0), pipeline_mode=pl.Buffered(1))
    qkv = [cols(lambda j: j), cols(lambda j: N_PAIRS + j), cols(lambda j: 2 * N_PAIRS + j)]
    seg_c = [cols(lambda j: j), cols(lambda j: C_QP // LANES), cols(lambda j: C_CKV // LANES), cols(lambda j: C_KPE // LANES)]
    kv_t = cache(2, 64, PAST_LEN, pair=True)
    w_pair = pl.BlockSpec((1, MLA_KV_RANK, LANES), lambda b, j: (l, 0, j))
    out = pl.BlockSpec((DEC_SEQ, LANES), lambda b, j: (b, j))
    return pl.pallas_call(
        _mix_sample_kernel,
        grid=(DEC_BATCH, N_PAIRS),
        in_specs=qkv + qkv + seg_c + [kv_t, kv_t, kv_t, kv_t, cache(PAST_LEN, MLA_KV_RANK), cache(MLA_ROPE, PAST_LEN),
                                      pl.BlockSpec((None, 2, N_DROW - 1, GRID_W, LANES), lambda b, j: (l, j, 0, 0, 0)),
                                      table, table, table, table, table, table,
                                      layer(2, LANES), layer(1, LANES), layer(1, MLA_KV_RANK), w_pair, w_pair],
        out_specs=[out, out, out],
        out_shape=[jax.ShapeDtypeStruct((N_SAMPLE, WIDTH), jnp.float32)] * 3,
        compiler_params=_params("parallel", "arbitrary"),
        name="mix_sample",
    )(pa, pa, pa, pb, pb, pb, pc, pc, pc, pc, *caches_t, tt, *tables, cst, g_sub2, g_ckv, w_uk, w_uv)


FF_CHUNK = 1024


def _outffn_kernel(n_x, first, final, *refs):
    x_refs, op_ref, os_refs = refs[:n_x], refs[n_x], refs[n_x + 1:n_x + 4]
    (od_ref, wout_ref, g1_ref, gffn_ref, sh2_ref, sc2_ref, g2_ref, w1_ref, w2_ref, gfin_ref, y_ref) = refs[n_x + 4:]
    o_att = jnp.where(first + pl.program_id(0) < TILES_PROMPT, op_ref[...],
                      jnp.concatenate([r[...] for r in os_refs], axis=1))
    acc = (_dot(_bf(o_att), wout_ref[0, 0:O_ATT, :])
           + _dot(_bf(od_ref[...]), wout_ref[0, O_ATT:O_ATT + WIDTH, :]))
    x1 = _read_tile(x_refs, first) + g1_ref[...] * acc
    hf = _bf(_rms(x1, gffn_ref[0]) * (1.0 + sc2_ref[...]) + sh2_ref[...])
    acc = jnp.zeros((TM, D_MODEL), jnp.float32)
    for c in range(D_FF // FF_CHUNK):
        cols = slice(FF_CHUNK * c, FF_CHUNK * (c + 1))
        a = jnp.square(jnp.maximum(_dot(hf, w1_ref[0, :, cols]), 0.0))
        acc += _dot(_bf(a), w2_ref[0, cols, :])
    y = x1 + g2_ref[...] * acc
    if final:
        y = _rms(y, gfin_ref[...])
    y_ref[...] = y


def _outffn(l, xs, o_p, o_s, od, w_out, g_ffn, mod, w1, w2, g_final, first, n_tiles):
    def mod_spec(j):
        return pl.BlockSpec((None, None, 1, D_MODEL), lambda i: (l, _row_group(first + i), 0, j))

    def resident(shape):
        return pl.BlockSpec(shape, lambda i: (l,) + (0,) * (len(shape) - 1), pipeline_mode=pl.Buffered(1))

    return pl.pallas_call(
        functools.partial(_outffn_kernel, len(xs), first, l == DEPTH - 1),
        grid=(n_tiles,),
        in_specs=_x_specs(len(xs) == 2, first) + _split_specs(O_ATT, first)[:1] + _split_specs(WIDTH, first)[1:] * 3 + [
            pl.BlockSpec((TM, WIDTH), lambda i: (first + i, 0)),
            resident((1, 4 * WIDTH, D_MODEL)),
            mod_spec(2),
            pl.BlockSpec((1, 1, D_MODEL), lambda i: (l, 0, 0)),
            mod_spec(3), mod_spec(4), mod_spec(5),
            resident((1, D_MODEL, D_FF)),
            resident((1, D_FF, D_MODEL)),
            pl.BlockSpec((1, D_MODEL), lambda i: (0, 0)),
        ],
        out_specs=pl.BlockSpec((TM, D_MODEL), lambda i: (i, 0)),
        out_shape=jax.ShapeDtypeStruct((n_tiles * TM, D_MODEL), jnp.float32),
        compiler_params=_params("parallel"),
        name="outffn",
    )(*xs, o_p, *o_s, od, w_out, mod, g_ffn, mod, mod, mod, w1, w2, g_final)


def _rope32_tables():
    t = np.arange(DEC_SEQ)
    rows, cols = (t // GRID_W).astype(np.float64), (t % GRID_W).astype(np.float64)
    half = 8
    freqs = ROPE_BASE ** (-np.arange(half, dtype=np.float64) / half)
    cos, sin = [], []
    for pos in (rows, cols):
        ang = pos[:, None] * freqs[None, :]
        cos += [np.cos(ang), np.cos(ang)]
        sin += [-np.sin(ang), np.sin(ang)]
    return np.concatenate(cos, axis=1).astype(np.float32), np.concatenate(sin, axis=1).astype(np.float32)


def _rope_tables():
    c32, s32 = _rope32_tables()
    tile = lambda a, n: np.tile(a, (1, n))
    pad = np.zeros((DEC_SEQ, LANES - MLA_ROPE), np.float32)
    cos_k = np.concatenate([c32, pad + 1.0], axis=1)
    sin_k = np.concatenate([s32, pad], axis=1)
    return (tile(c32, 4), tile(s32, 4),
            cos_k, sin_k)


def kernel(x_prompt, x_sample, cache_na_k, cache_na_v, cache_diff_k, cache_diff_v, cache_mla_ckv, cache_mla_kpe, c, c_ctx, w_ada, b_ada, g_mix, g_ffn, w_in, w_out, na_rpb, diff_lq1, diff_lk1, diff_lq2, diff_lk2, diff_g_subln, mla_g_ckv, mla_w_uk, mla_w_uv, sgu_g, sgu_w, sgu_b, w_ff1, w_ff2, g_final):
    f32 = jnp.float32
    m = jnp.concatenate([c_ctx[None, :], c, jnp.zeros((N_MOD_ROWS - 1 - DEC_BATCH, D_MODEL), f32)], axis=0)
    mod = _ada(m, w_ada, b_ada).reshape(DEPTH, N_MOD_ROWS, 1, 6 * D_MODEL)
    cst = _lam_consts(diff_lq1, diff_lk1, diff_lq2, diff_lk2)
    tt = _bias_tiles(na_rpb)
    cos4, sin4, cos_k, sin_k = [jnp.asarray(t) for t in _rope_tables()]
    tables = (cos4, sin4, cos4, sin4, cos_k, sin_k)

    t_last = lambda a: jnp.swapaxes(a, -1, -2)
    w_in_t = t_last(w_in)
    caches_t = (t_last(cache_na_k), t_last(cache_na_v), t_last(cache_diff_k), t_last(cache_diff_v),
                cache_mla_ckv, t_last(cache_mla_kpe))
    w_out_b, w1_b, w2_b = _bf(w_out), _bf(w_ff1), _bf(w_ff2)
    g_mix3 = g_mix.reshape(DEPTH, 1, D_MODEL)
    g_ffn3 = g_ffn.reshape(DEPTH, 1, D_MODEL)
    g_sub2 = jnp.tile(diff_g_subln, (1, 2)).reshape(DEPTH, 1, LANES)
    g_ckv3 = mla_g_ckv.reshape(DEPTH, 1, MLA_KV_RANK)
    sgu_g3 = sgu_g.reshape(DEPTH, 1, WIDTH)
    sgu_bt = sgu_b.transpose(0, 2, 1)
    g_fin2 = g_final.reshape(1, D_MODEL)

    xs = (x_prompt.reshape(N_PROMPT, D_MODEL), x_sample.reshape(N_SAMPLE, D_MODEL))
    new = ()
    for l in range(DEPTH):
        pa, pb, pc, od = _inproj(l, xs, g_mix3, mod, w_in_t, sgu_g3, sgu_w, sgu_bt)
        o_p, *new = _mix_prompt(l, pa, pb, pc, cst, g_sub2, g_ckv3, mla_w_uk, mla_w_uv, new)
        o_s = _mix_sample(l, pa, pb, pc, caches_t, tt, tables, cst, g_sub2, g_ckv3, mla_w_uk, mla_w_uv)
        ffn = functools.partial(_outffn, l, xs, o_p, o_s, od, w_out_b, g_ffn3, mod, w1_b, w2_b, g_fin2)
        if l < DEPTH - 1:
            xs = (ffn(0, TILES_PROMPT + TILES_SAMPLE),)
        else:
            xs = (ffn(0, TILES_PROMPT), ffn(TILES_PROMPT, TILES_SAMPLE))
    y_prompt = xs[0].reshape(BATCH, SEQ, D_MODEL)
    y_sample = xs[1].reshape(DEC_BATCH, DEC_SEQ, D_MODEL)
    na_k, na_v, diff_k, diff_v, mla_ckv, mla_kpe = new
    return (y_prompt, y_sample, t_last(na_k), t_last(na_v), t_last(diff_k), t_last(diff_v), mla_ckv, t_last(mla_kpe))
```

```python
import functools
import math

import numpy as np
import jax
import jax.numpy as jnp
from jax import lax
from jax.experimental import pallas as pl
from jax.experimental.pallas import tpu as pltpu

D_MODEL = 1024
BATCH = 16
SEQ = 256
DEPTH = 4
DEC_BATCH = 2
DEC_SEQ = 1024
PAST_LEN = 512
GRID_W = 64
GRID_ROWS = DEC_SEQ // GRID_W
HEAD_DIM = 64
NA_HEADS = 4
NA_WIN_ROWS = 8
NA_WIN_COLS = 16
DIFF_HEADS = 4
DIFF_QK_DIM = 32
DIFF_V_DIM = 64
MLA_HEADS = 4
MLA_NOPE = 64
MLA_ROPE = 32
MLA_V = 64
MLA_KV_RANK = 128
SGU_GROUPS = 4
SGU_GROUP_DIM = 64
SGU_CHUNK = 128
D_FF = 4 * D_MODEL
ROPE_BASE = 10000.0
EPS = 1e-6
NEG_INF = -1e30
LOG2E = 1.4426950408889634

N_HEADS = 4
N_PAIRS = N_HEADS // 2
LANES = 128
WIDTH = 256
N_PROMPT = BATCH * SEQ
N_SAMPLE = DEC_BATCH * DEC_SEQ
N_TOK = N_PROMPT + N_SAMPLE
N_MOD_ROWS = 8

SEG_A = 3 * WIDTH
SEG_B = 3 * WIDTH
SEG_C = 640
SEG_D = 2 * WIDTH
SEG_C_PAD = 96
IN_COLS_P = SEG_A + SEG_B + SEG_C + SEG_D
O_ATT = 3 * WIDTH

TM = 512
TILES_PROMPT = N_PROMPT // TM
TILES_SAMPLE = N_SAMPLE // TM
PB = 2
QB = 256
QB_EXACT = 64
VMEM_LIMIT = 56 * 1024 * 1024


def _bf(x):
    return x.astype(jnp.bfloat16)


def _dot(a, b):
    return jnp.dot(a, b, preferred_element_type=jnp.float32)


def _dot_nt(a, b):
    return lax.dot_general(a, b, (((1,), (1,)), ((), ())), preferred_element_type=jnp.float32)


def _rms(x, g):
    ms = jnp.mean(x * x, axis=-1, keepdims=True)
    return x * lax.rsqrt(ms + EPS) * g


def _lane_range(lo, hi, width=LANES):
    lane = lax.broadcasted_iota(jnp.int32, (1, width), 1)
    return (lane >= lo) & (lane < hi)


def _with_ones(v):
    return jnp.concatenate([v, jnp.ones((v.shape[0], LANES), jnp.bfloat16)], axis=1)


def _attend(scores, values, log=None, bound=None):
    if bound is None:
        m = functools.reduce(jnp.maximum, [jnp.max(s, axis=-1, keepdims=True) for s in scores])
        shifted = [s - m for s in scores]
    else:
        over = jnp.max(bound - scores[0][:, 0:LANES], axis=0, keepdims=True)
        log.worst = jnp.maximum(log.worst, jnp.min(over, axis=1, keepdims=True))
        shifted = [s - bound for s in scores]
    return functools.reduce(lambda a, b: a + b, [_dot(_bf(jnp.exp2(s)), v) for s, v in zip(shifted, values)])


BOUND_SLACK = 1.02
OVERSHOOT_LIMIT = 100.0


class _ShiftLog:
    def __init__(self, worst=None):
        self.worst = jnp.zeros((1, 1), jnp.float32) if worst is None else worst

    def unsafe(self):
        return jnp.logical_not(self.worst[0, 0] < OVERSHOOT_LIMIT)


def _group_matrix(width, n_groups, extra=None):
    i = lax.broadcasted_iota(jnp.int32, (width, LANES), 0)
    j = lax.broadcasted_iota(jnp.int32, (width, LANES), 1)
    size = LANES // n_groups
    hit = (i // size == j // size) & (i < LANES)
    for t in range(n_groups if extra else 0):
        lo, hi = extra(t)
        hit = hit | ((i >= lo) & (i < hi) & (j // size == t))
    return jnp.where(hit, 1.0, 0.0).astype(jnp.bfloat16)


def _squares(x):
    xf = x.astype(jnp.float32)
    return _bf(xf * xf)


def _key_bound(keys, groups):
    return functools.reduce(jnp.maximum, [jnp.max(_dot(_squares(k), groups), axis=0, keepdims=True) for k in keys])


def _bound(log, qs, k2max, n_groups, extra=0.0):
    if log is None:
        return None
    n, size = qs.shape[0] // n_groups, LANES // n_groups
    kmax = jnp.concatenate([jnp.broadcast_to(jnp.sqrt(k2max[:, size * t:size * t + 1]) * BOUND_SLACK, (n, 1))
                            for t in range(n_groups)], axis=0)
    qf = qs.astype(jnp.float32)
    return jnp.sqrt(jnp.sum(qf * qf, axis=-1, keepdims=True)) * kmax + extra


def _loop(n, log, body):
    if log is not None:
        for i in range(n):
            body(i, log)
    else:
        lax.fori_loop(0, n, lambda i, carry: body(i, None) or carry, 0)


def _aligned(start, multiple):
    return start if isinstance(start, int) else pl.multiple_of(start, multiple)


def _normalised(o_ext):
    return o_ext[:, 0:LANES] * (1.0 / o_ext[:, LANES:2 * LANES])


def _swap8(x):
    lane = lax.broadcasted_iota(jnp.int32, (1, LANES), 1)
    return jnp.where((lane & 15) < 8, pltpu.roll(x, LANES - 8, 1), pltpu.roll(x, 8, 1))


def _rope(x, cos, sin):
    outs = []
    for c in range(x.shape[1] // LANES):
        sl = slice(LANES * c, LANES * (c + 1))
        xc = x[:, sl]
        outs.append(xc * cos[:, sl] + _swap8(xc) * sin[:, sl])
    return outs[0] if len(outs) == 1 else jnp.concatenate(outs, axis=1)


def _tile4(x):
    return x + pltpu.roll(x, 32, 1) + pltpu.roll(x, 64, 1) + pltpu.roll(x, 96, 1)


def _params(*sem):
    return pltpu.CompilerParams(dimension_semantics=sem, vmem_limit_bytes=VMEM_LIMIT)


ADA_TN = 1536


def _ada_kernel(m_ref, w_ref, b_ref, o_ref):
    m = m_ref[...]
    s = m * jax.nn.sigmoid(m)
    o_ref[0] = _dot(_bf(s), _bf(w_ref[0])) + b_ref[0]


def _ada(m, w_ada, b_ada):
    n = 6 * D_MODEL
    return pl.pallas_call(
        _ada_kernel,
        grid=(DEPTH, n // ADA_TN),
        in_specs=[
            pl.BlockSpec((N_MOD_ROWS, D_MODEL), lambda l, j: (0, 0)),
            pl.BlockSpec((1, D_MODEL, ADA_TN), lambda l, j: (l, 0, j)),
            pl.BlockSpec((1, 1, ADA_TN), lambda l, j: (l, 0, j)),
        ],
        out_specs=pl.BlockSpec((1, N_MOD_ROWS, ADA_TN), lambda l, j: (l, 0, j)),
        out_shape=jax.ShapeDtypeStruct((DEPTH, N_MOD_ROWS, n), jnp.float32),
        compiler_params=_params("parallel", "parallel"),
        name="ada",
    )(m, w_ada, b_ada.reshape(DEPTH, 1, n))


def _lam_kernel(lq1_ref, lk1_ref, lq2_ref, lk2_ref, init_ref, o_ref):
    init = init_ref[...]
    a = jnp.exp(jnp.sum(lq1_ref[...] * lk1_ref[...], axis=-1, keepdims=True))
    b = jnp.exp(jnp.sum(lq2_ref[...] * lk2_ref[...], axis=-1, keepdims=True))
    lam = a - b + init
    post = 1.0 - init
    for l in range(DEPTH):
        o_ref[l, 0:1, :] = jnp.broadcast_to(lam[l:l + 1], (1, LANES))
        o_ref[l, 1:2, :] = jnp.broadcast_to(post[l:l + 1], (1, LANES))


def _lam_consts(lq1, lk1, lq2, lk2):
    init = np.array([[0.8 - 0.6 * math.exp(-0.3 * l)] for l in range(DEPTH)], np.float32)
    return pl.pallas_call(
        _lam_kernel,
        out_shape=jax.ShapeDtypeStruct((DEPTH, 2, LANES), jnp.float32),
        name="diff_lambda",
    )(lq1, lk1, lq2, lk2, jnp.asarray(init))


N_DROW = 2 * NA_WIN_ROWS - 1
N_DCOL = 2 * NA_WIN_COLS - 1


def _bias_kernel(rpb_ref, o_ref):
    l = pl.program_id(0)
    h = pl.program_id(1)
    base = (l * NA_HEADS + h) * (N_DROW * N_DCOL)
    cq = lax.broadcasted_iota(jnp.int32, (GRID_W, LANES), 0)
    lane = lax.broadcasted_iota(jnp.int32, (GRID_W, LANES), 1)
    ck = lane & (GRID_W - 1)
    dcol = jnp.clip(ck - cq, -(NA_WIN_COLS - 1), NA_WIN_COLS - 1) + (NA_WIN_COLS - 1)
    hi = lane >= GRID_W
    for a in range(N_DROW - 1):
        acc = jnp.zeros((GRID_W, LANES), jnp.float32)
        for j in range(N_DCOL):
            lo_v = rpb_ref[base + a * N_DCOL + j]
            hi_v = rpb_ref[base + (a + 1) * N_DCOL + j]
            acc = jnp.where(dcol == j, jnp.where(hi, hi_v, lo_v), acc)
        o_ref[0, 0, a] = acc * LOG2E


def _bias_tiles(na_rpb):
    return pl.pallas_call(
        _bias_kernel,
        grid=(DEPTH, NA_HEADS),
        in_specs=[pl.BlockSpec(memory_space=pltpu.SMEM)],
        out_specs=pl.BlockSpec((1, 1, N_DROW - 1, GRID_W, LANES), lambda l, h: (l, h, 0, 0, 0)),
        out_shape=jax.ShapeDtypeStruct((DEPTH, NA_HEADS, N_DROW - 1, GRID_W, LANES), jnp.float32),
        compiler_params=_params("parallel", "parallel"),
        name="na_bias_tiles",
    )(na_rpb.reshape(-1))


def _row_group(i):
    return jnp.where(i < TILES_PROMPT, 0, 1 + (i - TILES_PROMPT) // (DEC_SEQ // TM))


def _split_specs(width, first):
    return [pl.BlockSpec((TM, width), lambda i: (jnp.minimum(first + i, TILES_PROMPT - 1), 0)),
            pl.BlockSpec((TM, width), lambda i: (jnp.maximum(first + i - TILES_PROMPT, 0), 0))]


def _x_specs(split, first):
    if not split:
        return [pl.BlockSpec((TM, D_MODEL), lambda i: (first + i, 0))]
    return _split_specs(D_MODEL, first)


def _read_tile(refs, first):
    if len(refs) == 1:
        return refs[0][...]
    return jnp.where(first + pl.program_id(0) < TILES_PROMPT, refs[0][...], refs[1][...])


IN_COLS = 2592
IN_QC, IN_CKV, IN_D = 1536, 1920, 2080
TR_ROWS = 256


def _gelu_tanh(x):
    return 0.5 * x * (1.0 + jnp.tanh(math.sqrt(2.0 / math.pi) * (x + 0.044715 * (x * x * x))))


def _sgu(pd, g, w_ref, bt):
    u = _gelu_tanh(pd[:, 0:WIDTH])
    v = _gelu_tanh(pd[:, WIDTH:2 * WIDTH])
    grp = lax.broadcasted_iota(jnp.int32, (1, WIDTH), 1) // SGU_GROUP_DIM
    v2 = v * v
    ms = jnp.zeros_like(v)
    for gi in range(SGU_GROUPS):
        sel = grp == gi
        tot = jnp.sum(jnp.where(sel, v2, 0.0), axis=-1, keepdims=True)
        ms = jnp.where(sel, tot * (1.0 / SGU_GROUP_DIM), ms)
    vg = _bf(v * lax.rsqrt(ms + EPS) * g)
    outs = []
    for c in range(pd.shape[0] // SGU_CHUNK):
        rows = slice(SGU_CHUNK * c, SGU_CHUNK * (c + 1))
        mixed = jnp.zeros((SGU_CHUNK, WIDTH), jnp.float32)
        for gi in range(SGU_GROUPS):
            full = _dot(_bf(w_ref[0, gi]), vg[rows]) + bt[:, gi:gi + 1]
            mixed = jnp.where(grp == gi, full, mixed)
        outs.append(u[rows] * mixed)
    return jnp.concatenate(outs, axis=0)


def _w_in_row_pieces():
    qn = [(IN_QC + 96 * h, MLA_NOPE) for h in range(MLA_HEADS)]
    qp = [(IN_QC + 96 * h + MLA_NOPE, MLA_ROPE) for h in range(MLA_HEADS)]
    seg_c = qn + qp + [(IN_CKV, MLA_KV_RANK + MLA_ROPE)]
    return (0, SEG_A + SEG_B), seg_c, (IN_D, SEG_D)


def _load_w_in(wt_ref, w_scr):
    ab, seg_c, d = _w_in_row_pieces()
    c_rows = jnp.concatenate([wt_ref[0, s:s + n, :] for s, n in seg_c]
                             + [jnp.zeros((SEG_C_PAD, D_MODEL), jnp.float32)], axis=0)
    for t in range(SEG_C // LANES):
        w_scr[:, SEG_A + SEG_B + LANES * t:SEG_A + SEG_B + LANES * (t + 1)] = _bf(c_rows[LANES * t:LANES * (t + 1)].T)
    for (src, n), dst in ((ab, 0), (d, SEG_A + SEG_B + SEG_C)):
        for t in range(n // TR_ROWS):
            rows = wt_ref[0, src + TR_ROWS * t:src + TR_ROWS * (t + 1), :]
            w_scr[:, dst + TR_ROWS * t:dst + TR_ROWS * (t + 1)] = _bf(rows.T)


def _inproj_kernel(n_x, *refs):
    x_refs = refs[:n_x]
    (g_ref, sh_ref, sc_ref, wt_ref, sg_ref, sw_ref, sbt_ref, pa_ref, pb_ref, pc_ref, od_ref, w_scr) = refs[n_x:]

    @pl.when(pl.program_id(0) == 0)
    def _():
        _load_w_in(wt_ref, w_scr)

    h = _rms(_read_tile(x_refs, 0), g_ref[0]) * (1.0 + sc_ref[...]) + sh_ref[...]
    hb = _bf(h)
    off = SEG_A + SEG_B + SEG_C
    od_ref[...] = _sgu(_dot(hb, w_scr[:, off:off + SEG_D]), sg_ref[0], sw_ref, sbt_ref[0])
    off = 0
    for ref in (pa_ref, pb_ref, pc_ref):
        n = ref.shape[1]
        ref[...] = _dot(hb, w_scr[:, off:off + n])
        off += n


def _inproj(l, xs, g_mix, mod, w_in_t, sgu_g, sgu_w, sgu_bt):
    def mod_spec(j):
        return pl.BlockSpec((None, None, 1, D_MODEL), lambda i: (l, _row_group(i), 0, j))

    widths = (SEG_A, SEG_B, SEG_C, WIDTH)
    return pl.pallas_call(
        functools.partial(_inproj_kernel, len(xs)),
        grid=(N_TOK // TM,),
        in_specs=_x_specs(len(xs) == 2, 0) + [
            pl.BlockSpec((1, 1, D_MODEL), lambda i: (l, 0, 0)),
            mod_spec(0), mod_spec(1),
            pl.BlockSpec((1, IN_COLS, D_MODEL), lambda i: (l, 0, 0), pipeline_mode=pl.Buffered(1)),
            pl.BlockSpec((1, 1, WIDTH), lambda i: (l, 0, 0)),
            pl.BlockSpec((1, SGU_GROUPS, SGU_CHUNK, SGU_CHUNK), lambda i: (l, 0, 0, 0)),
            pl.BlockSpec((1, SGU_CHUNK, SGU_GROUPS), lambda i: (l, 0, 0)),
        ],
        out_specs=[pl.BlockSpec((TM, n), lambda i: (i, 0)) for n in widths],
        out_shape=[jax.ShapeDtypeStruct((N_TOK, n), jnp.float32) for n in widths],
        scratch_shapes=[pltpu.VMEM((D_MODEL, IN_COLS_P), jnp.bfloat16)],
        compiler_params=_params("arbitrary"),
        name="inproj",
    )(*xs, g_mix, mod, mod, w_in_t, sgu_g, sgu_w, sgu_bt)


C_QN, C_QP, C_CKV, C_KPE = 0, 256, 384, 512


def _stack_heads(qp):
    lo = _lane_range(0, 64)
    return jnp.concatenate([_bf(jnp.where(lo, qp, 0.0)), _bf(jnp.where(lo, 0.0, qp))], axis=0)


def _unstack_heads(o, n):
    return jnp.where(_lane_range(0, 64), o[0:n], o[n:2 * n])


def _pair_t(c_ref):
    return jnp.concatenate([c_ref[0], c_ref[1]], axis=0)


def _stack_components(qp):
    return jnp.concatenate([_bf(jnp.where(_lane_range(32 * t, 32 * (t + 1)), qp, 0.0)) for t in range(4)], axis=0)


def _group_mean_sq(x, groups, size):
    sq = x * x
    hi = _bf(sq)
    rest = sq - hi.astype(jnp.float32)
    mid = _bf(rest)
    lo = _bf(rest - mid.astype(jnp.float32))
    return (_dot(hi, groups) + _dot(mid, groups) + _dot(lo, groups)) * (1.0 / size)


def _diff_finish(o, n, lam, post, g2, by_head):
    den = o[:, LANES:2 * LANES]
    outs = []
    for t in range(2):
        p1 = o[2 * t * n:(2 * t + 1) * n, 0:LANES] * (1.0 / den[2 * t * n:(2 * t + 1) * n])
        p2 = o[(2 * t + 1) * n:(2 * t + 2) * n, 0:LANES] * (lam / den[(2 * t + 1) * n:(2 * t + 2) * n])
        outs.append(p1 - p2)
    d = jnp.where(_lane_range(0, 64), outs[0], outs[1])
    return d * lax.rsqrt(_group_mean_sq(d, by_head, DIFF_V_DIM) + EPS) * g2 * post


def _mla_groups(j):
    return _group_matrix(2 * LANES, 2,
                         lambda t: (LANES + MLA_ROPE * (2 * j + t), LANES + MLA_ROPE * (2 * j + t + 1)))


def _mla_queries(qn_pair, qp_all, j):
    halves = []
    for t in range(2):
        h = 2 * j + t
        halves.append(jnp.concatenate([
            _bf(jnp.where(_lane_range(64 * t, 64 * (t + 1)), qn_pair, 0.0)),
            _bf(jnp.where(_lane_range(MLA_ROPE * h, MLA_ROPE * (h + 1)), qp_all, 0.0))], axis=1))
    return jnp.concatenate(halves, axis=0)


def _write_heads_t(p_ref, rows, col0, out_ref, bb):
    xt = p_ref[rows, col0:col0 + WIDTH].T
    for h in range(N_HEADS):
        out_ref[bb, 0, h] = xt[64 * h:64 * (h + 1)]
    _clear_other_layers(out_ref, bb)


def _clear_other_layers(out_ref, bb):
    if out_ref.shape[1] > 1:
        out_ref[bb, 1:] = jnp.zeros(out_ref.shape[1:], jnp.float32)[1:]


def _mix_prompt_kernel(n_prev, *refs):
    ins, outs = refs[:8], refs[8 + n_prev:]
    log = _ShiftLog()
    _mix_prompt_pass(ins, outs, log)

    @pl.when(log.unsafe())
    def _():
        _mix_prompt_pass(ins, outs, None)


def _mix_prompt_pass(ins, outs, log):
    pa_ref, pb_ref, pc_ref, cst_ref, gsub_ref, gckv_ref, wuk_ref, wuv_ref = ins
    o_ref, nak_ref, nav_ref, dk_ref, dv_ref, ckv_ref, kpe_ref = outs
    first_pass = log is not None
    c_a = HEAD_DIM ** -0.5 * LOG2E
    c_b = DIFF_QK_DIM ** -0.5 * LOG2E
    c_c = (MLA_NOPE + MLA_ROPE) ** -0.5 * LOG2E
    lam = cst_ref[0, 0:1, 0:1]
    post = cst_ref[0, 1:2, 0:1]
    wuk, wuv = _bf(wuk_ref[0]), _bf(wuv_ref[0])
    by_head, by_comp = _group_matrix(LANES, 2), _group_matrix(LANES, 4)

    def sequence(bb, log):
        rows = pl.ds(_aligned(bb * SEQ, SEQ), SEQ)
        for j in range(N_PAIRS):
            cols = slice(LANES * j, LANES * (j + 1))
            k = _bf(pa_ref[rows, WIDTH + LANES * j:WIDTH + LANES * (j + 1)])
            v = _with_ones(_bf(pa_ref[rows, 2 * WIDTH + LANES * j:2 * WIDTH + LANES * (j + 1)]))
            qs = _stack_heads(pa_ref[rows, cols] * c_a)
            o = _attend([_dot_nt(qs, k)], [v], log, _bound(log, qs, _key_bound([k], by_head), 2))
            o_ref[rows, cols] = _unstack_heads(_normalised(o), SEQ)
        if first_pass:
            _write_heads_t(pa_ref, rows, WIDTH, nak_ref, bb)
            _write_heads_t(pa_ref, rows, 2 * WIDTH, nav_ref, bb)
        for j in range(N_PAIRS):
            cols = slice(LANES * j, LANES * (j + 1))
            k = _bf(pb_ref[rows, WIDTH + LANES * j:WIDTH + LANES * (j + 1)])
            v = _with_ones(_bf(pb_ref[rows, 2 * WIDTH + LANES * j:2 * WIDTH + LANES * (j + 1)]))
            qs = _stack_components(pb_ref[rows, cols] * c_b)
            o = _attend([_dot_nt(qs, k)], [v], log, _bound(log, qs, _key_bound([k], by_comp), 4))
            o_ref[rows, WIDTH + LANES * j:WIDTH + LANES * (j + 1)] = _diff_finish(o, SEQ, lam, post, gsub_ref[0], by_head)
        if first_pass:
            _write_heads_t(pb_ref, rows, WIDTH, dk_ref, bb)
            _write_heads_t(pb_ref, rows, 2 * WIDTH, dv_ref, bb)
        ckv = _rms(pc_ref[rows, C_CKV:C_CKV + MLA_KV_RANK], gckv_ref[0])
        kpe_slot = pc_ref[rows, C_KPE:C_KPE + LANES]
        if first_pass:
            ckv_ref[bb, 0] = ckv
            _clear_other_layers(ckv_ref, bb)
            kpe_ref[bb, 0] = kpe_slot.T[0:MLA_ROPE]
            _clear_other_layers(kpe_ref, bb)
        ckv_b = _bf(ckv)
        kn = _bf(_dot(ckv_b, wuk))
        vv = _bf(_dot(ckv_b, wuv))
        kpe4 = _bf(_tile4(kpe_slot))
        qn = pc_ref[rows, C_QN:C_QN + WIDTH] * c_c
        qp = pc_ref[rows, C_QP:C_QP + LANES] * c_c
        for j in range(N_PAIRS):
            cols = slice(LANES * j, LANES * (j + 1))
            k = jnp.concatenate([kn[:, cols], kpe4], axis=1)
            qs = _mla_queries(qn[:, cols], qp, j)
            groups = _mla_groups(j)
            o = _attend([_dot_nt(qs, k)], [_with_ones(vv[:, cols])], log,
                        _bound(log, qs, _key_bound([k], groups), 2))
            o_ref[rows, 2 * WIDTH + LANES * j:2 * WIDTH + LANES * (j + 1)] = _unstack_heads(_normalised(o), SEQ)

    _loop(PB, log, sequence)


def _mix_prompt(l, pa, pb, pc, cst, g_sub2, g_ckv, w_uk, w_uv, prev):
    n_prev = len(prev)
    tails = [(NA_HEADS, HEAD_DIM, SEQ)] * 2 + [(DIFF_HEADS, 64, SEQ)] * 2 + [(SEQ, MLA_KV_RANK), (MLA_ROPE, SEQ)]

    def cache_spec(tail):
        if l == 0:
            return pl.BlockSpec((PB, DEPTH) + tail, lambda b: (b, 0) + (0,) * len(tail))
        return pl.BlockSpec((PB, 1) + tail, lambda b: (b, l) + (0,) * len(tail))

    def rows(width):
        return pl.BlockSpec((PB * SEQ, width), lambda b: (b, 0))

    def layer(*tail):
        return pl.BlockSpec((1,) + tail, lambda b: (l,) + (0,) * len(tail))

    return pl.pallas_call(
        functools.partial(_mix_prompt_kernel, n_prev),
        grid=(BATCH // PB,),
        in_specs=[rows(SEG_A), rows(SEG_B), rows(SEG_C), layer(2, LANES), layer(1, LANES), layer(1, MLA_KV_RANK),
                  layer(MLA_KV_RANK, WIDTH), layer(MLA_KV_RANK, WIDTH)] + [pl.BlockSpec(memory_space=pl.ANY)] * n_prev,
        out_specs=[rows(O_ATT)] + [cache_spec(t) for t in tails],
        out_shape=[jax.ShapeDtypeStruct((N_PROMPT, O_ATT), jnp.float32)]
        + [jax.ShapeDtypeStruct((BATCH, DEPTH) + t, jnp.float32) for t in tails],
        input_output_aliases={8 + i: 1 + i for i in range(n_prev)},
        compiler_params=_params("parallel"),
        name="mix_prompt",
    )(pa, pb, pc, cst, g_sub2, g_ckv, w_uk, w_uv, *prev)


def _na_row_groups():
    kh = min(NA_WIN_ROWS, GRID_ROWS)
    r0s = [min(max(r - kh // 2, 0), GRID_ROWS - kh) for r in range(GRID_ROWS)]
    groups = []
    for r, r0 in enumerate(r0s):
        if groups and groups[-1][2] == r0:
            groups[-1][1] = r
        else:
            groups.append([r, r, r0])
    return kh, [tuple(g) for g in groups]


def _na_sample(q_ref, k_ref, v_ref, ck_ref, cv_ref, tt_ref, o_ref, log):
    c = HEAD_DIM ** -0.5 * LOG2E
    kh, groups = _na_row_groups()
    lk = kh * GRID_W
    edge = [g for g in groups if g[1] > g[0]]
    inner = [g for g in groups if g[1] == g[0]]
    depth = inner[0][0] - inner[0][2]
    assert all(g[0] - g[2] == depth for g in inner) and [g[0] for g in inner] == list(range(inner[0][0], inner[-1][0] + 1))

    def in_window(n):
        cq = lax.broadcasted_iota(jnp.int32, (n, lk), 0) & (GRID_W - 1)
        ck = lax.broadcasted_iota(jnp.int32, (n, lk), 1) & (GRID_W - 1)
        c0 = jnp.clip(cq - NA_WIN_COLS // 2, 0, GRID_W - NA_WIN_COLS)
        return (ck >= c0) & (ck < c0 + NA_WIN_COLS)

    kc_t = _bf(_pair_t(ck_ref))
    vc = _with_ones(_bf(_pair_t(cv_ref).T))
    by_head = _group_matrix(LANES, 2)
    k2max = _key_bound([_bf(k_ref[...]), _bf(_pair_t(ck_ref).T)], by_head)
    tmax = functools.reduce(jnp.maximum, [tt_ref[t, a] for t in range(2) for a in range(N_DROW - 1)])
    bplus = jnp.maximum(jnp.max(jnp.max(tmax, axis=-1, keepdims=True), axis=0, keepdims=True), 0.0)

    def group(row0, key0, offsets, log):
        n = len(offsets) * GRID_W
        rows, keys = pl.ds(row0, n), pl.ds(key0, lk)
        qg = _stack_heads(q_ref[rows, :] * c)
        k = _bf(k_ref[keys, :])
        v = _with_ones(_bf(v_ref[keys, :]))
        bias = jnp.concatenate([
            jnp.concatenate([tt_ref[t, 2 * i - off + NA_WIN_ROWS - 1] for i in range(kh // 2)], axis=1)
            for t in range(2) for off in offsets], axis=0)
        s_loc = jnp.where(in_window(2 * n), _dot_nt(qg, k) + bias, NEG_INF)
        o = _attend([_dot(qg, kc_t), s_loc], [vc, v], log, _bound(log, qg, k2max, 2, bplus))
        o_ref[rows, :] = _unstack_heads(_normalised(o), n)

    if log is None:
        def any_row(r, log):
            r0 = jnp.clip(r - kh // 2, 0, GRID_ROWS - kh)
            group(_aligned(r * GRID_W, GRID_W), _aligned(r0 * GRID_W, GRID_W), [r - r0], log)

        _loop(GRID_ROWS, log, any_row)
        return

    for (r_lo, r_hi, r0) in edge:
        group(r_lo * GRID_W, r0 * GRID_W, [r - r0 for r in range(r_lo, r_hi + 1)], log)

    def inner_row(i, log):
        r = inner[0][0] + i
        group(r * GRID_W, (r - depth) * GRID_W, [depth], log)

    _loop(len(inner), log, inner_row)


def _diff_sample(q_ref, k_ref, v_ref, ck_ref, cv_ref, cos_ref, sin_ref, cst_ref, g_ref, o_ref, log):
    c = DIFF_QK_DIM ** -0.5 * LOG2E
    lam = cst_ref[0, 0:1, 0:1]
    post = cst_ref[0, 1:2, 0:1]
    k_new = _bf(_rope(k_ref[...], cos_ref[...], sin_ref[...]))
    kc_t = _bf(_pair_t(ck_ref))
    vc = _with_ones(_bf(_pair_t(cv_ref).T))
    v = _with_ones(_bf(v_ref[...]))
    by_head, by_comp = _group_matrix(LANES, 2), _group_matrix(LANES, 4)
    k2max = _key_bound([k_new, _bf(_pair_t(ck_ref).T)], by_comp)

    qb = QB_EXACT if log is None else QB

    def block(qi, log):
        rows = pl.ds(_aligned(qi * qb, qb), qb)
        qs = _stack_components(_rope(q_ref[rows, :], cos_ref[rows, :], sin_ref[rows, :]) * c)
        o = _attend([_dot(qs, kc_t), _dot_nt(qs, k_new)], [vc, v], log, _bound(log, qs, k2max, 4))
        o_ref[rows, :] = _diff_finish(o, qb, lam, post, g_ref[0], by_head)

    _loop(DEC_SEQ // qb, log, block)


def _mla_sample(j, qn_ref, qp_ref, ckv_ref, kpe_ref, cckv_ref, ckpe_ref, cosq_ref, sinq_ref, cosk_ref, sink_ref,
                gckv_ref, wuk_ref, wuv_ref, o_ref, log):
    c = (MLA_NOPE + MLA_ROPE) ** -0.5 * LOG2E
    wuk, wuv = _bf(wuk_ref[0]), _bf(wuv_ref[0])
    ckv_new = _bf(_rms(ckv_ref[...], gckv_ref[0]))
    ckv_old = _bf(cckv_ref[...])
    kpe_new = _bf(_tile4(_rope(kpe_ref[...], cosk_ref[...], sink_ref[...])))
    kpe_old = _bf(jnp.concatenate([ckpe_ref[...]] * MLA_HEADS, axis=0).T)
    k_old = jnp.concatenate([_bf(_dot(ckv_old, wuk)), kpe_old], axis=1)
    k_new = jnp.concatenate([_bf(_dot(ckv_new, wuk)), kpe_new], axis=1)
    vo, vn = _with_ones(_bf(_dot(ckv_old, wuv))), _with_ones(_bf(_dot(ckv_new, wuv)))
    groups = _mla_groups(j)
    k2max = _key_bound([k_old, k_new], groups)

    qb = QB_EXACT if log is None else QB

    def block(qi, log):
        rows = pl.ds(_aligned(qi * qb, qb), qb)
        qp = _rope(qp_ref[rows, :], cosq_ref[rows, :], sinq_ref[rows, :]) * c
        qs = _mla_queries(qn_ref[rows, :] * c, qp, j)
        o = _attend([_dot_nt(qs, k_old), _dot_nt(qs, k_new)], [vo, vn], log, _bound(log, qs, k2max, 2))
        o_ref[rows, :] = _unstack_heads(_normalised(o), qb)

    _loop(DEC_SEQ // qb, log, block)


def _mix_sample_kernel(qa_ref, ka_ref, va_ref, qb_ref, kb_ref, vb_ref, qn_ref, qp_ref, ckv_ref, kpe_ref,
                       cnak_ref, cnav_ref, cdk_ref, cdv_ref, cckv_ref, ckpe_ref, tt_ref,
                       cosb_ref, sinb_ref, cosq_ref, sinq_ref, cosk_ref, sink_ref,
                       cst_ref, gsub_ref, gckv_ref, wuk_ref, wuv_ref, oa_ref, ob_ref, oc_ref):
    j = pl.program_id(1)

    def run(log):
        _na_sample(qa_ref, ka_ref, va_ref, cnak_ref, cnav_ref, tt_ref, oa_ref, log)
        _diff_sample(qb_ref, kb_ref, vb_ref, cdk_ref, cdv_ref, cosb_ref, sinb_ref, cst_ref, gsub_ref, ob_ref, log)
        _mla_sample(j, qn_ref, qp_ref, ckv_ref, kpe_ref, cckv_ref, ckpe_ref, cosq_ref, sinq_ref, cosk_ref, sink_ref,
                    gckv_ref, wuk_ref, wuv_ref, oc_ref, log)

    log = _ShiftLog()
    run(log)

    @pl.when(log.unsafe())
    def _():
        run(None)


def _mix_sample(l, pa, pb, pc, caches_t, tt, tables, cst, g_sub2, g_ckv, w_uk, w_uv):
    first = N_PROMPT // DEC_SEQ

    def cols(block):
        return pl.BlockSpec((DEC_SEQ, LANES), lambda b, j: (first + b, block(j)))

    def cache(*tail, pair=False):
        return pl.BlockSpec((None, None) + tail, lambda b, j: (b, l, j if pair else 0) + (0,) * (len(tail) - 1))

    def layer(*tail):
        return pl.BlockSpec((1,) + tail, lambda b, j: (l,) + (0,) * len(tail))

    table = pl.BlockSpec((DEC_SEQ, LANES), lambda b, j: (0, 0), pipeline_mode=pl.Buffered(1))
    qkv = [cols(lambda j: j), cols(lambda j: N_PAIRS + j), cols(lambda j: 2 * N_PAIRS + j)]
    seg_c = [cols(lambda j: j), cols(lambda j: C_QP // LANES), cols(lambda j: C_CKV // LANES), cols(lambda j: C_KPE // LANES)]
    kv_t = cache(2, 64, PAST_LEN, pair=True)
    w_pair = pl.BlockSpec((1, MLA_KV_RANK, LANES), lambda b, j: (l, 0, j))
    out = pl.BlockSpec((DEC_SEQ, LANES), lambda b, j: (b, j))
    return pl.pallas_call(
        _mix_sample_kernel,
        grid=(DEC_BATCH, N_PAIRS),
        in_specs=qkv + qkv + seg_c + [kv_t, kv_t, kv_t, kv_t, cache(PAST_LEN, MLA_KV_RANK), cache(MLA_ROPE, PAST_LEN),
                                      pl.BlockSpec((None, 2, N_DROW - 1, GRID_W, LANES), lambda b, j: (l, j, 0, 0, 0)),
                                      table, table, table, table, table, table,
                                      layer(2, LANES), layer(1, LANES), layer(1, MLA_KV_RANK), w_pair, w_pair],
        out_specs=[out, out, out],
        out_shape=[jax.ShapeDtypeStruct((N_SAMPLE, WIDTH), jnp.float32)] * 3,
        compiler_params=_params("parallel", "arbitrary"),
        name="mix_sample",
    )(pa, pa, pa, pb, pb, pb, pc, pc, pc, pc, *caches_t, tt, *tables, cst, g_sub2, g_ckv, w_uk, w_uv)


FF_CHUNK = 1024


def _outffn_kernel(n_x, first, final, *refs):
    x_refs, op_ref, os_refs = refs[:n_x], refs[n_x], refs[n_x + 1:n_x + 4]
    (od_ref, wout_ref, g1_ref, gffn_ref, sh2_ref, sc2_ref, g2_ref, w1_ref, w2_ref, gfin_ref, y_ref) = refs[n_x + 4:]
    o_att = jnp.where(first + pl.program_id(0) < TILES_PROMPT, op_ref[...],
                      jnp.concatenate([r[...] for r in os_refs], axis=1))
    acc = (_dot(_bf(o_att), wout_ref[0, 0:O_ATT, :])
           + _dot(_bf(od_ref[...]), wout_ref[0, O_ATT:O_ATT + WIDTH, :]))
    x1 = _read_tile(x_refs, first) + g1_ref[...] * acc
    hf = _bf(_rms(x1, gffn_ref[0]) * (1.0 + sc2_ref[...]) + sh2_ref[...])
    acc = jnp.zeros((TM, D_MODEL), jnp.float32)
    for c in range(D_FF // FF_CHUNK):
        cols = slice(FF_CHUNK * c, FF_CHUNK * (c + 1))
        a = jnp.square(jnp.maximum(_dot(hf, w1_ref[0, :, cols]), 0.0))
        acc += _dot(_bf(a), w2_ref[0, cols, :])
    y = x1 + g2_ref[...] * acc
    if final:
        y = _rms(y, gfin_ref[...])
    y_ref[...] = y


def _outffn(l, xs, o_p, o_s, od, w_out, g_ffn, mod, w1, w2, g_final, first, n_tiles):
    def mod_spec(j):
        return pl.BlockSpec((None, None, 1, D_MODEL), lambda i: (l, _row_group(first + i), 0, j))

    def resident(shape):
        return pl.BlockSpec(shape, lambda i: (l,) + (0,) * (len(shape) - 1), pipeline_mode=pl.Buffered(1))

    return pl.pallas_call(
        functools.partial(_outffn_kernel, len(xs), first, l == DEPTH - 1),
        grid=(n_tiles,),
        in_specs=_x_specs(len(xs) == 2, first) + _split_specs(O_ATT, first)[:1] + _split_specs(WIDTH, first)[1:] * 3 + [
            pl.BlockSpec((TM, WIDTH), lambda i: (first + i, 0)),
            resident((1, 4 * WIDTH, D_MODEL)),
            mod_spec(2),
            pl.BlockSpec((1, 1, D_MODEL), lambda i: (l, 0, 0)),
            mod_spec(3), mod_spec(4), mod_spec(5),
            resident((1, D_MODEL, D_FF)),
            resident((1, D_FF, D_MODEL)),
            pl.BlockSpec((1, D_MODEL), lambda i: (0, 0)),
        ],
        out_specs=pl.BlockSpec((TM, D_MODEL), lambda i: (i, 0)),
        out_shape=jax.ShapeDtypeStruct((n_tiles * TM, D_MODEL), jnp.float32),
        compiler_params=_params("parallel"),
        name="outffn",
    )(*xs, o_p, *o_s, od, w_out, mod, g_ffn, mod, mod, mod, w1, w2, g_final)


def _rope32_tables():
    t = np.arange(DEC_SEQ)
    rows, cols = (t // GRID_W).astype(np.float64), (t % GRID_W).astype(np.float64)
    half = 8
    freqs = ROPE_BASE ** (-np.arange(half, dtype=np.float64) / half)
    cos, sin = [], []
    for pos in (rows, cols):
        ang = pos[:, None] * freqs[None, :]
        cos += [np.cos(ang), np.cos(ang)]
        sin += [-np.sin(ang), np.sin(ang)]
    return np.concatenate(cos, axis=1).astype(np.float32), np.concatenate(sin, axis=1).astype(np.float32)


def _rope_tables():
    c32, s32 = _rope32_tables()
    tile = lambda a, n: np.tile(a, (1, n))
    pad = np.zeros((DEC_SEQ, LANES - MLA_ROPE), np.float32)
    cos_k = np.concatenate([c32, pad + 1.0], axis=1)
    sin_k = np.concatenate([s32, pad], axis=1)
    return (tile(c32, 4), tile(s32, 4),
            cos_k, sin_k)


def kernel(x_prompt, x_sample, cache_na_k, cache_na_v, cache_diff_k, cache_diff_v, cache_mla_ckv, cache_mla_kpe, c, c_ctx, w_ada, b_ada, g_mix, g_ffn, w_in, w_out, na_rpb, diff_lq1, diff_lk1, diff_lq2, diff_lk2, diff_g_subln, mla_g_ckv, mla_w_uk, mla_w_uv, sgu_g, sgu_w, sgu_b, w_ff1, w_ff2, g_final):
    f32 = jnp.float32
    m = jnp.concatenate([c_ctx[None, :], c, jnp.zeros((N_MOD_ROWS - 1 - DEC_BATCH, D_MODEL), f32)], axis=0)
    mod = _ada(m, w_ada, b_ada).reshape(DEPTH, N_MOD_ROWS, 1, 6 * D_MODEL)
    cst = _lam_consts(diff_lq1, diff_lk1, diff_lq2, diff_lk2)
    tt = _bias_tiles(na_rpb)
    cos4, sin4, cos_k, sin_k = [jnp.asarray(t) for t in _rope_tables()]
    tables = (cos4, sin4, cos4, sin4, cos_k, sin_k)

    t_last = lambda a: jnp.swapaxes(a, -1, -2)
    w_in_t = t_last(w_in)
    caches_t = (t_last(cache_na_k), t_last(cache_na_v), t_last(cache_diff_k), t_last(cache_diff_v),
                cache_mla_ckv, t_last(cache_mla_kpe))
    w_out_b, w1_b, w2_b = _bf(w_out), _bf(w_ff1), _bf(w_ff2)
    g_mix3 = g_mix.reshape(DEPTH, 1, D_MODEL)
    g_ffn3 = g_ffn.reshape(DEPTH, 1, D_MODEL)
    g_sub2 = jnp.tile(diff_g_subln, (1, 2)).reshape(DEPTH, 1, LANES)
    g_ckv3 = mla_g_ckv.reshape(DEPTH, 1, MLA_KV_RANK)
    sgu_g3 = sgu_g.reshape(DEPTH, 1, WIDTH)
    sgu_bt = sgu_b.transpose(0, 2, 1)
    g_fin2 = g_final.reshape(1, D_MODEL)

    xs = (x_prompt.reshape(N_PROMPT, D_MODEL), x_sample.reshape(N_SAMPLE, D_MODEL))
    new = ()
    for l in range(DEPTH):
        pa, pb, pc, od = _inproj(l, xs, g_mix3, mod, w_in_t, sgu_g3, sgu_w, sgu_bt)
        o_p, *new = _mix_prompt(l, pa, pb, pc, cst, g_sub2, g_ckv3, mla_w_uk, mla_w_uv, new)
        o_s = _mix_sample(l, pa, pb, pc, caches_t, tt, tables, cst, g_sub2, g_ckv3, mla_w_uk, mla_w_uv)
        ffn = functools.partial(_outffn, l, xs, o_p, o_s, od, w_out_b, g_ffn3, mod, w1_b, w2_b, g_fin2)
        if l < DEPTH - 1:
            xs = (ffn(0, TILES_PROMPT + TILES_SAMPLE),)
        else:
            xs = (ffn(0, TILES_PROMPT), ffn(TILES_PROMPT, TILES_SAMPLE))
    y_prompt = xs[0].reshape(BATCH, SEQ, D_MODEL)
    y_sample = xs[1].reshape(DEC_BATCH, DEC_SEQ, D_MODEL)
    na_k, na_v, diff_k, diff_v, mla_ckv, mla_kpe = new
    return (y_prompt, y_sample, t_last(na_k), t_last(na_v), t_last(diff_k), t_last(diff_v), mla_ckv, t_last(mla_kpe))
```

```python
import functools
import math

import numpy as np
import jax
import jax.numpy as jnp
from jax import lax
from jax.experimental import pallas as pl
from jax.experimental.pallas import tpu as pltpu

D_MODEL = 1024
BATCH = 16
SEQ = 256
DEPTH = 4
DEC_BATCH = 2
DEC_SEQ = 1024
PAST_LEN = 512
GRID_W = 64
GRID_ROWS = DEC_SEQ // GRID_W
HEAD_DIM = 64
NA_HEADS = 4
NA_WIN_ROWS = 8
NA_WIN_COLS = 16
DIFF_HEADS = 4
DIFF_QK_DIM = 32
DIFF_V_DIM = 64
MLA_HEADS = 4
MLA_NOPE = 64
MLA_ROPE = 32
MLA_V = 64
MLA_KV_RANK = 128
SGU_GROUPS = 4
SGU_GROUP_DIM = 64
SGU_CHUNK = 128
D_FF = 4 * D_MODEL
ROPE_BASE = 10000.0
EPS = 1e-6
NEG_INF = -1e30
LOG2E = 1.4426950408889634

N_HEADS = 4
N_PAIRS = N_HEADS // 2
LANES = 128
WIDTH = 256
N_PROMPT = BATCH * SEQ
N_SAMPLE = DEC_BATCH * DEC_SEQ
N_TOK = N_PROMPT + N_SAMPLE
N_MOD_ROWS = 8

SEG_A = 3 * WIDTH
SEG_B = 3 * WIDTH
SEG_C = 640
SEG_D = 2 * WIDTH
SEG_C_PAD = 96
IN_COLS_P = SEG_A + SEG_B + SEG_C + SEG_D
O_ATT = 3 * WIDTH

TM = 512
TILES_PROMPT = N_PROMPT // TM
TILES_SAMPLE = N_SAMPLE // TM
PB = 2
QB = 256
QB_EXACT = 64
VMEM_LIMIT = 56 * 1024 * 1024


def _bf(x):
    return x.astype(jnp.bfloat16)


def _dot(a, b):
    return jnp.dot(a, b, preferred_element_type=jnp.float32)


def _dot_nt(a, b):
    return lax.dot_general(a, b, (((1,), (1,)), ((), ())), preferred_element_type=jnp.float32)


def _rms(x, g):
    ms = jnp.mean(x * x, axis=-1, keepdims=True)
    return x * lax.rsqrt(ms + EPS) * g


def _lane_range(lo, hi, width=LANES):
    lane = lax.broadcasted_iota(jnp.int32, (1, width), 1)
    return (lane >= lo) & (lane < hi)


def _with_ones(v):
    return jnp.concatenate([v, jnp.ones((v.shape[0], LANES), jnp.bfloat16)], axis=1)


def _attend(scores, values, log=None, bound=None):
    if bound is None:
        m = functools.reduce(jnp.maximum, [jnp.max(s, axis=-1, keepdims=True) for s in scores])
        shifted = [s - m for s in scores]
    else:
        over = jnp.max(bound - scores[0][:, 0:LANES], axis=0, keepdims=True)
        log.worst = jnp.maximum(log.worst, jnp.min(over, axis=1, keepdims=True))
        shifted = [s - bound for s in scores]
    return functools.reduce(lambda a, b: a + b, [_dot(_bf(jnp.exp2(s)), v) for s, v in zip(shifted, values)])


BOUND_SLACK = 1.02
OVERSHOOT_LIMIT = 100.0


class _ShiftLog:
    def __init__(self, worst=None):
        self.worst = jnp.zeros((1, 1), jnp.float32) if worst is None else worst

    def unsafe(self):
        return jnp.logical_not(self.worst[0, 0] < OVERSHOOT_LIMIT)


def _group_matrix(width, n_groups, extra=None):
    i = lax.broadcasted_iota(jnp.int32, (width, LANES), 0)
    j = lax.broadcasted_iota(jnp.int32, (width, LANES), 1)
    size = LANES // n_groups
    hit = (i // size == j // size) & (i < LANES)
    for t in range(n_groups if extra else 0):
        lo, hi = extra(t)
        hit = hit | ((i >= lo) & (i < hi) & (j // size == t))
    return jnp.where(hit, 1.0, 0.0).astype(jnp.bfloat16)


def _squares(x):
    xf = x.astype(jnp.float32)
    return _bf(xf * xf)


def _key_bound(keys, groups):
    return functools.reduce(jnp.maximum, [jnp.max(_dot(_squares(k), groups), axis=0, keepdims=True) for k in keys])


def _bound(log, qs, k2max, n_groups, extra=0.0):
    if log is None:
        return None
    n, size = qs.shape[0] // n_groups, LANES // n_groups
    kmax = jnp.concatenate([jnp.broadcast_to(jnp.sqrt(k2max[:, size * t:size * t + 1]) * BOUND_SLACK, (n, 1))
                            for t in range(n_groups)], axis=0)
    qf = qs.astype(jnp.float32)
    return jnp.sqrt(jnp.sum(qf * qf, axis=-1, keepdims=True)) * kmax + extra


def _loop(n, log, body):
    if log is not None:
        for i in range(n):
            body(i, log)
    else:
        lax.fori_loop(0, n, lambda i, carry: body(i, None) or carry, 0)


def _aligned(start, multiple):
    return start if isinstance(start, int) else pl.multiple_of(start, multiple)


def _normalised(o_ext):
    return o_ext[:, 0:LANES] * (1.0 / o_ext[:, LANES:2 * LANES])


def _swap8(x):
    lane = lax.broadcasted_iota(jnp.int32, (1, LANES), 1)
    return jnp.where((lane & 15) < 8, pltpu.roll(x, LANES - 8, 1), pltpu.roll(x, 8, 1))


def _rope(x, cos, sin):
    outs = []
    for c in range(x.shape[1] // LANES):
        sl = slice(LANES * c, LANES * (c + 1))
        xc = x[:, sl]
        outs.append(xc * cos[:, sl] + _swap8(xc) * sin[:, sl])
    return outs[0] if len(outs) == 1 else jnp.concatenate(outs, axis=1)


def _tile4(x):
    return x + pltpu.roll(x, 32, 1) + pltpu.roll(x, 64, 1) + pltpu.roll(x, 96, 1)


def _params(*sem):
    return pltpu.CompilerParams(dimension_semantics=sem, vmem_limit_bytes=VMEM_LIMIT)


ADA_TN = 1536


def _ada_kernel(m_ref, w_ref, b_ref, o_ref):
    m = m_ref[...]
    s = m * jax.nn.sigmoid(m)
    o_ref[0] = _dot(_bf(s), _bf(w_ref[0])) + b_ref[0]


def _ada(m, w_ada, b_ada):
    n = 6 * D_MODEL
    return pl.pallas_call(
        _ada_kernel,
        grid=(DEPTH, n // ADA_TN),
        in_specs=[
            pl.BlockSpec((N_MOD_ROWS, D_MODEL), lambda l, j: (0, 0)),
            pl.BlockSpec((1, D_MODEL, ADA_TN), lambda l, j: (l, 0, j)),
            pl.BlockSpec((1, 1, ADA_TN), lambda l, j: (l, 0, j)),
        ],
        out_specs=pl.BlockSpec((1, N_MOD_ROWS, ADA_TN), lambda l, j: (l, 0, j)),
        out_shape=jax.ShapeDtypeStruct((DEPTH, N_MOD_ROWS, n), jnp.float32),
        compiler_params=_params("parallel", "parallel"),
        name="ada",
    )(m, w_ada, b_ada.reshape(DEPTH, 1, n))


def _lam_kernel(lq1_ref, lk1_ref, lq2_ref, lk2_ref, init_ref, o_ref):
    init = init_ref[...]
    a = jnp.exp(jnp.sum(lq1_ref[...] * lk1_ref[...], axis=-1, keepdims=True))
    b = jnp.exp(jnp.sum(lq2_ref[...] * lk2_ref[...], axis=-1, keepdims=True))
    lam = a - b + init
    post = 1.0 - init
    for l in range(DEPTH):
        o_ref[l, 0:1, :] = jnp.broadcast_to(lam[l:l + 1], (1, LANES))
        o_ref[l, 1:2, :] = jnp.broadcast_to(post[l:l + 1], (1, LANES))


def _lam_consts(lq1, lk1, lq2, lk2):
    init = np.array([[0.8 - 0.6 * math.exp(-0.3 * l)] for l in range(DEPTH)], np.float32)
    return pl.pallas_call(
        _lam_kernel,
        out_shape=jax.ShapeDtypeStruct((DEPTH, 2, LANES), jnp.float32),
        name="diff_lambda",
    )(lq1, lk1, lq2, lk2, jnp.asarray(init))


N_DROW = 2 * NA_WIN_ROWS - 1
N_DCOL = 2 * NA_WIN_COLS - 1


def _bias_kernel(rpb_ref, o_ref):
    l = pl.program_id(0)
    h = pl.program_id(1)
    base = (l * NA_HEADS + h) * (N_DROW * N_DCOL)
    cq = lax.broadcasted_iota(jnp.int32, (GRID_W, LANES), 0)
    lane = lax.broadcasted_iota(jnp.int32, (GRID_W, LANES), 1)
    ck = lane & (GRID_W - 1)
    dcol = jnp.clip(ck - cq, -(NA_WIN_COLS - 1), NA_WIN_COLS - 1) + (NA_WIN_COLS - 1)
    hi = lane >= GRID_W
    for a in range(N_DROW - 1):
        acc = jnp.zeros((GRID_W, LANES), jnp.float32)
        for j in range(N_DCOL):
            lo_v = rpb_ref[base + a * N_DCOL + j]
            hi_v = rpb_ref[base + (a + 1) * N_DCOL + j]
            acc = jnp.where(dcol == j, jnp.where(hi, hi_v, lo_v), acc)
        o_ref[0, 0, a] = acc * LOG2E


def _bias_tiles(na_rpb):
    return pl.pallas_call(
        _bias_kernel,
        grid=(DEPTH, NA_HEADS),
        in_specs=[pl.BlockSpec(memory_space=pltpu.SMEM)],
        out_specs=pl.BlockSpec((1, 1, N_DROW - 1, GRID_W, LANES), lambda l, h: (l, h, 0, 0, 0)),
        out_shape=jax.ShapeDtypeStruct((DEPTH, NA_HEADS, N_DROW - 1, GRID_W, LANES), jnp.float32),
        compiler_params=_params("parallel", "parallel"),
        name="na_bias_tiles",
    )(na_rpb.reshape(-1))


def _row_group(i):
    return jnp.where(i < TILES_PROMPT, 0, 1 + (i - TILES_PROMPT) // (DEC_SEQ // TM))


def _split_specs(width, first):
    return [pl.BlockSpec((TM, width), lambda i: (jnp.minimum(first + i, TILES_PROMPT - 1), 0)),
            pl.BlockSpec((TM, width), lambda i: (jnp.maximum(first + i - TILES_PROMPT, 0), 0))]


def _x_specs(split, first):
    if not split:
        return [pl.BlockSpec((TM, D_MODEL), lambda i: (first + i, 0))]
    return _split_specs(D_MODEL, first)


def _read_tile(refs, first):
    if len(refs) == 1:
        return refs[0][...]
    return jnp.where(first + pl.program_id(0) < TILES_PROMPT, refs[0][...], refs[1][...])


IN_COLS = 2592
IN_QC, IN_CKV, IN_D = 1536, 1920, 2080
TR_ROWS = 256


def _gelu_tanh(x):
    return 0.5 * x * (1.0 + jnp.tanh(math.sqrt(2.0 / math.pi) * (x + 0.044715 * (x * x * x))))


def _sgu(pd, g, w_ref, bt):
    u = _gelu_tanh(pd[:, 0:WIDTH])
    v = _gelu_tanh(pd[:, WIDTH:2 * WIDTH])
    grp = lax.broadcasted_iota(jnp.int32, (1, WIDTH), 1) // SGU_GROUP_DIM
    v2 = v * v
    ms = jnp.zeros_like(v)
    for gi in range(SGU_GROUPS):
        sel = grp == gi
        tot = jnp.sum(jnp.where(sel, v2, 0.0), axis=-1, keepdims=True)
        ms = jnp.where(sel, tot * (1.0 / SGU_GROUP_DIM), ms)
    vg = _bf(v * lax.rsqrt(ms + EPS) * g)
    outs = []
    for c in range(pd.shape[0] // SGU_CHUNK):
        rows = slice(SGU_CHUNK * c, SGU_CHUNK * (c + 1))
        mixed = jnp.zeros((SGU_CHUNK, WIDTH), jnp.float32)
        for gi in range(SGU_GROUPS):
            full = _dot(_bf(w_ref[0, gi]), vg[rows]) + bt[:, gi:gi + 1]
            mixed = jnp.where(grp == gi, full, mixed)
        outs.append(u[rows] * mixed)
    return jnp.concatenate(outs, axis=0)


def _w_in_row_pieces():
    qn = [(IN_QC + 96 * h, MLA_NOPE) for h in range(MLA_HEADS)]
    qp = [(IN_QC + 96 * h + MLA_NOPE, MLA_ROPE) for h in range(MLA_HEADS)]
    seg_c = qn + qp + [(IN_CKV, MLA_KV_RANK + MLA_ROPE)]
    return (0, SEG_A + SEG_B), seg_c, (IN_D, SEG_D)


def _load_w_in(wt_ref, w_scr):
    ab, seg_c, d = _w_in_row_pieces()
    c_rows = jnp.concatenate([wt_ref[0, s:s + n, :] for s, n in seg_c]
                             + [jnp.zeros((SEG_C_PAD, D_MODEL), jnp.float32)], axis=0)
    for t in range(SEG_C // LANES):
        w_scr[:, SEG_A + SEG_B + LANES * t:SEG_A + SEG_B + LANES * (t + 1)] = _bf(c_rows[LANES * t:LANES * (t + 1)].T)
    for (src, n), dst in ((ab, 0), (d, SEG_A + SEG_B + SEG_C)):
        for t in range(n // TR_ROWS):
            rows = wt_ref[0, src + TR_ROWS * t:src + TR_ROWS * (t + 1), :]
            w_scr[:, dst + TR_ROWS * t:dst + TR_ROWS * (t + 1)] = _bf(rows.T)


def _inproj_kernel(n_x, *refs):
    x_refs = refs[:n_x]
    (g_ref, sh_ref, sc_ref, wt_ref, sg_ref, sw_ref, sbt_ref, pa_ref, pb_ref, pc_ref, od_ref, w_scr) = refs[n_x:]

    @pl.when(pl.program_id(0) == 0)
    def _():
        _load_w_in(wt_ref, w_scr)

    h = _rms(_read_tile(x_refs, 0), g_ref[0]) * (1.0 + sc_ref[...]) + sh_ref[...]
    hb = _bf(h)
    off = SEG_A + SEG_B + SEG_C
    od_ref[...] = _sgu(_dot(hb, w_scr[:, off:off + SEG_D]), sg_ref[0], sw_ref, sbt_ref[0])
    off = 0
    for ref in (pa_ref, pb_ref, pc_ref):
        n = ref.shape[1]
        ref[...] = _dot(hb, w_scr[:, off:off + n])
        off += n


def _inproj(l, xs, g_mix, mod, w_in_t, sgu_g, sgu_w, sgu_bt):
    def mod_spec(j):
        return pl.BlockSpec((None, None, 1, D_MODEL), lambda i: (l, _row_group(i), 0, j))

    widths = (SEG_A, SEG_B, SEG_C, WIDTH)
    return pl.pallas_call(
        functools.partial(_inproj_kernel, len(xs)),
        grid=(N_TOK // TM,),
        in_specs=_x_specs(len(xs) == 2, 0) + [
            pl.BlockSpec((1, 1, D_MODEL), lambda i: (l, 0, 0)),
            mod_spec(0), mod_spec(1),
            pl.BlockSpec((1, IN_COLS, D_MODEL), lambda i: (l, 0, 0), pipeline_mode=pl.Buffered(1)),
            pl.BlockSpec((1, 1, WIDTH), lambda i: (l, 0, 0)),
            pl.BlockSpec((1, SGU_GROUPS, SGU_CHUNK, SGU_CHUNK), lambda i: (l, 0, 0, 0)),
            pl.BlockSpec((1, SGU_CHUNK, SGU_GROUPS), lambda i: (l, 0, 0)),
        ],
        out_specs=[pl.BlockSpec((TM, n), lambda i: (i, 0)) for n in widths],
        out_shape=[jax.ShapeDtypeStruct((N_TOK, n), jnp.float32) for n in widths],
        scratch_shapes=[pltpu.VMEM((D_MODEL, IN_COLS_P), jnp.bfloat16)],
        compiler_params=_params("arbitrary"),
        name="inproj",
    )(*xs, g_mix, mod, mod, w_in_t, sgu_g, sgu_w, sgu_bt)


C_QN, C_QP, C_CKV, C_KPE = 0, 256, 384, 512


def _stack_heads(qp):
    lo = _lane_range(0, 64)
    return jnp.concatenate([_bf(jnp.where(lo, qp, 0.0)), _bf(jnp.where(lo, 0.0, qp))], axis=0)


def _unstack_heads(o, n):
    return jnp.where(_lane_range(0, 64), o[0:n], o[n:2 * n])


def _pair_t(c_ref):
    return jnp.concatenate([c_ref[0], c_ref[1]], axis=0)


def _stack_components(qp):
    return jnp.concatenate([_bf(jnp.where(_lane_range(32 * t, 32 * (t + 1)), qp, 0.0)) for t in range(4)], axis=0)


def _group_mean_sq(x, groups, size):
    sq = x * x
    hi = _bf(sq)
    rest = sq - hi.astype(jnp.float32)
    mid = _bf(rest)
    lo = _bf(rest - mid.astype(jnp.float32))
    return (_dot(hi, groups) + _dot(mid, groups) + _dot(lo, groups)) * (1.0 / size)


def _diff_finish(o, n, lam, post, g2, by_head):
    den = o[:, LANES:2 * LANES]
    outs = []
    for t in range(2):
        p1 = o[2 * t * n:(2 * t + 1) * n, 0:LANES] * (1.0 / den[2 * t * n:(2 * t + 1) * n])
        p2 = o[(2 * t + 1) * n:(2 * t + 2) * n, 0:LANES] * (lam / den[(2 * t + 1) * n:(2 * t + 2) * n])
        outs.append(p1 - p2)
    d = jnp.where(_lane_range(0, 64), outs[0], outs[1])
    return d * lax.rsqrt(_group_mean_sq(d, by_head, DIFF_V_DIM) + EPS) * g2 * post


def _mla_groups(j):
    return _group_matrix(2 * LANES, 2,
                         lambda t: (LANES + MLA_ROPE * (2 * j + t), LANES + MLA_ROPE * (2 * j + t + 1)))


def _mla_queries(qn_pair, qp_all, j):
    halves = []
    for t in range(2):
        h = 2 * j + t
        halves.append(jnp.concatenate([
            _bf(jnp.where(_lane_range(64 * t, 64 * (t + 1)), qn_pair, 0.0)),
            _bf(jnp.where(_lane_range(MLA_ROPE * h, MLA_ROPE * (h + 1)), qp_all, 0.0))], axis=1))
    return jnp.concatenate(halves, axis=0)


def _write_heads_t(p_ref, rows, col0, out_ref, bb):
    xt = p_ref[rows, col0:col0 + WIDTH].T
    for h in range(N_HEADS):
        out_ref[bb, 0, h] = xt[64 * h:64 * (h + 1)]
    _clear_other_layers(out_ref, bb)


def _clear_other_layers(out_ref, bb):
    if out_ref.shape[1] > 1:
        out_ref[bb, 1:] = jnp.zeros(out_ref.shape[1:], jnp.float32)[1:]


def _mix_prompt_kernel(n_prev, *refs):
    ins, outs = refs[:8], refs[8 + n_prev:]
    log = _ShiftLog()
    _mix_prompt_pass(ins, outs, log)

    @pl.when(log.unsafe())
    def _():
        _mix_prompt_pass(ins, outs, None)


def _mix_prompt_pass(ins, outs, log):
    pa_ref, pb_ref, pc_ref, cst_ref, gsub_ref, gckv_ref, wuk_ref, wuv_ref = ins
    o_ref, nak_ref, nav_ref, dk_ref, dv_ref, ckv_ref, kpe_ref = outs
    first_pass = log is not None
    c_a = HEAD_DIM ** -0.5 * LOG2E
    c_b = DIFF_QK_DIM ** -0.5 * LOG2E
    c_c = (MLA_NOPE + MLA_ROPE) ** -0.5 * LOG2E
    lam = cst_ref[0, 0:1, 0:1]
    post = cst_ref[0, 1:2, 0:1]
    wuk, wuv = _bf(wuk_ref[0]), _bf(wuv_ref[0])
    by_head, by_comp = _group_matrix(LANES, 2), _group_matrix(LANES, 4)

    def sequence(bb, log):
        rows = pl.ds(_aligned(bb * SEQ, SEQ), SEQ)
        for j in range(N_PAIRS):
            cols = slice(LANES * j, LANES * (j + 1))
            k = _bf(pa_ref[rows, WIDTH + LANES * j:WIDTH + LANES * (j + 1)])
            v = _with_ones(_bf(pa_ref[rows, 2 * WIDTH + LANES * j:2 * WIDTH + LANES * (j + 1)]))
            qs = _stack_heads(pa_ref[rows, cols] * c_a)
            o = _attend([_dot_nt(qs, k)], [v], log, _bound(log, qs, _key_bound([k], by_head), 2))
            o_ref[rows, cols] = _unstack_heads(_normalised(o), SEQ)
        if first_pass:
            _write_heads_t(pa_ref, rows, WIDTH, nak_ref, bb)
            _write_heads_t(pa_ref, rows, 2 * WIDTH, nav_ref, bb)
        for j in range(N_PAIRS):
            cols = slice(LANES * j, LANES * (j + 1))
            k = _bf(pb_ref[rows, WIDTH + LANES * j:WIDTH + LANES * (j + 1)])
            v = _with_ones(_bf(pb_ref[rows, 2 * WIDTH + LANES * j:2 * WIDTH + LANES * (j + 1)]))
            qs = _stack_components(pb_ref[rows, cols] * c_b)
            o = _attend([_dot_nt(qs, k)], [v], log, _bound(log, qs, _key_bound([k], by_comp), 4))
            o_ref[rows, WIDTH + LANES * j:WIDTH + LANES * (j + 1)] = _diff_finish(o, SEQ, lam, post, gsub_ref[0], by_head)
        if first_pass:
            _write_heads_t(pb_ref, rows, WIDTH, dk_ref, bb)
            _write_heads_t(pb_ref, rows, 2 * WIDTH, dv_ref, bb)
        ckv = _rms(pc_ref[rows, C_CKV:C_CKV + MLA_KV_RANK], gckv_ref[0])
        kpe_slot = pc_ref[rows, C_KPE:C_KPE + LANES]
        if first_pass:
            ckv_ref[bb, 0] = ckv
            _clear_other_layers(ckv_ref, bb)
            kpe_ref[bb, 0] = kpe_slot.T[0:MLA_ROPE]
            _clear_other_layers(kpe_ref, bb)
        ckv_b = _bf(ckv)
        kn = _bf(_dot(ckv_b, wuk))
        vv = _bf(_dot(ckv_b, wuv))
        kpe4 = _bf(_tile4(kpe_slot))
        qn = pc_ref[rows, C_QN:C_QN + WIDTH] * c_c
        qp = pc_ref[rows, C_QP:C_QP + LANES] * c_c
        for j in range(N_PAIRS):
            cols = slice(LANES * j, LANES * (j + 1))
            k = jnp.concatenate([kn[:, cols], kpe4], axis=1)
            qs = _mla_queries(qn[:, cols], qp, j)
            groups = _mla_groups(j)
            o = _attend([_dot_nt(qs, k)], [_with_ones(vv[:, cols])], log,
                        _bound(log, qs, _key_bound([k], groups), 2))
            o_ref[rows, 2 * WIDTH + LANES * j:2 * WIDTH + LANES * (j + 1)] = _unstack_heads(_normalised(o), SEQ)

    _loop(PB, log, sequence)


def _mix_prompt(l, pa, pb, pc, cst, g_sub2, g_ckv, w_uk, w_uv, prev):
    n_prev = len(prev)
    tails = [(NA_HEADS, HEAD_DIM, SEQ)] * 2 + [(DIFF_HEADS, 64, SEQ)] * 2 + [(SEQ, MLA_KV_RANK), (MLA_ROPE, SEQ)]

    def cache_spec(tail):
        if l == 0:
            return pl.BlockSpec((PB, DEPTH) + tail, lambda b: (b, 0) + (0,) * len(tail))
        return pl.BlockSpec((PB, 1) + tail, lambda b: (b, l) + (0,) * len(tail))

    def rows(width):
        return pl.BlockSpec((PB * SEQ, width), lambda b: (b, 0))

    def layer(*tail):
        return pl.BlockSpec((1,) + tail, lambda b: (l,) + (0,) * len(tail))

    return pl.pallas_call(
        functools.partial(_mix_prompt_kernel, n_prev),
        grid=(BATCH // PB,),
        in_specs=[rows(SEG_A), rows(SEG_B), rows(SEG_C), layer(2, LANES), layer(1, LANES), layer(1, MLA_KV_RANK),
                  layer(MLA_KV_RANK, WIDTH), layer(MLA_KV_RANK, WIDTH)] + [pl.BlockSpec(memory_space=pl.ANY)] * n_prev,
        out_specs=[rows(O_ATT)] + [cache_spec(t) for t in tails],
        out_shape=[jax.ShapeDtypeStruct((N_PROMPT, O_ATT), jnp.float32)]
        + [jax.ShapeDtypeStruct((BATCH, DEPTH) + t, jnp.float32) for t in tails],
        input_output_aliases={8 + i: 1 + i for i in range(n_prev)},
        compiler_params=_params("parallel"),
        name="mix_prompt",
    )(pa, pb, pc, cst, g_sub2, g_ckv, w_uk, w_uv, *prev)


def _na_row_groups():
    kh = min(NA_WIN_ROWS, GRID_ROWS)
    r0s = [min(max(r - kh // 2, 0), GRID_ROWS - kh) for r in range(GRID_ROWS)]
    groups = []
    for r, r0 in enumerate(r0s):
        if groups and groups[-1][2] == r0:
            groups[-1][1] = r
        else:
            groups.append([r, r, r0])
    return kh, [tuple(g) for g in groups]


def _na_sample(q_ref, k_ref, v_ref, ck_ref, cv_ref, tt_ref, o_ref, log):
    c = HEAD_DIM ** -0.5 * LOG2E
    kh, groups = _na_row_groups()
    lk = kh * GRID_W
    edge = [g for g in groups if g[1] > g[0]]
    inner = [g for g in groups if g[1] == g[0]]
    depth = inner[0][0] - inner[0][2]
    assert all(g[0] - g[2] == depth for g in inner) and [g[0] for g in inner] == list(range(inner[0][0], inner[-1][0] + 1))

    def in_window(n):
        cq = lax.broadcasted_iota(jnp.int32, (n, lk), 0) & (GRID_W - 1)
        ck = lax.broadcasted_iota(jnp.int32, (n, lk), 1) & (GRID_W - 1)
        c0 = jnp.clip(cq - NA_WIN_COLS // 2, 0, GRID_W - NA_WIN_COLS)
        return (ck >= c0) & (ck < c0 + NA_WIN_COLS)

    kc_t = _bf(_pair_t(ck_ref))
    vc = _with_ones(_bf(_pair_t(cv_ref).T))
    by_head = _group_matrix(LANES, 2)
    k2max = _key_bound([_bf(k_ref[...]), _bf(_pair_t(ck_ref).T)], by_head)
    tmax = functools.reduce(jnp.maximum, [tt_ref[t, a] for t in range(2) for a in range(N_DROW - 1)])
    bplus = jnp.maximum(jnp.max(jnp.max(tmax, axis=-1, keepdims=True), axis=0, keepdims=True), 0.0)

    def group(row0, key0, offsets, log):
        n = len(offsets) * GRID_W
        rows, keys = pl.ds(row0, n), pl.ds(key0, lk)
        qg = _stack_heads(q_ref[rows, :] * c)
        k = _bf(k_ref[keys, :])
        v = _with_ones(_bf(v_ref[keys, :]))
        bias = jnp.concatenate([
            jnp.concatenate([tt_ref[t, 2 * i - off + NA_WIN_ROWS - 1] for i in range(kh // 2)], axis=1)
            for t in range(2) for off in offsets], axis=0)
        s_loc = jnp.where(in_window(2 * n), _dot_nt(qg, k) + bias, NEG_INF)
        o = _attend([_dot(qg, kc_t), s_loc], [vc, v], log, _bound(log, qg, k2max, 2, bplus))
        o_ref[rows, :] = _unstack_heads(_normalised(o), n)

    if log is None:
        def any_row(r, log):
            r0 = jnp.clip(r - kh // 2, 0, GRID_ROWS - kh)
            group(_aligned(r * GRID_W, GRID_W), _aligned(r0 * GRID_W, GRID_W), [r - r0], log)

        _loop(GRID_ROWS, log, any_row)
        return

    for (r_lo, r_hi, r0) in edge:
        group(r_lo * GRID_W, r0 * GRID_W, [r - r0 for r in range(r_lo, r_hi + 1)], log)

    def inner_row(i, log):
        r = inner[0][0] + i
        group(r * GRID_W, (r - depth) * GRID_W, [depth], log)

    _loop(len(inner), log, inner_row)


def _diff_sample(q_ref, k_ref, v_ref, ck_ref, cv_ref, cos_ref, sin_ref, cst_ref, g_ref, o_ref, log):
    c = DIFF_QK_DIM ** -0.5 * LOG2E
    lam = cst_ref[0, 0:1, 0:1]
    post = cst_ref[0, 1:2, 0:1]
    k_new = _bf(_rope(k_ref[...], cos_ref[...], sin_ref[...]))
    kc_t = _bf(_pair_t(ck_ref))
    vc = _with_ones(_bf(_pair_t(cv_ref).T))
    v = _with_ones(_bf(v_ref[...]))
    by_head, by_comp = _group_matrix(LANES, 2), _group_matrix(LANES, 4)
    k2max = _key_bound([k_new, _bf(_pair_t(ck_ref).T)], by_comp)

    qb = QB_EXACT if log is None else QB

    def block(qi, log):
        rows = pl.ds(_aligned(qi * qb, qb), qb)
        qs = _stack_components(_rope(q_ref[rows, :], cos_ref[rows, :], sin_ref[rows, :]) * c)
        o = _attend([_dot(qs, kc_t), _dot_nt(qs, k_new)], [vc, v], log, _bound(log, qs, k2max, 4))
        o_ref[rows, :] = _diff_finish(o, qb, lam, post, g_ref[0], by_head)

    _loop(DEC_SEQ // qb, log, block)


def _mla_sample(j, qn_ref, qp_ref, ckv_ref, kpe_ref, cckv_ref, ckpe_ref, cosq_ref, sinq_ref, cosk_ref, sink_ref,
                gckv_ref, wuk_ref, wuv_ref, o_ref, log):
    c = (MLA_NOPE + MLA_ROPE) ** -0.5 * LOG2E
    wuk, wuv = _bf(wuk_ref[0]), _bf(wuv_ref[0])
    ckv_new = _bf(_rms(ckv_ref[...], gckv_ref[0]))
    ckv_old = _bf(cckv_ref[...])
    kpe_new = _bf(_tile4(_rope(kpe_ref[...], cosk_ref[...], sink_ref[...])))
    kpe_old = _bf(jnp.concatenate([ckpe_ref[...]] * MLA_HEADS, axis=0).T)
    k_old = jnp.concatenate([_bf(_dot(ckv_old, wuk)), kpe_old], axis=1)
    k_new = jnp.concatenate([_bf(_dot(ckv_new, wuk)), kpe_new], axis=1)
    vo, vn = _with_ones(_bf(_dot(ckv_old, wuv))), _with_ones(_bf(_dot(ckv_new, wuv)))
    groups = _mla_groups(j)
    k2max = _key_bound([k_old, k_new], groups)

    qb = QB_EXACT if log is None else QB

    def block(qi, log):
        rows = pl.ds(_aligned(qi * qb, qb), qb)
        qp = _rope(qp_ref[rows, :], cosq_ref[rows, :], sinq_ref[rows, :]) * c
        qs = _mla_queries(qn_ref[rows, :] * c, qp, j)
        o = _attend([_dot_nt(qs, k_old), _dot_nt(qs, k_new)], [vo, vn], log, _bound(log, qs, k2max, 2))
        o_ref[rows, :] = _unstack_heads(_normalised(o), qb)

    _loop(DEC_SEQ // qb, log, block)


def _mix_sample_kernel(pa_ref, pb_ref, pc_ref, cnak_ref, cnav_ref, cdk_ref, cdv_ref, cckv_ref, ckpe_ref, tt_ref,
                       cosb_ref, sinb_ref, cosq_ref, sinq_ref, cosk_ref, sink_ref,
                       cst_ref, gsub_ref, gckv_ref, wuk_ref, wuv_ref, o_ref):
    def run(log):
        for j in range(N_PAIRS):
            pair = slice(2 * j, 2 * j + 2)

            def cols(ref, first):
                return ref.at[:, first + LANES * j:first + LANES * (j + 1)]

            _na_sample(cols(pa_ref, 0), cols(pa_ref, WIDTH), cols(pa_ref, 2 * WIDTH), cnak_ref.at[pair],
                       cnav_ref.at[pair], tt_ref.at[pair], cols(o_ref, 0), log)
            _diff_sample(cols(pb_ref, 0), cols(pb_ref, WIDTH), cols(pb_ref, 2 * WIDTH), cdk_ref.at[pair],
                         cdv_ref.at[pair], cosb_ref, sinb_ref, cst_ref, gsub_ref, cols(o_ref, WIDTH), log)
            _mla_sample(j, cols(pc_ref, C_QN), pc_ref.at[:, C_QP:C_QP + LANES], pc_ref.at[:, C_CKV:C_CKV + LANES],
                        pc_ref.at[:, C_KPE:C_KPE + LANES], cckv_ref, ckpe_ref, cosq_ref, sinq_ref, cosk_ref, sink_ref,
                        gckv_ref, wuk_ref.at[:, :, LANES * j:LANES * (j + 1)], wuv_ref.at[:, :, LANES * j:LANES * (j + 1)],
                        cols(o_ref, 2 * WIDTH), log)

    log = _ShiftLog()
    run(log)

    @pl.when(log.unsafe())
    def _():
        run(None)


def _mix_sample(l, pa, pb, pc, caches_t, tt, tables, cst, g_sub2, g_ckv, w_uk, w_uv):
    first = N_PROMPT // DEC_SEQ

    def rows(width):
        return pl.BlockSpec((DEC_SEQ, width), lambda b: (first + b, 0))

    def cache(*tail):
        return pl.BlockSpec((None, None) + tail, lambda b: (b, l) + (0,) * len(tail))

    def layer(*tail):
        return pl.BlockSpec((1,) + tail, lambda b: (l,) + (0,) * len(tail))

    table = pl.BlockSpec((DEC_SEQ, LANES), lambda b: (0, 0), pipeline_mode=pl.Buffered(1))
    kv_t = cache(N_HEADS, 64, PAST_LEN)
    return pl.pallas_call(
        _mix_sample_kernel,
        grid=(DEC_BATCH,),
        in_specs=[rows(SEG_A), rows(SEG_B), rows(SEG_C), kv_t, kv_t, kv_t, kv_t,
                  cache(PAST_LEN, MLA_KV_RANK), cache(MLA_ROPE, PAST_LEN),
                  pl.BlockSpec((None, NA_HEADS, N_DROW - 1, GRID_W, LANES), lambda b: (l, 0, 0, 0, 0),
                               pipeline_mode=pl.Buffered(1)),
                  table, table, table, table, table, table,
                  layer(2, LANES), layer(1, LANES), layer(1, MLA_KV_RANK),
                  layer(MLA_KV_RANK, WIDTH), layer(MLA_KV_RANK, WIDTH)],
        out_specs=pl.BlockSpec((DEC_SEQ, O_ATT), lambda b: (b, 0)),
        out_shape=jax.ShapeDtypeStruct((N_SAMPLE, O_ATT), jnp.float32),
        compiler_params=_params("parallel"),
        name="mix_sample",
    )(pa, pb, pc, *caches_t, tt, *tables, cst, g_sub2, g_ckv, w_uk, w_uv)


FF_CHUNK = 1024


def _outffn_kernel(n_x, first, final, *refs):
    x_refs, o_refs = refs[:n_x], refs[n_x:n_x + 2]
    (od_ref, wout_ref, g1_ref, gffn_ref, sh2_ref, sc2_ref, g2_ref, w1_ref, w2_ref, gfin_ref, y_ref) = refs[n_x + 2:]
    acc = (_dot(_bf(_read_tile(o_refs, first)), wout_ref[0, 0:O_ATT, :])
           + _dot(_bf(od_ref[...]), wout_ref[0, O_ATT:O_ATT + WIDTH, :]))
    x1 = _read_tile(x_refs, first) + g1_ref[...] * acc
    hf = _bf(_rms(x1, gffn_ref[0]) * (1.0 + sc2_ref[...]) + sh2_ref[...])
    acc = jnp.zeros((TM, D_MODEL), jnp.float32)
    for c in range(D_FF // FF_CHUNK):
        cols = slice(FF_CHUNK * c, FF_CHUNK * (c + 1))
        a = jnp.square(jnp.maximum(_dot(hf, w1_ref[0, :, cols]), 0.0))
        acc += _dot(_bf(a), w2_ref[0, cols, :])
    y = x1 + g2_ref[...] * acc
    if final:
        y = _rms(y, gfin_ref[...])
    y_ref[...] = y


def _outffn(l, xs, o_p, o_s, od, w_out, g_ffn, mod, w1, w2, g_final, first, n_tiles):
    def mod_spec(j):
        return pl.BlockSpec((None, None, 1, D_MODEL), lambda i: (l, _row_group(first + i), 0, j))

    def resident(shape):
        return pl.BlockSpec(shape, lambda i: (l,) + (0,) * (len(shape) - 1), pipeline_mode=pl.Buffered(1))

    return pl.pallas_call(
        functools.partial(_outffn_kernel, len(xs), first, l == DEPTH - 1),
        grid=(n_tiles,),
        in_specs=_x_specs(len(xs) == 2, first) + _split_specs(O_ATT, first) + [
            pl.BlockSpec((TM, WIDTH), lambda i: (first + i, 0)),
            resident((1, 4 * WIDTH, D_MODEL)),
            mod_spec(2),
            pl.BlockSpec((1, 1, D_MODEL), lambda i: (l, 0, 0)),
            mod_spec(3), mod_spec(4), mod_spec(5),
            resident((1, D_MODEL, D_FF)),
            resident((1, D_FF, D_MODEL)),
            pl.BlockSpec((1, D_MODEL), lambda i: (0, 0)),
        ],
        out_specs=pl.BlockSpec((TM, D_MODEL), lambda i: (i, 0)),
        out_shape=jax.ShapeDtypeStruct((n_tiles * TM, D_MODEL), jnp.float32),
        compiler_params=_params("parallel"),
        name="outffn",
    )(*xs, o_p, o_s, od, w_out, mod, g_ffn, mod, mod, mod, w1, w2, g_final)


def _rope32_tables():
    t = np.arange(DEC_SEQ)
    rows, cols = (t // GRID_W).astype(np.float64), (t % GRID_W).astype(np.float64)
    half = 8
    freqs = ROPE_BASE ** (-np.arange(half, dtype=np.float64) / half)
    cos, sin = [], []
    for pos in (rows, cols):
        ang = pos[:, None] * freqs[None, :]
        cos += [np.cos(ang), np.cos(ang)]
        sin += [-np.sin(ang), np.sin(ang)]
    return np.concatenate(cos, axis=1).astype(np.float32), np.concatenate(sin, axis=1).astype(np.float32)


def _rope_tables():
    c32, s32 = _rope32_tables()
    tile = lambda a, n: np.tile(a, (1, n))
    pad = np.zeros((DEC_SEQ, LANES - MLA_ROPE), np.float32)
    cos_k = np.concatenate([c32, pad + 1.0], axis=1)
    sin_k = np.concatenate([s32, pad], axis=1)
    return (tile(c32, 4), tile(s32, 4),
            cos_k, sin_k)


def kernel(x_prompt, x_sample, cache_na_k, cache_na_v, cache_diff_k, cache_diff_v, cache_mla_ckv, cache_mla_kpe, c, c_ctx, w_ada, b_ada, g_mix, g_ffn, w_in, w_out, na_rpb, diff_lq1, diff_lk1, diff_lq2, diff_lk2, diff_g_subln, mla_g_ckv, mla_w_uk, mla_w_uv, sgu_g, sgu_w, sgu_b, w_ff1, w_ff2, g_final):
    f32 = jnp.float32
    m = jnp.concatenate([c_ctx[None, :], c, jnp.zeros((N_MOD_ROWS - 1 - DEC_BATCH, D_MODEL), f32)], axis=0)
    mod = _ada(m, w_ada, b_ada).reshape(DEPTH, N_MOD_ROWS, 1, 6 * D_MODEL)
    cst = _lam_consts(diff_lq1, diff_lk1, diff_lq2, diff_lk2)
    tt = _bias_tiles(na_rpb)
    cos4, sin4, cos_k, sin_k = [jnp.asarray(t) for t in _rope_tables()]
    tables = (cos4, sin4, cos4, sin4, cos_k, sin_k)

    t_last = lambda a: jnp.swapaxes(a, -1, -2)
    w_in_t = t_last(w_in)
    caches_t = (t_last(cache_na_k), t_last(cache_na_v), t_last(cache_diff_k), t_last(cache_diff_v),
                cache_mla_ckv, t_last(cache_mla_kpe))
    w_out_b, w1_b, w2_b = _bf(w_out), _bf(w_ff1), _bf(w_ff2)
    g_mix3 = g_mix.reshape(DEPTH, 1, D_MODEL)
    g_ffn3 = g_ffn.reshape(DEPTH, 1, D_MODEL)
    g_sub2 = jnp.tile(diff_g_subln, (1, 2)).reshape(DEPTH, 1, LANES)
    g_ckv3 = mla_g_ckv.reshape(DEPTH, 1, MLA_KV_RANK)
    sgu_g3 = sgu_g.reshape(DEPTH, 1, WIDTH)
    sgu_bt = sgu_b.transpose(0, 2, 1)
    g_fin2 = g_final.reshape(1, D_MODEL)

    xs = (x_prompt.reshape(N_PROMPT, D_MODEL), x_sample.reshape(N_SAMPLE, D_MODEL))
    new = ()
    for l in range(DEPTH):
        pa, pb, pc, od = _inproj(l, xs, g_mix3, mod, w_in_t, sgu_g3, sgu_w, sgu_bt)
        o_p, *new = _mix_prompt(l, pa, pb, pc, cst, g_sub2, g_ckv3, mla_w_uk, mla_w_uv, new)
        o_s = _mix_sample(l, pa, pb, pc, caches_t, tt, tables, cst, g_sub2, g_ckv3, mla_w_uk, mla_w_uv)
        ffn = functools.partial(_outffn, l, xs, o_p, o_s, od, w_out_b, g_ffn3, mod, w1_b, w2_b, g_fin2)
        if l < DEPTH - 1:
            xs = (ffn(0, TILES_PROMPT + TILES_SAMPLE),)
        else:
            xs = (ffn(0, TILES_PROMPT), ffn(TILES_PROMPT, TILES_SAMPLE))
    y_prompt = xs[0].reshape(BATCH, SEQ, D_MODEL)
    y_sample = xs[1].reshape(DEC_BATCH, DEC_SEQ, D_MODEL)
    na_k, na_v, diff_k, diff_v, mla_ckv, mla_kpe = new
    return (y_prompt, y_sample, t_last(na_k), t_last(na_v), t_last(diff_k), t_last(diff_v), mla_ckv, t_last(mla_kpe))
```

```python
import functools
import math

import numpy as np
import jax
import jax.numpy as jnp
from jax import lax
from jax.experimental import pallas as pl
from jax.experimental.pallas import tpu as pltpu

D_MODEL = 1024
BATCH = 16
SEQ = 256
DEPTH = 4
DEC_BATCH = 2
DEC_SEQ = 1024
PAST_LEN = 512
GRID_W = 64
GRID_ROWS = DEC_SEQ // GRID_W
HEAD_DIM = 64
NA_HEADS = 4
NA_WIN_ROWS = 8
NA_WIN_COLS = 16
DIFF_HEADS = 4
DIFF_QK_DIM = 32
DIFF_V_DIM = 64
MLA_HEADS = 4
MLA_NOPE = 64
MLA_ROPE = 32
MLA_V = 64
MLA_KV_RANK = 128
SGU_GROUPS = 4
SGU_GROUP_DIM = 64
SGU_CHUNK = 128
D_FF = 4 * D_MODEL
ROPE_BASE = 10000.0
EPS = 1e-6
NEG_INF = -1e30
LOG2E = 1.4426950408889634

N_HEADS = 4
N_PAIRS = N_HEADS // 2
LANES = 128
WIDTH = 256
N_PROMPT = BATCH * SEQ
N_SAMPLE = DEC_BATCH * DEC_SEQ
N_TOK = N_PROMPT + N_SAMPLE
N_MOD_ROWS = 8

SEG_A = 3 * WIDTH
SEG_B = 3 * WIDTH
SEG_C = 640
SEG_D = 2 * WIDTH
SEG_C_PAD = 96
IN_COLS_P = SEG_A + SEG_B + SEG_C + SEG_D
O_ATT = 3 * WIDTH

TM = 512
TILES_PROMPT = N_PROMPT // TM
TILES_SAMPLE = N_SAMPLE // TM
PB = 1
QB = 256
QB_EXACT = 64
VMEM_LIMIT = 56 * 1024 * 1024


def _bf(x):
    return x.astype(jnp.bfloat16)


def _dot(a, b):
    return jnp.dot(a, b, preferred_element_type=jnp.float32)


def _dot_nt(a, b):
    return lax.dot_general(a, b, (((1,), (1,)), ((), ())), preferred_element_type=jnp.float32)


def _rms(x, g):
    ms = jnp.mean(x * x, axis=-1, keepdims=True)
    return x * lax.rsqrt(ms + EPS) * g


def _lane_range(lo, hi, width=LANES):
    lane = lax.broadcasted_iota(jnp.int32, (1, width), 1)
    return (lane >= lo) & (lane < hi)


def _with_ones(v):
    return jnp.concatenate([v, jnp.ones((v.shape[0], LANES), jnp.bfloat16)], axis=1)


def _attend(scores, values, log=None, bound=None):
    if bound is None:
        m = functools.reduce(jnp.maximum, [jnp.max(s, axis=-1, keepdims=True) for s in scores])
        shifted = [s - m for s in scores]
    else:
        over = jnp.max(bound - scores[0][:, 0:LANES], axis=0, keepdims=True)
        log.worst = jnp.maximum(log.worst, jnp.min(over, axis=1, keepdims=True))
        shifted = [s - bound for s in scores]
    return functools.reduce(lambda a, b: a + b, [_dot(_bf(jnp.exp2(s)), v) for s, v in zip(shifted, values)])


BOUND_SLACK = 1.02
OVERSHOOT_LIMIT = 100.0


class _ShiftLog:
    def __init__(self, worst=None):
        self.worst = jnp.zeros((1, 1), jnp.float32) if worst is None else worst

    def unsafe(self):
        return jnp.logical_not(self.worst[0, 0] < OVERSHOOT_LIMIT)


def _group_matrix(width, n_groups, extra=None):
    i = lax.broadcasted_iota(jnp.int32, (width, LANES), 0)
    j = lax.broadcasted_iota(jnp.int32, (width, LANES), 1)
    size = LANES // n_groups
    hit = (i // size == j // size) & (i < LANES)
    for t in range(n_groups if extra else 0):
        lo, hi = extra(t)
        hit = hit | ((i >= lo) & (i < hi) & (j // size == t))
    return jnp.where(hit, 1.0, 0.0).astype(jnp.bfloat16)


def _squares(x):
    xf = x.astype(jnp.float32)
    return _bf(xf * xf)


def _key_bound(keys, groups):
    return functools.reduce(jnp.maximum, [jnp.max(_dot(_squares(k), groups), axis=0, keepdims=True) for k in keys])


def _bound(log, qs, k2max, n_groups, extra=0.0):
    if log is None:
        return None
    n, size = qs.shape[0] // n_groups, LANES // n_groups
    kmax = jnp.concatenate([jnp.broadcast_to(jnp.sqrt(k2max[:, size * t:size * t + 1]) * BOUND_SLACK, (n, 1))
                            for t in range(n_groups)], axis=0)
    qf = qs.astype(jnp.float32)
    return jnp.sqrt(jnp.sum(qf * qf, axis=-1, keepdims=True)) * kmax + extra


def _loop(n, log, body):
    if log is not None:
        for i in range(n):
            body(i, log)
    else:
        lax.fori_loop(0, n, lambda i, carry: body(i, None) or carry, 0)


def _aligned(start, multiple):
    return start if isinstance(start, int) else pl.multiple_of(start, multiple)


def _normalised(o_ext):
    return o_ext[:, 0:LANES] * (1.0 / o_ext[:, LANES:2 * LANES])


def _swap8(x):
    lane = lax.broadcasted_iota(jnp.int32, (1, LANES), 1)
    return jnp.where((lane & 15) < 8, pltpu.roll(x, LANES - 8, 1), pltpu.roll(x, 8, 1))


def _rope(x, cos, sin):
    outs = []
    for c in range(x.shape[1] // LANES):
        sl = slice(LANES * c, LANES * (c + 1))
        xc = x[:, sl]
        outs.append(xc * cos[:, sl] + _swap8(xc) * sin[:, sl])
    return outs[0] if len(outs) == 1 else jnp.concatenate(outs, axis=1)


def _tile4(x):
    return x + pltpu.roll(x, 32, 1) + pltpu.roll(x, 64, 1) + pltpu.roll(x, 96, 1)


def _params(*sem):
    return pltpu.CompilerParams(dimension_semantics=sem, vmem_limit_bytes=VMEM_LIMIT)


ADA_TN = 1536


def _ada_kernel(m_ref, w_ref, b_ref, o_ref):
    m = m_ref[...]
    s = m * jax.nn.sigmoid(m)
    o_ref[0] = _dot(_bf(s), _bf(w_ref[0])) + b_ref[0]


def _ada(m, w_ada, b_ada):
    n = 6 * D_MODEL
    return pl.pallas_call(
        _ada_kernel,
        grid=(DEPTH, n // ADA_TN),
        in_specs=[
            pl.BlockSpec((N_MOD_ROWS, D_MODEL), lambda l, j: (0, 0)),
            pl.BlockSpec((1, D_MODEL, ADA_TN), lambda l, j: (l, 0, j)),
            pl.BlockSpec((1, 1, ADA_TN), lambda l, j: (l, 0, j)),
        ],
        out_specs=pl.BlockSpec((1, N_MOD_ROWS, ADA_TN), lambda l, j: (l, 0, j)),
        out_shape=jax.ShapeDtypeStruct((DEPTH, N_MOD_ROWS, n), jnp.float32),
        compiler_params=_params("parallel", "parallel"),
        name="ada",
    )(m, w_ada, b_ada.reshape(DEPTH, 1, n))


def _lam_kernel(lq1_ref, lk1_ref, lq2_ref, lk2_ref, init_ref, o_ref):
    init = init_ref[...]
    a = jnp.exp(jnp.sum(lq1_ref[...] * lk1_ref[...], axis=-1, keepdims=True))
    b = jnp.exp(jnp.sum(lq2_ref[...] * lk2_ref[...], axis=-1, keepdims=True))
    lam = a - b + init
    post = 1.0 - init
    for l in range(DEPTH):
        o_ref[l, 0:1, :] = jnp.broadcast_to(lam[l:l + 1], (1, LANES))
        o_ref[l, 1:2, :] = jnp.broadcast_to(post[l:l + 1], (1, LANES))


def _lam_consts(lq1, lk1, lq2, lk2):
    init = np.array([[0.8 - 0.6 * math.exp(-0.3 * l)] for l in range(DEPTH)], np.float32)
    return pl.pallas_call(
        _lam_kernel,
        out_shape=jax.ShapeDtypeStruct((DEPTH, 2, LANES), jnp.float32),
        name="diff_lambda",
    )(lq1, lk1, lq2, lk2, jnp.asarray(init))


N_DROW = 2 * NA_WIN_ROWS - 1
N_DCOL = 2 * NA_WIN_COLS - 1


def _bias_kernel(rpb_ref, o_ref):
    l = pl.program_id(0)
    h = pl.program_id(1)
    base = (l * NA_HEADS + h) * (N_DROW * N_DCOL)
    cq = lax.broadcasted_iota(jnp.int32, (GRID_W, LANES), 0)
    lane = lax.broadcasted_iota(jnp.int32, (GRID_W, LANES), 1)
    ck = lane & (GRID_W - 1)
    dcol = jnp.clip(ck - cq, -(NA_WIN_COLS - 1), NA_WIN_COLS - 1) + (NA_WIN_COLS - 1)
    hi = lane >= GRID_W
    for a in range(N_DROW - 1):
        acc = jnp.zeros((GRID_W, LANES), jnp.float32)
        for j in range(N_DCOL):
            lo_v = rpb_ref[base + a * N_DCOL + j]
            hi_v = rpb_ref[base + (a + 1) * N_DCOL + j]
            acc = jnp.where(dcol == j, jnp.where(hi, hi_v, lo_v), acc)
        o_ref[0, 0, a] = acc * LOG2E


def _bias_tiles(na_rpb):
    return pl.pallas_call(
        _bias_kernel,
        grid=(DEPTH, NA_HEADS),
        in_specs=[pl.BlockSpec(memory_space=pltpu.SMEM)],
        out_specs=pl.BlockSpec((1, 1, N_DROW - 1, GRID_W, LANES), lambda l, h: (l, h, 0, 0, 0)),
        out_shape=jax.ShapeDtypeStruct((DEPTH, NA_HEADS, N_DROW - 1, GRID_W, LANES), jnp.float32),
        compiler_params=_params("parallel", "parallel"),
        name="na_bias_tiles",
    )(na_rpb.reshape(-1))


def _row_group(i):
    return jnp.where(i < TILES_PROMPT, 0, 1 + (i - TILES_PROMPT) // (DEC_SEQ // TM))


def _split_specs(width, first):
    return [pl.BlockSpec((TM, width), lambda i: (jnp.minimum(first + i, TILES_PROMPT - 1), 0)),
            pl.BlockSpec((TM, width), lambda i: (jnp.maximum(first + i - TILES_PROMPT, 0), 0))]


def _x_specs(split, first):
    if not split:
        return [pl.BlockSpec((TM, D_MODEL), lambda i: (first + i, 0))]
    return _split_specs(D_MODEL, first)


def _read_tile(refs, first):
    if len(refs) == 1:
        return refs[0][...]
    return jnp.where(first + pl.program_id(0) < TILES_PROMPT, refs[0][...], refs[1][...])


IN_COLS = 2592
IN_QC, IN_CKV, IN_D = 1536, 1920, 2080
TR_ROWS = 256


def _gelu_tanh(x):
    return 0.5 * x * (1.0 + jnp.tanh(math.sqrt(2.0 / math.pi) * (x + 0.044715 * (x * x * x))))


def _sgu(pd, g, w_ref, bt):
    u = _gelu_tanh(pd[:, 0:WIDTH])
    v = _gelu_tanh(pd[:, WIDTH:2 * WIDTH])
    grp = lax.broadcasted_iota(jnp.int32, (1, WIDTH), 1) // SGU_GROUP_DIM
    v2 = v * v
    ms = jnp.zeros_like(v)
    for gi in range(SGU_GROUPS):
        sel = grp == gi
        tot = jnp.sum(jnp.where(sel, v2, 0.0), axis=-1, keepdims=True)
        ms = jnp.where(sel, tot * (1.0 / SGU_GROUP_DIM), ms)
    vg = _bf(v * lax.rsqrt(ms + EPS) * g)
    outs = []
    for c in range(pd.shape[0] // SGU_CHUNK):
        rows = slice(SGU_CHUNK * c, SGU_CHUNK * (c + 1))
        mixed = jnp.zeros((SGU_CHUNK, WIDTH), jnp.float32)
        for gi in range(SGU_GROUPS):
            full = _dot(_bf(w_ref[0, gi]), vg[rows]) + bt[:, gi:gi + 1]
            mixed = jnp.where(grp == gi, full, mixed)
        outs.append(u[rows] * mixed)
    return jnp.concatenate(outs, axis=0)


def _w_in_row_pieces():
    qn = [(IN_QC + 96 * h, MLA_NOPE) for h in range(MLA_HEADS)]
    qp = [(IN_QC + 96 * h + MLA_NOPE, MLA_ROPE) for h in range(MLA_HEADS)]
    seg_c = qn + qp + [(IN_CKV, MLA_KV_RANK + MLA_ROPE)]
    return (0, SEG_A + SEG_B), seg_c, (IN_D, SEG_D)


def _load_w_in(wt_ref, w_scr):
    ab, seg_c, d = _w_in_row_pieces()
    c_rows = jnp.concatenate([wt_ref[0, s:s + n, :] for s, n in seg_c]
                             + [jnp.zeros((SEG_C_PAD, D_MODEL), jnp.float32)], axis=0)
    for t in range(SEG_C // LANES):
        w_scr[:, SEG_A + SEG_B + LANES * t:SEG_A + SEG_B + LANES * (t + 1)] = _bf(c_rows[LANES * t:LANES * (t + 1)].T)
    for (src, n), dst in ((ab, 0), (d, SEG_A + SEG_B + SEG_C)):
        for t in range(n // TR_ROWS):
            rows = wt_ref[0, src + TR_ROWS * t:src + TR_ROWS * (t + 1), :]
            w_scr[:, dst + TR_ROWS * t:dst + TR_ROWS * (t + 1)] = _bf(rows.T)


def _inproj_kernel(n_x, *refs):
    x_refs = refs[:n_x]
    (g_ref, sh_ref, sc_ref, wt_ref, sg_ref, sw_ref, sbt_ref, pa_ref, pb_ref, pc_ref, od_ref, w_scr) = refs[n_x:]

    @pl.when(pl.program_id(0) == 0)
    def _():
        _load_w_in(wt_ref, w_scr)

    h = _rms(_read_tile(x_refs, 0), g_ref[0]) * (1.0 + sc_ref[...]) + sh_ref[...]
    hb = _bf(h)
    off = SEG_A + SEG_B + SEG_C
    od_ref[...] = _sgu(_dot(hb, w_scr[:, off:off + SEG_D]), sg_ref[0], sw_ref, sbt_ref[0])
    off = 0
    for ref in (pa_ref, pb_ref, pc_ref):
        n = ref.shape[1]
        ref[...] = _dot(hb, w_scr[:, off:off + n])
        off += n


def _inproj(l, xs, g_mix, mod, w_in_t, sgu_g, sgu_w, sgu_bt):
    def mod_spec(j):
        return pl.BlockSpec((None, None, 1, D_MODEL), lambda i: (l, _row_group(i), 0, j))

    widths = (SEG_A, SEG_B, SEG_C, WIDTH)
    return pl.pallas_call(
        functools.partial(_inproj_kernel, len(xs)),
        grid=(N_TOK // TM,),
        in_specs=_x_specs(len(xs) == 2, 0) + [
            pl.BlockSpec((1, 1, D_MODEL), lambda i: (l, 0, 0)),
            mod_spec(0), mod_spec(1),
            pl.BlockSpec((1, IN_COLS, D_MODEL), lambda i: (l, 0, 0), pipeline_mode=pl.Buffered(1)),
            pl.BlockSpec((1, 1, WIDTH), lambda i: (l, 0, 0)),
            pl.BlockSpec((1, SGU_GROUPS, SGU_CHUNK, SGU_CHUNK), lambda i: (l, 0, 0, 0)),
            pl.BlockSpec((1, SGU_CHUNK, SGU_GROUPS), lambda i: (l, 0, 0)),
        ],
        out_specs=[pl.BlockSpec((TM, n), lambda i: (i, 0)) for n in widths],
        out_shape=[jax.ShapeDtypeStruct((N_TOK, n), jnp.float32) for n in widths],
        scratch_shapes=[pltpu.VMEM((D_MODEL, IN_COLS_P), jnp.bfloat16)],
        compiler_params=_params("arbitrary"),
        name="inproj",
    )(*xs, g_mix, mod, mod, w_in_t, sgu_g, sgu_w, sgu_bt)


C_QN, C_QP, C_CKV, C_KPE = 0, 256, 384, 512


def _stack_heads(qp):
    lo = _lane_range(0, 64)
    return jnp.concatenate([_bf(jnp.where(lo, qp, 0.0)), _bf(jnp.where(lo, 0.0, qp))], axis=0)


def _unstack_heads(o, n):
    return jnp.where(_lane_range(0, 64), o[0:n], o[n:2 * n])


def _pair_t(c_ref):
    return jnp.concatenate([c_ref[0], c_ref[1]], axis=0)


def _stack_components(qp):
    return jnp.concatenate([_bf(jnp.where(_lane_range(32 * t, 32 * (t + 1)), qp, 0.0)) for t in range(4)], axis=0)


def _group_mean_sq(x, groups, size):
    sq = x * x
    hi = _bf(sq)
    rest = sq - hi.astype(jnp.float32)
    mid = _bf(rest)
    lo = _bf(rest - mid.astype(jnp.float32))
    return (_dot(hi, groups) + _dot(mid, groups) + _dot(lo, groups)) * (1.0 / size)


def _diff_finish(o, n, lam, post, g2, by_head):
    den = o[:, LANES:2 * LANES]
    outs = []
    for t in range(2):
        p1 = o[2 * t * n:(2 * t + 1) * n, 0:LANES] * (1.0 / den[2 * t * n:(2 * t + 1) * n])
        p2 = o[(2 * t + 1) * n:(2 * t + 2) * n, 0:LANES] * (lam / den[(2 * t + 1) * n:(2 * t + 2) * n])
        outs.append(p1 - p2)
    d = jnp.where(_lane_range(0, 64), outs[0], outs[1])
    return d * lax.rsqrt(_group_mean_sq(d, by_head, DIFF_V_DIM) + EPS) * g2 * post


def _mla_groups(j):
    return _group_matrix(2 * LANES, 2,
                         lambda t: (LANES + MLA_ROPE * (2 * j + t), LANES + MLA_ROPE * (2 * j + t + 1)))


def _mla_queries(qn_pair, qp_all, j):
    halves = []
    for t in range(2):
        h = 2 * j + t
        halves.append(jnp.concatenate([
            _bf(jnp.where(_lane_range(64 * t, 64 * (t + 1)), qn_pair, 0.0)),
            _bf(jnp.where(_lane_range(MLA_ROPE * h, MLA_ROPE * (h + 1)), qp_all, 0.0))], axis=1))
    return jnp.concatenate(halves, axis=0)


def _write_heads_t(p_ref, rows, col0, out_ref, bb):
    xt = p_ref[rows, col0:col0 + WIDTH].T
    for h in range(N_HEADS):
        out_ref[bb, 0, h] = xt[64 * h:64 * (h + 1)]
    _clear_other_layers(out_ref, bb)


def _clear_other_layers(out_ref, bb):
    if out_ref.shape[1] > 1:
        out_ref[bb, 1:] = jnp.zeros(out_ref.shape[1:], jnp.float32)[1:]


def _mix_prompt_kernel(n_prev, *refs):
    ins, outs = refs[:8], refs[8 + n_prev:]
    log = _ShiftLog()
    _mix_prompt_pass(ins, outs, log)

    @pl.when(log.unsafe())
    def _():
        _mix_prompt_pass(ins, outs, None)


def _mix_prompt_pass(ins, outs, log):
    pa_ref, pb_ref, pc_ref, cst_ref, gsub_ref, gckv_ref, wuk_ref, wuv_ref = ins
    o_ref, nak_ref, nav_ref, dk_ref, dv_ref, ckv_ref, kpe_ref = outs
    first_pass = log is not None
    c_a = HEAD_DIM ** -0.5 * LOG2E
    c_b = DIFF_QK_DIM ** -0.5 * LOG2E
    c_c = (MLA_NOPE + MLA_ROPE) ** -0.5 * LOG2E
    lam = cst_ref[0, 0:1, 0:1]
    post = cst_ref[0, 1:2, 0:1]
    wuk, wuv = _bf(wuk_ref[0]), _bf(wuv_ref[0])
    by_head, by_comp = _group_matrix(LANES, 2), _group_matrix(LANES, 4)

    def sequence(bb, log):
        rows = pl.ds(_aligned(bb * SEQ, SEQ), SEQ)
        for j in range(N_PAIRS):
            cols = slice(LANES * j, LANES * (j + 1))
            k = _bf(pa_ref[rows, WIDTH + LANES * j:WIDTH + LANES * (j + 1)])
            v = _with_ones(_bf(pa_ref[rows, 2 * WIDTH + LANES * j:2 * WIDTH + LANES * (j + 1)]))
            qs = _stack_heads(pa_ref[rows, cols] * c_a)
            o = _attend([_dot_nt(qs, k)], [v], log, _bound(log, qs, _key_bound([k], by_head), 2))
            o_ref[rows, cols] = _unstack_heads(_normalised(o), SEQ)
        if first_pass:
            _write_heads_t(pa_ref, rows, WIDTH, nak_ref, bb)
            _write_heads_t(pa_ref, rows, 2 * WIDTH, nav_ref, bb)
        for j in range(N_PAIRS):
            cols = slice(LANES * j, LANES * (j + 1))
            k = _bf(pb_ref[rows, WIDTH + LANES * j:WIDTH + LANES * (j + 1)])
            v = _with_ones(_bf(pb_ref[rows, 2 * WIDTH + LANES * j:2 * WIDTH + LANES * (j + 1)]))
            qs = _stack_components(pb_ref[rows, cols] * c_b)
            o = _attend([_dot_nt(qs, k)], [v], log, _bound(log, qs, _key_bound([k], by_comp), 4))
            o_ref[rows, WIDTH + LANES * j:WIDTH + LANES * (j + 1)] = _diff_finish(o, SEQ, lam, post, gsub_ref[0], by_head)
        if first_pass:
            _write_heads_t(pb_ref, rows, WIDTH, dk_ref, bb)
            _write_heads_t(pb_ref, rows, 2 * WIDTH, dv_ref, bb)
        ckv = _rms(pc_ref[rows, C_CKV:C_CKV + MLA_KV_RANK], gckv_ref[0])
        kpe_slot = pc_ref[rows, C_KPE:C_KPE + LANES]
        if first_pass:
            ckv_ref[bb, 0] = ckv
            _clear_other_layers(ckv_ref, bb)
            kpe_ref[bb, 0] = kpe_slot.T[0:MLA_ROPE]
            _clear_other_layers(kpe_ref, bb)
        ckv_b = _bf(ckv)
        kn = _bf(_dot(ckv_b, wuk))
        vv = _bf(_dot(ckv_b, wuv))
        kpe4 = _bf(_tile4(kpe_slot))
        qn = pc_ref[rows, C_QN:C_QN + WIDTH] * c_c
        qp = pc_ref[rows, C_QP:C_QP + LANES] * c_c
        for j in range(N_PAIRS):
            cols = slice(LANES * j, LANES * (j + 1))
            k = jnp.concatenate([kn[:, cols], kpe4], axis=1)
            qs = _mla_queries(qn[:, cols], qp, j)
            groups = _mla_groups(j)
            o = _attend([_dot_nt(qs, k)], [_with_ones(vv[:, cols])], log,
                        _bound(log, qs, _key_bound([k], groups), 2))
            o_ref[rows, 2 * WIDTH + LANES * j:2 * WIDTH + LANES * (j + 1)] = _unstack_heads(_normalised(o), SEQ)

    _loop(PB, log, sequence)


def _mix_prompt(l, pa, pb, pc, cst, g_sub2, g_ckv, w_uk, w_uv, prev):
    n_prev = len(prev)
    tails = [(NA_HEADS, HEAD_DIM, SEQ)] * 2 + [(DIFF_HEADS, 64, SEQ)] * 2 + [(SEQ, MLA_KV_RANK), (MLA_ROPE, SEQ)]

    def cache_spec(tail):
        if l == 0:
            return pl.BlockSpec((PB, DEPTH) + tail, lambda b: (b, 0) + (0,) * len(tail))
        return pl.BlockSpec((PB, 1) + tail, lambda b: (b, l) + (0,) * len(tail))

    def rows(width):
        return pl.BlockSpec((PB * SEQ, width), lambda b: (b, 0))

    def layer(*tail):
        return pl.BlockSpec((1,) + tail, lambda b: (l,) + (0,) * len(tail))

    return pl.pallas_call(
        functools.partial(_mix_prompt_kernel, n_prev),
        grid=(BATCH // PB,),
        in_specs=[rows(SEG_A), rows(SEG_B), rows(SEG_C), layer(2, LANES), layer(1, LANES), layer(1, MLA_KV_RANK),
                  layer(MLA_KV_RANK, WIDTH), layer(MLA_KV_RANK, WIDTH)] + [pl.BlockSpec(memory_space=pl.ANY)] * n_prev,
        out_specs=[rows(O_ATT)] + [cache_spec(t) for t in tails],
        out_shape=[jax.ShapeDtypeStruct((N_PROMPT, O_ATT), jnp.float32)]
        + [jax.ShapeDtypeStruct((BATCH, DEPTH) + t, jnp.float32) for t in tails],
        input_output_aliases={8 + i: 1 + i for i in range(n_prev)},
        compiler_params=_params("parallel"),
        name="mix_prompt",
    )(pa, pb, pc, cst, g_sub2, g_ckv, w_uk, w_uv, *prev)


def _na_row_groups():
    kh = min(NA_WIN_ROWS, GRID_ROWS)
    r0s = [min(max(r - kh // 2, 0), GRID_ROWS - kh) for r in range(GRID_ROWS)]
    groups = []
    for r, r0 in enumerate(r0s):
        if groups and groups[-1][2] == r0:
            groups[-1][1] = r
        else:
            groups.append([r, r, r0])
    return kh, [tuple(g) for g in groups]


def _na_sample(q_ref, k_ref, v_ref, ck_ref, cv_ref, tt_ref, o_ref, log):
    c = HEAD_DIM ** -0.5 * LOG2E
    kh, groups = _na_row_groups()
    lk = kh * GRID_W
    edge = [g for g in groups if g[1] > g[0]]
    inner = [g for g in groups if g[1] == g[0]]
    depth = inner[0][0] - inner[0][2]
    assert all(g[0] - g[2] == depth for g in inner) and [g[0] for g in inner] == list(range(inner[0][0], inner[-1][0] + 1))

    def in_window(n):
        cq = lax.broadcasted_iota(jnp.int32, (n, lk), 0) & (GRID_W - 1)
        ck = lax.broadcasted_iota(jnp.int32, (n, lk), 1) & (GRID_W - 1)
        c0 = jnp.clip(cq - NA_WIN_COLS // 2, 0, GRID_W - NA_WIN_COLS)
        return (ck >= c0) & (ck < c0 + NA_WIN_COLS)

    kc_t = _bf(_pair_t(ck_ref))
    vc = _with_ones(_bf(_pair_t(cv_ref).T))
    by_head = _group_matrix(LANES, 2)
    k2max = _key_bound([_bf(k_ref[...]), _bf(_pair_t(ck_ref).T)], by_head)
    tmax = functools.reduce(jnp.maximum, [tt_ref[t, a] for t in range(2) for a in range(N_DROW - 1)])
    bplus = jnp.maximum(jnp.max(jnp.max(tmax, axis=-1, keepdims=True), axis=0, keepdims=True), 0.0)

    def group(row0, key0, offsets, log):
        n = len(offsets) * GRID_W
        rows, keys = pl.ds(row0, n), pl.ds(key0, lk)
        qg = _stack_heads(q_ref[rows, :] * c)
        k = _bf(k_ref[keys, :])
        v = _with_ones(_bf(v_ref[keys, :]))
        bias = jnp.concatenate([
            jnp.concatenate([tt_ref[t, 2 * i - off + NA_WIN_ROWS - 1] for i in range(kh // 2)], axis=1)
            for t in range(2) for off in offsets], axis=0)
        s_loc = jnp.where(in_window(2 * n), _dot_nt(qg, k) + bias, NEG_INF)
        o = _attend([_dot(qg, kc_t), s_loc], [vc, v], log, _bound(log, qg, k2max, 2, bplus))
        o_ref[rows, :] = _unstack_heads(_normalised(o), n)

    if log is None:
        def any_row(r, log):
            r0 = jnp.clip(r - kh // 2, 0, GRID_ROWS - kh)
            group(_aligned(r * GRID_W, GRID_W), _aligned(r0 * GRID_W, GRID_W), [r - r0], log)

        _loop(GRID_ROWS, log, any_row)
        return

    for (r_lo, r_hi, r0) in edge:
        group(r_lo * GRID_W, r0 * GRID_W, [r - r0 for r in range(r_lo, r_hi + 1)], log)

    def inner_row(i, log):
        r = inner[0][0] + i
        group(r * GRID_W, (r - depth) * GRID_W, [depth], log)

    _loop(len(inner), log, inner_row)


def _diff_sample(q_ref, k_ref, v_ref, ck_ref, cv_ref, cos_ref, sin_ref, cst_ref, g_ref, o_ref, log):
    c = DIFF_QK_DIM ** -0.5 * LOG2E
    lam = cst_ref[0, 0:1, 0:1]
    post = cst_ref[0, 1:2, 0:1]
    k_new = _bf(_rope(k_ref[...], cos_ref[...], sin_ref[...]))
    kc_t = _bf(_pair_t(ck_ref))
    vc = _with_ones(_bf(_pair_t(cv_ref).T))
    v = _with_ones(_bf(v_ref[...]))
    by_head, by_comp = _group_matrix(LANES, 2), _group_matrix(LANES, 4)
    k2max = _key_bound([k_new, _bf(_pair_t(ck_ref).T)], by_comp)

    qb = QB_EXACT if log is None else QB

    def block(qi, log):
        rows = pl.ds(_aligned(qi * qb, qb), qb)
        qs = _stack_components(_rope(q_ref[rows, :], cos_ref[rows, :], sin_ref[rows, :]) * c)
        o = _attend([_dot(qs, kc_t), _dot_nt(qs, k_new)], [vc, v], log, _bound(log, qs, k2max, 4))
        o_ref[rows, :] = _diff_finish(o, qb, lam, post, g_ref[0], by_head)

    _loop(DEC_SEQ // qb, log, block)


def _mla_sample(j, qn_ref, qp_ref, ckv_ref, kpe_ref, cckv_ref, ckpe_ref, cosq_ref, sinq_ref, cosk_ref, sink_ref,
                gckv_ref, wuk_ref, wuv_ref, o_ref, log):
    c = (MLA_NOPE + MLA_ROPE) ** -0.5 * LOG2E
    wuk, wuv = _bf(wuk_ref[0]), _bf(wuv_ref[0])
    ckv_new = _bf(_rms(ckv_ref[...], gckv_ref[0]))
    ckv_old = _bf(cckv_ref[...])
    kpe_new = _bf(_tile4(_rope(kpe_ref[...], cosk_ref[...], sink_ref[...])))
    kpe_old = _bf(jnp.concatenate([ckpe_ref[...]] * MLA_HEADS, axis=0).T)
    k_old = jnp.concatenate([_bf(_dot(ckv_old, wuk)), kpe_old], axis=1)
    k_new = jnp.concatenate([_bf(_dot(ckv_new, wuk)), kpe_new], axis=1)
    vo, vn = _with_ones(_bf(_dot(ckv_old, wuv))), _with_ones(_bf(_dot(ckv_new, wuv)))
    groups = _mla_groups(j)
    k2max = _key_bound([k_old, k_new], groups)

    qb = QB_EXACT if log is None else QB

    def block(qi, log):
        rows = pl.ds(_aligned(qi * qb, qb), qb)
        qp = _rope(qp_ref[rows, :], cosq_ref[rows, :], sinq_ref[rows, :]) * c
        qs = _mla_queries(qn_ref[rows, :] * c, qp, j)
        o = _attend([_dot_nt(qs, k_old), _dot_nt(qs, k_new)], [vo, vn], log, _bound(log, qs, k2max, 2))
        o_ref[rows, :] = _unstack_heads(_normalised(o), qb)

    _loop(DEC_SEQ // qb, log, block)


def _mix_sample_kernel(qa_ref, ka_ref, va_ref, qb_ref, kb_ref, vb_ref, qn_ref, qp_ref, ckv_ref, kpe_ref,
                       cnak_ref, cnav_ref, cdk_ref, cdv_ref, cckv_ref, ckpe_ref, tt_ref,
                       cosb_ref, sinb_ref, cosq_ref, sinq_ref, cosk_ref, sink_ref,
                       cst_ref, gsub_ref, gckv_ref, wuk_ref, wuv_ref, oa_ref, ob_ref, oc_ref):
    j = pl.program_id(1)

    def run(log):
        _na_sample(qa_ref, ka_ref, va_ref, cnak_ref, cnav_ref, tt_ref, oa_ref, log)
        _diff_sample(qb_ref, kb_ref, vb_ref, cdk_ref, cdv_ref, cosb_ref, sinb_ref, cst_ref, gsub_ref, ob_ref, log)
        _mla_sample(j, qn_ref, qp_ref, ckv_ref, kpe_ref, cckv_ref, ckpe_ref, cosq_ref, sinq_ref, cosk_ref, sink_ref,
                    gckv_ref, wuk_ref, wuv_ref, oc_ref, log)

    log = _ShiftLog()
    run(log)

    @pl.when(log.unsafe())
    def _():
        run(None)


def _mix_sample(l, pa, pb, pc, caches_t, tt, tables, cst, g_sub2, g_ckv, w_uk, w_uv):
    first = N_PROMPT // DEC_SEQ

    def cols(block):
        return pl.BlockSpec((DEC_SEQ, LANES), lambda b, j: (first + b, block(j)))

    def cache(*tail, pair=False):
        return pl.BlockSpec((None, None) + tail, lambda b, j: (b, l, j if pair else 0) + (0,) * (len(tail) - 1))

    def layer(*tail):
        return pl.BlockSpec((1,) + tail, lambda b, j: (l,) + (0,) * len(tail))

    table = pl.BlockSpec((DEC_SEQ, LANES), lambda b, j: (0, 0), pipeline_mode=pl.Buffered(1))
    qkv = [cols(lambda j: j), cols(lambda j: N_PAIRS + j), cols(lambda j: 2 * N_PAIRS + j)]
    seg_c = [cols(lambda j: j), cols(lambda j: C_QP // LANES), cols(lambda j: C_CKV // LANES), cols(lambda j: C_KPE // LANES)]
    kv_t = cache(2, 64, PAST_LEN, pair=True)
    w_pair = pl.BlockSpec((1, MLA_KV_RANK, LANES), lambda b, j: (l, 0, j))
    out = pl.BlockSpec((DEC_SEQ, LANES), lambda b, j: (b, j))
    return pl.pallas_call(
        _mix_sample_kernel,
        grid=(DEC_BATCH, N_PAIRS),
        in_specs=qkv + qkv + seg_c + [kv_t, kv_t, kv_t, kv_t, cache(PAST_LEN, MLA_KV_RANK), cache(MLA_ROPE, PAST_LEN),
                                      pl.BlockSpec((None, 2, N_DROW - 1, GRID_W, LANES), lambda b, j: (l, j, 0, 0, 0)),
                                      table, table, table, table, table, table,
                                      layer(2, LANES), layer(1, LANES), layer(1, MLA_KV_RANK), w_pair, w_pair],
        out_specs=[out, out, out],
        out_shape=[jax.ShapeDtypeStruct((N_SAMPLE, WIDTH), jnp.float32)] * 3,
        compiler_params=_params("parallel", "arbitrary"),
        name="mix_sample",
    )(pa, pa, pa, pb, pb, pb, pc, pc, pc, pc, *caches_t, tt, *tables, cst, g_sub2, g_ckv, w_uk, w_uv)


FF_CHUNK = 1024


def _outffn_kernel(n_x, first, final, *refs):
    x_refs, op_ref, os_refs = refs[:n_x], refs[n_x], refs[n_x + 1:n_x + 4]
    (od_ref, wout_ref, g1_ref, gffn_ref, sh2_ref, sc2_ref, g2_ref, w1_ref, w2_ref, gfin_ref, y_ref) = refs[n_x + 4:]
    o_att = jnp.where(first + pl.program_id(0) < TILES_PROMPT, op_ref[...],
                      jnp.concatenate([r[...] for r in os_refs], axis=1))
    acc = (_dot(_bf(o_att), wout_ref[0, 0:O_ATT, :])
           + _dot(_bf(od_ref[...]), wout_ref[0, O_ATT:O_ATT + WIDTH, :]))
    x1 = _read_tile(x_refs, first) + g1_ref[...] * acc
    hf = _bf(_rms(x1, gffn_ref[0]) * (1.0 + sc2_ref[...]) + sh2_ref[...])
    acc = jnp.zeros((TM, D_MODEL), jnp.float32)
    for c in range(D_FF // FF_CHUNK):
        cols = slice(FF_CHUNK * c, FF_CHUNK * (c + 1))
        a = jnp.square(jnp.maximum(_dot(hf, w1_ref[0, :, cols]), 0.0))
        acc += _dot(_bf(a), w2_ref[0, cols, :])
    y = x1 + g2_ref[...] * acc
    if final:
        y = _rms(y, gfin_ref[...])
    y_ref[...] = y


def _outffn(l, xs, o_p, o_s, od, w_out, g_ffn, mod, w1, w2, g_final, first, n_tiles):
    def mod_spec(j):
        return pl.BlockSpec((None, None, 1, D_MODEL), lambda i: (l, _row_group(first + i), 0, j))

    def resident(shape):
        return pl.BlockSpec(shape, lambda i: (l,) + (0,) * (len(shape) - 1), pipeline_mode=pl.Buffered(1))

    return pl.pallas_call(
        functools.partial(_outffn_kernel, len(xs), first, l == DEPTH - 1),
        grid=(n_tiles,),
        in_specs=_x_specs(len(xs) == 2, first) + _split_specs(O_ATT, first)[:1] + _split_specs(WIDTH, first)[1:] * 3 + [
            pl.BlockSpec((TM, WIDTH), lambda i: (first + i, 0)),
            resident((1, 4 * WIDTH, D_MODEL)),
            mod_spec(2),
            pl.BlockSpec((1, 1, D_MODEL), lambda i: (l, 0, 0)),
            mod_spec(3), mod_spec(4), mod_spec(5),
            resident((1, D_MODEL, D_FF)),
            resident((1, D_FF, D_MODEL)),
            pl.BlockSpec((1, D_MODEL), lambda i: (0, 0)),
        ],
        out_specs=pl.BlockSpec((TM, D_MODEL), lambda i: (i, 0)),
        out_shape=jax.ShapeDtypeStruct((n_tiles * TM, D_MODEL), jnp.float32),
        compiler_params=_params("parallel"),
        name="outffn",
    )(*xs, o_p, *o_s, od, w_out, mod, g_ffn, mod, mod, mod, w1, w2, g_final)


def _rope32_tables():
    t = np.arange(DEC_SEQ)
    rows, cols = (t // GRID_W).astype(np.float64), (t % GRID_W).astype(np.float64)
    half = 8
    freqs = ROPE_BASE ** (-np.arange(half, dtype=np.float64) / half)
    cos, sin = [], []
    for pos in (rows, cols):
        ang = pos[:, None] * freqs[None, :]
        cos += [np.cos(ang), np.cos(ang)]
        sin += [-np.sin(ang), np.sin(ang)]
    return np.concatenate(cos, axis=1).astype(np.float32), np.concatenate(sin, axis=1).astype(np.float32)


def _rope_tables():
    c32, s32 = _rope32_tables()
    tile = lambda a, n: np.tile(a, (1, n))
    pad = np.zeros((DEC_SEQ, LANES - MLA_ROPE), np.float32)
    cos_k = np.concatenate([c32, pad + 1.0], axis=1)
    sin_k = np.concatenate([s32, pad], axis=1)
    return (tile(c32, 4), tile(s32, 4),
            cos_k, sin_k)


def kernel(x_prompt, x_sample, cache_na_k, cache_na_v, cache_diff_k, cache_diff_v, cache_mla_ckv, cache_mla_kpe, c, c_ctx, w_ada, b_ada, g_mix, g_ffn, w_in, w_out, na_rpb, diff_lq1, diff_lk1, diff_lq2, diff_lk2, diff_g_subln, mla_g_ckv, mla_w_uk, mla_w_uv, sgu_g, sgu_w, sgu_b, w_ff1, w_ff2, g_final):
    f32 = jnp.float32
    m = jnp.concatenate([c_ctx[None, :], c, jnp.zeros((N_MOD_ROWS - 1 - DEC_BATCH, D_MODEL), f32)], axis=0)
    mod = _ada(m, w_ada, b_ada).reshape(DEPTH, N_MOD_ROWS, 1, 6 * D_MODEL)
    cst = _lam_consts(diff_lq1, diff_lk1, diff_lq2, diff_lk2)
    tt = _bias_tiles(na_rpb)
    cos4, sin4, cos_k, sin_k = [jnp.asarray(t) for t in _rope_tables()]
    tables = (cos4, sin4, cos4, sin4, cos_k, sin_k)

    t_last = lambda a: jnp.swapaxes(a, -1, -2)
    w_in_t = t_last(w_in)
    caches_t = (t_last(cache_na_k), t_last(cache_na_v), t_last(cache_diff_k), t_last(cache_diff_v),
                cache_mla_ckv, t_last(cache_mla_kpe))
    w_out_b, w1_b, w2_b = _bf(w_out), _bf(w_ff1), _bf(w_ff2)
    g_mix3 = g_mix.reshape(DEPTH, 1, D_MODEL)
    g_ffn3 = g_ffn.reshape(DEPTH, 1, D_MODEL)
    g_sub2 = jnp.tile(diff_g_subln, (1, 2)).reshape(DEPTH, 1, LANES)
    g_ckv3 = mla_g_ckv.reshape(DEPTH, 1, MLA_KV_RANK)
    sgu_g3 = sgu_g.reshape(DEPTH, 1, WIDTH)
    sgu_bt = sgu_b.transpose(0, 2, 1)
    g_fin2 = g_final.reshape(1, D_MODEL)

    xs = (x_prompt.reshape(N_PROMPT, D_MODEL), x_sample.reshape(N_SAMPLE, D_MODEL))
    new = ()
    for l in range(DEPTH):
        pa, pb, pc, od = _inproj(l, xs, g_mix3, mod, w_in_t, sgu_g3, sgu_w, sgu_bt)
        o_p, *new = _mix_prompt(l, pa, pb, pc, cst, g_sub2, g_ckv3, mla_w_uk, mla_w_uv, new)
        o_s = _mix_sample(l, pa, pb, pc, caches_t, tt, tables, cst, g_sub2, g_ckv3, mla_w_uk, mla_w_uv)
        ffn = functools.partial(_outffn, l, xs, o_p, o_s, od, w_out_b, g_ffn3, mod, w1_b, w2_b, g_fin2)
        if l < DEPTH - 1:
            xs = (ffn(0, TILES_PROMPT + TILES_SAMPLE),)
        else:
            xs = (ffn(0, TILES_PROMPT), ffn(TILES_PROMPT, TILES_SAMPLE))
    y_prompt = xs[0].reshape(BATCH, SEQ, D_MODEL)
    y_sample = xs[1].reshape(DEC_BATCH, DEC_SEQ, D_MODEL)
    na_k, na_v, diff_k, diff_v, mla_ckv, mla_kpe = new
    return (y_prompt, y_sample, t_last(na_k), t_last(na_v), t_last(diff_k), t_last(diff_v), mla_ckv, t_last(mla_kpe))
```

```python
import functools
import math

import numpy as np
import jax
import jax.numpy as jnp
from jax import lax
from jax.experimental import pallas as pl
from jax.experimental.pallas import tpu as pltpu

D_MODEL = 1024
BATCH = 16
SEQ = 256
DEPTH = 4
DEC_BATCH = 2
DEC_SEQ = 1024
PAST_LEN = 512
GRID_W = 64
GRID_ROWS = DEC_SEQ // GRID_W
HEAD_DIM = 64
NA_HEADS = 4
NA_WIN_ROWS = 8
NA_WIN_COLS = 16
DIFF_HEADS = 4
DIFF_QK_DIM = 32
DIFF_V_DIM = 64
MLA_HEADS = 4
MLA_NOPE = 64
MLA_ROPE = 32
MLA_V = 64
MLA_KV_RANK = 128
SGU_GROUPS = 4
SGU_GROUP_DIM = 64
SGU_CHUNK = 128
D_FF = 4 * D_MODEL
ROPE_BASE = 10000.0
EPS = 1e-6
NEG_INF = -1e30
LOG2E = 1.4426950408889634

N_HEADS = 4
N_PAIRS = N_HEADS // 2
LANES = 128
WIDTH = 256
N_PROMPT = BATCH * SEQ
N_SAMPLE = DEC_BATCH * DEC_SEQ
N_TOK = N_PROMPT + N_SAMPLE
N_MOD_ROWS = 8

SEG_A = 3 * WIDTH
SEG_B = 3 * WIDTH
SEG_C = 640
SEG_D = 2 * WIDTH
SEG_C_PAD = 96
IN_COLS_P = SEG_A + SEG_B + SEG_C + SEG_D
O_ATT = 3 * WIDTH

TM = 512
TILES_PROMPT = N_PROMPT // TM
TILES_SAMPLE = N_SAMPLE // TM
PB = 2
QB = 256
QB_EXACT = 64
VMEM_LIMIT = 56 * 1024 * 1024


def _bf(x):
    return x.astype(jnp.bfloat16)


def _dot(a, b):
    return jnp.dot(a, b, preferred_element_type=jnp.float32)


def _dot_nt(a, b):
    return lax.dot_general(a, b, (((1,), (1,)), ((), ())), preferred_element_type=jnp.float32)


def _rms(x, g):
    ms = jnp.mean(x * x, axis=-1, keepdims=True)
    return x * lax.rsqrt(ms + EPS) * g


def _lane_range(lo, hi, width=LANES):
    lane = lax.broadcasted_iota(jnp.int32, (1, width), 1)
    return (lane >= lo) & (lane < hi)


def _with_ones(v):
    return jnp.concatenate([v, jnp.ones((v.shape[0], LANES), jnp.bfloat16)], axis=1)


def _attend(scores, values, log=None, bound=None):
    if bound is None:
        m = functools.reduce(jnp.maximum, [jnp.max(s, axis=-1, keepdims=True) for s in scores])
        shifted = [s - m for s in scores]
    else:
        over = jnp.max(bound - scores[0][:, 0:LANES], axis=0, keepdims=True)
        log.worst = jnp.maximum(log.worst, jnp.min(over, axis=1, keepdims=True))
        shifted = [s - bound for s in scores]
    return functools.reduce(lambda a, b: a + b, [_dot(_bf(jnp.exp2(s)), v) for s, v in zip(shifted, values)])


BOUND_SLACK = 1.02
OVERSHOOT_LIMIT = 100.0


class _ShiftLog:
    def __init__(self, worst=None):
        self.worst = jnp.zeros((1, 1), jnp.float32) if worst is None else worst

    def unsafe(self):
        return jnp.logical_not(self.worst[0, 0] < OVERSHOOT_LIMIT)


def _group_matrix(width, n_groups, extra=None):
    i = lax.broadcasted_iota(jnp.int32, (width, LANES), 0)
    j = lax.broadcasted_iota(jnp.int32, (width, LANES), 1)
    size = LANES // n_groups
    hit = (i // size == j // size) & (i < LANES)
    for t in range(n_groups if extra else 0):
        lo, hi = extra(t)
        hit = hit | ((i >= lo) & (i < hi) & (j // size == t))
    return jnp.where(hit, 1.0, 0.0).astype(jnp.bfloat16)


def _squares(x):
    xf = x.astype(jnp.float32)
    return _bf(xf * xf)


def _key_bound(keys, groups):
    return functools.reduce(jnp.maximum, [jnp.max(_dot(_squares(k), groups), axis=0, keepdims=True) for k in keys])


def _bound(log, q_parts, k2max, n_groups, extra=0.0):
    if log is None:
        return None
    size = LANES // n_groups
    q_norm = jnp.sqrt(functools.reduce(lambda a, b: a + b,
                                       [jnp.sum(q * q, axis=-1, keepdims=True) for q in q_parts])) * BOUND_SLACK
    return jnp.concatenate([q_norm * jnp.sqrt(k2max[:, size * t:size * t + 1]) + extra for t in range(n_groups)], axis=0)


def _loop(n, log, body):
    if log is not None:
        for i in range(n):
            body(i, log)
    else:
        lax.fori_loop(0, n, lambda i, carry: body(i, None) or carry, 0)


def _aligned(start, multiple):
    return start if isinstance(start, int) else pl.multiple_of(start, multiple)


def _normalised(o_ext):
    return o_ext[:, 0:LANES] * (1.0 / o_ext[:, LANES:2 * LANES])


def _swap8(x):
    lane = lax.broadcasted_iota(jnp.int32, (1, LANES), 1)
    return jnp.where((lane & 15) < 8, pltpu.roll(x, LANES - 8, 1), pltpu.roll(x, 8, 1))


def _rope(x, cos, sin):
    outs = []
    for c in range(x.shape[1] // LANES):
        sl = slice(LANES * c, LANES * (c + 1))
        xc = x[:, sl]
        outs.append(xc * cos[:, sl] + _swap8(xc) * sin[:, sl])
    return outs[0] if len(outs) == 1 else jnp.concatenate(outs, axis=1)


def _tile4(x):
    return x + pltpu.roll(x, 32, 1) + pltpu.roll(x, 64, 1) + pltpu.roll(x, 96, 1)


def _params(*sem):
    return pltpu.CompilerParams(dimension_semantics=sem, vmem_limit_bytes=VMEM_LIMIT)


ADA_TN = 1536


def _ada_kernel(m_ref, w_ref, b_ref, o_ref):
    m = m_ref[...]
    s = m * jax.nn.sigmoid(m)
    o_ref[0] = _dot(_bf(s), _bf(w_ref[0])) + b_ref[0]


def _ada(m, w_ada, b_ada):
    n = 6 * D_MODEL
    return pl.pallas_call(
        _ada_kernel,
        grid=(DEPTH, n // ADA_TN),
        in_specs=[
            pl.BlockSpec((N_MOD_ROWS, D_MODEL), lambda l, j: (0, 0)),
            pl.BlockSpec((1, D_MODEL, ADA_TN), lambda l, j: (l, 0, j)),
            pl.BlockSpec((1, 1, ADA_TN), lambda l, j: (l, 0, j)),
        ],
        out_specs=pl.BlockSpec((1, N_MOD_ROWS, ADA_TN), lambda l, j: (l, 0, j)),
        out_shape=jax.ShapeDtypeStruct((DEPTH, N_MOD_ROWS, n), jnp.float32),
        compiler_params=_params("parallel", "parallel"),
        name="ada",
    )(m, w_ada, b_ada.reshape(DEPTH, 1, n))


def _lam_kernel(lq1_ref, lk1_ref, lq2_ref, lk2_ref, init_ref, o_ref):
    init = init_ref[...]
    a = jnp.exp(jnp.sum(lq1_ref[...] * lk1_ref[...], axis=-1, keepdims=True))
    b = jnp.exp(jnp.sum(lq2_ref[...] * lk2_ref[...], axis=-1, keepdims=True))
    lam = a - b + init
    post = 1.0 - init
    for l in range(DEPTH):
        o_ref[l, 0:1, :] = jnp.broadcast_to(lam[l:l + 1], (1, LANES))
        o_ref[l, 1:2, :] = jnp.broadcast_to(post[l:l + 1], (1, LANES))


def _lam_consts(lq1, lk1, lq2, lk2):
    init = np.array([[0.8 - 0.6 * math.exp(-0.3 * l)] for l in range(DEPTH)], np.float32)
    return pl.pallas_call(
        _lam_kernel,
        out_shape=jax.ShapeDtypeStruct((DEPTH, 2, LANES), jnp.float32),
        name="diff_lambda",
    )(lq1, lk1, lq2, lk2, jnp.asarray(init))


N_DROW = 2 * NA_WIN_ROWS - 1
N_DCOL = 2 * NA_WIN_COLS - 1


def _bias_kernel(rpb_ref, o_ref):
    l = pl.program_id(0)
    h = pl.program_id(1)
    base = (l * NA_HEADS + h) * (N_DROW * N_DCOL)
    cq = lax.broadcasted_iota(jnp.int32, (GRID_W, LANES), 0)
    lane = lax.broadcasted_iota(jnp.int32, (GRID_W, LANES), 1)
    ck = lane & (GRID_W - 1)
    dcol = jnp.clip(ck - cq, -(NA_WIN_COLS - 1), NA_WIN_COLS - 1) + (NA_WIN_COLS - 1)
    hi = lane >= GRID_W
    for a in range(N_DROW - 1):
        acc = jnp.zeros((GRID_W, LANES), jnp.float32)
        for j in range(N_DCOL):
            lo_v = rpb_ref[base + a * N_DCOL + j]
            hi_v = rpb_ref[base + (a + 1) * N_DCOL + j]
            acc = jnp.where(dcol == j, jnp.where(hi, hi_v, lo_v), acc)
        o_ref[0, 0, a] = acc * LOG2E


def _bias_tiles(na_rpb):
    return pl.pallas_call(
        _bias_kernel,
        grid=(DEPTH, NA_HEADS),
        in_specs=[pl.BlockSpec(memory_space=pltpu.SMEM)],
        out_specs=pl.BlockSpec((1, 1, N_DROW - 1, GRID_W, LANES), lambda l, h: (l, h, 0, 0, 0)),
        out_shape=jax.ShapeDtypeStruct((DEPTH, NA_HEADS, N_DROW - 1, GRID_W, LANES), jnp.float32),
        compiler_params=_params("parallel", "parallel"),
        name="na_bias_tiles",
    )(na_rpb.reshape(-1))


def _row_group(i):
    return jnp.where(i < TILES_PROMPT, 0, 1 + (i - TILES_PROMPT) // (DEC_SEQ // TM))


def _split_specs(width, first):
    return [pl.BlockSpec((TM, width), lambda i: (jnp.minimum(first + i, TILES_PROMPT - 1), 0)),
            pl.BlockSpec((TM, width), lambda i: (jnp.maximum(first + i - TILES_PROMPT, 0), 0))]


def _x_specs(split, first):
    if not split:
        return [pl.BlockSpec((TM, D_MODEL), lambda i: (first + i, 0))]
    return _split_specs(D_MODEL, first)


def _read_tile(refs, first):
    if len(refs) == 1:
        return refs[0][...]
    return jnp.where(first + pl.program_id(0) < TILES_PROMPT, refs[0][...], refs[1][...])


IN_COLS = 2592
IN_QC, IN_CKV, IN_D = 1536, 1920, 2080
TR_ROWS = 256


def _gelu_tanh(x):
    return 0.5 * x * (1.0 + jnp.tanh(math.sqrt(2.0 / math.pi) * (x + 0.044715 * (x * x * x))))


def _sgu(pd, g, w_ref, bt):
    u = _gelu_tanh(pd[:, 0:WIDTH])
    v = _gelu_tanh(pd[:, WIDTH:2 * WIDTH])
    grp = lax.broadcasted_iota(jnp.int32, (1, WIDTH), 1) // SGU_GROUP_DIM
    v2 = v * v
    ms = jnp.zeros_like(v)
    for gi in range(SGU_GROUPS):
        sel = grp == gi
        tot = jnp.sum(jnp.where(sel, v2, 0.0), axis=-1, keepdims=True)
        ms = jnp.where(sel, tot * (1.0 / SGU_GROUP_DIM), ms)
    vg = _bf(v * lax.rsqrt(ms + EPS) * g)
    outs = []
    for c in range(pd.shape[0] // SGU_CHUNK):
        rows = slice(SGU_CHUNK * c, SGU_CHUNK * (c + 1))
        mixed = jnp.zeros((SGU_CHUNK, WIDTH), jnp.float32)
        for gi in range(SGU_GROUPS):
            full = _dot(_bf(w_ref[0, gi]), vg[rows]) + bt[:, gi:gi + 1]
            mixed = jnp.where(grp == gi, full, mixed)
        outs.append(u[rows] * mixed)
    return jnp.concatenate(outs, axis=0)


def _w_in_row_pieces():
    qn = [(IN_QC + 96 * h, MLA_NOPE) for h in range(MLA_HEADS)]
    qp = [(IN_QC + 96 * h + MLA_NOPE, MLA_ROPE) for h in range(MLA_HEADS)]
    seg_c = qn + qp + [(IN_CKV, MLA_KV_RANK + MLA_ROPE)]
    return (0, SEG_A + SEG_B), seg_c, (IN_D, SEG_D)


def _load_w_in(wt_ref, w_scr):
    ab, seg_c, d = _w_in_row_pieces()
    c_rows = jnp.concatenate([wt_ref[0, s:s + n, :] for s, n in seg_c]
                             + [jnp.zeros((SEG_C_PAD, D_MODEL), jnp.float32)], axis=0)
    for t in range(SEG_C // LANES):
        w_scr[:, SEG_A + SEG_B + LANES * t:SEG_A + SEG_B + LANES * (t + 1)] = _bf(c_rows[LANES * t:LANES * (t + 1)].T)
    for (src, n), dst in ((ab, 0), (d, SEG_A + SEG_B + SEG_C)):
        for t in range(n // TR_ROWS):
            rows = wt_ref[0, src + TR_ROWS * t:src + TR_ROWS * (t + 1), :]
            w_scr[:, dst + TR_ROWS * t:dst + TR_ROWS * (t + 1)] = _bf(rows.T)


def _inproj_kernel(n_x, *refs):
    x_refs = refs[:n_x]
    (g_ref, sh_ref, sc_ref, wt_ref, sg_ref, sw_ref, sbt_ref, pa_ref, pb_ref, pc_ref, od_ref, w_scr) = refs[n_x:]

    @pl.when(pl.program_id(0) == 0)
    def _():
        _load_w_in(wt_ref, w_scr)

    h = _rms(_read_tile(x_refs, 0), g_ref[0]) * (1.0 + sc_ref[...]) + sh_ref[...]
    hb = _bf(h)
    off = SEG_A + SEG_B + SEG_C
    od_ref[...] = _sgu(_dot(hb, w_scr[:, off:off + SEG_D]), sg_ref[0], sw_ref, sbt_ref[0])
    off = 0
    for ref in (pa_ref, pb_ref, pc_ref):
        n = ref.shape[1]
        ref[...] = _dot(hb, w_scr[:, off:off + n])
        off += n


def _inproj(l, xs, g_mix, mod, w_in_t, sgu_g, sgu_w, sgu_bt):
    def mod_spec(j):
        return pl.BlockSpec((None, None, 1, D_MODEL), lambda i: (l, _row_group(i), 0, j))

    widths = (SEG_A, SEG_B, SEG_C, WIDTH)
    return pl.pallas_call(
        functools.partial(_inproj_kernel, len(xs)),
        grid=(N_TOK // TM,),
        in_specs=_x_specs(len(xs) == 2, 0) + [
            pl.BlockSpec((1, 1, D_MODEL), lambda i: (l, 0, 0)),
            mod_spec(0), mod_spec(1),
            pl.BlockSpec((1, IN_COLS, D_MODEL), lambda i: (l, 0, 0), pipeline_mode=pl.Buffered(1)),
            pl.BlockSpec((1, 1, WIDTH), lambda i: (l, 0, 0)),
            pl.BlockSpec((1, SGU_GROUPS, SGU_CHUNK, SGU_CHUNK), lambda i: (l, 0, 0, 0)),
            pl.BlockSpec((1, SGU_CHUNK, SGU_GROUPS), lambda i: (l, 0, 0)),
        ],
        out_specs=[pl.BlockSpec((TM, n), lambda i: (i, 0)) for n in widths],
        out_shape=[jax.ShapeDtypeStruct((N_TOK, n), jnp.float32) for n in widths],
        scratch_shapes=[pltpu.VMEM((D_MODEL, IN_COLS_P), jnp.bfloat16)],
        compiler_params=_params("arbitrary"),
        name="inproj",
    )(*xs, g_mix, mod, mod, w_in_t, sgu_g, sgu_w, sgu_bt)


C_QN, C_QP, C_CKV, C_KPE = 0, 256, 384, 512


def _stack_heads(qp):
    lo = _lane_range(0, 64)
    return jnp.concatenate([_bf(jnp.where(lo, qp, 0.0)), _bf(jnp.where(lo, 0.0, qp))], axis=0)


def _unstack_heads(o, n):
    return jnp.where(_lane_range(0, 64), o[0:n], o[n:2 * n])


def _pair_t(c_ref):
    return jnp.concatenate([c_ref[0], c_ref[1]], axis=0)


def _stack_components(qp):
    return jnp.concatenate([_bf(jnp.where(_lane_range(32 * t, 32 * (t + 1)), qp, 0.0)) for t in range(4)], axis=0)


def _group_mean_sq(x, groups, size):
    sq = x * x
    hi = _bf(sq)
    rest = sq - hi.astype(jnp.float32)
    mid = _bf(rest)
    lo = _bf(rest - mid.astype(jnp.float32))
    return (_dot(hi, groups) + _dot(mid, groups) + _dot(lo, groups)) * (1.0 / size)


def _diff_finish(o, n, lam, post, g2, by_head):
    den = o[:, LANES:2 * LANES]
    outs = []
    for t in range(2):
        p1 = o[2 * t * n:(2 * t + 1) * n, 0:LANES] * (1.0 / den[2 * t * n:(2 * t + 1) * n])
        p2 = o[(2 * t + 1) * n:(2 * t + 2) * n, 0:LANES] * (lam / den[(2 * t + 1) * n:(2 * t + 2) * n])
        outs.append(p1 - p2)
    d = jnp.where(_lane_range(0, 64), outs[0], outs[1])
    return d * lax.rsqrt(_group_mean_sq(d, by_head, DIFF_V_DIM) + EPS) * g2 * post


def _mla_groups(j):
    return _group_matrix(2 * LANES, 2,
                         lambda t: (LANES + MLA_ROPE * (2 * j + t), LANES + MLA_ROPE * (2 * j + t + 1)))


def _mla_queries(qn_pair, qp_all, j):
    halves = []
    for t in range(2):
        h = 2 * j + t
        halves.append(jnp.concatenate([
            _bf(jnp.where(_lane_range(64 * t, 64 * (t + 1)), qn_pair, 0.0)),
            _bf(jnp.where(_lane_range(MLA_ROPE * h, MLA_ROPE * (h + 1)), qp_all, 0.0))], axis=1))
    return jnp.concatenate(halves, axis=0)


def _write_heads_t(p_ref, rows, col0, out_ref, bb):
    xt = p_ref[rows, col0:col0 + WIDTH].T
    for h in range(N_HEADS):
        out_ref[bb, 0, h] = xt[64 * h:64 * (h + 1)]
    _clear_other_layers(out_ref, bb)


def _clear_other_layers(out_ref, bb):
    if out_ref.shape[1] > 1:
        out_ref[bb, 1:] = jnp.zeros(out_ref.shape[1:], jnp.float32)[1:]


def _mix_prompt_kernel(n_prev, *refs):
    ins, outs = refs[:8], refs[8 + n_prev:]
    log = _ShiftLog()
    _mix_prompt_pass(ins, outs, log)

    @pl.when(log.unsafe())
    def _():
        _mix_prompt_pass(ins, outs, None)


def _mix_prompt_pass(ins, outs, log):
    pa_ref, pb_ref, pc_ref, cst_ref, gsub_ref, gckv_ref, wuk_ref, wuv_ref = ins
    o_ref, nak_ref, nav_ref, dk_ref, dv_ref, ckv_ref, kpe_ref = outs
    first_pass = log is not None
    c_a = HEAD_DIM ** -0.5 * LOG2E
    c_b = DIFF_QK_DIM ** -0.5 * LOG2E
    c_c = (MLA_NOPE + MLA_ROPE) ** -0.5 * LOG2E
    lam = cst_ref[0, 0:1, 0:1]
    post = cst_ref[0, 1:2, 0:1]
    wuk, wuv = _bf(wuk_ref[0]), _bf(wuv_ref[0])
    by_head, by_comp = _group_matrix(LANES, 2), _group_matrix(LANES, 4)

    def sequence(bb, log):
        rows = pl.ds(_aligned(bb * SEQ, SEQ), SEQ)
        for j in range(N_PAIRS):
            cols = slice(LANES * j, LANES * (j + 1))
            k = _bf(pa_ref[rows, WIDTH + LANES * j:WIDTH + LANES * (j + 1)])
            v = _with_ones(_bf(pa_ref[rows, 2 * WIDTH + LANES * j:2 * WIDTH + LANES * (j + 1)]))
            q = pa_ref[rows, cols] * c_a
            o = _attend([_dot_nt(_stack_heads(q), k)], [v], log, _bound(log, [q], _key_bound([k], by_head), 2))
            o_ref[rows, cols] = _unstack_heads(_normalised(o), SEQ)
        if first_pass:
            _write_heads_t(pa_ref, rows, WIDTH, nak_ref, bb)
            _write_heads_t(pa_ref, rows, 2 * WIDTH, nav_ref, bb)
        for j in range(N_PAIRS):
            cols = slice(LANES * j, LANES * (j + 1))
            k = _bf(pb_ref[rows, WIDTH + LANES * j:WIDTH + LANES * (j + 1)])
            v = _with_ones(_bf(pb_ref[rows, 2 * WIDTH + LANES * j:2 * WIDTH + LANES * (j + 1)]))
            q = pb_ref[rows, cols] * c_b
            o = _attend([_dot_nt(_stack_components(q), k)], [v], log, _bound(log, [q], _key_bound([k], by_comp), 4))
            o_ref[rows, WIDTH + LANES * j:WIDTH + LANES * (j + 1)] = _diff_finish(o, SEQ, lam, post, gsub_ref[0], by_head)
        if first_pass:
            _write_heads_t(pb_ref, rows, WIDTH, dk_ref, bb)
            _write_heads_t(pb_ref, rows, 2 * WIDTH, dv_ref, bb)
        ckv = _rms(pc_ref[rows, C_CKV:C_CKV + MLA_KV_RANK], gckv_ref[0])
        kpe_slot = pc_ref[rows, C_KPE:C_KPE + LANES]
        if first_pass:
            ckv_ref[bb, 0] = ckv
            _clear_other_layers(ckv_ref, bb)
            kpe_ref[bb, 0] = kpe_slot.T[0:MLA_ROPE]
            _clear_other_layers(kpe_ref, bb)
        ckv_b = _bf(ckv)
        kn = _bf(_dot(ckv_b, wuk))
        vv = _bf(_dot(ckv_b, wuv))
        kpe4 = _bf(_tile4(kpe_slot))
        qn = pc_ref[rows, C_QN:C_QN + WIDTH] * c_c
        qp = pc_ref[rows, C_QP:C_QP + LANES] * c_c
        for j in range(N_PAIRS):
            cols = slice(LANES * j, LANES * (j + 1))
            k = jnp.concatenate([kn[:, cols], kpe4], axis=1)
            qs = _mla_queries(qn[:, cols], qp, j)
            groups = _mla_groups(j)
            o = _attend([_dot_nt(qs, k)], [_with_ones(vv[:, cols])], log,
                        _bound(log, [qn[:, cols], qp], _key_bound([k], groups), 2))
            o_ref[rows, 2 * WIDTH + LANES * j:2 * WIDTH + LANES * (j + 1)] = _unstack_heads(_normalised(o), SEQ)

    _loop(PB, log, sequence)


def _mix_prompt(l, pa, pb, pc, cst, g_sub2, g_ckv, w_uk, w_uv, prev):
    n_prev = len(prev)
    tails = [(NA_HEADS, HEAD_DIM, SEQ)] * 2 + [(DIFF_HEADS, 64, SEQ)] * 2 + [(SEQ, MLA_KV_RANK), (MLA_ROPE, SEQ)]

    def cache_spec(tail):
        if l == 0:
            return pl.BlockSpec((PB, DEPTH) + tail, lambda b: (b, 0) + (0,) * len(tail))
        return pl.BlockSpec((PB, 1) + tail, lambda b: (b, l) + (0,) * len(tail))

    def rows(width):
        return pl.BlockSpec((PB * SEQ, width), lambda b: (b, 0))

    def layer(*tail):
        return pl.BlockSpec((1,) + tail, lambda b: (l,) + (0,) * len(tail))

    return pl.pallas_call(
        functools.partial(_mix_prompt_kernel, n_prev),
        grid=(BATCH // PB,),
        in_specs=[rows(SEG_A), rows(SEG_B), rows(SEG_C), layer(2, LANES), layer(1, LANES), layer(1, MLA_KV_RANK),
                  layer(MLA_KV_RANK, WIDTH), layer(MLA_KV_RANK, WIDTH)] + [pl.BlockSpec(memory_space=pl.ANY)] * n_prev,
        out_specs=[rows(O_ATT)] + [cache_spec(t) for t in tails],
        out_shape=[jax.ShapeDtypeStruct((N_PROMPT, O_ATT), jnp.float32)]
        + [jax.ShapeDtypeStruct((BATCH, DEPTH) + t, jnp.float32) for t in tails],
        input_output_aliases={8 + i: 1 + i for i in range(n_prev)},
        compiler_params=_params("parallel"),
        name="mix_prompt",
    )(pa, pb, pc, cst, g_sub2, g_ckv, w_uk, w_uv, *prev)


def _na_row_groups():
    kh = min(NA_WIN_ROWS, GRID_ROWS)
    r0s = [min(max(r - kh // 2, 0), GRID_ROWS - kh) for r in range(GRID_ROWS)]
    groups = []
    for r, r0 in enumerate(r0s):
        if groups and groups[-1][2] == r0:
            groups[-1][1] = r
        else:
            groups.append([r, r, r0])
    return kh, [tuple(g) for g in groups]


def _na_sample(q_ref, k_ref, v_ref, ck_ref, cv_ref, tt_ref, o_ref, log):
    c = HEAD_DIM ** -0.5 * LOG2E
    kh, groups = _na_row_groups()
    lk = kh * GRID_W
    edge = [g for g in groups if g[1] > g[0]]
    inner = [g for g in groups if g[1] == g[0]]
    depth = inner[0][0] - inner[0][2]
    assert all(g[0] - g[2] == depth for g in inner) and [g[0] for g in inner] == list(range(inner[0][0], inner[-1][0] + 1))

    def in_window(n):
        cq = lax.broadcasted_iota(jnp.int32, (n, lk), 0) & (GRID_W - 1)
        ck = lax.broadcasted_iota(jnp.int32, (n, lk), 1) & (GRID_W - 1)
        c0 = jnp.clip(cq - NA_WIN_COLS // 2, 0, GRID_W - NA_WIN_COLS)
        return (ck >= c0) & (ck < c0 + NA_WIN_COLS)

    kc_t = _bf(_pair_t(ck_ref))
    vc = _with_ones(_bf(_pair_t(cv_ref).T))
    by_head = _group_matrix(LANES, 2)
    k2max = _key_bound([_bf(k_ref[...]), _bf(_pair_t(ck_ref).T)], by_head)
    tmax = functools.reduce(jnp.maximum, [tt_ref[t, a] for t in range(2) for a in range(N_DROW - 1)])
    bplus = jnp.maximum(jnp.max(jnp.max(tmax, axis=-1, keepdims=True), axis=0, keepdims=True), 0.0)

    def group(row0, key0, offsets, log):
        n = len(offsets) * GRID_W
        rows, keys = pl.ds(row0, n), pl.ds(key0, lk)
        q = q_ref[rows, :] * c
        qg = _stack_heads(q)
        k = _bf(k_ref[keys, :])
        v = _with_ones(_bf(v_ref[keys, :]))
        bias = jnp.concatenate([
            jnp.concatenate([tt_ref[t, 2 * i - off + NA_WIN_ROWS - 1] for i in range(kh // 2)], axis=1)
            for t in range(2) for off in offsets], axis=0)
        s_loc = jnp.where(in_window(2 * n), _dot_nt(qg, k) + bias, NEG_INF)
        o = _attend([_dot(qg, kc_t), s_loc], [vc, v], log, _bound(log, [q], k2max, 2, bplus))
        o_ref[rows, :] = _unstack_heads(_normalised(o), n)

    if log is None:
        def any_row(r, log):
            r0 = jnp.clip(r - kh // 2, 0, GRID_ROWS - kh)
            group(_aligned(r * GRID_W, GRID_W), _aligned(r0 * GRID_W, GRID_W), [r - r0], log)

        _loop(GRID_ROWS, log, any_row)
        return

    for (r_lo, r_hi, r0) in edge:
        group(r_lo * GRID_W, r0 * GRID_W, [r - r0 for r in range(r_lo, r_hi + 1)], log)

    def inner_row(i, log):
        r = inner[0][0] + i
        group(r * GRID_W, (r - depth) * GRID_W, [depth], log)

    _loop(len(inner), log, inner_row)


def _diff_sample(q_ref, k_ref, v_ref, ck_ref, cv_ref, cos_ref, sin_ref, cst_ref, g_ref, o_ref, log):
    c = DIFF_QK_DIM ** -0.5 * LOG2E
    lam = cst_ref[0, 0:1, 0:1]
    post = cst_ref[0, 1:2, 0:1]
    k_new = _bf(_rope(k_ref[...], cos_ref[...], sin_ref[...]))
    kc_t = _bf(_pair_t(ck_ref))
    vc = _with_ones(_bf(_pair_t(cv_ref).T))
    v = _with_ones(_bf(v_ref[...]))
    by_head, by_comp = _group_matrix(LANES, 2), _group_matrix(LANES, 4)
    k2max = _key_bound([k_new, _bf(_pair_t(ck_ref).T)], by_comp)

    qb = QB_EXACT if log is None else QB

    def block(qi, log):
        rows = pl.ds(_aligned(qi * qb, qb), qb)
        q = _rope(q_ref[rows, :], cos_ref[rows, :], sin_ref[rows, :]) * c
        qs = _stack_components(q)
        o = _attend([_dot(qs, kc_t), _dot_nt(qs, k_new)], [vc, v], log, _bound(log, [q], k2max, 4))
        o_ref[rows, :] = _diff_finish(o, qb, lam, post, g_ref[0], by_head)

    _loop(DEC_SEQ // qb, log, block)


def _mla_sample(j, qn_ref, qp_ref, ckv_ref, kpe_ref, cckv_ref, ckpe_ref, cosq_ref, sinq_ref, cosk_ref, sink_ref,
                gckv_ref, wuk_ref, wuv_ref, o_ref, log):
    c = (MLA_NOPE + MLA_ROPE) ** -0.5 * LOG2E
    wuk, wuv = _bf(wuk_ref[0]), _bf(wuv_ref[0])
    ckv_new = _bf(_rms(ckv_ref[...], gckv_ref[0]))
    ckv_old = _bf(cckv_ref[...])
    kpe_new = _bf(_tile4(_rope(kpe_ref[...], cosk_ref[...], sink_ref[...])))
    kpe_old = _bf(jnp.concatenate([ckpe_ref[...]] * MLA_HEADS, axis=0).T)
    k_old = jnp.concatenate([_bf(_dot(ckv_old, wuk)), kpe_old], axis=1)
    k_new = jnp.concatenate([_bf(_dot(ckv_new, wuk)), kpe_new], axis=1)
    vo, vn = _with_ones(_bf(_dot(ckv_old, wuv))), _with_ones(_bf(_dot(ckv_new, wuv)))
    groups = _mla_groups(j)
    k2max = _key_bound([k_old, k_new], groups)

    qb = QB_EXACT if log is None else QB

    def block(qi, log):
        rows = pl.ds(_aligned(qi * qb, qb), qb)
        qp = _rope(qp_ref[rows, :], cosq_ref[rows, :], sinq_ref[rows, :]) * c
        qn = qn_ref[rows, :] * c
        qs = _mla_queries(qn, qp, j)
        o = _attend([_dot_nt(qs, k_old), _dot_nt(qs, k_new)], [vo, vn], log, _bound(log, [qn, qp], k2max, 2))
        o_ref[rows, :] = _unstack_heads(_normalised(o), qb)

    _loop(DEC_SEQ // qb, log, block)


def _mix_sample_kernel(qa_ref, ka_ref, va_ref, qb_ref, kb_ref, vb_ref, qn_ref, qp_ref, ckv_ref, kpe_ref,
                       cnak_ref, cnav_ref, cdk_ref, cdv_ref, cckv_ref, ckpe_ref, tt_ref,
                       cosb_ref, sinb_ref, cosq_ref, sinq_ref, cosk_ref, sink_ref,
                       cst_ref, gsub_ref, gckv_ref, wuk_ref, wuv_ref, oa_ref, ob_ref, oc_ref):
    j = pl.program_id(1)

    def run(log):
        _na_sample(qa_ref, ka_ref, va_ref, cnak_ref, cnav_ref, tt_ref, oa_ref, log)
        _diff_sample(qb_ref, kb_ref, vb_ref, cdk_ref, cdv_ref, cosb_ref, sinb_ref, cst_ref, gsub_ref, ob_ref, log)
        _mla_sample(j, qn_ref, qp_ref, ckv_ref, kpe_ref, cckv_ref, ckpe_ref, cosq_ref, sinq_ref, cosk_ref, sink_ref,
                    gckv_ref, wuk_ref, wuv_ref, oc_ref, log)

    log = _ShiftLog()
    run(log)

    @pl.when(log.unsafe())
    def _():
        run(None)


def _mix_sample(l, pa, pb, pc, caches_t, tt, tables, cst, g_sub2, g_ckv, w_uk, w_uv):
    first = N_PROMPT // DEC_SEQ

    def cols(block):
        return pl.BlockSpec((DEC_SEQ, LANES), lambda b, j: (first + b, block(j)))

    def cache(*tail, pair=False):
        return pl.BlockSpec((None, None) + tail, lambda b, j: (b, l, j if pair else 0) + (0,) * (len(tail) - 1))

    def layer(*tail):
        return pl.BlockSpec((1,) + tail, lambda b, j: (l,) + (0,) * len(tail))

    table = pl.BlockSpec((DEC_SEQ, LANES), lambda b, j: (0, 0), pipeline_mode=pl.Buffered(1))
    qkv = [cols(lambda j: j), cols(lambda j: N_PAIRS + j), cols(lambda j: 2 * N_PAIRS + j)]
    seg_c = [cols(lambda j: j), cols(lambda j: C_QP // LANES), cols(lambda j: C_CKV // LANES), cols(lambda j: C_KPE // LANES)]
    kv_t = cache(2, 64, PAST_LEN, pair=True)
    w_pair = pl.BlockSpec((1, MLA_KV_RANK, LANES), lambda b, j: (l, 0, j))
    out = pl.BlockSpec((DEC_SEQ, LANES), lambda b, j: (b, j))
    return pl.pallas_call(
        _mix_sample_kernel,
        grid=(DEC_BATCH, N_PAIRS),
        in_specs=qkv + qkv + seg_c + [kv_t, kv_t, kv_t, kv_t, cache(PAST_LEN, MLA_KV_RANK), cache(MLA_ROPE, PAST_LEN),
                                      pl.BlockSpec((None, 2, N_DROW - 1, GRID_W, LANES), lambda b, j: (l, j, 0, 0, 0)),
                                      table, table, table, table, table, table,
                                      layer(2, LANES), layer(1, LANES), layer(1, MLA_KV_RANK), w_pair, w_pair],
        out_specs=[out, out, out],
        out_shape=[jax.ShapeDtypeStruct((N_SAMPLE, WIDTH), jnp.float32)] * 3,
        compiler_params=_params("parallel", "arbitrary"),
        name="mix_sample",
    )(pa, pa, pa, pb, pb, pb, pc, pc, pc, pc, *caches_t, tt, *tables, cst, g_sub2, g_ckv, w_uk, w_uv)


FF_CHUNK = 1024


def _outffn_kernel(n_x, first, final, *refs):
    x_refs, op_ref, os_refs = refs[:n_x], refs[n_x], refs[n_x + 1:n_x + 4]
    (od_ref, wout_ref, g1_ref, gffn_ref, sh2_ref, sc2_ref, g2_ref, w1_ref, w2_ref, gfin_ref, y_ref) = refs[n_x + 4:]
    o_att = jnp.where(first + pl.program_id(0) < TILES_PROMPT, op_ref[...],
                      jnp.concatenate([r[...] for r in os_refs], axis=1))
    acc = (_dot(_bf(o_att), wout_ref[0, 0:O_ATT, :])
           + _dot(_bf(od_ref[...]), wout_ref[0, O_ATT:O_ATT + WIDTH, :]))
    x1 = _read_tile(x_refs, first) + g1_ref[...] * acc
    hf = _bf(_rms(x1, gffn_ref[0]) * (1.0 + sc2_ref[...]) + sh2_ref[...])
    acc = jnp.zeros((TM, D_MODEL), jnp.float32)
    for c in range(D_FF // FF_CHUNK):
        cols = slice(FF_CHUNK * c, FF_CHUNK * (c + 1))
        a = jnp.square(jnp.maximum(_dot(hf, w1_ref[0, :, cols]), 0.0))
        acc += _dot(_bf(a), w2_ref[0, cols, :])
    y = x1 + g2_ref[...] * acc
    if final:
        y = _rms(y, gfin_ref[...])
    y_ref[...] = y


def _outffn(l, xs, o_p, o_s, od, w_out, g_ffn, mod, w1, w2, g_final, first, n_tiles):
    def mod_spec(j):
        return pl.BlockSpec((None, None, 1, D_MODEL), lambda i: (l, _row_group(first + i), 0, j))

    def resident(shape):
        return pl.BlockSpec(shape, lambda i: (l,) + (0,) * (len(shape) - 1), pipeline_mode=pl.Buffered(1))

    return pl.pallas_call(
        functools.partial(_outffn_kernel, len(xs), first, l == DEPTH - 1),
        grid=(n_tiles,),
        in_specs=_x_specs(len(xs) == 2, first) + _split_specs(O_ATT, first)[:1] + _split_specs(WIDTH, first)[1:] * 3 + [
            pl.BlockSpec((TM, WIDTH), lambda i: (first + i, 0)),
            resident((1, 4 * WIDTH, D_MODEL)),
            mod_spec(2),
            pl.BlockSpec((1, 1, D_MODEL), lambda i: (l, 0, 0)),
            mod_spec(3), mod_spec(4), mod_spec(5),
            resident((1, D_MODEL, D_FF)),
            resident((1, D_FF, D_MODEL)),
            pl.BlockSpec((1, D_MODEL), lambda i: (0, 0)),
        ],
        out_specs=pl.BlockSpec((TM, D_MODEL), lambda i: (i, 0)),
        out_shape=jax.ShapeDtypeStruct((n_tiles * TM, D_MODEL), jnp.float32),
        compiler_params=_params("parallel"),
        name="outffn",
    )(*xs, o_p, *o_s, od, w_out, mod, g_ffn, mod, mod, mod, w1, w2, g_final)


def _rope32_tables():
    t = np.arange(DEC_SEQ)
    rows, cols = (t // GRID_W).astype(np.float64), (t % GRID_W).astype(np.float64)
    half = 8
    freqs = ROPE_BASE ** (-np.arange(half, dtype=np.float64) / half)
    cos, sin = [], []
    for pos in (rows, cols):
        ang = pos[:, None] * freqs[None, :]
        cos += [np.cos(ang), np.cos(ang)]
        sin += [-np.sin(ang), np.sin(ang)]
    return np.concatenate(cos, axis=1).astype(np.float32), np.concatenate(sin, axis=1).astype(np.float32)


def _rope_tables():
    c32, s32 = _rope32_tables()
    tile = lambda a, n: np.tile(a, (1, n))
    pad = np.zeros((DEC_SEQ, LANES - MLA_ROPE), np.float32)
    cos_k = np.concatenate([c32, pad + 1.0], axis=1)
    sin_k = np.concatenate([s32, pad], axis=1)
    return (tile(c32, 4), tile(s32, 4),
            cos_k, sin_k)


def kernel(x_prompt, x_sample, cache_na_k, cache_na_v, cache_diff_k, cache_diff_v, cache_mla_ckv, cache_mla_kpe, c, c_ctx, w_ada, b_ada, g_mix, g_ffn, w_in, w_out, na_rpb, diff_lq1, diff_lk1, diff_lq2, diff_lk2, diff_g_subln, mla_g_ckv, mla_w_uk, mla_w_uv, sgu_g, sgu_w, sgu_b, w_ff1, w_ff2, g_final):
    f32 = jnp.float32
    m = jnp.concatenate([c_ctx[None, :], c, jnp.zeros((N_MOD_ROWS - 1 - DEC_BATCH, D_MODEL), f32)], axis=0)
    mod = _ada(m, w_ada, b_ada).reshape(DEPTH, N_MOD_ROWS, 1, 6 * D_MODEL)
    cst = _lam_consts(diff_lq1, diff_lk1, diff_lq2, diff_lk2)
    tt = _bias_tiles(na_rpb)
    cos4, sin4, cos_k, sin_k = [jnp.asarray(t) for t in _rope_tables()]
    tables = (cos4, sin4, cos4, sin4, cos_k, sin_k)

    t_last = lambda a: jnp.swapaxes(a, -1, -2)
    w_in_t = t_last(w_in)
    caches_t = (t_last(cache_na_k), t_last(cache_na_v), t_last(cache_diff_k), t_last(cache_diff_v),
                cache_mla_ckv, t_last(cache_mla_kpe))
    w_out_b, w1_b, w2_b = _bf(w_out), _bf(w_ff1), _bf(w_ff2)
    g_mix3 = g_mix.reshape(DEPTH, 1, D_MODEL)
    g_ffn3 = g_ffn.reshape(DEPTH, 1, D_MODEL)
    g_sub2 = jnp.tile(diff_g_subln, (1, 2)).reshape(DEPTH, 1, LANES)
    g_ckv3 = mla_g_ckv.reshape(DEPTH, 1, MLA_KV_RANK)
    sgu_g3 = sgu_g.reshape(DEPTH, 1, WIDTH)
    sgu_bt = sgu_b.transpose(0, 2, 1)
    g_fin2 = g_final.reshape(1, D_MODEL)

    xs = (x_prompt.reshape(N_PROMPT, D_MODEL), x_sample.reshape(N_SAMPLE, D_MODEL))
    new = ()
    for l in range(DEPTH):
        pa, pb, pc, od = _inproj(l, xs, g_mix3, mod, w_in_t, sgu_g3, sgu_w, sgu_bt)
        o_p, *new = _mix_prompt(l, pa, pb, pc, cst, g_sub2, g_ckv3, mla_w_uk, mla_w_uv, new)
        o_s = _mix_sample(l, pa, pb, pc, caches_t, tt, tables, cst, g_sub2, g_ckv3, mla_w_uk, mla_w_uv)
        ffn = functools.partial(_outffn, l, xs, o_p, o_s, od, w_out_b, g_ffn3, mod, w1_b, w2_b, g_fin2)
        if l < DEPTH - 1:
            xs = (ffn(0, TILES_PROMPT + TILES_SAMPLE),)
        else:
            xs = (ffn(0, TILES_PROMPT), ffn(TILES_PROMPT, TILES_SAMPLE))
    y_prompt = xs[0].reshape(BATCH, SEQ, D_MODEL)
    y_sample = xs[1].reshape(DEC_BATCH, DEC_SEQ, D_MODEL)
    na_k, na_v, diff_k, diff_v, mla_ckv, mla_kpe = new
    return (y_prompt, y_sample, t_last(na_k), t_last(na_v), t_last(diff_k), t_last(diff_v), mla_ckv, t_last(mla_kpe))
```

```python
import functools
import math

import numpy as np
import jax
import jax.numpy as jnp
from jax import lax
from jax.experimental import pallas as pl
from jax.experimental.pallas import tpu as pltpu

D_MODEL = 1024
BATCH = 16
SEQ = 256
DEPTH = 4
DEC_BATCH = 2
DEC_SEQ = 1024
PAST_LEN = 512
GRID_W = 64
GRID_ROWS = DEC_SEQ // GRID_W
HEAD_DIM = 64
NA_HEADS = 4
NA_WIN_ROWS = 8
NA_WIN_COLS = 16
DIFF_HEADS = 4
DIFF_QK_DIM = 32
DIFF_V_DIM = 64
MLA_HEADS = 4
MLA_NOPE = 64
MLA_ROPE = 32
MLA_V = 64
MLA_KV_RANK = 128
SGU_GROUPS = 4
SGU_GROUP_DIM = 64
SGU_CHUNK = 128
D_FF = 4 * D_MODEL
ROPE_BASE = 10000.0
EPS = 1e-6
NEG_INF = -1e30
LOG2E = 1.4426950408889634

N_HEADS = 4
N_PAIRS = N_HEADS // 2
LANES = 128
WIDTH = 256
N_PROMPT = BATCH * SEQ
N_SAMPLE = DEC_BATCH * DEC_SEQ
N_TOK = N_PROMPT + N_SAMPLE
N_MOD_ROWS = 8

SEG_A = 3 * WIDTH
SEG_B = 3 * WIDTH
SEG_C = 640
SEG_D = 2 * WIDTH
SEG_C_PAD = 96
IN_COLS_P = SEG_A + SEG_B + SEG_C + SEG_D
O_ATT = 3 * WIDTH

TM = 512
TILES_PROMPT = N_PROMPT // TM
TILES_SAMPLE = N_SAMPLE // TM
PB = 2
QB = 256
QB_EXACT = 64
VMEM_LIMIT = 56 * 1024 * 1024


def _bf(x):
    return x.astype(jnp.bfloat16)


def _dot(a, b):
    return jnp.dot(a, b, preferred_element_type=jnp.float32)


def _dot_nt(a, b):
    return lax.dot_general(a, b, (((1,), (1,)), ((), ())), preferred_element_type=jnp.float32)


def _rms(x, g):
    ms = jnp.mean(x * x, axis=-1, keepdims=True)
    return x * lax.rsqrt(ms + EPS) * g


def _lane_range(lo, hi, width=LANES):
    lane = lax.broadcasted_iota(jnp.int32, (1, width), 1)
    return (lane >= lo) & (lane < hi)


def _with_ones(v):
    return jnp.concatenate([v, jnp.ones((v.shape[0], LANES), jnp.bfloat16)], axis=1)


def _attend(scores, values, log=None, bound=None):
    if bound is None:
        m = functools.reduce(jnp.maximum, [jnp.max(s, axis=-1, keepdims=True) for s in scores])
    else:
        m = bound
    o = functools.reduce(lambda a, b: a + b, [_dot(_bf(jnp.exp2(s - m)), v) for s, v in zip(scores, values)])
    if bound is not None:
        den = jnp.min(o[:, LANES:2 * LANES], axis=0, keepdims=True)
        log.least = jnp.minimum(log.least, jnp.min(den, axis=1, keepdims=True))
    return o


BOUND_SLACK = 1.02
MIN_DENOMINATOR = 2.0 ** -88


class _ShiftLog:
    def __init__(self):
        self.least = jnp.full((1, 1), jnp.inf, jnp.float32)

    def unsafe(self):
        return jnp.logical_not(self.least[0, 0] >= MIN_DENOMINATOR)


def _group_matrix(width, n_groups, extra=None):
    i = lax.broadcasted_iota(jnp.int32, (width, LANES), 0)
    j = lax.broadcasted_iota(jnp.int32, (width, LANES), 1)
    size = LANES // n_groups
    hit = (i // size == j // size) & (i < LANES)
    for t in range(n_groups if extra else 0):
        lo, hi = extra(t)
        hit = hit | ((i >= lo) & (i < hi) & (j // size == t))
    return jnp.where(hit, 1.0, 0.0).astype(jnp.bfloat16)


def _squares(x):
    xf = x.astype(jnp.float32)
    return _bf(xf * xf)


def _key_bound(keys, groups):
    return functools.reduce(jnp.maximum, [jnp.max(_dot(_squares(k), groups), axis=0, keepdims=True) for k in keys])


def _bound(log, q_parts, k2max, n_groups, extra=0.0):
    if log is None:
        return None
    size = LANES // n_groups
    q_norm = jnp.sqrt(functools.reduce(lambda a, b: a + b,
                                       [jnp.sum(q * q, axis=-1, keepdims=True) for q in q_parts])) * BOUND_SLACK
    return jnp.concatenate([q_norm * jnp.sqrt(k2max[:, size * t:size * t + 1]) + extra for t in range(n_groups)], axis=0)


def _loop(n, log, body):
    if log is not None:
        for i in range(n):
            body(i, log)
    else:
        lax.fori_loop(0, n, lambda i, carry: body(i, None) or carry, 0)


def _aligned(start, multiple):
    return start if isinstance(start, int) else pl.multiple_of(start, multiple)


def _normalised(o_ext):
    return o_ext[:, 0:LANES] * (1.0 / o_ext[:, LANES:2 * LANES])


def _swap8(x):
    lane = lax.broadcasted_iota(jnp.int32, (1, LANES), 1)
    return jnp.where((lane & 15) < 8, pltpu.roll(x, LANES - 8, 1), pltpu.roll(x, 8, 1))


def _rope(x, cos, sin):
    outs = []
    for c in range(x.shape[1] // LANES):
        sl = slice(LANES * c, LANES * (c + 1))
        xc = x[:, sl]
        outs.append(xc * cos[:, sl] + _swap8(xc) * sin[:, sl])
    return outs[0] if len(outs) == 1 else jnp.concatenate(outs, axis=1)


def _tile4(x):
    return x + pltpu.roll(x, 32, 1) + pltpu.roll(x, 64, 1) + pltpu.roll(x, 96, 1)


def _params(*sem):
    return pltpu.CompilerParams(dimension_semantics=sem, vmem_limit_bytes=VMEM_LIMIT)


ADA_TN = 1536


def _ada_kernel(m_ref, w_ref, b_ref, o_ref):
    m = m_ref[...]
    s = m * jax.nn.sigmoid(m)
    o_ref[0] = _dot(_bf(s), _bf(w_ref[0])) + b_ref[0]


def _ada(m, w_ada, b_ada):
    n = 6 * D_MODEL
    return pl.pallas_call(
        _ada_kernel,
        grid=(DEPTH, n // ADA_TN),
        in_specs=[
            pl.BlockSpec((N_MOD_ROWS, D_MODEL), lambda l, j: (0, 0)),
            pl.BlockSpec((1, D_MODEL, ADA_TN), lambda l, j: (l, 0, j)),
            pl.BlockSpec((1, 1, ADA_TN), lambda l, j: (l, 0, j)),
        ],
        out_specs=pl.BlockSpec((1, N_MOD_ROWS, ADA_TN), lambda l, j: (l, 0, j)),
        out_shape=jax.ShapeDtypeStruct((DEPTH, N_MOD_ROWS, n), jnp.float32),
        compiler_params=_params("parallel", "parallel"),
        name="ada",
    )(m, w_ada, b_ada.reshape(DEPTH, 1, n))


def _lam_kernel(lq1_ref, lk1_ref, lq2_ref, lk2_ref, init_ref, o_ref):
    init = init_ref[...]
    a = jnp.exp(jnp.sum(lq1_ref[...] * lk1_ref[...], axis=-1, keepdims=True))
    b = jnp.exp(jnp.sum(lq2_ref[...] * lk2_ref[...], axis=-1, keepdims=True))
    lam = a - b + init
    post = 1.0 - init
    for l in range(DEPTH):
        o_ref[l, 0:1, :] = jnp.broadcast_to(lam[l:l + 1], (1, LANES))
        o_ref[l, 1:2, :] = jnp.broadcast_to(post[l:l + 1], (1, LANES))


def _lam_consts(lq1, lk1, lq2, lk2):
    init = np.array([[0.8 - 0.6 * math.exp(-0.3 * l)] for l in range(DEPTH)], np.float32)
    return pl.pallas_call(
        _lam_kernel,
        out_shape=jax.ShapeDtypeStruct((DEPTH, 2, LANES), jnp.float32),
        name="diff_lambda",
    )(lq1, lk1, lq2, lk2, jnp.asarray(init))


N_DROW = 2 * NA_WIN_ROWS - 1
N_DCOL = 2 * NA_WIN_COLS - 1


def _bias_kernel(rpb_ref, o_ref):
    l = pl.program_id(0)
    h = pl.program_id(1)
    base = (l * NA_HEADS + h) * (N_DROW * N_DCOL)
    cq = lax.broadcasted_iota(jnp.int32, (GRID_W, LANES), 0)
    lane = lax.broadcasted_iota(jnp.int32, (GRID_W, LANES), 1)
    ck = lane & (GRID_W - 1)
    dcol = jnp.clip(ck - cq, -(NA_WIN_COLS - 1), NA_WIN_COLS - 1) + (NA_WIN_COLS - 1)
    hi = lane >= GRID_W
    for a in range(N_DROW - 1):
        acc = jnp.zeros((GRID_W, LANES), jnp.float32)
        for j in range(N_DCOL):
            lo_v = rpb_ref[base + a * N_DCOL + j]
            hi_v = rpb_ref[base + (a + 1) * N_DCOL + j]
            acc = jnp.where(dcol == j, jnp.where(hi, hi_v, lo_v), acc)
        o_ref[0, 0, a] = acc * LOG2E


def _bias_tiles(na_rpb):
    return pl.pallas_call(
        _bias_kernel,
        grid=(DEPTH, NA_HEADS),
        in_specs=[pl.BlockSpec(memory_space=pltpu.SMEM)],
        out_specs=pl.BlockSpec((1, 1, N_DROW - 1, GRID_W, LANES), lambda l, h: (l, h, 0, 0, 0)),
        out_shape=jax.ShapeDtypeStruct((DEPTH, NA_HEADS, N_DROW - 1, GRID_W, LANES), jnp.float32),
        compiler_params=_params("parallel", "parallel"),
        name="na_bias_tiles",
    )(na_rpb.reshape(-1))


def _row_group(i):
    return jnp.where(i < TILES_PROMPT, 0, 1 + (i - TILES_PROMPT) // (DEC_SEQ // TM))


def _split_specs(width, first):
    return [pl.BlockSpec((TM, width), lambda i: (jnp.minimum(first + i, TILES_PROMPT - 1), 0)),
            pl.BlockSpec((TM, width), lambda i: (jnp.maximum(first + i - TILES_PROMPT, 0), 0))]


def _x_specs(split, first):
    if not split:
        return [pl.BlockSpec((TM, D_MODEL), lambda i: (first + i, 0))]
    return _split_specs(D_MODEL, first)


def _read_tile(refs, first):
    if len(refs) == 1:
        return refs[0][...]
    return jnp.where(first + pl.program_id(0) < TILES_PROMPT, refs[0][...], refs[1][...])


IN_COLS = 2592
IN_QC, IN_CKV, IN_D = 1536, 1920, 2080
TR_ROWS = 256


def _gelu_tanh(x):
    return 0.5 * x * (1.0 + jnp.tanh(math.sqrt(2.0 / math.pi) * (x + 0.044715 * (x * x * x))))


def _sgu(pd, g, w_ref, bt):
    u = _gelu_tanh(pd[:, 0:WIDTH])
    v = _gelu_tanh(pd[:, WIDTH:2 * WIDTH])
    grp = lax.broadcasted_iota(jnp.int32, (1, WIDTH), 1) // SGU_GROUP_DIM
    v2 = v * v
    ms = jnp.zeros_like(v)
    for gi in range(SGU_GROUPS):
        sel = grp == gi
        tot = jnp.sum(jnp.where(sel, v2, 0.0), axis=-1, keepdims=True)
        ms = jnp.where(sel, tot * (1.0 / SGU_GROUP_DIM), ms)
    vg = _bf(v * lax.rsqrt(ms + EPS) * g)
    outs = []
    for c in range(pd.shape[0] // SGU_CHUNK):
        rows = slice(SGU_CHUNK * c, SGU_CHUNK * (c + 1))
        mixed = jnp.zeros((SGU_CHUNK, WIDTH), jnp.float32)
        for gi in range(SGU_GROUPS):
            full = _dot(_bf(w_ref[0, gi]), vg[rows]) + bt[:, gi:gi + 1]
            mixed = jnp.where(grp == gi, full, mixed)
        outs.append(u[rows] * mixed)
    return jnp.concatenate(outs, axis=0)


def _w_in_row_pieces():
    qn = [(IN_QC + 96 * h, MLA_NOPE) for h in range(MLA_HEADS)]
    qp = [(IN_QC + 96 * h + MLA_NOPE, MLA_ROPE) for h in range(MLA_HEADS)]
    seg_c = qn + qp + [(IN_CKV, MLA_KV_RANK + MLA_ROPE)]
    return (0, SEG_A + SEG_B), seg_c, (IN_D, SEG_D)


def _load_w_in(wt_ref, w_scr):
    ab, seg_c, d = _w_in_row_pieces()
    c_rows = jnp.concatenate([wt_ref[0, s:s + n, :] for s, n in seg_c]
                             + [jnp.zeros((SEG_C_PAD, D_MODEL), jnp.float32)], axis=0)
    for t in range(SEG_C // LANES):
        w_scr[:, SEG_A + SEG_B + LANES * t:SEG_A + SEG_B + LANES * (t + 1)] = _bf(c_rows[LANES * t:LANES * (t + 1)].T)
    for (src, n), dst in ((ab, 0), (d, SEG_A + SEG_B + SEG_C)):
        for t in range(n // TR_ROWS):
            rows = wt_ref[0, src + TR_ROWS * t:src + TR_ROWS * (t + 1), :]
            w_scr[:, dst + TR_ROWS * t:dst + TR_ROWS * (t + 1)] = _bf(rows.T)


def _inproj_kernel(n_x, *refs):
    x_refs = refs[:n_x]
    (g_ref, sh_ref, sc_ref, wt_ref, sg_ref, sw_ref, sbt_ref, pa_ref, pb_ref, pc_ref, od_ref, w_scr) = refs[n_x:]

    @pl.when(pl.program_id(0) == 0)
    def _():
        _load_w_in(wt_ref, w_scr)

    h = _rms(_read_tile(x_refs, 0), g_ref[0]) * (1.0 + sc_ref[...]) + sh_ref[...]
    hb = _bf(h)
    off = SEG_A + SEG_B + SEG_C
    od_ref[...] = _sgu(_dot(hb, w_scr[:, off:off + SEG_D]), sg_ref[0], sw_ref, sbt_ref[0])
    off = 0
    for ref in (pa_ref, pb_ref, pc_ref):
        n = ref.shape[1]
        ref[...] = _dot(hb, w_scr[:, off:off + n])
        off += n


def _inproj(l, xs, g_mix, mod, w_in_t, sgu_g, sgu_w, sgu_bt):
    def mod_spec(j):
        return pl.BlockSpec((None, None, 1, D_MODEL), lambda i: (l, _row_group(i), 0, j))

    widths = (SEG_A, SEG_B, SEG_C, WIDTH)
    return pl.pallas_call(
        functools.partial(_inproj_kernel, len(xs)),
        grid=(N_TOK // TM,),
        in_specs=_x_specs(len(xs) == 2, 0) + [
            pl.BlockSpec((1, 1, D_MODEL), lambda i: (l, 0, 0)),
            mod_spec(0), mod_spec(1),
            pl.BlockSpec((1, IN_COLS, D_MODEL), lambda i: (l, 0, 0), pipeline_mode=pl.Buffered(1)),
            pl.BlockSpec((1, 1, WIDTH), lambda i: (l, 0, 0)),
            pl.BlockSpec((1, SGU_GROUPS, SGU_CHUNK, SGU_CHUNK), lambda i: (l, 0, 0, 0)),
            pl.BlockSpec((1, SGU_CHUNK, SGU_GROUPS), lambda i: (l, 0, 0)),
        ],
        out_specs=[pl.BlockSpec((TM, n), lambda i: (i, 0)) for n in widths],
        out_shape=[jax.ShapeDtypeStruct((N_TOK, n), jnp.float32) for n in widths],
        scratch_shapes=[pltpu.VMEM((D_MODEL, IN_COLS_P), jnp.bfloat16)],
        compiler_params=_params("arbitrary"),
        name="inproj",
    )(*xs, g_mix, mod, mod, w_in_t, sgu_g, sgu_w, sgu_bt)


C_QN, C_QP, C_CKV, C_KPE = 0, 256, 384, 512


def _stack_heads(qp):
    lo = _lane_range(0, 64)
    return jnp.concatenate([_bf(jnp.where(lo, qp, 0.0)), _bf(jnp.where(lo, 0.0, qp))], axis=0)


def _unstack_heads(o, n):
    return jnp.where(_lane_range(0, 64), o[0:n], o[n:2 * n])


def _pair_t(c_ref):
    return jnp.concatenate([c_ref[0], c_ref[1]], axis=0)


def _stack_components(qp):
    return jnp.concatenate([_bf(jnp.where(_lane_range(32 * t, 32 * (t + 1)), qp, 0.0)) for t in range(4)], axis=0)


def _group_mean_sq(x, groups, size):
    sq = x * x
    hi = _bf(sq)
    rest = sq - hi.astype(jnp.float32)
    mid = _bf(rest)
    lo = _bf(rest - mid.astype(jnp.float32))
    return (_dot(hi, groups) + _dot(mid, groups) + _dot(lo, groups)) * (1.0 / size)


def _diff_finish(o, n, lam, post, g2, by_head):
    den = o[:, LANES:2 * LANES]
    outs = []
    for t in range(2):
        p1 = o[2 * t * n:(2 * t + 1) * n, 0:LANES] * (1.0 / den[2 * t * n:(2 * t + 1) * n])
        p2 = o[(2 * t + 1) * n:(2 * t + 2) * n, 0:LANES] * (lam / den[(2 * t + 1) * n:(2 * t + 2) * n])
        outs.append(p1 - p2)
    d = jnp.where(_lane_range(0, 64), outs[0], outs[1])
    return d * lax.rsqrt(_group_mean_sq(d, by_head, DIFF_V_DIM) + EPS) * g2 * post


def _mla_groups(j):
    return _group_matrix(2 * LANES, 2,
                         lambda t: (LANES + MLA_ROPE * (2 * j + t), LANES + MLA_ROPE * (2 * j + t + 1)))


def _mla_queries(qn_pair, qp_all, j):
    halves = []
    for t in range(2):
        h = 2 * j + t
        halves.append(jnp.concatenate([
            _bf(jnp.where(_lane_range(64 * t, 64 * (t + 1)), qn_pair, 0.0)),
            _bf(jnp.where(_lane_range(MLA_ROPE * h, MLA_ROPE * (h + 1)), qp_all, 0.0))], axis=1))
    return jnp.concatenate(halves, axis=0)


def _write_heads_t(p_ref, rows, col0, out_ref, bb):
    xt = p_ref[rows, col0:col0 + WIDTH].T
    for h in range(N_HEADS):
        out_ref[bb, 0, h] = xt[64 * h:64 * (h + 1)]
    _clear_other_layers(out_ref, bb)


def _clear_other_layers(out_ref, bb):
    if out_ref.shape[1] > 1:
        out_ref[bb, 1:] = jnp.zeros(out_ref.shape[1:], jnp.float32)[1:]


def _mix_prompt_kernel(n_prev, *refs):
    ins, outs = refs[:8], refs[8 + n_prev:]
    log = _ShiftLog()
    _mix_prompt_pass(ins, outs, log)

    @pl.when(log.unsafe())
    def _():
        _mix_prompt_pass(ins, outs, None)


def _mix_prompt_pass(ins, outs, log):
    pa_ref, pb_ref, pc_ref, cst_ref, gsub_ref, gckv_ref, wuk_ref, wuv_ref = ins
    o_ref, nak_ref, nav_ref, dk_ref, dv_ref, ckv_ref, kpe_ref = outs
    first_pass = log is not None
    c_a = HEAD_DIM ** -0.5 * LOG2E
    c_b = DIFF_QK_DIM ** -0.5 * LOG2E
    c_c = (MLA_NOPE + MLA_ROPE) ** -0.5 * LOG2E
    lam = cst_ref[0, 0:1, 0:1]
    post = cst_ref[0, 1:2, 0:1]
    wuk, wuv = _bf(wuk_ref[0]), _bf(wuv_ref[0])
    by_head, by_comp = _group_matrix(LANES, 2), _group_matrix(LANES, 4)

    def sequence(bb, log):
        rows = pl.ds(_aligned(bb * SEQ, SEQ), SEQ)
        for j in range(N_PAIRS):
            cols = slice(LANES * j, LANES * (j + 1))
            k = _bf(pa_ref[rows, WIDTH + LANES * j:WIDTH + LANES * (j + 1)])
            v = _with_ones(_bf(pa_ref[rows, 2 * WIDTH + LANES * j:2 * WIDTH + LANES * (j + 1)]))
            q = pa_ref[rows, cols] * c_a
            o = _attend([_dot_nt(_stack_heads(q), k)], [v], log, _bound(log, [q], _key_bound([k], by_head), 2))
            o_ref[rows, cols] = _unstack_heads(_normalised(o), SEQ)
        if first_pass:
            _write_heads_t(pa_ref, rows, WIDTH, nak_ref, bb)
            _write_heads_t(pa_ref, rows, 2 * WIDTH, nav_ref, bb)
        for j in range(N_PAIRS):
            cols = slice(LANES * j, LANES * (j + 1))
            k = _bf(pb_ref[rows, WIDTH + LANES * j:WIDTH + LANES * (j + 1)])
            v = _with_ones(_bf(pb_ref[rows, 2 * WIDTH + LANES * j:2 * WIDTH + LANES * (j + 1)]))
            q = pb_ref[rows, cols] * c_b
            o = _attend([_dot_nt(_stack_components(q), k)], [v], log, _bound(log, [q], _key_bound([k], by_comp), 4))
            o_ref[rows, WIDTH + LANES * j:WIDTH + LANES * (j + 1)] = _diff_finish(o, SEQ, lam, post, gsub_ref[0], by_head)
        if first_pass:
            _write_heads_t(pb_ref, rows, WIDTH, dk_ref, bb)
            _write_heads_t(pb_ref, rows, 2 * WIDTH, dv_ref, bb)
        ckv = _rms(pc_ref[rows, C_CKV:C_CKV + MLA_KV_RANK], gckv_ref[0])
        kpe_slot = pc_ref[rows, C_KPE:C_KPE + LANES]
        if first_pass:
            ckv_ref[bb, 0] = ckv
            _clear_other_layers(ckv_ref, bb)
            kpe_ref[bb, 0] = kpe_slot.T[0:MLA_ROPE]
            _clear_other_layers(kpe_ref, bb)
        ckv_b = _bf(ckv)
        kn = _bf(_dot(ckv_b, wuk))
        vv = _bf(_dot(ckv_b, wuv))
        kpe4 = _bf(_tile4(kpe_slot))
        qn = pc_ref[rows, C_QN:C_QN + WIDTH] * c_c
        qp = pc_ref[rows, C_QP:C_QP + LANES] * c_c
        for j in range(N_PAIRS):
            cols = slice(LANES * j, LANES * (j + 1))
            k = jnp.concatenate([kn[:, cols], kpe4], axis=1)
            qs = _mla_queries(qn[:, cols], qp, j)
            groups = _mla_groups(j)
            o = _attend([_dot_nt(qs, k)], [_with_ones(vv[:, cols])], log,
                        _bound(log, [qn[:, cols], qp], _key_bound([k], groups), 2))
            o_ref[rows, 2 * WIDTH + LANES * j:2 * WIDTH + LANES * (j + 1)] = _unstack_heads(_normalised(o), SEQ)

    _loop(PB, log, sequence)


def _mix_prompt(l, pa, pb, pc, cst, g_sub2, g_ckv, w_uk, w_uv, prev):
    n_prev = len(prev)
    tails = [(NA_HEADS, HEAD_DIM, SEQ)] * 2 + [(DIFF_HEADS, 64, SEQ)] * 2 + [(SEQ, MLA_KV_RANK), (MLA_ROPE, SEQ)]

    def cache_spec(tail):
        if l == 0:
            return pl.BlockSpec((PB, DEPTH) + tail, lambda b: (b, 0) + (0,) * len(tail))
        return pl.BlockSpec((PB, 1) + tail, lambda b: (b, l) + (0,) * len(tail))

    def rows(width):
        return pl.BlockSpec((PB * SEQ, width), lambda b: (b, 0))

    def layer(*tail):
        return pl.BlockSpec((1,) + tail, lambda b: (l,) + (0,) * len(tail))

    return pl.pallas_call(
        functools.partial(_mix_prompt_kernel, n_prev),
        grid=(BATCH // PB,),
        in_specs=[rows(SEG_A), rows(SEG_B), rows(SEG_C), layer(2, LANES), layer(1, LANES), layer(1, MLA_KV_RANK),
                  layer(MLA_KV_RANK, WIDTH), layer(MLA_KV_RANK, WIDTH)] + [pl.BlockSpec(memory_space=pl.ANY)] * n_prev,
        out_specs=[rows(O_ATT)] + [cache_spec(t) for t in tails],
        out_shape=[jax.ShapeDtypeStruct((N_PROMPT, O_ATT), jnp.float32)]
        + [jax.ShapeDtypeStruct((BATCH, DEPTH) + t, jnp.float32) for t in tails],
        input_output_aliases={8 + i: 1 + i for i in range(n_prev)},
        compiler_params=_params("parallel"),
        name="mix_prompt",
    )(pa, pb, pc, cst, g_sub2, g_ckv, w_uk, w_uv, *prev)


def _na_row_groups():
    kh = min(NA_WIN_ROWS, GRID_ROWS)
    r0s = [min(max(r - kh // 2, 0), GRID_ROWS - kh) for r in range(GRID_ROWS)]
    groups = []
    for r, r0 in enumerate(r0s):
        if groups and groups[-1][2] == r0:
            groups[-1][1] = r
        else:
            groups.append([r, r, r0])
    return kh, [tuple(g) for g in groups]


def _na_sample(q_ref, k_ref, v_ref, ck_ref, cv_ref, tt_ref, o_ref, log):
    c = HEAD_DIM ** -0.5 * LOG2E
    kh, groups = _na_row_groups()
    lk = kh * GRID_W
    edge = [g for g in groups if g[1] > g[0]]
    inner = [g for g in groups if g[1] == g[0]]
    depth = inner[0][0] - inner[0][2]
    assert all(g[0] - g[2] == depth for g in inner) and [g[0] for g in inner] == list(range(inner[0][0], inner[-1][0] + 1))

    def in_window(n):
        cq = lax.broadcasted_iota(jnp.int32, (n, lk), 0) & (GRID_W - 1)
        ck = lax.broadcasted_iota(jnp.int32, (n, lk), 1) & (GRID_W - 1)
        c0 = jnp.clip(cq - NA_WIN_COLS // 2, 0, GRID_W - NA_WIN_COLS)
        return (ck >= c0) & (ck < c0 + NA_WIN_COLS)

    kc_t = _bf(_pair_t(ck_ref))
    vc = _with_ones(_bf(_pair_t(cv_ref).T))
    by_head = _group_matrix(LANES, 2)
    k2max = _key_bound([_bf(k_ref[...]), _bf(_pair_t(ck_ref).T)], by_head)
    tmax = functools.reduce(jnp.maximum, [tt_ref[t, a] for t in range(2) for a in range(N_DROW - 1)])
    bplus = jnp.maximum(jnp.max(jnp.max(tmax, axis=-1, keepdims=True), axis=0, keepdims=True), 0.0)

    def group(row0, key0, offsets, log):
        n = len(offsets) * GRID_W
        rows, keys = pl.ds(row0, n), pl.ds(key0, lk)
        q = q_ref[rows, :] * c
        qg = _stack_heads(q)
        k = _bf(k_ref[keys, :])
        v = _with_ones(_bf(v_ref[keys, :]))
        bias = jnp.concatenate([
            jnp.concatenate([tt_ref[t, 2 * i - off + NA_WIN_ROWS - 1] for i in range(kh // 2)], axis=1)
            for t in range(2) for off in offsets], axis=0)
        s_loc = jnp.where(in_window(2 * n), _dot_nt(qg, k) + bias, NEG_INF)
        o = _attend([_dot(qg, kc_t), s_loc], [vc, v], log, _bound(log, [q], k2max, 2, bplus))
        o_ref[rows, :] = _unstack_heads(_normalised(o), n)

    if log is None:
        def any_row(r, log):
            r0 = jnp.clip(r - kh // 2, 0, GRID_ROWS - kh)
            group(_aligned(r * GRID_W, GRID_W), _aligned(r0 * GRID_W, GRID_W), [r - r0], log)

        _loop(GRID_ROWS, log, any_row)
        return

    for (r_lo, r_hi, r0) in edge:
        group(r_lo * GRID_W, r0 * GRID_W, [r - r0 for r in range(r_lo, r_hi + 1)], log)

    def inner_row(i, log):
        r = inner[0][0] + i
        group(r * GRID_W, (r - depth) * GRID_W, [depth], log)

    _loop(len(inner), log, inner_row)


def _diff_sample(q_ref, k_ref, v_ref, ck_ref, cv_ref, cos_ref, sin_ref, cst_ref, g_ref, o_ref, log):
    c = DIFF_QK_DIM ** -0.5 * LOG2E
    lam = cst_ref[0, 0:1, 0:1]
    post = cst_ref[0, 1:2, 0:1]
    k_new = _bf(_rope(k_ref[...], cos_ref[...], sin_ref[...]))
    kc_t = _bf(_pair_t(ck_ref))
    vc = _with_ones(_bf(_pair_t(cv_ref).T))
    v = _with_ones(_bf(v_ref[...]))
    by_head, by_comp = _group_matrix(LANES, 2), _group_matrix(LANES, 4)
    k2max = _key_bound([k_new, _bf(_pair_t(ck_ref).T)], by_comp)

    qb = QB_EXACT if log is None else QB

    def block(qi, log):
        rows = pl.ds(_aligned(qi * qb, qb), qb)
        q = _rope(q_ref[rows, :], cos_ref[rows, :], sin_ref[rows, :]) * c
        qs = _stack_components(q)
        o = _attend([_dot(qs, kc_t), _dot_nt(qs, k_new)], [vc, v], log, _bound(log, [q], k2max, 4))
        o_ref[rows, :] = _diff_finish(o, qb, lam, post, g_ref[0], by_head)

    _loop(DEC_SEQ // qb, log, block)


def _mla_sample(j, qn_ref, qp_ref, ckv_ref, kpe_ref, cckv_ref, ckpe_ref, cosq_ref, sinq_ref, cosk_ref, sink_ref,
                gckv_ref, wuk_ref, wuv_ref, o_ref, log):
    c = (MLA_NOPE + MLA_ROPE) ** -0.5 * LOG2E
    wuk, wuv = _bf(wuk_ref[0]), _bf(wuv_ref[0])
    ckv_new = _bf(_rms(ckv_ref[...], gckv_ref[0]))
    ckv_old = _bf(cckv_ref[...])
    kpe_new = _bf(_tile4(_rope(kpe_ref[...], cosk_ref[...], sink_ref[...])))
    kpe_old = _bf(jnp.concatenate([ckpe_ref[...]] * MLA_HEADS, axis=0).T)
    k_old = jnp.concatenate([_bf(_dot(ckv_old, wuk)), kpe_old], axis=1)
    k_new = jnp.concatenate([_bf(_dot(ckv_new, wuk)), kpe_new], axis=1)
    vo, vn = _with_ones(_bf(_dot(ckv_old, wuv))), _with_ones(_bf(_dot(ckv_new, wuv)))
    groups = _mla_groups(j)
    k2max = _key_bound([k_old, k_new], groups)

    qb = QB_EXACT if log is None else QB

    def block(qi, log):
        rows = pl.ds(_aligned(qi * qb, qb), qb)
        qp = _rope(qp_ref[rows, :], cosq_ref[rows, :], sinq_ref[rows, :]) * c
        qn = qn_ref[rows, :] * c
        qs = _mla_queries(qn, qp, j)
        o = _attend([_dot_nt(qs, k_old), _dot_nt(qs, k_new)], [vo, vn], log, _bound(log, [qn, qp], k2max, 2))
        o_ref[rows, :] = _unstack_heads(_normalised(o), qb)

    _loop(DEC_SEQ // qb, log, block)


def _mix_sample_kernel(qa_ref, ka_ref, va_ref, qb_ref, kb_ref, vb_ref, qn_ref, qp_ref, ckv_ref, kpe_ref,
                       cnak_ref, cnav_ref, cdk_ref, cdv_ref, cckv_ref, ckpe_ref, tt_ref,
                       cosb_ref, sinb_ref, cosq_ref, sinq_ref, cosk_ref, sink_ref,
                       cst_ref, gsub_ref, gckv_ref, wuk_ref, wuv_ref, oa_ref, ob_ref, oc_ref):
    j = pl.program_id(1)

    def run(log):
        _na_sample(qa_ref, ka_ref, va_ref, cnak_ref, cnav_ref, tt_ref, oa_ref, log)
        _diff_sample(qb_ref, kb_ref, vb_ref, cdk_ref, cdv_ref, cosb_ref, sinb_ref, cst_ref, gsub_ref, ob_ref, log)
        _mla_sample(j, qn_ref, qp_ref, ckv_ref, kpe_ref, cckv_ref, ckpe_ref, cosq_ref, sinq_ref, cosk_ref, sink_ref,
                    gckv_ref, wuk_ref, wuv_ref, oc_ref, log)

    log = _ShiftLog()
    run(log)

    @pl.when(log.unsafe())
    def _():
        run(None)


def _mix_sample(l, pa, pb, pc, caches_t, tt, tables, cst, g_sub2, g_ckv, w_uk, w_uv):
    first = N_PROMPT // DEC_SEQ

    def cols(block):
        return pl.BlockSpec((DEC_SEQ, LANES), lambda b, j: (first + b, block(j)))

    def cache(*tail, pair=False):
        return pl.BlockSpec((None, None) + tail, lambda b, j: (b, l, j if pair else 0) + (0,) * (len(tail) - 1))

    def layer(*tail):
        return pl.BlockSpec((1,) + tail, lambda b, j: (l,) + (0,) * len(tail))

    table = pl.BlockSpec((DEC_SEQ, LANES), lambda b, j: (0, 0), pipeline_mode=pl.Buffered(1))
    qkv = [cols(lambda j: j), cols(lambda j: N_PAIRS + j), cols(lambda j: 2 * N_PAIRS + j)]
    seg_c = [cols(lambda j: j), cols(lambda j: C_QP // LANES), cols(lambda j: C_CKV // LANES), cols(lambda j: C_KPE // LANES)]
    kv_t = cache(2, 64, PAST_LEN, pair=True)
    w_pair = pl.BlockSpec((1, MLA_KV_RANK, LANES), lambda b, j: (l, 0, j))
    out = pl.BlockSpec((DEC_SEQ, LANES), lambda b, j: (b, j))
    return pl.pallas_call(
        _mix_sample_kernel,
        grid=(DEC_BATCH, N_PAIRS),
        in_specs=qkv + qkv + seg_c + [kv_t, kv_t, kv_t, kv_t, cache(PAST_LEN, MLA_KV_RANK), cache(MLA_ROPE, PAST_LEN),
                                      pl.BlockSpec((None, 2, N_DROW - 1, GRID_W, LANES), lambda b, j: (l, j, 0, 0, 0)),
                                      table, table, table, table, table, table,
                                      layer(2, LANES), layer(1, LANES), layer(1, MLA_KV_RANK), w_pair, w_pair],
        out_specs=[out, out, out],
        out_shape=[jax.ShapeDtypeStruct((N_SAMPLE, WIDTH), jnp.float32)] * 3,
        compiler_params=_params("parallel", "arbitrary"),
        name="mix_sample",
    )(pa, pa, pa, pb, pb, pb, pc, pc, pc, pc, *caches_t, tt, *tables, cst, g_sub2, g_ckv, w_uk, w_uv)


FF_CHUNK = 1024


def _outffn_kernel(n_x, first, final, *refs):
    x_refs, op_ref, os_refs = refs[:n_x], refs[n_x], refs[n_x + 1:n_x + 4]
    (od_ref, wout_ref, g1_ref, gffn_ref, sh2_ref, sc2_ref, g2_ref, w1_ref, w2_ref, gfin_ref, y_ref) = refs[n_x + 4:]
    o_att = jnp.where(first + pl.program_id(0) < TILES_PROMPT, op_ref[...],
                      jnp.concatenate([r[...] for r in os_refs], axis=1))
    acc = (_dot(_bf(o_att), wout_ref[0, 0:O_ATT, :])
           + _dot(_bf(od_ref[...]), wout_ref[0, O_ATT:O_ATT + WIDTH, :]))
    x1 = _read_tile(x_refs, first) + g1_ref[...] * acc
    hf = _bf(_rms(x1, gffn_ref[0]) * (1.0 + sc2_ref[...]) + sh2_ref[...])
    acc = jnp.zeros((TM, D_MODEL), jnp.float32)
    for c in range(D_FF // FF_CHUNK):
        cols = slice(FF_CHUNK * c, FF_CHUNK * (c + 1))
        a = jnp.square(jnp.maximum(_dot(hf, w1_ref[0, :, cols]), 0.0))
        acc += _dot(_bf(a), w2_ref[0, cols, :])
    y = x1 + g2_ref[...] * acc
    if final:
        y = _rms(y, gfin_ref[...])
    y_ref[...] = y


def _outffn(l, xs, o_p, o_s, od, w_out, g_ffn, mod, w1, w2, g_final, first, n_tiles):
    def mod_spec(j):
        return pl.BlockSpec((None, None, 1, D_MODEL), lambda i: (l, _row_group(first + i), 0, j))

    def resident(shape):
        return pl.BlockSpec(shape, lambda i: (l,) + (0,) * (len(shape) - 1), pipeline_mode=pl.Buffered(1))

    return pl.pallas_call(
        functools.partial(_outffn_kernel, len(xs), first, l == DEPTH - 1),
        grid=(n_tiles,),
        in_specs=_x_specs(len(xs) == 2, first) + _split_specs(O_ATT, first)[:1] + _split_specs(WIDTH, first)[1:] * 3 + [
            pl.BlockSpec((TM, WIDTH), lambda i: (first + i, 0)),
            resident((1, 4 * WIDTH, D_MODEL)),
            mod_spec(2),
            pl.BlockSpec((1, 1, D_MODEL), lambda i: (l, 0, 0)),
            mod_spec(3), mod_spec(4), mod_spec(5),
            resident((1, D_MODEL, D_FF)),
            resident((1, D_FF, D_MODEL)),
            pl.BlockSpec((1, D_MODEL), lambda i: (0, 0)),
        ],
        out_specs=pl.BlockSpec((TM, D_MODEL), lambda i: (i, 0)),
        out_shape=jax.ShapeDtypeStruct((n_tiles * TM, D_MODEL), jnp.float32),
        compiler_params=_params("parallel"),
        name="outffn",
    )(*xs, o_p, *o_s, od, w_out, mod, g_ffn, mod, mod, mod, w1, w2, g_final)


def _rope32_tables():
    t = np.arange(DEC_SEQ)
    rows, cols = (t // GRID_W).astype(np.float64), (t % GRID_W).astype(np.float64)
    half = 8
    freqs = ROPE_BASE ** (-np.arange(half, dtype=np.float64) / half)
    cos, sin = [], []
    for pos in (rows, cols):
        ang = pos[:, None] * freqs[None, :]
        cos += [np.cos(ang), np.cos(ang)]
        sin += [-np.sin(ang), np.sin(ang)]
    return np.concatenate(cos, axis=1).astype(np.float32), np.concatenate(sin, axis=1).astype(np.float32)


def _rope_tables():
    c32, s32 = _rope32_tables()
    tile = lambda a, n: np.tile(a, (1, n))
    pad = np.zeros((DEC_SEQ, LANES - MLA_ROPE), np.float32)
    cos_k = np.concatenate([c32, pad + 1.0], axis=1)
    sin_k = np.concatenate([s32, pad], axis=1)
    return (tile(c32, 4), tile(s32, 4),
            cos_k, sin_k)


def kernel(x_prompt, x_sample, cache_na_k, cache_na_v, cache_diff_k, cache_diff_v, cache_mla_ckv, cache_mla_kpe, c, c_ctx, w_ada, b_ada, g_mix, g_ffn, w_in, w_out, na_rpb, diff_lq1, diff_lk1, diff_lq2, diff_lk2, diff_g_subln, mla_g_ckv, mla_w_uk, mla_w_uv, sgu_g, sgu_w, sgu_b, w_ff1, w_ff2, g_final):
    f32 = jnp.float32
    m = jnp.concatenate([c_ctx[None, :], c, jnp.zeros((N_MOD_ROWS - 1 - DEC_BATCH, D_MODEL), f32)], axis=0)
    mod = _ada(m, w_ada, b_ada).reshape(DEPTH, N_MOD_ROWS, 1, 6 * D_MODEL)
    cst = _lam_consts(diff_lq1, diff_lk1, diff_lq2, diff_lk2)
    tt = _bias_tiles(na_rpb)
    cos4, sin4, cos_k, sin_k = [jnp.asarray(t) for t in _rope_tables()]
    tables = (cos4, sin4, cos4, sin4, cos_k, sin_k)

    t_last = lambda a: jnp.swapaxes(a, -1, -2)
    w_in_t = t_last(w_in)
    caches_t = (t_last(cache_na_k), t_last(cache_na_v), t_last(cache_diff_k), t_last(cache_diff_v),
                cache_mla_ckv, t_last(cache_mla_kpe))
    w_out_b, w1_b, w2_b = _bf(w_out), _bf(w_ff1), _bf(w_ff2)
    g_mix3 = g_mix.reshape(DEPTH, 1, D_MODEL)
    g_ffn3 = g_ffn.reshape(DEPTH, 1, D_MODEL)
    g_sub2 = jnp.tile(diff_g_subln, (1, 2)).reshape(DEPTH, 1, LANES)
    g_ckv3 = mla_g_ckv.reshape(DEPTH, 1, MLA_KV_RANK)
    sgu_g3 = sgu_g.reshape(DEPTH, 1, WIDTH)
    sgu_bt = sgu_b.transpose(0, 2, 1)
    g_fin2 = g_final.reshape(1, D_MODEL)

    xs = (x_prompt.reshape(N_PROMPT, D_MODEL), x_sample.reshape(N_SAMPLE, D_MODEL))
    new = ()
    for l in range(DEPTH):
        pa, pb, pc, od = _inproj(l, xs, g_mix3, mod, w_in_t, sgu_g3, sgu_w, sgu_bt)
        o_p, *new = _mix_prompt(l, pa, pb, pc, cst, g_sub2, g_ckv3, mla_w_uk, mla_w_uv, new)
        o_s = _mix_sample(l, pa, pb, pc, caches_t, tt, tables, cst, g_sub2, g_ckv3, mla_w_uk, mla_w_uv)
        ffn = functools.partial(_outffn, l, xs, o_p, o_s, od, w_out_b, g_ffn3, mod, w1_b, w2_b, g_fin2)
        if l < DEPTH - 1:
            xs = (ffn(0, TILES_PROMPT + TILES_SAMPLE),)
        else:
            xs = (ffn(0, TILES_PROMPT), ffn(TILES_PROMPT, TILES_SAMPLE))
    y_prompt = xs[0].reshape(BATCH, SEQ, D_MODEL)
    y_sample = xs[1].reshape(DEC_BATCH, DEC_SEQ, D_MODEL)
    na_k, na_v, diff_k, diff_v, mla_ckv, mla_kpe = new
    return (y_prompt, y_sample, t_last(na_k), t_last(na_v), t_last(diff_k), t_last(diff_v), mla_ckv, t_last(mla_kpe))
```

```python
import functools
import math

import numpy as np
import jax
import jax.numpy as jnp
from jax import lax
from jax.experimental import pallas as pl
from jax.experimental.pallas import tpu as pltpu

D_MODEL = 1024
BATCH = 16
SEQ = 256
DEPTH = 4
DEC_BATCH = 2
DEC_SEQ = 1024
PAST_LEN = 512
GRID_W = 64
GRID_ROWS = DEC_SEQ // GRID_W
HEAD_DIM = 64
NA_HEADS = 4
NA_WIN_ROWS = 8
NA_WIN_COLS = 16
DIFF_HEADS = 4
DIFF_QK_DIM = 32
DIFF_V_DIM = 64
MLA_HEADS = 4
MLA_NOPE = 64
MLA_ROPE = 32
MLA_V = 64
MLA_KV_RANK = 128
SGU_GROUPS = 4
SGU_GROUP_DIM = 64
SGU_CHUNK = 128
D_FF = 4 * D_MODEL
ROPE_BASE = 10000.0
EPS = 1e-6
NEG_INF = -1e30
LOG2E = 1.4426950408889634

N_HEADS = 4
N_PAIRS = N_HEADS // 2
LANES = 128
WIDTH = 256
N_PROMPT = BATCH * SEQ
N_SAMPLE = DEC_BATCH * DEC_SEQ
N_TOK = N_PROMPT + N_SAMPLE
N_MOD_ROWS = 8

SEG_A = 3 * WIDTH
SEG_B = 3 * WIDTH
SEG_C = 640
SEG_D = 2 * WIDTH
SEG_C_PAD = 96
IN_COLS_P = SEG_A + SEG_B + SEG_C + SEG_D
O_ATT = 3 * WIDTH

TM = 512
TILES_PROMPT = N_PROMPT // TM
TILES_SAMPLE = N_SAMPLE // TM
PB = 2
QB = 512
QB_EXACT = 64
VMEM_LIMIT = 56 * 1024 * 1024


def _bf(x):
    return x.astype(jnp.bfloat16)


def _dot(a, b):
    return jnp.dot(a, b, preferred_element_type=jnp.float32)


def _dot_nt(a, b):
    return lax.dot_general(a, b, (((1,), (1,)), ((), ())), preferred_element_type=jnp.float32)


def _rms(x, g):
    ms = jnp.mean(x * x, axis=-1, keepdims=True)
    return x * lax.rsqrt(ms + EPS) * g


def _lane_range(lo, hi, width=LANES):
    lane = lax.broadcasted_iota(jnp.int32, (1, width), 1)
    return (lane >= lo) & (lane < hi)


def _with_ones(v):
    return jnp.concatenate([v, jnp.ones((v.shape[0], LANES), jnp.bfloat16)], axis=1)


def _attend(scores, values, log=None, bound=None):
    if bound is None:
        m = functools.reduce(jnp.maximum, [jnp.max(s, axis=-1, keepdims=True) for s in scores])
    else:
        m = bound
    o = functools.reduce(lambda a, b: a + b, [_dot(_bf(jnp.exp2(s - m)), v) for s, v in zip(scores, values)])
    if bound is not None:
        den = jnp.min(o[:, LANES:2 * LANES], axis=0, keepdims=True)
        log.least = jnp.minimum(log.least, jnp.min(den, axis=1, keepdims=True))
    return o


BOUND_SLACK = 1.02
MIN_DENOMINATOR = 2.0 ** -88


class _ShiftLog:
    def __init__(self):
        self.least = jnp.full((1, 1), jnp.inf, jnp.float32)

    def unsafe(self):
        return jnp.logical_not(self.least[0, 0] >= MIN_DENOMINATOR)


def _group_matrix(width, n_groups, extra=None):
    i = lax.broadcasted_iota(jnp.int32, (width, LANES), 0)
    j = lax.broadcasted_iota(jnp.int32, (width, LANES), 1)
    size = LANES // n_groups
    hit = (i // size == j // size) & (i < LANES)
    for t in range(n_groups if extra else 0):
        lo, hi = extra(t)
        hit = hit | ((i >= lo) & (i < hi) & (j // size == t))
    return jnp.where(hit, 1.0, 0.0).astype(jnp.bfloat16)


def _squares(x):
    xf = x.astype(jnp.float32)
    return _bf(xf * xf)


def _key_bound(keys, groups):
    return functools.reduce(jnp.maximum, [jnp.max(_dot(_squares(k), groups), axis=0, keepdims=True) for k in keys])


def _bound(log, q_parts, k2max, n_groups, extra=0.0):
    if log is None:
        return None
    size = LANES // n_groups
    q_norm = jnp.sqrt(functools.reduce(lambda a, b: a + b,
                                       [jnp.sum(q * q, axis=-1, keepdims=True) for q in q_parts])) * BOUND_SLACK
    return jnp.concatenate([q_norm * jnp.sqrt(k2max[:, size * t:size * t + 1]) + extra for t in range(n_groups)], axis=0)


def _loop(n, log, body):
    if log is not None:
        for i in range(n):
            body(i, log)
    else:
        lax.fori_loop(0, n, lambda i, carry: body(i, None) or carry, 0)


def _aligned(start, multiple):
    return start if isinstance(start, int) else pl.multiple_of(start, multiple)


def _normalised(o_ext):
    return o_ext[:, 0:LANES] * (1.0 / o_ext[:, LANES:2 * LANES])


def _swap8(x):
    lane = lax.broadcasted_iota(jnp.int32, (1, LANES), 1)
    return jnp.where((lane & 15) < 8, pltpu.roll(x, LANES - 8, 1), pltpu.roll(x, 8, 1))


def _rope(x, cos, sin):
    outs = []
    for c in range(x.shape[1] // LANES):
        sl = slice(LANES * c, LANES * (c + 1))
        xc = x[:, sl]
        outs.append(xc * cos[:, sl] + _swap8(xc) * sin[:, sl])
    return outs[0] if len(outs) == 1 else jnp.concatenate(outs, axis=1)


def _tile4(x):
    return x + pltpu.roll(x, 32, 1) + pltpu.roll(x, 64, 1) + pltpu.roll(x, 96, 1)


def _params(*sem):
    return pltpu.CompilerParams(dimension_semantics=sem, vmem_limit_bytes=VMEM_LIMIT)


ADA_TN = 1536


def _ada_kernel(m_ref, w_ref, b_ref, o_ref):
    m = m_ref[...]
    s = m * jax.nn.sigmoid(m)
    o_ref[0] = _dot(_bf(s), _bf(w_ref[0])) + b_ref[0]


def _ada(m, w_ada, b_ada):
    n = 6 * D_MODEL
    return pl.pallas_call(
        _ada_kernel,
        grid=(DEPTH, n // ADA_TN),
        in_specs=[
            pl.BlockSpec((N_MOD_ROWS, D_MODEL), lambda l, j: (0, 0)),
            pl.BlockSpec((1, D_MODEL, ADA_TN), lambda l, j: (l, 0, j)),
            pl.BlockSpec((1, 1, ADA_TN), lambda l, j: (l, 0, j)),
        ],
        out_specs=pl.BlockSpec((1, N_MOD_ROWS, ADA_TN), lambda l, j: (l, 0, j)),
        out_shape=jax.ShapeDtypeStruct((DEPTH, N_MOD_ROWS, n), jnp.float32),
        compiler_params=_params("parallel", "parallel"),
        name="ada",
    )(m, w_ada, b_ada.reshape(DEPTH, 1, n))


def _lam_kernel(lq1_ref, lk1_ref, lq2_ref, lk2_ref, init_ref, o_ref):
    init = init_ref[...]
    a = jnp.exp(jnp.sum(lq1_ref[...] * lk1_ref[...], axis=-1, keepdims=True))
    b = jnp.exp(jnp.sum(lq2_ref[...] * lk2_ref[...], axis=-1, keepdims=True))
    lam = a - b + init
    post = 1.0 - init
    for l in range(DEPTH):
        o_ref[l, 0:1, :] = jnp.broadcast_to(lam[l:l + 1], (1, LANES))
        o_ref[l, 1:2, :] = jnp.broadcast_to(post[l:l + 1], (1, LANES))


def _lam_consts(lq1, lk1, lq2, lk2):
    init = np.array([[0.8 - 0.6 * math.exp(-0.3 * l)] for l in range(DEPTH)], np.float32)
    return pl.pallas_call(
        _lam_kernel,
        out_shape=jax.ShapeDtypeStruct((DEPTH, 2, LANES), jnp.float32),
        name="diff_lambda",
    )(lq1, lk1, lq2, lk2, jnp.asarray(init))


N_DROW = 2 * NA_WIN_ROWS - 1
N_DCOL = 2 * NA_WIN_COLS - 1


def _bias_kernel(rpb_ref, o_ref):
    l = pl.program_id(0)
    h = pl.program_id(1)
    base = (l * NA_HEADS + h) * (N_DROW * N_DCOL)
    cq = lax.broadcasted_iota(jnp.int32, (GRID_W, LANES), 0)
    lane = lax.broadcasted_iota(jnp.int32, (GRID_W, LANES), 1)
    ck = lane & (GRID_W - 1)
    dcol = jnp.clip(ck - cq, -(NA_WIN_COLS - 1), NA_WIN_COLS - 1) + (NA_WIN_COLS - 1)
    hi = lane >= GRID_W
    for a in range(N_DROW - 1):
        acc = jnp.zeros((GRID_W, LANES), jnp.float32)
        for j in range(N_DCOL):
            lo_v = rpb_ref[base + a * N_DCOL + j]
            hi_v = rpb_ref[base + (a + 1) * N_DCOL + j]
            acc = jnp.where(dcol == j, jnp.where(hi, hi_v, lo_v), acc)
        o_ref[0, 0, a] = acc * LOG2E


def _bias_tiles(na_rpb):
    return pl.pallas_call(
        _bias_kernel,
        grid=(DEPTH, NA_HEADS),
        in_specs=[pl.BlockSpec(memory_space=pltpu.SMEM)],
        out_specs=pl.BlockSpec((1, 1, N_DROW - 1, GRID_W, LANES), lambda l, h: (l, h, 0, 0, 0)),
        out_shape=jax.ShapeDtypeStruct((DEPTH, NA_HEADS, N_DROW - 1, GRID_W, LANES), jnp.float32),
        compiler_params=_params("parallel", "parallel"),
        name="na_bias_tiles",
    )(na_rpb.reshape(-1))


def _row_group(i):
    return jnp.where(i < TILES_PROMPT, 0, 1 + (i - TILES_PROMPT) // (DEC_SEQ // TM))


def _split_specs(width, first):
    return [pl.BlockSpec((TM, width), lambda i: (jnp.minimum(first + i, TILES_PROMPT - 1), 0)),
            pl.BlockSpec((TM, width), lambda i: (jnp.maximum(first + i - TILES_PROMPT, 0), 0))]


def _x_specs(split, first):
    if not split:
        return [pl.BlockSpec((TM, D_MODEL), lambda i: (first + i, 0))]
    return _split_specs(D_MODEL, first)


def _read_tile(refs, first):
    if len(refs) == 1:
        return refs[0][...]
    return jnp.where(first + pl.program_id(0) < TILES_PROMPT, refs[0][...], refs[1][...])


IN_COLS = 2592
IN_QC, IN_CKV, IN_D = 1536, 1920, 2080
TR_ROWS = 256


def _gelu_tanh(x):
    return 0.5 * x * (1.0 + jnp.tanh(math.sqrt(2.0 / math.pi) * (x + 0.044715 * (x * x * x))))


def _sgu(pd, g, w_ref, bt):
    u = _gelu_tanh(pd[:, 0:WIDTH])
    v = _gelu_tanh(pd[:, WIDTH:2 * WIDTH])
    grp = lax.broadcasted_iota(jnp.int32, (1, WIDTH), 1) // SGU_GROUP_DIM
    v2 = v * v
    ms = jnp.zeros_like(v)
    for gi in range(SGU_GROUPS):
        sel = grp == gi
        tot = jnp.sum(jnp.where(sel, v2, 0.0), axis=-1, keepdims=True)
        ms = jnp.where(sel, tot * (1.0 / SGU_GROUP_DIM), ms)
    vg = _bf(v * lax.rsqrt(ms + EPS) * g)
    outs = []
    for c in range(pd.shape[0] // SGU_CHUNK):
        rows = slice(SGU_CHUNK * c, SGU_CHUNK * (c + 1))
        mixed = jnp.zeros((SGU_CHUNK, WIDTH), jnp.float32)
        for gi in range(SGU_GROUPS):
            full = _dot(_bf(w_ref[0, gi]), vg[rows]) + bt[:, gi:gi + 1]
            mixed = jnp.where(grp == gi, full, mixed)
        outs.append(u[rows] * mixed)
    return jnp.concatenate(outs, axis=0)


def _w_in_row_pieces():
    qn = [(IN_QC + 96 * h, MLA_NOPE) for h in range(MLA_HEADS)]
    qp = [(IN_QC + 96 * h + MLA_NOPE, MLA_ROPE) for h in range(MLA_HEADS)]
    seg_c = qn + qp + [(IN_CKV, MLA_KV_RANK + MLA_ROPE)]
    return (0, SEG_A + SEG_B), seg_c, (IN_D, SEG_D)


def _load_w_in(wt_ref, w_scr):
    ab, seg_c, d = _w_in_row_pieces()
    c_rows = jnp.concatenate([wt_ref[0, s:s + n, :] for s, n in seg_c]
                             + [jnp.zeros((SEG_C_PAD, D_MODEL), jnp.float32)], axis=0)
    for t in range(SEG_C // LANES):
        w_scr[:, SEG_A + SEG_B + LANES * t:SEG_A + SEG_B + LANES * (t + 1)] = _bf(c_rows[LANES * t:LANES * (t + 1)].T)
    for (src, n), dst in ((ab, 0), (d, SEG_A + SEG_B + SEG_C)):
        for t in range(n // TR_ROWS):
            rows = wt_ref[0, src + TR_ROWS * t:src + TR_ROWS * (t + 1), :]
            w_scr[:, dst + TR_ROWS * t:dst + TR_ROWS * (t + 1)] = _bf(rows.T)


def _inproj_kernel(n_x, *refs):
    x_refs = refs[:n_x]
    (g_ref, sh_ref, sc_ref, wt_ref, sg_ref, sw_ref, sbt_ref, pa_ref, pb_ref, pc_ref, od_ref, w_scr) = refs[n_x:]

    @pl.when(pl.program_id(0) == 0)
    def _():
        _load_w_in(wt_ref, w_scr)

    h = _rms(_read_tile(x_refs, 0), g_ref[0]) * (1.0 + sc_ref[...]) + sh_ref[...]
    hb = _bf(h)
    off = SEG_A + SEG_B + SEG_C
    od_ref[...] = _sgu(_dot(hb, w_scr[:, off:off + SEG_D]), sg_ref[0], sw_ref, sbt_ref[0])
    off = 0
    for ref in (pa_ref, pb_ref, pc_ref):
        n = ref.shape[1]
        ref[...] = _dot(hb, w_scr[:, off:off + n])
        off += n


def _inproj(l, xs, g_mix, mod, w_in_t, sgu_g, sgu_w, sgu_bt):
    def mod_spec(j):
        return pl.BlockSpec((None, None, 1, D_MODEL), lambda i: (l, _row_group(i), 0, j))

    widths = (SEG_A, SEG_B, SEG_C, WIDTH)
    return pl.pallas_call(
        functools.partial(_inproj_kernel, len(xs)),
        grid=(N_TOK // TM,),
        in_specs=_x_specs(len(xs) == 2, 0) + [
            pl.BlockSpec((1, 1, D_MODEL), lambda i: (l, 0, 0)),
            mod_spec(0), mod_spec(1),
            pl.BlockSpec((1, IN_COLS, D_MODEL), lambda i: (l, 0, 0), pipeline_mode=pl.Buffered(1)),
            pl.BlockSpec((1, 1, WIDTH), lambda i: (l, 0, 0)),
            pl.BlockSpec((1, SGU_GROUPS, SGU_CHUNK, SGU_CHUNK), lambda i: (l, 0, 0, 0)),
            pl.BlockSpec((1, SGU_CHUNK, SGU_GROUPS), lambda i: (l, 0, 0)),
        ],
        out_specs=[pl.BlockSpec((TM, n), lambda i: (i, 0)) for n in widths],
        out_shape=[jax.ShapeDtypeStruct((N_TOK, n), jnp.float32) for n in widths],
        scratch_shapes=[pltpu.VMEM((D_MODEL, IN_COLS_P), jnp.bfloat16)],
        compiler_params=_params("arbitrary"),
        name="inproj",
    )(*xs, g_mix, mod, mod, w_in_t, sgu_g, sgu_w, sgu_bt)


C_QN, C_QP, C_CKV, C_KPE = 0, 256, 384, 512


def _stack_heads(qp):
    lo = _lane_range(0, 64)
    return jnp.concatenate([_bf(jnp.where(lo, qp, 0.0)), _bf(jnp.where(lo, 0.0, qp))], axis=0)


def _unstack_heads(o, n):
    return jnp.where(_lane_range(0, 64), o[0:n], o[n:2 * n])


def _pair_t(c_ref):
    return jnp.concatenate([c_ref[0], c_ref[1]], axis=0)


def _stack_components(qp):
    return jnp.concatenate([_bf(jnp.where(_lane_range(32 * t, 32 * (t + 1)), qp, 0.0)) for t in range(4)], axis=0)


def _group_mean_sq(x, groups, size):
    sq = x * x
    hi = _bf(sq)
    rest = sq - hi.astype(jnp.float32)
    mid = _bf(rest)
    lo = _bf(rest - mid.astype(jnp.float32))
    return (_dot(hi, groups) + _dot(mid, groups) + _dot(lo, groups)) * (1.0 / size)


def _diff_finish(o, n, lam, post, g2, by_head):
    den = o[:, LANES:2 * LANES]
    outs = []
    for t in range(2):
        p1 = o[2 * t * n:(2 * t + 1) * n, 0:LANES] * (1.0 / den[2 * t * n:(2 * t + 1) * n])
        p2 = o[(2 * t + 1) * n:(2 * t + 2) * n, 0:LANES] * (lam / den[(2 * t + 1) * n:(2 * t + 2) * n])
        outs.append(p1 - p2)
    d = jnp.where(_lane_range(0, 64), outs[0], outs[1])
    return d * lax.rsqrt(_group_mean_sq(d, by_head, DIFF_V_DIM) + EPS) * g2 * post


def _mla_groups(j):
    return _group_matrix(2 * LANES, 2,
                         lambda t: (LANES + MLA_ROPE * (2 * j + t), LANES + MLA_ROPE * (2 * j + t + 1)))


def _mla_queries(qn_pair, qp_all, j):
    halves = []
    for t in range(2):
        h = 2 * j + t
        halves.append(jnp.concatenate([
            _bf(jnp.where(_lane_range(64 * t, 64 * (t + 1)), qn_pair, 0.0)),
            _bf(jnp.where(_lane_range(MLA_ROPE * h, MLA_ROPE * (h + 1)), qp_all, 0.0))], axis=1))
    return jnp.concatenate(halves, axis=0)


def _write_heads_t(p_ref, rows, col0, out_ref, bb):
    xt = p_ref[rows, col0:col0 + WIDTH].T
    for h in range(N_HEADS):
        out_ref[bb, 0, h] = xt[64 * h:64 * (h + 1)]
    _clear_other_layers(out_ref, bb)


def _clear_other_layers(out_ref, bb):
    if out_ref.shape[1] > 1:
        out_ref[bb, 1:] = jnp.zeros(out_ref.shape[1:], jnp.float32)[1:]


def _mix_prompt_kernel(n_prev, *refs):
    ins, outs = refs[:8], refs[8 + n_prev:]
    log = _ShiftLog()
    _mix_prompt_pass(ins, outs, log)

    @pl.when(log.unsafe())
    def _():
        _mix_prompt_pass(ins, outs, None)


def _mix_prompt_pass(ins, outs, log):
    pa_ref, pb_ref, pc_ref, cst_ref, gsub_ref, gckv_ref, wuk_ref, wuv_ref = ins
    o_ref, nak_ref, nav_ref, dk_ref, dv_ref, ckv_ref, kpe_ref = outs
    first_pass = log is not None
    c_a = HEAD_DIM ** -0.5 * LOG2E
    c_b = DIFF_QK_DIM ** -0.5 * LOG2E
    c_c = (MLA_NOPE + MLA_ROPE) ** -0.5 * LOG2E
    lam = cst_ref[0, 0:1, 0:1]
    post = cst_ref[0, 1:2, 0:1]
    wuk, wuv = _bf(wuk_ref[0]), _bf(wuv_ref[0])
    by_head, by_comp = _group_matrix(LANES, 2), _group_matrix(LANES, 4)

    def sequence(bb, log):
        rows = pl.ds(_aligned(bb * SEQ, SEQ), SEQ)
        for j in range(N_PAIRS):
            cols = slice(LANES * j, LANES * (j + 1))
            k = _bf(pa_ref[rows, WIDTH + LANES * j:WIDTH + LANES * (j + 1)])
            v = _with_ones(_bf(pa_ref[rows, 2 * WIDTH + LANES * j:2 * WIDTH + LANES * (j + 1)]))
            q = pa_ref[rows, cols] * c_a
            o = _attend([_dot_nt(_stack_heads(q), k)], [v], log, _bound(log, [q], _key_bound([k], by_head), 2))
            o_ref[rows, cols] = _unstack_heads(_normalised(o), SEQ)
        if first_pass:
            _write_heads_t(pa_ref, rows, WIDTH, nak_ref, bb)
            _write_heads_t(pa_ref, rows, 2 * WIDTH, nav_ref, bb)
        for j in range(N_PAIRS):
            cols = slice(LANES * j, LANES * (j + 1))
            k = _bf(pb_ref[rows, WIDTH + LANES * j:WIDTH + LANES * (j + 1)])
            v = _with_ones(_bf(pb_ref[rows, 2 * WIDTH + LANES * j:2 * WIDTH + LANES * (j + 1)]))
            q = pb_ref[rows, cols] * c_b
            o = _attend([_dot_nt(_stack_components(q), k)], [v], log, _bound(log, [q], _key_bound([k], by_comp), 4))
            o_ref[rows, WIDTH + LANES * j:WIDTH + LANES * (j + 1)] = _diff_finish(o, SEQ, lam, post, gsub_ref[0], by_head)
        if first_pass:
            _write_heads_t(pb_ref, rows, WIDTH, dk_ref, bb)
            _write_heads_t(pb_ref, rows, 2 * WIDTH, dv_ref, bb)
        ckv = _rms(pc_ref[rows, C_CKV:C_CKV + MLA_KV_RANK], gckv_ref[0])
        kpe_slot = pc_ref[rows, C_KPE:C_KPE + LANES]
        if first_pass:
            ckv_ref[bb, 0] = ckv
            _clear_other_layers(ckv_ref, bb)
            kpe_ref[bb, 0] = kpe_slot.T[0:MLA_ROPE]
            _clear_other_layers(kpe_ref, bb)
        ckv_b = _bf(ckv)
        kn = _bf(_dot(ckv_b, wuk))
        vv = _bf(_dot(ckv_b, wuv))
        kpe4 = _bf(_tile4(kpe_slot))
        qn = pc_ref[rows, C_QN:C_QN + WIDTH] * c_c
        qp = pc_ref[rows, C_QP:C_QP + LANES] * c_c
        for j in range(N_PAIRS):
            cols = slice(LANES * j, LANES * (j + 1))
            k = jnp.concatenate([kn[:, cols], kpe4], axis=1)
            qs = _mla_queries(qn[:, cols], qp, j)
            groups = _mla_groups(j)
            o = _attend([_dot_nt(qs, k)], [_with_ones(vv[:, cols])], log,
                        _bound(log, [qn[:, cols], qp], _key_bound([k], groups), 2))
            o_ref[rows, 2 * WIDTH + LANES * j:2 * WIDTH + LANES * (j + 1)] = _unstack_heads(_normalised(o), SEQ)

    _loop(PB, log, sequence)


def _mix_prompt(l, pa, pb, pc, cst, g_sub2, g_ckv, w_uk, w_uv, prev):
    n_prev = len(prev)
    tails = [(NA_HEADS, HEAD_DIM, SEQ)] * 2 + [(DIFF_HEADS, 64, SEQ)] * 2 + [(SEQ, MLA_KV_RANK), (MLA_ROPE, SEQ)]

    def cache_spec(tail):
        if l == 0:
            return pl.BlockSpec((PB, DEPTH) + tail, lambda b: (b, 0) + (0,) * len(tail))
        return pl.BlockSpec((PB, 1) + tail, lambda b: (b, l) + (0,) * len(tail))

    def rows(width):
        return pl.BlockSpec((PB * SEQ, width), lambda b: (b, 0))

    def layer(*tail):
        return pl.BlockSpec((1,) + tail, lambda b: (l,) + (0,) * len(tail))

    return pl.pallas_call(
        functools.partial(_mix_prompt_kernel, n_prev),
        grid=(BATCH // PB,),
        in_specs=[rows(SEG_A), rows(SEG_B), rows(SEG_C), layer(2, LANES), layer(1, LANES), layer(1, MLA_KV_RANK),
                  layer(MLA_KV_RANK, WIDTH), layer(MLA_KV_RANK, WIDTH)] + [pl.BlockSpec(memory_space=pl.ANY)] * n_prev,
        out_specs=[rows(O_ATT)] + [cache_spec(t) for t in tails],
        out_shape=[jax.ShapeDtypeStruct((N_PROMPT, O_ATT), jnp.float32)]
        + [jax.ShapeDtypeStruct((BATCH, DEPTH) + t, jnp.float32) for t in tails],
        input_output_aliases={8 + i: 1 + i for i in range(n_prev)},
        compiler_params=_params("parallel"),
        name="mix_prompt",
    )(pa, pb, pc, cst, g_sub2, g_ckv, w_uk, w_uv, *prev)


def _na_row_groups():
    kh = min(NA_WIN_ROWS, GRID_ROWS)
    r0s = [min(max(r - kh // 2, 0), GRID_ROWS - kh) for r in range(GRID_ROWS)]
    groups = []
    for r, r0 in enumerate(r0s):
        if groups and groups[-1][2] == r0:
            groups[-1][1] = r
        else:
            groups.append([r, r, r0])
    return kh, [tuple(g) for g in groups]


def _na_sample(q_ref, k_ref, v_ref, ck_ref, cv_ref, tt_ref, o_ref, log):
    c = HEAD_DIM ** -0.5 * LOG2E
    kh, groups = _na_row_groups()
    lk = kh * GRID_W
    edge = [g for g in groups if g[1] > g[0]]
    inner = [g for g in groups if g[1] == g[0]]
    depth = inner[0][0] - inner[0][2]
    assert all(g[0] - g[2] == depth for g in inner) and [g[0] for g in inner] == list(range(inner[0][0], inner[-1][0] + 1))

    def in_window(n):
        cq = lax.broadcasted_iota(jnp.int32, (n, lk), 0) & (GRID_W - 1)
        ck = lax.broadcasted_iota(jnp.int32, (n, lk), 1) & (GRID_W - 1)
        c0 = jnp.clip(cq - NA_WIN_COLS // 2, 0, GRID_W - NA_WIN_COLS)
        return (ck >= c0) & (ck < c0 + NA_WIN_COLS)

    kc_t = _bf(_pair_t(ck_ref))
    vc = _with_ones(_bf(_pair_t(cv_ref).T))
    by_head = _group_matrix(LANES, 2)
    k2max = _key_bound([_bf(k_ref[...]), _bf(_pair_t(ck_ref).T)], by_head)
    tmax = functools.reduce(jnp.maximum, [tt_ref[t, a] for t in range(2) for a in range(N_DROW - 1)])
    bplus = jnp.maximum(jnp.max(jnp.max(tmax, axis=-1, keepdims=True), axis=0, keepdims=True), 0.0)

    def group(row0, key0, offsets, log):
        n = len(offsets) * GRID_W
        rows, keys = pl.ds(row0, n), pl.ds(key0, lk)
        q = q_ref[rows, :] * c
        qg = _stack_heads(q)
        k = _bf(k_ref[keys, :])
        v = _with_ones(_bf(v_ref[keys, :]))
        bias = jnp.concatenate([
            jnp.concatenate([tt_ref[t, 2 * i - off + NA_WIN_ROWS - 1] for i in range(kh // 2)], axis=1)
            for t in range(2) for off in offsets], axis=0)
        s_loc = jnp.where(in_window(2 * n), _dot_nt(qg, k) + bias, NEG_INF)
        o = _attend([_dot(qg, kc_t), s_loc], [vc, v], log, _bound(log, [q], k2max, 2, bplus))
        o_ref[rows, :] = _unstack_heads(_normalised(o), n)

    if log is None:
        def any_row(r, log):
            r0 = jnp.clip(r - kh // 2, 0, GRID_ROWS - kh)
            group(_aligned(r * GRID_W, GRID_W), _aligned(r0 * GRID_W, GRID_W), [r - r0], log)

        _loop(GRID_ROWS, log, any_row)
        return

    for (r_lo, r_hi, r0) in edge:
        group(r_lo * GRID_W, r0 * GRID_W, [r - r0 for r in range(r_lo, r_hi + 1)], log)

    def inner_row(i, log):
        r = inner[0][0] + i
        group(r * GRID_W, (r - depth) * GRID_W, [depth], log)

    _loop(len(inner), log, inner_row)


def _diff_sample(q_ref, k_ref, v_ref, ck_ref, cv_ref, cos_ref, sin_ref, cst_ref, g_ref, o_ref, log):
    c = DIFF_QK_DIM ** -0.5 * LOG2E
    lam = cst_ref[0, 0:1, 0:1]
    post = cst_ref[0, 1:2, 0:1]
    k_new = _bf(_rope(k_ref[...], cos_ref[...], sin_ref[...]))
    kc_t = _bf(_pair_t(ck_ref))
    vc = _with_ones(_bf(_pair_t(cv_ref).T))
    v = _with_ones(_bf(v_ref[...]))
    by_head, by_comp = _group_matrix(LANES, 2), _group_matrix(LANES, 4)
    k2max = _key_bound([k_new, _bf(_pair_t(ck_ref).T)], by_comp)

    qb = QB_EXACT if log is None else QB

    def block(qi, log):
        rows = pl.ds(_aligned(qi * qb, qb), qb)
        q = _rope(q_ref[rows, :], cos_ref[rows, :], sin_ref[rows, :]) * c
        qs = _stack_components(q)
        o = _attend([_dot(qs, kc_t), _dot_nt(qs, k_new)], [vc, v], log, _bound(log, [q], k2max, 4))
        o_ref[rows, :] = _diff_finish(o, qb, lam, post, g_ref[0], by_head)

    _loop(DEC_SEQ // qb, log, block)


def _mla_sample(j, qn_ref, qp_ref, ckv_ref, kpe_ref, cckv_ref, ckpe_ref, cosq_ref, sinq_ref, cosk_ref, sink_ref,
                gckv_ref, wuk_ref, wuv_ref, o_ref, log):
    c = (MLA_NOPE + MLA_ROPE) ** -0.5 * LOG2E
    wuk, wuv = _bf(wuk_ref[0]), _bf(wuv_ref[0])
    ckv_new = _bf(_rms(ckv_ref[...], gckv_ref[0]))
    ckv_old = _bf(cckv_ref[...])
    kpe_new = _bf(_tile4(_rope(kpe_ref[...], cosk_ref[...], sink_ref[...])))
    kpe_old = _bf(jnp.concatenate([ckpe_ref[...]] * MLA_HEADS, axis=0).T)
    k_old = jnp.concatenate([_bf(_dot(ckv_old, wuk)), kpe_old], axis=1)
    k_new = jnp.concatenate([_bf(_dot(ckv_new, wuk)), kpe_new], axis=1)
    vo, vn = _with_ones(_bf(_dot(ckv_old, wuv))), _with_ones(_bf(_dot(ckv_new, wuv)))
    groups = _mla_groups(j)
    k2max = _key_bound([k_old, k_new], groups)

    qb = QB_EXACT if log is None else QB

    def block(qi, log):
        rows = pl.ds(_aligned(qi * qb, qb), qb)
        qp = _rope(qp_ref[rows, :], cosq_ref[rows, :], sinq_ref[rows, :]) * c
        qn = qn_ref[rows, :] * c
        qs = _mla_queries(qn, qp, j)
        o = _attend([_dot_nt(qs, k_old), _dot_nt(qs, k_new)], [vo, vn], log, _bound(log, [qn, qp], k2max, 2))
        o_ref[rows, :] = _unstack_heads(_normalised(o), qb)

    _loop(DEC_SEQ // qb, log, block)


def _mix_sample_kernel(qa_ref, ka_ref, va_ref, qb_ref, kb_ref, vb_ref, qn_ref, qp_ref, ckv_ref, kpe_ref,
                       cnak_ref, cnav_ref, cdk_ref, cdv_ref, cckv_ref, ckpe_ref, tt_ref,
                       cosb_ref, sinb_ref, cosq_ref, sinq_ref, cosk_ref, sink_ref,
                       cst_ref, gsub_ref, gckv_ref, wuk_ref, wuv_ref, oa_ref, ob_ref, oc_ref):
    j = pl.program_id(1)

    def run(log):
        _na_sample(qa_ref, ka_ref, va_ref, cnak_ref, cnav_ref, tt_ref, oa_ref, log)
        _diff_sample(qb_ref, kb_ref, vb_ref, cdk_ref, cdv_ref, cosb_ref, sinb_ref, cst_ref, gsub_ref, ob_ref, log)
        _mla_sample(j, qn_ref, qp_ref, ckv_ref, kpe_ref, cckv_ref, ckpe_ref, cosq_ref, sinq_ref, cosk_ref, sink_ref,
                    gckv_ref, wuk_ref, wuv_ref, oc_ref, log)

    log = _ShiftLog()
    run(log)

    @pl.when(log.unsafe())
    def _():
        run(None)


def _mix_sample(l, pa, pb, pc, caches_t, tt, tables, cst, g_sub2, g_ckv, w_uk, w_uv):
    first = N_PROMPT // DEC_SEQ

    def cols(block):
        return pl.BlockSpec((DEC_SEQ, LANES), lambda b, j: (first + b, block(j)))

    def cache(*tail, pair=False):
        return pl.BlockSpec((None, None) + tail, lambda b, j: (b, l, j if pair else 0) + (0,) * (len(tail) - 1))

    def layer(*tail):
        return pl.BlockSpec((1,) + tail, lambda b, j: (l,) + (0,) * len(tail))

    table = pl.BlockSpec((DEC_SEQ, LANES), lambda b, j: (0, 0), pipeline_mode=pl.Buffered(1))
    qkv = [cols(lambda j: j), cols(lambda j: N_PAIRS + j), cols(lambda j: 2 * N_PAIRS + j)]
    seg_c = [cols(lambda j: j), cols(lambda j: C_QP // LANES), cols(lambda j: C_CKV // LANES), cols(lambda j: C_KPE // LANES)]
    kv_t = cache(2, 64, PAST_LEN, pair=True)
    w_pair = pl.BlockSpec((1, MLA_KV_RANK, LANES), lambda b, j: (l, 0, j))
    out = pl.BlockSpec((DEC_SEQ, LANES), lambda b, j: (b, j))
    return pl.pallas_call(
        _mix_sample_kernel,
        grid=(DEC_BATCH, N_PAIRS),
        in_specs=qkv + qkv + seg_c + [kv_t, kv_t, kv_t, kv_t, cache(PAST_LEN, MLA_KV_RANK), cache(MLA_ROPE, PAST_LEN),
                                      pl.BlockSpec((None, 2, N_DROW - 1, GRID_W, LANES), lambda b, j: (l, j, 0, 0, 0)),
                                      table, table, table, table, table, table,
                                      layer(2, LANES), layer(1, LANES), layer(1, MLA_KV_RANK), w_pair, w_pair],
        out_specs=[out, out, out],
        out_shape=[jax.ShapeDtypeStruct((N_SAMPLE, WIDTH), jnp.float32)] * 3,
        compiler_params=_params("parallel", "arbitrary"),
        name="mix_sample",
    )(pa, pa, pa, pb, pb, pb, pc, pc, pc, pc, *caches_t, tt, *tables, cst, g_sub2, g_ckv, w_uk, w_uv)


FF_CHUNK = 1024


def _outffn_kernel(n_x, first, final, *refs):
    x_refs, op_ref, os_refs = refs[:n_x], refs[n_x], refs[n_x + 1:n_x + 4]
    (od_ref, wout_ref, g1_ref, gffn_ref, sh2_ref, sc2_ref, g2_ref, w1_ref, w2_ref, gfin_ref, y_ref) = refs[n_x + 4:]
    o_att = jnp.where(first + pl.program_id(0) < TILES_PROMPT, op_ref[...],
                      jnp.concatenate([r[...] for r in os_refs], axis=1))
    acc = (_dot(_bf(o_att), wout_ref[0, 0:O_ATT, :])
           + _dot(_bf(od_ref[...]), wout_ref[0, O_ATT:O_ATT + WIDTH, :]))
    x1 = _read_tile(x_refs, first) + g1_ref[...] * acc
    hf = _bf(_rms(x1, gffn_ref[0]) * (1.0 + sc2_ref[...]) + sh2_ref[...])
    acc = jnp.zeros((TM, D_MODEL), jnp.float32)
    for c in range(D_FF // FF_CHUNK):
        cols = slice(FF_CHUNK * c, FF_CHUNK * (c + 1))
        a = jnp.square(jnp.maximum(_dot(hf, w1_ref[0, :, cols]), 0.0))
        acc += _dot(_bf(a), w2_ref[0, cols, :])
    y = x1 + g2_ref[...] * acc
    if final:
        y = _rms(y, gfin_ref[...])
    y_ref[...] = y


def _outffn(l, xs, o_p, o_s, od, w_out, g_ffn, mod, w1, w2, g_final, first, n_tiles):
    def mod_spec(j):
        return pl.BlockSpec((None, None, 1, D_MODEL), lambda i: (l, _row_group(first + i), 0, j))

    def resident(shape):
        return pl.BlockSpec(shape, lambda i: (l,) + (0,) * (len(shape) - 1), pipeline_mode=pl.Buffered(1))

    return pl.pallas_call(
        functools.partial(_outffn_kernel, len(xs), first, l == DEPTH - 1),
        grid=(n_tiles,),
        in_specs=_x_specs(len(xs) == 2, first) + _split_specs(O_ATT, first)[:1] + _split_specs(WIDTH, first)[1:] * 3 + [
            pl.BlockSpec((TM, WIDTH), lambda i: (first + i, 0)),
            resident((1, 4 * WIDTH, D_MODEL)),
            mod_spec(2),
            pl.BlockSpec((1, 1, D_MODEL), lambda i: (l, 0, 0)),
            mod_spec(3), mod_spec(4), mod_spec(5),
            resident((1, D_MODEL, D_FF)),
            resident((1, D_FF, D_MODEL)),
            pl.BlockSpec((1, D_MODEL), lambda i: (0, 0)),
        ],
        out_specs=pl.BlockSpec((TM, D_MODEL), lambda i: (i, 0)),
        out_shape=jax.ShapeDtypeStruct((n_tiles * TM, D_MODEL), jnp.float32),
        compiler_params=_params("parallel"),
        name="outffn",
    )(*xs, o_p, *o_s, od, w_out, mod, g_ffn, mod, mod, mod, w1, w2, g_final)


def _rope32_tables():
    t = np.arange(DEC_SEQ)
    rows, cols = (t // GRID_W).astype(np.float64), (t % GRID_W).astype(np.float64)
    half = 8
    freqs = ROPE_BASE ** (-np.arange(half, dtype=np.float64) / half)
    cos, sin = [], []
    for pos in (rows, cols):
        ang = pos[:, None] * freqs[None, :]
        cos += [np.cos(ang), np.cos(ang)]
        sin += [-np.sin(ang), np.sin(ang)]
    return np.concatenate(cos, axis=1).astype(np.float32), np.concatenate(sin, axis=1).astype(np.float32)


def _rope_tables():
    c32, s32 = _rope32_tables()
    tile = lambda a, n: np.tile(a, (1, n))
    pad = np.zeros((DEC_SEQ, LANES - MLA_ROPE), np.float32)
    cos_k = np.concatenate([c32, pad + 1.0], axis=1)
    sin_k = np.concatenate([s32, pad], axis=1)
    return (tile(c32, 4), tile(s32, 4),
            cos_k, sin_k)


def kernel(x_prompt, x_sample, cache_na_k, cache_na_v, cache_diff_k, cache_diff_v, cache_mla_ckv, cache_mla_kpe, c, c_ctx, w_ada, b_ada, g_mix, g_ffn, w_in, w_out, na_rpb, diff_lq1, diff_lk1, diff_lq2, diff_lk2, diff_g_subln, mla_g_ckv, mla_w_uk, mla_w_uv, sgu_g, sgu_w, sgu_b, w_ff1, w_ff2, g_final):
    f32 = jnp.float32
    m = jnp.concatenate([c_ctx[None, :], c, jnp.zeros((N_MOD_ROWS - 1 - DEC_BATCH, D_MODEL), f32)], axis=0)
    mod = _ada(m, w_ada, b_ada).reshape(DEPTH, N_MOD_ROWS, 1, 6 * D_MODEL)
    cst = _lam_consts(diff_lq1, diff_lk1, diff_lq2, diff_lk2)
    tt = _bias_tiles(na_rpb)
    cos4, sin4, cos_k, sin_k = [jnp.asarray(t) for t in _rope_tables()]
    tables = (cos4, sin4, cos4, sin4, cos_k, sin_k)

    t_last = lambda a: jnp.swapaxes(a, -1, -2)
    w_in_t = t_last(w_in)
    caches_t = (t_last(cache_na_k), t_last(cache_na_v), t_last(cache_diff_k), t_last(cache_diff_v),
                cache_mla_ckv, t_last(cache_mla_kpe))
    w_out_b, w1_b, w2_b = _bf(w_out), _bf(w_ff1), _bf(w_ff2)
    g_mix3 = g_mix.reshape(DEPTH, 1, D_MODEL)
    g_ffn3 = g_ffn.reshape(DEPTH, 1, D_MODEL)
    g_sub2 = jnp.tile(diff_g_subln, (1, 2)).reshape(DEPTH, 1, LANES)
    g_ckv3 = mla_g_ckv.reshape(DEPTH, 1, MLA_KV_RANK)
    sgu_g3 = sgu_g.reshape(DEPTH, 1, WIDTH)
    sgu_bt = sgu_b.transpose(0, 2, 1)
    g_fin2 = g_final.reshape(1, D_MODEL)

    xs = (x_prompt.reshape(N_PROMPT, D_MODEL), x_sample.reshape(N_SAMPLE, D_MODEL))
    new = ()
    for l in range(DEPTH):
        pa, pb, pc, od = _inproj(l, xs, g_mix3, mod, w_in_t, sgu_g3, sgu_w, sgu_bt)
        o_p, *new = _mix_prompt(l, pa, pb, pc, cst, g_sub2, g_ckv3, mla_w_uk, mla_w_uv, new)
        o_s = _mix_sample(l, pa, pb, pc, caches_t, tt, tables, cst, g_sub2, g_ckv3, mla_w_uk, mla_w_uv)
        ffn = functools.partial(_outffn, l, xs, o_p, o_s, od, w_out_b, g_ffn3, mod, w1_b, w2_b, g_fin2)
        if l < DEPTH - 1:
            xs = (ffn(0, TILES_PROMPT + TILES_SAMPLE),)
        else:
            xs = (ffn(0, TILES_PROMPT), ffn(TILES_PROMPT, TILES_SAMPLE))
    y_prompt = xs[0].reshape(BATCH, SEQ, D_MODEL)
    y_sample = xs[1].reshape(DEC_BATCH, DEC_SEQ, D_MODEL)
    na_k, na_v, diff_k, diff_v, mla_ckv, mla_kpe = new
    return (y_prompt, y_sample, t_last(na_k), t_last(na_v), t_last(diff_k), t_last(diff_v), mla_ckv, t_last(mla_kpe))
```

```python
import functools
import math

import numpy as np
import jax
import jax.numpy as jnp
from jax import lax
from jax.experimental import pallas as pl
from jax.experimental.pallas import tpu as pltpu

D_MODEL = 1024
BATCH = 16
SEQ = 256
DEPTH = 4
DEC_BATCH = 2
DEC_SEQ = 1024
PAST_LEN = 512
GRID_W = 64
GRID_ROWS = DEC_SEQ // GRID_W
HEAD_DIM = 64
NA_HEADS = 4
NA_WIN_ROWS = 8
NA_WIN_COLS = 16
DIFF_HEADS = 4
DIFF_QK_DIM = 32
DIFF_V_DIM = 64
MLA_HEADS = 4
MLA_NOPE = 64
MLA_ROPE = 32
MLA_V = 64
MLA_KV_RANK = 128
SGU_GROUPS = 4
SGU_GROUP_DIM = 64
SGU_CHUNK = 128
D_FF = 4 * D_MODEL
ROPE_BASE = 10000.0
EPS = 1e-6
NEG_INF = -1e30
LOG2E = 1.4426950408889634

N_HEADS = 4
N_PAIRS = N_HEADS // 2
LANES = 128
WIDTH = 256
N_PROMPT = BATCH * SEQ
N_SAMPLE = DEC_BATCH * DEC_SEQ
N_TOK = N_PROMPT + N_SAMPLE
N_MOD_ROWS = 8

SEG_A = 3 * WIDTH
SEG_B = 3 * WIDTH
SEG_C = 640
SEG_D = 2 * WIDTH
SEG_C_PAD = 96
IN_COLS_P = SEG_A + SEG_B + SEG_C + SEG_D
O_ATT = 3 * WIDTH

TM = 512
TILES_PROMPT = N_PROMPT // TM
TILES_SAMPLE = N_SAMPLE // TM
PB = 2
QB = 512
QB_EXACT = 64
VMEM_LIMIT = 56 * 1024 * 1024


def _bf(x):
    return x.astype(jnp.bfloat16)


def _dot(a, b):
    return jnp.dot(a, b, preferred_element_type=jnp.float32)


def _dot_nt(a, b):
    return lax.dot_general(a, b, (((1,), (1,)), ((), ())), preferred_element_type=jnp.float32)


def _rms(x, g):
    ms = jnp.mean(x * x, axis=-1, keepdims=True)
    return x * lax.rsqrt(ms + EPS) * g


def _lane_range(lo, hi, width=LANES):
    lane = lax.broadcasted_iota(jnp.int32, (1, width), 1)
    return (lane >= lo) & (lane < hi)


def _with_ones(v):
    return jnp.concatenate([v, jnp.ones((v.shape[0], LANES), jnp.bfloat16)], axis=1)


def _attend(scores, values, log=None, bound=None):
    if bound is None:
        m = functools.reduce(jnp.maximum, [jnp.max(s, axis=-1, keepdims=True) for s in scores])
    else:
        m = bound
    o = functools.reduce(lambda a, b: a + b, [_dot(_bf(jnp.exp2(s - m)), v) for s, v in zip(scores, values)])
    if bound is not None:
        den = jnp.min(o[:, LANES:2 * LANES], axis=0, keepdims=True)
        log.least = jnp.minimum(log.least, jnp.min(den, axis=1, keepdims=True))
    return o


BOUND_SLACK = 1.02
MIN_DENOMINATOR = 2.0 ** -88


class _ShiftLog:
    def __init__(self):
        self.least = jnp.full((1, 1), jnp.inf, jnp.float32)

    def unsafe(self):
        return jnp.logical_not(self.least[0, 0] >= MIN_DENOMINATOR)


def _group_matrix(width, n_groups, extra=None):
    i = lax.broadcasted_iota(jnp.int32, (width, LANES), 0)
    j = lax.broadcasted_iota(jnp.int32, (width, LANES), 1)
    size = LANES // n_groups
    hit = (i // size == j // size) & (i < LANES)
    for t in range(n_groups if extra else 0):
        lo, hi = extra(t)
        hit = hit | ((i >= lo) & (i < hi) & (j // size == t))
    return jnp.where(hit, 1.0, 0.0).astype(jnp.bfloat16)


def _squares(x):
    xf = x.astype(jnp.float32)
    return _bf(xf * xf)


def _key_bound(keys, groups):
    return functools.reduce(jnp.maximum, [jnp.max(_dot(_squares(k), groups), axis=0, keepdims=True) for k in keys])


def _bound(log, q_parts, k2max, n_groups, extra=0.0):
    if log is None:
        return None
    size = LANES // n_groups
    q_norm = jnp.sqrt(functools.reduce(lambda a, b: a + b,
                                       [jnp.sum(q * q, axis=-1, keepdims=True) for q in q_parts])) * BOUND_SLACK
    return jnp.concatenate([q_norm * jnp.sqrt(k2max[:, size * t:size * t + 1]) + extra for t in range(n_groups)], axis=0)


def _loop(n, log, body):
    if log is not None:
        for i in range(n):
            body(i, log)
    else:
        lax.fori_loop(0, n, lambda i, carry: body(i, None) or carry, 0)


def _aligned(start, multiple):
    return start if isinstance(start, int) else pl.multiple_of(start, multiple)


def _normalised(o_ext):
    return o_ext[:, 0:LANES] * (1.0 / o_ext[:, LANES:2 * LANES])


def _swap8(x):
    lane = lax.broadcasted_iota(jnp.int32, (1, LANES), 1)
    return jnp.where((lane & 15) < 8, pltpu.roll(x, LANES - 8, 1), pltpu.roll(x, 8, 1))


def _rope(x, cos, sin):
    outs = []
    for c in range(x.shape[1] // LANES):
        sl = slice(LANES * c, LANES * (c + 1))
        xc = x[:, sl]
        outs.append(xc * cos[:, sl] + _swap8(xc) * sin[:, sl])
    return outs[0] if len(outs) == 1 else jnp.concatenate(outs, axis=1)


def _tile4(x):
    return x + pltpu.roll(x, 32, 1) + pltpu.roll(x, 64, 1) + pltpu.roll(x, 96, 1)


def _params(*sem):
    return pltpu.CompilerParams(dimension_semantics=sem, vmem_limit_bytes=VMEM_LIMIT)


ADA_TN = 1536


def _ada_kernel(m_ref, w_ref, b_ref, o_ref):
    m = m_ref[...]
    s = m * jax.nn.sigmoid(m)
    o_ref[0] = _dot(_bf(s), _bf(w_ref[0])) + b_ref[0]


def _ada(m, w_ada, b_ada):
    n = 6 * D_MODEL
    return pl.pallas_call(
        _ada_kernel,
        grid=(DEPTH, n // ADA_TN),
        in_specs=[
            pl.BlockSpec((N_MOD_ROWS, D_MODEL), lambda l, j: (0, 0)),
            pl.BlockSpec((1, D_MODEL, ADA_TN), lambda l, j: (l, 0, j)),
            pl.BlockSpec((1, 1, ADA_TN), lambda l, j: (l, 0, j)),
        ],
        out_specs=pl.BlockSpec((1, N_MOD_ROWS, ADA_TN), lambda l, j: (l, 0, j)),
        out_shape=jax.ShapeDtypeStruct((DEPTH, N_MOD_ROWS, n), jnp.float32),
        compiler_params=_params("parallel", "parallel"),
        name="ada",
    )(m, w_ada, b_ada.reshape(DEPTH, 1, n))


def _lam_kernel(lq1_ref, lk1_ref, lq2_ref, lk2_ref, init_ref, o_ref):
    init = init_ref[...]
    a = jnp.exp(jnp.sum(lq1_ref[...] * lk1_ref[...], axis=-1, keepdims=True))
    b = jnp.exp(jnp.sum(lq2_ref[...] * lk2_ref[...], axis=-1, keepdims=True))
    lam = a - b + init
    post = 1.0 - init
    for l in range(DEPTH):
        o_ref[l, 0:1, :] = jnp.broadcast_to(lam[l:l + 1], (1, LANES))
        o_ref[l, 1:2, :] = jnp.broadcast_to(post[l:l + 1], (1, LANES))


def _lam_consts(lq1, lk1, lq2, lk2):
    init = np.array([[0.8 - 0.6 * math.exp(-0.3 * l)] for l in range(DEPTH)], np.float32)
    return pl.pallas_call(
        _lam_kernel,
        out_shape=jax.ShapeDtypeStruct((DEPTH, 2, LANES), jnp.float32),
        name="diff_lambda",
    )(lq1, lk1, lq2, lk2, jnp.asarray(init))


N_DROW = 2 * NA_WIN_ROWS - 1
N_DCOL = 2 * NA_WIN_COLS - 1


def _bias_kernel(rpb_ref, o_ref):
    l = pl.program_id(0)
    h = pl.program_id(1)
    base = (l * NA_HEADS + h) * (N_DROW * N_DCOL)
    cq = lax.broadcasted_iota(jnp.int32, (GRID_W, LANES), 0)
    lane = lax.broadcasted_iota(jnp.int32, (GRID_W, LANES), 1)
    ck = lane & (GRID_W - 1)
    dcol = jnp.clip(ck - cq, -(NA_WIN_COLS - 1), NA_WIN_COLS - 1) + (NA_WIN_COLS - 1)
    hi = lane >= GRID_W
    for a in range(N_DROW - 1):
        acc = jnp.zeros((GRID_W, LANES), jnp.float32)
        for j in range(N_DCOL):
            lo_v = rpb_ref[base + a * N_DCOL + j]
            hi_v = rpb_ref[base + (a + 1) * N_DCOL + j]
            acc = jnp.where(dcol == j, jnp.where(hi, hi_v, lo_v), acc)
        o_ref[0, 0, a] = acc * LOG2E


def _bias_tiles(na_rpb):
    return pl.pallas_call(
        _bias_kernel,
        grid=(DEPTH, NA_HEADS),
        in_specs=[pl.BlockSpec(memory_space=pltpu.SMEM)],
        out_specs=pl.BlockSpec((1, 1, N_DROW - 1, GRID_W, LANES), lambda l, h: (l, h, 0, 0, 0)),
        out_shape=jax.ShapeDtypeStruct((DEPTH, NA_HEADS, N_DROW - 1, GRID_W, LANES), jnp.float32),
        compiler_params=_params("parallel", "parallel"),
        name="na_bias_tiles",
    )(na_rpb.reshape(-1))


def _row_group(i):
    return jnp.where(i < TILES_PROMPT, 0, 1 + (i - TILES_PROMPT) // (DEC_SEQ // TM))


def _split_specs(width, first):
    return [pl.BlockSpec((TM, width), lambda i: (jnp.minimum(first + i, TILES_PROMPT - 1), 0)),
            pl.BlockSpec((TM, width), lambda i: (jnp.maximum(first + i - TILES_PROMPT, 0), 0))]


def _x_specs(split, first):
    if not split:
        return [pl.BlockSpec((TM, D_MODEL), lambda i: (first + i, 0))]
    return _split_specs(D_MODEL, first)


def _read_tile(refs, first):
    if len(refs) == 1:
        return refs[0][...]
    return jnp.where(first + pl.program_id(0) < TILES_PROMPT, refs[0][...], refs[1][...])


IN_COLS = 2592
IN_QC, IN_CKV, IN_D = 1536, 1920, 2080
TR_ROWS = 256


def _gelu_tanh(x):
    return 0.5 * x * (1.0 + jnp.tanh(math.sqrt(2.0 / math.pi) * (x + 0.044715 * (x * x * x))))


def _sgu(pd, g, w_ref, bt):
    u = _gelu_tanh(pd[:, 0:WIDTH])
    v = _gelu_tanh(pd[:, WIDTH:2 * WIDTH])
    grp = lax.broadcasted_iota(jnp.int32, (1, WIDTH), 1) // SGU_GROUP_DIM
    v2 = v * v
    ms = jnp.zeros_like(v)
    for gi in range(SGU_GROUPS):
        sel = grp == gi
        tot = jnp.sum(jnp.where(sel, v2, 0.0), axis=-1, keepdims=True)
        ms = jnp.where(sel, tot * (1.0 / SGU_GROUP_DIM), ms)
    vg = _bf(v * lax.rsqrt(ms + EPS) * g)
    outs = []
    for c in range(pd.shape[0] // SGU_CHUNK):
        rows = slice(SGU_CHUNK * c, SGU_CHUNK * (c + 1))
        mixed = jnp.zeros((SGU_CHUNK, WIDTH), jnp.float32)
        for gi in range(SGU_GROUPS):
            full = _dot(_bf(w_ref[0, gi]), vg[rows]) + bt[:, gi:gi + 1]
            mixed = jnp.where(grp == gi, full, mixed)
        outs.append(u[rows] * mixed)
    return jnp.concatenate(outs, axis=0)


def _w_in_row_pieces():
    qn = [(IN_QC + 96 * h, MLA_NOPE) for h in range(MLA_HEADS)]
    qp = [(IN_QC + 96 * h + MLA_NOPE, MLA_ROPE) for h in range(MLA_HEADS)]
    seg_c = qn + qp + [(IN_CKV, MLA_KV_RANK + MLA_ROPE)]
    return (0, SEG_A + SEG_B), seg_c, (IN_D, SEG_D)


def _load_w_in(wt_ref, w_scr):
    ab, seg_c, d = _w_in_row_pieces()
    c_rows = jnp.concatenate([wt_ref[0, s:s + n, :] for s, n in seg_c]
                             + [jnp.zeros((SEG_C_PAD, D_MODEL), jnp.float32)], axis=0)
    for t in range(SEG_C // LANES):
        w_scr[:, SEG_A + SEG_B + LANES * t:SEG_A + SEG_B + LANES * (t + 1)] = _bf(c_rows[LANES * t:LANES * (t + 1)].T)
    for (src, n), dst in ((ab, 0), (d, SEG_A + SEG_B + SEG_C)):
        for t in range(n // TR_ROWS):
            rows = wt_ref[0, src + TR_ROWS * t:src + TR_ROWS * (t + 1), :]
            w_scr[:, dst + TR_ROWS * t:dst + TR_ROWS * (t + 1)] = _bf(rows.T)


def _inproj_kernel(n_x, *refs):
    x_refs = refs[:n_x]
    (g_ref, sh_ref, sc_ref, wt_ref, sg_ref, sw_ref, sbt_ref, pa_ref, pb_ref, pc_ref, od_ref, w_scr) = refs[n_x:]

    @pl.when(pl.program_id(0) == 0)
    def _():
        _load_w_in(wt_ref, w_scr)

    h = _rms(_read_tile(x_refs, 0), g_ref[0]) * (1.0 + sc_ref[...]) + sh_ref[...]
    hb = _bf(h)
    off = SEG_A + SEG_B + SEG_C
    od_ref[...] = _sgu(_dot(hb, w_scr[:, off:off + SEG_D]), sg_ref[0], sw_ref, sbt_ref[0])
    off = 0
    for ref in (pa_ref, pb_ref, pc_ref):
        n = ref.shape[1]
        ref[...] = _dot(hb, w_scr[:, off:off + n])
        off += n


def _inproj(l, xs, g_mix, mod, w_in_t, sgu_g, sgu_w, sgu_bt):
    def mod_spec(j):
        return pl.BlockSpec((None, None, 1, D_MODEL), lambda i: (l, _row_group(i), 0, j))

    widths = (SEG_A, SEG_B, SEG_C, WIDTH)
    return pl.pallas_call(
        functools.partial(_inproj_kernel, len(xs)),
        grid=(N_TOK // TM,),
        in_specs=_x_specs(len(xs) == 2, 0) + [
            pl.BlockSpec((1, 1, D_MODEL), lambda i: (l, 0, 0)),
            mod_spec(0), mod_spec(1),
            pl.BlockSpec((1, IN_COLS, D_MODEL), lambda i: (l, 0, 0), pipeline_mode=pl.Buffered(1)),
            pl.BlockSpec((1, 1, WIDTH), lambda i: (l, 0, 0)),
            pl.BlockSpec((1, SGU_GROUPS, SGU_CHUNK, SGU_CHUNK), lambda i: (l, 0, 0, 0)),
            pl.BlockSpec((1, SGU_CHUNK, SGU_GROUPS), lambda i: (l, 0, 0)),
        ],
        out_specs=[pl.BlockSpec((TM, n), lambda i: (i, 0)) for n in widths],
        out_shape=[jax.ShapeDtypeStruct((N_TOK, n), jnp.float32) for n in widths],
        scratch_shapes=[pltpu.VMEM((D_MODEL, IN_COLS_P), jnp.bfloat16)],
        compiler_params=_params("arbitrary"),
        name="inproj",
    )(*xs, g_mix, mod, mod, w_in_t, sgu_g, sgu_w, sgu_bt)


C_QN, C_QP, C_CKV, C_KPE = 0, 256, 384, 512


def _stack_heads(qp):
    lo = _lane_range(0, 64)
    return jnp.concatenate([_bf(jnp.where(lo, qp, 0.0)), _bf(jnp.where(lo, 0.0, qp))], axis=0)


def _unstack_heads(o, n):
    return jnp.where(_lane_range(0, 64), o[0:n], o[n:2 * n])


def _pair_t(c_ref):
    return jnp.concatenate([c_ref[0], c_ref[1]], axis=0)


def _stack_components(qp):
    return jnp.concatenate([_bf(jnp.where(_lane_range(32 * t, 32 * (t + 1)), qp, 0.0)) for t in range(4)], axis=0)


def _group_mean_sq(x, groups, size):
    sq = x * x
    hi = _bf(sq)
    rest = sq - hi.astype(jnp.float32)
    mid = _bf(rest)
    lo = _bf(rest - mid.astype(jnp.float32))
    return (_dot(hi, groups) + _dot(mid, groups) + _dot(lo, groups)) * (1.0 / size)


def _diff_finish(o, n, lam, post, g2, by_head):
    den = o[:, LANES:2 * LANES]
    outs = []
    for t in range(2):
        p1 = o[2 * t * n:(2 * t + 1) * n, 0:LANES] * (1.0 / den[2 * t * n:(2 * t + 1) * n])
        p2 = o[(2 * t + 1) * n:(2 * t + 2) * n, 0:LANES] * (lam / den[(2 * t + 1) * n:(2 * t + 2) * n])
        outs.append(p1 - p2)
    d = jnp.where(_lane_range(0, 64), outs[0], outs[1])
    return d * lax.rsqrt(_group_mean_sq(d, by_head, DIFF_V_DIM) + EPS) * g2 * post


def _mla_groups(j):
    return _group_matrix(2 * LANES, 2,
                         lambda t: (LANES + MLA_ROPE * (2 * j + t), LANES + MLA_ROPE * (2 * j + t + 1)))


def _mla_queries(qn_pair, qp_all, j):
    halves = []
    for t in range(2):
        h = 2 * j + t
        halves.append(jnp.concatenate([
            _bf(jnp.where(_lane_range(64 * t, 64 * (t + 1)), qn_pair, 0.0)),
            _bf(jnp.where(_lane_range(MLA_ROPE * h, MLA_ROPE * (h + 1)), qp_all, 0.0))], axis=1))
    return jnp.concatenate(halves, axis=0)


def _write_heads_t(p_ref, rows, col0, out_ref, bb):
    xt = p_ref[rows, col0:col0 + WIDTH].T
    for h in range(N_HEADS):
        out_ref[bb, 0, h] = xt[64 * h:64 * (h + 1)]
    _clear_other_layers(out_ref, bb)


def _clear_other_layers(out_ref, bb):
    if out_ref.shape[1] > 1:
        out_ref[bb, 1:] = jnp.zeros(out_ref.shape[1:], jnp.float32)[1:]


def _mix_prompt_kernel(n_prev, *refs):
    ins, outs = refs[:8], refs[8 + n_prev:]
    log = _ShiftLog()
    _mix_prompt_pass(ins, outs, log)

    @pl.when(log.unsafe())
    def _():
        _mix_prompt_pass(ins, outs, None)


def _mix_prompt_pass(ins, outs, log):
    pa_ref, pb_ref, pc_ref, cst_ref, gsub_ref, gckv_ref, wuk_ref, wuv_ref = ins
    o_ref, nak_ref, nav_ref, dk_ref, dv_ref, ckv_ref, kpe_ref = outs
    first_pass = log is not None
    c_a = HEAD_DIM ** -0.5 * LOG2E
    c_b = DIFF_QK_DIM ** -0.5 * LOG2E
    c_c = (MLA_NOPE + MLA_ROPE) ** -0.5 * LOG2E
    lam = cst_ref[0, 0:1, 0:1]
    post = cst_ref[0, 1:2, 0:1]
    wuk, wuv = _bf(wuk_ref[0]), _bf(wuv_ref[0])
    by_head, by_comp = _group_matrix(LANES, 2), _group_matrix(LANES, 4)

    def sequence(bb, log):
        rows = pl.ds(_aligned(bb * SEQ, SEQ), SEQ)
        for j in range(N_PAIRS):
            cols = slice(LANES * j, LANES * (j + 1))
            k = _bf(pa_ref[rows, WIDTH + LANES * j:WIDTH + LANES * (j + 1)])
            v = _with_ones(_bf(pa_ref[rows, 2 * WIDTH + LANES * j:2 * WIDTH + LANES * (j + 1)]))
            q = pa_ref[rows, cols] * c_a
            o = _attend([_dot_nt(_stack_heads(q), k)], [v], log, _bound(log, [q], _key_bound([k], by_head), 2))
            o_ref[rows, cols] = _unstack_heads(_normalised(o), SEQ)
        if first_pass:
            _write_heads_t(pa_ref, rows, WIDTH, nak_ref, bb)
            _write_heads_t(pa_ref, rows, 2 * WIDTH, nav_ref, bb)
        for j in range(N_PAIRS):
            cols = slice(LANES * j, LANES * (j + 1))
            k = _bf(pb_ref[rows, WIDTH + LANES * j:WIDTH + LANES * (j + 1)])
            v = _with_ones(_bf(pb_ref[rows, 2 * WIDTH + LANES * j:2 * WIDTH + LANES * (j + 1)]))
            q = pb_ref[rows, cols] * c_b
            o = _attend([_dot_nt(_stack_components(q), k)], [v], log, _bound(log, [q], _key_bound([k], by_comp), 4))
            o_ref[rows, WIDTH + LANES * j:WIDTH + LANES * (j + 1)] = _diff_finish(o, SEQ, lam, post, gsub_ref[0], by_head)
        if first_pass:
            _write_heads_t(pb_ref, rows, WIDTH, dk_ref, bb)
            _write_heads_t(pb_ref, rows, 2 * WIDTH, dv_ref, bb)
        ckv = _rms(pc_ref[rows, C_CKV:C_CKV + MLA_KV_RANK], gckv_ref[0])
        kpe_slot = pc_ref[rows, C_KPE:C_KPE + LANES]
        if first_pass:
            ckv_ref[bb, 0] = ckv
            _clear_other_layers(ckv_ref, bb)
            kpe_ref[bb, 0] = kpe_slot.T[0:MLA_ROPE]
            _clear_other_layers(kpe_ref, bb)
        ckv_b = _bf(ckv)
        kn = _bf(_dot(ckv_b, wuk))
        vv = _bf(_dot(ckv_b, wuv))
        kpe4 = _bf(_tile4(kpe_slot))
        qn = pc_ref[rows, C_QN:C_QN + WIDTH] * c_c
        qp = pc_ref[rows, C_QP:C_QP + LANES] * c_c
        for j in range(N_PAIRS):
            cols = slice(LANES * j, LANES * (j + 1))
            k = jnp.concatenate([kn[:, cols], kpe4], axis=1)
            qs = _mla_queries(qn[:, cols], qp, j)
            groups = _mla_groups(j)
            o = _attend([_dot_nt(qs, k)], [_with_ones(vv[:, cols])], log,
                        _bound(log, [qn[:, cols], qp], _key_bound([k], groups), 2))
            o_ref[rows, 2 * WIDTH + LANES * j:2 * WIDTH + LANES * (j + 1)] = _unstack_heads(_normalised(o), SEQ)

    _loop(PB, log, sequence)


def _mix_prompt(l, pa, pb, pc, cst, g_sub2, g_ckv, w_uk, w_uv, prev):
    n_prev = len(prev)
    tails = [(NA_HEADS, HEAD_DIM, SEQ)] * 2 + [(DIFF_HEADS, 64, SEQ)] * 2 + [(SEQ, MLA_KV_RANK), (MLA_ROPE, SEQ)]

    def cache_spec(tail):
        if l == 0:
            return pl.BlockSpec((PB, DEPTH) + tail, lambda b: (b, 0) + (0,) * len(tail))
        return pl.BlockSpec((PB, 1) + tail, lambda b: (b, l) + (0,) * len(tail))

    def rows(width):
        return pl.BlockSpec((PB * SEQ, width), lambda b: (b, 0))

    def layer(*tail):
        return pl.BlockSpec((1,) + tail, lambda b: (l,) + (0,) * len(tail))

    return pl.pallas_call(
        functools.partial(_mix_prompt_kernel, n_prev),
        grid=(BATCH // PB,),
        in_specs=[rows(SEG_A), rows(SEG_B), rows(SEG_C), layer(2, LANES), layer(1, LANES), layer(1, MLA_KV_RANK),
                  layer(MLA_KV_RANK, WIDTH), layer(MLA_KV_RANK, WIDTH)] + [pl.BlockSpec(memory_space=pl.ANY)] * n_prev,
        out_specs=[rows(O_ATT)] + [cache_spec(t) for t in tails],
        out_shape=[jax.ShapeDtypeStruct((N_PROMPT, O_ATT), jnp.float32)]
        + [jax.ShapeDtypeStruct((BATCH, DEPTH) + t, jnp.float32) for t in tails],
        input_output_aliases={8 + i: 1 + i for i in range(n_prev)},
        compiler_params=_params("parallel"),
        name="mix_prompt",
    )(pa, pb, pc, cst, g_sub2, g_ckv, w_uk, w_uv, *prev)


def _na_row_groups():
    kh = min(NA_WIN_ROWS, GRID_ROWS)
    r0s = [min(max(r - kh // 2, 0), GRID_ROWS - kh) for r in range(GRID_ROWS)]
    groups = []
    for r, r0 in enumerate(r0s):
        if groups and groups[-1][2] == r0:
            groups[-1][1] = r
        else:
            groups.append([r, r, r0])
    return kh, [tuple(g) for g in groups]


def _na_sample(q_ref, k_ref, v_ref, ck_ref, cv_ref, tt_ref, o_ref, log):
    c = HEAD_DIM ** -0.5 * LOG2E
    kh, groups = _na_row_groups()
    lk = kh * GRID_W
    edge = [g for g in groups if g[1] > g[0]]
    inner = [g for g in groups if g[1] == g[0]]
    depth = inner[0][0] - inner[0][2]
    assert all(g[0] - g[2] == depth for g in inner) and [g[0] for g in inner] == list(range(inner[0][0], inner[-1][0] + 1))

    def in_window(n):
        cq = lax.broadcasted_iota(jnp.int32, (n, lk), 0) & (GRID_W - 1)
        ck = lax.broadcasted_iota(jnp.int32, (n, lk), 1) & (GRID_W - 1)
        c0 = jnp.clip(cq - NA_WIN_COLS // 2, 0, GRID_W - NA_WIN_COLS)
        return (ck >= c0) & (ck < c0 + NA_WIN_COLS)

    kc_t = _bf(_pair_t(ck_ref))
    vc = _with_ones(_bf(_pair_t(cv_ref).T))
    by_head = _group_matrix(LANES, 2)
    k2max = _key_bound([_bf(k_ref[...]), _bf(_pair_t(ck_ref).T)], by_head)
    tmax = functools.reduce(jnp.maximum, [tt_ref[t, a] for t in range(2) for a in range(N_DROW - 1)])
    bplus = jnp.maximum(jnp.max(jnp.max(tmax, axis=-1, keepdims=True), axis=0, keepdims=True), 0.0)

    def group(row0, key0, offsets, log):
        n = len(offsets) * GRID_W
        rows, keys = pl.ds(row0, n), pl.ds(key0, lk)
        q = q_ref[rows, :] * c
        qg = _stack_heads(q)
        k = _bf(k_ref[keys, :])
        v = _with_ones(_bf(v_ref[keys, :]))
        bias = jnp.concatenate([
            jnp.concatenate([tt_ref[t, 2 * i - off + NA_WIN_ROWS - 1] for i in range(kh // 2)], axis=1)
            for t in range(2) for off in offsets], axis=0)
        s_loc = jnp.where(in_window(2 * n), _dot_nt(qg, k) + bias, NEG_INF)
        o = _attend([_dot(qg, kc_t), s_loc], [vc, v], log, _bound(log, [q], k2max, 2, bplus))
        o_ref[rows, :] = _unstack_heads(_normalised(o), n)

    if log is None:
        def any_row(r, log):
            r0 = jnp.clip(r - kh // 2, 0, GRID_ROWS - kh)
            group(_aligned(r * GRID_W, GRID_W), _aligned(r0 * GRID_W, GRID_W), [r - r0], log)

        _loop(GRID_ROWS, log, any_row)
        return

    for (r_lo, r_hi, r0) in edge:
        group(r_lo * GRID_W, r0 * GRID_W, [r - r0 for r in range(r_lo, r_hi + 1)], log)

    def inner_row(i, log):
        r = inner[0][0] + i
        group(r * GRID_W, (r - depth) * GRID_W, [depth], log)

    _loop(len(inner), log, inner_row)


def _diff_sample(q_ref, k_ref, v_ref, ck_ref, cv_ref, cos_ref, sin_ref, cst_ref, g_ref, o_ref, log):
    c = DIFF_QK_DIM ** -0.5 * LOG2E
    lam = cst_ref[0, 0:1, 0:1]
    post = cst_ref[0, 1:2, 0:1]
    k_new = _bf(_rope(k_ref[...], cos_ref[...], sin_ref[...]))
    kc_t = _bf(_pair_t(ck_ref))
    vc = _with_ones(_bf(_pair_t(cv_ref).T))
    v = _with_ones(_bf(v_ref[...]))
    by_head, by_comp = _group_matrix(LANES, 2), _group_matrix(LANES, 4)
    k2max = _key_bound([k_new, _bf(_pair_t(ck_ref).T)], by_comp)

    qb = QB_EXACT if log is None else QB

    def block(qi, log):
        rows = pl.ds(_aligned(qi * qb, qb), qb)
        q = _rope(q_ref[rows, :], cos_ref[rows, :], sin_ref[rows, :]) * c
        qs = _stack_components(q)
        o = _attend([_dot(qs, kc_t), _dot_nt(qs, k_new)], [vc, v], log, _bound(log, [q], k2max, 4))
        o_ref[rows, :] = _diff_finish(o, qb, lam, post, g_ref[0], by_head)

    _loop(DEC_SEQ // qb, log, block)


def _mla_sample(j, qn_ref, qp_ref, ckv_ref, kpe_ref, cckv_ref, ckpe_ref, cosq_ref, sinq_ref, cosk_ref, sink_ref,
                gckv_ref, wuk_ref, wuv_ref, o_ref, log):
    c = (MLA_NOPE + MLA_ROPE) ** -0.5 * LOG2E
    wuk, wuv = _bf(wuk_ref[0]), _bf(wuv_ref[0])
    ckv_new = _bf(_rms(ckv_ref[...], gckv_ref[0]))
    ckv_old = _bf(cckv_ref[...])
    kpe_new = _bf(_tile4(_rope(kpe_ref[...], cosk_ref[...], sink_ref[...])))
    kpe_old = _bf(jnp.concatenate([ckpe_ref[...]] * MLA_HEADS, axis=0).T)
    k_old = jnp.concatenate([_bf(_dot(ckv_old, wuk)), kpe_old], axis=1)
    k_new = jnp.concatenate([_bf(_dot(ckv_new, wuk)), kpe_new], axis=1)
    vo, vn = _with_ones(_bf(_dot(ckv_old, wuv))), _with_ones(_bf(_dot(ckv_new, wuv)))
    groups = _mla_groups(j)
    k2max = _key_bound([k_old, k_new], groups)

    qb = QB_EXACT if log is None else QB

    def block(qi, log):
        rows = pl.ds(_aligned(qi * qb, qb), qb)
        qp = _rope(qp_ref[rows, :], cosq_ref[rows, :], sinq_ref[rows, :]) * c
        qn = qn_ref[rows, :] * c
        qs = _mla_queries(qn, qp, j)
        o = _attend([_dot_nt(qs, k_old), _dot_nt(qs, k_new)], [vo, vn], log, _bound(log, [qn, qp], k2max, 2))
        o_ref[rows, :] = _unstack_heads(_normalised(o), qb)

    _loop(DEC_SEQ // qb, log, block)


def _mix_sample_kernel(qa_ref, ka_ref, va_ref, qb_ref, kb_ref, vb_ref, qn_ref, qp_ref, ckv_ref, kpe_ref,
                       cnak_ref, cnav_ref, cdk_ref, cdv_ref, cckv_ref, ckpe_ref, tt_ref,
                       cosb_ref, sinb_ref, cosq_ref, sinq_ref, cosk_ref, sink_ref,
                       cst_ref, gsub_ref, gckv_ref, wuk_ref, wuv_ref, oa_ref, ob_ref, oc_ref):
    j = pl.program_id(1)

    def run(log):
        _na_sample(qa_ref, ka_ref, va_ref, cnak_ref, cnav_ref, tt_ref, oa_ref, log)
        _diff_sample(qb_ref, kb_ref, vb_ref, cdk_ref, cdv_ref, cosb_ref, sinb_ref, cst_ref, gsub_ref, ob_ref, log)
        _mla_sample(j, qn_ref, qp_ref, ckv_ref, kpe_ref, cckv_ref, ckpe_ref, cosq_ref, sinq_ref, cosk_ref, sink_ref,
                    gckv_ref, wuk_ref, wuv_ref, oc_ref, log)

    log = _ShiftLog()
    run(log)

    @pl.when(log.unsafe())
    def _():
        run(None)


def _mix_sample(l, pa, pb, pc, caches_t, tt, tables, cst, g_sub2, g_ckv, w_uk, w_uv):
    first = N_PROMPT // DEC_SEQ

    def cols(block):
        return pl.BlockSpec((DEC_SEQ, LANES), lambda b, j: (first + b, block(j)))

    def cache(*tail, pair=False):
        return pl.BlockSpec((None, None) + tail, lambda b, j: (b, l, j if pair else 0) + (0,) * (len(tail) - 1))

    def layer(*tail):
        return pl.BlockSpec((1,) + tail, lambda b, j: (l,) + (0,) * len(tail))

    table = pl.BlockSpec((DEC_SEQ, LANES), lambda b, j: (0, 0), pipeline_mode=pl.Buffered(1))
    qkv = [cols(lambda j: j), cols(lambda j: N_PAIRS + j), cols(lambda j: 2 * N_PAIRS + j)]
    seg_c = [cols(lambda j: j), cols(lambda j: C_QP // LANES), cols(lambda j: C_CKV // LANES), cols(lambda j: C_KPE // LANES)]
    kv_t = cache(2, 64, PAST_LEN, pair=True)
    w_pair = pl.BlockSpec((1, MLA_KV_RANK, LANES), lambda b, j: (l, 0, j))
    out = pl.BlockSpec((DEC_SEQ, LANES), lambda b, j: (b, j))
    return pl.pallas_call(
        _mix_sample_kernel,
        grid=(DEC_BATCH, N_PAIRS),
        in_specs=qkv + qkv + seg_c + [kv_t, kv_t, kv_t, kv_t, cache(PAST_LEN, MLA_KV_RANK), cache(MLA_ROPE, PAST_LEN),
                                      pl.BlockSpec((None, 2, N_DROW - 1, GRID_W, LANES), lambda b, j: (l, j, 0, 0, 0)),
                                      table, table, table, table, table, table,
                                      layer(2, LANES), layer(1, LANES), layer(1, MLA_KV_RANK), w_pair, w_pair],
        out_specs=[out, out, out],
        out_shape=[jax.ShapeDtypeStruct((N_SAMPLE, WIDTH), jnp.float32)] * 3,
        compiler_params=_params("parallel", "arbitrary"),
        name="mix_sample",
    )(pa, pa, pa, pb, pb, pb, pc, pc, pc, pc, *caches_t, tt, *tables, cst, g_sub2, g_ckv, w_uk, w_uv)


FF_CHUNK = 2048


def _outffn_kernel(n_x, first, final, *refs):
    x_refs, op_ref, os_refs = refs[:n_x], refs[n_x], refs[n_x + 1:n_x + 4]
    (od_ref, wout_ref, g1_ref, gffn_ref, sh2_ref, sc2_ref, g2_ref, w1_ref, w2_ref, gfin_ref, y_ref) = refs[n_x + 4:]
    o_att = jnp.where(first + pl.program_id(0) < TILES_PROMPT, op_ref[...],
                      jnp.concatenate([r[...] for r in os_refs], axis=1))
    acc = (_dot(_bf(o_att), wout_ref[0, 0:O_ATT, :])
           + _dot(_bf(od_ref[...]), wout_ref[0, O_ATT:O_ATT + WIDTH, :]))
    x1 = _read_tile(x_refs, first) + g1_ref[...] * acc
    hf = _bf(_rms(x1, gffn_ref[0]) * (1.0 + sc2_ref[...]) + sh2_ref[...])
    acc = jnp.zeros((TM, D_MODEL), jnp.float32)
    for c in range(D_FF // FF_CHUNK):
        cols = slice(FF_CHUNK * c, FF_CHUNK * (c + 1))
        a = jnp.square(jnp.maximum(_dot(hf, w1_ref[0, :, cols]), 0.0))
        acc += _dot(_bf(a), w2_ref[0, cols, :])
    y = x1 + g2_ref[...] * acc
    if final:
        y = _rms(y, gfin_ref[...])
    y_ref[...] = y


def _outffn(l, xs, o_p, o_s, od, w_out, g_ffn, mod, w1, w2, g_final, first, n_tiles):
    def mod_spec(j):
        return pl.BlockSpec((None, None, 1, D_MODEL), lambda i: (l, _row_group(first + i), 0, j))

    def resident(shape):
        return pl.BlockSpec(shape, lambda i: (l,) + (0,) * (len(shape) - 1), pipeline_mode=pl.Buffered(1))

    return pl.pallas_call(
        functools.partial(_outffn_kernel, len(xs), first, l == DEPTH - 1),
        grid=(n_tiles,),
        in_specs=_x_specs(len(xs) == 2, first) + _split_specs(O_ATT, first)[:1] + _split_specs(WIDTH, first)[1:] * 3 + [
            pl.BlockSpec((TM, WIDTH), lambda i: (first + i, 0)),
            resident((1, 4 * WIDTH, D_MODEL)),
            mod_spec(2),
            pl.BlockSpec((1, 1, D_MODEL), lambda i: (l, 0, 0)),
            mod_spec(3), mod_spec(4), mod_spec(5),
            resident((1, D_MODEL, D_FF)),
            resident((1, D_FF, D_MODEL)),
            pl.BlockSpec((1, D_MODEL), lambda i: (0, 0)),
        ],
        out_specs=pl.BlockSpec((TM, D_MODEL), lambda i: (i, 0)),
        out_shape=jax.ShapeDtypeStruct((n_tiles * TM, D_MODEL), jnp.float32),
        compiler_params=_params("parallel"),
        name="outffn",
    )(*xs, o_p, *o_s, od, w_out, mod, g_ffn, mod, mod, mod, w1, w2, g_final)


def _rope32_tables():
    t = np.arange(DEC_SEQ)
    rows, cols = (t // GRID_W).astype(np.float64), (t % GRID_W).astype(np.float64)
    half = 8
    freqs = ROPE_BASE ** (-np.arange(half, dtype=np.float64) / half)
    cos, sin = [], []
    for pos in (rows, cols):
        ang = pos[:, None] * freqs[None, :]
        cos += [np.cos(ang), np.cos(ang)]
        sin += [-np.sin(ang), np.sin(ang)]
    return np.concatenate(cos, axis=1).astype(np.float32), np.concatenate(sin, axis=1).astype(np.float32)


def _rope_tables():
    c32, s32 = _rope32_tables()
    tile = lambda a, n: np.tile(a, (1, n))
    pad = np.zeros((DEC_SEQ, LANES - MLA_ROPE), np.float32)
    cos_k = np.concatenate([c32, pad + 1.0], axis=1)
    sin_k = np.concatenate([s32, pad], axis=1)
    return (tile(c32, 4), tile(s32, 4),
            cos_k, sin_k)


def kernel(x_prompt, x_sample, cache_na_k, cache_na_v, cache_diff_k, cache_diff_v, cache_mla_ckv, cache_mla_kpe, c, c_ctx, w_ada, b_ada, g_mix, g_ffn, w_in, w_out, na_rpb, diff_lq1, diff_lk1, diff_lq2, diff_lk2, diff_g_subln, mla_g_ckv, mla_w_uk, mla_w_uv, sgu_g, sgu_w, sgu_b, w_ff1, w_ff2, g_final):
    f32 = jnp.float32
    m = jnp.concatenate([c_ctx[None, :], c, jnp.zeros((N_MOD_ROWS - 1 - DEC_BATCH, D_MODEL), f32)], axis=0)
    mod = _ada(m, w_ada, b_ada).reshape(DEPTH, N_MOD_ROWS, 1, 6 * D_MODEL)
    cst = _lam_consts(diff_lq1, diff_lk1, diff_lq2, diff_lk2)
    tt = _bias_tiles(na_rpb)
    cos4, sin4, cos_k, sin_k = [jnp.asarray(t) for t in _rope_tables()]
    tables = (cos4, sin4, cos4, sin4, cos_k, sin_k)

    t_last = lambda a: jnp.swapaxes(a, -1, -2)
    w_in_t = t_last(w_in)
    caches_t = (t_last(cache_na_k), t_last(cache_na_v), t_last(cache_diff_k), t_last(cache_diff_v),
                cache_mla_ckv, t_last(cache_mla_kpe))
    w_out_b, w1_b, w2_b = _bf(w_out), _bf(w_ff1), _bf(w_ff2)
    g_mix3 = g_mix.reshape(DEPTH, 1, D_MODEL)
    g_ffn3 = g_ffn.reshape(DEPTH, 1, D_MODEL)
    g_sub2 = jnp.tile(diff_g_subln, (1, 2)).reshape(DEPTH, 1, LANES)
    g_ckv3 = mla_g_ckv.reshape(DEPTH, 1, MLA_KV_RANK)
    sgu_g3 = sgu_g.reshape(DEPTH, 1, WIDTH)
    sgu_bt = sgu_b.transpose(0, 2, 1)
    g_fin2 = g_final.reshape(1, D_MODEL)

    xs = (x_prompt.reshape(N_PROMPT, D_MODEL), x_sample.reshape(N_SAMPLE, D_MODEL))
    new = ()
    for l in range(DEPTH):
        pa, pb, pc, od = _inproj(l, xs, g_mix3, mod, w_in_t, sgu_g3, sgu_w, sgu_bt)
        o_p, *new = _mix_prompt(l, pa, pb, pc, cst, g_sub2, g_ckv3, mla_w_uk, mla_w_uv, new)
        o_s = _mix_sample(l, pa, pb, pc, caches_t, tt, tables, cst, g_sub2, g_ckv3, mla_w_uk, mla_w_uv)
        ffn = functools.partial(_outffn, l, xs, o_p, o_s, od, w_out_b, g_ffn3, mod, w1_b, w2_b, g_fin2)
        if l < DEPTH - 1:
            xs = (ffn(0, TILES_PROMPT + TILES_SAMPLE),)
        else:
            xs = (ffn(0, TILES_PROMPT), ffn(TILES_PROMPT, TILES_SAMPLE))
    y_prompt = xs[0].reshape(BATCH, SEQ, D_MODEL)
    y_sample = xs[1].reshape(DEC_BATCH, DEC_SEQ, D_MODEL)
    na_k, na_v, diff_k, diff_v, mla_ckv, mla_kpe = new
    return (y_prompt, y_sample, t_last(na_k), t_last(na_v), t_last(diff_k), t_last(diff_v), mla_ckv, t_last(mla_kpe))
```

```python
import functools
import math

import numpy as np
import jax
import jax.numpy as jnp
from jax import lax
from jax.experimental import pallas as pl
from jax.experimental.pallas import tpu as pltpu

D_MODEL = 1024
BATCH = 16
SEQ = 256
DEPTH = 4
DEC_BATCH = 2
DEC_SEQ = 1024
PAST_LEN = 512
GRID_W = 64
GRID_ROWS = DEC_SEQ // GRID_W
HEAD_DIM = 64
NA_HEADS = 4
NA_WIN_ROWS = 8
NA_WIN_COLS = 16
DIFF_HEADS = 4
DIFF_QK_DIM = 32
DIFF_V_DIM = 64
MLA_HEADS = 4
MLA_NOPE = 64
MLA_ROPE = 32
MLA_V = 64
MLA_KV_RANK = 128
SGU_GROUPS = 4
SGU_GROUP_DIM = 64
SGU_CHUNK = 128
D_FF = 4 * D_MODEL
ROPE_BASE = 10000.0
EPS = 1e-6
NEG_INF = -1e30
LOG2E = 1.4426950408889634

N_HEADS = 4
N_PAIRS = N_HEADS // 2
LANES = 128
WIDTH = 256
N_PROMPT = BATCH * SEQ
N_SAMPLE = DEC_BATCH * DEC_SEQ
N_TOK = N_PROMPT + N_SAMPLE
N_MOD_ROWS = 8

SEG_A = 3 * WIDTH
SEG_B = 3 * WIDTH
SEG_C = 640
SEG_D = 2 * WIDTH
SEG_C_PAD = 96
IN_COLS_P = SEG_A + SEG_B + SEG_C + SEG_D
O_ATT = 3 * WIDTH

TM = 512
TILES_PROMPT = N_PROMPT // TM
TILES_SAMPLE = N_SAMPLE // TM
PB = 2
QB = 512
QB_EXACT = 64
VMEM_LIMIT = 56 * 1024 * 1024


def _bf(x):
    return x.astype(jnp.bfloat16)


def _dot(a, b):
    return jnp.dot(a, b, preferred_element_type=jnp.float32)


def _dot_nt(a, b):
    return lax.dot_general(a, b, (((1,), (1,)), ((), ())), preferred_element_type=jnp.float32)


def _rms(x, g):
    ms = jnp.mean(x * x, axis=-1, keepdims=True)
    return x * lax.rsqrt(ms + EPS) * g


def _lane_range(lo, hi, width=LANES):
    lane = lax.broadcasted_iota(jnp.int32, (1, width), 1)
    return (lane >= lo) & (lane < hi)


def _with_ones(v):
    return jnp.concatenate([v, jnp.ones((v.shape[0], LANES), jnp.bfloat16)], axis=1)


def _attend(scores, values, log=None, bound=None):
    if bound is None:
        m = functools.reduce(jnp.maximum, [jnp.max(s, axis=-1, keepdims=True) for s in scores])
    else:
        m = bound
    o = functools.reduce(lambda a, b: a + b, [_dot(_bf(jnp.exp2(s - m)), v) for s, v in zip(scores, values)])
    if bound is not None:
        den = jnp.min(o[:, LANES:2 * LANES], axis=0, keepdims=True)
        log.least = jnp.minimum(log.least, jnp.min(den, axis=1, keepdims=True))
    return o


BOUND_SLACK = 1.02
MIN_DENOMINATOR = 2.0 ** -88


class _ShiftLog:
    def __init__(self):
        self.least = jnp.full((1, 1), jnp.inf, jnp.float32)

    def unsafe(self):
        return jnp.logical_not(self.least[0, 0] >= MIN_DENOMINATOR)


def _group_matrix(width, n_groups, extra=None):
    i = lax.broadcasted_iota(jnp.int32, (width, LANES), 0)
    j = lax.broadcasted_iota(jnp.int32, (width, LANES), 1)
    size = LANES // n_groups
    hit = (i // size == j // size) & (i < LANES)
    for t in range(n_groups if extra else 0):
        lo, hi = extra(t)
        hit = hit | ((i >= lo) & (i < hi) & (j // size == t))
    return jnp.where(hit, 1.0, 0.0).astype(jnp.bfloat16)


def _squares(x):
    xf = x.astype(jnp.float32)
    return _bf(xf * xf)


def _key_bound(keys, groups):
    return functools.reduce(jnp.maximum, [jnp.max(_dot(_squares(k), groups), axis=0, keepdims=True) for k in keys])


def _bound(log, q_parts, k2max, n_groups, extra=0.0):
    if log is None:
        return None
    size = LANES // n_groups
    q_norm = jnp.sqrt(functools.reduce(lambda a, b: a + b,
                                       [jnp.sum(q * q, axis=-1, keepdims=True) for q in q_parts])) * BOUND_SLACK
    return jnp.concatenate([q_norm * jnp.sqrt(k2max[:, size * t:size * t + 1]) + extra for t in range(n_groups)], axis=0)


def _loop(n, log, body):
    if log is not None:
        for i in range(n):
            body(i, log)
    else:
        lax.fori_loop(0, n, lambda i, carry: body(i, None) or carry, 0)


def _aligned(start, multiple):
    return start if isinstance(start, int) else pl.multiple_of(start, multiple)


def _normalised(o_ext):
    return o_ext[:, 0:LANES] * (1.0 / o_ext[:, LANES:2 * LANES])


def _swap8(x):
    lane = lax.broadcasted_iota(jnp.int32, (1, LANES), 1)
    return jnp.where((lane & 15) < 8, pltpu.roll(x, LANES - 8, 1), pltpu.roll(x, 8, 1))


def _rope(x, cos, sin):
    outs = []
    for c in range(x.shape[1] // LANES):
        sl = slice(LANES * c, LANES * (c + 1))
        xc = x[:, sl]
        outs.append(xc * cos[:, sl] + _swap8(xc) * sin[:, sl])
    return outs[0] if len(outs) == 1 else jnp.concatenate(outs, axis=1)


def _tile4(x):
    return x + pltpu.roll(x, 32, 1) + pltpu.roll(x, 64, 1) + pltpu.roll(x, 96, 1)


def _params(*sem):
    return pltpu.CompilerParams(dimension_semantics=sem, vmem_limit_bytes=VMEM_LIMIT)


ADA_TN = 1536


def _ada_kernel(m_ref, w_ref, b_ref, o_ref):
    m = m_ref[...]
    s = m * jax.nn.sigmoid(m)
    o_ref[0] = _dot(_bf(s), _bf(w_ref[0])) + b_ref[0]


def _ada(m, w_ada, b_ada):
    n = 6 * D_MODEL
    return pl.pallas_call(
        _ada_kernel,
        grid=(DEPTH, n // ADA_TN),
        in_specs=[
            pl.BlockSpec((N_MOD_ROWS, D_MODEL), lambda l, j: (0, 0)),
            pl.BlockSpec((1, D_MODEL, ADA_TN), lambda l, j: (l, 0, j)),
            pl.BlockSpec((1, 1, ADA_TN), lambda l, j: (l, 0, j)),
        ],
        out_specs=pl.BlockSpec((1, N_MOD_ROWS, ADA_TN), lambda l, j: (l, 0, j)),
        out_shape=jax.ShapeDtypeStruct((DEPTH, N_MOD_ROWS, n), jnp.float32),
        compiler_params=_params("parallel", "parallel"),
        name="ada",
    )(m, w_ada, b_ada.reshape(DEPTH, 1, n))


def _lam_kernel(lq1_ref, lk1_ref, lq2_ref, lk2_ref, init_ref, o_ref):
    init = init_ref[...]
    a = jnp.exp(jnp.sum(lq1_ref[...] * lk1_ref[...], axis=-1, keepdims=True))
    b = jnp.exp(jnp.sum(lq2_ref[...] * lk2_ref[...], axis=-1, keepdims=True))
    lam = a - b + init
    post = 1.0 - init
    for l in range(DEPTH):
        o_ref[l, 0:1, :] = jnp.broadcast_to(lam[l:l + 1], (1, LANES))
        o_ref[l, 1:2, :] = jnp.broadcast_to(post[l:l + 1], (1, LANES))


def _lam_consts(lq1, lk1, lq2, lk2):
    init = np.array([[0.8 - 0.6 * math.exp(-0.3 * l)] for l in range(DEPTH)], np.float32)
    return pl.pallas_call(
        _lam_kernel,
        out_shape=jax.ShapeDtypeStruct((DEPTH, 2, LANES), jnp.float32),
        name="diff_lambda",
    )(lq1, lk1, lq2, lk2, jnp.asarray(init))


N_DROW = 2 * NA_WIN_ROWS - 1
N_DCOL = 2 * NA_WIN_COLS - 1


def _bias_kernel(rpb_ref, o_ref):
    l = pl.program_id(0)
    h = pl.program_id(1)
    base = (l * NA_HEADS + h) * (N_DROW * N_DCOL)
    cq = lax.broadcasted_iota(jnp.int32, (GRID_W, LANES), 0)
    lane = lax.broadcasted_iota(jnp.int32, (GRID_W, LANES), 1)
    ck = lane & (GRID_W - 1)
    dcol = jnp.clip(ck - cq, -(NA_WIN_COLS - 1), NA_WIN_COLS - 1) + (NA_WIN_COLS - 1)
    hi = lax.broadcasted_iota(jnp.int32, (1, LANES), 1) >= GRID_W
    for a in range(N_DROW - 1):
        acc = jnp.zeros((GRID_W, LANES), jnp.float32)
        for j in range(N_DCOL):
            lo_v = rpb_ref[base + a * N_DCOL + j]
            hi_v = rpb_ref[base + (a + 1) * N_DCOL + j]
            acc = jnp.where(dcol == j, jnp.where(hi, hi_v, lo_v), acc)
        o_ref[0, 0, a] = acc * LOG2E


def _bias_tiles(na_rpb):
    return pl.pallas_call(
        _bias_kernel,
        grid=(DEPTH, NA_HEADS),
        in_specs=[pl.BlockSpec(memory_space=pltpu.SMEM)],
        out_specs=pl.BlockSpec((1, 1, N_DROW - 1, GRID_W, LANES), lambda l, h: (l, h, 0, 0, 0)),
        out_shape=jax.ShapeDtypeStruct((DEPTH, NA_HEADS, N_DROW - 1, GRID_W, LANES), jnp.float32),
        compiler_params=_params("parallel", "parallel"),
        name="na_bias_tiles",
    )(na_rpb.reshape(-1))


def _row_group(i):
    return jnp.where(i < TILES_PROMPT, 0, 1 + (i - TILES_PROMPT) // (DEC_SEQ // TM))


def _split_specs(width, first):
    return [pl.BlockSpec((TM, width), lambda i: (jnp.minimum(first + i, TILES_PROMPT - 1), 0)),
            pl.BlockSpec((TM, width), lambda i: (jnp.maximum(first + i - TILES_PROMPT, 0), 0))]


def _x_specs(split, first):
    if not split:
        return [pl.BlockSpec((TM, D_MODEL), lambda i: (first + i, 0))]
    return _split_specs(D_MODEL, first)


def _read_tile(refs, first):
    if len(refs) == 1:
        return refs[0][...]
    return jnp.where(first + pl.program_id(0) < TILES_PROMPT, refs[0][...], refs[1][...])


IN_COLS = 2592
IN_QC, IN_CKV, IN_D = 1536, 1920, 2080
TR_ROWS = 256


def _gelu_tanh(x):
    return 0.5 * x * (1.0 + jnp.tanh(math.sqrt(2.0 / math.pi) * (x + 0.044715 * (x * x * x))))


def _sgu(pd, g, w_ref, bt):
    u = _gelu_tanh(pd[:, 0:WIDTH])
    v = _gelu_tanh(pd[:, WIDTH:2 * WIDTH])
    grp = lax.broadcasted_iota(jnp.int32, (1, WIDTH), 1) // SGU_GROUP_DIM
    v2 = v * v
    ms = jnp.zeros_like(v)
    for gi in range(SGU_GROUPS):
        sel = grp == gi
        tot = jnp.sum(jnp.where(sel, v2, 0.0), axis=-1, keepdims=True)
        ms = jnp.where(sel, tot * (1.0 / SGU_GROUP_DIM), ms)
    vg = _bf(v * lax.rsqrt(ms + EPS) * g)
    outs = []
    for c in range(pd.shape[0] // SGU_CHUNK):
        rows = slice(SGU_CHUNK * c, SGU_CHUNK * (c + 1))
        mixed = jnp.zeros((SGU_CHUNK, WIDTH), jnp.float32)
        for gi in range(SGU_GROUPS):
            full = _dot(_bf(w_ref[0, gi]), vg[rows]) + bt[:, gi:gi + 1]
            mixed = jnp.where(grp == gi, full, mixed)
        outs.append(u[rows] * mixed)
    return jnp.concatenate(outs, axis=0)


def _w_in_row_pieces():
    qn = [(IN_QC + 96 * h, MLA_NOPE) for h in range(MLA_HEADS)]
    qp = [(IN_QC + 96 * h + MLA_NOPE, MLA_ROPE) for h in range(MLA_HEADS)]
    seg_c = qn + qp + [(IN_CKV, MLA_KV_RANK + MLA_ROPE)]
    return (0, SEG_A + SEG_B), seg_c, (IN_D, SEG_D)


def _load_w_in(wt_ref, w_scr):
    ab, seg_c, d = _w_in_row_pieces()
    c_rows = jnp.concatenate([wt_ref[0, s:s + n, :] for s, n in seg_c]
                             + [jnp.zeros((SEG_C_PAD, D_MODEL), jnp.float32)], axis=0)
    for t in range(SEG_C // LANES):
        w_scr[:, SEG_A + SEG_B + LANES * t:SEG_A + SEG_B + LANES * (t + 1)] = _bf(c_rows[LANES * t:LANES * (t + 1)].T)
    for (src, n), dst in ((ab, 0), (d, SEG_A + SEG_B + SEG_C)):
        for t in range(n // TR_ROWS):
            rows = wt_ref[0, src + TR_ROWS * t:src + TR_ROWS * (t + 1), :]
            w_scr[:, dst + TR_ROWS * t:dst + TR_ROWS * (t + 1)] = _bf(rows.T)


def _inproj_kernel(n_x, *refs):
    x_refs = refs[:n_x]
    (g_ref, sh_ref, sc_ref, wt_ref, sg_ref, sw_ref, sbt_ref, pa_ref, pb_ref, pc_ref, od_ref, w_scr) = refs[n_x:]

    @pl.when(pl.program_id(0) == 0)
    def _():
        _load_w_in(wt_ref, w_scr)

    h = _rms(_read_tile(x_refs, 0), g_ref[0]) * (1.0 + sc_ref[...]) + sh_ref[...]
    hb = _bf(h)
    off = SEG_A + SEG_B + SEG_C
    od_ref[...] = _sgu(_dot(hb, w_scr[:, off:off + SEG_D]), sg_ref[0], sw_ref, sbt_ref[0])
    off = 0
    for ref in (pa_ref, pb_ref, pc_ref):
        n = ref.shape[1]
        ref[...] = _dot(hb, w_scr[:, off:off + n])
        off += n


def _inproj(l, xs, g_mix, mod, w_in_t, sgu_g, sgu_w, sgu_bt):
    def mod_spec(j):
        return pl.BlockSpec((None, None, 1, D_MODEL), lambda i: (l, _row_group(i), 0, j))

    widths = (SEG_A, SEG_B, SEG_C, WIDTH)
    return pl.pallas_call(
        functools.partial(_inproj_kernel, len(xs)),
        grid=(N_TOK // TM,),
        in_specs=_x_specs(len(xs) == 2, 0) + [
            pl.BlockSpec((1, 1, D_MODEL), lambda i: (l, 0, 0)),
            mod_spec(0), mod_spec(1),
            pl.BlockSpec((1, IN_COLS, D_MODEL), lambda i: (l, 0, 0), pipeline_mode=pl.Buffered(1)),
            pl.BlockSpec((1, 1, WIDTH), lambda i: (l, 0, 0)),
            pl.BlockSpec((1, SGU_GROUPS, SGU_CHUNK, SGU_CHUNK), lambda i: (l, 0, 0, 0)),
            pl.BlockSpec((1, SGU_CHUNK, SGU_GROUPS), lambda i: (l, 0, 0)),
        ],
        out_specs=[pl.BlockSpec((TM, n), lambda i: (i, 0)) for n in widths],
        out_shape=[jax.ShapeDtypeStruct((N_TOK, n), jnp.float32) for n in widths],
        scratch_shapes=[pltpu.VMEM((D_MODEL, IN_COLS_P), jnp.bfloat16)],
        compiler_params=_params("arbitrary"),
        name="inproj",
    )(*xs, g_mix, mod, mod, w_in_t, sgu_g, sgu_w, sgu_bt)


C_QN, C_QP, C_CKV, C_KPE = 0, 256, 384, 512


def _stack_heads(qp):
    lo = _lane_range(0, 64)
    return jnp.concatenate([_bf(jnp.where(lo, qp, 0.0)), _bf(jnp.where(lo, 0.0, qp))], axis=0)


def _unstack_heads(o, n):
    return jnp.where(_lane_range(0, 64), o[0:n], o[n:2 * n])


def _pair_t(c_ref):
    return jnp.concatenate([c_ref[0], c_ref[1]], axis=0)


def _stack_components(qp):
    return jnp.concatenate([_bf(jnp.where(_lane_range(32 * t, 32 * (t + 1)), qp, 0.0)) for t in range(4)], axis=0)


def _group_mean_sq(x, groups, size):
    sq = x * x
    hi = _bf(sq)
    rest = sq - hi.astype(jnp.float32)
    mid = _bf(rest)
    lo = _bf(rest - mid.astype(jnp.float32))
    return (_dot(hi, groups) + _dot(mid, groups) + _dot(lo, groups)) * (1.0 / size)


def _diff_finish(o, n, lam, post, g2, by_head):
    den = o[:, LANES:2 * LANES]
    outs = []
    for t in range(2):
        p1 = o[2 * t * n:(2 * t + 1) * n, 0:LANES] * (1.0 / den[2 * t * n:(2 * t + 1) * n])
        p2 = o[(2 * t + 1) * n:(2 * t + 2) * n, 0:LANES] * (lam / den[(2 * t + 1) * n:(2 * t + 2) * n])
        outs.append(p1 - p2)
    d = jnp.where(_lane_range(0, 64), outs[0], outs[1])
    return d * lax.rsqrt(_group_mean_sq(d, by_head, DIFF_V_DIM) + EPS) * g2 * post


def _mla_groups(j):
    return _group_matrix(2 * LANES, 2,
                         lambda t: (LANES + MLA_ROPE * (2 * j + t), LANES + MLA_ROPE * (2 * j + t + 1)))


def _mla_queries(qn_pair, qp_all, j):
    halves = []
    for t in range(2):
        h = 2 * j + t
        halves.append(jnp.concatenate([
            _bf(jnp.where(_lane_range(64 * t, 64 * (t + 1)), qn_pair, 0.0)),
            _bf(jnp.where(_lane_range(MLA_ROPE * h, MLA_ROPE * (h + 1)), qp_all, 0.0))], axis=1))
    return jnp.concatenate(halves, axis=0)


def _write_heads_t(p_ref, rows, col0, out_ref, bb):
    xt = p_ref[rows, col0:col0 + WIDTH].T
    for h in range(N_HEADS):
        out_ref[bb, 0, h] = xt[64 * h:64 * (h + 1)]
    _clear_other_layers(out_ref, bb)


def _clear_other_layers(out_ref, bb):
    if out_ref.shape[1] > 1:
        out_ref[bb, 1:] = jnp.zeros(out_ref.shape[1:], jnp.float32)[1:]


def _mix_prompt_kernel(n_prev, *refs):
    ins, outs = refs[:8], refs[8 + n_prev:]
    log = _ShiftLog()
    _mix_prompt_pass(ins, outs, log)

    @pl.when(log.unsafe())
    def _():
        _mix_prompt_pass(ins, outs, None)


def _mix_prompt_pass(ins, outs, log):
    pa_ref, pb_ref, pc_ref, cst_ref, gsub_ref, gckv_ref, wuk_ref, wuv_ref = ins
    o_ref, nak_ref, nav_ref, dk_ref, dv_ref, ckv_ref, kpe_ref = outs
    first_pass = log is not None
    c_a = HEAD_DIM ** -0.5 * LOG2E
    c_b = DIFF_QK_DIM ** -0.5 * LOG2E
    c_c = (MLA_NOPE + MLA_ROPE) ** -0.5 * LOG2E
    lam = cst_ref[0, 0:1, 0:1]
    post = cst_ref[0, 1:2, 0:1]
    wuk, wuv = _bf(wuk_ref[0]), _bf(wuv_ref[0])
    by_head, by_comp = _group_matrix(LANES, 2), _group_matrix(LANES, 4)

    def sequence(bb, log):
        rows = pl.ds(_aligned(bb * SEQ, SEQ), SEQ)
        for j in range(N_PAIRS):
            cols = slice(LANES * j, LANES * (j + 1))
            k = _bf(pa_ref[rows, WIDTH + LANES * j:WIDTH + LANES * (j + 1)])
            v = _with_ones(_bf(pa_ref[rows, 2 * WIDTH + LANES * j:2 * WIDTH + LANES * (j + 1)]))
            q = pa_ref[rows, cols] * c_a
            o = _attend([_dot_nt(_stack_heads(q), k)], [v], log, _bound(log, [q], _key_bound([k], by_head), 2))
            o_ref[rows, cols] = _unstack_heads(_normalised(o), SEQ)
        if first_pass:
            _write_heads_t(pa_ref, rows, WIDTH, nak_ref, bb)
            _write_heads_t(pa_ref, rows, 2 * WIDTH, nav_ref, bb)
        for j in range(N_PAIRS):
            cols = slice(LANES * j, LANES * (j + 1))
            k = _bf(pb_ref[rows, WIDTH + LANES * j:WIDTH + LANES * (j + 1)])
            v = _with_ones(_bf(pb_ref[rows, 2 * WIDTH + LANES * j:2 * WIDTH + LANES * (j + 1)]))
            q = pb_ref[rows, cols] * c_b
            o = _attend([_dot_nt(_stack_components(q), k)], [v], log, _bound(log, [q], _key_bound([k], by_comp), 4))
            o_ref[rows, WIDTH + LANES * j:WIDTH + LANES * (j + 1)] = _diff_finish(o, SEQ, lam, post, gsub_ref[0], by_head)
        if first_pass:
            _write_heads_t(pb_ref, rows, WIDTH, dk_ref, bb)
            _write_heads_t(pb_ref, rows, 2 * WIDTH, dv_ref, bb)
        ckv = _rms(pc_ref[rows, C_CKV:C_CKV + MLA_KV_RANK], gckv_ref[0])
        kpe_slot = pc_ref[rows, C_KPE:C_KPE + LANES]
        if first_pass:
            ckv_ref[bb, 0] = ckv
            _clear_other_layers(ckv_ref, bb)
            kpe_ref[bb, 0] = kpe_slot.T[0:MLA_ROPE]
            _clear_other_layers(kpe_ref, bb)
        ckv_b = _bf(ckv)
        kn = _bf(_dot(ckv_b, wuk))
        vv = _bf(_dot(ckv_b, wuv))
        kpe4 = _bf(_tile4(kpe_slot))
        qn = pc_ref[rows, C_QN:C_QN + WIDTH] * c_c
        qp = pc_ref[rows, C_QP:C_QP + LANES] * c_c
        for j in range(N_PAIRS):
            cols = slice(LANES * j, LANES * (j + 1))
            k = jnp.concatenate([kn[:, cols], kpe4], axis=1)
            qs = _mla_queries(qn[:, cols], qp, j)
            groups = _mla_groups(j)
            o = _attend([_dot_nt(qs, k)], [_with_ones(vv[:, cols])], log,
                        _bound(log, [qn[:, cols], qp], _key_bound([k], groups), 2))
            o_ref[rows, 2 * WIDTH + LANES * j:2 * WIDTH + LANES * (j + 1)] = _unstack_heads(_normalised(o), SEQ)

    _loop(PB, log, sequence)


def _mix_prompt(l, pa, pb, pc, cst, g_sub2, g_ckv, w_uk, w_uv, prev):
    n_prev = len(prev)
    tails = [(NA_HEADS, HEAD_DIM, SEQ)] * 2 + [(DIFF_HEADS, 64, SEQ)] * 2 + [(SEQ, MLA_KV_RANK), (MLA_ROPE, SEQ)]

    def cache_spec(tail):
        if l == 0:
            return pl.BlockSpec((PB, DEPTH) + tail, lambda b: (b, 0) + (0,) * len(tail))
        return pl.BlockSpec((PB, 1) + tail, lambda b: (b, l) + (0,) * len(tail))

    def rows(width):
        return pl.BlockSpec((PB * SEQ, width), lambda b: (b, 0))

    def layer(*tail):
        return pl.BlockSpec((1,) + tail, lambda b: (l,) + (0,) * len(tail))

    return pl.pallas_call(
        functools.partial(_mix_prompt_kernel, n_prev),
        grid=(BATCH // PB,),
        in_specs=[rows(SEG_A), rows(SEG_B), rows(SEG_C), layer(2, LANES), layer(1, LANES), layer(1, MLA_KV_RANK),
                  layer(MLA_KV_RANK, WIDTH), layer(MLA_KV_RANK, WIDTH)] + [pl.BlockSpec(memory_space=pl.ANY)] * n_prev,
        out_specs=[rows(O_ATT)] + [cache_spec(t) for t in tails],
        out_shape=[jax.ShapeDtypeStruct((N_PROMPT, O_ATT), jnp.float32)]
        + [jax.ShapeDtypeStruct((BATCH, DEPTH) + t, jnp.float32) for t in tails],
        input_output_aliases={8 + i: 1 + i for i in range(n_prev)},
        compiler_params=_params("parallel"),
        name="mix_prompt",
    )(pa, pb, pc, cst, g_sub2, g_ckv, w_uk, w_uv, *prev)


def _na_row_groups():
    kh = min(NA_WIN_ROWS, GRID_ROWS)
    r0s = [min(max(r - kh // 2, 0), GRID_ROWS - kh) for r in range(GRID_ROWS)]
    groups = []
    for r, r0 in enumerate(r0s):
        if groups and groups[-1][2] == r0:
            groups[-1][1] = r
        else:
            groups.append([r, r, r0])
    return kh, [tuple(g) for g in groups]


def _na_sample(q_ref, k_ref, v_ref, ck_ref, cv_ref, tt_ref, o_ref, log):
    c = HEAD_DIM ** -0.5 * LOG2E
    kh, groups = _na_row_groups()
    lk = kh * GRID_W
    edge = [g for g in groups if g[1] > g[0]]
    inner = [g for g in groups if g[1] == g[0]]
    depth = inner[0][0] - inner[0][2]
    assert all(g[0] - g[2] == depth for g in inner) and [g[0] for g in inner] == list(range(inner[0][0], inner[-1][0] + 1))

    def in_window(n):
        cq = lax.broadcasted_iota(jnp.int32, (n, lk), 0) & (GRID_W - 1)
        ck = lax.broadcasted_iota(jnp.int32, (n, lk), 1) & (GRID_W - 1)
        c0 = jnp.clip(cq - NA_WIN_COLS // 2, 0, GRID_W - NA_WIN_COLS)
        return (ck >= c0) & (ck < c0 + NA_WIN_COLS)

    kc_t = _bf(_pair_t(ck_ref))
    vc = _with_ones(_bf(_pair_t(cv_ref).T))
    by_head = _group_matrix(LANES, 2)
    k2max = _key_bound([_bf(k_ref[...]), _bf(_pair_t(ck_ref).T)], by_head)
    tmax = functools.reduce(jnp.maximum, [tt_ref[t, a] for t in range(2) for a in range(N_DROW - 1)])
    bplus = jnp.maximum(jnp.max(jnp.max(tmax, axis=-1, keepdims=True), axis=0, keepdims=True), 0.0)

    def group(row0, key0, offsets, log):
        n = len(offsets) * GRID_W
        rows, keys = pl.ds(row0, n), pl.ds(key0, lk)
        q = q_ref[rows, :] * c
        qg = _stack_heads(q)
        k = _bf(k_ref[keys, :])
        v = _with_ones(_bf(v_ref[keys, :]))
        bias = jnp.concatenate([
            jnp.concatenate([tt_ref[t, 2 * i - off + NA_WIN_ROWS - 1] for i in range(kh // 2)], axis=1)
            for t in range(2) for off in offsets], axis=0)
        s_loc = jnp.where(in_window(2 * n), _dot_nt(qg, k) + bias, NEG_INF)
        o = _attend([_dot(qg, kc_t), s_loc], [vc, v], log, _bound(log, [q], k2max, 2, bplus))
        o_ref[rows, :] = _unstack_heads(_normalised(o), n)

    if log is None:
        def any_row(r, log):
            r0 = jnp.clip(r - kh // 2, 0, GRID_ROWS - kh)
            group(_aligned(r * GRID_W, GRID_W), _aligned(r0 * GRID_W, GRID_W), [r - r0], log)

        _loop(GRID_ROWS, log, any_row)
        return

    for (r_lo, r_hi, r0) in edge:
        group(r_lo * GRID_W, r0 * GRID_W, [r - r0 for r in range(r_lo, r_hi + 1)], log)

    def inner_row(i, log):
        r = inner[0][0] + i
        group(r * GRID_W, (r - depth) * GRID_W, [depth], log)

    _loop(len(inner), log, inner_row)


def _diff_sample(q_ref, k_ref, v_ref, ck_ref, cv_ref, cos_ref, sin_ref, cst_ref, g_ref, o_ref, log):
    c = DIFF_QK_DIM ** -0.5 * LOG2E
    lam = cst_ref[0, 0:1, 0:1]
    post = cst_ref[0, 1:2, 0:1]
    k_new = _bf(_rope(k_ref[...], cos_ref[...], sin_ref[...]))
    kc_t = _bf(_pair_t(ck_ref))
    vc = _with_ones(_bf(_pair_t(cv_ref).T))
    v = _with_ones(_bf(v_ref[...]))
    by_head, by_comp = _group_matrix(LANES, 2), _group_matrix(LANES, 4)
    k2max = _key_bound([k_new, _bf(_pair_t(ck_ref).T)], by_comp)

    qb = QB_EXACT if log is None else QB

    def block(qi, log):
        rows = pl.ds(_aligned(qi * qb, qb), qb)
        q = _rope(q_ref[rows, :], cos_ref[rows, :], sin_ref[rows, :]) * c
        qs = _stack_components(q)
        o = _attend([_dot(qs, kc_t), _dot_nt(qs, k_new)], [vc, v], log, _bound(log, [q], k2max, 4))
        o_ref[rows, :] = _diff_finish(o, qb, lam, post, g_ref[0], by_head)

    _loop(DEC_SEQ // qb, log, block)


def _mla_sample(j, qn_ref, qp_ref, ckv_ref, kpe_ref, cckv_ref, ckpe_ref, cosq_ref, sinq_ref, cosk_ref, sink_ref,
                gckv_ref, wuk_ref, wuv_ref, o_ref, log):
    c = (MLA_NOPE + MLA_ROPE) ** -0.5 * LOG2E
    wuk, wuv = _bf(wuk_ref[0]), _bf(wuv_ref[0])
    ckv_new = _bf(_rms(ckv_ref[...], gckv_ref[0]))
    ckv_old = _bf(cckv_ref[...])
    kpe_new = _bf(_tile4(_rope(kpe_ref[...], cosk_ref[...], sink_ref[...])))
    kpe_old = _bf(jnp.concatenate([ckpe_ref[...]] * MLA_HEADS, axis=0).T)
    k_old = jnp.concatenate([_bf(_dot(ckv_old, wuk)), kpe_old], axis=1)
    k_new = jnp.concatenate([_bf(_dot(ckv_new, wuk)), kpe_new], axis=1)
    vo, vn = _with_ones(_bf(_dot(ckv_old, wuv))), _with_ones(_bf(_dot(ckv_new, wuv)))
    groups = _mla_groups(j)
    k2max = _key_bound([k_old, k_new], groups)

    qb = QB_EXACT if log is None else QB

    def block(qi, log):
        rows = pl.ds(_aligned(qi * qb, qb), qb)
        qp = _rope(qp_ref[rows, :], cosq_ref[rows, :], sinq_ref[rows, :]) * c
        qn = qn_ref[rows, :] * c
        qs = _mla_queries(qn, qp, j)
        o = _attend([_dot_nt(qs, k_old), _dot_nt(qs, k_new)], [vo, vn], log, _bound(log, [qn, qp], k2max, 2))
        o_ref[rows, :] = _unstack_heads(_normalised(o), qb)

    _loop(DEC_SEQ // qb, log, block)


def _mix_sample_kernel(qa_ref, ka_ref, va_ref, qb_ref, kb_ref, vb_ref, qn_ref, qp_ref, ckv_ref, kpe_ref,
                       cnak_ref, cnav_ref, cdk_ref, cdv_ref, cckv_ref, ckpe_ref, tt_ref,
                       cosb_ref, sinb_ref, cosq_ref, sinq_ref, cosk_ref, sink_ref,
                       cst_ref, gsub_ref, gckv_ref, wuk_ref, wuv_ref, oa_ref, ob_ref, oc_ref):
    j = pl.program_id(1)

    def run(log):
        _na_sample(qa_ref, ka_ref, va_ref, cnak_ref, cnav_ref, tt_ref, oa_ref, log)
        _diff_sample(qb_ref, kb_ref, vb_ref, cdk_ref, cdv_ref, cosb_ref, sinb_ref, cst_ref, gsub_ref, ob_ref, log)
        _mla_sample(j, qn_ref, qp_ref, ckv_ref, kpe_ref, cckv_ref, ckpe_ref, cosq_ref, sinq_ref, cosk_ref, sink_ref,
                    gckv_ref, wuk_ref, wuv_ref, oc_ref, log)

    log = _ShiftLog()
    run(log)

    @pl.when(log.unsafe())
    def _():
        run(None)


def _mix_sample(l, pa, pb, pc, caches_t, tt, tables, cst, g_sub2, g_ckv, w_uk, w_uv):
    first = N_PROMPT // DEC_SEQ

    def cols(block):
        return pl.BlockSpec((DEC_SEQ, LANES), lambda b, j: (first + b, block(j)))

    def cache(*tail, pair=False):
        return pl.BlockSpec((None, None) + tail, lambda b, j: (b, l, j if pair else 0) + (0,) * (len(tail) - 1))

    def layer(*tail):
        return pl.BlockSpec((1,) + tail, lambda b, j: (l,) + (0,) * len(tail))

    table = pl.BlockSpec((DEC_SEQ, LANES), lambda b, j: (0, 0), pipeline_mode=pl.Buffered(1))
    qkv = [cols(lambda j: j), cols(lambda j: N_PAIRS + j), cols(lambda j: 2 * N_PAIRS + j)]
    seg_c = [cols(lambda j: j), cols(lambda j: C_QP // LANES), cols(lambda j: C_CKV // LANES), cols(lambda j: C_KPE // LANES)]
    kv_t = cache(2, 64, PAST_LEN, pair=True)
    w_pair = pl.BlockSpec((1, MLA_KV_RANK, LANES), lambda b, j: (l, 0, j))
    out = pl.BlockSpec((DEC_SEQ, LANES), lambda b, j: (b, j))
    return pl.pallas_call(
        _mix_sample_kernel,
        grid=(DEC_BATCH, N_PAIRS),
        in_specs=qkv + qkv + seg_c + [kv_t, kv_t, kv_t, kv_t, cache(PAST_LEN, MLA_KV_RANK), cache(MLA_ROPE, PAST_LEN),
                                      pl.BlockSpec((None, 2, N_DROW - 1, GRID_W, LANES), lambda b, j: (l, j, 0, 0, 0)),
                                      table, table, table, table, table, table,
                                      layer(2, LANES), layer(1, LANES), layer(1, MLA_KV_RANK), w_pair, w_pair],
        out_specs=[out, out, out],
        out_shape=[jax.ShapeDtypeStruct((N_SAMPLE, WIDTH), jnp.float32)] * 3,
        compiler_params=_params("parallel", "arbitrary"),
        name="mix_sample",
    )(pa, pa, pa, pb, pb, pb, pc, pc, pc, pc, *caches_t, tt, *tables, cst, g_sub2, g_ckv, w_uk, w_uv)


FF_CHUNK = 1024


def _outffn_kernel(n_x, first, final, *refs):
    x_refs, op_ref, os_refs = refs[:n_x], refs[n_x], refs[n_x + 1:n_x + 4]
    (od_ref, wout_ref, g1_ref, gffn_ref, sh2_ref, sc2_ref, g2_ref, w1_ref, w2_ref, gfin_ref, y_ref) = refs[n_x + 4:]
    o_att = jnp.where(first + pl.program_id(0) < TILES_PROMPT, op_ref[...],
                      jnp.concatenate([r[...] for r in os_refs], axis=1))
    acc = (_dot(_bf(o_att), wout_ref[0, 0:O_ATT, :])
           + _dot(_bf(od_ref[...]), wout_ref[0, O_ATT:O_ATT + WIDTH, :]))
    x1 = _read_tile(x_refs, first) + g1_ref[...] * acc
    hf = _bf(_rms(x1, gffn_ref[0]) * (1.0 + sc2_ref[...]) + sh2_ref[...])
    acc = jnp.zeros((TM, D_MODEL), jnp.float32)
    for c in range(D_FF // FF_CHUNK):
        cols = slice(FF_CHUNK * c, FF_CHUNK * (c + 1))
        a = jnp.square(jnp.maximum(_dot(hf, w1_ref[0, :, cols]), 0.0))
        acc += _dot(_bf(a), w2_ref[0, cols, :])
    y = x1 + g2_ref[...] * acc
    if final:
        y = _rms(y, gfin_ref[...])
    y_ref[...] = y


def _outffn(l, xs, o_p, o_s, od, w_out, g_ffn, mod, w1, w2, g_final, first, n_tiles):
    def mod_spec(j):
        return pl.BlockSpec((None, None, 1, D_MODEL), lambda i: (l, _row_group(first + i), 0, j))

    def resident(shape):
        return pl.BlockSpec(shape, lambda i: (l,) + (0,) * (len(shape) - 1), pipeline_mode=pl.Buffered(1))

    return pl.pallas_call(
        functools.partial(_outffn_kernel, len(xs), first, l == DEPTH - 1),
        grid=(n_tiles,),
        in_specs=_x_specs(len(xs) == 2, first) + _split_specs(O_ATT, first)[:1] + _split_specs(WIDTH, first)[1:] * 3 + [
            pl.BlockSpec((TM, WIDTH), lambda i: (first + i, 0)),
            resident((1, 4 * WIDTH, D_MODEL)),
            mod_spec(2),
            pl.BlockSpec((1, 1, D_MODEL), lambda i: (l, 0, 0)),
            mod_spec(3), mod_spec(4), mod_spec(5),
            resident((1, D_MODEL, D_FF)),
            resident((1, D_FF, D_MODEL)),
            pl.BlockSpec((1, D_MODEL), lambda i: (0, 0)),
        ],
        out_specs=pl.BlockSpec((TM, D_MODEL), lambda i: (i, 0)),
        out_shape=jax.ShapeDtypeStruct((n_tiles * TM, D_MODEL), jnp.float32),
        compiler_params=_params("parallel"),
        name="outffn",
    )(*xs, o_p, *o_s, od, w_out, mod, g_ffn, mod, mod, mod, w1, w2, g_final)


def _rope32_tables():
    t = np.arange(DEC_SEQ)
    rows, cols = (t // GRID_W).astype(np.float64), (t % GRID_W).astype(np.float64)
    half = 8
    freqs = ROPE_BASE ** (-np.arange(half, dtype=np.float64) / half)
    cos, sin = [], []
    for pos in (rows, cols):
        ang = pos[:, None] * freqs[None, :]
        cos += [np.cos(ang), np.cos(ang)]
        sin += [-np.sin(ang), np.sin(ang)]
    return np.concatenate(cos, axis=1).astype(np.float32), np.concatenate(sin, axis=1).astype(np.float32)


def _rope_tables():
    c32, s32 = _rope32_tables()
    tile = lambda a, n: np.tile(a, (1, n))
    pad = np.zeros((DEC_SEQ, LANES - MLA_ROPE), np.float32)
    cos_k = np.concatenate([c32, pad + 1.0], axis=1)
    sin_k = np.concatenate([s32, pad], axis=1)
    return (tile(c32, 4), tile(s32, 4),
            cos_k, sin_k)


def kernel(x_prompt, x_sample, cache_na_k, cache_na_v, cache_diff_k, cache_diff_v, cache_mla_ckv, cache_mla_kpe, c, c_ctx, w_ada, b_ada, g_mix, g_ffn, w_in, w_out, na_rpb, diff_lq1, diff_lk1, diff_lq2, diff_lk2, diff_g_subln, mla_g_ckv, mla_w_uk, mla_w_uv, sgu_g, sgu_w, sgu_b, w_ff1, w_ff2, g_final):
    f32 = jnp.float32
    m = jnp.concatenate([c_ctx[None, :], c, jnp.zeros((N_MOD_ROWS - 1 - DEC_BATCH, D_MODEL), f32)], axis=0)
    mod = _ada(m, w_ada, b_ada).reshape(DEPTH, N_MOD_ROWS, 1, 6 * D_MODEL)
    cst = _lam_consts(diff_lq1, diff_lk1, diff_lq2, diff_lk2)
    tt = _bias_tiles(na_rpb)
    cos4, sin4, cos_k, sin_k = [jnp.asarray(t) for t in _rope_tables()]
    tables = (cos4, sin4, cos4, sin4, cos_k, sin_k)

    t_last = lambda a: jnp.swapaxes(a, -1, -2)
    w_in_t = t_last(w_in)
    caches_t = (t_last(cache_na_k), t_last(cache_na_v), t_last(cache_diff_k), t_last(cache_diff_v),
                cache_mla_ckv, t_last(cache_mla_kpe))
    w_out_b, w1_b, w2_b = _bf(w_out), _bf(w_ff1), _bf(w_ff2)
    g_mix3 = g_mix.reshape(DEPTH, 1, D_MODEL)
    g_ffn3 = g_ffn.reshape(DEPTH, 1, D_MODEL)
    g_sub2 = jnp.tile(diff_g_subln, (1, 2)).reshape(DEPTH, 1, LANES)
    g_ckv3 = mla_g_ckv.reshape(DEPTH, 1, MLA_KV_RANK)
    sgu_g3 = sgu_g.reshape(DEPTH, 1, WIDTH)
    sgu_bt = sgu_b.transpose(0, 2, 1)
    g_fin2 = g_final.reshape(1, D_MODEL)

    xs = (x_prompt.reshape(N_PROMPT, D_MODEL), x_sample.reshape(N_SAMPLE, D_MODEL))
    new = ()
    for l in range(DEPTH):
        pa, pb, pc, od = _inproj(l, xs, g_mix3, mod, w_in_t, sgu_g3, sgu_w, sgu_bt)
        o_p, *new = _mix_prompt(l, pa, pb, pc, cst, g_sub2, g_ckv3, mla_w_uk, mla_w_uv, new)
        o_s = _mix_sample(l, pa, pb, pc, caches_t, tt, tables, cst, g_sub2, g_ckv3, mla_w_uk, mla_w_uv)
        ffn = functools.partial(_outffn, l, xs, o_p, o_s, od, w_out_b, g_ffn3, mod, w1_b, w2_b, g_fin2)
        if l < DEPTH - 1:
            xs = (ffn(0, TILES_PROMPT + TILES_SAMPLE),)
        else:
            xs = (ffn(0, TILES_PROMPT), ffn(TILES_PROMPT, TILES_SAMPLE))
    y_prompt = xs[0].reshape(BATCH, SEQ, D_MODEL)
    y_sample = xs[1].reshape(DEC_BATCH, DEC_SEQ, D_MODEL)
    na_k, na_v, diff_k, diff_v, mla_ckv, mla_kpe = new
    return (y_prompt, y_sample, t_last(na_k), t_last(na_v), t_last(diff_k), t_last(diff_v), mla_ckv, t_last(mla_kpe))
```

```python
import functools
import math

import numpy as np
import jax
import jax.numpy as jnp
from jax import lax
from jax.experimental import pallas as pl
from jax.experimental.pallas import tpu as pltpu

D_MODEL = 1024
BATCH = 16
SEQ = 256
DEPTH = 4
DEC_BATCH = 2
DEC_SEQ = 1024
PAST_LEN = 512
GRID_W = 64
GRID_ROWS = DEC_SEQ // GRID_W
HEAD_DIM = 64
NA_HEADS = 4
NA_WIN_ROWS = 8
NA_WIN_COLS = 16
DIFF_HEADS = 4
DIFF_QK_DIM = 32
DIFF_V_DIM = 64
MLA_HEADS = 4
MLA_NOPE = 64
MLA_ROPE = 32
MLA_V = 64
MLA_KV_RANK = 128
SGU_GROUPS = 4
SGU_GROUP_DIM = 64
SGU_CHUNK = 128
D_FF = 4 * D_MODEL
ROPE_BASE = 10000.0
EPS = 1e-6
NEG_INF = -1e30
LOG2E = 1.4426950408889634

N_HEADS = 4
N_PAIRS = N_HEADS // 2
LANES = 128
WIDTH = 256
N_PROMPT = BATCH * SEQ
N_SAMPLE = DEC_BATCH * DEC_SEQ
N_TOK = N_PROMPT + N_SAMPLE
N_MOD_ROWS = 8

SEG_A = 3 * WIDTH
SEG_B = 3 * WIDTH
SEG_C = 640
SEG_D = 2 * WIDTH
SEG_C_PAD = 96
IN_COLS_P = SEG_A + SEG_B + SEG_C + SEG_D
O_ATT = 3 * WIDTH

TM = 512
TILES_PROMPT = N_PROMPT // TM
TILES_SAMPLE = N_SAMPLE // TM
PB = 2
QB = 512
QB_EXACT = 64
VMEM_LIMIT = 56 * 1024 * 1024


def _bf(x):
    return x.astype(jnp.bfloat16)


def _dot(a, b):
    return jnp.dot(a, b, preferred_element_type=jnp.float32)


def _dot_nt(a, b):
    return lax.dot_general(a, b, (((1,), (1,)), ((), ())), preferred_element_type=jnp.float32)


def _rms(x, g):
    ms = jnp.mean(x * x, axis=-1, keepdims=True)
    return x * lax.rsqrt(ms + EPS) * g


def _lane_range(lo, hi, width=LANES):
    lane = lax.broadcasted_iota(jnp.int32, (1, width), 1)
    return (lane >= lo) & (lane < hi)


def _with_ones(v):
    return jnp.concatenate([v, jnp.ones((v.shape[0], LANES), jnp.bfloat16)], axis=1)


def _attend(scores, values, log=None, bound=None):
    if bound is None:
        m = functools.reduce(jnp.maximum, [jnp.max(s, axis=-1, keepdims=True) for s in scores])
    else:
        m = bound
    o = functools.reduce(lambda a, b: a + b, [_dot(_bf(jnp.exp2(s - m)), v) for s, v in zip(scores, values)])
    if bound is not None:
        _note_denominators(log, o)
    return o


def _note_denominators(log, o_ext):
    den = jnp.min(o_ext[:, LANES:2 * LANES], axis=0, keepdims=True)
    log.least = jnp.minimum(log.least, jnp.min(den, axis=1, keepdims=True))


BOUND_SLACK = 1.02
MIN_DENOMINATOR = 2.0 ** -88


class _ShiftLog:
    def __init__(self):
        self.least = jnp.full((1, 1), jnp.inf, jnp.float32)

    def unsafe(self):
        return jnp.logical_not(self.least[0, 0] >= MIN_DENOMINATOR)


def _group_matrix(width, n_groups, extra=None):
    i = lax.broadcasted_iota(jnp.int32, (width, LANES), 0)
    j = lax.broadcasted_iota(jnp.int32, (width, LANES), 1)
    size = LANES // n_groups
    hit = (i // size == j // size) & (i < LANES)
    for t in range(n_groups if extra else 0):
        lo, hi = extra(t)
        hit = hit | ((i >= lo) & (i < hi) & (j // size == t))
    return jnp.where(hit, 1.0, 0.0).astype(jnp.bfloat16)


def _squares(x):
    xf = x.astype(jnp.float32)
    return _bf(xf * xf)


def _key_bound(keys, groups):
    return functools.reduce(jnp.maximum, [jnp.max(_dot(_squares(k), groups), axis=0, keepdims=True) for k in keys])


def _bound(log, q_parts, k2max, n_groups, extra=0.0):
    if log is None:
        return None
    size = LANES // n_groups
    q_norm = jnp.sqrt(functools.reduce(lambda a, b: a + b,
                                       [jnp.sum(q * q, axis=-1, keepdims=True) for q in q_parts])) * BOUND_SLACK
    return jnp.concatenate([q_norm * jnp.sqrt(k2max[:, size * t:size * t + 1]) + extra for t in range(n_groups)], axis=0)


def _loop(n, log, body):
    if log is not None:
        for i in range(n):
            body(i, log)
    else:
        lax.fori_loop(0, n, lambda i, carry: body(i, None) or carry, 0)


def _aligned(start, multiple):
    return start if isinstance(start, int) else pl.multiple_of(start, multiple)


def _normalised(o_ext):
    return o_ext[:, 0:LANES] * (1.0 / o_ext[:, LANES:2 * LANES])


def _swap8(x):
    lane = lax.broadcasted_iota(jnp.int32, (1, LANES), 1)
    return jnp.where((lane & 15) < 8, pltpu.roll(x, LANES - 8, 1), pltpu.roll(x, 8, 1))


def _rope(x, cos, sin):
    outs = []
    for c in range(x.shape[1] // LANES):
        sl = slice(LANES * c, LANES * (c + 1))
        xc = x[:, sl]
        outs.append(xc * cos[:, sl] + _swap8(xc) * sin[:, sl])
    return outs[0] if len(outs) == 1 else jnp.concatenate(outs, axis=1)


def _tile4(x):
    return x + pltpu.roll(x, 32, 1) + pltpu.roll(x, 64, 1) + pltpu.roll(x, 96, 1)


def _params(*sem):
    return pltpu.CompilerParams(dimension_semantics=sem, vmem_limit_bytes=VMEM_LIMIT)


ADA_TN = 1536


def _ada_kernel(m_ref, w_ref, b_ref, o_ref):
    m = m_ref[...]
    s = m * jax.nn.sigmoid(m)
    o_ref[0] = _dot(_bf(s), _bf(w_ref[0])) + b_ref[0]


def _ada(m, w_ada, b_ada):
    n = 6 * D_MODEL
    return pl.pallas_call(
        _ada_kernel,
        grid=(DEPTH, n // ADA_TN),
        in_specs=[
            pl.BlockSpec((N_MOD_ROWS, D_MODEL), lambda l, j: (0, 0)),
            pl.BlockSpec((1, D_MODEL, ADA_TN), lambda l, j: (l, 0, j)),
            pl.BlockSpec((1, 1, ADA_TN), lambda l, j: (l, 0, j)),
        ],
        out_specs=pl.BlockSpec((1, N_MOD_ROWS, ADA_TN), lambda l, j: (l, 0, j)),
        out_shape=jax.ShapeDtypeStruct((DEPTH, N_MOD_ROWS, n), jnp.float32),
        compiler_params=_params("parallel", "parallel"),
        name="ada",
    )(m, w_ada, b_ada.reshape(DEPTH, 1, n))


def _lam_kernel(lq1_ref, lk1_ref, lq2_ref, lk2_ref, init_ref, o_ref):
    init = init_ref[...]
    a = jnp.exp(jnp.sum(lq1_ref[...] * lk1_ref[...], axis=-1, keepdims=True))
    b = jnp.exp(jnp.sum(lq2_ref[...] * lk2_ref[...], axis=-1, keepdims=True))
    lam = a - b + init
    post = 1.0 - init
    for l in range(DEPTH):
        o_ref[l, 0:1, :] = jnp.broadcast_to(lam[l:l + 1], (1, LANES))
        o_ref[l, 1:2, :] = jnp.broadcast_to(post[l:l + 1], (1, LANES))


def _lam_consts(lq1, lk1, lq2, lk2):
    init = np.array([[0.8 - 0.6 * math.exp(-0.3 * l)] for l in range(DEPTH)], np.float32)
    return pl.pallas_call(
        _lam_kernel,
        out_shape=jax.ShapeDtypeStruct((DEPTH, 2, LANES), jnp.float32),
        name="diff_lambda",
    )(lq1, lk1, lq2, lk2, jnp.asarray(init))


N_DROW = 2 * NA_WIN_ROWS - 1
N_DCOL = 2 * NA_WIN_COLS - 1


def _bias_kernel(rpb_ref, o_ref):
    l = pl.program_id(0)
    h = pl.program_id(1)
    base = (l * NA_HEADS + h) * (N_DROW * N_DCOL)
    cq = lax.broadcasted_iota(jnp.int32, (GRID_W, LANES), 0)
    lane = lax.broadcasted_iota(jnp.int32, (GRID_W, LANES), 1)
    ck = lane & (GRID_W - 1)
    dcol = jnp.clip(ck - cq, -(NA_WIN_COLS - 1), NA_WIN_COLS - 1) + (NA_WIN_COLS - 1)
    hi = lane >= GRID_W
    for a in range(N_DROW - 1):
        acc = jnp.zeros((GRID_W, LANES), jnp.float32)
        for j in range(N_DCOL):
            lo_v = rpb_ref[base + a * N_DCOL + j]
            hi_v = rpb_ref[base + (a + 1) * N_DCOL + j]
            acc = jnp.where(dcol == j, jnp.where(hi, hi_v, lo_v), acc)
        o_ref[0, 0, a] = acc * LOG2E


def _bias_tiles(na_rpb):
    return pl.pallas_call(
        _bias_kernel,
        grid=(DEPTH, NA_HEADS),
        in_specs=[pl.BlockSpec(memory_space=pltpu.SMEM)],
        out_specs=pl.BlockSpec((1, 1, N_DROW - 1, GRID_W, LANES), lambda l, h: (l, h, 0, 0, 0)),
        out_shape=jax.ShapeDtypeStruct((DEPTH, NA_HEADS, N_DROW - 1, GRID_W, LANES), jnp.float32),
        compiler_params=_params("parallel", "parallel"),
        name="na_bias_tiles",
    )(na_rpb.reshape(-1))


def _row_group(i):
    return jnp.where(i < TILES_PROMPT, 0, 1 + (i - TILES_PROMPT) // (DEC_SEQ // TM))


def _split_specs(width, first):
    return [pl.BlockSpec((TM, width), lambda i: (jnp.minimum(first + i, TILES_PROMPT - 1), 0)),
            pl.BlockSpec((TM, width), lambda i: (jnp.maximum(first + i - TILES_PROMPT, 0), 0))]


def _x_specs(split, first):
    if not split:
        return [pl.BlockSpec((TM, D_MODEL), lambda i: (first + i, 0))]
    return _split_specs(D_MODEL, first)


def _read_tile(refs, first):
    if len(refs) == 1:
        return refs[0][...]
    return jnp.where(first + pl.program_id(0) < TILES_PROMPT, refs[0][...], refs[1][...])


IN_COLS = 2592
IN_QC, IN_CKV, IN_D = 1536, 1920, 2080
TR_ROWS = 256


def _gelu_tanh(x):
    return 0.5 * x * (1.0 + jnp.tanh(math.sqrt(2.0 / math.pi) * (x + 0.044715 * (x * x * x))))


def _sgu(pd, g, w_ref, bt):
    u = _gelu_tanh(pd[:, 0:WIDTH])
    v = _gelu_tanh(pd[:, WIDTH:2 * WIDTH])
    grp = lax.broadcasted_iota(jnp.int32, (1, WIDTH), 1) // SGU_GROUP_DIM
    v2 = v * v
    ms = jnp.zeros_like(v)
    for gi in range(SGU_GROUPS):
        sel = grp == gi
        tot = jnp.sum(jnp.where(sel, v2, 0.0), axis=-1, keepdims=True)
        ms = jnp.where(sel, tot * (1.0 / SGU_GROUP_DIM), ms)
    vg = _bf(v * lax.rsqrt(ms + EPS) * g)
    outs = []
    for c in range(pd.shape[0] // SGU_CHUNK):
        rows = slice(SGU_CHUNK * c, SGU_CHUNK * (c + 1))
        mixed = jnp.zeros((SGU_CHUNK, WIDTH), jnp.float32)
        for gi in range(SGU_GROUPS):
            full = _dot(_bf(w_ref[0, gi]), vg[rows]) + bt[:, gi:gi + 1]
            mixed = jnp.where(grp == gi, full, mixed)
        outs.append(u[rows] * mixed)
    return jnp.concatenate(outs, axis=0)


def _w_in_row_pieces():
    qn = [(IN_QC + 96 * h, MLA_NOPE) for h in range(MLA_HEADS)]
    qp = [(IN_QC + 96 * h + MLA_NOPE, MLA_ROPE) for h in range(MLA_HEADS)]
    seg_c = qn + qp + [(IN_CKV, MLA_KV_RANK + MLA_ROPE)]
    return (0, SEG_A + SEG_B), seg_c, (IN_D, SEG_D)


def _load_w_in(wt_ref, w_scr):
    ab, seg_c, d = _w_in_row_pieces()
    c_rows = jnp.concatenate([wt_ref[0, s:s + n, :] for s, n in seg_c]
                             + [jnp.zeros((SEG_C_PAD, D_MODEL), jnp.float32)], axis=0)
    for t in range(SEG_C // LANES):
        w_scr[:, SEG_A + SEG_B + LANES * t:SEG_A + SEG_B + LANES * (t + 1)] = _bf(c_rows[LANES * t:LANES * (t + 1)].T)
    for (src, n), dst in ((ab, 0), (d, SEG_A + SEG_B + SEG_C)):
        for t in range(n // TR_ROWS):
            rows = wt_ref[0, src + TR_ROWS * t:src + TR_ROWS * (t + 1), :]
            w_scr[:, dst + TR_ROWS * t:dst + TR_ROWS * (t + 1)] = _bf(rows.T)


def _inproj_kernel(n_x, *refs):
    x_refs = refs[:n_x]
    (g_ref, sh_ref, sc_ref, wt_ref, sg_ref, sw_ref, sbt_ref, pa_ref, pb_ref, pc_ref, od_ref, w_scr) = refs[n_x:]

    @pl.when(pl.program_id(0) == 0)
    def _():
        _load_w_in(wt_ref, w_scr)

    h = _rms(_read_tile(x_refs, 0), g_ref[0]) * (1.0 + sc_ref[...]) + sh_ref[...]
    hb = _bf(h)
    off = SEG_A + SEG_B + SEG_C
    od_ref[...] = _sgu(_dot(hb, w_scr[:, off:off + SEG_D]), sg_ref[0], sw_ref, sbt_ref[0])
    off = 0
    for ref in (pa_ref, pb_ref, pc_ref):
        n = ref.shape[1]
        ref[...] = _dot(hb, w_scr[:, off:off + n])
        off += n


def _inproj(l, xs, g_mix, mod, w_in_t, sgu_g, sgu_w, sgu_bt):
    def mod_spec(j):
        return pl.BlockSpec((None, None, 1, D_MODEL), lambda i: (l, _row_group(i), 0, j))

    widths = (SEG_A, SEG_B, SEG_C, WIDTH)
    return pl.pallas_call(
        functools.partial(_inproj_kernel, len(xs)),
        grid=(N_TOK // TM,),
        in_specs=_x_specs(len(xs) == 2, 0) + [
            pl.BlockSpec((1, 1, D_MODEL), lambda i: (l, 0, 0)),
            mod_spec(0), mod_spec(1),
            pl.BlockSpec((1, IN_COLS, D_MODEL), lambda i: (l, 0, 0), pipeline_mode=pl.Buffered(1)),
            pl.BlockSpec((1, 1, WIDTH), lambda i: (l, 0, 0)),
            pl.BlockSpec((1, SGU_GROUPS, SGU_CHUNK, SGU_CHUNK), lambda i: (l, 0, 0, 0)),
            pl.BlockSpec((1, SGU_CHUNK, SGU_GROUPS), lambda i: (l, 0, 0)),
        ],
        out_specs=[pl.BlockSpec((TM, n), lambda i: (i, 0)) for n in widths],
        out_shape=[jax.ShapeDtypeStruct((N_TOK, n), jnp.float32) for n in widths],
        scratch_shapes=[pltpu.VMEM((D_MODEL, IN_COLS_P), jnp.bfloat16)],
        compiler_params=_params("arbitrary"),
        name="inproj",
    )(*xs, g_mix, mod, mod, w_in_t, sgu_g, sgu_w, sgu_bt)


C_QN, C_QP, C_CKV, C_KPE = 0, 256, 384, 512


def _stack_heads(qp):
    lo = _lane_range(0, 64)
    return jnp.concatenate([_bf(jnp.where(lo, qp, 0.0)), _bf(jnp.where(lo, 0.0, qp))], axis=0)


def _unstack_heads(o, n):
    return jnp.where(_lane_range(0, 64), o[0:n], o[n:2 * n])


def _pair_t(c_ref):
    return jnp.concatenate([c_ref[0], c_ref[1]], axis=0)


def _stack_components(qp):
    return jnp.concatenate([_bf(jnp.where(_lane_range(32 * t, 32 * (t + 1)), qp, 0.0)) for t in range(4)], axis=0)


def _group_mean_sq(x, groups, size):
    sq = x * x
    hi = _bf(sq)
    rest = sq - hi.astype(jnp.float32)
    mid = _bf(rest)
    lo = _bf(rest - mid.astype(jnp.float32))
    return (_dot(hi, groups) + _dot(mid, groups) + _dot(lo, groups)) * (1.0 / size)


def _diff_finish(o, n, lam, post, g2, by_head):
    den = o[:, LANES:2 * LANES]
    outs = []
    for t in range(2):
        p1 = o[2 * t * n:(2 * t + 1) * n, 0:LANES] * (1.0 / den[2 * t * n:(2 * t + 1) * n])
        p2 = o[(2 * t + 1) * n:(2 * t + 2) * n, 0:LANES] * (lam / den[(2 * t + 1) * n:(2 * t + 2) * n])
        outs.append(p1 - p2)
    d = jnp.where(_lane_range(0, 64), outs[0], outs[1])
    return d * lax.rsqrt(_group_mean_sq(d, by_head, DIFF_V_DIM) + EPS) * g2 * post


def _mla_groups(j):
    return _group_matrix(2 * LANES, 2,
                         lambda t: (LANES + MLA_ROPE * (2 * j + t), LANES + MLA_ROPE * (2 * j + t + 1)))


def _mla_queries(qn_pair, qp_all, j):
    halves = []
    for t in range(2):
        h = 2 * j + t
        halves.append(jnp.concatenate([
            _bf(jnp.where(_lane_range(64 * t, 64 * (t + 1)), qn_pair, 0.0)),
            _bf(jnp.where(_lane_range(MLA_ROPE * h, MLA_ROPE * (h + 1)), qp_all, 0.0))], axis=1))
    return jnp.concatenate(halves, axis=0)


def _write_heads_t(p_ref, rows, col0, out_ref, bb):
    xt = p_ref[rows, col0:col0 + WIDTH].T
    for h in range(N_HEADS):
        out_ref[bb, 0, h] = xt[64 * h:64 * (h + 1)]
    _clear_other_layers(out_ref, bb)


def _clear_other_layers(out_ref, bb):
    if out_ref.shape[1] > 1:
        out_ref[bb, 1:] = jnp.zeros(out_ref.shape[1:], jnp.float32)[1:]


def _mix_prompt_kernel(n_prev, *refs):
    ins, outs = refs[:8], refs[8 + n_prev:]
    log = _ShiftLog()
    _mix_prompt_pass(ins, outs, log)

    @pl.when(log.unsafe())
    def _():
        _mix_prompt_pass(ins, outs, None)


def _mix_prompt_pass(ins, outs, log):
    pa_ref, pb_ref, pc_ref, cst_ref, gsub_ref, gckv_ref, wuk_ref, wuv_ref = ins
    o_ref, nak_ref, nav_ref, dk_ref, dv_ref, ckv_ref, kpe_ref = outs
    first_pass = log is not None
    c_a = HEAD_DIM ** -0.5 * LOG2E
    c_b = DIFF_QK_DIM ** -0.5 * LOG2E
    c_c = (MLA_NOPE + MLA_ROPE) ** -0.5 * LOG2E
    lam = cst_ref[0, 0:1, 0:1]
    post = cst_ref[0, 1:2, 0:1]
    wuk, wuv = _bf(wuk_ref[0]), _bf(wuv_ref[0])
    by_head, by_comp = _group_matrix(LANES, 2), _group_matrix(LANES, 4)

    def sequence(bb, log):
        rows = pl.ds(_aligned(bb * SEQ, SEQ), SEQ)
        for j in range(N_PAIRS):
            cols = slice(LANES * j, LANES * (j + 1))
            k = _bf(pa_ref[rows, WIDTH + LANES * j:WIDTH + LANES * (j + 1)])
            v = _with_ones(_bf(pa_ref[rows, 2 * WIDTH + LANES * j:2 * WIDTH + LANES * (j + 1)]))
            q = pa_ref[rows, cols] * c_a
            o = _attend([_dot_nt(_stack_heads(q), k)], [v], log, _bound(log, [q], _key_bound([k], by_head), 2))
            o_ref[rows, cols] = _unstack_heads(_normalised(o), SEQ)
        if first_pass:
            _write_heads_t(pa_ref, rows, WIDTH, nak_ref, bb)
            _write_heads_t(pa_ref, rows, 2 * WIDTH, nav_ref, bb)
        for j in range(N_PAIRS):
            cols = slice(LANES * j, LANES * (j + 1))
            k = _bf(pb_ref[rows, WIDTH + LANES * j:WIDTH + LANES * (j + 1)])
            v = _with_ones(_bf(pb_ref[rows, 2 * WIDTH + LANES * j:2 * WIDTH + LANES * (j + 1)]))
            q = pb_ref[rows, cols] * c_b
            o = _attend([_dot_nt(_stack_components(q), k)], [v], log, _bound(log, [q], _key_bound([k], by_comp), 4))
            o_ref[rows, WIDTH + LANES * j:WIDTH + LANES * (j + 1)] = _diff_finish(o, SEQ, lam, post, gsub_ref[0], by_head)
        if first_pass:
            _write_heads_t(pb_ref, rows, WIDTH, dk_ref, bb)
            _write_heads_t(pb_ref, rows, 2 * WIDTH, dv_ref, bb)
        ckv = _rms(pc_ref[rows, C_CKV:C_CKV + MLA_KV_RANK], gckv_ref[0])
        kpe_slot = pc_ref[rows, C_KPE:C_KPE + LANES]
        if first_pass:
            ckv_ref[bb, 0] = ckv
            _clear_other_layers(ckv_ref, bb)
            kpe_ref[bb, 0] = kpe_slot.T[0:MLA_ROPE]
            _clear_other_layers(kpe_ref, bb)
        ckv_b = _bf(ckv)
        kn = _bf(_dot(ckv_b, wuk))
        vv = _bf(_dot(ckv_b, wuv))
        kpe4 = _bf(_tile4(kpe_slot))
        qn = pc_ref[rows, C_QN:C_QN + WIDTH] * c_c
        qp = pc_ref[rows, C_QP:C_QP + LANES] * c_c
        for j in range(N_PAIRS):
            cols = slice(LANES * j, LANES * (j + 1))
            k = jnp.concatenate([kn[:, cols], kpe4], axis=1)
            qs = _mla_queries(qn[:, cols], qp, j)
            groups = _mla_groups(j)
            o = _attend([_dot_nt(qs, k)], [_with_ones(vv[:, cols])], log,
                        _bound(log, [qn[:, cols], qp], _key_bound([k], groups), 2))
            o_ref[rows, 2 * WIDTH + LANES * j:2 * WIDTH + LANES * (j + 1)] = _unstack_heads(_normalised(o), SEQ)

    _loop(PB, log, sequence)


def _mix_prompt(l, pa, pb, pc, cst, g_sub2, g_ckv, w_uk, w_uv, prev):
    n_prev = len(prev)
    tails = [(NA_HEADS, HEAD_DIM, SEQ)] * 2 + [(DIFF_HEADS, 64, SEQ)] * 2 + [(SEQ, MLA_KV_RANK), (MLA_ROPE, SEQ)]

    def cache_spec(tail):
        if l == 0:
            return pl.BlockSpec((PB, DEPTH) + tail, lambda b: (b, 0) + (0,) * len(tail))
        return pl.BlockSpec((PB, 1) + tail, lambda b: (b, l) + (0,) * len(tail))

    def rows(width):
        return pl.BlockSpec((PB * SEQ, width), lambda b: (b, 0))

    def layer(*tail):
        return pl.BlockSpec((1,) + tail, lambda b: (l,) + (0,) * len(tail))

    return pl.pallas_call(
        functools.partial(_mix_prompt_kernel, n_prev),
        grid=(BATCH // PB,),
        in_specs=[rows(SEG_A), rows(SEG_B), rows(SEG_C), layer(2, LANES), layer(1, LANES), layer(1, MLA_KV_RANK),
                  layer(MLA_KV_RANK, WIDTH), layer(MLA_KV_RANK, WIDTH)] + [pl.BlockSpec(memory_space=pl.ANY)] * n_prev,
        out_specs=[rows(O_ATT)] + [cache_spec(t) for t in tails],
        out_shape=[jax.ShapeDtypeStruct((N_PROMPT, O_ATT), jnp.float32)]
        + [jax.ShapeDtypeStruct((BATCH, DEPTH) + t, jnp.float32) for t in tails],
        input_output_aliases={8 + i: 1 + i for i in range(n_prev)},
        compiler_params=_params("parallel"),
        name="mix_prompt",
    )(pa, pb, pc, cst, g_sub2, g_ckv, w_uk, w_uv, *prev)


def _na_row_groups():
    kh = min(NA_WIN_ROWS, GRID_ROWS)
    r0s = [min(max(r - kh // 2, 0), GRID_ROWS - kh) for r in range(GRID_ROWS)]
    groups = []
    for r, r0 in enumerate(r0s):
        if groups and groups[-1][2] == r0:
            groups[-1][1] = r
        else:
            groups.append([r, r, r0])
    return kh, [tuple(g) for g in groups]


def _na_sample(q_ref, k_ref, v_ref, ck_ref, cv_ref, tt_ref, o_ref, log):
    c = HEAD_DIM ** -0.5 * LOG2E
    kh, groups = _na_row_groups()
    lk = kh * GRID_W

    def in_window(n):
        cq = lax.broadcasted_iota(jnp.int32, (n, lk), 0) & (GRID_W - 1)
        ck = lax.broadcasted_iota(jnp.int32, (n, lk), 1) & (GRID_W - 1)
        c0 = jnp.clip(cq - NA_WIN_COLS // 2, 0, GRID_W - NA_WIN_COLS)
        return (ck >= c0) & (ck < c0 + NA_WIN_COLS)

    kc_t = _bf(_pair_t(ck_ref))
    vc = _with_ones(_bf(_pair_t(cv_ref).T))
    by_head = _group_matrix(LANES, 2)
    k2max = _key_bound([_bf(k_ref[...]), _bf(_pair_t(ck_ref).T)], by_head)
    tmax = functools.reduce(jnp.maximum, [tt_ref[t, a] for t in range(2) for a in range(N_DROW - 1)])
    bplus = jnp.maximum(jnp.max(jnp.max(tmax, axis=-1, keepdims=True), axis=0, keepdims=True), 0.0)

    def group(row0, key0, offsets, log):
        n = len(offsets) * GRID_W
        rows, keys = pl.ds(row0, n), pl.ds(key0, lk)
        q = q_ref[rows, :] * c
        qg = _stack_heads(q)
        k = _bf(k_ref[keys, :])
        v = _with_ones(_bf(v_ref[keys, :]))
        bias = jnp.concatenate([
            jnp.concatenate([tt_ref[t, 2 * i - off + NA_WIN_ROWS - 1] for i in range(kh // 2)], axis=1)
            for t in range(2) for off in offsets], axis=0)
        s_loc = jnp.where(in_window(2 * n), _dot_nt(qg, k) + bias, NEG_INF)
        o = _attend([_dot(qg, kc_t), s_loc], [vc, v], log, _bound(log, [q], k2max, 2, bplus))
        o_ref[rows, :] = _unstack_heads(_normalised(o), n)

    if log is None:
        def any_row(r, log):
            r0 = jnp.clip(r - kh // 2, 0, GRID_ROWS - kh)
            group(_aligned(r * GRID_W, GRID_W), _aligned(r0 * GRID_W, GRID_W), [r - r0], log)

        _loop(GRID_ROWS, log, any_row)
        return

    q_all = q_ref[...] * c
    qs_all = _stack_heads(q_all)
    m_all = _bound(log, [q_all], k2max, 2, bplus)
    o_ctx = _dot(_bf(jnp.exp2(_dot(qs_all, kc_t) - m_all)), vc)

    def fast_group(r_lo, r0, offsets):
        n = len(offsets) * GRID_W

        def pick(x):
            return jnp.concatenate([x[r_lo * GRID_W:r_lo * GRID_W + n],
                                    x[DEC_SEQ + r_lo * GRID_W:DEC_SEQ + r_lo * GRID_W + n]], axis=0)

        keys = slice(r0 * GRID_W, r0 * GRID_W + lk)
        bias = jnp.concatenate([
            jnp.concatenate([tt_ref[t, 2 * i - off + NA_WIN_ROWS - 1] for i in range(kh // 2)], axis=1)
            for t in range(2) for off in offsets], axis=0)
        s_loc = jnp.where(in_window(2 * n), _dot_nt(pick(qs_all), _bf(k_ref[keys, :])) + bias, NEG_INF)
        o = pick(o_ctx) + _dot(_bf(jnp.exp2(s_loc - pick(m_all))), _with_ones(_bf(v_ref[keys, :])))
        _note_denominators(log, o)
        o_ref[r_lo * GRID_W:r_lo * GRID_W + n, :] = _unstack_heads(_normalised(o), n)

    for (r_lo, r_hi, r0) in groups:
        fast_group(r_lo, r0, [r - r0 for r in range(r_lo, r_hi + 1)])


def _diff_sample(q_ref, k_ref, v_ref, ck_ref, cv_ref, cos_ref, sin_ref, cst_ref, g_ref, o_ref, log):
    c = DIFF_QK_DIM ** -0.5 * LOG2E
    lam = cst_ref[0, 0:1, 0:1]
    post = cst_ref[0, 1:2, 0:1]
    k_new = _bf(_rope(k_ref[...], cos_ref[...], sin_ref[...]))
    kc_t = _bf(_pair_t(ck_ref))
    vc = _with_ones(_bf(_pair_t(cv_ref).T))
    v = _with_ones(_bf(v_ref[...]))
    by_head, by_comp = _group_matrix(LANES, 2), _group_matrix(LANES, 4)
    k2max = _key_bound([k_new, _bf(_pair_t(ck_ref).T)], by_comp)

    qb = QB_EXACT if log is None else QB

    def block(qi, log):
        rows = pl.ds(_aligned(qi * qb, qb), qb)
        q = _rope(q_ref[rows, :], cos_ref[rows, :], sin_ref[rows, :]) * c
        qs = _stack_components(q)
        o = _attend([_dot(qs, kc_t), _dot_nt(qs, k_new)], [vc, v], log, _bound(log, [q], k2max, 4))
        o_ref[rows, :] = _diff_finish(o, qb, lam, post, g_ref[0], by_head)

    _loop(DEC_SEQ // qb, log, block)


def _mla_sample(j, qn_ref, qp_ref, ckv_ref, kpe_ref, cckv_ref, ckpe_ref, cosq_ref, sinq_ref, cosk_ref, sink_ref,
                gckv_ref, wuk_ref, wuv_ref, o_ref, log):
    c = (MLA_NOPE + MLA_ROPE) ** -0.5 * LOG2E
    wuk, wuv = _bf(wuk_ref[0]), _bf(wuv_ref[0])
    ckv_new = _bf(_rms(ckv_ref[...], gckv_ref[0]))
    ckv_old = _bf(cckv_ref[...])
    kpe_new = _bf(_tile4(_rope(kpe_ref[...], cosk_ref[...], sink_ref[...])))
    kpe_old = _bf(jnp.concatenate([ckpe_ref[...]] * MLA_HEADS, axis=0).T)
    k_old = jnp.concatenate([_bf(_dot(ckv_old, wuk)), kpe_old], axis=1)
    k_new = jnp.concatenate([_bf(_dot(ckv_new, wuk)), kpe_new], axis=1)
    vo, vn = _with_ones(_bf(_dot(ckv_old, wuv))), _with_ones(_bf(_dot(ckv_new, wuv)))
    groups = _mla_groups(j)
    k2max = _key_bound([k_old, k_new], groups)

    qb = QB_EXACT if log is None else QB

    def block(qi, log):
        rows = pl.ds(_aligned(qi * qb, qb), qb)
        qp = _rope(qp_ref[rows, :], cosq_ref[rows, :], sinq_ref[rows, :]) * c
        qn = qn_ref[rows, :] * c
        qs = _mla_queries(qn, qp, j)
        o = _attend([_dot_nt(qs, k_old), _dot_nt(qs, k_new)], [vo, vn], log, _bound(log, [qn, qp], k2max, 2))
        o_ref[rows, :] = _unstack_heads(_normalised(o), qb)

    _loop(DEC_SEQ // qb, log, block)


def _mix_sample_kernel(qa_ref, ka_ref, va_ref, qb_ref, kb_ref, vb_ref, qn_ref, qp_ref, ckv_ref, kpe_ref,
                       cnak_ref, cnav_ref, cdk_ref, cdv_ref, cckv_ref, ckpe_ref, tt_ref,
                       cosb_ref, sinb_ref, cosq_ref, sinq_ref, cosk_ref, sink_ref,
                       cst_ref, gsub_ref, gckv_ref, wuk_ref, wuv_ref, oa_ref, ob_ref, oc_ref):
    j = pl.program_id(1)

    def run(log):
        _na_sample(qa_ref, ka_ref, va_ref, cnak_ref, cnav_ref, tt_ref, oa_ref, log)
        _diff_sample(qb_ref, kb_ref, vb_ref, cdk_ref, cdv_ref, cosb_ref, sinb_ref, cst_ref, gsub_ref, ob_ref, log)
        _mla_sample(j, qn_ref, qp_ref, ckv_ref, kpe_ref, cckv_ref, ckpe_ref, cosq_ref, sinq_ref, cosk_ref, sink_ref,
                    gckv_ref, wuk_ref, wuv_ref, oc_ref, log)

    log = _ShiftLog()
    run(log)

    @pl.when(log.unsafe())
    def _():
        run(None)


def _mix_sample(l, pa, pb, pc, caches_t, tt, tables, cst, g_sub2, g_ckv, w_uk, w_uv):
    first = N_PROMPT // DEC_SEQ

    def cols(block):
        return pl.BlockSpec((DEC_SEQ, LANES), lambda b, j: (first + b, block(j)))

    def cache(*tail, pair=False):
        return pl.BlockSpec((None, None) + tail, lambda b, j: (b, l, j if pair else 0) + (0,) * (len(tail) - 1))

    def layer(*tail):
        return pl.BlockSpec((1,) + tail, lambda b, j: (l,) + (0,) * len(tail))

    table = pl.BlockSpec((DEC_SEQ, LANES), lambda b, j: (0, 0), pipeline_mode=pl.Buffered(1))
    qkv = [cols(lambda j: j), cols(lambda j: N_PAIRS + j), cols(lambda j: 2 * N_PAIRS + j)]
    seg_c = [cols(lambda j: j), cols(lambda j: C_QP // LANES), cols(lambda j: C_CKV // LANES), cols(lambda j: C_KPE // LANES)]
    kv_t = cache(2, 64, PAST_LEN, pair=True)
    w_pair = pl.BlockSpec((1, MLA_KV_RANK, LANES), lambda b, j: (l, 0, j))
    out = pl.BlockSpec((DEC_SEQ, LANES), lambda b, j: (b, j))
    return pl.pallas_call(
        _mix_sample_kernel,
        grid=(DEC_BATCH, N_PAIRS),
        in_specs=qkv + qkv + seg_c + [kv_t, kv_t, kv_t, kv_t, cache(PAST_LEN, MLA_KV_RANK), cache(MLA_ROPE, PAST_LEN),
                                      pl.BlockSpec((None, 2, N_DROW - 1, GRID_W, LANES), lambda b, j: (l, j, 0, 0, 0)),
                                      table, table, table, table, table, table,
                                      layer(2, LANES), layer(1, LANES), layer(1, MLA_KV_RANK), w_pair, w_pair],
        out_specs=[out, out, out],
        out_shape=[jax.ShapeDtypeStruct((N_SAMPLE, WIDTH), jnp.float32)] * 3,
        compiler_params=_params("parallel", "arbitrary"),
        name="mix_sample",
    )(pa, pa, pa, pb, pb, pb, pc, pc, pc, pc, *caches_t, tt, *tables, cst, g_sub2, g_ckv, w_uk, w_uv)


FF_CHUNK = 1024


def _outffn_kernel(n_x, first, final, *refs):
    x_refs, op_ref, os_refs = refs[:n_x], refs[n_x], refs[n_x + 1:n_x + 4]
    (od_ref, wout_ref, g1_ref, gffn_ref, sh2_ref, sc2_ref, g2_ref, w1_ref, w2_ref, gfin_ref, y_ref) = refs[n_x + 4:]
    o_att = jnp.where(first + pl.program_id(0) < TILES_PROMPT, op_ref[...],
                      jnp.concatenate([r[...] for r in os_refs], axis=1))
    acc = (_dot(_bf(o_att), wout_ref[0, 0:O_ATT, :])
           + _dot(_bf(od_ref[...]), wout_ref[0, O_ATT:O_ATT + WIDTH, :]))
    x1 = _read_tile(x_refs, first) + g1_ref[...] * acc
    hf = _bf(_rms(x1, gffn_ref[0]) * (1.0 + sc2_ref[...]) + sh2_ref[...])
    acc = jnp.zeros((TM, D_MODEL), jnp.float32)
    for c in range(D_FF // FF_CHUNK):
        cols = slice(FF_CHUNK * c, FF_CHUNK * (c + 1))
        a = jnp.square(jnp.maximum(_dot(hf, w1_ref[0, :, cols]), 0.0))
        acc += _dot(_bf(a), w2_ref[0, cols, :])
    y = x1 + g2_ref[...] * acc
    if final:
        y = _rms(y, gfin_ref[...])
    y_ref[...] = y


def _outffn(l, xs, o_p, o_s, od, w_out, g_ffn, mod, w1, w2, g_final, first, n_tiles):
    def mod_spec(j):
        return pl.BlockSpec((None, None, 1, D_MODEL), lambda i: (l, _row_group(first + i), 0, j))

    def resident(shape):
        return pl.BlockSpec(shape, lambda i: (l,) + (0,) * (len(shape) - 1), pipeline_mode=pl.Buffered(1))

    return pl.pallas_call(
        functools.partial(_outffn_kernel, len(xs), first, l == DEPTH - 1),
        grid=(n_tiles,),
        in_specs=_x_specs(len(xs) == 2, first) + _split_specs(O_ATT, first)[:1] + _split_specs(WIDTH, first)[1:] * 3 + [
            pl.BlockSpec((TM, WIDTH), lambda i: (first + i, 0)),
            resident((1, 4 * WIDTH, D_MODEL)),
            mod_spec(2),
            pl.BlockSpec((1, 1, D_MODEL), lambda i: (l, 0, 0)),
            mod_spec(3), mod_spec(4), mod_spec(5),
            resident((1, D_MODEL, D_FF)),
            resident((1, D_FF, D_MODEL)),
            pl.BlockSpec((1, D_MODEL), lambda i: (0, 0)),
        ],
        out_specs=pl.BlockSpec((TM, D_MODEL), lambda i: (i, 0)),
        out_shape=jax.ShapeDtypeStruct((n_tiles * TM, D_MODEL), jnp.float32),
        compiler_params=_params("parallel"),
        name="outffn",
    )(*xs, o_p, *o_s, od, w_out, mod, g_ffn, mod, mod, mod, w1, w2, g_final)


def _rope32_tables():
    t = np.arange(DEC_SEQ)
    rows, cols = (t // GRID_W).astype(np.float64), (t % GRID_W).astype(np.float64)
    half = 8
    freqs = ROPE_BASE ** (-np.arange(half, dtype=np.float64) / half)
    cos, sin = [], []
    for pos in (rows, cols):
        ang = pos[:, None] * freqs[None, :]
        cos += [np.cos(ang), np.cos(ang)]
        sin += [-np.sin(ang), np.sin(ang)]
    return np.concatenate(cos, axis=1).astype(np.float32), np.concatenate(sin, axis=1).astype(np.float32)


def _rope_tables():
    c32, s32 = _rope32_tables()
    tile = lambda a, n: np.tile(a, (1, n))
    pad = np.zeros((DEC_SEQ, LANES - MLA_ROPE), np.float32)
    cos_k = np.concatenate([c32, pad + 1.0], axis=1)
    sin_k = np.concatenate([s32, pad], axis=1)
    return (tile(c32, 4), tile(s32, 4),
            cos_k, sin_k)


def kernel(x_prompt, x_sample, cache_na_k, cache_na_v, cache_diff_k, cache_diff_v, cache_mla_ckv, cache_mla_kpe, c, c_ctx, w_ada, b_ada, g_mix, g_ffn, w_in, w_out, na_rpb, diff_lq1, diff_lk1, diff_lq2, diff_lk2, diff_g_subln, mla_g_ckv, mla_w_uk, mla_w_uv, sgu_g, sgu_w, sgu_b, w_ff1, w_ff2, g_final):
    f32 = jnp.float32
    m = jnp.concatenate([c_ctx[None, :], c, jnp.zeros((N_MOD_ROWS - 1 - DEC_BATCH, D_MODEL), f32)], axis=0)
    mod = _ada(m, w_ada, b_ada).reshape(DEPTH, N_MOD_ROWS, 1, 6 * D_MODEL)
    cst = _lam_consts(diff_lq1, diff_lk1, diff_lq2, diff_lk2)
    tt = _bias_tiles(na_rpb)
    cos4, sin4, cos_k, sin_k = [jnp.asarray(t) for t in _rope_tables()]
    tables = (cos4, sin4, cos4, sin4, cos_k, sin_k)

    t_last = lambda a: jnp.swapaxes(a, -1, -2)
    w_in_t = t_last(w_in)
    caches_t = (t_last(cache_na_k), t_last(cache_na_v), t_last(cache_diff_k), t_last(cache_diff_v),
                cache_mla_ckv, t_last(cache_mla_kpe))
    w_out_b, w1_b, w2_b = _bf(w_out), _bf(w_ff1), _bf(w_ff2)
    g_mix3 = g_mix.reshape(DEPTH, 1, D_MODEL)
    g_ffn3 = g_ffn.reshape(DEPTH, 1, D_MODEL)
    g_sub2 = jnp.tile(diff_g_subln, (1, 2)).reshape(DEPTH, 1, LANES)
    g_ckv3 = mla_g_ckv.reshape(DEPTH, 1, MLA_KV_RANK)
    sgu_g3 = sgu_g.reshape(DEPTH, 1, WIDTH)
    sgu_bt = sgu_b.transpose(0, 2, 1)
    g_fin2 = g_final.reshape(1, D_MODEL)

    xs = (x_prompt.reshape(N_PROMPT, D_MODEL), x_sample.reshape(N_SAMPLE, D_MODEL))
    new = ()
    for l in range(DEPTH):
        pa, pb, pc, od = _inproj(l, xs, g_mix3, mod, w_in_t, sgu_g3, sgu_w, sgu_bt)
        o_p, *new = _mix_prompt(l, pa, pb, pc, cst, g_sub2, g_ckv3, mla_w_uk, mla_w_uv, new)
        o_s = _mix_sample(l, pa, pb, pc, caches_t, tt, tables, cst, g_sub2, g_ckv3, mla_w_uk, mla_w_uv)
        ffn = functools.partial(_outffn, l, xs, o_p, o_s, od, w_out_b, g_ffn3, mod, w1_b, w2_b, g_fin2)
        if l < DEPTH - 1:
            xs = (ffn(0, TILES_PROMPT + TILES_SAMPLE),)
        else:
            xs = (ffn(0, TILES_PROMPT), ffn(TILES_PROMPT, TILES_SAMPLE))
    y_prompt = xs[0].reshape(BATCH, SEQ, D_MODEL)
    y_sample = xs[1].reshape(DEC_BATCH, DEC_SEQ, D_MODEL)
    na_k, na_v, diff_k, diff_v, mla_ckv, mla_kpe = new
    return (y_prompt, y_sample, t_last(na_k), t_last(na_v), t_last(diff_k), t_last(diff_v), mla_ckv, t_last(mla_kpe))
```

```python
import functools
import math

import numpy as np
import jax
import jax.numpy as jnp
from jax import lax
from jax.experimental import pallas as pl
from jax.experimental.pallas import tpu as pltpu

D_MODEL = 1024
BATCH = 16
SEQ = 256
DEPTH = 4
DEC_BATCH = 2
DEC_SEQ = 1024
PAST_LEN = 512
GRID_W = 64
GRID_ROWS = DEC_SEQ // GRID_W
HEAD_DIM = 64
NA_HEADS = 4
NA_WIN_ROWS = 8
NA_WIN_COLS = 16
DIFF_HEADS = 4
DIFF_QK_DIM = 32
DIFF_V_DIM = 64
MLA_HEADS = 4
MLA_NOPE = 64
MLA_ROPE = 32
MLA_V = 64
MLA_KV_RANK = 128
SGU_GROUPS = 4
SGU_GROUP_DIM = 64
SGU_CHUNK = 128
D_FF = 4 * D_MODEL
ROPE_BASE = 10000.0
EPS = 1e-6
NEG_INF = -1e30
LOG2E = 1.4426950408889634

N_HEADS = 4
N_PAIRS = N_HEADS // 2
LANES = 128
WIDTH = 256
N_PROMPT = BATCH * SEQ
N_SAMPLE = DEC_BATCH * DEC_SEQ
N_TOK = N_PROMPT + N_SAMPLE
N_MOD_ROWS = 8

SEG_A = 3 * WIDTH
SEG_B = 3 * WIDTH
SEG_C = 640
SEG_D = 2 * WIDTH
SEG_C_PAD = 96
IN_COLS_P = SEG_A + SEG_B + SEG_C + SEG_D
O_ATT = 3 * WIDTH

TM = 512
TILES_PROMPT = N_PROMPT // TM
TILES_SAMPLE = N_SAMPLE // TM
PB = 2
QB = 512
QB_EXACT = 64
VMEM_LIMIT = 56 * 1024 * 1024


def _bf(x):
    return x.astype(jnp.bfloat16)


def _dot(a, b):
    return jnp.dot(a, b, preferred_element_type=jnp.float32)


def _dot_nt(a, b):
    return lax.dot_general(a, b, (((1,), (1,)), ((), ())), preferred_element_type=jnp.float32)


def _rms(x, g):
    ms = jnp.mean(x * x, axis=-1, keepdims=True)
    return x * lax.rsqrt(ms + EPS) * g


def _lane_range(lo, hi, width=LANES):
    lane = lax.broadcasted_iota(jnp.int32, (1, width), 1)
    return (lane >= lo) & (lane < hi)


def _with_ones(v):
    return jnp.concatenate([v, jnp.ones((v.shape[0], LANES), jnp.bfloat16)], axis=1)


def _attend(scores, values, log=None, bound=None):
    if bound is None:
        m = functools.reduce(jnp.maximum, [jnp.max(s, axis=-1, keepdims=True) for s in scores])
    else:
        m = bound
    o = functools.reduce(lambda a, b: a + b, [_dot(_bf(jnp.exp2(s - m)), v) for s, v in zip(scores, values)])
    if bound is not None:
        den = jnp.min(o[:, LANES:2 * LANES], axis=0, keepdims=True)
        log.least = jnp.minimum(log.least, jnp.min(den, axis=1, keepdims=True))
    return o


BOUND_SLACK = 1.02
MIN_DENOMINATOR = 2.0 ** -88


class _ShiftLog:
    def __init__(self):
        self.least = jnp.full((1, 1), jnp.inf, jnp.float32)

    def unsafe(self):
        return jnp.logical_not(self.least[0, 0] >= MIN_DENOMINATOR)


def _group_matrix(width, n_groups, extra=None):
    i = lax.broadcasted_iota(jnp.int32, (width, LANES), 0)
    j = lax.broadcasted_iota(jnp.int32, (width, LANES), 1)
    size = LANES // n_groups
    hit = (i // size == j // size) & (i < LANES)
    for t in range(n_groups if extra else 0):
        lo, hi = extra(t)
        hit = hit | ((i >= lo) & (i < hi) & (j // size == t))
    return jnp.where(hit, 1.0, 0.0).astype(jnp.bfloat16)


def _squares(x):
    xf = x.astype(jnp.float32)
    return _bf(xf * xf)


def _key_bound(keys, groups):
    return functools.reduce(jnp.maximum, [jnp.max(_dot(_squares(k), groups), axis=0, keepdims=True) for k in keys])


def _bound(log, q_parts, k2max, n_groups, extra=0.0):
    if log is None:
        return None
    size = LANES // n_groups
    q_norm = jnp.sqrt(functools.reduce(lambda a, b: a + b,
                                       [jnp.sum(q * q, axis=-1, keepdims=True) for q in q_parts])) * BOUND_SLACK
    return jnp.concatenate([q_norm * jnp.sqrt(k2max[:, size * t:size * t + 1]) + extra for t in range(n_groups)], axis=0)


def _loop(n, log, body):
    if log is not None:
        for i in range(n):
            body(i, log)
    else:
        lax.fori_loop(0, n, lambda i, carry: body(i, None) or carry, 0)


def _aligned(start, multiple):
    return start if isinstance(start, int) else pl.multiple_of(start, multiple)


def _normalised(o_ext):
    return o_ext[:, 0:LANES] * (1.0 / o_ext[:, LANES:2 * LANES])


def _swap8(x):
    lane = lax.broadcasted_iota(jnp.int32, (1, LANES), 1)
    return jnp.where((lane & 15) < 8, pltpu.roll(x, LANES - 8, 1), pltpu.roll(x, 8, 1))


def _rope(x, cos, sin):
    outs = []
    for c in range(x.shape[1] // LANES):
        sl = slice(LANES * c, LANES * (c + 1))
        xc = x[:, sl]
        outs.append(xc * cos[:, sl] + _swap8(xc) * sin[:, sl])
    return outs[0] if len(outs) == 1 else jnp.concatenate(outs, axis=1)


def _tile4(x):
    return x + pltpu.roll(x, 32, 1) + pltpu.roll(x, 64, 1) + pltpu.roll(x, 96, 1)


def _params(*sem):
    return pltpu.CompilerParams(dimension_semantics=sem, vmem_limit_bytes=VMEM_LIMIT)


ADA_TN = 1536


def _ada_kernel(m_ref, w_ref, b_ref, o_ref):
    m = m_ref[...]
    s = m * jax.nn.sigmoid(m)
    o_ref[0] = _dot(_bf(s), _bf(w_ref[0])) + b_ref[0]


def _ada(m, w_ada, b_ada):
    n = 6 * D_MODEL
    return pl.pallas_call(
        _ada_kernel,
        grid=(DEPTH, n // ADA_TN),
        in_specs=[
            pl.BlockSpec((N_MOD_ROWS, D_MODEL), lambda l, j: (0, 0)),
            pl.BlockSpec((1, D_MODEL, ADA_TN), lambda l, j: (l, 0, j)),
            pl.BlockSpec((1, 1, ADA_TN), lambda l, j: (l, 0, j)),
        ],
        out_specs=pl.BlockSpec((1, N_MOD_ROWS, ADA_TN), lambda l, j: (l, 0, j)),
        out_shape=jax.ShapeDtypeStruct((DEPTH, N_MOD_ROWS, n), jnp.float32),
        compiler_params=_params("parallel", "parallel"),
        name="ada",
    )(m, w_ada, b_ada.reshape(DEPTH, 1, n))


def _lam_kernel(lq1_ref, lk1_ref, lq2_ref, lk2_ref, init_ref, o_ref):
    init = init_ref[...]
    a = jnp.exp(jnp.sum(lq1_ref[...] * lk1_ref[...], axis=-1, keepdims=True))
    b = jnp.exp(jnp.sum(lq2_ref[...] * lk2_ref[...], axis=-1, keepdims=True))
    lam = a - b + init
    post = 1.0 - init
    for l in range(DEPTH):
        o_ref[l, 0:1, :] = jnp.broadcast_to(lam[l:l + 1], (1, LANES))
        o_ref[l, 1:2, :] = jnp.broadcast_to(post[l:l + 1], (1, LANES))


def _lam_consts(lq1, lk1, lq2, lk2):
    init = np.array([[0.8 - 0.6 * math.exp(-0.3 * l)] for l in range(DEPTH)], np.float32)
    return pl.pallas_call(
        _lam_kernel,
        out_shape=jax.ShapeDtypeStruct((DEPTH, 2, LANES), jnp.float32),
        name="diff_lambda",
    )(lq1, lk1, lq2, lk2, jnp.asarray(init))


N_DROW = 2 * NA_WIN_ROWS - 1
N_DCOL = 2 * NA_WIN_COLS - 1


def _bias_kernel(rpb_ref, o_ref):
    l = pl.program_id(0)
    h = pl.program_id(1)
    base = (l * NA_HEADS + h) * (N_DROW * N_DCOL)
    cq = lax.broadcasted_iota(jnp.int32, (GRID_W, LANES), 0)
    lane = lax.broadcasted_iota(jnp.int32, (GRID_W, LANES), 1)
    ck = lane & (GRID_W - 1)
    dcol = jnp.clip(ck - cq, -(NA_WIN_COLS - 1), NA_WIN_COLS - 1) + (NA_WIN_COLS - 1)
    hi = lane >= GRID_W
    for a in range(N_DROW - 1):
        acc = jnp.zeros((GRID_W, LANES), jnp.float32)
        for j in range(N_DCOL):
            lo_v = rpb_ref[base + a * N_DCOL + j]
            hi_v = rpb_ref[base + (a + 1) * N_DCOL + j]
            acc = jnp.where(dcol == j, jnp.where(hi, hi_v, lo_v), acc)
        o_ref[0, 0, a] = acc * LOG2E


def _bias_tiles(na_rpb):
    return pl.pallas_call(
        _bias_kernel,
        grid=(DEPTH, NA_HEADS),
        in_specs=[pl.BlockSpec(memory_space=pltpu.SMEM)],
        out_specs=pl.BlockSpec((1, 1, N_DROW - 1, GRID_W, LANES), lambda l, h: (l, h, 0, 0, 0)),
        out_shape=jax.ShapeDtypeStruct((DEPTH, NA_HEADS, N_DROW - 1, GRID_W, LANES), jnp.float32),
        compiler_params=_params("parallel", "parallel"),
        name="na_bias_tiles",
    )(na_rpb.reshape(-1))


def _row_group(i):
    return jnp.where(i < TILES_PROMPT, 0, 1 + (i - TILES_PROMPT) // (DEC_SEQ // TM))


def _split_specs(width, first):
    return [pl.BlockSpec((TM, width), lambda i: (jnp.minimum(first + i, TILES_PROMPT - 1), 0)),
            pl.BlockSpec((TM, width), lambda i: (jnp.maximum(first + i - TILES_PROMPT, 0), 0))]


def _x_specs(split, first):
    if not split:
        return [pl.BlockSpec((TM, D_MODEL), lambda i: (first + i, 0))]
    return _split_specs(D_MODEL, first)


def _read_tile(refs, first):
    if len(refs) == 1:
        return refs[0][...]
    return jnp.where(first + pl.program_id(0) < TILES_PROMPT, refs[0][...], refs[1][...])


IN_COLS = 2592
IN_QC, IN_CKV, IN_D = 1536, 1920, 2080
TR_ROWS = 256


def _gelu_tanh(x):
    return 0.5 * x * (1.0 + jnp.tanh(math.sqrt(2.0 / math.pi) * (x + 0.044715 * (x * x * x))))


def _sgu(pd, g, w_ref, bt):
    u = _gelu_tanh(pd[:, 0:WIDTH])
    v = _gelu_tanh(pd[:, WIDTH:2 * WIDTH])
    grp = lax.broadcasted_iota(jnp.int32, (1, WIDTH), 1) // SGU_GROUP_DIM
    v2 = v * v
    ms = jnp.zeros_like(v)
    for gi in range(SGU_GROUPS):
        sel = grp == gi
        tot = jnp.sum(jnp.where(sel, v2, 0.0), axis=-1, keepdims=True)
        ms = jnp.where(sel, tot * (1.0 / SGU_GROUP_DIM), ms)
    vg = _bf(v * lax.rsqrt(ms + EPS) * g)
    outs = []
    for c in range(pd.shape[0] // SGU_CHUNK):
        rows = slice(SGU_CHUNK * c, SGU_CHUNK * (c + 1))
        mixed = jnp.zeros((SGU_CHUNK, WIDTH), jnp.float32)
        for gi in range(SGU_GROUPS):
            full = _dot(_bf(w_ref[0, gi]), vg[rows]) + bt[:, gi:gi + 1]
            mixed = jnp.where(grp == gi, full, mixed)
        outs.append(u[rows] * mixed)
    return jnp.concatenate(outs, axis=0)


def _w_in_row_pieces():
    qn = [(IN_QC + 96 * h, MLA_NOPE) for h in range(MLA_HEADS)]
    qp = [(IN_QC + 96 * h + MLA_NOPE, MLA_ROPE) for h in range(MLA_HEADS)]
    seg_c = qn + qp + [(IN_CKV, MLA_KV_RANK + MLA_ROPE)]
    return (0, SEG_A + SEG_B), seg_c, (IN_D, SEG_D)


def _load_w_in(wt_ref, w_scr):
    ab, seg_c, d = _w_in_row_pieces()
    c_rows = jnp.concatenate([wt_ref[0, s:s + n, :] for s, n in seg_c]
                             + [jnp.zeros((SEG_C_PAD, D_MODEL), jnp.float32)], axis=0)
    for t in range(SEG_C // LANES):
        w_scr[:, SEG_A + SEG_B + LANES * t:SEG_A + SEG_B + LANES * (t + 1)] = _bf(c_rows[LANES * t:LANES * (t + 1)].T)
    for (src, n), dst in ((ab, 0), (d, SEG_A + SEG_B + SEG_C)):
        for t in range(n // TR_ROWS):
            rows = wt_ref[0, src + TR_ROWS * t:src + TR_ROWS * (t + 1), :]
            w_scr[:, dst + TR_ROWS * t:dst + TR_ROWS * (t + 1)] = _bf(rows.T)


def _inproj_kernel(n_x, *refs):
    x_refs = refs[:n_x]
    (g_ref, sh_ref, sc_ref, wt_ref, sg_ref, sw_ref, sbt_ref, pa_ref, pb_ref, pc_ref, od_ref, w_scr) = refs[n_x:]

    @pl.when(pl.program_id(0) == 0)
    def _():
        _load_w_in(wt_ref, w_scr)

    h = _rms(_read_tile(x_refs, 0), g_ref[0]) * (1.0 + sc_ref[...]) + sh_ref[...]
    hb = _bf(h)
    off = 0
    for ref in (pa_ref, pb_ref, pc_ref):
        n = ref.shape[1]
        ref[...] = _dot(hb, w_scr[:, off:off + n])
        off += n
    od_ref[...] = _sgu(_dot(hb, w_scr[:, off:off + SEG_D]), sg_ref[0], sw_ref, sbt_ref[0])


def _inproj(l, xs, g_mix, mod, w_in_t, sgu_g, sgu_w, sgu_bt):
    def mod_spec(j):
        return pl.BlockSpec((None, None, 1, D_MODEL), lambda i: (l, _row_group(i), 0, j))

    widths = (SEG_A, SEG_B, SEG_C, WIDTH)
    return pl.pallas_call(
        functools.partial(_inproj_kernel, len(xs)),
        grid=(N_TOK // TM,),
        in_specs=_x_specs(len(xs) == 2, 0) + [
            pl.BlockSpec((1, 1, D_MODEL), lambda i: (l, 0, 0)),
            mod_spec(0), mod_spec(1),
            pl.BlockSpec((1, IN_COLS, D_MODEL), lambda i: (l, 0, 0), pipeline_mode=pl.Buffered(1)),
            pl.BlockSpec((1, 1, WIDTH), lambda i: (l, 0, 0)),
            pl.BlockSpec((1, SGU_GROUPS, SGU_CHUNK, SGU_CHUNK), lambda i: (l, 0, 0, 0)),
            pl.BlockSpec((1, SGU_CHUNK, SGU_GROUPS), lambda i: (l, 0, 0)),
        ],
        out_specs=[pl.BlockSpec((TM, n), lambda i: (i, 0)) for n in widths],
        out_shape=[jax.ShapeDtypeStruct((N_TOK, n), jnp.float32) for n in widths],
        scratch_shapes=[pltpu.VMEM((D_MODEL, IN_COLS_P), jnp.bfloat16)],
        compiler_params=_params("arbitrary"),
        name="inproj",
    )(*xs, g_mix, mod, mod, w_in_t, sgu_g, sgu_w, sgu_bt)


C_QN, C_QP, C_CKV, C_KPE = 0, 256, 384, 512


def _stack_heads(qp):
    lo = _lane_range(0, 64)
    return jnp.concatenate([_bf(jnp.where(lo, qp, 0.0)), _bf(jnp.where(lo, 0.0, qp))], axis=0)


def _unstack_heads(o, n):
    return jnp.where(_lane_range(0, 64), o[0:n], o[n:2 * n])


def _pair_t(c_ref):
    return jnp.concatenate([c_ref[0], c_ref[1]], axis=0)


def _stack_components(qp):
    return jnp.concatenate([_bf(jnp.where(_lane_range(32 * t, 32 * (t + 1)), qp, 0.0)) for t in range(4)], axis=0)


def _group_mean_sq(x, groups, size):
    sq = x * x
    hi = _bf(sq)
    rest = sq - hi.astype(jnp.float32)
    mid = _bf(rest)
    lo = _bf(rest - mid.astype(jnp.float32))
    return (_dot(hi, groups) + _dot(mid, groups) + _dot(lo, groups)) * (1.0 / size)


def _diff_finish(o, n, lam, post, g2, by_head):
    den = o[:, LANES:2 * LANES]
    outs = []
    for t in range(2):
        p1 = o[2 * t * n:(2 * t + 1) * n, 0:LANES] * (1.0 / den[2 * t * n:(2 * t + 1) * n])
        p2 = o[(2 * t + 1) * n:(2 * t + 2) * n, 0:LANES] * (lam / den[(2 * t + 1) * n:(2 * t + 2) * n])
        outs.append(p1 - p2)
    d = jnp.where(_lane_range(0, 64), outs[0], outs[1])
    return d * lax.rsqrt(_group_mean_sq(d, by_head, DIFF_V_DIM) + EPS) * g2 * post


def _mla_groups(j):
    return _group_matrix(2 * LANES, 2,
                         lambda t: (LANES + MLA_ROPE * (2 * j + t), LANES + MLA_ROPE * (2 * j + t + 1)))


def _mla_queries(qn_pair, qp_all, j):
    halves = []
    for t in range(2):
        h = 2 * j + t
        halves.append(jnp.concatenate([
            _bf(jnp.where(_lane_range(64 * t, 64 * (t + 1)), qn_pair, 0.0)),
            _bf(jnp.where(_lane_range(MLA_ROPE * h, MLA_ROPE * (h + 1)), qp_all, 0.0))], axis=1))
    return jnp.concatenate(halves, axis=0)


def _write_heads_t(p_ref, rows, col0, out_ref, bb):
    xt = p_ref[rows, col0:col0 + WIDTH].T
    for h in range(N_HEADS):
        out_ref[bb, 0, h] = xt[64 * h:64 * (h + 1)]
    _clear_other_layers(out_ref, bb)


def _clear_other_layers(out_ref, bb):
    if out_ref.shape[1] > 1:
        out_ref[bb, 1:] = jnp.zeros(out_ref.shape[1:], jnp.float32)[1:]


def _mix_prompt_kernel(n_prev, *refs):
    ins, outs = refs[:8], refs[8 + n_prev:]
    log = _ShiftLog()
    _mix_prompt_pass(ins, outs, log)

    @pl.when(log.unsafe())
    def _():
        _mix_prompt_pass(ins, outs, None)


def _mix_prompt_pass(ins, outs, log):
    pa_ref, pb_ref, pc_ref, cst_ref, gsub_ref, gckv_ref, wuk_ref, wuv_ref = ins
    o_ref, nak_ref, nav_ref, dk_ref, dv_ref, ckv_ref, kpe_ref = outs
    first_pass = log is not None
    c_a = HEAD_DIM ** -0.5 * LOG2E
    c_b = DIFF_QK_DIM ** -0.5 * LOG2E
    c_c = (MLA_NOPE + MLA_ROPE) ** -0.5 * LOG2E
    lam = cst_ref[0, 0:1, 0:1]
    post = cst_ref[0, 1:2, 0:1]
    wuk, wuv = _bf(wuk_ref[0]), _bf(wuv_ref[0])
    by_head, by_comp = _group_matrix(LANES, 2), _group_matrix(LANES, 4)

    def sequence(bb, log):
        rows = pl.ds(_aligned(bb * SEQ, SEQ), SEQ)
        for j in range(N_PAIRS):
            cols = slice(LANES * j, LANES * (j + 1))
            k = _bf(pa_ref[rows, WIDTH + LANES * j:WIDTH + LANES * (j + 1)])
            v = _with_ones(_bf(pa_ref[rows, 2 * WIDTH + LANES * j:2 * WIDTH + LANES * (j + 1)]))
            q = pa_ref[rows, cols] * c_a
            o = _attend([_dot_nt(_stack_heads(q), k)], [v], log, _bound(log, [q], _key_bound([k], by_head), 2))
            o_ref[rows, cols] = _unstack_heads(_normalised(o), SEQ)
        for j in range(N_PAIRS):
            cols = slice(LANES * j, LANES * (j + 1))
            k = _bf(pb_ref[rows, WIDTH + LANES * j:WIDTH + LANES * (j + 1)])
            v = _with_ones(_bf(pb_ref[rows, 2 * WIDTH + LANES * j:2 * WIDTH + LANES * (j + 1)]))
            q = pb_ref[rows, cols] * c_b
            o = _attend([_dot_nt(_stack_components(q), k)], [v], log, _bound(log, [q], _key_bound([k], by_comp), 4))
            o_ref[rows, WIDTH + LANES * j:WIDTH + LANES * (j + 1)] = _diff_finish(o, SEQ, lam, post, gsub_ref[0], by_head)
        ckv = _rms(pc_ref[rows, C_CKV:C_CKV + MLA_KV_RANK], gckv_ref[0])
        kpe_slot = pc_ref[rows, C_KPE:C_KPE + LANES]
        if first_pass:
            ckv_ref[bb, 0] = ckv
            _clear_other_layers(ckv_ref, bb)
            kpe_ref[bb, 0] = kpe_slot.T[0:MLA_ROPE]
            _clear_other_layers(kpe_ref, bb)
        ckv_b = _bf(ckv)
        kn = _bf(_dot(ckv_b, wuk))
        vv = _bf(_dot(ckv_b, wuv))
        kpe4 = _bf(_tile4(kpe_slot))
        qn = pc_ref[rows, C_QN:C_QN + WIDTH] * c_c
        qp = pc_ref[rows, C_QP:C_QP + LANES] * c_c
        for j in range(N_PAIRS):
            cols = slice(LANES * j, LANES * (j + 1))
            k = jnp.concatenate([kn[:, cols], kpe4], axis=1)
            qs = _mla_queries(qn[:, cols], qp, j)
            groups = _mla_groups(j)
            o = _attend([_dot_nt(qs, k)], [_with_ones(vv[:, cols])], log,
                        _bound(log, [qn[:, cols], qp], _key_bound([k], groups), 2))
            o_ref[rows, 2 * WIDTH + LANES * j:2 * WIDTH + LANES * (j + 1)] = _unstack_heads(_normalised(o), SEQ)
        if first_pass:
            _write_heads_t(pa_ref, rows, WIDTH, nak_ref, bb)
            _write_heads_t(pa_ref, rows, 2 * WIDTH, nav_ref, bb)
            _write_heads_t(pb_ref, rows, WIDTH, dk_ref, bb)
            _write_heads_t(pb_ref, rows, 2 * WIDTH, dv_ref, bb)

    _loop(PB, log, sequence)


def _mix_prompt(l, pa, pb, pc, cst, g_sub2, g_ckv, w_uk, w_uv, prev):
    n_prev = len(prev)
    tails = [(NA_HEADS, HEAD_DIM, SEQ)] * 2 + [(DIFF_HEADS, 64, SEQ)] * 2 + [(SEQ, MLA_KV_RANK), (MLA_ROPE, SEQ)]

    def cache_spec(tail):
        if l == 0:
            return pl.BlockSpec((PB, DEPTH) + tail, lambda b: (b, 0) + (0,) * len(tail))
        return pl.BlockSpec((PB, 1) + tail, lambda b: (b, l) + (0,) * len(tail))

    def rows(width):
        return pl.BlockSpec((PB * SEQ, width), lambda b: (b, 0))

    def layer(*tail):
        return pl.BlockSpec((1,) + tail, lambda b: (l,) + (0,) * len(tail))

    return pl.pallas_call(
        functools.partial(_mix_prompt_kernel, n_prev),
        grid=(BATCH // PB,),
        in_specs=[rows(SEG_A), rows(SEG_B), rows(SEG_C), layer(2, LANES), layer(1, LANES), layer(1, MLA_KV_RANK),
                  layer(MLA_KV_RANK, WIDTH), layer(MLA_KV_RANK, WIDTH)] + [pl.BlockSpec(memory_space=pl.ANY)] * n_prev,
        out_specs=[rows(O_ATT)] + [cache_spec(t) for t in tails],
        out_shape=[jax.ShapeDtypeStruct((N_PROMPT, O_ATT), jnp.float32)]
        + [jax.ShapeDtypeStruct((BATCH, DEPTH) + t, jnp.float32) for t in tails],
        input_output_aliases={8 + i: 1 + i for i in range(n_prev)},
        compiler_params=_params("parallel"),
        name="mix_prompt",
    )(pa, pb, pc, cst, g_sub2, g_ckv, w_uk, w_uv, *prev)


def _na_row_groups():
    kh = min(NA_WIN_ROWS, GRID_ROWS)
    r0s = [min(max(r - kh // 2, 0), GRID_ROWS - kh) for r in range(GRID_ROWS)]
    groups = []
    for r, r0 in enumerate(r0s):
        if groups and groups[-1][2] == r0:
            groups[-1][1] = r
        else:
            groups.append([r, r, r0])
    return kh, [tuple(g) for g in groups]


def _na_sample(q_ref, k_ref, v_ref, ck_ref, cv_ref, tt_ref, o_ref, log):
    c = HEAD_DIM ** -0.5 * LOG2E
    kh, groups = _na_row_groups()
    lk = kh * GRID_W
    edge = [g for g in groups if g[1] > g[0]]
    inner = [g for g in groups if g[1] == g[0]]
    depth = inner[0][0] - inner[0][2]
    assert all(g[0] - g[2] == depth for g in inner) and [g[0] for g in inner] == list(range(inner[0][0], inner[-1][0] + 1))

    def in_window(n):
        cq = lax.broadcasted_iota(jnp.int32, (n, lk), 0) & (GRID_W - 1)
        ck = lax.broadcasted_iota(jnp.int32, (n, lk), 1) & (GRID_W - 1)
        c0 = jnp.clip(cq - NA_WIN_COLS // 2, 0, GRID_W - NA_WIN_COLS)
        return (ck >= c0) & (ck < c0 + NA_WIN_COLS)

    kc_t = _bf(_pair_t(ck_ref))
    vc = _with_ones(_bf(_pair_t(cv_ref).T))
    by_head = _group_matrix(LANES, 2)
    k2max = _key_bound([_bf(k_ref[...]), _bf(_pair_t(ck_ref).T)], by_head)
    tmax = functools.reduce(jnp.maximum, [tt_ref[t, a] for t in range(2) for a in range(N_DROW - 1)])
    bplus = jnp.maximum(jnp.max(jnp.max(tmax, axis=-1, keepdims=True), axis=0, keepdims=True), 0.0)

    def group(row0, key0, offsets, log):
        n = len(offsets) * GRID_W
        rows, keys = pl.ds(row0, n), pl.ds(key0, lk)
        q = q_ref[rows, :] * c
        qg = _stack_heads(q)
        k = _bf(k_ref[keys, :])
        v = _with_ones(_bf(v_ref[keys, :]))
        bias = jnp.concatenate([
            jnp.concatenate([tt_ref[t, 2 * i - off + NA_WIN_ROWS - 1] for i in range(kh // 2)], axis=1)
            for t in range(2) for off in offsets], axis=0)
        s_loc = jnp.where(in_window(2 * n), _dot_nt(qg, k) + bias, NEG_INF)
        o = _attend([_dot(qg, kc_t), s_loc], [vc, v], log, _bound(log, [q], k2max, 2, bplus))
        o_ref[rows, :] = _unstack_heads(_normalised(o), n)

    if log is None:
        def any_row(r, log):
            r0 = jnp.clip(r - kh // 2, 0, GRID_ROWS - kh)
            group(_aligned(r * GRID_W, GRID_W), _aligned(r0 * GRID_W, GRID_W), [r - r0], log)

        _loop(GRID_ROWS, log, any_row)
        return

    for (r_lo, r_hi, r0) in edge:
        group(r_lo * GRID_W, r0 * GRID_W, [r - r0 for r in range(r_lo, r_hi + 1)], log)

    def inner_row(i, log):
        r = inner[0][0] + i
        group(r * GRID_W, (r - depth) * GRID_W, [depth], log)

    _loop(len(inner), log, inner_row)


def _diff_sample(q_ref, k_ref, v_ref, ck_ref, cv_ref, cos_ref, sin_ref, cst_ref, g_ref, o_ref, log):
    c = DIFF_QK_DIM ** -0.5 * LOG2E
    lam = cst_ref[0, 0:1, 0:1]
    post = cst_ref[0, 1:2, 0:1]
    k_new = _bf(_rope(k_ref[...], cos_ref[...], sin_ref[...]))
    kc_t = _bf(_pair_t(ck_ref))
    vc = _with_ones(_bf(_pair_t(cv_ref).T))
    v = _with_ones(_bf(v_ref[...]))
    by_head, by_comp = _group_matrix(LANES, 2), _group_matrix(LANES, 4)
    k2max = _key_bound([k_new, _bf(_pair_t(ck_ref).T)], by_comp)

    qb = QB_EXACT if log is None else QB

    def block(qi, log):
        rows = pl.ds(_aligned(qi * qb, qb), qb)
        q = _rope(q_ref[rows, :], cos_ref[rows, :], sin_ref[rows, :]) * c
        qs = _stack_components(q)
        o = _attend([_dot(qs, kc_t), _dot_nt(qs, k_new)], [vc, v], log, _bound(log, [q], k2max, 4))
        o_ref[rows, :] = _diff_finish(o, qb, lam, post, g_ref[0], by_head)

    _loop(DEC_SEQ // qb, log, block)


def _mla_sample(j, qn_ref, qp_ref, ckv_ref, kpe_ref, cckv_ref, ckpe_ref, cosq_ref, sinq_ref, cosk_ref, sink_ref,
                gckv_ref, wuk_ref, wuv_ref, o_ref, log):
    c = (MLA_NOPE + MLA_ROPE) ** -0.5 * LOG2E
    wuk, wuv = _bf(wuk_ref[0]), _bf(wuv_ref[0])
    ckv_new = _bf(_rms(ckv_ref[...], gckv_ref[0]))
    ckv_old = _bf(cckv_ref[...])
    kpe_new = _bf(_tile4(_rope(kpe_ref[...], cosk_ref[...], sink_ref[...])))
    kpe_old = _bf(jnp.concatenate([ckpe_ref[...]] * MLA_HEADS, axis=0).T)
    k_old = jnp.concatenate([_bf(_dot(ckv_old, wuk)), kpe_old], axis=1)
    k_new = jnp.concatenate([_bf(_dot(ckv_new, wuk)), kpe_new], axis=1)
    vo, vn = _with_ones(_bf(_dot(ckv_old, wuv))), _with_ones(_bf(_dot(ckv_new, wuv)))
    groups = _mla_groups(j)
    k2max = _key_bound([k_old, k_new], groups)

    qb = QB_EXACT if log is None else QB

    def block(qi, log):
        rows = pl.ds(_aligned(qi * qb, qb), qb)
        qp = _rope(qp_ref[rows, :], cosq_ref[rows, :], sinq_ref[rows, :]) * c
        qn = qn_ref[rows, :] * c
        qs = _mla_queries(qn, qp, j)
        o = _attend([_dot_nt(qs, k_old), _dot_nt(qs, k_new)], [vo, vn], log, _bound(log, [qn, qp], k2max, 2))
        o_ref[rows, :] = _unstack_heads(_normalised(o), qb)

    _loop(DEC_SEQ // qb, log, block)


def _mix_sample_kernel(qa_ref, ka_ref, va_ref, qb_ref, kb_ref, vb_ref, qn_ref, qp_ref, ckv_ref, kpe_ref,
                       cnak_ref, cnav_ref, cdk_ref, cdv_ref, cckv_ref, ckpe_ref, tt_ref,
                       cosb_ref, sinb_ref, cosq_ref, sinq_ref, cosk_ref, sink_ref,
                       cst_ref, gsub_ref, gckv_ref, wuk_ref, wuv_ref, oa_ref, ob_ref, oc_ref):
    j = pl.program_id(1)

    def run(log):
        _na_sample(qa_ref, ka_ref, va_ref, cnak_ref, cnav_ref, tt_ref, oa_ref, log)
        _diff_sample(qb_ref, kb_ref, vb_ref, cdk_ref, cdv_ref, cosb_ref, sinb_ref, cst_ref, gsub_ref, ob_ref, log)
        _mla_sample(j, qn_ref, qp_ref, ckv_ref, kpe_ref, cckv_ref, ckpe_ref, cosq_ref, sinq_ref, cosk_ref, sink_ref,
                    gckv_ref, wuk_ref, wuv_ref, oc_ref, log)

    log = _ShiftLog()
    run(log)

    @pl.when(log.unsafe())
    def _():
        run(None)


def _mix_sample(l, pa, pb, pc, caches_t, tt, tables, cst, g_sub2, g_ckv, w_uk, w_uv):
    first = N_PROMPT // DEC_SEQ

    def cols(block):
        return pl.BlockSpec((DEC_SEQ, LANES), lambda b, j: (first + b, block(j)))

    def cache(*tail, pair=False):
        return pl.BlockSpec((None, None) + tail, lambda b, j: (b, l, j if pair else 0) + (0,) * (len(tail) - 1))

    def layer(*tail):
        return pl.BlockSpec((1,) + tail, lambda b, j: (l,) + (0,) * len(tail))

    table = pl.BlockSpec((DEC_SEQ, LANES), lambda b, j: (0, 0), pipeline_mode=pl.Buffered(1))
    qkv = [cols(lambda j: j), cols(lambda j: N_PAIRS + j), cols(lambda j: 2 * N_PAIRS + j)]
    seg_c = [cols(lambda j: j), cols(lambda j: C_QP // LANES), cols(lambda j: C_CKV // LANES), cols(lambda j: C_KPE // LANES)]
    kv_t = cache(2, 64, PAST_LEN, pair=True)
    w_pair = pl.BlockSpec((1, MLA_KV_RANK, LANES), lambda b, j: (l, 0, j))
    out = pl.BlockSpec((DEC_SEQ, LANES), lambda b, j: (b, j))
    return pl.pallas_call(
        _mix_sample_kernel,
        grid=(DEC_BATCH, N_PAIRS),
        in_specs=qkv + qkv + seg_c + [kv_t, kv_t, kv_t, kv_t, cache(PAST_LEN, MLA_KV_RANK), cache(MLA_ROPE, PAST_LEN),
                                      pl.BlockSpec((None, 2, N_DROW - 1, GRID_W, LANES), lambda b, j: (l, j, 0, 0, 0)),
                                      table, table, table, table, table, table,
                                      layer(2, LANES), layer(1, LANES), layer(1, MLA_KV_RANK), w_pair, w_pair],
        out_specs=[out, out, out],
        out_shape=[jax.ShapeDtypeStruct((N_SAMPLE, WIDTH), jnp.float32)] * 3,
        compiler_params=_params("parallel", "arbitrary"),
        name="mix_sample",
    )(pa, pa, pa, pb, pb, pb, pc, pc, pc, pc, *caches_t, tt, *tables, cst, g_sub2, g_ckv, w_uk, w_uv)


FF_CHUNK = 1024


def _outffn_kernel(n_x, first, final, *refs):
    x_refs, op_ref, os_refs = refs[:n_x], refs[n_x], refs[n_x + 1:n_x + 4]
    (od_ref, wout_ref, g1_ref, gffn_ref, sh2_ref, sc2_ref, g2_ref, w1_ref, w2_ref, gfin_ref, y_ref) = refs[n_x + 4:]
    o_att = jnp.where(first + pl.program_id(0) < TILES_PROMPT, op_ref[...],
                      jnp.concatenate([r[...] for r in os_refs], axis=1))
    acc = (_dot(_bf(o_att), wout_ref[0, 0:O_ATT, :])
           + _dot(_bf(od_ref[...]), wout_ref[0, O_ATT:O_ATT + WIDTH, :]))
    x1 = _read_tile(x_refs, first) + g1_ref[...] * acc
    hf = _bf(_rms(x1, gffn_ref[0]) * (1.0 + sc2_ref[...]) + sh2_ref[...])
    acc = jnp.zeros((TM, D_MODEL), jnp.float32)
    for c in range(D_FF // FF_CHUNK):
        cols = slice(FF_CHUNK * c, FF_CHUNK * (c + 1))
        a = jnp.square(jnp.maximum(_dot(hf, w1_ref[0, :, cols]), 0.0))
        acc += _dot(_bf(a), w2_ref[0, cols, :])
    y = x1 + g2_ref[...] * acc
    if final:
        y = _rms(y, gfin_ref[...])
    y_ref[...] = y


def _outffn(l, xs, o_p, o_s, od, w_out, g_ffn, mod, w1, w2, g_final, first, n_tiles):
    def mod_spec(j):
        return pl.BlockSpec((None, None, 1, D_MODEL), lambda i: (l, _row_group(first + i), 0, j))

    def resident(shape):
        return pl.BlockSpec(shape, lambda i: (l,) + (0,) * (len(shape) - 1), pipeline_mode=pl.Buffered(1))

    return pl.pallas_call(
        functools.partial(_outffn_kernel, len(xs), first, l == DEPTH - 1),
        grid=(n_tiles,),
        in_specs=_x_specs(len(xs) == 2, first) + _split_specs(O_ATT, first)[:1] + _split_specs(WIDTH, first)[1:] * 3 + [
            pl.BlockSpec((TM, WIDTH), lambda i: (first + i, 0)),
            resident((1, 4 * WIDTH, D_MODEL)),
            mod_spec(2),
            pl.BlockSpec((1, 1, D_MODEL), lambda i: (l, 0, 0)),
            mod_spec(3), mod_spec(4), mod_spec(5),
            resident((1, D_MODEL, D_FF)),
            resident((1, D_FF, D_MODEL)),
            pl.BlockSpec((1, D_MODEL), lambda i: (0, 0)),
        ],
        out_specs=pl.BlockSpec((TM, D_MODEL), lambda i: (i, 0)),
        out_shape=jax.ShapeDtypeStruct((n_tiles * TM, D_MODEL), jnp.float32),
        compiler_params=_params("parallel"),
        name="outffn",
    )(*xs, o_p, *o_s, od, w_out, mod, g_ffn, mod, mod, mod, w1, w2, g_final)


def _rope32_tables():
    t = np.arange(DEC_SEQ)
    rows, cols = (t // GRID_W).astype(np.float64), (t % GRID_W).astype(np.float64)
    half = 8
    freqs = ROPE_BASE ** (-np.arange(half, dtype=np.float64) / half)
    cos, sin = [], []
    for pos in (rows, cols):
        ang = pos[:, None] * freqs[None, :]
        cos += [np.cos(ang), np.cos(ang)]
        sin += [-np.sin(ang), np.sin(ang)]
    return np.concatenate(cos, axis=1).astype(np.float32), np.concatenate(sin, axis=1).astype(np.float32)


def _rope_tables():
    c32, s32 = _rope32_tables()
    tile = lambda a, n: np.tile(a, (1, n))
    pad = np.zeros((DEC_SEQ, LANES - MLA_ROPE), np.float32)
    cos_k = np.concatenate([c32, pad + 1.0], axis=1)
    sin_k = np.concatenate([s32, pad], axis=1)
    return (tile(c32, 4), tile(s32, 4),
            cos_k, sin_k)


def kernel(x_prompt, x_sample, cache_na_k, cache_na_v, cache_diff_k, cache_diff_v, cache_mla_ckv, cache_mla_kpe, c, c_ctx, w_ada, b_ada, g_mix, g_ffn, w_in, w_out, na_rpb, diff_lq1, diff_lk1, diff_lq2, diff_lk2, diff_g_subln, mla_g_ckv, mla_w_uk, mla_w_uv, sgu_g, sgu_w, sgu_b, w_ff1, w_ff2, g_final):
    f32 = jnp.float32
    m = jnp.concatenate([c_ctx[None, :], c, jnp.zeros((N_MOD_ROWS - 1 - DEC_BATCH, D_MODEL), f32)], axis=0)
    mod = _ada(m, w_ada, b_ada).reshape(DEPTH, N_MOD_ROWS, 1, 6 * D_MODEL)
    cst = _lam_consts(diff_lq1, diff_lk1, diff_lq2, diff_lk2)
    tt = _bias_tiles(na_rpb)
    cos4, sin4, cos_k, sin_k = [jnp.asarray(t) for t in _rope_tables()]
    tables = (cos4, sin4, cos4, sin4, cos_k, sin_k)

    t_last = lambda a: jnp.swapaxes(a, -1, -2)
    w_in_t = t_last(w_in)
    caches_t = (t_last(cache_na_k), t_last(cache_na_v), t_last(cache_diff_k), t_last(cache_diff_v),
                cache_mla_ckv, t_last(cache_mla_kpe))
    w_out_b, w1_b, w2_b = _bf(w_out), _bf(w_ff1), _bf(w_ff2)
    g_mix3 = g_mix.reshape(DEPTH, 1, D_MODEL)
    g_ffn3 = g_ffn.reshape(DEPTH, 1, D_MODEL)
    g_sub2 = jnp.tile(diff_g_subln, (1, 2)).reshape(DEPTH, 1, LANES)
    g_ckv3 = mla_g_ckv.reshape(DEPTH, 1, MLA_KV_RANK)
    sgu_g3 = sgu_g.reshape(DEPTH, 1, WIDTH)
    sgu_bt = sgu_b.transpose(0, 2, 1)
    g_fin2 = g_final.reshape(1, D_MODEL)

    xs = (x_prompt.reshape(N_PROMPT, D_MODEL), x_sample.reshape(N_SAMPLE, D_MODEL))
    new = ()
    for l in range(DEPTH):
        pa, pb, pc, od = _inproj(l, xs, g_mix3, mod, w_in_t, sgu_g3, sgu_w, sgu_bt)
        o_p, *new = _mix_prompt(l, pa, pb, pc, cst, g_sub2, g_ckv3, mla_w_uk, mla_w_uv, new)
        o_s = _mix_sample(l, pa, pb, pc, caches_t, tt, tables, cst, g_sub2, g_ckv3, mla_w_uk, mla_w_uv)
        ffn = functools.partial(_outffn, l, xs, o_p, o_s, od, w_out_b, g_ffn3, mod, w1_b, w2_b, g_fin2)
        if l < DEPTH - 1:
            xs = (ffn(0, TILES_PROMPT + TILES_SAMPLE),)
        else:
            xs = (ffn(0, TILES_PROMPT), ffn(TILES_PROMPT, TILES_SAMPLE))
    y_prompt = xs[0].reshape(BATCH, SEQ, D_MODEL)
    y_sample = xs[1].reshape(DEC_BATCH, DEC_SEQ, D_MODEL)
    na_k, na_v, diff_k, diff_v, mla_ckv, mla_kpe = new
    return (y_prompt, y_sample, t_last(na_k), t_last(na_v), t_last(diff_k), t_last(diff_v), mla_ckv, t_last(mla_kpe))
```

```python
import functools
import math

import numpy as np
import jax
import jax.numpy as jnp
from jax import lax
from jax.experimental import pallas as pl
from jax.experimental.pallas import tpu as pltpu

D_MODEL = 1024
BATCH = 16
SEQ = 256
DEPTH = 4
DEC_BATCH = 2
DEC_SEQ = 1024
PAST_LEN = 512
GRID_W = 64
GRID_ROWS = DEC_SEQ // GRID_W
HEAD_DIM = 64
NA_HEADS = 4
NA_WIN_ROWS = 8
NA_WIN_COLS = 16
DIFF_HEADS = 4
DIFF_QK_DIM = 32
DIFF_V_DIM = 64
MLA_HEADS = 4
MLA_NOPE = 64
MLA_ROPE = 32
MLA_V = 64
MLA_KV_RANK = 128
SGU_GROUPS = 4
SGU_GROUP_DIM = 64
SGU_CHUNK = 128
D_FF = 4 * D_MODEL
ROPE_BASE = 10000.0
EPS = 1e-6
NEG_INF = -1e30
LOG2E = 1.4426950408889634

N_HEADS = 4
N_PAIRS = N_HEADS // 2
LANES = 128
WIDTH = 256
N_PROMPT = BATCH * SEQ
N_SAMPLE = DEC_BATCH * DEC_SEQ
N_TOK = N_PROMPT + N_SAMPLE
N_MOD_ROWS = 8

SEG_A = 3 * WIDTH
SEG_B = 3 * WIDTH
SEG_C = 640
SEG_D = 2 * WIDTH
SEG_C_PAD = 96
IN_COLS_P = SEG_A + SEG_B + SEG_C + SEG_D
O_ATT = 3 * WIDTH

TM = 512
TILES_PROMPT = N_PROMPT // TM
TILES_SAMPLE = N_SAMPLE // TM
PB = 2
QB = 512
QB_EXACT = 64
VMEM_LIMIT = 56 * 1024 * 1024


def _bf(x):
    return x.astype(jnp.bfloat16)


def _dot(a, b):
    return jnp.dot(a, b, preferred_element_type=jnp.float32)


def _dot_nt(a, b):
    return lax.dot_general(a, b, (((1,), (1,)), ((), ())), preferred_element_type=jnp.float32)


def _rms(x, g):
    ms = jnp.mean(x * x, axis=-1, keepdims=True)
    return x * lax.rsqrt(ms + EPS) * g


def _lane_range(lo, hi, width=LANES):
    lane = lax.broadcasted_iota(jnp.int32, (1, width), 1)
    return (lane >= lo) & (lane < hi)


def _with_ones(v):
    return jnp.concatenate([v, jnp.ones((v.shape[0], LANES), jnp.bfloat16)], axis=1)


def _attend(scores, values, log=None, bound=None):
    if bound is None:
        m = functools.reduce(jnp.maximum, [jnp.max(s, axis=-1, keepdims=True) for s in scores])
    else:
        m = bound
    o = functools.reduce(lambda a, b: a + b, [_dot(_bf(jnp.exp2(s - m)), v) for s, v in zip(scores, values)])
    if bound is not None:
        den = jnp.min(o[:, LANES:2 * LANES], axis=0, keepdims=True)
        log.least = jnp.minimum(log.least, jnp.min(den, axis=1, keepdims=True))
    return o


BOUND_SLACK = 1.02
MIN_DENOMINATOR = 2.0 ** -88


class _ShiftLog:
    def __init__(self):
        self.least = jnp.full((1, 1), jnp.inf, jnp.float32)

    def unsafe(self):
        return jnp.logical_not(self.least[0, 0] >= MIN_DENOMINATOR)


def _group_matrix(width, n_groups, extra=None):
    i = lax.broadcasted_iota(jnp.int32, (width, LANES), 0)
    j = lax.broadcasted_iota(jnp.int32, (width, LANES), 1)
    size = LANES // n_groups
    hit = (i // size == j // size) & (i < LANES)
    for t in range(n_groups if extra else 0):
        lo, hi = extra(t)
        hit = hit | ((i >= lo) & (i < hi) & (j // size == t))
    return jnp.where(hit, 1.0, 0.0).astype(jnp.bfloat16)


def _squares(x):
    xf = x.astype(jnp.float32)
    return _bf(xf * xf)


def _key_bound(keys, groups):
    return functools.reduce(jnp.maximum, [jnp.max(_dot(_squares(k), groups), axis=0, keepdims=True) for k in keys])


def _bound(log, q_parts, k2max, n_groups, extra=0.0):
    if log is None:
        return None
    size = LANES // n_groups
    q_norm = jnp.sqrt(functools.reduce(lambda a, b: a + b,
                                       [jnp.sum(q * q, axis=-1, keepdims=True) for q in q_parts])) * BOUND_SLACK
    return jnp.concatenate([q_norm * jnp.sqrt(k2max[:, size * t:size * t + 1]) + extra for t in range(n_groups)], axis=0)


def _loop(n, log, body):
    if log is not None:
        for i in range(n):
            body(i, log)
    else:
        lax.fori_loop(0, n, lambda i, carry: body(i, None) or carry, 0)


def _aligned(start, multiple):
    return start if isinstance(start, int) else pl.multiple_of(start, multiple)


def _normalised(o_ext):
    return o_ext[:, 0:LANES] * (1.0 / o_ext[:, LANES:2 * LANES])


def _swap8(x):
    lane = lax.broadcasted_iota(jnp.int32, (1, LANES), 1)
    return jnp.where((lane & 15) < 8, pltpu.roll(x, LANES - 8, 1), pltpu.roll(x, 8, 1))


def _rope(x, cos, sin):
    outs = []
    for c in range(x.shape[1] // LANES):
        sl = slice(LANES * c, LANES * (c + 1))
        xc = x[:, sl]
        outs.append(xc * cos[:, sl] + _swap8(xc) * sin[:, sl])
    return outs[0] if len(outs) == 1 else jnp.concatenate(outs, axis=1)


def _tile4(x):
    return x + pltpu.roll(x, 32, 1) + pltpu.roll(x, 64, 1) + pltpu.roll(x, 96, 1)


def _params(*sem):
    return pltpu.CompilerParams(dimension_semantics=sem, vmem_limit_bytes=VMEM_LIMIT)


ADA_TN = 1536


def _ada_kernel(m_ref, w_ref, b_ref, o_ref):
    m = m_ref[...]
    s = m * jax.nn.sigmoid(m)
    o_ref[0] = _dot(_bf(s), _bf(w_ref[0])) + b_ref[0]


def _ada(m, w_ada, b_ada):
    n = 6 * D_MODEL
    return pl.pallas_call(
        _ada_kernel,
        grid=(DEPTH, n // ADA_TN),
        in_specs=[
            pl.BlockSpec((N_MOD_ROWS, D_MODEL), lambda l, j: (0, 0)),
            pl.BlockSpec((1, D_MODEL, ADA_TN), lambda l, j: (l, 0, j)),
            pl.BlockSpec((1, 1, ADA_TN), lambda l, j: (l, 0, j)),
        ],
        out_specs=pl.BlockSpec((1, N_MOD_ROWS, ADA_TN), lambda l, j: (l, 0, j)),
        out_shape=jax.ShapeDtypeStruct((DEPTH, N_MOD_ROWS, n), jnp.float32),
        compiler_params=_params("parallel", "parallel"),
        name="ada",
    )(m, w_ada, b_ada.reshape(DEPTH, 1, n))


def _lam_kernel(lq1_ref, lk1_ref, lq2_ref, lk2_ref, init_ref, o_ref):
    init = init_ref[...]
    a = jnp.exp(jnp.sum(lq1_ref[...] * lk1_ref[...], axis=-1, keepdims=True))
    b = jnp.exp(jnp.sum(lq2_ref[...] * lk2_ref[...], axis=-1, keepdims=True))
    lam = a - b + init
    post = 1.0 - init
    for l in range(DEPTH):
        o_ref[l, 0:1, :] = jnp.broadcast_to(lam[l:l + 1], (1, LANES))
        o_ref[l, 1:2, :] = jnp.broadcast_to(post[l:l + 1], (1, LANES))


def _lam_consts(lq1, lk1, lq2, lk2):
    init = np.array([[0.8 - 0.6 * math.exp(-0.3 * l)] for l in range(DEPTH)], np.float32)
    return pl.pallas_call(
        _lam_kernel,
        out_shape=jax.ShapeDtypeStruct((DEPTH, 2, LANES), jnp.float32),
        name="diff_lambda",
    )(lq1, lk1, lq2, lk2, jnp.asarray(init))


N_DROW = 2 * NA_WIN_ROWS - 1
N_DCOL = 2 * NA_WIN_COLS - 1


def _bias_kernel(rpb_ref, o_ref):
    l = pl.program_id(0)
    h = pl.program_id(1)
    base = (l * NA_HEADS + h) * (N_DROW * N_DCOL)
    cq = lax.broadcasted_iota(jnp.int32, (GRID_W, LANES), 0)
    lane = lax.broadcasted_iota(jnp.int32, (GRID_W, LANES), 1)
    ck = lane & (GRID_W - 1)
    dcol = jnp.clip(ck - cq, -(NA_WIN_COLS - 1), NA_WIN_COLS - 1) + (NA_WIN_COLS - 1)
    hi = lane >= GRID_W
    for a in range(N_DROW - 1):
        acc = jnp.zeros((GRID_W, LANES), jnp.float32)
        for j in range(N_DCOL):
            lo_v = rpb_ref[base + a * N_DCOL + j]
            hi_v = rpb_ref[base + (a + 1) * N_DCOL + j]
            acc = jnp.where(dcol == j, jnp.where(hi, hi_v, lo_v), acc)
        o_ref[0, 0, a] = acc * LOG2E


def _bias_tiles(na_rpb):
    return pl.pallas_call(
        _bias_kernel,
        grid=(DEPTH, NA_HEADS),
        in_specs=[pl.BlockSpec(memory_space=pltpu.SMEM)],
        out_specs=pl.BlockSpec((1, 1, N_DROW - 1, GRID_W, LANES), lambda l, h: (l, h, 0, 0, 0)),
        out_shape=jax.ShapeDtypeStruct((DEPTH, NA_HEADS, N_DROW - 1, GRID_W, LANES), jnp.float32),
        compiler_params=_params("parallel", "parallel"),
        name="na_bias_tiles",
    )(na_rpb.reshape(-1))


def _row_group(i):
    return jnp.where(i < TILES_PROMPT, 0, 1 + (i - TILES_PROMPT) // (DEC_SEQ // TM))


def _split_specs(width, first):
    return [pl.BlockSpec((TM, width), lambda i: (jnp.minimum(first + i, TILES_PROMPT - 1), 0)),
            pl.BlockSpec((TM, width), lambda i: (jnp.maximum(first + i - TILES_PROMPT, 0), 0))]


def _x_specs(split, first):
    if not split:
        return [pl.BlockSpec((TM, D_MODEL), lambda i: (first + i, 0))]
    return _split_specs(D_MODEL, first)


def _read_tile(refs, first):
    if len(refs) == 1:
        return refs[0][...]
    return jnp.where(first + pl.program_id(0) < TILES_PROMPT, refs[0][...], refs[1][...])


IN_COLS = 2592
IN_QC, IN_CKV, IN_D = 1536, 1920, 2080
TR_ROWS = 256


def _gelu_tanh(x):
    return 0.5 * x * (1.0 + jnp.tanh(math.sqrt(2.0 / math.pi) * (x + 0.044715 * (x * x * x))))


def _sgu(pd, g, w_ref, bt):
    u = _gelu_tanh(pd[:, 0:WIDTH])
    v = _gelu_tanh(pd[:, WIDTH:2 * WIDTH])
    grp = lax.broadcasted_iota(jnp.int32, (1, WIDTH), 1) // SGU_GROUP_DIM
    v2 = v * v
    ms = jnp.zeros_like(v)
    for gi in range(SGU_GROUPS):
        sel = grp == gi
        tot = jnp.sum(jnp.where(sel, v2, 0.0), axis=-1, keepdims=True)
        ms = jnp.where(sel, tot * (1.0 / SGU_GROUP_DIM), ms)
    vg = _bf(v * lax.rsqrt(ms + EPS) * g)
    outs = []
    for c in range(pd.shape[0] // SGU_CHUNK):
        rows = slice(SGU_CHUNK * c, SGU_CHUNK * (c + 1))
        mixed = jnp.zeros((SGU_CHUNK, WIDTH), jnp.float32)
        for gi in range(SGU_GROUPS):
            full = _dot(_bf(w_ref[0, gi]), vg[rows]) + bt[:, gi:gi + 1]
            mixed = jnp.where(grp == gi, full, mixed)
        outs.append(u[rows] * mixed)
    return jnp.concatenate(outs, axis=0)


def _w_in_row_pieces():
    qn = [(IN_QC + 96 * h, MLA_NOPE) for h in range(MLA_HEADS)]
    qp = [(IN_QC + 96 * h + MLA_NOPE, MLA_ROPE) for h in range(MLA_HEADS)]
    seg_c = qn + qp + [(IN_CKV, MLA_KV_RANK + MLA_ROPE)]
    return (0, SEG_A + SEG_B), seg_c, (IN_D, SEG_D)


def _load_w_in(wt_ref, w_scr):
    ab, seg_c, d = _w_in_row_pieces()
    c_rows = jnp.concatenate([wt_ref[0, s:s + n, :] for s, n in seg_c]
                             + [jnp.zeros((SEG_C_PAD, D_MODEL), jnp.float32)], axis=0)
    for t in range(SEG_C // LANES):
        w_scr[:, SEG_A + SEG_B + LANES * t:SEG_A + SEG_B + LANES * (t + 1)] = _bf(c_rows[LANES * t:LANES * (t + 1)].T)
    for (src, n), dst in ((ab, 0), (d, SEG_A + SEG_B + SEG_C)):
        for t in range(n // TR_ROWS):
            rows = wt_ref[0, src + TR_ROWS * t:src + TR_ROWS * (t + 1), :]
            w_scr[:, dst + TR_ROWS * t:dst + TR_ROWS * (t + 1)] = _bf(rows.T)


def _inproj_kernel(n_x, *refs):
    x_refs = refs[:n_x]
    (g_ref, sh_ref, sc_ref, wt_ref, sg_ref, sw_ref, sbt_ref, pa_ref, pb_ref, pc_ref, od_ref, w_scr) = refs[n_x:]

    @pl.when(pl.program_id(0) == 0)
    def _():
        _load_w_in(wt_ref, w_scr)

    h = _rms(_read_tile(x_refs, 0), g_ref[0]) * (1.0 + sc_ref[...]) + sh_ref[...]
    hb = _bf(h)
    off = 0
    for ref in (pa_ref, pb_ref, pc_ref):
        n = ref.shape[1]
        ref[...] = _dot(hb, w_scr[:, off:off + n])
        off += n
    od_ref[...] = _sgu(_dot(hb, w_scr[:, off:off + SEG_D]), sg_ref[0], sw_ref, sbt_ref[0])


def _inproj(l, xs, g_mix, mod, w_in_t, sgu_g, sgu_w, sgu_bt):
    def mod_spec(j):
        return pl.BlockSpec((None, None, 1, D_MODEL), lambda i: (l, _row_group(i), 0, j))

    widths = (SEG_A, SEG_B, SEG_C, WIDTH)
    return pl.pallas_call(
        functools.partial(_inproj_kernel, len(xs)),
        grid=(N_TOK // TM,),
        in_specs=_x_specs(len(xs) == 2, 0) + [
            pl.BlockSpec((1, 1, D_MODEL), lambda i: (l, 0, 0)),
            mod_spec(0), mod_spec(1),
            pl.BlockSpec((1, IN_COLS, D_MODEL), lambda i: (l, 0, 0), pipeline_mode=pl.Buffered(1)),
            pl.BlockSpec((1, 1, WIDTH), lambda i: (l, 0, 0)),
            pl.BlockSpec((1, SGU_GROUPS, SGU_CHUNK, SGU_CHUNK), lambda i: (l, 0, 0, 0)),
            pl.BlockSpec((1, SGU_CHUNK, SGU_GROUPS), lambda i: (l, 0, 0)),
        ],
        out_specs=[pl.BlockSpec((TM, n), lambda i: (i, 0)) for n in widths],
        out_shape=[jax.ShapeDtypeStruct((N_TOK, n), jnp.float32) for n in widths],
        scratch_shapes=[pltpu.VMEM((D_MODEL, IN_COLS_P), jnp.bfloat16)],
        compiler_params=_params("arbitrary"),
        name="inproj",
    )(*xs, g_mix, mod, mod, w_in_t, sgu_g, sgu_w, sgu_bt)


C_QN, C_QP, C_CKV, C_KPE = 0, 256, 384, 512


def _stack_heads(qp):
    lo = _lane_range(0, 64)
    return jnp.concatenate([_bf(jnp.where(lo, qp, 0.0)), _bf(jnp.where(lo, 0.0, qp))], axis=0)


def _unstack_heads(o, n):
    return jnp.where(_lane_range(0, 64), o[0:n], o[n:2 * n])


def _pair_t(c_ref):
    return jnp.concatenate([c_ref[0], c_ref[1]], axis=0)


def _stack_components(qp):
    return jnp.concatenate([_bf(jnp.where(_lane_range(32 * t, 32 * (t + 1)), qp, 0.0)) for t in range(4)], axis=0)


def _group_mean_sq(x, groups, size):
    sq = x * x
    hi = _bf(sq)
    rest = sq - hi.astype(jnp.float32)
    mid = _bf(rest)
    lo = _bf(rest - mid.astype(jnp.float32))
    return (_dot(hi, groups) + _dot(mid, groups) + _dot(lo, groups)) * (1.0 / size)


def _diff_finish(o, n, lam, post, g2, by_head):
    den = o[:, LANES:2 * LANES]
    outs = []
    for t in range(2):
        p1 = o[2 * t * n:(2 * t + 1) * n, 0:LANES] * (1.0 / den[2 * t * n:(2 * t + 1) * n])
        p2 = o[(2 * t + 1) * n:(2 * t + 2) * n, 0:LANES] * (lam / den[(2 * t + 1) * n:(2 * t + 2) * n])
        outs.append(p1 - p2)
    d = jnp.where(_lane_range(0, 64), outs[0], outs[1])
    return d * lax.rsqrt(_group_mean_sq(d, by_head, DIFF_V_DIM) + EPS) * g2 * post


def _mla_groups(j):
    return _group_matrix(2 * LANES, 2,
                         lambda t: (LANES + MLA_ROPE * (2 * j + t), LANES + MLA_ROPE * (2 * j + t + 1)))


def _mla_queries(qn_pair, qp_all, j):
    halves = []
    for t in range(2):
        h = 2 * j + t
        halves.append(jnp.concatenate([
            _bf(jnp.where(_lane_range(64 * t, 64 * (t + 1)), qn_pair, 0.0)),
            _bf(jnp.where(_lane_range(MLA_ROPE * h, MLA_ROPE * (h + 1)), qp_all, 0.0))], axis=1))
    return jnp.concatenate(halves, axis=0)


def _write_heads_t(p_ref, rows, col0, out_ref, bb):
    xt = p_ref[rows, col0:col0 + WIDTH].T
    for h in range(N_HEADS):
        out_ref[bb, 0, h] = xt[64 * h:64 * (h + 1)]
    _clear_other_layers(out_ref, bb)


def _clear_other_layers(out_ref, bb):
    if out_ref.shape[1] > 1:
        out_ref[bb, 1:] = jnp.zeros(out_ref.shape[1:], jnp.float32)[1:]


def _mix_prompt_kernel(n_prev, *refs):
    ins, outs = refs[:8], refs[8 + n_prev:]
    log = _ShiftLog()
    _mix_prompt_pass(ins, outs, log)

    @pl.when(log.unsafe())
    def _():
        _mix_prompt_pass(ins, outs, None)


def _mix_prompt_pass(ins, outs, log):
    pa_ref, pb_ref, pc_ref, cst_ref, gsub_ref, gckv_ref, wuk_ref, wuv_ref = ins
    o_ref, nak_ref, nav_ref, dk_ref, dv_ref, ckv_ref, kpe_ref = outs
    first_pass = log is not None
    c_a = HEAD_DIM ** -0.5 * LOG2E
    c_b = DIFF_QK_DIM ** -0.5 * LOG2E
    c_c = (MLA_NOPE + MLA_ROPE) ** -0.5 * LOG2E
    lam = cst_ref[0, 0:1, 0:1]
    post = cst_ref[0, 1:2, 0:1]
    wuk, wuv = _bf(wuk_ref[0]), _bf(wuv_ref[0])
    by_head, by_comp = _group_matrix(LANES, 2), _group_matrix(LANES, 4)

    def sequence(bb, log):
        rows = pl.ds(_aligned(bb * SEQ, SEQ), SEQ)
        for j in range(N_PAIRS):
            cols = slice(LANES * j, LANES * (j + 1))
            k = _bf(pa_ref[rows, WIDTH + LANES * j:WIDTH + LANES * (j + 1)])
            v = _with_ones(_bf(pa_ref[rows, 2 * WIDTH + LANES * j:2 * WIDTH + LANES * (j + 1)]))
            q = pa_ref[rows, cols] * c_a
            o = _attend([_dot_nt(_stack_heads(q), k)], [v], log, _bound(log, [q], _key_bound([k], by_head), 2))
            o_ref[rows, cols] = _unstack_heads(_normalised(o), SEQ)
        if first_pass:
            _write_heads_t(pa_ref, rows, WIDTH, nak_ref, bb)
            _write_heads_t(pa_ref, rows, 2 * WIDTH, nav_ref, bb)
        for j in range(N_PAIRS):
            cols = slice(LANES * j, LANES * (j + 1))
            k = _bf(pb_ref[rows, WIDTH + LANES * j:WIDTH + LANES * (j + 1)])
            v = _with_ones(_bf(pb_ref[rows, 2 * WIDTH + LANES * j:2 * WIDTH + LANES * (j + 1)]))
            q = pb_ref[rows, cols] * c_b
            o = _attend([_dot_nt(_stack_components(q), k)], [v], log, _bound(log, [q], _key_bound([k], by_comp), 4))
            o_ref[rows, WIDTH + LANES * j:WIDTH + LANES * (j + 1)] = _diff_finish(o, SEQ, lam, post, gsub_ref[0], by_head)
        if first_pass:
            _write_heads_t(pb_ref, rows, WIDTH, dk_ref, bb)
            _write_heads_t(pb_ref, rows, 2 * WIDTH, dv_ref, bb)
        ckv = _rms(pc_ref[rows, C_CKV:C_CKV + MLA_KV_RANK], gckv_ref[0])
        kpe_slot = pc_ref[rows, C_KPE:C_KPE + LANES]
        if first_pass:
            ckv_ref[bb, 0] = ckv
            _clear_other_layers(ckv_ref, bb)
            kpe_ref[bb, 0] = kpe_slot.T[0:MLA_ROPE]
            _clear_other_layers(kpe_ref, bb)
        ckv_b = _bf(ckv)
        kn = _bf(_dot(ckv_b, wuk))
        vv = _bf(_dot(ckv_b, wuv))
        kpe4 = _bf(_tile4(kpe_slot))
        qn = pc_ref[rows, C_QN:C_QN + WIDTH] * c_c
        qp = pc_ref[rows, C_QP:C_QP + LANES] * c_c
        for j in range(N_PAIRS):
            cols = slice(LANES * j, LANES * (j + 1))
            k = jnp.concatenate([kn[:, cols], kpe4], axis=1)
            qs = _mla_queries(qn[:, cols], qp, j)
            groups = _mla_groups(j)
            o = _attend([_dot_nt(qs, k)], [_with_ones(vv[:, cols])], log,
                        _bound(log, [qn[:, cols], qp], _key_bound([k], groups), 2))
            o_ref[rows, 2 * WIDTH + LANES * j:2 * WIDTH + LANES * (j + 1)] = _unstack_heads(_normalised(o), SEQ)

    _loop(PB, log, sequence)


def _mix_prompt(l, pa, pb, pc, cst, g_sub2, g_ckv, w_uk, w_uv, prev):
    n_prev = len(prev)
    tails = [(NA_HEADS, HEAD_DIM, SEQ)] * 2 + [(DIFF_HEADS, 64, SEQ)] * 2 + [(SEQ, MLA_KV_RANK), (MLA_ROPE, SEQ)]

    def cache_spec(tail):
        if l == 0:
            return pl.BlockSpec((PB, DEPTH) + tail, lambda b: (b, 0) + (0,) * len(tail))
        return pl.BlockSpec((PB, 1) + tail, lambda b: (b, l) + (0,) * len(tail))

    def rows(width):
        return pl.BlockSpec((PB * SEQ, width), lambda b: (b, 0))

    def layer(*tail):
        return pl.BlockSpec((1,) + tail, lambda b: (l,) + (0,) * len(tail))

    return pl.pallas_call(
        functools.partial(_mix_prompt_kernel, n_prev),
        grid=(BATCH // PB,),
        in_specs=[rows(SEG_A), rows(SEG_B), rows(SEG_C), layer(2, LANES), layer(1, LANES), layer(1, MLA_KV_RANK),
                  layer(MLA_KV_RANK, WIDTH), layer(MLA_KV_RANK, WIDTH)] + [pl.BlockSpec(memory_space=pl.ANY)] * n_prev,
        out_specs=[rows(O_ATT)] + [cache_spec(t) for t in tails],
        out_shape=[jax.ShapeDtypeStruct((N_PROMPT, O_ATT), jnp.float32)]
        + [jax.ShapeDtypeStruct((BATCH, DEPTH) + t, jnp.float32) for t in tails],
        input_output_aliases={8 + i: 1 + i for i in range(n_prev)},
        compiler_params=_params("parallel"),
        name="mix_prompt",
    )(pa, pb, pc, cst, g_sub2, g_ckv, w_uk, w_uv, *prev)


def _na_row_groups():
    kh = min(NA_WIN_ROWS, GRID_ROWS)
    r0s = [min(max(r - kh // 2, 0), GRID_ROWS - kh) for r in range(GRID_ROWS)]
    groups = []
    for r, r0 in enumerate(r0s):
        if groups and groups[-1][2] == r0:
            groups[-1][1] = r
        else:
            groups.append([r, r, r0])
    return kh, [tuple(g) for g in groups]


def _na_sample(q_ref, k_ref, v_ref, ck_ref, cv_ref, tt_ref, o_ref, log):
    c = HEAD_DIM ** -0.5 * LOG2E
    kh, groups = _na_row_groups()
    lk = kh * GRID_W
    edge = [g for g in groups if g[1] > g[0]]
    inner = [g for g in groups if g[1] == g[0]]
    depth = inner[0][0] - inner[0][2]
    assert all(g[0] - g[2] == depth for g in inner) and [g[0] for g in inner] == list(range(inner[0][0], inner[-1][0] + 1))

    def in_window(n):
        cq = lax.broadcasted_iota(jnp.int32, (n, lk), 0) & (GRID_W - 1)
        ck = lax.broadcasted_iota(jnp.int32, (n, lk), 1) & (GRID_W - 1)
        c0 = jnp.clip(cq - NA_WIN_COLS // 2, 0, GRID_W - NA_WIN_COLS)
        return (ck >= c0) & (ck < c0 + NA_WIN_COLS)

    kc_t = _bf(_pair_t(ck_ref))
    vc = _with_ones(_bf(_pair_t(cv_ref).T))
    by_head = _group_matrix(LANES, 2)
    k2max = _key_bound([_bf(k_ref[...]), _bf(_pair_t(ck_ref).T)], by_head)
    tmax = functools.reduce(jnp.maximum, [tt_ref[t, a] for t in range(2) for a in range(N_DROW - 1)])
    bplus = jnp.maximum(jnp.max(jnp.max(tmax, axis=-1, keepdims=True), axis=0, keepdims=True), 0.0)

    def group(row0, key0, offsets, log):
        n = len(offsets) * GRID_W
        rows, keys = pl.ds(row0, n), pl.ds(key0, lk)
        q = q_ref[rows, :] * c
        qg = _stack_heads(q)
        k = _bf(k_ref[keys, :])
        v = _with_ones(_bf(v_ref[keys, :]))
        bias = jnp.concatenate([
            jnp.concatenate([tt_ref[t, 2 * i - off + NA_WIN_ROWS - 1] for i in range(kh // 2)], axis=1)
            for t in range(2) for off in offsets], axis=0)
        s_loc = jnp.where(in_window(2 * n), _dot_nt(qg, k) + bias, NEG_INF)
        o = _attend([_dot(qg, kc_t), s_loc], [vc, v], log, _bound(log, [q], k2max, 2, bplus))
        o_ref[rows, :] = _unstack_heads(_normalised(o), n)

    if log is None:
        def any_row(r, log):
            r0 = jnp.clip(r - kh // 2, 0, GRID_ROWS - kh)
            group(_aligned(r * GRID_W, GRID_W), _aligned(r0 * GRID_W, GRID_W), [r - r0], log)

        _loop(GRID_ROWS, log, any_row)
        return

    for (r_lo, r_hi, r0) in edge:
        group(r_lo * GRID_W, r0 * GRID_W, [r - r0 for r in range(r_lo, r_hi + 1)], log)

    def inner_row(i, log):
        r = inner[0][0] + i
        group(r * GRID_W, (r - depth) * GRID_W, [depth], log)

    _loop(len(inner), log, inner_row)


def _diff_sample(q_ref, k_ref, v_ref, ck_ref, cv_ref, cos_ref, sin_ref, cst_ref, g_ref, o_ref, log):
    c = DIFF_QK_DIM ** -0.5 * LOG2E
    lam = cst_ref[0, 0:1, 0:1]
    post = cst_ref[0, 1:2, 0:1]
    k_new = _bf(_rope(k_ref[...], cos_ref[...], sin_ref[...]))
    kc_t = _bf(_pair_t(ck_ref))
    vc = _with_ones(_bf(_pair_t(cv_ref).T))
    v = _with_ones(_bf(v_ref[...]))
    by_head, by_comp = _group_matrix(LANES, 2), _group_matrix(LANES, 4)
    k2max = _key_bound([k_new, _bf(_pair_t(ck_ref).T)], by_comp)

    qb = QB_EXACT if log is None else QB

    def block(qi, log):
        rows = pl.ds(_aligned(qi * qb, qb), qb)
        q = _rope(q_ref[rows, :], cos_ref[rows, :], sin_ref[rows, :]) * c
        qs = _stack_components(q)
        o = _attend([_dot(qs, kc_t), _dot_nt(qs, k_new)], [vc, v], log, _bound(log, [q], k2max, 4))
        o_ref[rows, :] = _diff_finish(o, qb, lam, post, g_ref[0], by_head)

    _loop(DEC_SEQ // qb, log, block)


def _mla_sample(j, qn_ref, qp_ref, ckv_ref, kpe_ref, cckv_ref, ckpe_ref, cosq_ref, sinq_ref, cosk_ref, sink_ref,
                gckv_ref, wuk_ref, wuv_ref, o_ref, log):
    c = (MLA_NOPE + MLA_ROPE) ** -0.5 * LOG2E
    wuk, wuv = _bf(wuk_ref[0]), _bf(wuv_ref[0])
    ckv_new = _bf(_rms(ckv_ref[...], gckv_ref[0]))
    ckv_old = _bf(cckv_ref[...])
    kpe_new = _bf(_tile4(_rope(kpe_ref[...], cosk_ref[...], sink_ref[...])))
    kpe_old = _bf(jnp.concatenate([ckpe_ref[...]] * MLA_HEADS, axis=0).T)
    k_old = jnp.concatenate([_bf(_dot(ckv_old, wuk)), kpe_old], axis=1)
    k_new = jnp.concatenate([_bf(_dot(ckv_new, wuk)), kpe_new], axis=1)
    vo, vn = _with_ones(_bf(_dot(ckv_old, wuv))), _with_ones(_bf(_dot(ckv_new, wuv)))
    groups = _mla_groups(j)
    k2max = _key_bound([k_old, k_new], groups)

    qb = QB_EXACT if log is None else QB

    def block(qi, log):
        rows = pl.ds(_aligned(qi * qb, qb), qb)
        qp = _rope(qp_ref[rows, :], cosq_ref[rows, :], sinq_ref[rows, :]) * c
        qn = qn_ref[rows, :] * c
        qs = _mla_queries(qn, qp, j)
        o = _attend([_dot_nt(qs, k_old), _dot_nt(qs, k_new)], [vo, vn], log, _bound(log, [qn, qp], k2max, 2))
        o_ref[rows, :] = _unstack_heads(_normalised(o), qb)

    _loop(DEC_SEQ // qb, log, block)


def _mix_sample_kernel(qa_ref, ka_ref, va_ref, qb_ref, kb_ref, vb_ref, qn_ref, qp_ref, ckv_ref, kpe_ref,
                       cnak_ref, cnav_ref, cdk_ref, cdv_ref, cckv_ref, ckpe_ref, tt_ref,
                       cosb_ref, sinb_ref, cosq_ref, sinq_ref, cosk_ref, sink_ref,
                       cst_ref, gsub_ref, gckv_ref, wuk_ref, wuv_ref, oa_ref, ob_ref, oc_ref):
    j = pl.program_id(1)

    def run(log):
        _diff_sample(qb_ref, kb_ref, vb_ref, cdk_ref, cdv_ref, cosb_ref, sinb_ref, cst_ref, gsub_ref, ob_ref, log)
        _mla_sample(j, qn_ref, qp_ref, ckv_ref, kpe_ref, cckv_ref, ckpe_ref, cosq_ref, sinq_ref, cosk_ref, sink_ref,
                    gckv_ref, wuk_ref, wuv_ref, oc_ref, log)
        _na_sample(qa_ref, ka_ref, va_ref, cnak_ref, cnav_ref, tt_ref, oa_ref, log)

    log = _ShiftLog()
    run(log)

    @pl.when(log.unsafe())
    def _():
        run(None)


def _mix_sample(l, pa, pb, pc, caches_t, tt, tables, cst, g_sub2, g_ckv, w_uk, w_uv):
    first = N_PROMPT // DEC_SEQ

    def cols(block):
        return pl.BlockSpec((DEC_SEQ, LANES), lambda b, j: (first + b, block(j)))

    def cache(*tail, pair=False):
        return pl.BlockSpec((None, None) + tail, lambda b, j: (b, l, j if pair else 0) + (0,) * (len(tail) - 1))

    def layer(*tail):
        return pl.BlockSpec((1,) + tail, lambda b, j: (l,) + (0,) * len(tail))

    table = pl.BlockSpec((DEC_SEQ, LANES), lambda b, j: (0, 0), pipeline_mode=pl.Buffered(1))
    qkv = [cols(lambda j: j), cols(lambda j: N_PAIRS + j), cols(lambda j: 2 * N_PAIRS + j)]
    seg_c = [cols(lambda j: j), cols(lambda j: C_QP // LANES), cols(lambda j: C_CKV // LANES), cols(lambda j: C_KPE // LANES)]
    kv_t = cache(2, 64, PAST_LEN, pair=True)
    w_pair = pl.BlockSpec((1, MLA_KV_RANK, LANES), lambda b, j: (l, 0, j))
    out = pl.BlockSpec((DEC_SEQ, LANES), lambda b, j: (b, j))
    return pl.pallas_call(
        _mix_sample_kernel,
        grid=(DEC_BATCH, N_PAIRS),
        in_specs=qkv + qkv + seg_c + [kv_t, kv_t, kv_t, kv_t, cache(PAST_LEN, MLA_KV_RANK), cache(MLA_ROPE, PAST_LEN),
                                      pl.BlockSpec((None, 2, N_DROW - 1, GRID_W, LANES), lambda b, j: (l, j, 0, 0, 0)),
                                      table, table, table, table, table, table,
                                      layer(2, LANES), layer(1, LANES), layer(1, MLA_KV_RANK), w_pair, w_pair],
        out_specs=[out, out, out],
        out_shape=[jax.ShapeDtypeStruct((N_SAMPLE, WIDTH), jnp.float32)] * 3,
        compiler_params=_params("parallel", "arbitrary"),
        name="mix_sample",
    )(pa, pa, pa, pb, pb, pb, pc, pc, pc, pc, *caches_t, tt, *tables, cst, g_sub2, g_ckv, w_uk, w_uv)


FF_CHUNK = 1024


def _outffn_kernel(n_x, first, final, *refs):
    x_refs, op_ref, os_refs = refs[:n_x], refs[n_x], refs[n_x + 1:n_x + 4]
    (od_ref, wout_ref, g1_ref, gffn_ref, sh2_ref, sc2_ref, g2_ref, w1_ref, w2_ref, gfin_ref, y_ref) = refs[n_x + 4:]
    o_att = jnp.where(first + pl.program_id(0) < TILES_PROMPT, op_ref[...],
                      jnp.concatenate([r[...] for r in os_refs], axis=1))
    acc = (_dot(_bf(o_att), wout_ref[0, 0:O_ATT, :])
           + _dot(_bf(od_ref[...]), wout_ref[0, O_ATT:O_ATT + WIDTH, :]))
    x1 = _read_tile(x_refs, first) + g1_ref[...] * acc
    hf = _bf(_rms(x1, gffn_ref[0]) * (1.0 + sc2_ref[...]) + sh2_ref[...])
    acc = jnp.zeros((TM, D_MODEL), jnp.float32)
    for c in range(D_FF // FF_CHUNK):
        cols = slice(FF_CHUNK * c, FF_CHUNK * (c + 1))
        a = jnp.square(jnp.maximum(_dot(hf, w1_ref[0, :, cols]), 0.0))
        acc += _dot(_bf(a), w2_ref[0, cols, :])
    y = x1 + g2_ref[...] * acc
    if final:
        y = _rms(y, gfin_ref[...])
    y_ref[...] = y


def _outffn(l, xs, o_p, o_s, od, w_out, g_ffn, mod, w1, w2, g_final, first, n_tiles):
    def mod_spec(j):
        return pl.BlockSpec((None, None, 1, D_MODEL), lambda i: (l, _row_group(first + i), 0, j))

    def resident(shape):
        return pl.BlockSpec(shape, lambda i: (l,) + (0,) * (len(shape) - 1), pipeline_mode=pl.Buffered(1))

    return pl.pallas_call(
        functools.partial(_outffn_kernel, len(xs), first, l == DEPTH - 1),
        grid=(n_tiles,),
        in_specs=_x_specs(len(xs) == 2, first) + _split_specs(O_ATT, first)[:1] + _split_specs(WIDTH, first)[1:] * 3 + [
            pl.BlockSpec((TM, WIDTH), lambda i: (first + i, 0)),
            resident((1, 4 * WIDTH, D_MODEL)),
            mod_spec(2),
            pl.BlockSpec((1, 1, D_MODEL), lambda i: (l, 0, 0)),
            mod_spec(3), mod_spec(4), mod_spec(5),
            resident((1, D_MODEL, D_FF)),
            resident((1, D_FF, D_MODEL)),
            pl.BlockSpec((1, D_MODEL), lambda i: (0, 0)),
        ],
        out_specs=pl.BlockSpec((TM, D_MODEL), lambda i: (i, 0)),
        out_shape=jax.ShapeDtypeStruct((n_tiles * TM, D_MODEL), jnp.float32),
        compiler_params=_params("parallel"),
        name="outffn",
    )(*xs, o_p, *o_s, od, w_out, mod, g_ffn, mod, mod, mod, w1, w2, g_final)


def _rope32_tables():
    t = np.arange(DEC_SEQ)
    rows, cols = (t // GRID_W).astype(np.float64), (t % GRID_W).astype(np.float64)
    half = 8
    freqs = ROPE_BASE ** (-np.arange(half, dtype=np.float64) / half)
    cos, sin = [], []
    for pos in (rows, cols):
        ang = pos[:, None] * freqs[None, :]
        cos += [np.cos(ang), np.cos(ang)]
        sin += [-np.sin(ang), np.sin(ang)]
    return np.concatenate(cos, axis=1).astype(np.float32), np.concatenate(sin, axis=1).astype(np.float32)


def _rope_tables():
    c32, s32 = _rope32_tables()
    tile = lambda a, n: np.tile(a, (1, n))
    pad = np.zeros((DEC_SEQ, LANES - MLA_ROPE), np.float32)
    cos_k = np.concatenate([c32, pad + 1.0], axis=1)
    sin_k = np.concatenate([s32, pad], axis=1)
    return (tile(c32, 4), tile(s32, 4),
            cos_k, sin_k)


def kernel(x_prompt, x_sample, cache_na_k, cache_na_v, cache_diff_k, cache_diff_v, cache_mla_ckv, cache_mla_kpe, c, c_ctx, w_ada, b_ada, g_mix, g_ffn, w_in, w_out, na_rpb, diff_lq1, diff_lk1, diff_lq2, diff_lk2, diff_g_subln, mla_g_ckv, mla_w_uk, mla_w_uv, sgu_g, sgu_w, sgu_b, w_ff1, w_ff2, g_final):
    f32 = jnp.float32
    m = jnp.concatenate([c_ctx[None, :], c, jnp.zeros((N_MOD_ROWS - 1 - DEC_BATCH, D_MODEL), f32)], axis=0)
    mod = _ada(m, w_ada, b_ada).reshape(DEPTH, N_MOD_ROWS, 1, 6 * D_MODEL)
    cst = _lam_consts(diff_lq1, diff_lk1, diff_lq2, diff_lk2)
    tt = _bias_tiles(na_rpb)
    cos4, sin4, cos_k, sin_k = [jnp.asarray(t) for t in _rope_tables()]
    tables = (cos4, sin4, cos4, sin4, cos_k, sin_k)

    t_last = lambda a: jnp.swapaxes(a, -1, -2)
    w_in_t = t_last(w_in)
    caches_t = (t_last(cache_na_k), t_last(cache_na_v), t_last(cache_diff_k), t_last(cache_diff_v),
                cache_mla_ckv, t_last(cache_mla_kpe))
    w_out_b, w1_b, w2_b = _bf(w_out), _bf(w_ff1), _bf(w_ff2)
    g_mix3 = g_mix.reshape(DEPTH, 1, D_MODEL)
    g_ffn3 = g_ffn.reshape(DEPTH, 1, D_MODEL)
    g_sub2 = jnp.tile(diff_g_subln, (1, 2)).reshape(DEPTH, 1, LANES)
    g_ckv3 = mla_g_ckv.reshape(DEPTH, 1, MLA_KV_RANK)
    sgu_g3 = sgu_g.reshape(DEPTH, 1, WIDTH)
    sgu_bt = sgu_b.transpose(0, 2, 1)
    g_fin2 = g_final.reshape(1, D_MODEL)

    xs = (x_prompt.reshape(N_PROMPT, D_MODEL), x_sample.reshape(N_SAMPLE, D_MODEL))
    new = ()
    for l in range(DEPTH):
        pa, pb, pc, od = _inproj(l, xs, g_mix3, mod, w_in_t, sgu_g3, sgu_w, sgu_bt)
        o_p, *new = _mix_prompt(l, pa, pb, pc, cst, g_sub2, g_ckv3, mla_w_uk, mla_w_uv, new)
        o_s = _mix_sample(l, pa, pb, pc, caches_t, tt, tables, cst, g_sub2, g_ckv3, mla_w_uk, mla_w_uv)
        ffn = functools.partial(_outffn, l, xs, o_p, o_s, od, w_out_b, g_ffn3, mod, w1_b, w2_b, g_fin2)
        if l < DEPTH - 1:
            xs = (ffn(0, TILES_PROMPT + TILES_SAMPLE),)
        else:
            xs = (ffn(0, TILES_PROMPT), ffn(TILES_PROMPT, TILES_SAMPLE))
    y_prompt = xs[0].reshape(BATCH, SEQ, D_MODEL)
    y_sample = xs[1].reshape(DEC_BATCH, DEC_SEQ, D_MODEL)
    na_k, na_v, diff_k, diff_v, mla_ckv, mla_kpe = new
    return (y_prompt, y_sample, t_last(na_k), t_last(na_v), t_last(diff_k), t_last(diff_v), mla_ckv, t_last(mla_kpe))
```

```python
import functools
import math

import numpy as np
import jax
import jax.numpy as jnp
from jax import lax
from jax.experimental import pallas as pl
from jax.experimental.pallas import tpu as pltpu

D_MODEL = 1024
BATCH = 16
SEQ = 256
DEPTH = 4
DEC_BATCH = 2
DEC_SEQ = 1024
PAST_LEN = 512
GRID_W = 64
GRID_ROWS = DEC_SEQ // GRID_W
HEAD_DIM = 64
NA_HEADS = 4
NA_WIN_ROWS = 8
NA_WIN_COLS = 16
DIFF_HEADS = 4
DIFF_QK_DIM = 32
DIFF_V_DIM = 64
MLA_HEADS = 4
MLA_NOPE = 64
MLA_ROPE = 32
MLA_V = 64
MLA_KV_RANK = 128
SGU_GROUPS = 4
SGU_GROUP_DIM = 64
SGU_CHUNK = 128
D_FF = 4 * D_MODEL
ROPE_BASE = 10000.0
EPS = 1e-6
NEG_INF = -1e30
LOG2E = 1.4426950408889634

N_HEADS = 4
N_PAIRS = N_HEADS // 2
LANES = 128
WIDTH = 256
N_PROMPT = BATCH * SEQ
N_SAMPLE = DEC_BATCH * DEC_SEQ
N_TOK = N_PROMPT + N_SAMPLE
N_MOD_ROWS = 8

SEG_A = 3 * WIDTH
SEG_B = 3 * WIDTH
SEG_C = 640
SEG_D = 2 * WIDTH
SEG_C_PAD = 96
IN_COLS_P = SEG_A + SEG_B + SEG_C + SEG_D
O_ATT = 3 * WIDTH

TM = 512
TILES_PROMPT = N_PROMPT // TM
TILES_SAMPLE = N_SAMPLE // TM
PB = 2
QB = 512
QB_EXACT = 64
VMEM_LIMIT = 56 * 1024 * 1024


def _bf(x):
    return x.astype(jnp.bfloat16)


def _dot(a, b):
    return jnp.dot(a, b, preferred_element_type=jnp.float32)


def _dot_nt(a, b):
    return lax.dot_general(a, b, (((1,), (1,)), ((), ())), preferred_element_type=jnp.float32)


def _rms(x, g):
    ms = jnp.mean(x * x, axis=-1, keepdims=True)
    return x * lax.rsqrt(ms + EPS) * g


def _lane_range(lo, hi, width=LANES):
    lane = lax.broadcasted_iota(jnp.int32, (1, width), 1)
    return (lane >= lo) & (lane < hi)


def _with_ones(v):
    return jnp.concatenate([v, jnp.ones((v.shape[0], LANES), jnp.bfloat16)], axis=1)


def _attend(scores, values, log=None, bound=None):
    if bound is None:
        m = functools.reduce(jnp.maximum, [jnp.max(s, axis=-1, keepdims=True) for s in scores])
    else:
        m = bound
    o = functools.reduce(lambda a, b: a + b, [_dot(_bf(jnp.exp2(s - m)), v) for s, v in zip(scores, values)])
    if bound is not None:
        den = jnp.min(o[:, LANES:2 * LANES], axis=0, keepdims=True)
        log.least = jnp.minimum(log.least, jnp.min(den, axis=1, keepdims=True))
    return o


BOUND_SLACK = 1.02
MIN_DENOMINATOR = 2.0 ** -88


class _ShiftLog:
    def __init__(self):
        self.least = jnp.full((1, 1), jnp.inf, jnp.float32)

    def unsafe(self):
        return jnp.logical_not(self.least[0, 0] >= MIN_DENOMINATOR)


def _group_matrix(width, n_groups, extra=None):
    i = lax.broadcasted_iota(jnp.int32, (width, LANES), 0)
    j = lax.broadcasted_iota(jnp.int32, (width, LANES), 1)
    size = LANES // n_groups
    hit = (i // size == j // size) & (i < LANES)
    for t in range(n_groups if extra else 0):
        lo, hi = extra(t)
        hit = hit | ((i >= lo) & (i < hi) & (j // size == t))
    return jnp.where(hit, 1.0, 0.0).astype(jnp.bfloat16)


def _squares(x):
    xf = x.astype(jnp.float32)
    return _bf(xf * xf)


def _key_bound(keys, groups):
    return functools.reduce(jnp.maximum, [jnp.max(_dot(_squares(k), groups), axis=0, keepdims=True) for k in keys])


def _bound(log, q_parts, k2max, n_groups, extra=0.0):
    if log is None:
        return None
    size = LANES // n_groups
    q_norm = jnp.sqrt(functools.reduce(lambda a, b: a + b,
                                       [jnp.sum(q * q, axis=-1, keepdims=True) for q in q_parts])) * BOUND_SLACK
    return jnp.concatenate([q_norm * jnp.sqrt(k2max[:, size * t:size * t + 1]) + extra for t in range(n_groups)], axis=0)


def _loop(n, log, body):
    if log is not None:
        for i in range(n):
            body(i, log)
    else:
        lax.fori_loop(0, n, lambda i, carry: body(i, None) or carry, 0)


def _aligned(start, multiple):
    return start if isinstance(start, int) else pl.multiple_of(start, multiple)


def _normalised(o_ext):
    return o_ext[:, 0:LANES] * (1.0 / o_ext[:, LANES:2 * LANES])


def _swap8(x):
    lane = lax.broadcasted_iota(jnp.int32, (1, LANES), 1)
    return jnp.where((lane & 15) < 8, pltpu.roll(x, LANES - 8, 1), pltpu.roll(x, 8, 1))


def _rope(x, cos, sin):
    outs = []
    for c in range(x.shape[1] // LANES):
        sl = slice(LANES * c, LANES * (c + 1))
        xc = x[:, sl]
        outs.append(xc * cos[:, sl] + _swap8(xc) * sin[:, sl])
    return outs[0] if len(outs) == 1 else jnp.concatenate(outs, axis=1)


def _tile4(x):
    return x + pltpu.roll(x, 32, 1) + pltpu.roll(x, 64, 1) + pltpu.roll(x, 96, 1)


def _params(*sem):
    return pltpu.CompilerParams(dimension_semantics=sem, vmem_limit_bytes=VMEM_LIMIT)


ADA_TN = 1536


def _ada_kernel(m_ref, w_ref, b_ref, o_ref):
    m = m_ref[...]
    s = m * jax.nn.sigmoid(m)
    o_ref[0] = _dot(_bf(s), _bf(w_ref[0])) + b_ref[0]


def _ada(m, w_ada, b_ada):
    n = 6 * D_MODEL
    return pl.pallas_call(
        _ada_kernel,
        grid=(DEPTH, n // ADA_TN),
        in_specs=[
            pl.BlockSpec((N_MOD_ROWS, D_MODEL), lambda l, j: (0, 0)),
            pl.BlockSpec((1, D_MODEL, ADA_TN), lambda l, j: (l, 0, j)),
            pl.BlockSpec((1, 1, ADA_TN), lambda l, j: (l, 0, j)),
        ],
        out_specs=pl.BlockSpec((1, N_MOD_ROWS, ADA_TN), lambda l, j: (l, 0, j)),
        out_shape=jax.ShapeDtypeStruct((DEPTH, N_MOD_ROWS, n), jnp.float32),
        compiler_params=_params("parallel", "parallel"),
        name="ada",
    )(m, w_ada, b_ada.reshape(DEPTH, 1, n))


def _lam_kernel(lq1_ref, lk1_ref, lq2_ref, lk2_ref, init_ref, o_ref):
    init = init_ref[...]
    a = jnp.exp(jnp.sum(lq1_ref[...] * lk1_ref[...], axis=-1, keepdims=True))
    b = jnp.exp(jnp.sum(lq2_ref[...] * lk2_ref[...], axis=-1, keepdims=True))
    lam = a - b + init
    post = 1.0 - init
    for l in range(DEPTH):
        o_ref[l, 0:1, :] = jnp.broadcast_to(lam[l:l + 1], (1, LANES))
        o_ref[l, 1:2, :] = jnp.broadcast_to(post[l:l + 1], (1, LANES))


def _lam_consts(lq1, lk1, lq2, lk2):
    init = np.array([[0.8 - 0.6 * math.exp(-0.3 * l)] for l in range(DEPTH)], np.float32)
    return pl.pallas_call(
        _lam_kernel,
        out_shape=jax.ShapeDtypeStruct((DEPTH, 2, LANES), jnp.float32),
        name="diff_lambda",
    )(lq1, lk1, lq2, lk2, jnp.asarray(init))


N_DROW = 2 * NA_WIN_ROWS - 1
N_DCOL = 2 * NA_WIN_COLS - 1


def _bias_kernel(rpb_ref, o_ref):
    l = pl.program_id(0)
    h = pl.program_id(1)
    base = (l * NA_HEADS + h) * (N_DROW * N_DCOL)
    cq = lax.broadcasted_iota(jnp.int32, (GRID_W, LANES), 0)
    lane = lax.broadcasted_iota(jnp.int32, (GRID_W, LANES), 1)
    ck = lane & (GRID_W - 1)
    dcol = jnp.clip(ck - cq, -(NA_WIN_COLS - 1), NA_WIN_COLS - 1) + (NA_WIN_COLS - 1)
    hi = lane >= GRID_W
    for a in range(N_DROW - 1):
        acc = jnp.zeros((GRID_W, LANES), jnp.float32)
        for j in range(N_DCOL):
            lo_v = rpb_ref[base + a * N_DCOL + j]
            hi_v = rpb_ref[base + (a + 1) * N_DCOL + j]
            acc = jnp.where(dcol == j, jnp.where(hi, hi_v, lo_v), acc)
        o_ref[0, 0, a] = acc * LOG2E


def _bias_tiles(na_rpb):
    return pl.pallas_call(
        _bias_kernel,
        grid=(DEPTH, NA_HEADS),
        in_specs=[pl.BlockSpec(memory_space=pltpu.SMEM)],
        out_specs=pl.BlockSpec((1, 1, N_DROW - 1, GRID_W, LANES), lambda l, h: (l, h, 0, 0, 0)),
        out_shape=jax.ShapeDtypeStruct((DEPTH, NA_HEADS, N_DROW - 1, GRID_W, LANES), jnp.float32),
        compiler_params=_params("parallel", "parallel"),
        name="na_bias_tiles",
    )(na_rpb.reshape(-1))


def _row_group(i):
    return jnp.where(i < TILES_PROMPT, 0, 1 + (i - TILES_PROMPT) // (DEC_SEQ // TM))


def _split_specs(width, first):
    return [pl.BlockSpec((TM, width), lambda i: (jnp.minimum(first + i, TILES_PROMPT - 1), 0)),
            pl.BlockSpec((TM, width), lambda i: (jnp.maximum(first + i - TILES_PROMPT, 0), 0))]


def _x_specs(split, first):
    if not split:
        return [pl.BlockSpec((TM, D_MODEL), lambda i: (first + i, 0))]
    return _split_specs(D_MODEL, first)


def _read_tile(refs, first):
    if len(refs) == 1:
        return refs[0][...]
    return jnp.where(first + pl.program_id(0) < TILES_PROMPT, refs[0][...], refs[1][...])


IN_COLS = 2592
IN_QC, IN_CKV, IN_D = 1536, 1920, 2080
TR_ROWS = 256


def _gelu_tanh(x):
    return 0.5 * x * (1.0 + jnp.tanh(math.sqrt(2.0 / math.pi) * (x + 0.044715 * (x * x * x))))


def _sgu(pd, g, w_ref, bt):
    u = _gelu_tanh(pd[:, 0:WIDTH])
    v = _gelu_tanh(pd[:, WIDTH:2 * WIDTH])
    grp = lax.broadcasted_iota(jnp.int32, (1, WIDTH), 1) // SGU_GROUP_DIM
    v2 = v * v
    ms = jnp.zeros_like(v)
    for gi in range(SGU_GROUPS):
        sel = grp == gi
        tot = jnp.sum(jnp.where(sel, v2, 0.0), axis=-1, keepdims=True)
        ms = jnp.where(sel, tot * (1.0 / SGU_GROUP_DIM), ms)
    vg = _bf(v * lax.rsqrt(ms + EPS) * g)
    outs = []
    for c in range(pd.shape[0] // SGU_CHUNK):
        rows = slice(SGU_CHUNK * c, SGU_CHUNK * (c + 1))
        mixed = jnp.zeros((SGU_CHUNK, WIDTH), jnp.float32)
        for gi in range(SGU_GROUPS):
            full = _dot(_bf(w_ref[0, gi]), vg[rows]) + bt[:, gi:gi + 1]
            mixed = jnp.where(grp == gi, full, mixed)
        outs.append(u[rows] * mixed)
    return jnp.concatenate(outs, axis=0)


def _w_in_row_pieces():
    qn = [(IN_QC + 96 * h, MLA_NOPE) for h in range(MLA_HEADS)]
    qp = [(IN_QC + 96 * h + MLA_NOPE, MLA_ROPE) for h in range(MLA_HEADS)]
    seg_c = qn + qp + [(IN_CKV, MLA_KV_RANK + MLA_ROPE)]
    return (0, SEG_A + SEG_B), seg_c, (IN_D, SEG_D)


def _load_w_in(wt_ref, w_scr):
    ab, seg_c, d = _w_in_row_pieces()
    c_rows = jnp.concatenate([wt_ref[0, s:s + n, :] for s, n in seg_c]
                             + [jnp.zeros((SEG_C_PAD, D_MODEL), jnp.float32)], axis=0)
    for t in range(SEG_C // LANES):
        w_scr[:, SEG_A + SEG_B + LANES * t:SEG_A + SEG_B + LANES * (t + 1)] = _bf(c_rows[LANES * t:LANES * (t + 1)].T)
    for (src, n), dst in ((ab, 0), (d, SEG_A + SEG_B + SEG_C)):
        for t in range(n // TR_ROWS):
            rows = wt_ref[0, src + TR_ROWS * t:src + TR_ROWS * (t + 1), :]
            w_scr[:, dst + TR_ROWS * t:dst + TR_ROWS * (t + 1)] = _bf(rows.T)


def _inproj_kernel(n_x, *refs):
    x_refs = refs[:n_x]
    (g_ref, sh_ref, sc_ref, wt_ref, sg_ref, sw_ref, sbt_ref, pa_ref, pb_ref, pc_ref, od_ref, w_scr) = refs[n_x:]

    @pl.when(pl.program_id(0) == 0)
    def _():
        _load_w_in(wt_ref, w_scr)

    h = _rms(_read_tile(x_refs, 0), g_ref[0]) * (1.0 + sc_ref[...]) + sh_ref[...]
    hb = _bf(h)
    pa_ref[...] = _dot(hb, w_scr[:, 0:SEG_A])
    pb_ref[...] = _dot(hb, w_scr[:, SEG_A:SEG_A + SEG_B])
    off = SEG_A + SEG_B + SEG_C
    od_ref[...] = _sgu(_dot(hb, w_scr[:, off:off + SEG_D]), sg_ref[0], sw_ref, sbt_ref[0])
    pc_ref[...] = _dot(hb, w_scr[:, SEG_A + SEG_B:off])


def _inproj(l, xs, g_mix, mod, w_in_t, sgu_g, sgu_w, sgu_bt):
    def mod_spec(j):
        return pl.BlockSpec((None, None, 1, D_MODEL), lambda i: (l, _row_group(i), 0, j))

    widths = (SEG_A, SEG_B, SEG_C, WIDTH)
    return pl.pallas_call(
        functools.partial(_inproj_kernel, len(xs)),
        grid=(N_TOK // TM,),
        in_specs=_x_specs(len(xs) == 2, 0) + [
            pl.BlockSpec((1, 1, D_MODEL), lambda i: (l, 0, 0)),
            mod_spec(0), mod_spec(1),
            pl.BlockSpec((1, IN_COLS, D_MODEL), lambda i: (l, 0, 0), pipeline_mode=pl.Buffered(1)),
            pl.BlockSpec((1, 1, WIDTH), lambda i: (l, 0, 0)),
            pl.BlockSpec((1, SGU_GROUPS, SGU_CHUNK, SGU_CHUNK), lambda i: (l, 0, 0, 0)),
            pl.BlockSpec((1, SGU_CHUNK, SGU_GROUPS), lambda i: (l, 0, 0)),
        ],
        out_specs=[pl.BlockSpec((TM, n), lambda i: (i, 0)) for n in widths],
        out_shape=[jax.ShapeDtypeStruct((N_TOK, n), jnp.float32) for n in widths],
        scratch_shapes=[pltpu.VMEM((D_MODEL, IN_COLS_P), jnp.bfloat16)],
        compiler_params=_params("arbitrary"),
        name="inproj",
    )(*xs, g_mix, mod, mod, w_in_t, sgu_g, sgu_w, sgu_bt)


C_QN, C_QP, C_CKV, C_KPE = 0, 256, 384, 512


def _stack_heads(qp):
    lo = _lane_range(0, 64)
    return jnp.concatenate([_bf(jnp.where(lo, qp, 0.0)), _bf(jnp.where(lo, 0.0, qp))], axis=0)


def _unstack_heads(o, n):
    return jnp.where(_lane_range(0, 64), o[0:n], o[n:2 * n])


def _pair_t(c_ref):
    return jnp.concatenate([c_ref[0], c_ref[1]], axis=0)


def _stack_components(qp):
    return jnp.concatenate([_bf(jnp.where(_lane_range(32 * t, 32 * (t + 1)), qp, 0.0)) for t in range(4)], axis=0)


def _group_mean_sq(x, groups, size):
    sq = x * x
    hi = _bf(sq)
    rest = sq - hi.astype(jnp.float32)
    mid = _bf(rest)
    lo = _bf(rest - mid.astype(jnp.float32))
    return (_dot(hi, groups) + _dot(mid, groups) + _dot(lo, groups)) * (1.0 / size)


def _diff_finish(o, n, lam, post, g2, by_head):
    den = o[:, LANES:2 * LANES]
    outs = []
    for t in range(2):
        p1 = o[2 * t * n:(2 * t + 1) * n, 0:LANES] * (1.0 / den[2 * t * n:(2 * t + 1) * n])
        p2 = o[(2 * t + 1) * n:(2 * t + 2) * n, 0:LANES] * (lam / den[(2 * t + 1) * n:(2 * t + 2) * n])
        outs.append(p1 - p2)
    d = jnp.where(_lane_range(0, 64), outs[0], outs[1])
    return d * lax.rsqrt(_group_mean_sq(d, by_head, DIFF_V_DIM) + EPS) * g2 * post


def _mla_groups(j):
    return _group_matrix(2 * LANES, 2,
                         lambda t: (LANES + MLA_ROPE * (2 * j + t), LANES + MLA_ROPE * (2 * j + t + 1)))


def _mla_queries(qn_pair, qp_all, j):
    halves = []
    for t in range(2):
        h = 2 * j + t
        halves.append(jnp.concatenate([
            _bf(jnp.where(_lane_range(64 * t, 64 * (t + 1)), qn_pair, 0.0)),
            _bf(jnp.where(_lane_range(MLA_ROPE * h, MLA_ROPE * (h + 1)), qp_all, 0.0))], axis=1))
    return jnp.concatenate(halves, axis=0)


def _write_heads_t(p_ref, rows, col0, out_ref, bb):
    xt = p_ref[rows, col0:col0 + WIDTH].T
    for h in range(N_HEADS):
        out_ref[bb, 0, h] = xt[64 * h:64 * (h + 1)]
    _clear_other_layers(out_ref, bb)


def _clear_other_layers(out_ref, bb):
    if out_ref.shape[1] > 1:
        out_ref[bb, 1:] = jnp.zeros(out_ref.shape[1:], jnp.float32)[1:]


def _mix_prompt_kernel(n_prev, *refs):
    ins, outs = refs[:8], refs[8 + n_prev:]
    log = _ShiftLog()
    _mix_prompt_pass(ins, outs, log)

    @pl.when(log.unsafe())
    def _():
        _mix_prompt_pass(ins, outs, None)


def _mix_prompt_pass(ins, outs, log):
    pa_ref, pb_ref, pc_ref, cst_ref, gsub_ref, gckv_ref, wuk_ref, wuv_ref = ins
    o_ref, nak_ref, nav_ref, dk_ref, dv_ref, ckv_ref, kpe_ref = outs
    first_pass = log is not None
    c_a = HEAD_DIM ** -0.5 * LOG2E
    c_b = DIFF_QK_DIM ** -0.5 * LOG2E
    c_c = (MLA_NOPE + MLA_ROPE) ** -0.5 * LOG2E
    lam = cst_ref[0, 0:1, 0:1]
    post = cst_ref[0, 1:2, 0:1]
    wuk, wuv = _bf(wuk_ref[0]), _bf(wuv_ref[0])
    by_head, by_comp = _group_matrix(LANES, 2), _group_matrix(LANES, 4)

    def sequence(bb, log):
        rows = pl.ds(_aligned(bb * SEQ, SEQ), SEQ)
        for j in range(N_PAIRS):
            cols = slice(LANES * j, LANES * (j + 1))
            k = _bf(pa_ref[rows, WIDTH + LANES * j:WIDTH + LANES * (j + 1)])
            v = _with_ones(_bf(pa_ref[rows, 2 * WIDTH + LANES * j:2 * WIDTH + LANES * (j + 1)]))
            q = pa_ref[rows, cols] * c_a
            o = _attend([_dot_nt(_stack_heads(q), k)], [v], log, _bound(log, [q], _key_bound([k], by_head), 2))
            o_ref[rows, cols] = _unstack_heads(_normalised(o), SEQ)
        if first_pass:
            _write_heads_t(pa_ref, rows, WIDTH, nak_ref, bb)
            _write_heads_t(pa_ref, rows, 2 * WIDTH, nav_ref, bb)
        for j in range(N_PAIRS):
            cols = slice(LANES * j, LANES * (j + 1))
            k = _bf(pb_ref[rows, WIDTH + LANES * j:WIDTH + LANES * (j + 1)])
            v = _with_ones(_bf(pb_ref[rows, 2 * WIDTH + LANES * j:2 * WIDTH + LANES * (j + 1)]))
            q = pb_ref[rows, cols] * c_b
            o = _attend([_dot_nt(_stack_components(q), k)], [v], log, _bound(log, [q], _key_bound([k], by_comp), 4))
            o_ref[rows, WIDTH + LANES * j:WIDTH + LANES * (j + 1)] = _diff_finish(o, SEQ, lam, post, gsub_ref[0], by_head)
        if first_pass:
            _write_heads_t(pb_ref, rows, WIDTH, dk_ref, bb)
            _write_heads_t(pb_ref, rows, 2 * WIDTH, dv_ref, bb)
        ckv = _rms(pc_ref[rows, C_CKV:C_CKV + MLA_KV_RANK], gckv_ref[0])
        kpe_slot = pc_ref[rows, C_KPE:C_KPE + LANES]
        if first_pass:
            ckv_ref[bb, 0] = ckv
            _clear_other_layers(ckv_ref, bb)
            kpe_ref[bb, 0] = kpe_slot.T[0:MLA_ROPE]
            _clear_other_layers(kpe_ref, bb)
        ckv_b = _bf(ckv)
        kn = _bf(_dot(ckv_b, wuk))
        vv = _bf(_dot(ckv_b, wuv))
        kpe4 = _bf(_tile4(kpe_slot))
        qn = pc_ref[rows, C_QN:C_QN + WIDTH] * c_c
        qp = pc_ref[rows, C_QP:C_QP + LANES] * c_c
        for j in range(N_PAIRS):
            cols = slice(LANES * j, LANES * (j + 1))
            k = jnp.concatenate([kn[:, cols], kpe4], axis=1)
            qs = _mla_queries(qn[:, cols], qp, j)
            groups = _mla_groups(j)
            o = _attend([_dot_nt(qs, k)], [_with_ones(vv[:, cols])], log,
                        _bound(log, [qn[:, cols], qp], _key_bound([k], groups), 2))
            o_ref[rows, 2 * WIDTH + LANES * j:2 * WIDTH + LANES * (j + 1)] = _unstack_heads(_normalised(o), SEQ)

    _loop(PB, log, sequence)


def _mix_prompt(l, pa, pb, pc, cst, g_sub2, g_ckv, w_uk, w_uv, prev):
    n_prev = len(prev)
    tails = [(NA_HEADS, HEAD_DIM, SEQ)] * 2 + [(DIFF_HEADS, 64, SEQ)] * 2 + [(SEQ, MLA_KV_RANK), (MLA_ROPE, SEQ)]

    def cache_spec(tail):
        if l == 0:
            return pl.BlockSpec((PB, DEPTH) + tail, lambda b: (b, 0) + (0,) * len(tail))
        return pl.BlockSpec((PB, 1) + tail, lambda b: (b, l) + (0,) * len(tail))

    def rows(width):
        return pl.BlockSpec((PB * SEQ, width), lambda b: (b, 0))

    def layer(*tail):
        return pl.BlockSpec((1,) + tail, lambda b: (l,) + (0,) * len(tail))

    return pl.pallas_call(
        functools.partial(_mix_prompt_kernel, n_prev),
        grid=(BATCH // PB,),
        in_specs=[rows(SEG_A), rows(SEG_B), rows(SEG_C), layer(2, LANES), layer(1, LANES), layer(1, MLA_KV_RANK),
                  layer(MLA_KV_RANK, WIDTH), layer(MLA_KV_RANK, WIDTH)] + [pl.BlockSpec(memory_space=pl.ANY)] * n_prev,
        out_specs=[rows(O_ATT)] + [cache_spec(t) for t in tails],
        out_shape=[jax.ShapeDtypeStruct((N_PROMPT, O_ATT), jnp.float32)]
        + [jax.ShapeDtypeStruct((BATCH, DEPTH) + t, jnp.float32) for t in tails],
        input_output_aliases={8 + i: 1 + i for i in range(n_prev)},
        compiler_params=_params("parallel"),
        name="mix_prompt",
    )(pa, pb, pc, cst, g_sub2, g_ckv, w_uk, w_uv, *prev)


def _na_row_groups():
    kh = min(NA_WIN_ROWS, GRID_ROWS)
    r0s = [min(max(r - kh // 2, 0), GRID_ROWS - kh) for r in range(GRID_ROWS)]
    groups = []
    for r, r0 in enumerate(r0s):
        if groups and groups[-1][2] == r0:
            groups[-1][1] = r
        else:
            groups.append([r, r, r0])
    return kh, [tuple(g) for g in groups]


def _na_sample(q_ref, k_ref, v_ref, ck_ref, cv_ref, tt_ref, o_ref, log):
    c = HEAD_DIM ** -0.5 * LOG2E
    kh, groups = _na_row_groups()
    lk = kh * GRID_W
    edge = [g for g in groups if g[1] > g[0]]
    inner = [g for g in groups if g[1] == g[0]]
    depth = inner[0][0] - inner[0][2]
    assert all(g[0] - g[2] == depth for g in inner) and [g[0] for g in inner] == list(range(inner[0][0], inner[-1][0] + 1))

    def in_window(n):
        cq = lax.broadcasted_iota(jnp.int32, (n, lk), 0) & (GRID_W - 1)
        ck = lax.broadcasted_iota(jnp.int32, (n, lk), 1) & (GRID_W - 1)
        c0 = jnp.clip(cq - NA_WIN_COLS // 2, 0, GRID_W - NA_WIN_COLS)
        return (ck >= c0) & (ck < c0 + NA_WIN_COLS)

    kc_t = _bf(_pair_t(ck_ref))
    vc = _with_ones(_bf(_pair_t(cv_ref).T))
    by_head = _group_matrix(LANES, 2)
    k2max = _key_bound([_bf(k_ref[...]), _bf(_pair_t(ck_ref).T)], by_head)
    tmax = functools.reduce(jnp.maximum, [tt_ref[t, a] for t in range(2) for a in range(N_DROW - 1)])
    bplus = jnp.maximum(jnp.max(jnp.max(tmax, axis=-1, keepdims=True), axis=0, keepdims=True), 0.0)

    def group(row0, key0, offsets, log):
        n = len(offsets) * GRID_W
        rows, keys = pl.ds(row0, n), pl.ds(key0, lk)
        q = q_ref[rows, :] * c
        qg = _stack_heads(q)
        k = _bf(k_ref[keys, :])
        v = _with_ones(_bf(v_ref[keys, :]))
        bias = jnp.concatenate([
            jnp.concatenate([tt_ref[t, 2 * i - off + NA_WIN_ROWS - 1] for i in range(kh // 2)], axis=1)
            for t in range(2) for off in offsets], axis=0)
        s_loc = jnp.where(in_window(2 * n), _dot_nt(qg, k) + bias, NEG_INF)
        o = _attend([_dot(qg, kc_t), s_loc], [vc, v], log, _bound(log, [q], k2max, 2, bplus))
        o_ref[rows, :] = _unstack_heads(_normalised(o), n)

    if log is None:
        def any_row(r, log):
            r0 = jnp.clip(r - kh // 2, 0, GRID_ROWS - kh)
            group(_aligned(r * GRID_W, GRID_W), _aligned(r0 * GRID_W, GRID_W), [r - r0], log)

        _loop(GRID_ROWS, log, any_row)
        return

    for (r_lo, r_hi, r0) in edge:
        group(r_lo * GRID_W, r0 * GRID_W, [r - r0 for r in range(r_lo, r_hi + 1)], log)

    def inner_row(i, log):
        r = inner[0][0] + i
        group(r * GRID_W, (r - depth) * GRID_W, [depth], log)

    _loop(len(inner), log, inner_row)


def _diff_sample(q_ref, k_ref, v_ref, ck_ref, cv_ref, cos_ref, sin_ref, cst_ref, g_ref, o_ref, log):
    c = DIFF_QK_DIM ** -0.5 * LOG2E
    lam = cst_ref[0, 0:1, 0:1]
    post = cst_ref[0, 1:2, 0:1]
    k_new = _bf(_rope(k_ref[...], cos_ref[...], sin_ref[...]))
    kc_t = _bf(_pair_t(ck_ref))
    vc = _with_ones(_bf(_pair_t(cv_ref).T))
    v = _with_ones(_bf(v_ref[...]))
    by_head, by_comp = _group_matrix(LANES, 2), _group_matrix(LANES, 4)
    k2max = _key_bound([k_new, _bf(_pair_t(ck_ref).T)], by_comp)

    qb = QB_EXACT if log is None else QB

    def block(qi, log):
        rows = pl.ds(_aligned(qi * qb, qb), qb)
        q = _rope(q_ref[rows, :], cos_ref[rows, :], sin_ref[rows, :]) * c
        qs = _stack_components(q)
        o = _attend([_dot(qs, kc_t), _dot_nt(qs, k_new)], [vc, v], log, _bound(log, [q], k2max, 4))
        o_ref[rows, :] = _diff_finish(o, qb, lam, post, g_ref[0], by_head)

    _loop(DEC_SEQ // qb, log, block)


def _mla_sample(j, qn_ref, qp_ref, ckv_ref, kpe_ref, cckv_ref, ckpe_ref, cosq_ref, sinq_ref, cosk_ref, sink_ref,
                gckv_ref, wuk_ref, wuv_ref, o_ref, log):
    c = (MLA_NOPE + MLA_ROPE) ** -0.5 * LOG2E
    wuk, wuv = _bf(wuk_ref[0]), _bf(wuv_ref[0])
    ckv_new = _bf(_rms(ckv_ref[...], gckv_ref[0]))
    ckv_old = _bf(cckv_ref[...])
    kpe_new = _bf(_tile4(_rope(kpe_ref[...], cosk_ref[...], sink_ref[...])))
    kpe_old = _bf(jnp.concatenate([ckpe_ref[...]] * MLA_HEADS, axis=0).T)
    k_old = jnp.concatenate([_bf(_dot(ckv_old, wuk)), kpe_old], axis=1)
    k_new = jnp.concatenate([_bf(_dot(ckv_new, wuk)), kpe_new], axis=1)
    vo, vn = _with_ones(_bf(_dot(ckv_old, wuv))), _with_ones(_bf(_dot(ckv_new, wuv)))
    groups = _mla_groups(j)
    k2max = _key_bound([k_old, k_new], groups)

    qb = QB_EXACT if log is None else QB

    def block(qi, log):
        rows = pl.ds(_aligned(qi * qb, qb), qb)
        qp = _rope(qp_ref[rows, :], cosq_ref[rows, :], sinq_ref[rows, :]) * c
        qn = qn_ref[rows, :] * c
        qs = _mla_queries(qn, qp, j)
        o = _attend([_dot_nt(qs, k_old), _dot_nt(qs, k_new)], [vo, vn], log, _bound(log, [qn, qp], k2max, 2))
        o_ref[rows, :] = _unstack_heads(_normalised(o), qb)

    _loop(DEC_SEQ // qb, log, block)


def _mix_sample_kernel(qa_ref, ka_ref, va_ref, qb_ref, kb_ref, vb_ref, qn_ref, qp_ref, ckv_ref, kpe_ref,
                       cnak_ref, cnav_ref, cdk_ref, cdv_ref, cckv_ref, ckpe_ref, tt_ref,
                       cosb_ref, sinb_ref, cosq_ref, sinq_ref, cosk_ref, sink_ref,
                       cst_ref, gsub_ref, gckv_ref, wuk_ref, wuv_ref, oa_ref, ob_ref, oc_ref):
    j = pl.program_id(1)

    def run(log):
        _na_sample(qa_ref, ka_ref, va_ref, cnak_ref, cnav_ref, tt_ref, oa_ref, log)
        _diff_sample(qb_ref, kb_ref, vb_ref, cdk_ref, cdv_ref, cosb_ref, sinb_ref, cst_ref, gsub_ref, ob_ref, log)
        _mla_sample(j, qn_ref, qp_ref, ckv_ref, kpe_ref, cckv_ref, ckpe_ref, cosq_ref, sinq_ref, cosk_ref, sink_ref,
                    gckv_ref, wuk_ref, wuv_ref, oc_ref, log)

    log = _ShiftLog()
    run(log)

    @pl.when(log.unsafe())
    def _():
        run(None)


def _mix_sample(l, pa, pb, pc, caches_t, tt, tables, cst, g_sub2, g_ckv, w_uk, w_uv):
    first = N_PROMPT // DEC_SEQ

    def cols(block):
        return pl.BlockSpec((DEC_SEQ, LANES), lambda b, j: (first + b, block(j)))

    def cache(*tail, pair=False):
        return pl.BlockSpec((None, None) + tail, lambda b, j: (b, l, j if pair else 0) + (0,) * (len(tail) - 1))

    def layer(*tail):
        return pl.BlockSpec((1,) + tail, lambda b, j: (l,) + (0,) * len(tail))

    table = pl.BlockSpec((DEC_SEQ, LANES), lambda b, j: (0, 0), pipeline_mode=pl.Buffered(1))
    qkv = [cols(lambda j: j), cols(lambda j: N_PAIRS + j), cols(lambda j: 2 * N_PAIRS + j)]
    seg_c = [cols(lambda j: j), cols(lambda j: C_QP // LANES), cols(lambda j: C_CKV // LANES), cols(lambda j: C_KPE // LANES)]
    kv_t = cache(2, 64, PAST_LEN, pair=True)
    w_pair = pl.BlockSpec((1, MLA_KV_RANK, LANES), lambda b, j: (l, 0, j))
    out = pl.BlockSpec((DEC_SEQ, LANES), lambda b, j: (b, j))
    return pl.pallas_call(
        _mix_sample_kernel,
        grid=(DEC_BATCH, N_PAIRS),
        in_specs=qkv + qkv + seg_c + [kv_t, kv_t, kv_t, kv_t, cache(PAST_LEN, MLA_KV_RANK), cache(MLA_ROPE, PAST_LEN),
                                      pl.BlockSpec((None, 2, N_DROW - 1, GRID_W, LANES), lambda b, j: (l, j, 0, 0, 0)),
                                      table, table, table, table, table, table,
                                      layer(2, LANES), layer(1, LANES), layer(1, MLA_KV_RANK), w_pair, w_pair],
        out_specs=[out, out, out],
        out_shape=[jax.ShapeDtypeStruct((N_SAMPLE, WIDTH), jnp.float32)] * 3,
        compiler_params=_params("parallel", "arbitrary"),
        name="mix_sample",
    )(pa, pa, pa, pb, pb, pb, pc, pc, pc, pc, *caches_t, tt, *tables, cst, g_sub2, g_ckv, w_uk, w_uv)


FF_CHUNK = 1024


def _outffn_kernel(n_x, first, final, *refs):
    x_refs, op_ref, os_refs = refs[:n_x], refs[n_x], refs[n_x + 1:n_x + 4]
    (od_ref, wout_ref, g1_ref, gffn_ref, sh2_ref, sc2_ref, g2_ref, w1_ref, w2_ref, gfin_ref, y_ref) = refs[n_x + 4:]
    o_att = jnp.where(first + pl.program_id(0) < TILES_PROMPT, op_ref[...],
                      jnp.concatenate([r[...] for r in os_refs], axis=1))
    acc = (_dot(_bf(o_att), wout_ref[0, 0:O_ATT, :])
           + _dot(_bf(od_ref[...]), wout_ref[0, O_ATT:O_ATT + WIDTH, :]))
    x1 = _read_tile(x_refs, first) + g1_ref[...] * acc
    hf = _bf(_rms(x1, gffn_ref[0]) * (1.0 + sc2_ref[...]) + sh2_ref[...])
    acc = jnp.zeros((TM, D_MODEL), jnp.float32)
    for c in range(D_FF // FF_CHUNK):
        cols = slice(FF_CHUNK * c, FF_CHUNK * (c + 1))
        a = jnp.square(jnp.maximum(_dot(hf, w1_ref[0, :, cols]), 0.0))
        acc += _dot(_bf(a), w2_ref[0, cols, :])
    y = x1 + g2_ref[...] * acc
    if final:
        y = _rms(y, gfin_ref[...])
    y_ref[...] = y


def _outffn(l, xs, o_p, o_s, od, w_out, g_ffn, mod, w1, w2, g_final, first, n_tiles):
    def mod_spec(j):
        return pl.BlockSpec((None, None, 1, D_MODEL), lambda i: (l, _row_group(first + i), 0, j))

    def resident(shape):
        return pl.BlockSpec(shape, lambda i: (l,) + (0,) * (len(shape) - 1), pipeline_mode=pl.Buffered(1))

    return pl.pallas_call(
        functools.partial(_outffn_kernel, len(xs), first, l == DEPTH - 1),
        grid=(n_tiles,),
        in_specs=_x_specs(len(xs) == 2, first) + _split_specs(O_ATT, first)[:1] + _split_specs(WIDTH, first)[1:] * 3 + [
            pl.BlockSpec((TM, WIDTH), lambda i: (first + i, 0)),
            resident((1, 4 * WIDTH, D_MODEL)),
            mod_spec(2),
            pl.BlockSpec((1, 1, D_MODEL), lambda i: (l, 0, 0)),
            mod_spec(3), mod_spec(4), mod_spec(5),
            resident((1, D_MODEL, D_FF)),
            resident((1, D_FF, D_MODEL)),
            pl.BlockSpec((1, D_MODEL), lambda i: (0, 0)),
        ],
        out_specs=pl.BlockSpec((TM, D_MODEL), lambda i: (i, 0)),
        out_shape=jax.ShapeDtypeStruct((n_tiles * TM, D_MODEL), jnp.float32),
        compiler_params=_params("parallel"),
        name="outffn",
    )(*xs, o_p, *o_s, od, w_out, mod, g_ffn, mod, mod, mod, w1, w2, g_final)


def _rope32_tables():
    t = np.arange(DEC_SEQ)
    rows, cols = (t // GRID_W).astype(np.float64), (t % GRID_W).astype(np.float64)
    half = 8
    freqs = ROPE_BASE ** (-np.arange(half, dtype=np.float64) / half)
    cos, sin = [], []
    for pos in (rows, cols):
        ang = pos[:, None] * freqs[None, :]
        cos += [np.cos(ang), np.cos(ang)]
        sin += [-np.sin(ang), np.sin(ang)]
    return np.concatenate(cos, axis=1).astype(np.float32), np.concatenate(sin, axis=1).astype(np.float32)


def _rope_tables():
    c32, s32 = _rope32_tables()
    tile = lambda a, n: np.tile(a, (1, n))
    pad = np.zeros((DEC_SEQ, LANES - MLA_ROPE), np.float32)
    cos_k = np.concatenate([c32, pad + 1.0], axis=1)
    sin_k = np.concatenate([s32, pad], axis=1)
    return (tile(c32, 4), tile(s32, 4),
            cos_k, sin_k)


def kernel(x_prompt, x_sample, cache_na_k, cache_na_v, cache_diff_k, cache_diff_v, cache_mla_ckv, cache_mla_kpe, c, c_ctx, w_ada, b_ada, g_mix, g_ffn, w_in, w_out, na_rpb, diff_lq1, diff_lk1, diff_lq2, diff_lk2, diff_g_subln, mla_g_ckv, mla_w_uk, mla_w_uv, sgu_g, sgu_w, sgu_b, w_ff1, w_ff2, g_final):
    f32 = jnp.float32
    m = jnp.concatenate([c_ctx[None, :], c, jnp.zeros((N_MOD_ROWS - 1 - DEC_BATCH, D_MODEL), f32)], axis=0)
    mod = _ada(m, w_ada, b_ada).reshape(DEPTH, N_MOD_ROWS, 1, 6 * D_MODEL)
    cst = _lam_consts(diff_lq1, diff_lk1, diff_lq2, diff_lk2)
    tt = _bias_tiles(na_rpb)
    cos4, sin4, cos_k, sin_k = [jnp.asarray(t) for t in _rope_tables()]
    tables = (cos4, sin4, cos4, sin4, cos_k, sin_k)

    t_last = lambda a: jnp.swapaxes(a, -1, -2)
    w_in_t = t_last(w_in)
    caches_t = (t_last(cache_na_k), t_last(cache_na_v), t_last(cache_diff_k), t_last(cache_diff_v),
                cache_mla_ckv, t_last(cache_mla_kpe))
    w_out_b, w1_b, w2_b = _bf(w_out), _bf(w_ff1), _bf(w_ff2)
    g_mix3 = g_mix.reshape(DEPTH, 1, D_MODEL)
    g_ffn3 = g_ffn.reshape(DEPTH, 1, D_MODEL)
    g_sub2 = jnp.tile(diff_g_subln, (1, 2)).reshape(DEPTH, 1, LANES)
    g_ckv3 = mla_g_ckv.reshape(DEPTH, 1, MLA_KV_RANK)
    sgu_g3 = sgu_g.reshape(DEPTH, 1, WIDTH)
    sgu_bt = sgu_b.transpose(0, 2, 1)
    g_fin2 = g_final.reshape(1, D_MODEL)

    xs = (x_prompt.reshape(N_PROMPT, D_MODEL), x_sample.reshape(N_SAMPLE, D_MODEL))
    new = ()
    for l in range(DEPTH):
        pa, pb, pc, od = _inproj(l, xs, g_mix3, mod, w_in_t, sgu_g3, sgu_w, sgu_bt)
        o_p, *new = _mix_prompt(l, pa, pb, pc, cst, g_sub2, g_ckv3, mla_w_uk, mla_w_uv, new)
        o_s = _mix_sample(l, pa, pb, pc, caches_t, tt, tables, cst, g_sub2, g_ckv3, mla_w_uk, mla_w_uv)
        ffn = functools.partial(_outffn, l, xs, o_p, o_s, od, w_out_b, g_ffn3, mod, w1_b, w2_b, g_fin2)
        if l < DEPTH - 1:
            xs = (ffn(0, TILES_PROMPT + TILES_SAMPLE),)
        else:
            xs = (ffn(0, TILES_PROMPT), ffn(TILES_PROMPT, TILES_SAMPLE))
    y_prompt = xs[0].reshape(BATCH, SEQ, D_MODEL)
    y_sample = xs[1].reshape(DEC_BATCH, DEC_SEQ, D_MODEL)
    na_k, na_v, diff_k, diff_v, mla_ckv, mla_kpe = new
    return (y_prompt, y_sample, t_last(na_k), t_last(na_v), t_last(diff_k), t_last(diff_v), mla_ckv, t_last(mla_kpe))
```
